```python
import math
import jax, jax.numpy as jnp
from jax import lax
import numpy as np

D_MODEL = 1024
BATCH = 16
SEQ = 256
DEPTH = 2
DEC_BATCH = 2
DEC_SEQ = 1024
PAST_LEN = 512

GRID_W = 64
HEAD_DIM = 64
MIX_WIDTH = D_MODEL
GROUP_W = MIX_WIDTH // 4
H_A = GROUP_W // HEAD_DIM
KV_A = 2
H_B = GROUP_W // HEAD_DIM
KV_B = 2
WINDOW = 128
ATT_BLOCK = 128
ROPE_BASE = 10000.0
W_C = GROUP_W
C_BLOCKS = 4
C_BLOCK_W = W_C // C_BLOCKS
CONV_W = 4
CONV_PAD_L = 2
LRU_C = 8.0
H_D = GROUP_W // HEAD_DIM
RET_CHUNK = 128
D_FF = 2816
N_MOD = 9
EPS = 1e-6
NEG_INF = -1e30
IN_SPLITS = (H_A * HEAD_DIM, KV_A * HEAD_DIM, KV_A * HEAD_DIM,
             H_B * HEAD_DIM, KV_B * HEAD_DIM, KV_B * HEAD_DIM,
             W_C, W_C,
             GROUP_W, GROUP_W, GROUP_W, GROUP_W)
IN_WIDTH = sum(IN_SPLITS)

kernel_name = 'hybrid_prefix_diffusion_step'


def rms_norm(x, g):
    xf = x.astype(jnp.float32)
    y = xf * lax.rsqrt(jnp.mean(xf * xf, axis=-1, keepdims=True) + EPS)
    return (y * g.astype(jnp.float32)).astype(x.dtype)


def swiglu(h, wg, wu, wd):
    return (jax.nn.silu(h @ wg) * (h @ wu)) @ wd


def split_columns(u):
    idx = np.cumsum(IN_SPLITS)[:-1].tolist()
    return jnp.split(u, idx, axis=-1)


def grid_positions(n_tokens):
    rows = n_tokens // GRID_W
    row = jnp.repeat(jnp.arange(rows), GRID_W)
    col = jnp.tile(jnp.arange(GRID_W), rows)
    return row.astype(jnp.float32), col.astype(jnp.float32)


def axial_rope(x):
    T = x.shape[1]
    row, col = grid_positions(T)
    half = HEAD_DIM // 2
    inv = 1.0 / (ROPE_BASE ** (jnp.arange(0, half, 2, dtype=jnp.float32) / half))
    bshape = (1, T) + (1,) * (x.ndim - 3) + (half // 2,)

    def rotate(xp, pos):
        ang = pos[:, None] * inv[None, :]
        cos = jnp.cos(ang).reshape(bshape)
        sin = jnp.sin(ang).reshape(bshape)
        xf = xp.astype(jnp.float32)
        x1, x2 = xf[..., :half // 2], xf[..., half // 2:]
        return jnp.concatenate([x1 * cos - x2 * sin, x1 * sin + x2 * cos], axis=-1)

    out = jnp.concatenate([rotate(x[..., :half], row), rotate(x[..., half:], col)], axis=-1)
    return out.astype(x.dtype)


def qk_heads(q, k, gq, gk, n_q, n_kv):
    B, T, _ = q.shape
    q = rms_norm(q.reshape(B, T, n_kv, n_q // n_kv, HEAD_DIM), gq)
    k = rms_norm(k.reshape(B, T, n_kv, HEAD_DIM), gk)
    return q, k


def attn_probs(s, sink):
    if sink is None:
        return jax.nn.softmax(s, axis=-1)
    sk = sink.astype(jnp.float32)[:, :, None, None]
    m = jnp.maximum(jnp.max(s, axis=-1, keepdims=True), sk)
    p = jnp.exp(s - m)
    return p / (jnp.sum(p, axis=-1, keepdims=True) + jnp.exp(sk - m))


def dense_attention(q, k, v, sink):
    B, T, KV, G, D = q.shape
    nb = T // ATT_BLOCK
    qb = jnp.moveaxis(q.reshape(B, nb, ATT_BLOCK, KV, G, D), 1, 0)
    scale = D ** -0.5

    def block(qi):
        s = jnp.einsum('bqkgd,bskd->bkgqs', qi, k).astype(jnp.float32) * scale
        p = attn_probs(s, sink).astype(v.dtype)
        return jnp.einsum('bkgqs,bskd->bqkgd', p, v)

    o = lax.map(block, qb)
    return jnp.moveaxis(o, 0, 1).reshape(B, T, KV * G * D)


def window_attention(q, k, v, kc, vc, sink):
    B, T, KV, G, D = q.shape
    W = ATT_BLOCK
    nb = T // W
    Nc = kc.shape[1]
    scale = D ** -0.5
    qb = q.reshape(B, nb, W, KV, G, D)

    def band(x):
        pad = jnp.zeros((B, W) + x.shape[2:], x.dtype)
        xp = jnp.concatenate([pad, x, pad], axis=1).reshape((B, nb + 2, W) + x.shape[2:])
        return jnp.concatenate([xp[:, :-2], xp[:, 1:-1], xp[:, 2:]], axis=2)

    kb, vb = band(k), band(v)
    qi = jnp.arange(W)[:, None]
    kj = jnp.arange(3 * W)[None, :]
    key_pos = (jnp.arange(nb)[:, None, None] - 1) * W + kj[None]
    rel = kj - W - qi
    valid = (jnp.abs(rel) <= WINDOW)[None] & (key_pos >= 0) & (key_pos < T)
    s_band = jnp.einsum('bnqkgd,bnskd->bnkgqs', qb, kb).astype(jnp.float32) * scale
    s_band = jnp.where(valid[None, :, None, None], s_band, NEG_INF)
    s_ctx = jnp.einsum('bnqkgd,bckd->bnkgqc', qb, kc).astype(jnp.float32) * scale
    p = attn_probs(jnp.concatenate([s_ctx, s_band], axis=-1), sink).astype(v.dtype)
    o = (jnp.einsum('bnkgqc,bckd->bnqkgd', p[..., :Nc], vc)
         + jnp.einsum('bnkgqs,bnskd->bnqkgd', p[..., Nc:], vb))
    return o.reshape(B, T, KV * G * D)


def centred_conv(x, w, b):
    C = x.shape[-1]
    y = lax.conv_general_dilated(x, w[:, None, :].astype(x.dtype), (1,),
                                 [(CONV_PAD_L, CONV_W - 1 - CONV_PAD_L)],
                                 dimension_numbers=('NWC', 'WIO', 'NWC'),
                                 feature_group_count=C)
    return y + b


def rglru_scan(x, wa, ba, wx, bx, lam, h0):
    B, T, W = x.shape
    xf = x.astype(jnp.float32)
    xb = xf.reshape(B, T, C_BLOCKS, C_BLOCK_W)
    r = jax.nn.sigmoid(jnp.einsum('btnc,ncd->btnd', xb, wa.astype(jnp.float32)).reshape(B, T, W) + ba)
    i = jax.nn.sigmoid(jnp.einsum('btnc,ncd->btnd', xb, wx.astype(jnp.float32)).reshape(B, T, W) + bx)
    log_a = -LRU_C * r * jax.nn.softplus(-lam.astype(jnp.float32))
    a = jnp.exp(log_a)
    b = jnp.sqrt(-jnp.expm1(2.0 * log_a)) * (i * xf)
    b = b.at[:, 0].add(a[:, 0] * h0.astype(jnp.float32))

    def combine(left, right):
        a1, b1 = left
        a2, b2 = right
        return a1 * a2, a2 * b1 + b2

    _, h = lax.associative_scan(combine, (a, b), axis=1)
    return h


def rglru_bidir(xc, p, h0f, h0b):
    hf = rglru_scan(xc, p['c_wa'][0], p['c_ba'][0], p['c_wx'][0], p['c_bx'][0], p['c_lam'][0], h0f)
    hb = jnp.flip(rglru_scan(jnp.flip(xc, 1), p['c_wa'][1], p['c_ba'][1], p['c_wx'][1],
                             p['c_bx'][1], p['c_lam'][1], h0b), 1)
    return hf, hb


def retention_scan(q, k, v, log_g, S0):
    B, T, H, D = q.shape
    C = RET_CHUNK
    n = T // C
    to_chunks = lambda t: t.reshape(B, n, C, H, D).transpose(1, 0, 3, 2, 4)
    idx = jnp.arange(C, dtype=jnp.float32)
    rel = idx[:, None] - idx[None, :]
    decay_mask = jnp.where(rel >= 0, jnp.exp(log_g[:, None, None] * jnp.maximum(rel, 0.0)), 0.0)
    q_decay = jnp.exp(log_g[:, None] * (idx + 1.0))[..., None]
    k_decay = jnp.exp(log_g[:, None] * (C - 1.0 - idx))[..., None]
    chunk_decay = jnp.exp(log_g * C)[:, None, None]

    def step(S, inp):
        qi, ki, vi = inp
        inner = jnp.einsum('bhqd,bhsd->bhqs', qi, ki) * decay_mask
        o = jnp.einsum('bhqs,bhsv->bhqv', inner, vi) + jnp.einsum('bhqd,bhdv->bhqv', qi * q_decay, S)
        S = S * chunk_decay + jnp.einsum('bhsd,bhsv->bhdv', ki * k_decay, vi)
        return S, o

    S, o = lax.scan(step, S0.astype(jnp.float32), (to_chunks(q), to_chunks(k), to_chunks(v)))
    return o.transpose(1, 0, 3, 2, 4).reshape(B, T, H, D), S


def retention_mixer(dq, dk, dv, dg, theta, gain, S0f, S0b):
    B, T, _ = dq.shape
    q = dq.reshape(B, T, H_D, HEAD_DIM).astype(jnp.float32)
    k = dk.reshape(B, T, H_D, HEAD_DIM).astype(jnp.float32) * HEAD_DIM ** -0.5
    v = dv.reshape(B, T, H_D, HEAD_DIM).astype(jnp.float32)
    log_g = jnp.log1p(-jnp.exp(theta.astype(jnp.float32)))
    of, Sf = retention_scan(q, k, v, log_g[0], S0f)
    ob, Sb = retention_scan(jnp.flip(q, 1), jnp.flip(k, 1), jnp.flip(v, 1), log_g[1], S0b)
    o = of + jnp.flip(ob, 1)
    o = o * lax.rsqrt(jnp.mean(o * o, axis=-1, keepdims=True) + EPS) * gain.astype(jnp.float32).reshape(H_D, HEAD_DIM)
    out = jax.nn.silu(dg.astype(jnp.float32)) * o.reshape(B, T, GROUP_W)
    return out.astype(dq.dtype), Sf, Sb


def mixer_context(u, p):
    B, T, _ = u.shape
    aq, ak, av, bq, bk, bv, cx, cy, dq, dk, dv, dg = split_columns(u)
    qa, ka = qk_heads(aq, ak, p['a_qn'], p['a_kn'], H_A, KV_A)
    va = av.reshape(B, T, KV_A, HEAD_DIM)
    oa = dense_attention(qa, ka, va, p['a_sink'].reshape(KV_A, H_A // KV_A))
    qb, kb = qk_heads(bq, bk, p['b_qn'], p['b_kn'], H_B, KV_B)
    vb = bv.reshape(B, T, KV_B, HEAD_DIM)
    ob = dense_attention(qb, kb, vb, None)
    xc = centred_conv(cx, p['c_conv_w'], p['c_conv_b'])
    h0 = jnp.zeros((B, W_C), jnp.float32)
    hf, hb = rglru_bidir(xc, p, h0, h0)
    oc = (hf + hb).astype(u.dtype) * jax.nn.gelu(cy)
    S0 = jnp.zeros((B, H_D, HEAD_DIM, HEAD_DIM), jnp.float32)
    od, Sf, Sb = retention_mixer(dq, dk, dv, dg, p['d_theta'], p['d_norm'], S0, S0)
    out = jnp.concatenate([oa, ob, oc, od], axis=-1)
    st_c = jnp.stack([hf[:, -1], hb[:, 0]], axis=1).astype(u.dtype)
    st_d = jnp.stack([Sf, Sb], axis=1).astype(u.dtype)
    return out, (ka, va, kb, vb, st_c, st_d)


def mixer_latent(u, p, cache):
    kca, vca, kcb, vcb, st_c, st_d = cache
    B, T, _ = u.shape
    aq, ak, av, bq, bk, bv, cx, cy, dq, dk, dv, dg = split_columns(u)
    qa, ka = qk_heads(aq, ak, p['a_qn'], p['a_kn'], H_A, KV_A)
    va = av.reshape(B, T, KV_A, HEAD_DIM)
    oa = window_attention(axial_rope(qa), axial_rope(ka), va, kca, vca,
                          p['a_sink'].reshape(KV_A, H_A // KV_A))
    qb, kb = qk_heads(bq, bk, p['b_qn'], p['b_kn'], H_B, KV_B)
    vb = bv.reshape(B, T, KV_B, HEAD_DIM)
    ob = dense_attention(axial_rope(qb), jnp.concatenate([kcb, axial_rope(kb)], axis=1),
                         jnp.concatenate([vcb, vb], axis=1), None)
    xc = centred_conv(cx, p['c_conv_w'], p['c_conv_b'])
    hf, hb = rglru_bidir(xc, p, st_c[:, 0], st_c[:, 1])
    oc = (hf + hb).astype(u.dtype) * jax.nn.gelu(cy)
    od, _, _ = retention_mixer(dq, dk, dv, dg, p['d_theta'], p['d_norm'], st_d[:, 0], st_d[:, 1])
    return jnp.concatenate([oa, ob, oc, od], axis=-1), ()


def trunk_layer(x, mod, p, mixer):
    sh1, sc1, g1, sh2, sc2, g2, sh3, sc3, g3 = jnp.split(mod[:, None, :], N_MOD, axis=-1)
    h = rms_norm(x, p['n1']) * (1.0 + sc1) + sh1
    x = x + 0.5 * g1 * swiglu(h, p['f1g'], p['f1u'], p['f1d'])
    h = rms_norm(x, p['n2']) * (1.0 + sc2) + sh2
    mixed, ctx = mixer(h @ p['w_in'])
    x = x + g2 * (mixed @ p['w_out'])
    h = rms_norm(x, p['n3']) * (1.0 + sc3) + sh3
    x = x + 0.5 * g3 * swiglu(h, p['f2g'], p['f2u'], p['f2d'])
    return x, ctx


def setup_inputs(seed: int = 0) -> dict:
    key = jax.random.key(seed)
    keys = iter(jax.random.split(key, 64))
    f32 = jnp.float32
    L = DEPTH

    def nrm(shape, scale=1.0):
        return jax.random.normal(next(keys), shape, f32) * scale

    def gain(shape):
        return 1.0 + nrm(shape, 0.02)

    u = jax.random.uniform(next(keys), (L, 2, W_C), f32, 0.9, 0.999)
    a_base = u ** (1.0 / LRU_C)
    c_lambda = jnp.log(a_base) - jnp.log1p(-a_base)
    theta_base = -(5.0 + jnp.arange(H_D, dtype=f32)) * math.log(2.0)
    d_theta = theta_base[None, None, :] + nrm((L, 2, H_D), 0.05)
    return {
        'x_prompt': nrm((BATCH, SEQ, D_MODEL)),
        'x_sample': nrm((DEC_BATCH, DEC_SEQ, D_MODEL)),
        'cache_a_k': nrm((DEC_BATCH, L, PAST_LEN, KV_A, HEAD_DIM)),
        'cache_a_v': nrm((DEC_BATCH, L, PAST_LEN, KV_A, HEAD_DIM)),
        'cache_b_k': nrm((DEC_BATCH, L, PAST_LEN, KV_B, HEAD_DIM)),
        'cache_b_v': nrm((DEC_BATCH, L, PAST_LEN, KV_B, HEAD_DIM)),
        'state_c': nrm((DEC_BATCH, L, 2, W_C), 0.5),
        'state_d': nrm((DEC_BATCH, L, 2, H_D, HEAD_DIM, HEAD_DIM)),
        'c': nrm((DEC_BATCH, D_MODEL)),
        'c_ctx': nrm((D_MODEL,)),
        'norm1_g': gain((L, D_MODEL)),
        'norm2_g': gain((L, D_MODEL)),
        'norm3_g': gain((L, D_MODEL)),
        'w_mod': nrm((L, D_MODEL, N_MOD * D_MODEL), D_MODEL ** -0.5),
        'b_mod': nrm((L, N_MOD * D_MODEL), 0.01),
        'ffn1_wg': nrm((L, D_MODEL, D_FF), D_MODEL ** -0.5),
        'ffn1_wu': nrm((L, D_MODEL, D_FF), D_MODEL ** -0.5),
        'ffn1_wd': nrm((L, D_FF, D_MODEL), D_FF ** -0.5),
        'ffn2_wg': nrm((L, D_MODEL, D_FF), D_MODEL ** -0.5),
        'ffn2_wu': nrm((L, D_MODEL, D_FF), D_MODEL ** -0.5),
        'ffn2_wd': nrm((L, D_FF, D_MODEL), D_FF ** -0.5),
        'w_in': nrm((L, D_MODEL, IN_WIDTH), D_MODEL ** -0.5),
        'w_out': nrm((L, MIX_WIDTH, D_MODEL), MIX_WIDTH ** -0.5),
        'a_qn': gain((L, HEAD_DIM)),
        'a_kn': gain((L, HEAD_DIM)),
        'a_sink': nrm((L, H_A), 0.5),
        'b_qn': gain((L, HEAD_DIM)),
        'b_kn': gain((L, HEAD_DIM)),
        'c_conv_w': nrm((L, CONV_W, W_C), CONV_W ** -0.5),
        'c_conv_b': nrm((L, W_C), 0.01),
        'c_wa': nrm((L, 2, C_BLOCKS, C_BLOCK_W, C_BLOCK_W), C_BLOCK_W ** -0.5),
        'c_ba': nrm((L, 2, W_C), 0.01),
        'c_wx': nrm((L, 2, C_BLOCKS, C_BLOCK_W, C_BLOCK_W), C_BLOCK_W ** -0.5),
        'c_bx': nrm((L, 2, W_C), 0.01),
        'c_lambda': c_lambda,
        'd_theta': d_theta,
        'd_norm_g': gain((L, GROUP_W)),
    }


def reference(x_prompt, x_sample, cache_a_k, cache_a_v, cache_b_k, cache_b_v, state_c, state_d,
              c, c_ctx, norm1_g, norm2_g, norm3_g, w_mod, b_mod,
              ffn1_wg, ffn1_wu, ffn1_wd, ffn2_wg, ffn2_wu, ffn2_wd, w_in, w_out,
              a_qn, a_kn, a_sink, b_qn, b_kn, c_conv_w, c_conv_b, c_wa, c_ba, c_wx, c_bx,
              c_lambda, d_theta, d_norm_g):
    cond_ctx = jax.nn.silu(c_ctx)[None, :]
    cond_lat = jax.nn.silu(c)
    y_p = x_prompt
    y_s = x_sample
    collected = [[], [], [], [], [], []]
    for l in range(DEPTH):
        p = {'n1': norm1_g[l], 'n2': norm2_g[l], 'n3': norm3_g[l],
             'f1g': ffn1_wg[l], 'f1u': ffn1_wu[l], 'f1d': ffn1_wd[l],
             'f2g': ffn2_wg[l], 'f2u': ffn2_wu[l], 'f2d': ffn2_wd[l],
             'w_in': w_in[l], 'w_out': w_out[l],
             'a_qn': a_qn[l], 'a_kn': a_kn[l], 'a_sink': a_sink[l],
             'b_qn': b_qn[l], 'b_kn': b_kn[l],
             'c_conv_w': c_conv_w[l], 'c_conv_b': c_conv_b[l],
             'c_wa': c_wa[l], 'c_ba': c_ba[l], 'c_wx': c_wx[l], 'c_bx': c_bx[l], 'c_lam': c_lambda[l],
             'd_theta': d_theta[l], 'd_norm': d_norm_g[l]}
        mod_ctx = cond_ctx @ w_mod[l] + b_mod[l]
        mod_lat = cond_lat @ w_mod[l] + b_mod[l]
        y_p, ctx = trunk_layer(y_p, mod_ctx, p, lambda u: mixer_context(u, p))
        for store, t in zip(collected, ctx):
            store.append(t)
        cache_l = (cache_a_k[:, l], cache_a_v[:, l], cache_b_k[:, l], cache_b_v[:, l],
                   state_c[:, l], state_d[:, l])
        y_s, _ = trunk_layer(y_s, mod_lat, p, lambda u: mixer_latent(u, p, cache_l))
    new_cache_a_k = jnp.stack(collected[0], axis=1)
    new_cache_a_v = jnp.stack(collected[1], axis=1)
    new_cache_b_k = jnp.stack(collected[2], axis=1)
    new_cache_b_v = jnp.stack(collected[3], axis=1)
    new_state_c = jnp.stack(collected[4], axis=1)
    new_state_d = jnp.stack(collected[5], axis=1)
    return (y_p, y_s, new_cache_a_k, new_cache_a_v, new_cache_b_k, new_cache_b_v, new_state_c, new_state_d)
```

```python
import functools
import math

import jax
import jax.numpy as jnp
from jax import lax
from jax.experimental import pallas as pl
from jax.experimental.pallas import tpu as pltpu

F32 = jnp.float32
BF16 = jnp.bfloat16

D_MODEL = 1024
BATCH = 16
SEQ = 256
DEPTH = 2
DEC_BATCH = 2
DEC_SEQ = 1024
PAST_LEN = 512
GRID_W = 64
HEAD_DIM = 64
HEAD_SHIFT = 6
GROUP_W = 256
WINDOW = 128
ATT_BLOCK = 128
ROPE_BASE = 10000.0
LRU_C = 8.0
D_FF = 2816
N_MOD = 9
EPS = 1e-6
NEG_INF = -1e30
IN_WIDTH = 2560

N_CTX_ROWS = BATCH * SEQ
N_LAT_ROWS = DEC_BATCH * DEC_SEQ
N_ROWS = N_CTX_ROWS + N_LAT_ROWS
MOD_ROWS = 8
MOD_GROUP = 1024

VMEM_LIMIT_BYTES = 56 * 1024 * 1024

COL_AQ, COL_AK, COL_AV = 0, 256, 384
COL_BQ, COL_BK, COL_BV = 512, 768, 896
COL_CX, COL_CY = 1024, 1280
COL_DQ, COL_DK, COL_DV, COL_DG = 1536, 1792, 2048, 2304


def _cparams(*sem):
    return pltpu.CompilerParams(dimension_semantics=sem, vmem_limit_bytes=VMEM_LIMIT_BYTES)


def _dot(a, b):
    return jnp.dot(a, b, preferred_element_type=F32)


def _dot_nt(a, b):
    return lax.dot_general(a, b, (((1,), (1,)), ((), ())), preferred_element_type=F32)


def _dot_tn(a, b):
    return lax.dot_general(a, b, (((0,), (0,)), ((), ())), preferred_element_type=F32)


def _silu(x):
    return x * jax.nn.sigmoid(x)


def _gelu_tanh(x):
    return 0.5 * x * (1.0 + jnp.tanh(math.sqrt(2.0 / math.pi) * (x + 0.044715 * (x * x * x))))


def _mod_row(i, tm, s):
    if tm >= MOD_GROUP:
        block_index = i * (tm // MOD_GROUP) + s
    else:
        block_index = i >> int(math.log2(MOD_GROUP // tm))
    return jnp.maximum(block_index - (N_CTX_ROWS // MOD_GROUP - 1), 0)


def _norm_mod(x, g, sc, sh):
    ms = jnp.mean(x * x, axis=-1, keepdims=True)
    return (x * lax.rsqrt(ms + EPS) * g) * (1.0 + sc) + sh


MOD_TN = 1024


def _mod_kernel(c_ref, w_ref, b_ref, o_ref):
    s = _silu(c_ref[...]).astype(BF16)
    o_ref[...] = _dot(s, w_ref[...].astype(BF16)) + b_ref[...]


def _modulation(cond, w_mod, b_mod):
    n = N_MOD * D_MODEL
    return pl.pallas_call(
        _mod_kernel,
        grid=(DEPTH, n // MOD_TN),
        in_specs=[
            pl.BlockSpec((MOD_ROWS, D_MODEL), lambda l, j: (0, 0)),
            pl.BlockSpec((None, D_MODEL, MOD_TN), lambda l, j: (l, 0, j)),
            pl.BlockSpec((None, 1, MOD_TN), lambda l, j: (l, 0, j)),
        ],
        out_specs=pl.BlockSpec((None, MOD_ROWS, MOD_TN), lambda l, j: (l, 0, j)),
        out_shape=jax.ShapeDtypeStruct((DEPTH, MOD_ROWS, n), F32),
        compiler_params=_cparams("arbitrary", "arbitrary"),
        name="modulation",
    )(cond, w_mod, b_mod.reshape(DEPTH, 1, n))


FFN_TM = 1024
FFN_TF = 256


def _ffn_kernel(x_ref, n_ref, sh_ref, sc_ref, g_ref, wg_ref, wu_ref, wd_ref, o_ref, h_ref, *, tm):
    i = pl.program_id(0)
    j = pl.program_id(1)
    sub = min(tm, MOD_GROUP)

    @pl.when(j == 0)
    def _():
        for s in range(tm // sub):
            r = _mod_row(i, tm, s)
            rows = pl.ds(s * sub, sub)
            x = x_ref[rows, :]
            h = _norm_mod(x, n_ref[...], sc_ref[pl.ds(r, 1), :], sh_ref[pl.ds(r, 1), :])
            h_ref[rows, :] = h.astype(BF16)
            o_ref[rows, :] = x

    h = h_ref[...]
    a = _silu(_dot(h, wg_ref[...].astype(BF16))) * _dot(h, wu_ref[...].astype(BF16))
    y = _dot(a.astype(BF16), wd_ref[...].astype(BF16))
    for s in range(tm // sub):
        r = _mod_row(i, tm, s)
        rows = pl.ds(s * sub, sub)
        o_ref[rows, :] += (0.5 * g_ref[pl.ds(r, 1), :]) * y[s * sub:(s + 1) * sub, :]


def _ffn(x, mod, layer, chunk0, norm_g, wg, wu, wd):
    tm, tf = FFN_TM, FFN_TF
    mod_spec = lambda c: pl.BlockSpec((None, MOD_ROWS, D_MODEL), lambda i, j: (layer, 0, c))
    return pl.pallas_call(
        functools.partial(_ffn_kernel, tm=tm),
        grid=(N_ROWS // tm, D_FF // tf),
        in_specs=[
            pl.BlockSpec((tm, D_MODEL), lambda i, j: (i, 0)),
            pl.BlockSpec((None, 1, D_MODEL), lambda i, j: (layer, 0, 0)),
            mod_spec(chunk0), mod_spec(chunk0 + 1), mod_spec(chunk0 + 2),
            pl.BlockSpec((None, D_MODEL, tf), lambda i, j: (layer, 0, j)),
            pl.BlockSpec((None, D_MODEL, tf), lambda i, j: (layer, 0, j)),
            pl.BlockSpec((None, tf, D_MODEL), lambda i, j: (layer, j, 0)),
        ],
        out_specs=pl.BlockSpec((tm, D_MODEL), lambda i, j: (i, 0)),
        out_shape=jax.ShapeDtypeStruct((N_ROWS, D_MODEL), F32),
        scratch_shapes=[pltpu.VMEM((tm, D_MODEL), BF16)],
        compiler_params=_cparams("arbitrary", "arbitrary"),
        name="ffn",
    )(x, norm_g.reshape(DEPTH, 1, D_MODEL), mod, mod, mod, wg, wu, wd)


INP_TM = 1024
INP_TN = 512


def _inproj_kernel(x_ref, n_ref, sh_ref, sc_ref, w_ref, o_ref, h_ref, *, tm):
    i = pl.program_id(0)
    sub = min(tm, MOD_GROUP)

    @pl.when(pl.program_id(1) == 0)
    def _():
        for s in range(tm // sub):
            r = _mod_row(i, tm, s)
            rows = pl.ds(s * sub, sub)
            h = _norm_mod(x_ref[rows, :], n_ref[...], sc_ref[pl.ds(r, 1), :], sh_ref[pl.ds(r, 1), :])
            h_ref[rows, :] = h.astype(BF16)

    o_ref[...] = _dot(h_ref[...], w_ref[...].astype(BF16))


def _inproj(x, mod, layer, norm_g, w_in):
    tm, tn = INP_TM, INP_TN
    mod_spec = lambda c: pl.BlockSpec((None, MOD_ROWS, D_MODEL), lambda i, j: (layer, 0, c))
    return pl.pallas_call(
        functools.partial(_inproj_kernel, tm=tm),
        grid=(N_ROWS // tm, IN_WIDTH // tn),
        in_specs=[
            pl.BlockSpec((tm, D_MODEL), lambda i, j: (i, 0)),
            pl.BlockSpec((None, 1, D_MODEL), lambda i, j: (layer, 0, 0)),
            mod_spec(3), mod_spec(4),
            pl.BlockSpec((None, D_MODEL, tn), lambda i, j: (layer, 0, j)),
        ],
        out_specs=pl.BlockSpec((tm, tn), lambda i, j: (i, j)),
        out_shape=jax.ShapeDtypeStruct((N_ROWS, IN_WIDTH), F32),
        scratch_shapes=[pltpu.VMEM((tm, D_MODEL), BF16)],
        compiler_params=_cparams("arbitrary", "arbitrary"),
        name="inproj",
    )(x, norm_g.reshape(DEPTH, 1, D_MODEL), mod, mod, w_in)


OUTP_TM = 1024


def _outproj_kernel(x_ref, m_ref, g_ref, w_ref, o_ref, *, tm):
    i = pl.program_id(0)
    sub = min(tm, MOD_GROUP)
    y = _dot(m_ref[...], w_ref[...].astype(BF16))
    for s in range(tm // sub):
        r = _mod_row(i, tm, s)
        rows = pl.ds(s * sub, sub)
        o_ref[rows, :] = x_ref[rows, :] + g_ref[pl.ds(r, 1), :] * y[s * sub:(s + 1) * sub, :]


def _outproj(x, mixed, mod, layer, w_out):
    tm = OUTP_TM
    return pl.pallas_call(
        functools.partial(_outproj_kernel, tm=tm),
        grid=(N_ROWS // tm,),
        in_specs=[
            pl.BlockSpec((tm, D_MODEL), lambda i: (i, 0)),
            pl.BlockSpec((tm, D_MODEL), lambda i: (i, 0)),
            pl.BlockSpec((None, MOD_ROWS, D_MODEL), lambda i: (layer, 0, 5)),
            pl.BlockSpec((None, D_MODEL, D_MODEL), lambda i: (layer, 0, 0)),
        ],
        out_specs=pl.BlockSpec((tm, D_MODEL), lambda i: (i, 0)),
        out_shape=jax.ShapeDtypeStruct((N_ROWS, D_MODEL), F32),
        compiler_params=_cparams("arbitrary"),
        name="outproj",
    )(x, mixed, mod, w_out)


def _head_mean_square(x):
    n = x.shape[-1]
    x2 = x * x
    hi = x2.astype(BF16)
    lo = (x2 - hi.astype(F32)).astype(BF16)
    r = lax.broadcasted_iota(jnp.int32, (n, n), 0) >> HEAD_SHIFT
    c = lax.broadcasted_iota(jnp.int32, (n, n), 1) >> HEAD_SHIFT
    ones_bd = jnp.where(r == c, 1.0, 0.0).astype(BF16)
    return (_dot(hi, ones_bd) + _dot(lo, ones_bd)) * (1.0 / HEAD_DIM)


def _head_norm(x, gain_row):
    return x * lax.rsqrt(_head_mean_square(x) + EPS) * gain_row


def _head_cols(x, h):
    return x[:, h * HEAD_DIM:(h + 1) * HEAD_DIM].astype(BF16)


def _rope(x, cos, sin_lo, sin_hi):
    cols = []
    for c in range(x.shape[-1] // 128):
        xc = x[:, c * 128:(c + 1) * 128]
        cols.append(xc * cos + pltpu.roll(xc, 112, 1) * sin_lo + pltpu.roll(xc, 16, 1) * sin_hi)
    return cols[0] if len(cols) == 1 else jnp.concatenate(cols, axis=-1)


def _softmax_pv(scores, values, sink):
    m = jnp.max(scores[0], axis=-1, keepdims=True)
    for s in scores[1:]:
        m = jnp.maximum(m, jnp.max(s, axis=-1, keepdims=True))
    if sink is not None:
        m = jnp.maximum(m, sink)
    denom = None
    acc = None
    for s, v in zip(scores, values):
        p = jnp.exp(s - m)
        d = jnp.sum(p, axis=-1, keepdims=True)
        o = _dot(p.astype(BF16), v)
        denom = d if denom is None else denom + d
        acc = o if acc is None else acc + o
    if sink is not None:
        denom = denom + jnp.exp(sink - m)
    return acc / denom


def _rglru_gates(xc, wa, ba, wx, bx, lam):
    xb = xc.astype(BF16)
    r = jax.nn.sigmoid(_dot(xb, wa) + ba)
    i = jax.nn.sigmoid(_dot(xb, wx) + bx)
    softplus = jnp.maximum(-lam, 0.0) + jnp.log1p(jnp.exp(-jnp.abs(lam)))
    log_a = (-LRU_C) * r * softplus
    a = jnp.exp(log_a)
    b = jnp.sqrt(-jnp.tanh(log_a) * (a * a + 1.0)) * (i * xc)
    return a, b


def _block_prefix(a, b, reverse):
    t = a.shape[0]
    row = lax.broadcasted_iota(jnp.int32, a.shape, 0) & 7
    for d in (1, 2, 4):
        if reverse:
            a_s = pltpu.roll(a, t - d, 0)
            b_s = pltpu.roll(b, t - d, 0)
            ok = row < 8 - d
        else:
            a_s = pltpu.roll(a, d, 0)
            b_s = pltpu.roll(b, d, 0)
            ok = row >= d
        b = jnp.where(ok, a * b_s + b, b)
        a = jnp.where(ok, a * a_s, a)
    return a, b


def _conv4(x, w_ref, b_row):
    t = x.shape[0]
    row = lax.broadcasted_iota(jnp.int32, x.shape, 0)
    xm2 = jnp.where(row >= 2, pltpu.roll(x, 2, 0), 0.0)
    xm1 = jnp.where(row >= 1, pltpu.roll(x, 1, 0), 0.0)
    xp1 = jnp.where(row < t - 1, pltpu.roll(x, t - 1, 0), 0.0)
    return (xm2 * w_ref[0:1, :] + xm1 * w_ref[1:2, :] + x * w_ref[2:3, :] + xp1 * w_ref[3:4, :]) + b_row


def _rglru_mixer(cx, cy, conv_w_ref, conv_b, wa_ref, ba_ref, wx_ref, bx_ref, lam_ref, h0f, h0b,
                 af_ref, bf_ref, ab_ref, bb_ref, hf_ref, hb_ref):
    t = cx.shape[0]
    xc = _conv4(cx, conv_w_ref, conv_b)
    a, b = _rglru_gates(xc, wa_ref[0], ba_ref[0:1, :], wx_ref[0], bx_ref[0:1, :], lam_ref[0:1, :])
    a, b = _block_prefix(a, b, reverse=False)
    af_ref[...] = a
    bf_ref[...] = b
    a, b = _rglru_gates(xc, wa_ref[1], ba_ref[1:2, :], wx_ref[1], bx_ref[1:2, :], lam_ref[1:2, :])
    a, b = _block_prefix(a, b, reverse=True)
    ab_ref[...] = a
    bb_ref[...] = b
    nblk = t // 8

    def body(k, carry):
        cf, cb = carry
        rf = pl.ds(pl.multiple_of(k * 8, 8), 8)
        hf = bf_ref[rf, :] + af_ref[rf, :] * cf
        hf_ref[rf, :] = hf
        rb = pl.ds(pl.multiple_of((nblk - 1 - k) * 8, 8), 8)
        hb = bb_ref[rb, :] + ab_ref[rb, :] * cb
        hb_ref[rb, :] = hb
        return hf[7:8, :], hb[0:1, :]

    cf, cb = lax.fori_loop(0, nblk, body, (h0f, h0b))
    oc = (hf_ref[...] + hb_ref[...]) * _gelu_tanh(cy)
    return oc, cf, cb


def _lane_head_masks(n):
    lane = lax.broadcasted_iota(jnp.int32, (1, n), 1) >> HEAD_SHIFT
    return [jnp.where(lane == h, 1.0, 0.0) for h in range(n // HEAD_DIM)]


def _retention_block(q, k8b, vb, q0, lgf, lgb, masks):
    tq, w = q.shape
    s_len = k8b.shape[0]
    nh = w // HEAD_DIM
    q_stack = jnp.concatenate([(q * masks[h]).astype(BF16) for h in range(nh)], axis=0)
    raw = _dot_nt(q_stack, k8b)
    qi = lax.broadcasted_iota(jnp.int32, (tq, s_len), 0) + q0
    si = lax.broadcasted_iota(jnp.int32, (tq, s_len), 1)
    rel = (qi - si).astype(F32)
    decs = []
    for h in range(nh):
        gf = lgf[:, h * HEAD_DIM:h * HEAD_DIM + 1]
        gb = lgb[:, h * HEAD_DIM:h * HEAD_DIM + 1]
        e = jnp.exp(jnp.where(rel >= 0, gf * rel, gb * (-rel)))
        decs.append(jnp.where(rel == 0, 2.0, e))
    inner = (raw * jnp.concatenate(decs, axis=0)).astype(BF16)
    out = _dot(inner, vb)
    o = out[0:tq, :] * masks[0]
    for h in range(1, nh):
        o = o + out[h * tq:(h + 1) * tq, :] * masks[h]
    return o


def _log_decay(theta_row):
    return jnp.log1p(-jnp.exp(theta_row))


def _ctx_mixer_kernel(u_ref, aqn_ref, akn_ref, bqn_ref, bkn_ref, sink_ref,
                      convw_ref, convb_ref, wa_ref, ba_ref, wx_ref, bx_ref, lam_ref,
                      theta_ref, dn_ref,
                      mixed_ref, ka_ref, va_ref, kb_ref, vb_ref, stc_ref, std_ref,
                      af_ref, bf_ref, ab_ref, bb_ref, hf_ref, hb_ref):
    t = SEQ
    for (cq, ck, cv, qn_ref, kn_ref, k_out, v_out, col0, use_sink) in (
            (COL_AQ, COL_AK, COL_AV, aqn_ref, akn_ref, ka_ref, va_ref, 0, True),
            (COL_BQ, COL_BK, COL_BV, bqn_ref, bkn_ref, kb_ref, vb_ref, GROUP_W, False)):
        q = _head_norm(u_ref[:, cq:cq + 256], qn_ref[...])
        k = _head_norm(u_ref[:, ck:ck + 128], kn_ref[:, 0:128])
        v = u_ref[:, cv:cv + 128]
        k_out[...] = k
        v_out[...] = v
        qs = q * (HEAD_DIM ** -0.5)
        heads = []
        for h in range(4):
            kv = h // 2
            s = _dot_nt(_head_cols(qs, h), _head_cols(k, kv))
            sink = jnp.full((t, 1), sink_ref[h], F32) if use_sink else None
            heads.append(_softmax_pv([s], [_head_cols(v, kv)], sink))
        mixed_ref[:, col0:col0 + GROUP_W] = jnp.concatenate(heads, axis=-1).astype(BF16)

    zero = jnp.zeros((1, GROUP_W), F32)
    oc, cf, cb = _rglru_mixer(u_ref[:, COL_CX:COL_CX + 256], u_ref[:, COL_CY:COL_CY + 256],
                              convw_ref, convb_ref[...], wa_ref, ba_ref, wx_ref, bx_ref, lam_ref,
                              zero, zero, af_ref, bf_ref, ab_ref, bb_ref, hf_ref, hb_ref)
    mixed_ref[:, 2 * GROUP_W:3 * GROUP_W] = oc.astype(BF16)
    stc_ref[0:1, :] = cf
    stc_ref[1:2, :] = cb

    lgf = _log_decay(theta_ref[0:1, :])
    lgb = _log_decay(theta_ref[1:2, :])
    masks = _lane_head_masks(GROUP_W)
    q = u_ref[:, COL_DQ:COL_DQ + 256]
    k8 = u_ref[:, COL_DK:COL_DK + 256] * (HEAD_DIM ** -0.5)
    vb = u_ref[:, COL_DV:COL_DV + 256].astype(BF16)
    o = _retention_block(q, k8.astype(BF16), vb, 0, lgf, lgb, masks)
    o = _head_norm(o, dn_ref[...]) * _silu(u_ref[:, COL_DG:COL_DG + 256])
    mixed_ref[:, 3 * GROUP_W:4 * GROUP_W] = o.astype(BF16)
    pos = lax.broadcasted_iota(jnp.int32, (t, GROUP_W), 0).astype(F32)
    for d, (lg, expo) in enumerate(((lgf, (t - 1.0) - pos), (lgb, pos))):
        s_full = _dot_tn((k8 * jnp.exp(lg * expo)).astype(BF16), vb)
        for h in range(4):
            std_ref[d, h] = s_full[h * 64:(h + 1) * 64, h * 64:(h + 1) * 64]


def _ctx_mixers(u, p):
    full = lambda shape: pl.BlockSpec(shape, lambda b: (0,) * len(shape))
    seq_out = lambda w: pl.BlockSpec((None, SEQ, w), lambda b: (b, 0, 0))
    scr = pltpu.VMEM((SEQ, GROUP_W), F32)
    return pl.pallas_call(
        _ctx_mixer_kernel,
        grid=(BATCH,),
        in_specs=[
            pl.BlockSpec((SEQ, IN_WIDTH), lambda b: (b, 0)),
            full((1, 256)), full((1, 256)), full((1, 256)), full((1, 256)),
            pl.BlockSpec(memory_space=pltpu.SMEM),
            full((4, GROUP_W)), full((1, GROUP_W)),
            full((2, GROUP_W, GROUP_W)), full((2, GROUP_W)),
            full((2, GROUP_W, GROUP_W)), full((2, GROUP_W)), full((2, GROUP_W)),
            full((2, GROUP_W)), full((1, GROUP_W)),
        ],
        out_specs=[
            pl.BlockSpec((SEQ, D_MODEL), lambda b: (b, 0)),
            seq_out(128), seq_out(128), seq_out(128), seq_out(128),
            pl.BlockSpec((None, 2, GROUP_W), lambda b: (b, 0, 0)),
            pl.BlockSpec((None, 2, 4, HEAD_DIM, HEAD_DIM), lambda b: (b, 0, 0, 0, 0)),
        ],
        out_shape=[
            jax.ShapeDtypeStruct((N_ROWS, D_MODEL), BF16),
            jax.ShapeDtypeStruct((BATCH, SEQ, 128), F32),
            jax.ShapeDtypeStruct((BATCH, SEQ, 128), F32),
            jax.ShapeDtypeStruct((BATCH, SEQ, 128), F32),
            jax.ShapeDtypeStruct((BATCH, SEQ, 128), F32),
            jax.ShapeDtypeStruct((BATCH, 2, GROUP_W), F32),
            jax.ShapeDtypeStruct((BATCH, 2, 4, HEAD_DIM, HEAD_DIM), F32),
        ],
        scratch_shapes=[scr] * 6,
        compiler_params=_cparams("arbitrary"),
        name="ctx_mixers",
    )(u, p["a_qn"], p["a_kn"], p["b_qn"], p["b_kn"], p["a_sink"],
      p["conv_w"], p["conv_b"], p["wa"], p["ba"], p["wx"], p["bx"], p["lam"],
      p["theta"], p["d_norm"])


LAT_BLOCK0 = N_CTX_ROWS // DEC_SEQ


def _lat_attn_kernel(mixed_in_ref, u_ref, kca_ref, vca_ref, kcb_ref, vcb_ref,
                     aqn_ref, akn_ref, bqn_ref, bkn_ref, sink_ref, cos_ref, sinl_ref, sinh_ref,
                     o_ref):
    del mixed_in_ref
    t = DEC_SEQ
    cos, sin_lo, sin_hi = cos_ref[...], sinl_ref[...], sinh_ref[...]
    scale = HEAD_DIM ** -0.5

    q = _rope(_head_norm(u_ref[:, COL_AQ:COL_AQ + 256], aqn_ref[...]), cos, sin_lo, sin_hi)
    k = _rope(_head_norm(u_ref[:, COL_AK:COL_AK + 128], akn_ref[:, 0:128]), cos, sin_lo, sin_hi)
    qh = [_head_cols(q * scale, h) for h in range(4)]
    v = u_ref[:, COL_AV:COL_AV + 128]
    kh = [_head_cols(k, kv) for kv in range(2)]
    vh = [_head_cols(v, kv) for kv in range(2)]
    kch = [_head_cols(kca_ref[...], kv) for kv in range(2)]
    vch = [_head_cols(vca_ref[...], kv) for kv in range(2)]
    w = ATT_BLOCK
    span = 3 * w
    for n in range(t // w):
        start = min(max((n - 1) * w, 0), t - span)
        rows = slice(n * w, (n + 1) * w)
        band = slice(start, start + span)
        qpos = (lax.broadcasted_iota(jnp.int32, (2 * w, span), 0) & (w - 1)) + n * w
        kpos = lax.broadcasted_iota(jnp.int32, (2 * w, span), 1) + start
        valid = jnp.abs(qpos - kpos) <= WINDOW
        heads = []
        for kv in range(2):
            qp = jnp.concatenate([qh[2 * kv][rows, :], qh[2 * kv + 1][rows, :]], axis=0)
            s_ctx = _dot_nt(qp, kch[kv])
            s_band = jnp.where(valid, _dot_nt(qp, kh[kv][band, :]), NEG_INF)
            row = lax.broadcasted_iota(jnp.int32, (2 * w, 1), 0)
            sink = jnp.where(row < w, sink_ref[2 * kv], sink_ref[2 * kv + 1])
            o = _softmax_pv([s_ctx, s_band], [vch[kv], vh[kv][band, :]], sink)
            heads += [o[0:w, :], o[w:2 * w, :]]
        o_ref[rows, 0:GROUP_W] = jnp.concatenate(heads, axis=-1).astype(BF16)

    q = _rope(_head_norm(u_ref[:, COL_BQ:COL_BQ + 256], bqn_ref[...]), cos, sin_lo, sin_hi)
    k = _rope(_head_norm(u_ref[:, COL_BK:COL_BK + 128], bkn_ref[:, 0:128]), cos, sin_lo, sin_hi)
    qh = [_head_cols(q * scale, h) for h in range(4)]
    v = u_ref[:, COL_BV:COL_BV + 128]
    kh = [_head_cols(k, kv) for kv in range(2)]
    vh = [_head_cols(v, kv) for kv in range(2)]
    kch = [_head_cols(kcb_ref[...], kv) for kv in range(2)]
    vch = [_head_cols(vcb_ref[...], kv) for kv in range(2)]
    tq = 256
    for n in range(t // tq):
        rows = slice(n * tq, (n + 1) * tq)
        heads = []
        for kv in range(2):
            qp = jnp.concatenate([qh[2 * kv][rows, :], qh[2 * kv + 1][rows, :]], axis=0)
            o = _softmax_pv([_dot_nt(qp, kch[kv]), _dot_nt(qp, kh[kv])], [vch[kv], vh[kv]], None)
            heads += [o[0:tq, :], o[tq:2 * tq, :]]
        o_ref[rows, GROUP_W:2 * GROUP_W] = jnp.concatenate(heads, axis=-1).astype(BF16)


def _lat_rglru_kernel(mixed_in_ref, cx_ref, cy_ref, h0_ref,
                      convw_ref, convb_ref, wa_ref, ba_ref, wx_ref, bx_ref, lam_ref,
                      o_ref, af_ref, bf_ref, ab_ref, bb_ref, hf_ref, hb_ref):
    del mixed_in_ref
    oc, _, _ = _rglru_mixer(cx_ref[...], cy_ref[...], convw_ref, convb_ref[...],
                            wa_ref, ba_ref, wx_ref, bx_ref, lam_ref,
                            h0_ref[0:1, :], h0_ref[1:2, :],
                            af_ref, bf_ref, ab_ref, bb_ref, hf_ref, hb_ref)
    o_ref[...] = oc.astype(BF16)


def _lat_retention_kernel(mixed_in_ref, q_ref, k_ref, v_ref, g_ref, s0_ref, theta_ref, dn_ref, o_ref):
    del mixed_in_ref
    t = DEC_SEQ
    lgf = _log_decay(theta_ref[0:1, :])
    lgb = _log_decay(theta_ref[1:2, :])
    masks = _lane_head_masks(GROUP_W)
    k8b = (k_ref[...] * (HEAD_DIM ** -0.5)).astype(BF16)
    vb = v_ref[...].astype(BF16)
    s0f = s0_ref[0].astype(BF16)
    s0b = s0_ref[1].astype(BF16)
    tq = 256
    for n in range(t // tq):
        rows = slice(n * tq, (n + 1) * tq)
        q = q_ref[rows, :]
        o = _retention_block(q, k8b, vb, n * tq, lgf, lgb, masks)
        pos = lax.broadcasted_iota(jnp.int32, (tq, GROUP_W), 0).astype(F32) + float(n * tq)
        o = o + _dot((q * jnp.exp(lgf * (pos + 1.0))).astype(BF16), s0f)
        o = o + _dot((q * jnp.exp(lgb * (float(t) - pos))).astype(BF16), s0b)
        o = _head_norm(o, dn_ref[...]) * _silu(g_ref[rows, :])
        o_ref[rows, :] = o.astype(BF16)


def _lat_mixers(mixed, u, cache, rope, p):
    kca, vca, kcb, vcb, h0, s0 = cache
    full = lambda shape: pl.BlockSpec(shape, lambda b: (0,) * len(shape))
    any_spec = pl.BlockSpec(memory_space=pl.ANY)
    ucols = lambda w, c: pl.BlockSpec((DEC_SEQ, w), lambda b: (LAT_BLOCK0 + b, c))
    ocols = lambda w, c: pl.BlockSpec((DEC_SEQ, w), lambda b: (LAT_BLOCK0 + b, c))
    cache_spec = pl.BlockSpec((None, PAST_LEN, 128), lambda b: (b, 0, 0))
    out_shape = jax.ShapeDtypeStruct((N_ROWS, D_MODEL), BF16)

    mixed = pl.pallas_call(
        _lat_attn_kernel,
        grid=(DEC_BATCH,),
        in_specs=[any_spec, ucols(1024, 0), cache_spec, cache_spec, cache_spec, cache_spec,
                  full((1, 256)), full((1, 256)), full((1, 256)), full((1, 256)),
                  pl.BlockSpec(memory_space=pltpu.SMEM),
                  full((DEC_SEQ, 128)), full((DEC_SEQ, 128)), full((DEC_SEQ, 128))],
        out_specs=ocols(2 * GROUP_W, 0),
        out_shape=out_shape,
        input_output_aliases={0: 0},
        compiler_params=_cparams("arbitrary"),
        name="lat_attention",
    )(mixed, u, kca, vca, kcb, vcb, p["a_qn"], p["a_kn"], p["b_qn"], p["b_kn"], p["a_sink"], *rope)

    scr = pltpu.VMEM((DEC_SEQ, GROUP_W), F32)
    mixed = pl.pallas_call(
        _lat_rglru_kernel,
        grid=(DEC_BATCH,),
        in_specs=[any_spec, ucols(GROUP_W, COL_CX // GROUP_W), ucols(GROUP_W, COL_CY // GROUP_W),
                  pl.BlockSpec((None, 2, GROUP_W), lambda b: (b, 0, 0)),
                  full((4, GROUP_W)), full((1, GROUP_W)),
                  full((2, GROUP_W, GROUP_W)), full((2, GROUP_W)),
                  full((2, GROUP_W, GROUP_W)), full((2, GROUP_W)), full((2, GROUP_W))],
        out_specs=ocols(GROUP_W, 2),
        out_shape=out_shape,
        input_output_aliases={0: 0},
        scratch_shapes=[scr] * 6,
        compiler_params=_cparams("arbitrary"),
        name="lat_rglru",
    )(mixed, u, u, h0, p["conv_w"], p["conv_b"], p["wa"], p["ba"], p["wx"], p["bx"], p["lam"])

    mixed = pl.pallas_call(
        _lat_retention_kernel,
        grid=(DEC_BATCH,),
        in_specs=[any_spec,
                  ucols(GROUP_W, COL_DQ // GROUP_W), ucols(GROUP_W, COL_DK // GROUP_W),
                  ucols(GROUP_W, COL_DV // GROUP_W), ucols(GROUP_W, COL_DG // GROUP_W),
                  pl.BlockSpec((None, 2, GROUP_W, GROUP_W), lambda b: (b, 0, 0, 0)),
                  full((2, GROUP_W)), full((1, GROUP_W))],
        out_specs=ocols(GROUP_W, 3),
        out_shape=out_shape,
        input_output_aliases={0: 0},
        compiler_params=_cparams("arbitrary"),
        name="lat_retention",
    )(mixed, u, u, u, u, s0, p["theta"], p["d_norm"])
    return mixed


def _block_diag(blocks):
    n, w = blocks.shape[-3], blocks.shape[-1]
    eye = jnp.eye(n, dtype=blocks.dtype)
    out = blocks[..., :, :, None, :] * eye[:, None, :, None]
    return out.reshape(blocks.shape[:-3] + (n * w, n * w))


def _rope_tables():
    t = jnp.arange(DEC_SEQ)
    row = (t // GRID_W).astype(F32)[:, None]
    col = (t % GRID_W).astype(F32)[:, None]
    half = HEAD_DIM // 2
    inv = 1.0 / (ROPE_BASE ** (jnp.arange(0, half, 2, dtype=F32) / half))
    lane = jnp.arange(128)
    j = lane % HEAD_DIM
    freq = inv[j % (half // 2)][None, :]
    pos = jnp.where((j < half)[None, :], row, col)
    ang = pos * freq
    first = ((j % half) < half // 2)[None, :]
    cos = jnp.cos(ang)
    sin = jnp.sin(ang)
    return cos, jnp.where(first, -sin, 0.0), jnp.where(first, 0.0, sin)


def _layer_params(l, a_qn, a_kn, a_sink, b_qn, b_kn, c_conv_w, c_conv_b, c_wa, c_ba, c_wx, c_bx,
                  c_lambda, d_theta, d_norm_g):
    tile4 = lambda g: jnp.tile(g, 4)[None, :]
    return {
        "a_qn": tile4(a_qn[l]), "a_kn": tile4(a_kn[l]), "b_qn": tile4(b_qn[l]), "b_kn": tile4(b_kn[l]),
        "a_sink": a_sink[l],
        "conv_w": c_conv_w[l], "conv_b": c_conv_b[l][None, :],
        "wa": _block_diag(c_wa[l]).astype(BF16), "ba": c_ba[l],
        "wx": _block_diag(c_wx[l]).astype(BF16), "bx": c_bx[l],
        "lam": c_lambda[l],
        "theta": jnp.repeat(d_theta[l], HEAD_DIM, axis=-1),
        "d_norm": d_norm_g[l][None, :],
    }


def kernel(x_prompt, x_sample, cache_a_k, cache_a_v, cache_b_k, cache_b_v, state_c, state_d, c, c_ctx, norm1_g, norm2_g, norm3_g, w_mod, b_mod, ffn1_wg, ffn1_wu, ffn1_wd, ffn2_wg, ffn2_wu, ffn2_wd, w_in, w_out, a_qn, a_kn, a_sink, b_qn, b_kn, c_conv_w, c_conv_b, c_wa, c_ba, c_wx, c_bx, c_lambda, d_theta, d_norm_g):
    cond = jnp.concatenate([c_ctx[None, :], c, jnp.zeros((MOD_ROWS - 1 - DEC_BATCH, D_MODEL), F32)], axis=0)
    mod = _modulation(cond, w_mod, b_mod)
    x = jnp.concatenate([x_prompt.reshape(N_CTX_ROWS, D_MODEL), x_sample.reshape(N_LAT_ROWS, D_MODEL)], axis=0)
    rope = _rope_tables()
    collected = [[] for _ in range(6)]
    for l in range(DEPTH):
        p = _layer_params(l, a_qn, a_kn, a_sink, b_qn, b_kn, c_conv_w, c_conv_b, c_wa, c_ba, c_wx, c_bx,
                          c_lambda, d_theta, d_norm_g)
        x = _ffn(x, mod, l, 0, norm1_g, ffn1_wg, ffn1_wu, ffn1_wd)
        u = _inproj(x, mod, l, norm2_g, w_in)
        mixed, ka, va, kb, vb, st_c, st_d = _ctx_mixers(u, p)
        cache = (cache_a_k[:, l].reshape(DEC_BATCH, PAST_LEN, 128),
                 cache_a_v[:, l].reshape(DEC_BATCH, PAST_LEN, 128),
                 cache_b_k[:, l].reshape(DEC_BATCH, PAST_LEN, 128),
                 cache_b_v[:, l].reshape(DEC_BATCH, PAST_LEN, 128),
                 state_c[:, l], _block_diag(state_d[:, l]))
        mixed = _lat_mixers(mixed, u, cache, rope, p)
        x = _outproj(x, mixed, mod, l, w_out)
        x = _ffn(x, mod, l, 6, norm3_g, ffn2_wg, ffn2_wu, ffn2_wd)
        for store, t in zip(collected, (ka, va, kb, vb, st_c, st_d)):
            store.append(t)
    y_p = x[:N_CTX_ROWS].reshape(BATCH, SEQ, D_MODEL)
    y_s = x[N_CTX_ROWS:].reshape(DEC_BATCH, DEC_SEQ, D_MODEL)
    kv_shape = (BATCH, DEPTH, SEQ, 2, HEAD_DIM)
    return (y_p, y_s,
            jnp.stack(collected[0], axis=1).reshape(kv_shape),
            jnp.stack(collected[1], axis=1).reshape(kv_shape),
            jnp.stack(collected[2], axis=1).reshape(kv_shape),
            jnp.stack(collected[3], axis=1).reshape(kv_shape),
            jnp.stack(collected[4], axis=1),
            jnp.stack(collected[5], axis=1))
```

```python
import functools
import math

import numpy as np
import jax
import jax.numpy as jnp
from jax import lax
from jax.experimental import pallas as pl
from jax.experimental.pallas import tpu as pltpu

F32 = jnp.float32
BF16 = jnp.bfloat16

D_MODEL = 1024
BATCH = 16
SEQ = 256
DEPTH = 2
DEC_BATCH = 2
DEC_SEQ = 1024
PAST_LEN = 512
GRID_W = 64
HEAD_DIM = 64
HEAD_SHIFT = 6
N_HEADS = 4
GROUP_W = 256
WINDOW = 128
ATT_BLOCK = 128
ROPE_BASE = 10000.0
LRU_C = 8.0
D_FF = 2816
N_MOD = 9
EPS = 1e-6
NEG_INF = -1e30
IN_WIDTH = 2560

N_CTX_ROWS = BATCH * SEQ
N_LAT_ROWS = DEC_BATCH * DEC_SEQ
N_ROWS = N_CTX_ROWS + N_LAT_ROWS
MOD_ROWS = 8
MOD_GROUP = 1024

VMEM_LIMIT_BYTES = 56 * 1024 * 1024

COL_AQ, COL_AK, COL_AV = 0, 256, 384
COL_BQ, COL_BK, COL_BV = 512, 768, 896
COL_CX, COL_CY = 1024, 1280
COL_DQ, COL_DK, COL_DV, COL_DG = 1536, 1792, 2048, 2304


def _cparams(*sem):
    return pltpu.CompilerParams(dimension_semantics=sem, vmem_limit_bytes=VMEM_LIMIT_BYTES)


def _dot(a, b):
    return jnp.dot(a, b, preferred_element_type=F32)


def _dot_nt(a, b):
    return lax.dot_general(a, b, (((1,), (1,)), ((), ())), preferred_element_type=F32)


def _dot_tn(a, b):
    return lax.dot_general(a, b, (((0,), (0,)), ((), ())), preferred_element_type=F32)


def _silu(x):
    return x * jax.nn.sigmoid(x)


def _gelu_tanh(x):
    return 0.5 * x * (1.0 + jnp.tanh(math.sqrt(2.0 / math.pi) * (x + 0.044715 * (x * x * x))))


def _mod_row(i, tm, s):
    if tm >= MOD_GROUP:
        block_index = i * (tm // MOD_GROUP) + s
    else:
        block_index = i >> int(math.log2(MOD_GROUP // tm))
    return jnp.maximum(block_index - (N_CTX_ROWS // MOD_GROUP - 1), 0)


def _norm_mod(x, g, sc, sh):
    ms = jnp.mean(x * x, axis=-1, keepdims=True)
    return (x * lax.rsqrt(ms + EPS) * g) * (1.0 + sc) + sh


def _full(shape):
    return pl.BlockSpec(shape, lambda *_: (0,) * len(shape))


def _layer_block(shape, layer):
    return pl.BlockSpec((None,) + shape, lambda *_: (layer,) + (0,) * len(shape))


MOD_TN = 1024


def _mod_kernel(cc_ref, c_ref, w_ref, b_ref, o_ref):
    l = pl.program_id(0)
    pad = jnp.zeros((MOD_ROWS - 1 - DEC_BATCH, D_MODEL), F32)
    cond = jnp.concatenate([cc_ref[...], c_ref[...], pad], axis=0)
    o_ref[...] = _dot(_silu(cond).astype(BF16), w_ref[...].astype(BF16)) + b_ref[pl.ds(l, 1), :]


def _modulation(c_ctx, c, w_mod, b_mod):
    n = N_MOD * D_MODEL
    return pl.pallas_call(
        _mod_kernel,
        grid=(DEPTH, n // MOD_TN),
        in_specs=[
            pl.BlockSpec((1, D_MODEL), lambda l, j: (0, 0)),
            pl.BlockSpec((DEC_BATCH, D_MODEL), lambda l, j: (0, 0)),
            pl.BlockSpec((None, D_MODEL, MOD_TN), lambda l, j: (l, 0, j)),
            pl.BlockSpec((DEPTH, MOD_TN), lambda l, j: (0, j)),
        ],
        out_specs=pl.BlockSpec((None, MOD_ROWS, MOD_TN), lambda l, j: (l, 0, j)),
        out_shape=jax.ShapeDtypeStruct((DEPTH, MOD_ROWS, n), F32),
        compiler_params=_cparams("arbitrary", "arbitrary"),
        name="modulation",
    )(c_ctx.reshape(1, D_MODEL), c, w_mod, b_mod)


FFN_TM = 1024
FFN_TF = 256
N_CTX_TILES = N_CTX_ROWS // FFN_TM


def _on_stream_part(i, refs, fn):
    if len(refs) == 1:
        fn(refs[0])
    else:
        pl.when(i < N_CTX_TILES)(lambda: fn(refs[0]))
        pl.when(i >= N_CTX_TILES)(lambda: fn(refs[1]))


def _ffn_kernel(*refs, layer, n_in, n_out):
    x_refs = refs[:n_in]
    n_ref, sh_ref, sc_ref, g_ref, wg_ref, wu_ref, wd_ref = refs[n_in:n_in + 7]
    o_refs = refs[n_in + 7:n_in + 7 + n_out]
    h_ref, wg_s, wu_s, wd_s = refs[n_in + 7 + n_out:]
    tm = FFN_TM
    i = pl.program_id(0)
    j = pl.program_id(1)
    r = _mod_row(i, tm, 0)

    @pl.when(j == 0)
    def _():
        def init(x_ref):
            x = x_ref[...]
            h = _norm_mod(x, n_ref[layer:layer + 1, :], sc_ref[pl.ds(r, 1), :], sh_ref[pl.ds(r, 1), :])
            h_ref[...] = h.astype(BF16)
            _on_stream_part(i, o_refs, lambda o_ref: o_ref.__setitem__(Ellipsis, x))
        _on_stream_part(i, x_refs, init)

    @pl.when(i == 0)
    def _():
        wg_s[j] = wg_ref[...].astype(BF16)
        wu_s[j] = wu_ref[...].astype(BF16)
        wd_s[j] = wd_ref[...].astype(BF16)

    h = h_ref[...]
    a = _silu(_dot(h, wg_s[j])) * _dot(h, wu_s[j])
    y = (0.5 * g_ref[pl.ds(r, 1), :]) * _dot(a.astype(BF16), wd_s[j])

    def accumulate(o_ref):
        o_ref[...] += y
    _on_stream_part(i, o_refs, accumulate)


def _stream_specs(split, buffered_once):
    tm = FFN_TM
    kw = {"pipeline_mode": pl.Buffered(1)} if buffered_once else {}
    if not split:
        return [pl.BlockSpec((tm, D_MODEL), lambda i, j: (i, 0))]
    last_ctx = N_CTX_TILES - 1
    return [pl.BlockSpec((tm, D_MODEL), lambda i, j: (jnp.minimum(i, last_ctx), 0), **kw),
            pl.BlockSpec((tm, D_MODEL), lambda i, j: (jnp.maximum(i - N_CTX_TILES, 0), 0), **kw)]


def _ffn(xs, mod, layer, chunk0, norm_g, wg, wu, wd, split_out=False):
    tm, tf = FFN_TM, FFN_TF
    nj = D_FF // tf
    split_in = len(xs) == 2
    mod_spec = lambda c: pl.BlockSpec((None, MOD_ROWS, D_MODEL), lambda i, j: (layer, 0, c))
    w_col = lambda i, j: (layer, 0, jnp.where(i == 0, j, nj - 1))
    w_row = lambda i, j: (layer, jnp.where(i == 0, j, nj - 1), 0)
    if split_out:
        out_shape = [jax.ShapeDtypeStruct((N_CTX_ROWS, D_MODEL), F32),
                     jax.ShapeDtypeStruct((N_LAT_ROWS, D_MODEL), F32)]
    else:
        out_shape = [jax.ShapeDtypeStruct((N_ROWS, D_MODEL), F32)]
    out = pl.pallas_call(
        functools.partial(_ffn_kernel, layer=layer, n_in=len(xs), n_out=len(out_shape)),
        grid=(N_ROWS // tm, nj),
        in_specs=_stream_specs(split_in, True) + [
            _full((DEPTH, D_MODEL)),
            mod_spec(chunk0), mod_spec(chunk0 + 1), mod_spec(chunk0 + 2),
            pl.BlockSpec((None, D_MODEL, tf), w_col),
            pl.BlockSpec((None, D_MODEL, tf), w_col),
            pl.BlockSpec((None, tf, D_MODEL), w_row),
        ],
        out_specs=_stream_specs(split_out, True),
        out_shape=out_shape,
        scratch_shapes=[pltpu.VMEM((tm, D_MODEL), BF16),
                        pltpu.VMEM((nj, D_MODEL, tf), BF16),
                        pltpu.VMEM((nj, D_MODEL, tf), BF16),
                        pltpu.VMEM((nj, tf, D_MODEL), BF16)],
        compiler_params=_cparams("arbitrary", "arbitrary"),
        name="ffn",
    )(*xs, norm_g, mod, mod, mod, wg, wu, wd)
    return tuple(out)


INP_TM = 1024
INP_TN = 512


def _inproj_kernel(x_ref, n_ref, sh_ref, sc_ref, w_ref, o_ref, h_ref, w_s, *, layer):
    tm = INP_TM
    i = pl.program_id(0)
    j = pl.program_id(1)

    @pl.when(j == 0)
    def _():
        r = _mod_row(i, tm, 0)
        h = _norm_mod(x_ref[...], n_ref[layer:layer + 1, :], sc_ref[pl.ds(r, 1), :], sh_ref[pl.ds(r, 1), :])
        h_ref[...] = h.astype(BF16)

    @pl.when(i == 0)
    def _():
        w_s[j] = w_ref[...].astype(BF16)

    o_ref[...] = _dot(h_ref[...], w_s[j])


def _inproj(x, mod, layer, norm_g, w_in):
    tm, tn = INP_TM, INP_TN
    nj = IN_WIDTH // tn
    mod_spec = lambda c: pl.BlockSpec((None, MOD_ROWS, D_MODEL), lambda i, j: (layer, 0, c))
    return pl.pallas_call(
        functools.partial(_inproj_kernel, layer=layer),
        grid=(N_ROWS // tm, nj),
        in_specs=[
            pl.BlockSpec((tm, D_MODEL), lambda i, j: (i, 0)),
            _full((DEPTH, D_MODEL)),
            mod_spec(3), mod_spec(4),
            pl.BlockSpec((None, D_MODEL, tn), lambda i, j: (layer, 0, jnp.where(i == 0, j, nj - 1))),
        ],
        out_specs=pl.BlockSpec((tm, tn), lambda i, j: (i, j)),
        out_shape=jax.ShapeDtypeStruct((N_ROWS, IN_WIDTH), F32),
        scratch_shapes=[pltpu.VMEM((tm, D_MODEL), BF16), pltpu.VMEM((nj, D_MODEL, tn), BF16)],
        compiler_params=_cparams("arbitrary", "arbitrary"),
        name="inproj",
    )(x, norm_g, mod, mod, w_in)


OUTP_TM = 1024


def _outproj_kernel(x_ref, m_ref, g_ref, w_ref, o_ref):
    r = _mod_row(pl.program_id(0), OUTP_TM, 0)
    y = _dot(m_ref[...], w_ref[...].astype(BF16))
    o_ref[...] = x_ref[...] + g_ref[pl.ds(r, 1), :] * y


def _outproj(x, mixed, mod, layer, w_out):
    tm = OUTP_TM
    return pl.pallas_call(
        _outproj_kernel,
        grid=(N_ROWS // tm,),
        in_specs=[
            pl.BlockSpec((tm, D_MODEL), lambda i: (i, 0)),
            pl.BlockSpec((tm, D_MODEL), lambda i: (i, 0)),
            pl.BlockSpec((None, MOD_ROWS, D_MODEL), lambda i: (layer, 0, 5)),
            _layer_block((D_MODEL, D_MODEL), layer),
        ],
        out_specs=pl.BlockSpec((tm, D_MODEL), lambda i: (i, 0)),
        out_shape=jax.ShapeDtypeStruct((N_ROWS, D_MODEL), F32),
        compiler_params=_cparams("arbitrary"),
        name="outproj",
    )(x, mixed, mod, w_out)


def _head_mean_square(x):
    n = x.shape[-1]
    x2 = x * x
    hi = x2.astype(BF16)
    lo = (x2 - hi.astype(F32)).astype(BF16)
    r = lax.broadcasted_iota(jnp.int32, (n, n), 0) >> HEAD_SHIFT
    c = lax.broadcasted_iota(jnp.int32, (n, n), 1) >> HEAD_SHIFT
    ones_bd = jnp.where(r == c, 1.0, 0.0).astype(BF16)
    return (_dot(hi, ones_bd) + _dot(lo, ones_bd)) * (1.0 / HEAD_DIM)


def _head_norm(x, head_gain):
    gain_row = jnp.concatenate([head_gain] * (x.shape[-1] // HEAD_DIM), axis=-1)
    return x * lax.rsqrt(_head_mean_square(x) + EPS) * gain_row


def _head_cols(x, h):
    return x[:, h * HEAD_DIM:(h + 1) * HEAD_DIM].astype(BF16)


def _rope(x, cos, sin_lo, sin_hi):
    cols = []
    for c in range(x.shape[-1] // 128):
        xc = x[:, c * 128:(c + 1) * 128]
        cols.append(xc * cos + pltpu.roll(xc, 112, 1) * sin_lo + pltpu.roll(xc, 16, 1) * sin_hi)
    return cols[0] if len(cols) == 1 else jnp.concatenate(cols, axis=-1)


def _softmax_pv(scores, values, sink):
    m = jnp.max(scores[0], axis=-1, keepdims=True)
    for s in scores[1:]:
        m = jnp.maximum(m, jnp.max(s, axis=-1, keepdims=True))
    if sink is not None:
        m = jnp.maximum(m, sink)
    denom = None
    acc = None
    for s, v in zip(scores, values):
        p = jnp.exp(s - m)
        d = jnp.sum(p, axis=-1, keepdims=True)
        o = _dot(p.astype(BF16), v)
        denom = d if denom is None else denom + d
        acc = o if acc is None else acc + o
    if sink is not None:
        denom = denom + jnp.exp(sink - m)
    return acc / denom


def _block_diag(blocks):
    n = len(blocks)
    w = blocks[0].shape[0]
    rows = []
    for k, blk in enumerate(blocks):
        parts = []
        if k > 0:
            parts.append(jnp.zeros((w, k * w), F32))
        parts.append(blk)
        if k < n - 1:
            parts.append(jnp.zeros((w, (n - 1 - k) * w), F32))
        rows.append(jnp.concatenate(parts, axis=-1))
    return jnp.concatenate(rows, axis=0).astype(BF16)


def _rglru_gates(xc, wa, ba, wx, bx, lam):
    xb = xc.astype(BF16)
    r = jax.nn.sigmoid(_dot(xb, wa) + ba)
    i = jax.nn.sigmoid(_dot(xb, wx) + bx)
    softplus = jnp.maximum(-lam, 0.0) + jnp.log1p(jnp.exp(-jnp.abs(lam)))
    log_a = (-LRU_C) * r * softplus
    a = jnp.exp(log_a)
    b = jnp.sqrt(-jnp.tanh(log_a) * (a * a + 1.0)) * (i * xc)
    return a, b


def _block_prefix(a, b, reverse):
    t = a.shape[0]
    row = lax.broadcasted_iota(jnp.int32, a.shape, 0) & 7
    for d in (1, 2, 4):
        if reverse:
            a_s = pltpu.roll(a, t - d, 0)
            b_s = pltpu.roll(b, t - d, 0)
            ok = row < 8 - d
        else:
            a_s = pltpu.roll(a, d, 0)
            b_s = pltpu.roll(b, d, 0)
            ok = row >= d
        b = jnp.where(ok, a * b_s + b, b)
        a = jnp.where(ok, a * a_s, a)
    return a, b


def _conv4(x, w_ref, b_row):
    t = x.shape[0]
    row = lax.broadcasted_iota(jnp.int32, x.shape, 0)
    xm2 = jnp.where(row >= 2, pltpu.roll(x, 2, 0), 0.0)
    xm1 = jnp.where(row >= 1, pltpu.roll(x, 1, 0), 0.0)
    xp1 = jnp.where(row < t - 1, pltpu.roll(x, t - 1, 0), 0.0)
    return (xm2 * w_ref[0:1, :] + xm1 * w_ref[1:2, :] + x * w_ref[2:3, :] + xp1 * w_ref[3:4, :]) + b_row


def _rglru_mixer(cx, cy, conv_w_ref, conv_b, gate_w_ref, ba_ref, bx_ref, lam_ref, h0f, h0b,
                 af_ref, bf_ref, ab_ref, bb_ref, hf_ref, hb_ref):
    t = cx.shape[0]
    xc = _conv4(cx, conv_w_ref, conv_b)
    a, b = _rglru_gates(xc, gate_w_ref[0], ba_ref[0:1, :], gate_w_ref[1], bx_ref[0:1, :], lam_ref[0:1, :])
    a, b = _block_prefix(a, b, reverse=False)
    af_ref[...] = a
    bf_ref[...] = b
    a, b = _rglru_gates(xc, gate_w_ref[2], ba_ref[1:2, :], gate_w_ref[3], bx_ref[1:2, :], lam_ref[1:2, :])
    a, b = _block_prefix(a, b, reverse=True)
    ab_ref[...] = a
    bb_ref[...] = b
    nblk = t // 8

    def body(k, carry):
        cf, cb = carry
        rf = pl.ds(pl.multiple_of(k * 8, 8), 8)
        hf = bf_ref[rf, :] + af_ref[rf, :] * cf
        hf_ref[rf, :] = hf
        rb = pl.ds(pl.multiple_of((nblk - 1 - k) * 8, 8), 8)
        hb = bb_ref[rb, :] + ab_ref[rb, :] * cb
        hb_ref[rb, :] = hb
        return hf[7:8, :], hb[0:1, :]

    cf, cb = lax.fori_loop(0, nblk, body, (h0f, h0b))
    oc = (hf_ref[...] + hb_ref[...]) * _gelu_tanh(cy)
    return oc, cf, cb


def _store_gate_weights(gate_w_ref, wa_ref, wx_ref):
    for d in range(2):
        gate_w_ref[2 * d] = _block_diag([wa_ref[d, n] for n in range(N_HEADS)])
        gate_w_ref[2 * d + 1] = _block_diag([wx_ref[d, n] for n in range(N_HEADS)])


def _lane_head_masks(n):
    lane = lax.broadcasted_iota(jnp.int32, (1, n), 1) >> HEAD_SHIFT
    return [jnp.where(lane == h, 1.0, 0.0) for h in range(n // HEAD_DIM)]


def _log_decays(theta_ref, masks):
    theta = theta_ref[...]
    lanes = theta[:, 0:1] * masks[0]
    for h in range(1, N_HEADS):
        lanes = lanes + theta[:, h:h + 1] * masks[h]
    lg = jnp.log1p(-jnp.exp(lanes))
    return lg[0:1, :], lg[1:2, :]


def _retention_block(q, k8b, vb, q0, lgf, lgb, masks):
    tq, w = q.shape
    s_len = k8b.shape[0]
    nh = w // HEAD_DIM
    q_stack = jnp.concatenate([(q * masks[h]).astype(BF16) for h in range(nh)], axis=0)
    raw = _dot_nt(q_stack, k8b)
    qi = lax.broadcasted_iota(jnp.int32, (tq, s_len), 0) + q0
    si = lax.broadcasted_iota(jnp.int32, (tq, s_len), 1)
    rel = (qi - si).astype(F32)
    decs = []
    for h in range(nh):
        gf = lgf[:, h * HEAD_DIM:h * HEAD_DIM + 1]
        gb = lgb[:, h * HEAD_DIM:h * HEAD_DIM + 1]
        e = jnp.exp(jnp.where(rel >= 0, gf * rel, gb * (-rel)))
        decs.append(jnp.where(rel == 0, 2.0, e))
    inner = (raw * jnp.concatenate(decs, axis=0)).astype(BF16)
    out = _dot(inner, vb)
    o = out[0:tq, :] * masks[0]
    for h in range(1, nh):
        o = o + out[h * tq:(h + 1) * tq, :] * masks[h]
    return o


def _ctx_mixer_kernel(*refs, layer, n_alias):
    refs = refs[n_alias:]
    (u_ref, aqn_ref, akn_ref, bqn_ref, bkn_ref, sink_ref,
     convw_ref, convb_ref, wa_ref, ba_ref, wx_ref, bx_ref, lam_ref, theta_ref, dn_ref,
     mixed_ref, ka_ref, va_ref, kb_ref, vb_ref, stc_ref, std_ref,
     gate_w_ref, af_ref, bf_ref, ab_ref, bb_ref, hf_ref, hb_ref) = refs
    t = SEQ
    lrow = slice(layer, layer + 1)

    @pl.when(pl.program_id(0) == 0)
    def _():
        _store_gate_weights(gate_w_ref, wa_ref, wx_ref)

    for (cq, ck, cv, qn_ref, kn_ref, k_out, v_out, col0, use_sink) in (
            (COL_AQ, COL_AK, COL_AV, aqn_ref, akn_ref, ka_ref, va_ref, 0, True),
            (COL_BQ, COL_BK, COL_BV, bqn_ref, bkn_ref, kb_ref, vb_ref, GROUP_W, False)):
        q = _head_norm(u_ref[:, cq:cq + 256], qn_ref[lrow, :])
        k = _head_norm(u_ref[:, ck:ck + 128], kn_ref[lrow, :])
        v = u_ref[:, cv:cv + 128]
        k_out[...] = k
        v_out[...] = v
        qs = q * (HEAD_DIM ** -0.5)
        heads = []
        for h in range(N_HEADS):
            kv = h // 2
            s = _dot_nt(_head_cols(qs, h), _head_cols(k, kv))
            sink = jnp.full((t, 1), sink_ref[layer, h], F32) if use_sink else None
            heads.append(_softmax_pv([s], [_head_cols(v, kv)], sink))
        mixed_ref[:, col0:col0 + GROUP_W] = jnp.concatenate(heads, axis=-1).astype(BF16)

    zero = jnp.zeros((1, GROUP_W), F32)
    oc, cf, cb = _rglru_mixer(u_ref[:, COL_CX:COL_CX + 256], u_ref[:, COL_CY:COL_CY + 256],
                              convw_ref, convb_ref[lrow, :], gate_w_ref, ba_ref, bx_ref, lam_ref,
                              zero, zero, af_ref, bf_ref, ab_ref, bb_ref, hf_ref, hb_ref)
    mixed_ref[:, 2 * GROUP_W:3 * GROUP_W] = oc.astype(BF16)
    stc_ref[0:1, :] = cf
    stc_ref[1:2, :] = cb

    masks = _lane_head_masks(GROUP_W)
    lgf, lgb = _log_decays(theta_ref, masks)
    q = u_ref[:, COL_DQ:COL_DQ + 256]
    k8 = u_ref[:, COL_DK:COL_DK + 256] * (HEAD_DIM ** -0.5)
    vb = u_ref[:, COL_DV:COL_DV + 256].astype(BF16)
    o = _retention_block(q, k8.astype(BF16), vb, 0, lgf, lgb, masks)
    dn = dn_ref[lrow, :]
    o = o * lax.rsqrt(_head_mean_square(o) + EPS) * dn * _silu(u_ref[:, COL_DG:COL_DG + 256])
    mixed_ref[:, 3 * GROUP_W:4 * GROUP_W] = o.astype(BF16)
    pos = lax.broadcasted_iota(jnp.int32, (t, GROUP_W), 0).astype(F32)
    for d, (lg, expo) in enumerate(((lgf, (t - 1.0) - pos), (lgb, pos))):
        s_full = _dot_tn((k8 * jnp.exp(lg * expo)).astype(BF16), vb)
        for h in range(N_HEADS):
            std_ref[d, h] = s_full[h * 64:(h + 1) * 64, h * 64:(h + 1) * 64]


def _ctx_mixers(u, layer, prev, a_qn, a_kn, a_sink, b_qn, b_kn, c_conv_w, c_conv_b, c_wa, c_ba, c_wx, c_bx,
                c_lambda, d_theta, d_norm_g):
    kv_out = pl.BlockSpec((None, None, SEQ, 128), lambda b: (b, layer, 0, 0))
    kv_shape = jax.ShapeDtypeStruct((BATCH, DEPTH, SEQ, 128), F32)
    scr = pltpu.VMEM((SEQ, GROUP_W), F32)
    n_alias = len(prev)
    out = pl.pallas_call(
        functools.partial(_ctx_mixer_kernel, layer=layer, n_alias=n_alias),
        grid=(BATCH,),
        in_specs=[pl.BlockSpec(memory_space=pl.ANY)] * n_alias + [
            pl.BlockSpec((SEQ, IN_WIDTH), lambda b: (b, 0)),
            _full((DEPTH, HEAD_DIM)), _full((DEPTH, HEAD_DIM)), _full((DEPTH, HEAD_DIM)), _full((DEPTH, HEAD_DIM)),
            pl.BlockSpec(memory_space=pltpu.SMEM),
            _layer_block((4, GROUP_W), layer), _full((DEPTH, GROUP_W)),
            _layer_block((2, N_HEADS, HEAD_DIM, HEAD_DIM), layer), _layer_block((2, GROUP_W), layer),
            _layer_block((2, N_HEADS, HEAD_DIM, HEAD_DIM), layer), _layer_block((2, GROUP_W), layer),
            _layer_block((2, GROUP_W), layer),
            _layer_block((2, N_HEADS), layer), _full((DEPTH, GROUP_W)),
        ],
        out_specs=[
            pl.BlockSpec((SEQ, D_MODEL), lambda b: (b, 0)),
            kv_out, kv_out, kv_out, kv_out,
            pl.BlockSpec((None, None, 2, GROUP_W), lambda b: (b, layer, 0, 0)),
            pl.BlockSpec((None, None, 2, N_HEADS, HEAD_DIM, HEAD_DIM), lambda b: (b, layer, 0, 0, 0, 0)),
        ],
        out_shape=[
            jax.ShapeDtypeStruct((N_ROWS, D_MODEL), BF16),
            kv_shape, kv_shape, kv_shape, kv_shape,
            jax.ShapeDtypeStruct((BATCH, DEPTH, 2, GROUP_W), F32),
            jax.ShapeDtypeStruct((BATCH, DEPTH, 2, N_HEADS, HEAD_DIM, HEAD_DIM), F32),
        ],
        input_output_aliases={k: k + 1 for k in range(n_alias)},
        scratch_shapes=[pltpu.VMEM((4, GROUP_W, GROUP_W), BF16)] + [scr] * 6,
        compiler_params=_cparams("arbitrary"),
        name="ctx_mixers",
    )(*prev, u, a_qn, a_kn, b_qn, b_kn, a_sink, c_conv_w, c_conv_b, c_wa, c_ba, c_wx, c_bx,
      c_lambda, d_theta, d_norm_g)
    return out[0], tuple(out[1:])


LAT_BLOCK0 = N_CTX_ROWS // DEC_SEQ


def _lat_attn_kernel(mixed_in_ref, u_ref, kca_ref, vca_ref, kcb_ref, vcb_ref,
                     aqn_ref, akn_ref, bqn_ref, bkn_ref, sink_ref, cos_ref, sinl_ref, sinh_ref,
                     o_ref, *, layer):
    del mixed_in_ref
    t = DEC_SEQ
    lrow = slice(layer, layer + 1)
    cos, sin_lo, sin_hi = cos_ref[...], sinl_ref[...], sinh_ref[...]
    scale = HEAD_DIM ** -0.5

    q = _rope(_head_norm(u_ref[:, COL_AQ:COL_AQ + 256], aqn_ref[lrow, :]), cos, sin_lo, sin_hi)
    k = _rope(_head_norm(u_ref[:, COL_AK:COL_AK + 128], akn_ref[lrow, :]), cos, sin_lo, sin_hi)
    qh = [_head_cols(q * scale, h) for h in range(4)]
    v = u_ref[:, COL_AV:COL_AV + 128]
    kh = [_head_cols(k, kv) for kv in range(2)]
    vh = [_head_cols(v, kv) for kv in range(2)]
    kch = [_head_cols(kca_ref[...], kv) for kv in range(2)]
    vch = [_head_cols(vca_ref[...], kv) for kv in range(2)]
    w = ATT_BLOCK
    span = 3 * w
    for n in range(t // w):
        start = min(max((n - 1) * w, 0), t - span)
        rows = slice(n * w, (n + 1) * w)
        band = slice(start, start + span)
        qpos = (lax.broadcasted_iota(jnp.int32, (2 * w, span), 0) & (w - 1)) + n * w
        kpos = lax.broadcasted_iota(jnp.int32, (2 * w, span), 1) + start
        valid = jnp.abs(qpos - kpos) <= WINDOW
        heads = []
        for kv in range(2):
            qp = jnp.concatenate([qh[2 * kv][rows, :], qh[2 * kv + 1][rows, :]], axis=0)
            s_ctx = _dot_nt(qp, kch[kv])
            s_band = jnp.where(valid, _dot_nt(qp, kh[kv][band, :]), NEG_INF)
            row = lax.broadcasted_iota(jnp.int32, (2 * w, 1), 0)
            sink = jnp.where(row < w, sink_ref[layer, 2 * kv], sink_ref[layer, 2 * kv + 1])
            o = _softmax_pv([s_ctx, s_band], [vch[kv], vh[kv][band, :]], sink)
            heads += [o[0:w, :], o[w:2 * w, :]]
        o_ref[rows, 0:GROUP_W] = jnp.concatenate(heads, axis=-1).astype(BF16)

    q = _rope(_head_norm(u_ref[:, COL_BQ:COL_BQ + 256], bqn_ref[lrow, :]), cos, sin_lo, sin_hi)
    k = _rope(_head_norm(u_ref[:, COL_BK:COL_BK + 128], bkn_ref[lrow, :]), cos, sin_lo, sin_hi)
    qh = [_head_cols(q * scale, h) for h in range(4)]
    v = u_ref[:, COL_BV:COL_BV + 128]
    kh = [_head_cols(k, kv) for kv in range(2)]
    vh = [_head_cols(v, kv) for kv in range(2)]
    kch = [_head_cols(kcb_ref[...], kv) for kv in range(2)]
    vch = [_head_cols(vcb_ref[...], kv) for kv in range(2)]
    tq = 256
    for n in range(t // tq):
        rows = slice(n * tq, (n + 1) * tq)
        heads = []
        for kv in range(2):
            qp = jnp.concatenate([qh[2 * kv][rows, :], qh[2 * kv + 1][rows, :]], axis=0)
            o = _softmax_pv([_dot_nt(qp, kch[kv]), _dot_nt(qp, kh[kv])], [vch[kv], vh[kv]], None)
            heads += [o[0:tq, :], o[tq:2 * tq, :]]
        o_ref[rows, GROUP_W:2 * GROUP_W] = jnp.concatenate(heads, axis=-1).astype(BF16)


def _lat_rglru_kernel(mixed_in_ref, cx_ref, cy_ref, h0_ref,
                      convw_ref, convb_ref, wa_ref, ba_ref, wx_ref, bx_ref, lam_ref,
                      o_ref, gate_w_ref, af_ref, bf_ref, ab_ref, bb_ref, hf_ref, hb_ref, *, layer):
    del mixed_in_ref

    @pl.when(pl.program_id(0) == 0)
    def _():
        _store_gate_weights(gate_w_ref, wa_ref, wx_ref)

    oc, _, _ = _rglru_mixer(cx_ref[...], cy_ref[...], convw_ref, convb_ref[layer:layer + 1, :],
                            gate_w_ref, ba_ref, bx_ref, lam_ref,
                            h0_ref[0:1, :], h0_ref[1:2, :],
                            af_ref, bf_ref, ab_ref, bb_ref, hf_ref, hb_ref)
    o_ref[...] = oc.astype(BF16)


def _lat_retention_kernel(mixed_in_ref, q_ref, k_ref, v_ref, g_ref, s0_ref, theta_ref, dn_ref, o_ref,
                          *, layer):
    del mixed_in_ref
    t = DEC_SEQ
    masks = _lane_head_masks(GROUP_W)
    lgf, lgb = _log_decays(theta_ref, masks)
    k8b = (k_ref[...] * (HEAD_DIM ** -0.5)).astype(BF16)
    vb = v_ref[...].astype(BF16)
    s0f = _block_diag([s0_ref[0, h] for h in range(N_HEADS)])
    s0b = _block_diag([s0_ref[1, h] for h in range(N_HEADS)])
    dn = dn_ref[layer:layer + 1, :]
    tq = 256
    for n in range(t // tq):
        rows = slice(n * tq, (n + 1) * tq)
        q = q_ref[rows, :]
        o = _retention_block(q, k8b, vb, n * tq, lgf, lgb, masks)
        pos = lax.broadcasted_iota(jnp.int32, (tq, GROUP_W), 0).astype(F32) + float(n * tq)
        o = o + _dot((q * jnp.exp(lgf * (pos + 1.0))).astype(BF16), s0f)
        o = o + _dot((q * jnp.exp(lgb * (float(t) - pos))).astype(BF16), s0b)
        o = o * lax.rsqrt(_head_mean_square(o) + EPS) * dn * _silu(g_ref[rows, :])
        o_ref[rows, :] = o.astype(BF16)


def _lat_mixers(mixed, u, layer, caches, state_c, state_d, rope,
                a_qn, a_kn, a_sink, b_qn, b_kn, c_conv_w, c_conv_b, c_wa, c_ba, c_wx, c_bx,
                c_lambda, d_theta, d_norm_g):
    any_spec = pl.BlockSpec(memory_space=pl.ANY)
    ucols = lambda w, c: pl.BlockSpec((DEC_SEQ, w), lambda b: (LAT_BLOCK0 + b, c))
    cache_spec = pl.BlockSpec((None, None, PAST_LEN, 128), lambda b: (b, layer, 0, 0))
    gain = _full((DEPTH, HEAD_DIM))
    out_shape = jax.ShapeDtypeStruct((N_ROWS, D_MODEL), BF16)

    mixed = pl.pallas_call(
        functools.partial(_lat_attn_kernel, layer=layer),
        grid=(DEC_BATCH,),
        in_specs=[any_spec, ucols(1024, 0), cache_spec, cache_spec, cache_spec, cache_spec,
                  gain, gain, gain, gain,
                  pl.BlockSpec(memory_space=pltpu.SMEM),
                  _full((DEC_SEQ, 128)), _full((DEC_SEQ, 128)), _full((DEC_SEQ, 128))],
        out_specs=ucols(2 * GROUP_W, 0),
        out_shape=out_shape,
        input_output_aliases={0: 0},
        compiler_params=_cparams("arbitrary"),
        name="lat_attention",
    )(mixed, u, *caches, a_qn, a_kn, b_qn, b_kn, a_sink, *rope)

    scr = pltpu.VMEM((DEC_SEQ, GROUP_W), F32)
    mixed = pl.pallas_call(
        functools.partial(_lat_rglru_kernel, layer=layer),
        grid=(DEC_BATCH,),
        in_specs=[any_spec, ucols(GROUP_W, COL_CX // GROUP_W), ucols(GROUP_W, COL_CY // GROUP_W),
                  pl.BlockSpec((None, None, 2, GROUP_W), lambda b: (b, layer, 0, 0)),
                  _layer_block((4, GROUP_W), layer), _full((DEPTH, GROUP_W)),
                  _layer_block((2, N_HEADS, HEAD_DIM, HEAD_DIM), layer), _layer_block((2, GROUP_W), layer),
                  _layer_block((2, N_HEADS, HEAD_DIM, HEAD_DIM), layer), _layer_block((2, GROUP_W), layer),
                  _layer_block((2, GROUP_W), layer)],
        out_specs=ucols(GROUP_W, 2),
        out_shape=out_shape,
        input_output_aliases={0: 0},
        scratch_shapes=[pltpu.VMEM((4, GROUP_W, GROUP_W), BF16)] + [scr] * 6,
        compiler_params=_cparams("arbitrary"),
        name="lat_rglru",
    )(mixed, u, u, state_c, c_conv_w, c_conv_b, c_wa, c_ba, c_wx, c_bx, c_lambda)

    mixed = pl.pallas_call(
        functools.partial(_lat_retention_kernel, layer=layer),
        grid=(DEC_BATCH,),
        in_specs=[any_spec,
                  ucols(GROUP_W, COL_DQ // GROUP_W), ucols(GROUP_W, COL_DK // GROUP_W),
                  ucols(GROUP_W, COL_DV // GROUP_W), ucols(GROUP_W, COL_DG // GROUP_W),
                  pl.BlockSpec((None, None, 2, N_HEADS, HEAD_DIM, HEAD_DIM), lambda b: (b, layer, 0, 0, 0, 0)),
                  _layer_block((2, N_HEADS), layer), _full((DEPTH, GROUP_W))],
        out_specs=ucols(GROUP_W, 3),
        out_shape=out_shape,
        input_output_aliases={0: 0},
        compiler_params=_cparams("arbitrary"),
        name="lat_retention",
    )(mixed, u, u, u, u, state_d, d_theta, d_norm_g)
    return mixed


def _rope_tables():
    t = np.arange(DEC_SEQ)
    row = (t // GRID_W).astype(np.float64)[:, None]
    col = (t % GRID_W).astype(np.float64)[:, None]
    half = HEAD_DIM // 2
    inv = 1.0 / (ROPE_BASE ** (np.arange(0, half, 2, dtype=np.float64) / half))
    j = np.arange(128) % HEAD_DIM
    ang = np.where((j < half)[None, :], row, col) * inv[j % (half // 2)][None, :]
    first = ((j % half) < half // 2)[None, :]
    cos, sin = np.cos(ang), np.sin(ang)
    return tuple(jnp.asarray(a, F32) for a in (cos, np.where(first, -sin, 0.0), np.where(first, 0.0, sin)))


def kernel(x_prompt, x_sample, cache_a_k, cache_a_v, cache_b_k, cache_b_v, state_c, state_d, c, c_ctx, norm1_g, norm2_g, norm3_g, w_mod, b_mod, ffn1_wg, ffn1_wu, ffn1_wd, ffn2_wg, ffn2_wu, ffn2_wd, w_in, w_out, a_qn, a_kn, a_sink, b_qn, b_kn, c_conv_w, c_conv_b, c_wa, c_ba, c_wx, c_bx, c_lambda, d_theta, d_norm_g):
    mod = _modulation(c_ctx, c, w_mod, b_mod)
    rope = _rope_tables()
    caches = tuple(t.reshape(DEC_BATCH, DEPTH, PAST_LEN, 128) for t in (cache_a_k, cache_a_v, cache_b_k, cache_b_v))
    mixer_params = (a_qn, a_kn, a_sink, b_qn, b_kn, c_conv_w, c_conv_b, c_wa, c_ba, c_wx, c_bx,
                    c_lambda, d_theta, d_norm_g)
    xs = (x_prompt.reshape(N_CTX_ROWS, D_MODEL), x_sample.reshape(N_LAT_ROWS, D_MODEL))
    states = ()
    for l in range(DEPTH):
        (x,) = _ffn(xs, mod, l, 0, norm1_g, ffn1_wg, ffn1_wu, ffn1_wd)
        u = _inproj(x, mod, l, norm2_g, w_in)
        mixed, states = _ctx_mixers(u, l, states, *mixer_params)
        mixed = _lat_mixers(mixed, u, l, caches, state_c, state_d, rope, *mixer_params)
        x = _outproj(x, mixed, mod, l, w_out)
        xs = _ffn((x,), mod, l, 6, norm3_g, ffn2_wg, ffn2_wu, ffn2_wd, split_out=(l == DEPTH - 1))
    y_p, y_s = xs
    ka, va, kb, vb, st_c, st_d = states
    kv_shape = (BATCH, DEPTH, SEQ, 2, HEAD_DIM)
    return (y_p.reshape(BATCH, SEQ, D_MODEL), y_s.reshape(DEC_BATCH, DEC_SEQ, D_MODEL),
            ka.reshape(kv_shape), va.reshape(kv_shape), kb.reshape(kv_shape), vb.reshape(kv_shape),
            st_c, st_d)
```

```python
import functools
import math

import numpy as np
import jax
import jax.numpy as jnp
from jax import lax
from jax.experimental import pallas as pl
from jax.experimental.pallas import tpu as pltpu

F32 = jnp.float32
BF16 = jnp.bfloat16

D_MODEL = 1024
BATCH = 16
SEQ = 256
DEPTH = 2
DEC_BATCH = 2
DEC_SEQ = 1024
PAST_LEN = 512
GRID_W = 64
HEAD_DIM = 64
HEAD_SHIFT = 6
N_HEADS = 4
GROUP_W = 256
WINDOW = 128
ATT_BLOCK = 128
ROPE_BASE = 10000.0
LRU_C = 8.0
D_FF = 2816
N_MOD = 9
EPS = 1e-6
NEG_INF = -1e30
IN_WIDTH = 2560

N_CTX_ROWS = BATCH * SEQ
N_LAT_ROWS = DEC_BATCH * DEC_SEQ
N_ROWS = N_CTX_ROWS + N_LAT_ROWS
MOD_ROWS = 8
MOD_GROUP = 1024

VMEM_LIMIT_BYTES = 56 * 1024 * 1024

COL_AQ, COL_AK, COL_AV = 0, 256, 384
COL_BQ, COL_BK, COL_BV = 512, 768, 896
COL_CX, COL_CY = 1024, 1280
COL_DQ, COL_DK, COL_DV, COL_DG = 1536, 1792, 2048, 2304


def _cparams(*sem):
    return pltpu.CompilerParams(dimension_semantics=sem, vmem_limit_bytes=VMEM_LIMIT_BYTES)


def _dot(a, b):
    return jnp.dot(a, b, preferred_element_type=F32)


def _dot_nt(a, b):
    return lax.dot_general(a, b, (((1,), (1,)), ((), ())), preferred_element_type=F32)


def _dot_tn(a, b):
    return lax.dot_general(a, b, (((0,), (0,)), ((), ())), preferred_element_type=F32)


def _silu(x):
    return x * jax.nn.sigmoid(x)


def _gelu_tanh(x):
    return 0.5 * x * (1.0 + jnp.tanh(math.sqrt(2.0 / math.pi) * (x + 0.044715 * (x * x * x))))


def _mod_row(i, tm, s):
    if tm >= MOD_GROUP:
        block_index = i * (tm // MOD_GROUP) + s
    else:
        block_index = i >> int(math.log2(MOD_GROUP // tm))
    return jnp.maximum(block_index - (N_CTX_ROWS // MOD_GROUP - 1), 0)


def _norm_mod(x, g, sc, sh):
    ms = jnp.mean(x * x, axis=-1, keepdims=True)
    return (x * lax.rsqrt(ms + EPS) * g) * (1.0 + sc) + sh


def _full(shape):
    return pl.BlockSpec(shape, lambda *_: (0,) * len(shape))


def _layer_block(shape, layer):
    return pl.BlockSpec((None,) + shape, lambda *_: (layer,) + (0,) * len(shape))


MOD_TN = 1024


def _mod_kernel(cc_ref, c_ref, w_ref, b_ref, o_ref):
    l = pl.program_id(0)
    pad = jnp.zeros((MOD_ROWS - 1 - DEC_BATCH, D_MODEL), F32)
    cond = jnp.concatenate([cc_ref[...], c_ref[...], pad], axis=0)
    o_ref[...] = _dot(_silu(cond).astype(BF16), w_ref[...].astype(BF16)) + b_ref[pl.ds(l, 1), :]


def _modulation(c_ctx, c, w_mod, b_mod):
    n = N_MOD * D_MODEL
    return pl.pallas_call(
        _mod_kernel,
        grid=(DEPTH, n // MOD_TN),
        in_specs=[
            pl.BlockSpec((1, D_MODEL), lambda l, j: (0, 0)),
            pl.BlockSpec((DEC_BATCH, D_MODEL), lambda l, j: (0, 0)),
            pl.BlockSpec((None, D_MODEL, MOD_TN), lambda l, j: (l, 0, j)),
            pl.BlockSpec((DEPTH, MOD_TN), lambda l, j: (0, j)),
        ],
        out_specs=pl.BlockSpec((None, MOD_ROWS, MOD_TN), lambda l, j: (l, 0, j)),
        out_shape=jax.ShapeDtypeStruct((DEPTH, MOD_ROWS, n), F32),
        compiler_params=_cparams("arbitrary", "arbitrary"),
        name="modulation",
    )(c_ctx.reshape(1, D_MODEL), c, w_mod, b_mod)


FFN_TM = 1024
FFN_TF = 256
N_CTX_TILES = N_CTX_ROWS // FFN_TM


def _on_stream_part(i, refs, fn):
    if len(refs) == 1:
        fn(refs[0])
    else:
        pl.when(i < N_CTX_TILES)(lambda: fn(refs[0]))
        pl.when(i >= N_CTX_TILES)(lambda: fn(refs[1]))


def _ffn_kernel(*refs, layer, n_in, n_out):
    x_refs = refs[:n_in]
    n_ref, sh_ref, sc_ref, g_ref, wg_ref, wu_ref, wd_ref = refs[n_in:n_in + 7]
    o_refs = refs[n_in + 7:n_in + 7 + n_out]
    h_ref, wg_s, wu_s, wd_s = refs[n_in + 7 + n_out:]
    tm = FFN_TM
    i = pl.program_id(0)
    j = pl.program_id(1)
    r = _mod_row(i, tm, 0)

    @pl.when(j == 0)
    def _():
        def init(x_ref):
            x = x_ref[...]
            h = _norm_mod(x, n_ref[layer:layer + 1, :], sc_ref[pl.ds(r, 1), :], sh_ref[pl.ds(r, 1), :])
            h_ref[...] = h.astype(BF16)
            _on_stream_part(i, o_refs, lambda o_ref: o_ref.__setitem__(Ellipsis, x))
        _on_stream_part(i, x_refs, init)

    @pl.when(i == 0)
    def _():
        wg_s[j] = wg_ref[...].astype(BF16)
        wu_s[j] = wu_ref[...].astype(BF16)
        wd_s[j] = wd_ref[...].astype(BF16)

    h = h_ref[...]
    a = _silu(_dot(h, wg_s[j])) * _dot(h, wu_s[j])
    y = (0.5 * g_ref[pl.ds(r, 1), :]) * _dot(a.astype(BF16), wd_s[j])

    def accumulate(o_ref):
        o_ref[...] += y
    _on_stream_part(i, o_refs, accumulate)


def _stream_specs(split, buffered_once):
    tm = FFN_TM
    kw = {"pipeline_mode": pl.Buffered(1)} if buffered_once else {}
    if not split:
        return [pl.BlockSpec((tm, D_MODEL), lambda i, j: (i, 0))]
    last_ctx = N_CTX_TILES - 1
    return [pl.BlockSpec((tm, D_MODEL), lambda i, j: (jnp.minimum(i, last_ctx), 0), **kw),
            pl.BlockSpec((tm, D_MODEL), lambda i, j: (jnp.maximum(i - N_CTX_TILES, 0), 0), **kw)]


def _ffn(xs, mod, layer, chunk0, norm_g, wg, wu, wd, split_out=False):
    tm, tf = FFN_TM, FFN_TF
    nj = D_FF // tf
    split_in = len(xs) == 2
    mod_spec = lambda c: pl.BlockSpec((None, MOD_ROWS, D_MODEL), lambda i, j: (layer, 0, c))
    w_col = lambda i, j: (layer, 0, jnp.where(i == 0, j, nj - 1))
    w_row = lambda i, j: (layer, jnp.where(i == 0, j, nj - 1), 0)
    if split_out:
        out_shape = [jax.ShapeDtypeStruct((N_CTX_ROWS, D_MODEL), F32),
                     jax.ShapeDtypeStruct((N_LAT_ROWS, D_MODEL), F32)]
    else:
        out_shape = [jax.ShapeDtypeStruct((N_ROWS, D_MODEL), F32)]
    out = pl.pallas_call(
        functools.partial(_ffn_kernel, layer=layer, n_in=len(xs), n_out=len(out_shape)),
        grid=(N_ROWS // tm, nj),
        in_specs=_stream_specs(split_in, True) + [
            _full((DEPTH, D_MODEL)),
            mod_spec(chunk0), mod_spec(chunk0 + 1), mod_spec(chunk0 + 2),
            pl.BlockSpec((None, D_MODEL, tf), w_col),
            pl.BlockSpec((None, D_MODEL, tf), w_col),
            pl.BlockSpec((None, tf, D_MODEL), w_row),
        ],
        out_specs=_stream_specs(split_out, True),
        out_shape=out_shape,
        scratch_shapes=[pltpu.VMEM((tm, D_MODEL), BF16),
                        pltpu.VMEM((nj, D_MODEL, tf), BF16),
                        pltpu.VMEM((nj, D_MODEL, tf), BF16),
                        pltpu.VMEM((nj, tf, D_MODEL), BF16)],
        compiler_params=_cparams("arbitrary", "arbitrary"),
        name="ffn",
    )(*xs, norm_g, mod, mod, mod, wg, wu, wd)
    return tuple(out)


def _once(shape, index_map):
    return pl.BlockSpec(shape, index_map, pipeline_mode=pl.Buffered(1))


def _mod_chunk(layer, c):
    return pl.BlockSpec((None, MOD_ROWS, D_MODEL), lambda *_: (layer, 0, c))


def _project_in(x, n_row, sc_row, sh_row, w_bf16):
    return _dot(_norm_mod(x, n_row, sc_row, sh_row).astype(BF16), w_bf16)

def _head_mean_square(x):
    n = x.shape[-1]
    x2 = x * x
    hi = x2.astype(BF16)
    lo = (x2 - hi.astype(F32)).astype(BF16)
    r = lax.broadcasted_iota(jnp.int32, (n, n), 0) >> HEAD_SHIFT
    c = lax.broadcasted_iota(jnp.int32, (n, n), 1) >> HEAD_SHIFT
    ones_bd = jnp.where(r == c, 1.0, 0.0).astype(BF16)
    return (_dot(hi, ones_bd) + _dot(lo, ones_bd)) * (1.0 / HEAD_DIM)


def _head_norm(x, head_gain):
    gain_row = jnp.concatenate([head_gain] * (x.shape[-1] // HEAD_DIM), axis=-1)
    return x * lax.rsqrt(_head_mean_square(x) + EPS) * gain_row


def _head_cols(x, h):
    return x[:, h * HEAD_DIM:(h + 1) * HEAD_DIM].astype(BF16)


def _rope(x, cos, sin_lo, sin_hi):
    cols = []
    for c in range(x.shape[-1] // 128):
        xc = x[:, c * 128:(c + 1) * 128]
        cols.append(xc * cos + pltpu.roll(xc, 112, 1) * sin_lo + pltpu.roll(xc, 16, 1) * sin_hi)
    return cols[0] if len(cols) == 1 else jnp.concatenate(cols, axis=-1)


def _softmax_pv(scores, values, sink):
    m = jnp.max(scores[0], axis=-1, keepdims=True)
    for s in scores[1:]:
        m = jnp.maximum(m, jnp.max(s, axis=-1, keepdims=True))
    if sink is not None:
        m = jnp.maximum(m, sink)
    denom = None
    acc = None
    for s, v in zip(scores, values):
        p = jnp.exp(s - m)
        d = jnp.sum(p, axis=-1, keepdims=True)
        o = _dot(p.astype(BF16), v)
        denom = d if denom is None else denom + d
        acc = o if acc is None else acc + o
    if sink is not None:
        denom = denom + jnp.exp(sink - m)
    return acc / denom


def _block_diag(blocks):
    n = len(blocks)
    w = blocks[0].shape[0]
    rows = []
    for k, blk in enumerate(blocks):
        parts = []
        if k > 0:
            parts.append(jnp.zeros((w, k * w), F32))
        parts.append(blk)
        if k < n - 1:
            parts.append(jnp.zeros((w, (n - 1 - k) * w), F32))
        rows.append(jnp.concatenate(parts, axis=-1))
    return jnp.concatenate(rows, axis=0).astype(BF16)


def _rglru_gates(xc, wa, ba, wx, bx, lam):
    xb = xc.astype(BF16)
    r = jax.nn.sigmoid(_dot(xb, wa) + ba)
    i = jax.nn.sigmoid(_dot(xb, wx) + bx)
    softplus = jnp.maximum(-lam, 0.0) + jnp.log1p(jnp.exp(-jnp.abs(lam)))
    log_a = (-LRU_C) * r * softplus
    a = jnp.exp(log_a)
    b = jnp.sqrt(-jnp.tanh(log_a) * (a * a + 1.0)) * (i * xc)
    return a, b


def _block_prefix(a, b, reverse):
    t = a.shape[0]
    row = lax.broadcasted_iota(jnp.int32, a.shape, 0) & 7
    for d in (1, 2, 4):
        if reverse:
            a_s = pltpu.roll(a, t - d, 0)
            b_s = pltpu.roll(b, t - d, 0)
            ok = row < 8 - d
        else:
            a_s = pltpu.roll(a, d, 0)
            b_s = pltpu.roll(b, d, 0)
            ok = row >= d
        b = jnp.where(ok, a * b_s + b, b)
        a = jnp.where(ok, a * a_s, a)
    return a, b


def _conv4(x, w_ref, b_row):
    t = x.shape[0]
    row = lax.broadcasted_iota(jnp.int32, x.shape, 0)
    xm2 = jnp.where(row >= 2, pltpu.roll(x, 2, 0), 0.0)
    xm1 = jnp.where(row >= 1, pltpu.roll(x, 1, 0), 0.0)
    xp1 = jnp.where(row < t - 1, pltpu.roll(x, t - 1, 0), 0.0)
    return (xm2 * w_ref[0:1, :] + xm1 * w_ref[1:2, :] + x * w_ref[2:3, :] + xp1 * w_ref[3:4, :]) + b_row


def _rglru_mixer(cx, cy, conv_w_ref, conv_b, gate_w_ref, ba_ref, bx_ref, lam_ref, h0f, h0b,
                 af_ref, bf_ref, ab_ref, bb_ref, hf_ref, hb_ref):
    t = cx.shape[0]
    xc = _conv4(cx, conv_w_ref, conv_b)
    a, b = _rglru_gates(xc, gate_w_ref[0], ba_ref[0:1, :], gate_w_ref[1], bx_ref[0:1, :], lam_ref[0:1, :])
    a, b = _block_prefix(a, b, reverse=False)
    af_ref[...] = a
    bf_ref[...] = b
    a, b = _rglru_gates(xc, gate_w_ref[2], ba_ref[1:2, :], gate_w_ref[3], bx_ref[1:2, :], lam_ref[1:2, :])
    a, b = _block_prefix(a, b, reverse=True)
    ab_ref[...] = a
    bb_ref[...] = b
    nblk = t // 8

    def body(k, carry):
        cf, cb = carry
        rf = pl.ds(pl.multiple_of(k * 8, 8), 8)
        hf = bf_ref[rf, :] + af_ref[rf, :] * cf
        hf_ref[rf, :] = hf
        rb = pl.ds(pl.multiple_of((nblk - 1 - k) * 8, 8), 8)
        hb = bb_ref[rb, :] + ab_ref[rb, :] * cb
        hb_ref[rb, :] = hb
        return hf[7:8, :], hb[0:1, :]

    cf, cb = lax.fori_loop(0, nblk, body, (h0f, h0b))
    oc = (hf_ref[...] + hb_ref[...]) * _gelu_tanh(cy)
    return oc, cf, cb


def _store_gate_weights(gate_w_ref, wa_ref, wx_ref):
    for d in range(2):
        gate_w_ref[2 * d] = _block_diag([wa_ref[d, n] for n in range(N_HEADS)])
        gate_w_ref[2 * d + 1] = _block_diag([wx_ref[d, n] for n in range(N_HEADS)])


def _lane_head_masks(n):
    lane = lax.broadcasted_iota(jnp.int32, (1, n), 1) >> HEAD_SHIFT
    return [jnp.where(lane == h, 1.0, 0.0) for h in range(n // HEAD_DIM)]


def _log_decays(theta_ref, masks):
    theta = theta_ref[...]
    lanes = theta[:, 0:1] * masks[0]
    for h in range(1, N_HEADS):
        lanes = lanes + theta[:, h:h + 1] * masks[h]
    lg = jnp.log1p(-jnp.exp(lanes))
    return lg[0:1, :], lg[1:2, :]


def _retention_block(q, k8b, vb, q0, lgf, lgb, masks):
    tq, w = q.shape
    s_len = k8b.shape[0]
    nh = w // HEAD_DIM
    q_stack = jnp.concatenate([(q * masks[h]).astype(BF16) for h in range(nh)], axis=0)
    raw = _dot_nt(q_stack, k8b)
    qi = lax.broadcasted_iota(jnp.int32, (tq, s_len), 0) + q0
    si = lax.broadcasted_iota(jnp.int32, (tq, s_len), 1)
    rel = (qi - si).astype(F32)
    decs = []
    for h in range(nh):
        gf = lgf[:, h * HEAD_DIM:h * HEAD_DIM + 1]
        gb = lgb[:, h * HEAD_DIM:h * HEAD_DIM + 1]
        e = jnp.exp(jnp.where(rel >= 0, gf * rel, gb * (-rel)))
        decs.append(jnp.where(rel == 0, 2.0, e))
    inner = (raw * jnp.concatenate(decs, axis=0)).astype(BF16)
    out = _dot(inner, vb)
    o = out[0:tq, :] * masks[0]
    for h in range(1, nh):
        o = o + out[h * tq:(h + 1) * tq, :] * masks[h]
    return o


def _ctx_mixer_kernel(*refs, layer, n_alias):
    refs = refs[n_alias:]
    (x_ref, n2_ref, sh_ref, sc_ref, g2_ref, win_ref, wout_ref,
     aqn_ref, akn_ref, bqn_ref, bkn_ref, sink_ref,
     convw_ref, convb_ref, wa_ref, ba_ref, wx_ref, bx_ref, lam_ref, theta_ref, dn_ref,
     xn_ref, ka_ref, va_ref, kb_ref, vb_ref, stc_ref, std_ref,
     win_s, wout_s, u_ref, mixed_ref,
     gate_w_ref, af_ref, bf_ref, ab_ref, bb_ref, hf_ref, hb_ref) = refs
    t = SEQ
    lrow = slice(layer, layer + 1)

    @pl.when(pl.program_id(0) == 0)
    def _():
        for c in range(IN_WIDTH // 512):
            win_s[:, c * 512:(c + 1) * 512] = win_ref[:, c * 512:(c + 1) * 512].astype(BF16)
        wout_s[...] = wout_ref[...].astype(BF16)
        _store_gate_weights(gate_w_ref, wa_ref, wx_ref)

    x = x_ref[...]
    u_ref[...] = _project_in(x, n2_ref[lrow, :], sc_ref[0:1, :], sh_ref[0:1, :], win_s[...])

    for (cq, ck, cv, qn_ref, kn_ref, k_out, v_out, col0, use_sink) in (
            (COL_AQ, COL_AK, COL_AV, aqn_ref, akn_ref, ka_ref, va_ref, 0, True),
            (COL_BQ, COL_BK, COL_BV, bqn_ref, bkn_ref, kb_ref, vb_ref, GROUP_W, False)):
        q = _head_norm(u_ref[:, cq:cq + 256], qn_ref[lrow, :])
        k = _head_norm(u_ref[:, ck:ck + 128], kn_ref[lrow, :])
        v = u_ref[:, cv:cv + 128]
        k_out[...] = k
        v_out[...] = v
        qs = q * (HEAD_DIM ** -0.5)
        heads = []
        for h in range(N_HEADS):
            kv = h // 2
            s = _dot_nt(_head_cols(qs, h), _head_cols(k, kv))
            sink = jnp.full((t, 1), sink_ref[layer, h], F32) if use_sink else None
            heads.append(_softmax_pv([s], [_head_cols(v, kv)], sink))
        mixed_ref[:, col0:col0 + GROUP_W] = jnp.concatenate(heads, axis=-1).astype(BF16)

    zero = jnp.zeros((1, GROUP_W), F32)
    oc, cf, cb = _rglru_mixer(u_ref[:, COL_CX:COL_CX + 256], u_ref[:, COL_CY:COL_CY + 256],
                              convw_ref, convb_ref[lrow, :], gate_w_ref, ba_ref, bx_ref, lam_ref,
                              zero, zero, af_ref, bf_ref, ab_ref, bb_ref, hf_ref, hb_ref)
    mixed_ref[:, 2 * GROUP_W:3 * GROUP_W] = oc.astype(BF16)
    stc_ref[0:1, :] = cf
    stc_ref[1:2, :] = cb

    masks = _lane_head_masks(GROUP_W)
    lgf, lgb = _log_decays(theta_ref, masks)
    q = u_ref[:, COL_DQ:COL_DQ + 256]
    k8 = u_ref[:, COL_DK:COL_DK + 256] * (HEAD_DIM ** -0.5)
    vb = u_ref[:, COL_DV:COL_DV + 256].astype(BF16)
    o = _retention_block(q, k8.astype(BF16), vb, 0, lgf, lgb, masks)
    dn = dn_ref[lrow, :]
    o = o * lax.rsqrt(_head_mean_square(o) + EPS) * dn * _silu(u_ref[:, COL_DG:COL_DG + 256])
    mixed_ref[:, 3 * GROUP_W:4 * GROUP_W] = o.astype(BF16)
    pos = lax.broadcasted_iota(jnp.int32, (t, GROUP_W), 0).astype(F32)
    for d, (lg, expo) in enumerate(((lgf, (t - 1.0) - pos), (lgb, pos))):
        s_full = _dot_tn((k8 * jnp.exp(lg * expo)).astype(BF16), vb)
        for h in range(N_HEADS):
            std_ref[d, h] = s_full[h * 64:(h + 1) * 64, h * 64:(h + 1) * 64]

    xn_ref[...] = x + g2_ref[0:1, :] * _dot(mixed_ref[...], wout_s[...])


def _ctx_mixers(x, mod, layer, prev, norm2_g, w_in, w_out,
                a_qn, a_kn, a_sink, b_qn, b_kn, c_conv_w, c_conv_b, c_wa, c_ba, c_wx, c_bx,
                c_lambda, d_theta, d_norm_g):
    kv_out = pl.BlockSpec((None, None, SEQ, 128), lambda b: (b, layer, 0, 0))
    kv_shape = jax.ShapeDtypeStruct((BATCH, DEPTH, SEQ, 128), F32)
    scr = pltpu.VMEM((SEQ, GROUP_W), F32)
    n_alias = len(prev)
    out = pl.pallas_call(
        functools.partial(_ctx_mixer_kernel, layer=layer, n_alias=n_alias),
        grid=(BATCH,),
        in_specs=[pl.BlockSpec(memory_space=pl.ANY)] * n_alias + [
            pl.BlockSpec((SEQ, D_MODEL), lambda b: (b, 0)),
            _full((DEPTH, D_MODEL)),
            _mod_chunk(layer, 3), _mod_chunk(layer, 4), _mod_chunk(layer, 5),
            _once((None, D_MODEL, IN_WIDTH), lambda b: (layer, 0, 0)),
            _once((None, D_MODEL, D_MODEL), lambda b: (layer, 0, 0)),
            _full((DEPTH, HEAD_DIM)), _full((DEPTH, HEAD_DIM)), _full((DEPTH, HEAD_DIM)), _full((DEPTH, HEAD_DIM)),
            pl.BlockSpec(memory_space=pltpu.SMEM),
            _layer_block((4, GROUP_W), layer), _full((DEPTH, GROUP_W)),
            _layer_block((2, N_HEADS, HEAD_DIM, HEAD_DIM), layer), _layer_block((2, GROUP_W), layer),
            _layer_block((2, N_HEADS, HEAD_DIM, HEAD_DIM), layer), _layer_block((2, GROUP_W), layer),
            _layer_block((2, GROUP_W), layer),
            _layer_block((2, N_HEADS), layer), _full((DEPTH, GROUP_W)),
        ],
        out_specs=[
            pl.BlockSpec((SEQ, D_MODEL), lambda b: (b, 0)),
            kv_out, kv_out, kv_out, kv_out,
            pl.BlockSpec((None, None, 2, GROUP_W), lambda b: (b, layer, 0, 0)),
            pl.BlockSpec((None, None, 2, N_HEADS, HEAD_DIM, HEAD_DIM), lambda b: (b, layer, 0, 0, 0, 0)),
        ],
        out_shape=[
            jax.ShapeDtypeStruct((N_ROWS, D_MODEL), F32),
            kv_shape, kv_shape, kv_shape, kv_shape,
            jax.ShapeDtypeStruct((BATCH, DEPTH, 2, GROUP_W), F32),
            jax.ShapeDtypeStruct((BATCH, DEPTH, 2, N_HEADS, HEAD_DIM, HEAD_DIM), F32),
        ],
        input_output_aliases={k: k + 1 for k in range(n_alias)},
        scratch_shapes=[pltpu.VMEM((D_MODEL, IN_WIDTH), BF16), pltpu.VMEM((D_MODEL, D_MODEL), BF16),
                        pltpu.VMEM((SEQ, IN_WIDTH), F32), pltpu.VMEM((SEQ, D_MODEL), BF16),
                        pltpu.VMEM((4, GROUP_W, GROUP_W), BF16)] + [scr] * 6,
        compiler_params=_cparams("arbitrary"),
        name="ctx_mixers",
    )(*prev, x, norm2_g, mod, mod, mod, w_in, w_out,
      a_qn, a_kn, b_qn, b_kn, a_sink, c_conv_w, c_conv_b, c_wa, c_ba, c_wx, c_bx,
      c_lambda, d_theta, d_norm_g)
    return out[0], tuple(out[1:])


LAT_BLOCK0 = N_CTX_ROWS // DEC_SEQ


def _lat_attn_kernel(xn_in_ref, x_ref, n2_ref, sh_ref, sc_ref, g2_ref, win_ref, wout_ref,
                     kca_ref, vca_ref, kcb_ref, vcb_ref,
                     aqn_ref, akn_ref, bqn_ref, bkn_ref, sink_ref, cos_ref, sinl_ref, sinh_ref,
                     xn_ref, u_ref, o_ref, *, layer):
    del xn_in_ref
    t = DEC_SEQ
    lrow = slice(layer, layer + 1)
    mrow = pl.ds(1 + pl.program_id(0), 1)
    cos, sin_lo, sin_hi = cos_ref[...], sinl_ref[...], sinh_ref[...]
    scale = HEAD_DIM ** -0.5
    x = x_ref[...]
    u_ref[...] = _project_in(x, n2_ref[lrow, :], sc_ref[mrow, :], sh_ref[mrow, :], win_ref[...].astype(BF16))

    q = _rope(_head_norm(u_ref[:, COL_AQ:COL_AQ + 256], aqn_ref[lrow, :]), cos, sin_lo, sin_hi)
    k = _rope(_head_norm(u_ref[:, COL_AK:COL_AK + 128], akn_ref[lrow, :]), cos, sin_lo, sin_hi)
    qh = [_head_cols(q * scale, h) for h in range(4)]
    v = u_ref[:, COL_AV:COL_AV + 128]
    kh = [_head_cols(k, kv) for kv in range(2)]
    vh = [_head_cols(v, kv) for kv in range(2)]
    kch = [_head_cols(kca_ref[...], kv) for kv in range(2)]
    vch = [_head_cols(vca_ref[...], kv) for kv in range(2)]
    w = ATT_BLOCK
    span = 3 * w
    for n in range(t // w):
        start = min(max((n - 1) * w, 0), t - span)
        rows = slice(n * w, (n + 1) * w)
        band = slice(start, start + span)
        qpos = (lax.broadcasted_iota(jnp.int32, (2 * w, span), 0) & (w - 1)) + n * w
        kpos = lax.broadcasted_iota(jnp.int32, (2 * w, span), 1) + start
        valid = jnp.abs(qpos - kpos) <= WINDOW
        heads = []
        for kv in range(2):
            qp = jnp.concatenate([qh[2 * kv][rows, :], qh[2 * kv + 1][rows, :]], axis=0)
            s_ctx = _dot_nt(qp, kch[kv])
            s_band = jnp.where(valid, _dot_nt(qp, kh[kv][band, :]), NEG_INF)
            row = lax.broadcasted_iota(jnp.int32, (2 * w, 1), 0)
            sink = jnp.where(row < w, sink_ref[layer, 2 * kv], sink_ref[layer, 2 * kv + 1])
            o = _softmax_pv([s_ctx, s_band], [vch[kv], vh[kv][band, :]], sink)
            heads += [o[0:w, :], o[w:2 * w, :]]
        o_ref[rows, 0:GROUP_W] = jnp.concatenate(heads, axis=-1).astype(BF16)

    q = _rope(_head_norm(u_ref[:, COL_BQ:COL_BQ + 256], bqn_ref[lrow, :]), cos, sin_lo, sin_hi)
    k = _rope(_head_norm(u_ref[:, COL_BK:COL_BK + 128], bkn_ref[lrow, :]), cos, sin_lo, sin_hi)
    qh = [_head_cols(q * scale, h) for h in range(4)]
    v = u_ref[:, COL_BV:COL_BV + 128]
    kh = [_head_cols(k, kv) for kv in range(2)]
    vh = [_head_cols(v, kv) for kv in range(2)]
    kch = [_head_cols(kcb_ref[...], kv) for kv in range(2)]
    vch = [_head_cols(vcb_ref[...], kv) for kv in range(2)]
    tq = 256
    for n in range(t // tq):
        rows = slice(n * tq, (n + 1) * tq)
        heads = []
        for kv in range(2):
            qp = jnp.concatenate([qh[2 * kv][rows, :], qh[2 * kv + 1][rows, :]], axis=0)
            o = _softmax_pv([_dot_nt(qp, kch[kv]), _dot_nt(qp, kh[kv])], [vch[kv], vh[kv]], None)
            heads += [o[0:tq, :], o[tq:2 * tq, :]]
        o_ref[rows, GROUP_W:2 * GROUP_W] = jnp.concatenate(heads, axis=-1).astype(BF16)

    xn_ref[...] = x + g2_ref[mrow, :] * _dot(o_ref[...], wout_ref[...].astype(BF16))


def _lat_rglru_kernel(xn_in_ref, x_ref, n2_ref, sh_ref, sc_ref, g2_ref, win_ref, wout_ref, h0_ref,
                      convw_ref, convb_ref, wa_ref, ba_ref, wx_ref, bx_ref, lam_ref,
                      xn_ref, gate_w_ref, af_ref, bf_ref, ab_ref, bb_ref, hf_ref, hb_ref, *, layer):
    lrow = slice(layer, layer + 1)
    mrow = pl.ds(1 + pl.program_id(0), 1)

    @pl.when(pl.program_id(0) == 0)
    def _():
        _store_gate_weights(gate_w_ref, wa_ref, wx_ref)

    u = _project_in(x_ref[...], n2_ref[lrow, :], sc_ref[mrow, :], sh_ref[mrow, :], win_ref[...].astype(BF16))
    oc, _, _ = _rglru_mixer(u[:, 0:GROUP_W], u[:, GROUP_W:2 * GROUP_W], convw_ref, convb_ref[lrow, :],
                            gate_w_ref, ba_ref, bx_ref, lam_ref,
                            h0_ref[0:1, :], h0_ref[1:2, :],
                            af_ref, bf_ref, ab_ref, bb_ref, hf_ref, hb_ref)
    xn_ref[...] = xn_in_ref[...] + g2_ref[mrow, :] * _dot(oc.astype(BF16), wout_ref[...].astype(BF16))


def _lat_retention_kernel(xn_in_ref, x_ref, n2_ref, sh_ref, sc_ref, g2_ref, wqk_ref, wvg_ref, wout_ref,
                          s0_ref, theta_ref, dn_ref, xn_ref, o_ref, *, layer):
    t = DEC_SEQ
    lrow = slice(layer, layer + 1)
    mrow = pl.ds(1 + pl.program_id(0), 1)
    h = _norm_mod(x_ref[...], n2_ref[lrow, :], sc_ref[mrow, :], sh_ref[mrow, :]).astype(BF16)
    uqk = _dot(h, wqk_ref[...].astype(BF16))
    uvg = _dot(h, wvg_ref[...].astype(BF16))
    masks = _lane_head_masks(GROUP_W)
    lgf, lgb = _log_decays(theta_ref, masks)
    k8b = (uqk[:, GROUP_W:2 * GROUP_W] * (HEAD_DIM ** -0.5)).astype(BF16)
    vb = uvg[:, 0:GROUP_W].astype(BF16)
    s0f = _block_diag([s0_ref[0, hd] for hd in range(N_HEADS)])
    s0b = _block_diag([s0_ref[1, hd] for hd in range(N_HEADS)])
    dn = dn_ref[lrow, :]
    tq = 256
    for n in range(t // tq):
        rows = slice(n * tq, (n + 1) * tq)
        q = uqk[rows, 0:GROUP_W]
        o = _retention_block(q, k8b, vb, n * tq, lgf, lgb, masks)
        pos = lax.broadcasted_iota(jnp.int32, (tq, GROUP_W), 0).astype(F32) + float(n * tq)
        o = o + _dot((q * jnp.exp(lgf * (pos + 1.0))).astype(BF16), s0f)
        o = o + _dot((q * jnp.exp(lgb * (float(t) - pos))).astype(BF16), s0b)
        o = o * lax.rsqrt(_head_mean_square(o) + EPS) * dn * _silu(uvg[rows, GROUP_W:2 * GROUP_W])
        o_ref[rows, :] = o.astype(BF16)
    xn_ref[...] = xn_in_ref[...] + g2_ref[mrow, :] * _dot(o_ref[...], wout_ref[...].astype(BF16))


def _lat_mixers(xn, x, mod, layer, caches, state_c, state_d, rope, norm2_g, w_in, w_out,
                a_qn, a_kn, a_sink, b_qn, b_kn, c_conv_w, c_conv_b, c_wa, c_ba, c_wx, c_bx,
                c_lambda, d_theta, d_norm_g):
    rows = pl.BlockSpec((DEC_SEQ, D_MODEL), lambda b: (LAT_BLOCK0 + b, 0))
    cache_spec = pl.BlockSpec((None, None, PAST_LEN, 128), lambda b: (b, layer, 0, 0))
    gain = _full((DEPTH, HEAD_DIM))
    out_shape = jax.ShapeDtypeStruct((N_ROWS, D_MODEL), F32)
    win_cols = lambda w, c: _once((None, D_MODEL, w), lambda b: (layer, 0, c))
    wout_rows = lambda h, r: _once((None, h, D_MODEL), lambda b: (layer, r, 0))
    common = [rows, _full((DEPTH, D_MODEL)), _mod_chunk(layer, 3), _mod_chunk(layer, 4), _mod_chunk(layer, 5)]
    common_args = (x, norm2_g, mod, mod, mod)

    xn = pl.pallas_call(
        functools.partial(_lat_attn_kernel, layer=layer),
        grid=(DEC_BATCH,),
        in_specs=[pl.BlockSpec(memory_space=pl.ANY)] + common + [
            win_cols(4 * GROUP_W, 0), wout_rows(2 * GROUP_W, 0),
            cache_spec, cache_spec, cache_spec, cache_spec,
            gain, gain, gain, gain,
            pl.BlockSpec(memory_space=pltpu.SMEM),
            _full((DEC_SEQ, 128)), _full((DEC_SEQ, 128)), _full((DEC_SEQ, 128))],
        out_specs=rows,
        out_shape=out_shape,
        input_output_aliases={0: 0},
        scratch_shapes=[pltpu.VMEM((DEC_SEQ, 4 * GROUP_W), F32), pltpu.VMEM((DEC_SEQ, 2 * GROUP_W), BF16)],
        compiler_params=_cparams("arbitrary"),
        name="lat_attention",
    )(xn, *common_args, w_in, w_out, *caches, a_qn, a_kn, b_qn, b_kn, a_sink, *rope)

    scr = pltpu.VMEM((DEC_SEQ, GROUP_W), F32)
    xn = pl.pallas_call(
        functools.partial(_lat_rglru_kernel, layer=layer),
        grid=(DEC_BATCH,),
        in_specs=[rows] + common + [
            win_cols(2 * GROUP_W, COL_CX // (2 * GROUP_W)), wout_rows(GROUP_W, 2),
            pl.BlockSpec((None, None, 2, GROUP_W), lambda b: (b, layer, 0, 0)),
            _layer_block((4, GROUP_W), layer), _full((DEPTH, GROUP_W)),
            _layer_block((2, N_HEADS, HEAD_DIM, HEAD_DIM), layer), _layer_block((2, GROUP_W), layer),
            _layer_block((2, N_HEADS, HEAD_DIM, HEAD_DIM), layer), _layer_block((2, GROUP_W), layer),
            _layer_block((2, GROUP_W), layer)],
        out_specs=rows,
        out_shape=out_shape,
        input_output_aliases={0: 0},
        scratch_shapes=[pltpu.VMEM((4, GROUP_W, GROUP_W), BF16)] + [scr] * 6,
        compiler_params=_cparams("arbitrary"),
        name="lat_rglru",
    )(xn, *common_args, w_in, w_out, state_c, c_conv_w, c_conv_b, c_wa, c_ba, c_wx, c_bx, c_lambda)

    xn = pl.pallas_call(
        functools.partial(_lat_retention_kernel, layer=layer),
        grid=(DEC_BATCH,),
        in_specs=[rows] + common + [
            win_cols(2 * GROUP_W, COL_DQ // (2 * GROUP_W)), win_cols(2 * GROUP_W, COL_DV // (2 * GROUP_W)),
            wout_rows(GROUP_W, 3),
            pl.BlockSpec((None, None, 2, N_HEADS, HEAD_DIM, HEAD_DIM), lambda b: (b, layer, 0, 0, 0, 0)),
            _layer_block((2, N_HEADS), layer), _full((DEPTH, GROUP_W))],
        out_specs=rows,
        out_shape=out_shape,
        input_output_aliases={0: 0},
        scratch_shapes=[pltpu.VMEM((DEC_SEQ, GROUP_W), BF16)],
        compiler_params=_cparams("arbitrary"),
        name="lat_retention",
    )(xn, *common_args, w_in, w_in, w_out, state_d, d_theta, d_norm_g)
    return xn


def _rope_tables():
    t = np.arange(DEC_SEQ)
    row = (t // GRID_W).astype(np.float64)[:, None]
    col = (t % GRID_W).astype(np.float64)[:, None]
    half = HEAD_DIM // 2
    inv = 1.0 / (ROPE_BASE ** (np.arange(0, half, 2, dtype=np.float64) / half))
    j = np.arange(128) % HEAD_DIM
    ang = np.where((j < half)[None, :], row, col) * inv[j % (half // 2)][None, :]
    first = ((j % half) < half // 2)[None, :]
    cos, sin = np.cos(ang), np.sin(ang)
    return tuple(jnp.asarray(a, F32) for a in (cos, np.where(first, -sin, 0.0), np.where(first, 0.0, sin)))


def kernel(x_prompt, x_sample, cache_a_k, cache_a_v, cache_b_k, cache_b_v, state_c, state_d, c, c_ctx, norm1_g, norm2_g, norm3_g, w_mod, b_mod, ffn1_wg, ffn1_wu, ffn1_wd, ffn2_wg, ffn2_wu, ffn2_wd, w_in, w_out, a_qn, a_kn, a_sink, b_qn, b_kn, c_conv_w, c_conv_b, c_wa, c_ba, c_wx, c_bx, c_lambda, d_theta, d_norm_g):
    mod = _modulation(c_ctx, c, w_mod, b_mod)
    rope = _rope_tables()
    caches = tuple(t.reshape(DEC_BATCH, DEPTH, PAST_LEN, 128) for t in (cache_a_k, cache_a_v, cache_b_k, cache_b_v))
    mixer_params = (a_qn, a_kn, a_sink, b_qn, b_kn, c_conv_w, c_conv_b, c_wa, c_ba, c_wx, c_bx,
                    c_lambda, d_theta, d_norm_g)
    xs = (x_prompt.reshape(N_CTX_ROWS, D_MODEL), x_sample.reshape(N_LAT_ROWS, D_MODEL))
    states = ()
    for l in range(DEPTH):
        (x,) = _ffn(xs, mod, l, 0, norm1_g, ffn1_wg, ffn1_wu, ffn1_wd)
        xn, states = _ctx_mixers(x, mod, l, states, norm2_g, w_in, w_out, *mixer_params)
        xn = _lat_mixers(xn, x, mod, l, caches, state_c, state_d, rope, norm2_g, w_in, w_out, *mixer_params)
        xs = _ffn((xn,), mod, l, 6, norm3_g, ffn2_wg, ffn2_wu, ffn2_wd, split_out=(l == DEPTH - 1))
    y_p, y_s = xs
    ka, va, kb, vb, st_c, st_d = states
    kv_shape = (BATCH, DEPTH, SEQ, 2, HEAD_DIM)
    return (y_p.reshape(BATCH, SEQ, D_MODEL), y_s.reshape(DEC_BATCH, DEC_SEQ, D_MODEL),
            ka.reshape(kv_shape), va.reshape(kv_shape), kb.reshape(kv_shape), vb.reshape(kv_shape),
            st_c, st_d)
```

```python
import functools
import math

import numpy as np
import jax
import jax.numpy as jnp
from jax import lax
from jax.experimental import pallas as pl
from jax.experimental.pallas import tpu as pltpu

F32 = jnp.float32
BF16 = jnp.bfloat16

D_MODEL = 1024
BATCH = 16
SEQ = 256
DEPTH = 2
DEC_BATCH = 2
DEC_SEQ = 1024
PAST_LEN = 512
GRID_W = 64
HEAD_DIM = 64
HEAD_SHIFT = 6
N_HEADS = 4
GROUP_W = 256
WINDOW = 128
ATT_BLOCK = 128
ROPE_BASE = 10000.0
LRU_C = 8.0
D_FF = 2816
N_MOD = 9
EPS = 1e-6
NEG_INF = -1e30
IN_WIDTH = 2560

N_CTX_ROWS = BATCH * SEQ
N_LAT_ROWS = DEC_BATCH * DEC_SEQ
N_ROWS = N_CTX_ROWS + N_LAT_ROWS
MOD_ROWS = 8
MOD_GROUP = 1024

VMEM_LIMIT_BYTES = 56 * 1024 * 1024

COL_AQ, COL_AK, COL_AV = 0, 256, 384
COL_BQ, COL_BK, COL_BV = 512, 768, 896
COL_CX, COL_CY = 1024, 1280
COL_DQ, COL_DK, COL_DV, COL_DG = 1536, 1792, 2048, 2304


def _cparams(*sem):
    return pltpu.CompilerParams(dimension_semantics=sem, vmem_limit_bytes=VMEM_LIMIT_BYTES)


def _dot(a, b):
    return jnp.dot(a, b, preferred_element_type=F32)


def _dot_nt(a, b):
    return lax.dot_general(a, b, (((1,), (1,)), ((), ())), preferred_element_type=F32)


def _dot_tn(a, b):
    return lax.dot_general(a, b, (((0,), (0,)), ((), ())), preferred_element_type=F32)


def _silu(x):
    return x * jax.nn.sigmoid(x)


def _gelu_tanh(x):
    return 0.5 * x * (1.0 + jnp.tanh(math.sqrt(2.0 / math.pi) * (x + 0.044715 * (x * x * x))))


def _mod_row(i, tm, s):
    if tm >= MOD_GROUP:
        block_index = i * (tm // MOD_GROUP) + s
    else:
        block_index = i >> int(math.log2(MOD_GROUP // tm))
    return jnp.maximum(block_index - (N_CTX_ROWS // MOD_GROUP - 1), 0)


def _norm_mod(x, g, sc, sh):
    ms = jnp.mean(x * x, axis=-1, keepdims=True)
    return (x * lax.rsqrt(ms + EPS) * g) * (1.0 + sc) + sh


def _full(shape):
    return pl.BlockSpec(shape, lambda *_: (0,) * len(shape))


def _layer_block(shape, layer):
    return pl.BlockSpec((None,) + shape, lambda *_: (layer,) + (0,) * len(shape))


MOD_TN = 1024


def _mod_kernel(cc_ref, c_ref, w_ref, b_ref, o_ref):
    l = pl.program_id(0)
    pad = jnp.zeros((MOD_ROWS - 1 - DEC_BATCH, D_MODEL), F32)
    cond = jnp.concatenate([cc_ref[...], c_ref[...], pad], axis=0)
    o_ref[...] = _dot(_silu(cond).astype(BF16), w_ref[...].astype(BF16)) + b_ref[pl.ds(l, 1), :]


def _modulation(c_ctx, c, w_mod, b_mod):
    n = N_MOD * D_MODEL
    return pl.pallas_call(
        _mod_kernel,
        grid=(DEPTH, n // MOD_TN),
        in_specs=[
            pl.BlockSpec((1, D_MODEL), lambda l, j: (0, 0)),
            pl.BlockSpec((DEC_BATCH, D_MODEL), lambda l, j: (0, 0)),
            pl.BlockSpec((None, D_MODEL, MOD_TN), lambda l, j: (l, 0, j)),
            pl.BlockSpec((DEPTH, MOD_TN), lambda l, j: (0, j)),
        ],
        out_specs=pl.BlockSpec((None, MOD_ROWS, MOD_TN), lambda l, j: (l, 0, j)),
        out_shape=jax.ShapeDtypeStruct((DEPTH, MOD_ROWS, n), F32),
        compiler_params=_cparams("arbitrary", "arbitrary"),
        name="modulation",
    )(c_ctx.reshape(1, D_MODEL), c, w_mod, b_mod)


FFN_TM = 1024
FFN_TF = 256
N_CTX_TILES = N_CTX_ROWS // FFN_TM


FFN_NJ = D_FF // FFN_TF
N_FFN_TILES = N_ROWS // FFN_TM
N_FFN_STEPS = FFN_NJ + N_FFN_TILES - 1


def _ffn_tile(step):
    return jnp.maximum(step - (FFN_NJ - 1), 0)


def _on_stream_part(tile, x_refs, o_refs, fn):
    if len(x_refs) == 1 and len(o_refs) == 1:
        fn(x_refs[0], o_refs[0])
    else:
        pl.when(tile < N_CTX_TILES)(lambda: fn(x_refs[0], o_refs[0]))
        pl.when(tile >= N_CTX_TILES)(lambda: fn(x_refs[-1], o_refs[-1]))


def _ffn_kernel(*refs, layer, n_in, n_out):
    x_refs = refs[:n_in]
    n_ref, sh_ref, sc_ref, g_ref, wg_ref, wu_ref, wd_ref = refs[n_in:n_in + 7]
    o_refs = refs[n_in + 7:n_in + 7 + n_out]
    h_ref, acc_ref, wg_s, wu_s, wd_s = refs[n_in + 7 + n_out:]
    nj = FFN_NJ
    s = pl.program_id(0)
    tile = _ffn_tile(s)
    r = _mod_row(tile, FFN_TM, 0)

    def load_tile():
        def init(x_ref, _):
            h = _norm_mod(x_ref[...], n_ref[layer:layer + 1, :], sc_ref[pl.ds(r, 1), :], sh_ref[pl.ds(r, 1), :])
            h_ref[...] = h.astype(BF16)
        _on_stream_part(tile, x_refs, o_refs, init)

    def chunk(j):
        h = h_ref[...]
        a = _silu(_dot(h, wg_s[j])) * _dot(h, wu_s[j])
        return _dot(a.astype(BF16), wd_s[j])

    def store_tile(y):
        gate = 0.5 * g_ref[pl.ds(r, 1), :]

        def store(x_ref, o_ref):
            o_ref[...] = x_ref[...] + gate * y
        _on_stream_part(tile, x_refs, o_refs, store)

    @pl.when(s == 0)
    def _():
        load_tile()

    @pl.when(s < nj)
    def _():
        wg_s[s] = wg_ref[...].astype(BF16)
        wu_s[s] = wu_ref[...].astype(BF16)
        wd_s[s] = wd_ref[...].astype(BF16)
        y = chunk(s)

        @pl.when(s == 0)
        def _():
            acc_ref[...] = y

        @pl.when(s > 0)
        def _():
            acc_ref[...] += y

    @pl.when(s == nj - 1)
    def _():
        store_tile(acc_ref[...])

    @pl.when(s >= nj)
    def _():
        load_tile()
        y = chunk(0)
        for j in range(1, nj):
            y = y + chunk(j)
        store_tile(y)


def _stream_specs(split, buffered_once):
    tm = FFN_TM
    kw = {"pipeline_mode": pl.Buffered(1)} if buffered_once else {}
    if not split:
        return [pl.BlockSpec((tm, D_MODEL), lambda s: (_ffn_tile(s), 0), **kw)]
    last_ctx = N_CTX_TILES - 1
    return [pl.BlockSpec((tm, D_MODEL), lambda s: (jnp.minimum(_ffn_tile(s), last_ctx), 0), **kw),
            pl.BlockSpec((tm, D_MODEL), lambda s: (jnp.maximum(_ffn_tile(s) - N_CTX_TILES, 0), 0), **kw)]


def _ffn(xs, mod, layer, chunk0, norm_g, wg, wu, wd, split_out=False):
    tm, tf, nj = FFN_TM, FFN_TF, FFN_NJ
    split_in = len(xs) == 2
    mod_spec = lambda c: pl.BlockSpec((None, MOD_ROWS, D_MODEL), lambda s: (layer, 0, c))
    w_col = lambda s: (layer, 0, jnp.minimum(s, nj - 1))
    w_row = lambda s: (layer, jnp.minimum(s, nj - 1), 0)
    if split_out:
        out_shape = [jax.ShapeDtypeStruct((N_CTX_ROWS, D_MODEL), F32),
                     jax.ShapeDtypeStruct((N_LAT_ROWS, D_MODEL), F32)]
    else:
        out_shape = [jax.ShapeDtypeStruct((N_ROWS, D_MODEL), F32)]
    out = pl.pallas_call(
        functools.partial(_ffn_kernel, layer=layer, n_in=len(xs), n_out=len(out_shape)),
        grid=(N_FFN_STEPS,),
        in_specs=_stream_specs(split_in, split_in) + [
            _full((DEPTH, D_MODEL)),
            mod_spec(chunk0), mod_spec(chunk0 + 1), mod_spec(chunk0 + 2),
            pl.BlockSpec((None, D_MODEL, tf), w_col),
            pl.BlockSpec((None, D_MODEL, tf), w_col),
            pl.BlockSpec((None, tf, D_MODEL), w_row),
        ],
        out_specs=_stream_specs(split_out, True),
        out_shape=out_shape,
        scratch_shapes=[pltpu.VMEM((tm, D_MODEL), BF16),
                        pltpu.VMEM((tm, D_MODEL), F32),
                        pltpu.VMEM((nj, D_MODEL, tf), BF16),
                        pltpu.VMEM((nj, D_MODEL, tf), BF16),
                        pltpu.VMEM((nj, tf, D_MODEL), BF16)],
        compiler_params=_cparams("arbitrary"),
        name="ffn",
    )(*xs, norm_g, mod, mod, mod, wg, wu, wd)
    return tuple(out)


def _once(shape, index_map):
    return pl.BlockSpec(shape, index_map, pipeline_mode=pl.Buffered(1))


def _mod_chunk(layer, c):
    return pl.BlockSpec((None, MOD_ROWS, D_MODEL), lambda *_: (layer, 0, c))


def _project_in(x, n_row, sc_row, sh_row, w_bf16):
    return _dot(_norm_mod(x, n_row, sc_row, sh_row).astype(BF16), w_bf16)

def _head_mean_square(x):
    n = x.shape[-1]
    x2 = x * x
    hi = x2.astype(BF16)
    lo = (x2 - hi.astype(F32)).astype(BF16)
    r = lax.broadcasted_iota(jnp.int32, (n, n), 0) >> HEAD_SHIFT
    c = lax.broadcasted_iota(jnp.int32, (n, n), 1) >> HEAD_SHIFT
    ones_bd = jnp.where(r == c, 1.0, 0.0).astype(BF16)
    return (_dot(hi, ones_bd) + _dot(lo, ones_bd)) * (1.0 / HEAD_DIM)


def _head_norm(x, head_gain):
    gain_row = jnp.concatenate([head_gain] * (x.shape[-1] // HEAD_DIM), axis=-1)
    return x * lax.rsqrt(_head_mean_square(x) + EPS) * gain_row


def _head_cols(x, h):
    return x[:, h * HEAD_DIM:(h + 1) * HEAD_DIM].astype(BF16)


def _rope(x, cos, sin_lo, sin_hi):
    cols = []
    for c in range(x.shape[-1] // 128):
        xc = x[:, c * 128:(c + 1) * 128]
        cols.append(xc * cos + pltpu.roll(xc, 112, 1) * sin_lo + pltpu.roll(xc, 16, 1) * sin_hi)
    return cols[0] if len(cols) == 1 else jnp.concatenate(cols, axis=-1)


def _softmax_pv(scores, values, sink):
    m = jnp.max(scores[0], axis=-1, keepdims=True)
    for s in scores[1:]:
        m = jnp.maximum(m, jnp.max(s, axis=-1, keepdims=True))
    if sink is not None:
        m = jnp.maximum(m, sink)
    denom = None
    acc = None
    for s, v in zip(scores, values):
        p = jnp.exp(s - m)
        d = jnp.sum(p, axis=-1, keepdims=True)
        o = _dot(p.astype(BF16), v)
        denom = d if denom is None else denom + d
        acc = o if acc is None else acc + o
    if sink is not None:
        denom = denom + jnp.exp(sink - m)
    return acc / denom


def _block_diag(blocks):
    n = len(blocks)
    w = blocks[0].shape[0]
    rows = []
    for k, blk in enumerate(blocks):
        parts = []
        if k > 0:
            parts.append(jnp.zeros((w, k * w), F32))
        parts.append(blk)
        if k < n - 1:
            parts.append(jnp.zeros((w, (n - 1 - k) * w), F32))
        rows.append(jnp.concatenate(parts, axis=-1))
    return jnp.concatenate(rows, axis=0).astype(BF16)


def _rglru_gates(xc, wa, ba, wx, bx, lam):
    xb = xc.astype(BF16)
    r = jax.nn.sigmoid(_dot(xb, wa) + ba)
    i = jax.nn.sigmoid(_dot(xb, wx) + bx)
    softplus = jnp.maximum(-lam, 0.0) + jnp.log1p(jnp.exp(-jnp.abs(lam)))
    log_a = (-LRU_C) * r * softplus
    a = jnp.exp(log_a)
    b = jnp.sqrt(-jnp.tanh(log_a) * (a * a + 1.0)) * (i * xc)
    return a, b


def _block_prefix(a, b, reverse):
    t = a.shape[0]
    row = lax.broadcasted_iota(jnp.int32, a.shape, 0) & 7
    for d in (1, 2, 4):
        if reverse:
            a_s = pltpu.roll(a, t - d, 0)
            b_s = pltpu.roll(b, t - d, 0)
            ok = row < 8 - d
        else:
            a_s = pltpu.roll(a, d, 0)
            b_s = pltpu.roll(b, d, 0)
            ok = row >= d
        b = jnp.where(ok, a * b_s + b, b)
        a = jnp.where(ok, a * a_s, a)
    return a, b


def _conv4(x, w_ref, b_row):
    t = x.shape[0]
    row = lax.broadcasted_iota(jnp.int32, x.shape, 0)
    xm2 = jnp.where(row >= 2, pltpu.roll(x, 2, 0), 0.0)
    xm1 = jnp.where(row >= 1, pltpu.roll(x, 1, 0), 0.0)
    xp1 = jnp.where(row < t - 1, pltpu.roll(x, t - 1, 0), 0.0)
    return (xm2 * w_ref[0:1, :] + xm1 * w_ref[1:2, :] + x * w_ref[2:3, :] + xp1 * w_ref[3:4, :]) + b_row


def _rglru_mixer(cx, cy, conv_w_ref, conv_b, gate_w_ref, ba_ref, bx_ref, lam_ref, h0f, h0b,
                 af_ref, bf_ref, ab_ref, bb_ref, hf_ref, hb_ref):
    t = cx.shape[0]
    xc = _conv4(cx, conv_w_ref, conv_b)
    a, b = _rglru_gates(xc, gate_w_ref[0], ba_ref[0:1, :], gate_w_ref[1], bx_ref[0:1, :], lam_ref[0:1, :])
    a, b = _block_prefix(a, b, reverse=False)
    af_ref[...] = a
    bf_ref[...] = b
    a, b = _rglru_gates(xc, gate_w_ref[2], ba_ref[1:2, :], gate_w_ref[3], bx_ref[1:2, :], lam_ref[1:2, :])
    a, b = _block_prefix(a, b, reverse=True)
    ab_ref[...] = a
    bb_ref[...] = b
    nblk = t // 8

    def body(k, carry):
        cf, cb = carry
        rf = pl.ds(pl.multiple_of(k * 8, 8), 8)
        hf = bf_ref[rf, :] + af_ref[rf, :] * cf
        hf_ref[rf, :] = hf
        rb = pl.ds(pl.multiple_of((nblk - 1 - k) * 8, 8), 8)
        hb = bb_ref[rb, :] + ab_ref[rb, :] * cb
        hb_ref[rb, :] = hb
        return hf[7:8, :], hb[0:1, :]

    cf, cb = lax.fori_loop(0, nblk, body, (h0f, h0b))
    oc = (hf_ref[...] + hb_ref[...]) * _gelu_tanh(cy)
    return oc, cf, cb


def _store_gate_weights(gate_w_ref, wa_ref, wx_ref):
    for d in range(2):
        gate_w_ref[2 * d] = _block_diag([wa_ref[d, n] for n in range(N_HEADS)])
        gate_w_ref[2 * d + 1] = _block_diag([wx_ref[d, n] for n in range(N_HEADS)])


def _lane_head_masks(n):
    lane = lax.broadcasted_iota(jnp.int32, (1, n), 1) >> HEAD_SHIFT
    return [jnp.where(lane == h, 1.0, 0.0) for h in range(n // HEAD_DIM)]


def _log_decays(theta_ref, masks):
    theta = theta_ref[...]
    lanes = theta[:, 0:1] * masks[0]
    for h in range(1, N_HEADS):
        lanes = lanes + theta[:, h:h + 1] * masks[h]
    lg = jnp.log1p(-jnp.exp(lanes))
    return lg[0:1, :], lg[1:2, :]


def _retention_block(q, k8b, vb, q0, lgf, lgb, masks):
    tq, w = q.shape
    s_len = k8b.shape[0]
    nh = w // HEAD_DIM
    q_stack = jnp.concatenate([(q * masks[h]).astype(BF16) for h in range(nh)], axis=0)
    raw = _dot_nt(q_stack, k8b)
    qi = lax.broadcasted_iota(jnp.int32, (tq, s_len), 0) + q0
    si = lax.broadcasted_iota(jnp.int32, (tq, s_len), 1)
    rel = (qi - si).astype(F32)
    decs = []
    for h in range(nh):
        gf = lgf[:, h * HEAD_DIM:h * HEAD_DIM + 1]
        gb = lgb[:, h * HEAD_DIM:h * HEAD_DIM + 1]
        e = jnp.exp(jnp.where(rel >= 0, gf * rel, gb * (-rel)))
        decs.append(jnp.where(rel == 0, 2.0, e))
    inner = (raw * jnp.concatenate(decs, axis=0)).astype(BF16)
    out = _dot(inner, vb)
    o = out[0:tq, :] * masks[0]
    for h in range(1, nh):
        o = o + out[h * tq:(h + 1) * tq, :] * masks[h]
    return o


def _ctx_mixer_kernel(*refs, layer, n_alias):
    refs = refs[n_alias:]
    (x_ref, n2_ref, sh_ref, sc_ref, g2_ref, win_ref, wout_ref,
     aqn_ref, akn_ref, bqn_ref, bkn_ref, sink_ref,
     convw_ref, convb_ref, wa_ref, ba_ref, wx_ref, bx_ref, lam_ref, theta_ref, dn_ref,
     xn_ref, ka_ref, va_ref, kb_ref, vb_ref, stc_ref, std_ref,
     win_s, wout_s, u_ref, mixed_ref,
     gate_w_ref, af_ref, bf_ref, ab_ref, bb_ref, hf_ref, hb_ref) = refs
    t = SEQ
    lrow = slice(layer, layer + 1)

    @pl.when(pl.program_id(0) == 0)
    def _():
        for c in range(IN_WIDTH // 512):
            win_s[:, c * 512:(c + 1) * 512] = win_ref[:, c * 512:(c + 1) * 512].astype(BF16)
        wout_s[...] = wout_ref[...].astype(BF16)
        _store_gate_weights(gate_w_ref, wa_ref, wx_ref)

    x = x_ref[...]
    u_ref[...] = _project_in(x, n2_ref[lrow, :], sc_ref[0:1, :], sh_ref[0:1, :], win_s[...])

    for (cq, ck, cv, qn_ref, kn_ref, k_out, v_out, col0, use_sink) in (
            (COL_AQ, COL_AK, COL_AV, aqn_ref, akn_ref, ka_ref, va_ref, 0, True),
            (COL_BQ, COL_BK, COL_BV, bqn_ref, bkn_ref, kb_ref, vb_ref, GROUP_W, False)):
        q = _head_norm(u_ref[:, cq:cq + 256], qn_ref[lrow, :])
        k = _head_norm(u_ref[:, ck:ck + 128], kn_ref[lrow, :])
        v = u_ref[:, cv:cv + 128]
        k_out[...] = k
        v_out[...] = v
        qs = q * (HEAD_DIM ** -0.5)
        heads = []
        for h in range(N_HEADS):
            kv = h // 2
            s = _dot_nt(_head_cols(qs, h), _head_cols(k, kv))
            sink = jnp.full((t, 1), sink_ref[layer, h], F32) if use_sink else None
            heads.append(_softmax_pv([s], [_head_cols(v, kv)], sink))
        mixed_ref[:, col0:col0 + GROUP_W] = jnp.concatenate(heads, axis=-1).astype(BF16)

    zero = jnp.zeros((1, GROUP_W), F32)
    oc, cf, cb = _rglru_mixer(u_ref[:, COL_CX:COL_CX + 256], u_ref[:, COL_CY:COL_CY + 256],
                              convw_ref, convb_ref[lrow, :], gate_w_ref, ba_ref, bx_ref, lam_ref,
                              zero, zero, af_ref, bf_ref, ab_ref, bb_ref, hf_ref, hb_ref)
    mixed_ref[:, 2 * GROUP_W:3 * GROUP_W] = oc.astype(BF16)
    stc_ref[0:1, :] = cf
    stc_ref[1:2, :] = cb

    masks = _lane_head_masks(GROUP_W)
    lgf, lgb = _log_decays(theta_ref, masks)
    q = u_ref[:, COL_DQ:COL_DQ + 256]
    k8 = u_ref[:, COL_DK:COL_DK + 256] * (HEAD_DIM ** -0.5)
    vb = u_ref[:, COL_DV:COL_DV + 256].astype(BF16)
    o = _retention_block(q, k8.astype(BF16), vb, 0, lgf, lgb, masks)
    dn = dn_ref[lrow, :]
    o = o * lax.rsqrt(_head_mean_square(o) + EPS) * dn * _silu(u_ref[:, COL_DG:COL_DG + 256])
    mixed_ref[:, 3 * GROUP_W:4 * GROUP_W] = o.astype(BF16)
    pos = lax.broadcasted_iota(jnp.int32, (t, GROUP_W), 0).astype(F32)
    for d, (lg, expo) in enumerate(((lgf, (t - 1.0) - pos), (lgb, pos))):
        s_full = _dot_tn((k8 * jnp.exp(lg * expo)).astype(BF16), vb)
        for h in range(N_HEADS):
            std_ref[d, h] = s_full[h * 64:(h + 1) * 64, h * 64:(h + 1) * 64]

    xn_ref[...] = x + g2_ref[0:1, :] * _dot(mixed_ref[...], wout_s[...])


def _ctx_mixers(x, mod, layer, prev, norm2_g, w_in, w_out,
                a_qn, a_kn, a_sink, b_qn, b_kn, c_conv_w, c_conv_b, c_wa, c_ba, c_wx, c_bx,
                c_lambda, d_theta, d_norm_g):
    kv_out = pl.BlockSpec((None, None, SEQ, 128), lambda b: (b, layer, 0, 0))
    kv_shape = jax.ShapeDtypeStruct((BATCH, DEPTH, SEQ, 128), F32)
    scr = pltpu.VMEM((SEQ, GROUP_W), F32)
    n_alias = len(prev)
    out = pl.pallas_call(
        functools.partial(_ctx_mixer_kernel, layer=layer, n_alias=n_alias),
        grid=(BATCH,),
        in_specs=[pl.BlockSpec(memory_space=pl.ANY)] * n_alias + [
            pl.BlockSpec((SEQ, D_MODEL), lambda b: (b, 0)),
            _full((DEPTH, D_MODEL)),
            _mod_chunk(layer, 3), _mod_chunk(layer, 4), _mod_chunk(layer, 5),
            _once((None, D_MODEL, IN_WIDTH), lambda b: (layer, 0, 0)),
            _once((None, D_MODEL, D_MODEL), lambda b: (layer, 0, 0)),
            _full((DEPTH, HEAD_DIM)), _full((DEPTH, HEAD_DIM)), _full((DEPTH, HEAD_DIM)), _full((DEPTH, HEAD_DIM)),
            pl.BlockSpec(memory_space=pltpu.SMEM),
            _layer_block((4, GROUP_W), layer), _full((DEPTH, GROUP_W)),
            _layer_block((2, N_HEADS, HEAD_DIM, HEAD_DIM), layer), _layer_block((2, GROUP_W), layer),
            _layer_block((2, N_HEADS, HEAD_DIM, HEAD_DIM), layer), _layer_block((2, GROUP_W), layer),
            _layer_block((2, GROUP_W), layer),
            _layer_block((2, N_HEADS), layer), _full((DEPTH, GROUP_W)),
        ],
        out_specs=[
            pl.BlockSpec((SEQ, D_MODEL), lambda b: (b, 0)),
            kv_out, kv_out, kv_out, kv_out,
            pl.BlockSpec((None, None, 2, GROUP_W), lambda b: (b, layer, 0, 0)),
            pl.BlockSpec((None, None, 2, N_HEADS, HEAD_DIM, HEAD_DIM), lambda b: (b, layer, 0, 0, 0, 0)),
        ],
        out_shape=[
            jax.ShapeDtypeStruct((N_ROWS, D_MODEL), F32),
            kv_shape, kv_shape, kv_shape, kv_shape,
            jax.ShapeDtypeStruct((BATCH, DEPTH, 2, GROUP_W), F32),
            jax.ShapeDtypeStruct((BATCH, DEPTH, 2, N_HEADS, HEAD_DIM, HEAD_DIM), F32),
        ],
        input_output_aliases={k: k + 1 for k in range(n_alias)},
        scratch_shapes=[pltpu.VMEM((D_MODEL, IN_WIDTH), BF16), pltpu.VMEM((D_MODEL, D_MODEL), BF16),
                        pltpu.VMEM((SEQ, IN_WIDTH), F32), pltpu.VMEM((SEQ, D_MODEL), BF16),
                        pltpu.VMEM((4, GROUP_W, GROUP_W), BF16)] + [scr] * 6,
        compiler_params=_cparams("arbitrary"),
        name="ctx_mixers",
    )(*prev, x, norm2_g, mod, mod, mod, w_in, w_out,
      a_qn, a_kn, b_qn, b_kn, a_sink, c_conv_w, c_conv_b, c_wa, c_ba, c_wx, c_bx,
      c_lambda, d_theta, d_norm_g)
    return out[0], tuple(out[1:])


LAT_BLOCK0 = N_CTX_ROWS // DEC_SEQ


def _lat_attn_kernel(xn_in_ref, x_ref, n2_ref, sh_ref, sc_ref, g2_ref, win_ref, wout_ref,
                     kca_ref, vca_ref, kcb_ref, vcb_ref,
                     aqn_ref, akn_ref, bqn_ref, bkn_ref, sink_ref, cos_ref, sinl_ref, sinh_ref,
                     xn_ref, u_ref, o_ref, *, layer):
    del xn_in_ref
    t = DEC_SEQ
    lrow = slice(layer, layer + 1)
    mrow = pl.ds(1 + pl.program_id(0), 1)
    cos, sin_lo, sin_hi = cos_ref[...], sinl_ref[...], sinh_ref[...]
    scale = HEAD_DIM ** -0.5
    x = x_ref[...]
    u_ref[...] = _project_in(x, n2_ref[lrow, :], sc_ref[mrow, :], sh_ref[mrow, :], win_ref[...].astype(BF16))

    q = _rope(_head_norm(u_ref[:, COL_AQ:COL_AQ + 256], aqn_ref[lrow, :]), cos, sin_lo, sin_hi)
    k = _rope(_head_norm(u_ref[:, COL_AK:COL_AK + 128], akn_ref[lrow, :]), cos, sin_lo, sin_hi)
    qh = [_head_cols(q * scale, h) for h in range(4)]
    v = u_ref[:, COL_AV:COL_AV + 128]
    kh = [_head_cols(k, kv) for kv in range(2)]
    vh = [_head_cols(v, kv) for kv in range(2)]
    kch = [_head_cols(kca_ref[...], kv) for kv in range(2)]
    vch = [_head_cols(vca_ref[...], kv) for kv in range(2)]
    w = ATT_BLOCK
    span = 3 * w
    for n in range(t // w):
        start = min(max((n - 1) * w, 0), t - span)
        rows = slice(n * w, (n + 1) * w)
        band = slice(start, start + span)
        qpos = (lax.broadcasted_iota(jnp.int32, (2 * w, span), 0) & (w - 1)) + n * w
        kpos = lax.broadcasted_iota(jnp.int32, (2 * w, span), 1) + start
        valid = jnp.abs(qpos - kpos) <= WINDOW
        heads = []
        for kv in range(2):
            qp = jnp.concatenate([qh[2 * kv][rows, :], qh[2 * kv + 1][rows, :]], axis=0)
            s_ctx = _dot_nt(qp, kch[kv])
            s_band = jnp.where(valid, _dot_nt(qp, kh[kv][band, :]), NEG_INF)
            row = lax.broadcasted_iota(jnp.int32, (2 * w, 1), 0)
            sink = jnp.where(row < w, sink_ref[layer, 2 * kv], sink_ref[layer, 2 * kv + 1])
            o = _softmax_pv([s_ctx, s_band], [vch[kv], vh[kv][band, :]], sink)
            heads += [o[0:w, :], o[w:2 * w, :]]
        o_ref[rows, 0:GROUP_W] = jnp.concatenate(heads, axis=-1).astype(BF16)

    q = _rope(_head_norm(u_ref[:, COL_BQ:COL_BQ + 256], bqn_ref[lrow, :]), cos, sin_lo, sin_hi)
    k = _rope(_head_norm(u_ref[:, COL_BK:COL_BK + 128], bkn_ref[lrow, :]), cos, sin_lo, sin_hi)
    qh = [_head_cols(q * scale, h) for h in range(4)]
    v = u_ref[:, COL_BV:COL_BV + 128]
    kh = [_head_cols(k, kv) for kv in range(2)]
    vh = [_head_cols(v, kv) for kv in range(2)]
    kch = [_head_cols(kcb_ref[...], kv) for kv in range(2)]
    vch = [_head_cols(vcb_ref[...], kv) for kv in range(2)]
    tq = 256
    for n in range(t // tq):
        rows = slice(n * tq, (n + 1) * tq)
        heads = []
        for kv in range(2):
            qp = jnp.concatenate([qh[2 * kv][rows, :], qh[2 * kv + 1][rows, :]], axis=0)
            o = _softmax_pv([_dot_nt(qp, kch[kv]), _dot_nt(qp, kh[kv])], [vch[kv], vh[kv]], None)
            heads += [o[0:tq, :], o[tq:2 * tq, :]]
        o_ref[rows, GROUP_W:2 * GROUP_W] = jnp.concatenate(heads, axis=-1).astype(BF16)

    xn_ref[...] = x + g2_ref[mrow, :] * _dot(o_ref[...], wout_ref[...].astype(BF16))


def _lat_rglru_kernel(xn_in_ref, x_ref, n2_ref, sh_ref, sc_ref, g2_ref, win_ref, wout_ref, h0_ref,
                      convw_ref, convb_ref, wa_ref, ba_ref, wx_ref, bx_ref, lam_ref,
                      xn_ref, gate_w_ref, af_ref, bf_ref, ab_ref, bb_ref, hf_ref, hb_ref, *, layer):
    lrow = slice(layer, layer + 1)
    mrow = pl.ds(1 + pl.program_id(0), 1)

    @pl.when(pl.program_id(0) == 0)
    def _():
        _store_gate_weights(gate_w_ref, wa_ref, wx_ref)

    u = _project_in(x_ref[...], n2_ref[lrow, :], sc_ref[mrow, :], sh_ref[mrow, :], win_ref[...].astype(BF16))
    oc, _, _ = _rglru_mixer(u[:, 0:GROUP_W], u[:, GROUP_W:2 * GROUP_W], convw_ref, convb_ref[lrow, :],
                            gate_w_ref, ba_ref, bx_ref, lam_ref,
                            h0_ref[0:1, :], h0_ref[1:2, :],
                            af_ref, bf_ref, ab_ref, bb_ref, hf_ref, hb_ref)
    xn_ref[...] = xn_in_ref[...] + g2_ref[mrow, :] * _dot(oc.astype(BF16), wout_ref[...].astype(BF16))


def _lat_retention_kernel(xn_in_ref, x_ref, n2_ref, sh_ref, sc_ref, g2_ref, wqk_ref, wvg_ref, wout_ref,
                          s0_ref, theta_ref, dn_ref, xn_ref, o_ref, *, layer):
    t = DEC_SEQ
    lrow = slice(layer, layer + 1)
    mrow = pl.ds(1 + pl.program_id(0), 1)
    h = _norm_mod(x_ref[...], n2_ref[lrow, :], sc_ref[mrow, :], sh_ref[mrow, :]).astype(BF16)
    uqk = _dot(h, wqk_ref[...].astype(BF16))
    uvg = _dot(h, wvg_ref[...].astype(BF16))
    masks = _lane_head_masks(GROUP_W)
    lgf, lgb = _log_decays(theta_ref, masks)
    k8b = (uqk[:, GROUP_W:2 * GROUP_W] * (HEAD_DIM ** -0.5)).astype(BF16)
    vb = uvg[:, 0:GROUP_W].astype(BF16)
    s0f = _block_diag([s0_ref[0, hd] for hd in range(N_HEADS)])
    s0b = _block_diag([s0_ref[1, hd] for hd in range(N_HEADS)])
    dn = dn_ref[lrow, :]
    tq = 256
    for n in range(t // tq):
        rows = slice(n * tq, (n + 1) * tq)
        q = uqk[rows, 0:GROUP_W]
        o = _retention_block(q, k8b, vb, n * tq, lgf, lgb, masks)
        pos = lax.broadcasted_iota(jnp.int32, (tq, GROUP_W), 0).astype(F32) + float(n * tq)
        o = o + _dot((q * jnp.exp(lgf * (pos + 1.0))).astype(BF16), s0f)
        o = o + _dot((q * jnp.exp(lgb * (float(t) - pos))).astype(BF16), s0b)
        o = o * lax.rsqrt(_head_mean_square(o) + EPS) * dn * _silu(uvg[rows, GROUP_W:2 * GROUP_W])
        o_ref[rows, :] = o.astype(BF16)
    xn_ref[...] = xn_in_ref[...] + g2_ref[mrow, :] * _dot(o_ref[...], wout_ref[...].astype(BF16))


def _lat_mixers(xn, x, mod, layer, caches, state_c, state_d, rope, norm2_g, w_in, w_out,
                a_qn, a_kn, a_sink, b_qn, b_kn, c_conv_w, c_conv_b, c_wa, c_ba, c_wx, c_bx,
                c_lambda, d_theta, d_norm_g):
    rows = pl.BlockSpec((DEC_SEQ, D_MODEL), lambda b: (LAT_BLOCK0 + b, 0))
    cache_spec = pl.BlockSpec((None, None, PAST_LEN, 128), lambda b: (b, layer, 0, 0))
    gain = _full((DEPTH, HEAD_DIM))
    out_shape = jax.ShapeDtypeStruct((N_ROWS, D_MODEL), F32)
    win_cols = lambda w, c: _once((None, D_MODEL, w), lambda b: (layer, 0, c))
    wout_rows = lambda h, r: _once((None, h, D_MODEL), lambda b: (layer, r, 0))
    common = [rows, _full((DEPTH, D_MODEL)), _mod_chunk(layer, 3), _mod_chunk(layer, 4), _mod_chunk(layer, 5)]
    common_args = (x, norm2_g, mod, mod, mod)

    xn = pl.pallas_call(
        functools.partial(_lat_attn_kernel, layer=layer),
        grid=(DEC_BATCH,),
        in_specs=[pl.BlockSpec(memory_space=pl.ANY)] + common + [
            win_cols(4 * GROUP_W, 0), wout_rows(2 * GROUP_W, 0),
            cache_spec, cache_spec, cache_spec, cache_spec,
            gain, gain, gain, gain,
            pl.BlockSpec(memory_space=pltpu.SMEM),
            _full((DEC_SEQ, 128)), _full((DEC_SEQ, 128)), _full((DEC_SEQ, 128))],
        out_specs=rows,
        out_shape=out_shape,
        input_output_aliases={0: 0},
        scratch_shapes=[pltpu.VMEM((DEC_SEQ, 4 * GROUP_W), F32), pltpu.VMEM((DEC_SEQ, 2 * GROUP_W), BF16)],
        compiler_params=_cparams("arbitrary"),
        name="lat_attention",
    )(xn, *common_args, w_in, w_out, *caches, a_qn, a_kn, b_qn, b_kn, a_sink, *rope)

    scr = pltpu.VMEM((DEC_SEQ, GROUP_W), F32)
    xn = pl.pallas_call(
        functools.partial(_lat_rglru_kernel, layer=layer),
        grid=(DEC_BATCH,),
        in_specs=[rows] + common + [
            win_cols(2 * GROUP_W, COL_CX // (2 * GROUP_W)), wout_rows(GROUP_W, 2),
            pl.BlockSpec((None, None, 2, GROUP_W), lambda b: (b, layer, 0, 0)),
            _layer_block((4, GROUP_W), layer), _full((DEPTH, GROUP_W)),
            _layer_block((2, N_HEADS, HEAD_DIM, HEAD_DIM), layer), _layer_block((2, GROUP_W), layer),
            _layer_block((2, N_HEADS, HEAD_DIM, HEAD_DIM), layer), _layer_block((2, GROUP_W), layer),
            _layer_block((2, GROUP_W), layer)],
        out_specs=rows,
        out_shape=out_shape,
        input_output_aliases={0: 0},
        scratch_shapes=[pltpu.VMEM((4, GROUP_W, GROUP_W), BF16)] + [scr] * 6,
        compiler_params=_cparams("arbitrary"),
        name="lat_rglru",
    )(xn, *common_args, w_in, w_out, state_c, c_conv_w, c_conv_b, c_wa, c_ba, c_wx, c_bx, c_lambda)

    xn = pl.pallas_call(
        functools.partial(_lat_retention_kernel, layer=layer),
        grid=(DEC_BATCH,),
        in_specs=[rows] + common + [
            win_cols(2 * GROUP_W, COL_DQ // (2 * GROUP_W)), win_cols(2 * GROUP_W, COL_DV // (2 * GROUP_W)),
            wout_rows(GROUP_W, 3),
            pl.BlockSpec((None, None, 2, N_HEADS, HEAD_DIM, HEAD_DIM), lambda b: (b, layer, 0, 0, 0, 0)),
            _layer_block((2, N_HEADS), layer), _full((DEPTH, GROUP_W))],
        out_specs=rows,
        out_shape=out_shape,
        input_output_aliases={0: 0},
        scratch_shapes=[pltpu.VMEM((DEC_SEQ, GROUP_W), BF16)],
        compiler_params=_cparams("arbitrary"),
        name="lat_retention",
    )(xn, *common_args, w_in, w_in, w_out, state_d, d_theta, d_norm_g)
    return xn


def _rope_tables():
    t = np.arange(DEC_SEQ)
    row = (t // GRID_W).astype(np.float64)[:, None]
    col = (t % GRID_W).astype(np.float64)[:, None]
    half = HEAD_DIM // 2
    inv = 1.0 / (ROPE_BASE ** (np.arange(0, half, 2, dtype=np.float64) / half))
    j = np.arange(128) % HEAD_DIM
    ang = np.where((j < half)[None, :], row, col) * inv[j % (half // 2)][None, :]
    first = ((j % half) < half // 2)[None, :]
    cos, sin = np.cos(ang), np.sin(ang)
    return tuple(jnp.asarray(a, F32) for a in (cos, np.where(first, -sin, 0.0), np.where(first, 0.0, sin)))


def kernel(x_prompt, x_sample, cache_a_k, cache_a_v, cache_b_k, cache_b_v, state_c, state_d, c, c_ctx, norm1_g, norm2_g, norm3_g, w_mod, b_mod, ffn1_wg, ffn1_wu, ffn1_wd, ffn2_wg, ffn2_wu, ffn2_wd, w_in, w_out, a_qn, a_kn, a_sink, b_qn, b_kn, c_conv_w, c_conv_b, c_wa, c_ba, c_wx, c_bx, c_lambda, d_theta, d_norm_g):
    mod = _modulation(c_ctx, c, w_mod, b_mod)
    rope = _rope_tables()
    caches = tuple(t.reshape(DEC_BATCH, DEPTH, PAST_LEN, 128) for t in (cache_a_k, cache_a_v, cache_b_k, cache_b_v))
    mixer_params = (a_qn, a_kn, a_sink, b_qn, b_kn, c_conv_w, c_conv_b, c_wa, c_ba, c_wx, c_bx,
                    c_lambda, d_theta, d_norm_g)
    xs = (x_prompt.reshape(N_CTX_ROWS, D_MODEL), x_sample.reshape(N_LAT_ROWS, D_MODEL))
    states = ()
    for l in range(DEPTH):
        (x,) = _ffn(xs, mod, l, 0, norm1_g, ffn1_wg, ffn1_wu, ffn1_wd)
        xn, states = _ctx_mixers(x, mod, l, states, norm2_g, w_in, w_out, *mixer_params)
        xn = _lat_mixers(xn, x, mod, l, caches, state_c, state_d, rope, norm2_g, w_in, w_out, *mixer_params)
        xs = _ffn((xn,), mod, l, 6, norm3_g, ffn2_wg, ffn2_wu, ffn2_wd, split_out=(l == DEPTH - 1))
    y_p, y_s = xs
    ka, va, kb, vb, st_c, st_d = states
    kv_shape = (BATCH, DEPTH, SEQ, 2, HEAD_DIM)
    return (y_p.reshape(BATCH, SEQ, D_MODEL), y_s.reshape(DEC_BATCH, DEC_SEQ, D_MODEL),
            ka.reshape(kv_shape), va.reshape(kv_shape), kb.reshape(kv_shape), vb.reshape(kv_shape),
            st_c, st_d)
```

```python
import functools
import math

import numpy as np
import jax
import jax.numpy as jnp
from jax import lax
from jax.experimental import pallas as pl
from jax.experimental.pallas import tpu as pltpu

F32 = jnp.float32
BF16 = jnp.bfloat16

D_MODEL = 1024
BATCH = 16
SEQ = 256
DEPTH = 2
DEC_BATCH = 2
DEC_SEQ = 1024
PAST_LEN = 512
GRID_W = 64
HEAD_DIM = 64
HEAD_SHIFT = 6
N_HEADS = 4
GROUP_W = 256
WINDOW = 128
ATT_BLOCK = 128
ROPE_BASE = 10000.0
LRU_C = 8.0
D_FF = 2816
N_MOD = 9
EPS = 1e-6
NEG_INF = -1e30
IN_WIDTH = 2560

N_CTX_ROWS = BATCH * SEQ
N_LAT_ROWS = DEC_BATCH * DEC_SEQ
N_ROWS = N_CTX_ROWS + N_LAT_ROWS
MOD_ROWS = 8
MOD_GROUP = 1024

VMEM_LIMIT_BYTES = 56 * 1024 * 1024

COL_AQ, COL_AK, COL_AV = 0, 256, 384
COL_BQ, COL_BK, COL_BV = 512, 768, 896
COL_CX, COL_CY = 1024, 1280
COL_DQ, COL_DK, COL_DV, COL_DG = 1536, 1792, 2048, 2304


def _cparams(*sem):
    return pltpu.CompilerParams(dimension_semantics=sem, vmem_limit_bytes=VMEM_LIMIT_BYTES)


def _dot(a, b):
    return jnp.dot(a, b, preferred_element_type=F32)


def _dot_nt(a, b):
    return lax.dot_general(a, b, (((1,), (1,)), ((), ())), preferred_element_type=F32)


def _dot_tn(a, b):
    return lax.dot_general(a, b, (((0,), (0,)), ((), ())), preferred_element_type=F32)


def _silu(x):
    return x * jax.nn.sigmoid(x)


def _gelu_tanh(x):
    return 0.5 * x * (1.0 + jnp.tanh(math.sqrt(2.0 / math.pi) * (x + 0.044715 * (x * x * x))))


def _mod_row(i, tm, s):
    if tm >= MOD_GROUP:
        block_index = i * (tm // MOD_GROUP) + s
    else:
        block_index = i >> int(math.log2(MOD_GROUP // tm))
    return jnp.maximum(block_index - (N_CTX_ROWS // MOD_GROUP - 1), 0)


def _norm_mod(x, g, sc, sh):
    ms = jnp.mean(x * x, axis=-1, keepdims=True)
    return (x * lax.rsqrt(ms + EPS) * g) * (1.0 + sc) + sh


def _full(shape):
    return pl.BlockSpec(shape, lambda *_: (0,) * len(shape))


def _layer_block(shape, layer):
    return pl.BlockSpec((None,) + shape, lambda *_: (layer,) + (0,) * len(shape))


MOD_TN = 1024


def _mod_kernel(cc_ref, c_ref, w_ref, b_ref, o_ref):
    l = pl.program_id(0)
    pad = jnp.zeros((MOD_ROWS - 1 - DEC_BATCH, D_MODEL), F32)
    cond = jnp.concatenate([cc_ref[...], c_ref[...], pad], axis=0)
    o_ref[...] = _dot(_silu(cond).astype(BF16), w_ref[...].astype(BF16)) + b_ref[pl.ds(l, 1), :]


def _modulation(c_ctx, c, w_mod, b_mod):
    n = N_MOD * D_MODEL
    return pl.pallas_call(
        _mod_kernel,
        grid=(DEPTH, n // MOD_TN),
        in_specs=[
            pl.BlockSpec((1, D_MODEL), lambda l, j: (0, 0)),
            pl.BlockSpec((DEC_BATCH, D_MODEL), lambda l, j: (0, 0)),
            pl.BlockSpec((None, D_MODEL, MOD_TN), lambda l, j: (l, 0, j)),
            pl.BlockSpec((DEPTH, MOD_TN), lambda l, j: (0, j)),
        ],
        out_specs=pl.BlockSpec((None, MOD_ROWS, MOD_TN), lambda l, j: (l, 0, j)),
        out_shape=jax.ShapeDtypeStruct((DEPTH, MOD_ROWS, n), F32),
        compiler_params=_cparams("arbitrary", "arbitrary"),
        name="modulation",
    )(c_ctx.reshape(1, D_MODEL), c, w_mod, b_mod)


FFN_TM = 1024
FFN_TF = 256
N_CTX_TILES = N_CTX_ROWS // FFN_TM


FFN_NJ = D_FF // FFN_TF
N_FFN_TILES = N_ROWS // FFN_TM
N_FFN_STEPS = FFN_NJ + N_FFN_TILES - 1


def _ffn_tile(step):
    return jnp.maximum(step - (FFN_NJ - 1), 0)


def _on_stream_part(tile, x_refs, o_refs, fn):
    if len(x_refs) == 1 and len(o_refs) == 1:
        fn(x_refs[0], o_refs[0])
    else:
        pl.when(tile < N_CTX_TILES)(lambda: fn(x_refs[0], o_refs[0]))
        pl.when(tile >= N_CTX_TILES)(lambda: fn(x_refs[-1], o_refs[-1]))


def _ffn_kernel(*refs, layer, n_in, n_out):
    x_refs = refs[:n_in]
    n_ref, sh_ref, sc_ref, g_ref, wg_ref, wu_ref, wd_ref = refs[n_in:n_in + 7]
    o_refs = refs[n_in + 7:n_in + 7 + n_out]
    h_ref, a_ref, wg_s, wu_s, wd_s = refs[n_in + 7 + n_out:]
    nj, tf = FFN_NJ, FFN_TF
    s = pl.program_id(0)
    tile = _ffn_tile(s)
    r = _mod_row(tile, FFN_TM, 0)

    def load_tile():
        def init(x_ref, _):
            h = _norm_mod(x_ref[...], n_ref[layer:layer + 1, :], sc_ref[pl.ds(r, 1), :], sh_ref[pl.ds(r, 1), :])
            h_ref[...] = h.astype(BF16)
        _on_stream_part(tile, x_refs, o_refs, init)

    def up_chunk(j, cols):
        h = h_ref[...]
        a_ref[:, cols] = (_silu(_dot(h, wg_s[j])) * _dot(h, wu_s[j])).astype(BF16)

    def down_and_store():
        y = (0.5 * g_ref[pl.ds(r, 1), :]) * _dot(a_ref[...], wd_s[...])

        def store(x_ref, o_ref):
            o_ref[...] = x_ref[...] + y
        _on_stream_part(tile, x_refs, o_refs, store)

    @pl.when(s == 0)
    def _():
        load_tile()

    @pl.when(s < nj)
    def _():
        lo = pl.multiple_of(s * tf, tf)
        wg_s[s] = wg_ref[...].astype(BF16)
        wu_s[s] = wu_ref[...].astype(BF16)
        wd_s[pl.ds(lo, tf), :] = wd_ref[...].astype(BF16)
        up_chunk(s, pl.ds(lo, tf))

    @pl.when(s == nj - 1)
    def _():
        down_and_store()

    @pl.when(s >= nj)
    def _():
        load_tile()
        for j in range(nj):
            up_chunk(j, slice(j * tf, (j + 1) * tf))
        down_and_store()


def _stream_specs(split, buffered_once):
    tm = FFN_TM
    kw = {"pipeline_mode": pl.Buffered(1)} if buffered_once else {}
    if not split:
        return [pl.BlockSpec((tm, D_MODEL), lambda s: (_ffn_tile(s), 0), **kw)]
    last_ctx = N_CTX_TILES - 1
    return [pl.BlockSpec((tm, D_MODEL), lambda s: (jnp.minimum(_ffn_tile(s), last_ctx), 0), **kw),
            pl.BlockSpec((tm, D_MODEL), lambda s: (jnp.maximum(_ffn_tile(s) - N_CTX_TILES, 0), 0), **kw)]


def _ffn(xs, mod, layer, chunk0, norm_g, wg, wu, wd, split_out=False):
    tm, tf, nj = FFN_TM, FFN_TF, FFN_NJ
    split_in = len(xs) == 2
    mod_spec = lambda c: pl.BlockSpec((None, MOD_ROWS, D_MODEL), lambda s: (layer, 0, c))
    w_col = lambda s: (layer, 0, jnp.minimum(s, nj - 1))
    w_row = lambda s: (layer, jnp.minimum(s, nj - 1), 0)
    if split_out:
        out_shape = [jax.ShapeDtypeStruct((N_CTX_ROWS, D_MODEL), F32),
                     jax.ShapeDtypeStruct((N_LAT_ROWS, D_MODEL), F32)]
    else:
        out_shape = [jax.ShapeDtypeStruct((N_ROWS, D_MODEL), F32)]
    out = pl.pallas_call(
        functools.partial(_ffn_kernel, layer=layer, n_in=len(xs), n_out=len(out_shape)),
        grid=(N_FFN_STEPS,),
        in_specs=_stream_specs(split_in, split_in) + [
            _full((DEPTH, D_MODEL)),
            mod_spec(chunk0), mod_spec(chunk0 + 1), mod_spec(chunk0 + 2),
            pl.BlockSpec((None, D_MODEL, tf), w_col),
            pl.BlockSpec((None, D_MODEL, tf), w_col),
            pl.BlockSpec((None, tf, D_MODEL), w_row),
        ],
        out_specs=_stream_specs(split_out, True),
        out_shape=out_shape,
        scratch_shapes=[pltpu.VMEM((tm, D_MODEL), BF16),
                        pltpu.VMEM((tm, D_FF), BF16),
                        pltpu.VMEM((nj, D_MODEL, tf), BF16),
                        pltpu.VMEM((nj, D_MODEL, tf), BF16),
                        pltpu.VMEM((D_FF, D_MODEL), BF16)],
        compiler_params=_cparams("arbitrary"),
        name="ffn",
    )(*xs, norm_g, mod, mod, mod, wg, wu, wd)
    return tuple(out)


def _once(shape, index_map):
    return pl.BlockSpec(shape, index_map, pipeline_mode=pl.Buffered(1))


def _mod_chunk(layer, c):
    return pl.BlockSpec((None, MOD_ROWS, D_MODEL), lambda *_: (layer, 0, c))


def _project_in(x, n_row, sc_row, sh_row, w_bf16):
    return _dot(_norm_mod(x, n_row, sc_row, sh_row).astype(BF16), w_bf16)

def _head_mean_square(x):
    n = x.shape[-1]
    x2 = x * x
    hi = x2.astype(BF16)
    lo = (x2 - hi.astype(F32)).astype(BF16)
    r = lax.broadcasted_iota(jnp.int32, (n, n), 0) >> HEAD_SHIFT
    c = lax.broadcasted_iota(jnp.int32, (n, n), 1) >> HEAD_SHIFT
    ones_bd = jnp.where(r == c, 1.0, 0.0).astype(BF16)
    return (_dot(hi, ones_bd) + _dot(lo, ones_bd)) * (1.0 / HEAD_DIM)


def _head_norm(x, head_gain):
    gain_row = jnp.concatenate([head_gain] * (x.shape[-1] // HEAD_DIM), axis=-1)
    return x * lax.rsqrt(_head_mean_square(x) + EPS) * gain_row


def _head_cols(x, h):
    return x[:, h * HEAD_DIM:(h + 1) * HEAD_DIM].astype(BF16)


def _rope(x, cos, sin_lo, sin_hi):
    cols = []
    for c in range(x.shape[-1] // 128):
        xc = x[:, c * 128:(c + 1) * 128]
        cols.append(xc * cos + pltpu.roll(xc, 112, 1) * sin_lo + pltpu.roll(xc, 16, 1) * sin_hi)
    return cols[0] if len(cols) == 1 else jnp.concatenate(cols, axis=-1)


def _softmax_pv(scores, values, sink):
    m = jnp.max(scores[0], axis=-1, keepdims=True)
    for s in scores[1:]:
        m = jnp.maximum(m, jnp.max(s, axis=-1, keepdims=True))
    if sink is not None:
        m = jnp.maximum(m, sink)
    denom = None
    acc = None
    for s, v in zip(scores, values):
        p = jnp.exp(s - m)
        d = jnp.sum(p, axis=-1, keepdims=True)
        o = _dot(p.astype(BF16), v)
        denom = d if denom is None else denom + d
        acc = o if acc is None else acc + o
    if sink is not None:
        denom = denom + jnp.exp(sink - m)
    return acc / denom


def _block_diag(blocks):
    n = len(blocks)
    w = blocks[0].shape[0]
    rows = []
    for k, blk in enumerate(blocks):
        parts = []
        if k > 0:
            parts.append(jnp.zeros((w, k * w), F32))
        parts.append(blk)
        if k < n - 1:
            parts.append(jnp.zeros((w, (n - 1 - k) * w), F32))
        rows.append(jnp.concatenate(parts, axis=-1))
    return jnp.concatenate(rows, axis=0).astype(BF16)


def _rglru_gates(xc, wa, ba, wx, bx, lam):
    xb = xc.astype(BF16)
    r = jax.nn.sigmoid(_dot(xb, wa) + ba)
    i = jax.nn.sigmoid(_dot(xb, wx) + bx)
    softplus = jnp.maximum(-lam, 0.0) + jnp.log1p(jnp.exp(-jnp.abs(lam)))
    log_a = (-LRU_C) * r * softplus
    a = jnp.exp(log_a)
    b = jnp.sqrt(-jnp.tanh(log_a) * (a * a + 1.0)) * (i * xc)
    return a, b


def _block_prefix(a, b, reverse):
    t = a.shape[0]
    row = lax.broadcasted_iota(jnp.int32, a.shape, 0) & 7
    for d in (1, 2, 4):
        if reverse:
            a_s = pltpu.roll(a, t - d, 0)
            b_s = pltpu.roll(b, t - d, 0)
            ok = row < 8 - d
        else:
            a_s = pltpu.roll(a, d, 0)
            b_s = pltpu.roll(b, d, 0)
            ok = row >= d
        b = jnp.where(ok, a * b_s + b, b)
        a = jnp.where(ok, a * a_s, a)
    return a, b


def _conv4(x, w_ref, b_row):
    t = x.shape[0]
    row = lax.broadcasted_iota(jnp.int32, x.shape, 0)
    xm2 = jnp.where(row >= 2, pltpu.roll(x, 2, 0), 0.0)
    xm1 = jnp.where(row >= 1, pltpu.roll(x, 1, 0), 0.0)
    xp1 = jnp.where(row < t - 1, pltpu.roll(x, t - 1, 0), 0.0)
    return (xm2 * w_ref[0:1, :] + xm1 * w_ref[1:2, :] + x * w_ref[2:3, :] + xp1 * w_ref[3:4, :]) + b_row


def _rglru_mixer(cx, cy, conv_w_ref, conv_b, gate_w_ref, ba_ref, bx_ref, lam_ref, h0f, h0b,
                 af_ref, bf_ref, ab_ref, bb_ref, hf_ref, hb_ref):
    t = cx.shape[0]
    xc = _conv4(cx, conv_w_ref, conv_b)
    a, b = _rglru_gates(xc, gate_w_ref[0], ba_ref[0:1, :], gate_w_ref[1], bx_ref[0:1, :], lam_ref[0:1, :])
    a, b = _block_prefix(a, b, reverse=False)
    af_ref[...] = a
    bf_ref[...] = b
    a, b = _rglru_gates(xc, gate_w_ref[2], ba_ref[1:2, :], gate_w_ref[3], bx_ref[1:2, :], lam_ref[1:2, :])
    a, b = _block_prefix(a, b, reverse=True)
    ab_ref[...] = a
    bb_ref[...] = b
    nblk = t // 8

    def body(k, carry):
        cf, cb = carry
        rf = pl.ds(pl.multiple_of(k * 8, 8), 8)
        hf = bf_ref[rf, :] + af_ref[rf, :] * cf
        hf_ref[rf, :] = hf
        rb = pl.ds(pl.multiple_of((nblk - 1 - k) * 8, 8), 8)
        hb = bb_ref[rb, :] + ab_ref[rb, :] * cb
        hb_ref[rb, :] = hb
        return hf[7:8, :], hb[0:1, :]

    cf, cb = lax.fori_loop(0, nblk, body, (h0f, h0b))
    oc = (hf_ref[...] + hb_ref[...]) * _gelu_tanh(cy)
    return oc, cf, cb


def _store_gate_weights(gate_w_ref, wa_ref, wx_ref):
    for d in range(2):
        gate_w_ref[2 * d] = _block_diag([wa_ref[d, n] for n in range(N_HEADS)])
        gate_w_ref[2 * d + 1] = _block_diag([wx_ref[d, n] for n in range(N_HEADS)])


def _lane_head_masks(n):
    lane = lax.broadcasted_iota(jnp.int32, (1, n), 1) >> HEAD_SHIFT
    return [jnp.where(lane == h, 1.0, 0.0) for h in range(n // HEAD_DIM)]


def _log_decays(theta_ref, masks):
    theta = theta_ref[...]
    lanes = theta[:, 0:1] * masks[0]
    for h in range(1, N_HEADS):
        lanes = lanes + theta[:, h:h + 1] * masks[h]
    lg = jnp.log1p(-jnp.exp(lanes))
    return lg[0:1, :], lg[1:2, :]


def _retention_block(q, k8b, vb, q0, lgf, lgb, masks):
    tq, w = q.shape
    s_len = k8b.shape[0]
    nh = w // HEAD_DIM
    q_stack = jnp.concatenate([(q * masks[h]).astype(BF16) for h in range(nh)], axis=0)
    raw = _dot_nt(q_stack, k8b)
    qi = lax.broadcasted_iota(jnp.int32, (tq, s_len), 0) + q0
    si = lax.broadcasted_iota(jnp.int32, (tq, s_len), 1)
    rel = (qi - si).astype(F32)
    decs = []
    for h in range(nh):
        gf = lgf[:, h * HEAD_DIM:h * HEAD_DIM + 1]
        gb = lgb[:, h * HEAD_DIM:h * HEAD_DIM + 1]
        e = jnp.exp(jnp.where(rel >= 0, gf * rel, gb * (-rel)))
        decs.append(jnp.where(rel == 0, 2.0, e))
    inner = (raw * jnp.concatenate(decs, axis=0)).astype(BF16)
    out = _dot(inner, vb)
    o = out[0:tq, :] * masks[0]
    for h in range(1, nh):
        o = o + out[h * tq:(h + 1) * tq, :] * masks[h]
    return o


def _ctx_mixer_kernel(*refs, layer, n_alias):
    refs = refs[n_alias:]
    (x_ref, n2_ref, sh_ref, sc_ref, g2_ref, win_ref, wout_ref,
     aqn_ref, akn_ref, bqn_ref, bkn_ref, sink_ref,
     convw_ref, convb_ref, wa_ref, ba_ref, wx_ref, bx_ref, lam_ref, theta_ref, dn_ref,
     xn_ref, ka_ref, va_ref, kb_ref, vb_ref, stc_ref, std_ref,
     win_s, wout_s, u_ref, mixed_ref,
     gate_w_ref, af_ref, bf_ref, ab_ref, bb_ref, hf_ref, hb_ref) = refs
    t = SEQ
    lrow = slice(layer, layer + 1)

    @pl.when(pl.program_id(0) == 0)
    def _():
        for c in range(IN_WIDTH // 512):
            win_s[:, c * 512:(c + 1) * 512] = win_ref[:, c * 512:(c + 1) * 512].astype(BF16)
        wout_s[...] = wout_ref[...].astype(BF16)
        _store_gate_weights(gate_w_ref, wa_ref, wx_ref)

    x = x_ref[...]
    u_ref[...] = _project_in(x, n2_ref[lrow, :], sc_ref[0:1, :], sh_ref[0:1, :], win_s[...])

    for (cq, ck, cv, qn_ref, kn_ref, k_out, v_out, col0, use_sink) in (
            (COL_AQ, COL_AK, COL_AV, aqn_ref, akn_ref, ka_ref, va_ref, 0, True),
            (COL_BQ, COL_BK, COL_BV, bqn_ref, bkn_ref, kb_ref, vb_ref, GROUP_W, False)):
        q = _head_norm(u_ref[:, cq:cq + 256], qn_ref[lrow, :])
        k = _head_norm(u_ref[:, ck:ck + 128], kn_ref[lrow, :])
        v = u_ref[:, cv:cv + 128]
        k_out[...] = k
        v_out[...] = v
        qs = q * (HEAD_DIM ** -0.5)
        heads = []
        for h in range(N_HEADS):
            kv = h // 2
            s = _dot_nt(_head_cols(qs, h), _head_cols(k, kv))
            sink = jnp.full((t, 1), sink_ref[layer, h], F32) if use_sink else None
            heads.append(_softmax_pv([s], [_head_cols(v, kv)], sink))
        mixed_ref[:, col0:col0 + GROUP_W] = jnp.concatenate(heads, axis=-1).astype(BF16)

    zero = jnp.zeros((1, GROUP_W), F32)
    oc, cf, cb = _rglru_mixer(u_ref[:, COL_CX:COL_CX + 256], u_ref[:, COL_CY:COL_CY + 256],
                              convw_ref, convb_ref[lrow, :], gate_w_ref, ba_ref, bx_ref, lam_ref,
                              zero, zero, af_ref, bf_ref, ab_ref, bb_ref, hf_ref, hb_ref)
    mixed_ref[:, 2 * GROUP_W:3 * GROUP_W] = oc.astype(BF16)
    stc_ref[0:1, :] = cf
    stc_ref[1:2, :] = cb

    masks = _lane_head_masks(GROUP_W)
    lgf, lgb = _log_decays(theta_ref, masks)
    q = u_ref[:, COL_DQ:COL_DQ + 256]
    k8 = u_ref[:, COL_DK:COL_DK + 256] * (HEAD_DIM ** -0.5)
    vb = u_ref[:, COL_DV:COL_DV + 256].astype(BF16)
    o = _retention_block(q, k8.astype(BF16), vb, 0, lgf, lgb, masks)
    dn = dn_ref[lrow, :]
    o = o * lax.rsqrt(_head_mean_square(o) + EPS) * dn * _silu(u_ref[:, COL_DG:COL_DG + 256])
    mixed_ref[:, 3 * GROUP_W:4 * GROUP_W] = o.astype(BF16)
    pos = lax.broadcasted_iota(jnp.int32, (t, GROUP_W), 0).astype(F32)
    for d, (lg, expo) in enumerate(((lgf, (t - 1.0) - pos), (lgb, pos))):
        s_full = _dot_tn((k8 * jnp.exp(lg * expo)).astype(BF16), vb)
        for h in range(N_HEADS):
            std_ref[d, h] = s_full[h * 64:(h + 1) * 64, h * 64:(h + 1) * 64]

    xn_ref[...] = x + g2_ref[0:1, :] * _dot(mixed_ref[...], wout_s[...])


def _ctx_mixers(x, mod, layer, prev, norm2_g, w_in, w_out,
                a_qn, a_kn, a_sink, b_qn, b_kn, c_conv_w, c_conv_b, c_wa, c_ba, c_wx, c_bx,
                c_lambda, d_theta, d_norm_g):
    kv_out = pl.BlockSpec((None, None, SEQ, 128), lambda b: (b, layer, 0, 0))
    kv_shape = jax.ShapeDtypeStruct((BATCH, DEPTH, SEQ, 128), F32)
    scr = pltpu.VMEM((SEQ, GROUP_W), F32)
    n_alias = len(prev)
    out = pl.pallas_call(
        functools.partial(_ctx_mixer_kernel, layer=layer, n_alias=n_alias),
        grid=(BATCH,),
        in_specs=[pl.BlockSpec(memory_space=pl.ANY)] * n_alias + [
            pl.BlockSpec((SEQ, D_MODEL), lambda b: (b, 0)),
            _full((DEPTH, D_MODEL)),
            _mod_chunk(layer, 3), _mod_chunk(layer, 4), _mod_chunk(layer, 5),
            _once((None, D_MODEL, IN_WIDTH), lambda b: (layer, 0, 0)),
            _once((None, D_MODEL, D_MODEL), lambda b: (layer, 0, 0)),
            _full((DEPTH, HEAD_DIM)), _full((DEPTH, HEAD_DIM)), _full((DEPTH, HEAD_DIM)), _full((DEPTH, HEAD_DIM)),
            pl.BlockSpec(memory_space=pltpu.SMEM),
            _layer_block((4, GROUP_W), layer), _full((DEPTH, GROUP_W)),
            _layer_block((2, N_HEADS, HEAD_DIM, HEAD_DIM), layer), _layer_block((2, GROUP_W), layer),
            _layer_block((2, N_HEADS, HEAD_DIM, HEAD_DIM), layer), _layer_block((2, GROUP_W), layer),
            _layer_block((2, GROUP_W), layer),
            _layer_block((2, N_HEADS), layer), _full((DEPTH, GROUP_W)),
        ],
        out_specs=[
            pl.BlockSpec((SEQ, D_MODEL), lambda b: (b, 0)),
            kv_out, kv_out, kv_out, kv_out,
            pl.BlockSpec((None, None, 2, GROUP_W), lambda b: (b, layer, 0, 0)),
            pl.BlockSpec((None, None, 2, N_HEADS, HEAD_DIM, HEAD_DIM), lambda b: (b, layer, 0, 0, 0, 0)),
        ],
        out_shape=[
            jax.ShapeDtypeStruct((N_ROWS, D_MODEL), F32),
            kv_shape, kv_shape, kv_shape, kv_shape,
            jax.ShapeDtypeStruct((BATCH, DEPTH, 2, GROUP_W), F32),
            jax.ShapeDtypeStruct((BATCH, DEPTH, 2, N_HEADS, HEAD_DIM, HEAD_DIM), F32),
        ],
        input_output_aliases={k: k + 1 for k in range(n_alias)},
        scratch_shapes=[pltpu.VMEM((D_MODEL, IN_WIDTH), BF16), pltpu.VMEM((D_MODEL, D_MODEL), BF16),
                        pltpu.VMEM((SEQ, IN_WIDTH), F32), pltpu.VMEM((SEQ, D_MODEL), BF16),
                        pltpu.VMEM((4, GROUP_W, GROUP_W), BF16)] + [scr] * 6,
        compiler_params=_cparams("arbitrary"),
        name="ctx_mixers",
    )(*prev, x, norm2_g, mod, mod, mod, w_in, w_out,
      a_qn, a_kn, b_qn, b_kn, a_sink, c_conv_w, c_conv_b, c_wa, c_ba, c_wx, c_bx,
      c_lambda, d_theta, d_norm_g)
    return out[0], tuple(out[1:])


LAT_BLOCK0 = N_CTX_ROWS // DEC_SEQ


def _lat_attn_kernel(xn_in_ref, x_ref, n2_ref, sh_ref, sc_ref, g2_ref, win_ref, wout_ref,
                     kca_ref, vca_ref, kcb_ref, vcb_ref,
                     aqn_ref, akn_ref, bqn_ref, bkn_ref, sink_ref, cos_ref, sinl_ref, sinh_ref,
                     xn_ref, u_ref, o_ref, *, layer):
    del xn_in_ref
    t = DEC_SEQ
    lrow = slice(layer, layer + 1)
    mrow = pl.ds(1 + pl.program_id(0), 1)
    cos, sin_lo, sin_hi = cos_ref[...], sinl_ref[...], sinh_ref[...]
    scale = HEAD_DIM ** -0.5
    x = x_ref[...]
    u_ref[...] = _project_in(x, n2_ref[lrow, :], sc_ref[mrow, :], sh_ref[mrow, :], win_ref[...].astype(BF16))

    q = _rope(_head_norm(u_ref[:, COL_AQ:COL_AQ + 256], aqn_ref[lrow, :]), cos, sin_lo, sin_hi)
    k = _rope(_head_norm(u_ref[:, COL_AK:COL_AK + 128], akn_ref[lrow, :]), cos, sin_lo, sin_hi)
    qh = [_head_cols(q * scale, h) for h in range(4)]
    v = u_ref[:, COL_AV:COL_AV + 128]
    kh = [_head_cols(k, kv) for kv in range(2)]
    vh = [_head_cols(v, kv) for kv in range(2)]
    kch = [_head_cols(kca_ref[...], kv) for kv in range(2)]
    vch = [_head_cols(vca_ref[...], kv) for kv in range(2)]
    w = ATT_BLOCK
    span = 3 * w
    for n in range(t // w):
        start = min(max((n - 1) * w, 0), t - span)
        rows = slice(n * w, (n + 1) * w)
        band = slice(start, start + span)
        qpos = (lax.broadcasted_iota(jnp.int32, (2 * w, span), 0) & (w - 1)) + n * w
        kpos = lax.broadcasted_iota(jnp.int32, (2 * w, span), 1) + start
        valid = jnp.abs(qpos - kpos) <= WINDOW
        heads = []
        for kv in range(2):
            qp = jnp.concatenate([qh[2 * kv][rows, :], qh[2 * kv + 1][rows, :]], axis=0)
            s_ctx = _dot_nt(qp, kch[kv])
            s_band = jnp.where(valid, _dot_nt(qp, kh[kv][band, :]), NEG_INF)
            row = lax.broadcasted_iota(jnp.int32, (2 * w, 1), 0)
            sink = jnp.where(row < w, sink_ref[layer, 2 * kv], sink_ref[layer, 2 * kv + 1])
            o = _softmax_pv([s_ctx, s_band], [vch[kv], vh[kv][band, :]], sink)
            heads += [o[0:w, :], o[w:2 * w, :]]
        o_ref[rows, 0:GROUP_W] = jnp.concatenate(heads, axis=-1).astype(BF16)

    q = _rope(_head_norm(u_ref[:, COL_BQ:COL_BQ + 256], bqn_ref[lrow, :]), cos, sin_lo, sin_hi)
    k = _rope(_head_norm(u_ref[:, COL_BK:COL_BK + 128], bkn_ref[lrow, :]), cos, sin_lo, sin_hi)
    qh = [_head_cols(q * scale, h) for h in range(4)]
    v = u_ref[:, COL_BV:COL_BV + 128]
    kh = [_head_cols(k, kv) for kv in range(2)]
    vh = [_head_cols(v, kv) for kv in range(2)]
    kch = [_head_cols(kcb_ref[...], kv) for kv in range(2)]
    vch = [_head_cols(vcb_ref[...], kv) for kv in range(2)]
    tq = 256
    for n in range(t // tq):
        rows = slice(n * tq, (n + 1) * tq)
        heads = []
        for kv in range(2):
            qp = jnp.concatenate([qh[2 * kv][rows, :], qh[2 * kv + 1][rows, :]], axis=0)
            o = _softmax_pv([_dot_nt(qp, kch[kv]), _dot_nt(qp, kh[kv])], [vch[kv], vh[kv]], None)
            heads += [o[0:tq, :], o[tq:2 * tq, :]]
        o_ref[rows, GROUP_W:2 * GROUP_W] = jnp.concatenate(heads, axis=-1).astype(BF16)

    xn_ref[...] = x + g2_ref[mrow, :] * _dot(o_ref[...], wout_ref[...].astype(BF16))


def _lat_rglru_kernel(xn_in_ref, x_ref, n2_ref, sh_ref, sc_ref, g2_ref, win_ref, wout_ref, h0_ref,
                      convw_ref, convb_ref, wa_ref, ba_ref, wx_ref, bx_ref, lam_ref,
                      xn_ref, gate_w_ref, af_ref, bf_ref, ab_ref, bb_ref, hf_ref, hb_ref, *, layer):
    lrow = slice(layer, layer + 1)
    mrow = pl.ds(1 + pl.program_id(0), 1)

    @pl.when(pl.program_id(0) == 0)
    def _():
        _store_gate_weights(gate_w_ref, wa_ref, wx_ref)

    u = _project_in(x_ref[...], n2_ref[lrow, :], sc_ref[mrow, :], sh_ref[mrow, :], win_ref[...].astype(BF16))
    oc, _, _ = _rglru_mixer(u[:, 0:GROUP_W], u[:, GROUP_W:2 * GROUP_W], convw_ref, convb_ref[lrow, :],
                            gate_w_ref, ba_ref, bx_ref, lam_ref,
                            h0_ref[0:1, :], h0_ref[1:2, :],
                            af_ref, bf_ref, ab_ref, bb_ref, hf_ref, hb_ref)
    xn_ref[...] = xn_in_ref[...] + g2_ref[mrow, :] * _dot(oc.astype(BF16), wout_ref[...].astype(BF16))


def _lat_retention_kernel(xn_in_ref, x_ref, n2_ref, sh_ref, sc_ref, g2_ref, wqk_ref, wvg_ref, wout_ref,
                          s0_ref, theta_ref, dn_ref, xn_ref, o_ref, *, layer):
    t = DEC_SEQ
    lrow = slice(layer, layer + 1)
    mrow = pl.ds(1 + pl.program_id(0), 1)
    h = _norm_mod(x_ref[...], n2_ref[lrow, :], sc_ref[mrow, :], sh_ref[mrow, :]).astype(BF16)
    uqk = _dot(h, wqk_ref[...].astype(BF16))
    uvg = _dot(h, wvg_ref[...].astype(BF16))
    masks = _lane_head_masks(GROUP_W)
    lgf, lgb = _log_decays(theta_ref, masks)
    k8b = (uqk[:, GROUP_W:2 * GROUP_W] * (HEAD_DIM ** -0.5)).astype(BF16)
    vb = uvg[:, 0:GROUP_W].astype(BF16)
    s0f = _block_diag([s0_ref[0, hd] for hd in range(N_HEADS)])
    s0b = _block_diag([s0_ref[1, hd] for hd in range(N_HEADS)])
    dn = dn_ref[lrow, :]
    tq = 256
    for n in range(t // tq):
        rows = slice(n * tq, (n + 1) * tq)
        q = uqk[rows, 0:GROUP_W]
        o = _retention_block(q, k8b, vb, n * tq, lgf, lgb, masks)
        pos = lax.broadcasted_iota(jnp.int32, (tq, GROUP_W), 0).astype(F32) + float(n * tq)
        o = o + _dot((q * jnp.exp(lgf * (pos + 1.0))).astype(BF16), s0f)
        o = o + _dot((q * jnp.exp(lgb * (float(t) - pos))).astype(BF16), s0b)
        o = o * lax.rsqrt(_head_mean_square(o) + EPS) * dn * _silu(uvg[rows, GROUP_W:2 * GROUP_W])
        o_ref[rows, :] = o.astype(BF16)
    xn_ref[...] = xn_in_ref[...] + g2_ref[mrow, :] * _dot(o_ref[...], wout_ref[...].astype(BF16))


def _lat_mixers(xn, x, mod, layer, caches, state_c, state_d, rope, norm2_g, w_in, w_out,
                a_qn, a_kn, a_sink, b_qn, b_kn, c_conv_w, c_conv_b, c_wa, c_ba, c_wx, c_bx,
                c_lambda, d_theta, d_norm_g):
    rows = pl.BlockSpec((DEC_SEQ, D_MODEL), lambda b: (LAT_BLOCK0 + b, 0))
    cache_spec = pl.BlockSpec((None, None, PAST_LEN, 128), lambda b: (b, layer, 0, 0))
    gain = _full((DEPTH, HEAD_DIM))
    out_shape = jax.ShapeDtypeStruct((N_ROWS, D_MODEL), F32)
    win_cols = lambda w, c: _once((None, D_MODEL, w), lambda b: (layer, 0, c))
    wout_rows = lambda h, r: _once((None, h, D_MODEL), lambda b: (layer, r, 0))
    common = [rows, _full((DEPTH, D_MODEL)), _mod_chunk(layer, 3), _mod_chunk(layer, 4), _mod_chunk(layer, 5)]
    common_args = (x, norm2_g, mod, mod, mod)

    xn = pl.pallas_call(
        functools.partial(_lat_attn_kernel, layer=layer),
        grid=(DEC_BATCH,),
        in_specs=[pl.BlockSpec(memory_space=pl.ANY)] + common + [
            win_cols(4 * GROUP_W, 0), wout_rows(2 * GROUP_W, 0),
            cache_spec, cache_spec, cache_spec, cache_spec,
            gain, gain, gain, gain,
            pl.BlockSpec(memory_space=pltpu.SMEM),
            _full((DEC_SEQ, 128)), _full((DEC_SEQ, 128)), _full((DEC_SEQ, 128))],
        out_specs=rows,
        out_shape=out_shape,
        input_output_aliases={0: 0},
        scratch_shapes=[pltpu.VMEM((DEC_SEQ, 4 * GROUP_W), F32), pltpu.VMEM((DEC_SEQ, 2 * GROUP_W), BF16)],
        compiler_params=_cparams("arbitrary"),
        name="lat_attention",
    )(xn, *common_args, w_in, w_out, *caches, a_qn, a_kn, b_qn, b_kn, a_sink, *rope)

    scr = pltpu.VMEM((DEC_SEQ, GROUP_W), F32)
    xn = pl.pallas_call(
        functools.partial(_lat_rglru_kernel, layer=layer),
        grid=(DEC_BATCH,),
        in_specs=[rows] + common + [
            win_cols(2 * GROUP_W, COL_CX // (2 * GROUP_W)), wout_rows(GROUP_W, 2),
            pl.BlockSpec((None, None, 2, GROUP_W), lambda b: (b, layer, 0, 0)),
            _layer_block((4, GROUP_W), layer), _full((DEPTH, GROUP_W)),
            _layer_block((2, N_HEADS, HEAD_DIM, HEAD_DIM), layer), _layer_block((2, GROUP_W), layer),
            _layer_block((2, N_HEADS, HEAD_DIM, HEAD_DIM), layer), _layer_block((2, GROUP_W), layer),
            _layer_block((2, GROUP_W), layer)],
        out_specs=rows,
        out_shape=out_shape,
        input_output_aliases={0: 0},
        scratch_shapes=[pltpu.VMEM((4, GROUP_W, GROUP_W), BF16)] + [scr] * 6,
        compiler_params=_cparams("arbitrary"),
        name="lat_rglru",
    )(xn, *common_args, w_in, w_out, state_c, c_conv_w, c_conv_b, c_wa, c_ba, c_wx, c_bx, c_lambda)

    xn = pl.pallas_call(
        functools.partial(_lat_retention_kernel, layer=layer),
        grid=(DEC_BATCH,),
        in_specs=[rows] + common + [
            win_cols(2 * GROUP_W, COL_DQ // (2 * GROUP_W)), win_cols(2 * GROUP_W, COL_DV // (2 * GROUP_W)),
            wout_rows(GROUP_W, 3),
            pl.BlockSpec((None, None, 2, N_HEADS, HEAD_DIM, HEAD_DIM), lambda b: (b, layer, 0, 0, 0, 0)),
            _layer_block((2, N_HEADS), layer), _full((DEPTH, GROUP_W))],
        out_specs=rows,
        out_shape=out_shape,
        input_output_aliases={0: 0},
        scratch_shapes=[pltpu.VMEM((DEC_SEQ, GROUP_W), BF16)],
        compiler_params=_cparams("arbitrary"),
        name="lat_retention",
    )(xn, *common_args, w_in, w_in, w_out, state_d, d_theta, d_norm_g)
    return xn


def _rope_tables():
    t = np.arange(DEC_SEQ)
    row = (t // GRID_W).astype(np.float64)[:, None]
    col = (t % GRID_W).astype(np.float64)[:, None]
    half = HEAD_DIM // 2
    inv = 1.0 / (ROPE_BASE ** (np.arange(0, half, 2, dtype=np.float64) / half))
    j = np.arange(128) % HEAD_DIM
    ang = np.where((j < half)[None, :], row, col) * inv[j % (half // 2)][None, :]
    first = ((j % half) < half // 2)[None, :]
    cos, sin = np.cos(ang), np.sin(ang)
    return tuple(jnp.asarray(a, F32) for a in (cos, np.where(first, -sin, 0.0), np.where(first, 0.0, sin)))


def kernel(x_prompt, x_sample, cache_a_k, cache_a_v, cache_b_k, cache_b_v, state_c, state_d, c, c_ctx, norm1_g, norm2_g, norm3_g, w_mod, b_mod, ffn1_wg, ffn1_wu, ffn1_wd, ffn2_wg, ffn2_wu, ffn2_wd, w_in, w_out, a_qn, a_kn, a_sink, b_qn, b_kn, c_conv_w, c_conv_b, c_wa, c_ba, c_wx, c_bx, c_lambda, d_theta, d_norm_g):
    mod = _modulation(c_ctx, c, w_mod, b_mod)
    rope = _rope_tables()
    caches = tuple(t.reshape(DEC_BATCH, DEPTH, PAST_LEN, 128) for t in (cache_a_k, cache_a_v, cache_b_k, cache_b_v))
    mixer_params = (a_qn, a_kn, a_sink, b_qn, b_kn, c_conv_w, c_conv_b, c_wa, c_ba, c_wx, c_bx,
                    c_lambda, d_theta, d_norm_g)
    xs = (x_prompt.reshape(N_CTX_ROWS, D_MODEL), x_sample.reshape(N_LAT_ROWS, D_MODEL))
    states = ()
    for l in range(DEPTH):
        (x,) = _ffn(xs, mod, l, 0, norm1_g, ffn1_wg, ffn1_wu, ffn1_wd)
        xn, states = _ctx_mixers(x, mod, l, states, norm2_g, w_in, w_out, *mixer_params)
        xn = _lat_mixers(xn, x, mod, l, caches, state_c, state_d, rope, norm2_g, w_in, w_out, *mixer_params)
        xs = _ffn((xn,), mod, l, 6, norm3_g, ffn2_wg, ffn2_wu, ffn2_wd, split_out=(l == DEPTH - 1))
    y_p, y_s = xs
    ka, va, kb, vb, st_c, st_d = states
    kv_shape = (BATCH, DEPTH, SEQ, 2, HEAD_DIM)
    return (y_p.reshape(BATCH, SEQ, D_MODEL), y_s.reshape(DEC_BATCH, DEC_SEQ, D_MODEL),
            ka.reshape(kv_shape), va.reshape(kv_shape), kb.reshape(kv_shape), vb.reshape(kv_shape),
            st_c, st_d)
```

```python
import functools
import math

import numpy as np
import jax
import jax.numpy as jnp
from jax import lax
from jax.experimental import pallas as pl
from jax.experimental.pallas import tpu as pltpu

F32 = jnp.float32
BF16 = jnp.bfloat16

D_MODEL = 1024
BATCH = 16
SEQ = 256
DEPTH = 2
DEC_BATCH = 2
DEC_SEQ = 1024
PAST_LEN = 512
GRID_W = 64
HEAD_DIM = 64
HEAD_SHIFT = 6
N_HEADS = 4
GROUP_W = 256
WINDOW = 128
ATT_BLOCK = 128
ROPE_BASE = 10000.0
LRU_C = 8.0
D_FF = 2816
N_MOD = 9
EPS = 1e-6
NEG_INF = -1e30
IN_WIDTH = 2560

N_CTX_ROWS = BATCH * SEQ
N_LAT_ROWS = DEC_BATCH * DEC_SEQ
N_ROWS = N_CTX_ROWS + N_LAT_ROWS
MOD_ROWS = 8
MOD_GROUP = 1024

VMEM_LIMIT_BYTES = 56 * 1024 * 1024

COL_AQ, COL_AK, COL_AV = 0, 256, 384
COL_BQ, COL_BK, COL_BV = 512, 768, 896
COL_CX, COL_CY = 1024, 1280
COL_DQ, COL_DK, COL_DV, COL_DG = 1536, 1792, 2048, 2304


def _cparams(*sem):
    return pltpu.CompilerParams(dimension_semantics=sem, vmem_limit_bytes=VMEM_LIMIT_BYTES)


def _dot(a, b):
    return jnp.dot(a, b, preferred_element_type=F32)


def _dot_nt(a, b):
    return lax.dot_general(a, b, (((1,), (1,)), ((), ())), preferred_element_type=F32)


def _dot_tn(a, b):
    return lax.dot_general(a, b, (((0,), (0,)), ((), ())), preferred_element_type=F32)


def _silu(x):
    return x * jax.nn.sigmoid(x)


def _gelu_tanh(x):
    return 0.5 * x * (1.0 + jnp.tanh(math.sqrt(2.0 / math.pi) * (x + 0.044715 * (x * x * x))))


def _mod_row(i, tm, s):
    if tm >= MOD_GROUP:
        block_index = i * (tm // MOD_GROUP) + s
    else:
        block_index = i >> int(math.log2(MOD_GROUP // tm))
    return jnp.maximum(block_index - (N_CTX_ROWS // MOD_GROUP - 1), 0)


def _norm_mod(x, g, sc, sh):
    ms = jnp.mean(x * x, axis=-1, keepdims=True)
    return (x * lax.rsqrt(ms + EPS) * g) * (1.0 + sc) + sh


def _full(shape):
    return pl.BlockSpec(shape, lambda *_: (0,) * len(shape))


def _layer_block(shape, layer):
    return pl.BlockSpec((None,) + shape, lambda *_: (layer,) + (0,) * len(shape))


MOD_TN = 1024


def _mod_kernel(cc_ref, c_ref, w_ref, b_ref, o_ref):
    l = pl.program_id(0)
    pad = jnp.zeros((MOD_ROWS - 1 - DEC_BATCH, D_MODEL), F32)
    cond = jnp.concatenate([cc_ref[...], c_ref[...], pad], axis=0)
    o_ref[...] = _dot(_silu(cond).astype(BF16), w_ref[...].astype(BF16)) + b_ref[pl.ds(l, 1), :]


def _modulation(c_ctx, c, w_mod, b_mod):
    n = N_MOD * D_MODEL
    return pl.pallas_call(
        _mod_kernel,
        grid=(DEPTH, n // MOD_TN),
        in_specs=[
            pl.BlockSpec((1, D_MODEL), lambda l, j: (0, 0)),
            pl.BlockSpec((DEC_BATCH, D_MODEL), lambda l, j: (0, 0)),
            pl.BlockSpec((None, D_MODEL, MOD_TN), lambda l, j: (l, 0, j)),
            pl.BlockSpec((DEPTH, MOD_TN), lambda l, j: (0, j)),
        ],
        out_specs=pl.BlockSpec((None, MOD_ROWS, MOD_TN), lambda l, j: (l, 0, j)),
        out_shape=jax.ShapeDtypeStruct((DEPTH, MOD_ROWS, n), F32),
        compiler_params=_cparams("arbitrary", "arbitrary"),
        name="modulation",
    )(c_ctx.reshape(1, D_MODEL), c, w_mod, b_mod)


FFN_TM = 1024
FFN_TF = 256
N_CTX_TILES = N_CTX_ROWS // FFN_TM


FFN_NJ = D_FF // FFN_TF
N_FFN_TILES = N_ROWS // FFN_TM
N_FFN_STEPS = FFN_NJ + N_FFN_TILES - 1


def _ffn_tile(step):
    return jnp.maximum(step - (FFN_NJ - 1), 0)


def _on_stream_part(tile, x_refs, o_refs, fn):
    if len(x_refs) == 1 and len(o_refs) == 1:
        fn(x_refs[0], o_refs[0])
    else:
        pl.when(tile < N_CTX_TILES)(lambda: fn(x_refs[0], o_refs[0]))
        pl.when(tile >= N_CTX_TILES)(lambda: fn(x_refs[-1], o_refs[-1]))


def _ffn_kernel(*refs, layer, n_in, n_out):
    x_refs = refs[:n_in]
    n_ref, sh_ref, sc_ref, g_ref, wg_ref, wu_ref, wd_ref = refs[n_in:n_in + 7]
    o_refs = refs[n_in + 7:n_in + 7 + n_out]
    h_ref, a_ref, wg_s, wu_s, wd_s = refs[n_in + 7 + n_out:]
    nj, tf = FFN_NJ, FFN_TF
    s = pl.program_id(0)
    tile = _ffn_tile(s)
    r = _mod_row(tile, FFN_TM, 0)

    def load_tile():
        def init(x_ref, _):
            h = _norm_mod(x_ref[...], n_ref[layer:layer + 1, :], sc_ref[pl.ds(r, 1), :], sh_ref[pl.ds(r, 1), :])
            h_ref[...] = h.astype(BF16)
        _on_stream_part(tile, x_refs, o_refs, init)

    def up_chunk(j, cols):
        h = h_ref[...]
        a_ref[:, cols] = (_silu(_dot(h, wg_s[j])) * _dot(h, wu_s[j])).astype(BF16)

    def down_and_store():
        y = (0.5 * g_ref[pl.ds(r, 1), :]) * _dot(a_ref[...], wd_s[...])

        def store(x_ref, o_ref):
            o_ref[...] = x_ref[...] + y
        _on_stream_part(tile, x_refs, o_refs, store)

    @pl.when(s == 0)
    def _():
        load_tile()

    @pl.when(s < nj)
    def _():
        lo = pl.multiple_of(s * tf, tf)
        wg_s[s] = wg_ref[...].astype(BF16)
        wu_s[s] = wu_ref[...].astype(BF16)
        wd_s[pl.ds(lo, tf), :] = wd_ref[...].astype(BF16)
        up_chunk(s, pl.ds(lo, tf))

    @pl.when(s == nj - 1)
    def _():
        down_and_store()

    @pl.when(s >= nj)
    def _():
        load_tile()
        for j in range(nj):
            up_chunk(j, slice(j * tf, (j + 1) * tf))
        down_and_store()


def _stream_specs(split, buffered_once):
    tm = FFN_TM
    kw = {"pipeline_mode": pl.Buffered(1)} if buffered_once else {}
    if not split:
        return [pl.BlockSpec((tm, D_MODEL), lambda s: (_ffn_tile(s), 0), **kw)]
    last_ctx = N_CTX_TILES - 1
    return [pl.BlockSpec((tm, D_MODEL), lambda s: (jnp.minimum(_ffn_tile(s), last_ctx), 0), **kw),
            pl.BlockSpec((tm, D_MODEL), lambda s: (jnp.maximum(_ffn_tile(s) - N_CTX_TILES, 0), 0), **kw)]


def _ffn(xs, mod, layer, chunk0, norm_g, wg, wu, wd, split_out=False):
    tm, tf, nj = FFN_TM, FFN_TF, FFN_NJ
    split_in = len(xs) == 2
    mod_spec = lambda c: pl.BlockSpec((None, MOD_ROWS, D_MODEL), lambda s: (layer, 0, c))
    w_col = lambda s: (layer, 0, jnp.minimum(s, nj - 1))
    w_row = lambda s: (layer, jnp.minimum(s, nj - 1), 0)
    if split_out:
        out_shape = [jax.ShapeDtypeStruct((N_CTX_ROWS, D_MODEL), F32),
                     jax.ShapeDtypeStruct((N_LAT_ROWS, D_MODEL), F32)]
    else:
        out_shape = [jax.ShapeDtypeStruct((N_ROWS, D_MODEL), F32)]
    out = pl.pallas_call(
        functools.partial(_ffn_kernel, layer=layer, n_in=len(xs), n_out=len(out_shape)),
        grid=(N_FFN_STEPS,),
        in_specs=_stream_specs(split_in, split_in) + [
            _full((DEPTH, D_MODEL)),
            mod_spec(chunk0), mod_spec(chunk0 + 1), mod_spec(chunk0 + 2),
            pl.BlockSpec((None, D_MODEL, tf), w_col),
            pl.BlockSpec((None, D_MODEL, tf), w_col),
            pl.BlockSpec((None, tf, D_MODEL), w_row),
        ],
        out_specs=_stream_specs(split_out, True),
        out_shape=out_shape,
        scratch_shapes=[pltpu.VMEM((tm, D_MODEL), BF16),
                        pltpu.VMEM((tm, D_FF), BF16),
                        pltpu.VMEM((nj, D_MODEL, tf), BF16),
                        pltpu.VMEM((nj, D_MODEL, tf), BF16),
                        pltpu.VMEM((D_FF, D_MODEL), BF16)],
        compiler_params=_cparams("arbitrary"),
        name="ffn",
    )(*xs, norm_g, mod, mod, mod, wg, wu, wd)
    return tuple(out)


def _once(shape, index_map):
    return pl.BlockSpec(shape, index_map, pipeline_mode=pl.Buffered(1))


def _mod_chunk(layer, c):
    return pl.BlockSpec((None, MOD_ROWS, D_MODEL), lambda *_: (layer, 0, c))


def _project_in(x, n_row, sc_row, sh_row, w_bf16):
    return _dot(_norm_mod(x, n_row, sc_row, sh_row).astype(BF16), w_bf16)

def _head_mean_square(x):
    n = x.shape[-1]
    x2 = x * x
    hi = x2.astype(BF16)
    lo = (x2 - hi.astype(F32)).astype(BF16)
    r = lax.broadcasted_iota(jnp.int32, (n, n), 0) >> HEAD_SHIFT
    c = lax.broadcasted_iota(jnp.int32, (n, n), 1) >> HEAD_SHIFT
    ones_bd = jnp.where(r == c, 1.0, 0.0).astype(BF16)
    return (_dot(hi, ones_bd) + _dot(lo, ones_bd)) * (1.0 / HEAD_DIM)


def _head_norm(x, head_gain):
    gain_row = jnp.concatenate([head_gain] * (x.shape[-1] // HEAD_DIM), axis=-1)
    return x * lax.rsqrt(_head_mean_square(x) + EPS) * gain_row


def _head_cols(x, h):
    return x[:, h * HEAD_DIM:(h + 1) * HEAD_DIM].astype(BF16)


def _rope(x, cos, sin_lo, sin_hi):
    cols = []
    for c in range(x.shape[-1] // 128):
        xc = x[:, c * 128:(c + 1) * 128]
        cols.append(xc * cos + pltpu.roll(xc, 112, 1) * sin_lo + pltpu.roll(xc, 16, 1) * sin_hi)
    return cols[0] if len(cols) == 1 else jnp.concatenate(cols, axis=-1)


def _softmax_pv(scores, values, sink):
    m = jnp.max(scores[0], axis=-1, keepdims=True)
    for s in scores[1:]:
        m = jnp.maximum(m, jnp.max(s, axis=-1, keepdims=True))
    if sink is not None:
        m = jnp.maximum(m, sink)
    denom = None
    acc = None
    for s, v in zip(scores, values):
        p = jnp.exp(s - m)
        d = jnp.sum(p, axis=-1, keepdims=True)
        o = _dot(p.astype(BF16), v)
        denom = d if denom is None else denom + d
        acc = o if acc is None else acc + o
    if sink is not None:
        denom = denom + jnp.exp(sink - m)
    return acc / denom


def _block_diag(blocks):
    n = len(blocks)
    w = blocks[0].shape[0]
    rows = []
    for k, blk in enumerate(blocks):
        parts = []
        if k > 0:
            parts.append(jnp.zeros((w, k * w), F32))
        parts.append(blk)
        if k < n - 1:
            parts.append(jnp.zeros((w, (n - 1 - k) * w), F32))
        rows.append(jnp.concatenate(parts, axis=-1))
    return jnp.concatenate(rows, axis=0).astype(BF16)


def _rglru_gates(xc, wa, ba, wx, bx, lam):
    xb = xc.astype(BF16)
    r = jax.nn.sigmoid(_dot(xb, wa) + ba)
    i = jax.nn.sigmoid(_dot(xb, wx) + bx)
    softplus = jnp.maximum(-lam, 0.0) + jnp.log1p(jnp.exp(-jnp.abs(lam)))
    log_a = (-LRU_C) * r * softplus
    a = jnp.exp(log_a)
    b = jnp.sqrt(-jnp.tanh(log_a) * (a * a + 1.0)) * (i * xc)
    return a, b


def _block_prefix(a, b, reverse):
    t = a.shape[0]
    row = lax.broadcasted_iota(jnp.int32, a.shape, 0) & 7
    for d in (1, 2, 4):
        if reverse:
            a_s = pltpu.roll(a, t - d, 0)
            b_s = pltpu.roll(b, t - d, 0)
            ok = row < 8 - d
        else:
            a_s = pltpu.roll(a, d, 0)
            b_s = pltpu.roll(b, d, 0)
            ok = row >= d
        b = jnp.where(ok, a * b_s + b, b)
        a = jnp.where(ok, a * a_s, a)
    return a, b


def _conv4(x, w_ref, b_row):
    t = x.shape[0]
    row = lax.broadcasted_iota(jnp.int32, x.shape, 0)
    xm2 = jnp.where(row >= 2, pltpu.roll(x, 2, 0), 0.0)
    xm1 = jnp.where(row >= 1, pltpu.roll(x, 1, 0), 0.0)
    xp1 = jnp.where(row < t - 1, pltpu.roll(x, t - 1, 0), 0.0)
    return (xm2 * w_ref[0:1, :] + xm1 * w_ref[1:2, :] + x * w_ref[2:3, :] + xp1 * w_ref[3:4, :]) + b_row


def _rglru_mixer(cx, cy, conv_w_ref, conv_b, gate_w_ref, ba_ref, bx_ref, lam_ref, h0f, h0b,
                 af_ref, bf_ref, ab_ref, bb_ref, hf_ref, hb_ref):
    t = cx.shape[0]
    xc = _conv4(cx, conv_w_ref, conv_b)
    a, b = _rglru_gates(xc, gate_w_ref[0], ba_ref[0:1, :], gate_w_ref[1], bx_ref[0:1, :], lam_ref[0:1, :])
    a, b = _block_prefix(a, b, reverse=False)
    af_ref[...] = a
    bf_ref[...] = b
    a, b = _rglru_gates(xc, gate_w_ref[2], ba_ref[1:2, :], gate_w_ref[3], bx_ref[1:2, :], lam_ref[1:2, :])
    a, b = _block_prefix(a, b, reverse=True)
    ab_ref[...] = a
    bb_ref[...] = b
    nblk = t // 8

    def body(k, carry):
        cf, cb = carry
        rf = pl.ds(pl.multiple_of(k * 8, 8), 8)
        hf = bf_ref[rf, :] + af_ref[rf, :] * cf
        hf_ref[rf, :] = hf
        rb = pl.ds(pl.multiple_of((nblk - 1 - k) * 8, 8), 8)
        hb = bb_ref[rb, :] + ab_ref[rb, :] * cb
        hb_ref[rb, :] = hb
        return hf[7:8, :], hb[0:1, :]

    cf, cb = lax.fori_loop(0, nblk, body, (h0f, h0b))
    oc = (hf_ref[...] + hb_ref[...]) * _gelu_tanh(cy)
    return oc, cf, cb


def _store_gate_weights(gate_w_ref, wa_ref, wx_ref):
    for d in range(2):
        gate_w_ref[2 * d] = _block_diag([wa_ref[d, n] for n in range(N_HEADS)])
        gate_w_ref[2 * d + 1] = _block_diag([wx_ref[d, n] for n in range(N_HEADS)])


def _lane_head_masks(n):
    lane = lax.broadcasted_iota(jnp.int32, (1, n), 1) >> HEAD_SHIFT
    return [jnp.where(lane == h, 1.0, 0.0) for h in range(n // HEAD_DIM)]


def _log_decays(theta_ref, masks):
    theta = theta_ref[...]
    lanes = theta[:, 0:1] * masks[0]
    for h in range(1, N_HEADS):
        lanes = lanes + theta[:, h:h + 1] * masks[h]
    lg = jnp.log1p(-jnp.exp(lanes))
    return lg[0:1, :], lg[1:2, :]


def _retention_block(q, k8b, vb, q0, lgf, lgb, masks):
    tq, w = q.shape
    s_len = k8b.shape[0]
    nh = w // HEAD_DIM
    q_stack = jnp.concatenate([(q * masks[h]).astype(BF16) for h in range(nh)], axis=0)
    raw = _dot_nt(q_stack, k8b)
    qi = lax.broadcasted_iota(jnp.int32, (tq, s_len), 0) + q0
    si = lax.broadcasted_iota(jnp.int32, (tq, s_len), 1)
    rel = (qi - si).astype(F32)
    decs = []
    for h in range(nh):
        gf = lgf[:, h * HEAD_DIM:h * HEAD_DIM + 1]
        gb = lgb[:, h * HEAD_DIM:h * HEAD_DIM + 1]
        e = jnp.exp(jnp.where(rel >= 0, gf * rel, gb * (-rel)))
        decs.append(jnp.where(rel == 0, 2.0, e))
    inner = (raw * jnp.concatenate(decs, axis=0)).astype(BF16)
    out = _dot(inner, vb)
    o = out[0:tq, :] * masks[0]
    for h in range(1, nh):
        o = o + out[h * tq:(h + 1) * tq, :] * masks[h]
    return o


def _ctx_mixer_kernel(*refs, layer):
    (x_ref, n2_ref, sh_ref, sc_ref, g2_ref, win_ref, wout_ref,
     aqn_ref, akn_ref, bqn_ref, bkn_ref, sink_ref,
     convw_ref, convb_ref, wa_ref, ba_ref, wx_ref, bx_ref, lam_ref, theta_ref, dn_ref,
     xn_ref, ka_ref, va_ref, kb_ref, vb_ref, stc_ref, std_ref,
     win_s, wout_s, u_ref, mixed_ref,
     gate_w_ref, af_ref, bf_ref, ab_ref, bb_ref, hf_ref, hb_ref) = refs
    t = SEQ
    lrow = slice(layer, layer + 1)

    @pl.when(pl.program_id(0) == 0)
    def _():
        for c in range(IN_WIDTH // 512):
            win_s[:, c * 512:(c + 1) * 512] = win_ref[:, c * 512:(c + 1) * 512].astype(BF16)
        wout_s[...] = wout_ref[...].astype(BF16)
        _store_gate_weights(gate_w_ref, wa_ref, wx_ref)

    x = x_ref[...]
    u_ref[...] = _project_in(x, n2_ref[lrow, :], sc_ref[0:1, :], sh_ref[0:1, :], win_s[...])

    for (cq, ck, cv, qn_ref, kn_ref, k_out, v_out, col0, use_sink) in (
            (COL_AQ, COL_AK, COL_AV, aqn_ref, akn_ref, ka_ref, va_ref, 0, True),
            (COL_BQ, COL_BK, COL_BV, bqn_ref, bkn_ref, kb_ref, vb_ref, GROUP_W, False)):
        q = _head_norm(u_ref[:, cq:cq + 256], qn_ref[lrow, :])
        k = _head_norm(u_ref[:, ck:ck + 128], kn_ref[lrow, :])
        v = u_ref[:, cv:cv + 128]
        k_out[...] = k
        v_out[...] = v
        qs = q * (HEAD_DIM ** -0.5)
        heads = []
        for h in range(N_HEADS):
            kv = h // 2
            s = _dot_nt(_head_cols(qs, h), _head_cols(k, kv))
            sink = jnp.full((t, 1), sink_ref[layer, h], F32) if use_sink else None
            heads.append(_softmax_pv([s], [_head_cols(v, kv)], sink))
        mixed_ref[:, col0:col0 + GROUP_W] = jnp.concatenate(heads, axis=-1).astype(BF16)

    zero = jnp.zeros((1, GROUP_W), F32)
    oc, cf, cb = _rglru_mixer(u_ref[:, COL_CX:COL_CX + 256], u_ref[:, COL_CY:COL_CY + 256],
                              convw_ref, convb_ref[lrow, :], gate_w_ref, ba_ref, bx_ref, lam_ref,
                              zero, zero, af_ref, bf_ref, ab_ref, bb_ref, hf_ref, hb_ref)
    mixed_ref[:, 2 * GROUP_W:3 * GROUP_W] = oc.astype(BF16)
    stc_ref[0:1, :] = cf
    stc_ref[1:2, :] = cb

    masks = _lane_head_masks(GROUP_W)
    lgf, lgb = _log_decays(theta_ref, masks)
    q = u_ref[:, COL_DQ:COL_DQ + 256]
    k8 = u_ref[:, COL_DK:COL_DK + 256] * (HEAD_DIM ** -0.5)
    vb = u_ref[:, COL_DV:COL_DV + 256].astype(BF16)
    o = _retention_block(q, k8.astype(BF16), vb, 0, lgf, lgb, masks)
    dn = dn_ref[lrow, :]
    o = o * lax.rsqrt(_head_mean_square(o) + EPS) * dn * _silu(u_ref[:, COL_DG:COL_DG + 256])
    mixed_ref[:, 3 * GROUP_W:4 * GROUP_W] = o.astype(BF16)
    pos = lax.broadcasted_iota(jnp.int32, (t, GROUP_W), 0).astype(F32)
    for d, (lg, expo) in enumerate(((lgf, (t - 1.0) - pos), (lgb, pos))):
        s_full = _dot_tn((k8 * jnp.exp(lg * expo)).astype(BF16), vb)
        for h in range(N_HEADS):
            std_ref[d, h] = s_full[h * 64:(h + 1) * 64, h * 64:(h + 1) * 64]

    xn_ref[...] = x + g2_ref[0:1, :] * _dot(mixed_ref[...], wout_s[...])


def _ctx_mixers(x, mod, layer, norm2_g, w_in, w_out,
                a_qn, a_kn, a_sink, b_qn, b_kn, c_conv_w, c_conv_b, c_wa, c_ba, c_wx, c_bx,
                c_lambda, d_theta, d_norm_g):
    kv_out = pl.BlockSpec((None, SEQ, 128), lambda b: (b, 0, 0))
    kv_shape = jax.ShapeDtypeStruct((BATCH, SEQ, 128), F32)
    scr = pltpu.VMEM((SEQ, GROUP_W), F32)
    out = pl.pallas_call(
        functools.partial(_ctx_mixer_kernel, layer=layer),
        grid=(BATCH,),
        in_specs=[
            pl.BlockSpec((SEQ, D_MODEL), lambda b: (b, 0)),
            _full((DEPTH, D_MODEL)),
            _mod_chunk(layer, 3), _mod_chunk(layer, 4), _mod_chunk(layer, 5),
            _once((None, D_MODEL, IN_WIDTH), lambda b: (layer, 0, 0)),
            _once((None, D_MODEL, D_MODEL), lambda b: (layer, 0, 0)),
            _full((DEPTH, HEAD_DIM)), _full((DEPTH, HEAD_DIM)), _full((DEPTH, HEAD_DIM)), _full((DEPTH, HEAD_DIM)),
            pl.BlockSpec(memory_space=pltpu.SMEM),
            _layer_block((4, GROUP_W), layer), _full((DEPTH, GROUP_W)),
            _layer_block((2, N_HEADS, HEAD_DIM, HEAD_DIM), layer), _layer_block((2, GROUP_W), layer),
            _layer_block((2, N_HEADS, HEAD_DIM, HEAD_DIM), layer), _layer_block((2, GROUP_W), layer),
            _layer_block((2, GROUP_W), layer),
            _layer_block((2, N_HEADS), layer), _full((DEPTH, GROUP_W)),
        ],
        out_specs=[
            pl.BlockSpec((SEQ, D_MODEL), lambda b: (b, 0)),
            kv_out, kv_out, kv_out, kv_out,
            pl.BlockSpec((None, 2, GROUP_W), lambda b: (b, 0, 0)),
            pl.BlockSpec((None, 2, N_HEADS, HEAD_DIM, HEAD_DIM), lambda b: (b, 0, 0, 0, 0)),
        ],
        out_shape=[
            jax.ShapeDtypeStruct((N_ROWS, D_MODEL), F32),
            kv_shape, kv_shape, kv_shape, kv_shape,
            jax.ShapeDtypeStruct((BATCH, 2, GROUP_W), F32),
            jax.ShapeDtypeStruct((BATCH, 2, N_HEADS, HEAD_DIM, HEAD_DIM), F32),
        ],
        input_output_aliases={0: 0},
        scratch_shapes=[pltpu.VMEM((D_MODEL, IN_WIDTH), BF16), pltpu.VMEM((D_MODEL, D_MODEL), BF16),
                        pltpu.VMEM((SEQ, IN_WIDTH), F32), pltpu.VMEM((SEQ, D_MODEL), BF16),
                        pltpu.VMEM((4, GROUP_W, GROUP_W), BF16)] + [scr] * 6,
        compiler_params=_cparams("arbitrary"),
        name="ctx_mixers",
    )(x, norm2_g, mod, mod, mod, w_in, w_out,
      a_qn, a_kn, b_qn, b_kn, a_sink, c_conv_w, c_conv_b, c_wa, c_ba, c_wx, c_bx,
      c_lambda, d_theta, d_norm_g)
    return out[0], tuple(out[1:])


LAT_BLOCK0 = N_CTX_ROWS // DEC_SEQ


def _lat_attn_kernel(x_ref, n2_ref, sh_ref, sc_ref, g2_ref, win_ref, wout_ref,
                     kca_ref, vca_ref, kcb_ref, vcb_ref,
                     aqn_ref, akn_ref, bqn_ref, bkn_ref, sink_ref, cos_ref, sinl_ref, sinh_ref,
                     xn_ref, h_ref, u_ref, o_ref, *, layer):
    t = DEC_SEQ
    lrow = slice(layer, layer + 1)
    mrow = pl.ds(1 + pl.program_id(0), 1)
    cos, sin_lo, sin_hi = cos_ref[...], sinl_ref[...], sinh_ref[...]
    scale = HEAD_DIM ** -0.5
    x = x_ref[...]
    h_ref[...] = _norm_mod(x, n2_ref[lrow, :], sc_ref[mrow, :], sh_ref[mrow, :]).astype(BF16)
    u_ref[...] = _dot(h_ref[...], win_ref[...].astype(BF16))

    q = _rope(_head_norm(u_ref[:, COL_AQ:COL_AQ + 256], aqn_ref[lrow, :]), cos, sin_lo, sin_hi)
    k = _rope(_head_norm(u_ref[:, COL_AK:COL_AK + 128], akn_ref[lrow, :]), cos, sin_lo, sin_hi)
    qh = [_head_cols(q * scale, h) for h in range(4)]
    v = u_ref[:, COL_AV:COL_AV + 128]
    kh = [_head_cols(k, kv) for kv in range(2)]
    vh = [_head_cols(v, kv) for kv in range(2)]
    kch = [_head_cols(kca_ref[...], kv) for kv in range(2)]
    vch = [_head_cols(vca_ref[...], kv) for kv in range(2)]
    w = ATT_BLOCK
    span = 3 * w
    for n in range(t // w):
        start = min(max((n - 1) * w, 0), t - span)
        rows = slice(n * w, (n + 1) * w)
        band = slice(start, start + span)
        qpos = (lax.broadcasted_iota(jnp.int32, (2 * w, span), 0) & (w - 1)) + n * w
        kpos = lax.broadcasted_iota(jnp.int32, (2 * w, span), 1) + start
        valid = jnp.abs(qpos - kpos) <= WINDOW
        heads = []
        for kv in range(2):
            qp = jnp.concatenate([qh[2 * kv][rows, :], qh[2 * kv + 1][rows, :]], axis=0)
            s_ctx = _dot_nt(qp, kch[kv])
            s_band = jnp.where(valid, _dot_nt(qp, kh[kv][band, :]), NEG_INF)
            row = lax.broadcasted_iota(jnp.int32, (2 * w, 1), 0)
            sink = jnp.where(row < w, sink_ref[layer, 2 * kv], sink_ref[layer, 2 * kv + 1])
            o = _softmax_pv([s_ctx, s_band], [vch[kv], vh[kv][band, :]], sink)
            heads += [o[0:w, :], o[w:2 * w, :]]
        o_ref[rows, 0:GROUP_W] = jnp.concatenate(heads, axis=-1).astype(BF16)

    q = _rope(_head_norm(u_ref[:, COL_BQ:COL_BQ + 256], bqn_ref[lrow, :]), cos, sin_lo, sin_hi)
    k = _rope(_head_norm(u_ref[:, COL_BK:COL_BK + 128], bkn_ref[lrow, :]), cos, sin_lo, sin_hi)
    qh = [_head_cols(q * scale, h) for h in range(4)]
    v = u_ref[:, COL_BV:COL_BV + 128]
    kh = [_head_cols(k, kv) for kv in range(2)]
    vh = [_head_cols(v, kv) for kv in range(2)]
    kch = [_head_cols(kcb_ref[...], kv) for kv in range(2)]
    vch = [_head_cols(vcb_ref[...], kv) for kv in range(2)]
    tq = 256
    for n in range(t // tq):
        rows = slice(n * tq, (n + 1) * tq)
        heads = []
        for kv in range(2):
            qp = jnp.concatenate([qh[2 * kv][rows, :], qh[2 * kv + 1][rows, :]], axis=0)
            o = _softmax_pv([_dot_nt(qp, kch[kv]), _dot_nt(qp, kh[kv])], [vch[kv], vh[kv]], None)
            heads += [o[0:tq, :], o[tq:2 * tq, :]]
        o_ref[rows, GROUP_W:2 * GROUP_W] = jnp.concatenate(heads, axis=-1).astype(BF16)

    xn_ref[...] = x + g2_ref[mrow, :] * _dot(o_ref[...], wout_ref[...].astype(BF16))


def _lat_rglru_kernel(xn_in_ref, h_ref, g2_ref, win_ref, wout_ref, h0_ref,
                      convw_ref, convb_ref, wa_ref, ba_ref, wx_ref, bx_ref, lam_ref,
                      xn_ref, gate_w_ref, af_ref, bf_ref, ab_ref, bb_ref, hf_ref, hb_ref, *, layer):
    lrow = slice(layer, layer + 1)
    mrow = pl.ds(1 + pl.program_id(0), 1)

    @pl.when(pl.program_id(0) == 0)
    def _():
        _store_gate_weights(gate_w_ref, wa_ref, wx_ref)

    u = _dot(h_ref[...], win_ref[...].astype(BF16))
    oc, _, _ = _rglru_mixer(u[:, 0:GROUP_W], u[:, GROUP_W:2 * GROUP_W], convw_ref, convb_ref[lrow, :],
                            gate_w_ref, ba_ref, bx_ref, lam_ref,
                            h0_ref[0:1, :], h0_ref[1:2, :],
                            af_ref, bf_ref, ab_ref, bb_ref, hf_ref, hb_ref)
    xn_ref[...] = xn_in_ref[...] + g2_ref[mrow, :] * _dot(oc.astype(BF16), wout_ref[...].astype(BF16))


def _lat_retention_kernel(xn_in_ref, h_ref, g2_ref, wqk_ref, wvg_ref, wout_ref,
                          s0_ref, theta_ref, dn_ref, xn_ref, o_ref, *, layer):
    t = DEC_SEQ
    lrow = slice(layer, layer + 1)
    mrow = pl.ds(1 + pl.program_id(0), 1)
    h = h_ref[...]
    uqk = _dot(h, wqk_ref[...].astype(BF16))
    uvg = _dot(h, wvg_ref[...].astype(BF16))
    masks = _lane_head_masks(GROUP_W)
    lgf, lgb = _log_decays(theta_ref, masks)
    k8b = (uqk[:, GROUP_W:2 * GROUP_W] * (HEAD_DIM ** -0.5)).astype(BF16)
    vb = uvg[:, 0:GROUP_W].astype(BF16)
    s0f = _block_diag([s0_ref[0, hd] for hd in range(N_HEADS)])
    s0b = _block_diag([s0_ref[1, hd] for hd in range(N_HEADS)])
    dn = dn_ref[lrow, :]
    tq = 256
    for n in range(t // tq):
        rows = slice(n * tq, (n + 1) * tq)
        q = uqk[rows, 0:GROUP_W]
        o = _retention_block(q, k8b, vb, n * tq, lgf, lgb, masks)
        pos = lax.broadcasted_iota(jnp.int32, (tq, GROUP_W), 0).astype(F32) + float(n * tq)
        o = o + _dot((q * jnp.exp(lgf * (pos + 1.0))).astype(BF16), s0f)
        o = o + _dot((q * jnp.exp(lgb * (float(t) - pos))).astype(BF16), s0b)
        o = o * lax.rsqrt(_head_mean_square(o) + EPS) * dn * _silu(uvg[rows, GROUP_W:2 * GROUP_W])
        o_ref[rows, :] = o.astype(BF16)
    xn_ref[...] = xn_in_ref[...] + g2_ref[mrow, :] * _dot(o_ref[...], wout_ref[...].astype(BF16))


def _lat_mixers(x, mod, layer, caches, state_c, state_d, rope, norm2_g, w_in, w_out,
                a_qn, a_kn, a_sink, b_qn, b_kn, c_conv_w, c_conv_b, c_wa, c_ba, c_wx, c_bx,
                c_lambda, d_theta, d_norm_g):
    rows = pl.BlockSpec((DEC_SEQ, D_MODEL), lambda b: (LAT_BLOCK0 + b, 0))
    h_rows = pl.BlockSpec((DEC_SEQ, D_MODEL), lambda b: (b, 0))
    cache_spec = pl.BlockSpec((None, None, PAST_LEN, 128), lambda b: (b, layer, 0, 0))
    gain = _full((DEPTH, HEAD_DIM))
    table = _once((DEC_SEQ, 128), lambda b: (0, 0))
    out_shape = jax.ShapeDtypeStruct((N_ROWS, D_MODEL), F32)
    win_cols = lambda w, c: _once((None, D_MODEL, w), lambda b: (layer, 0, c))
    wout_rows = lambda h, r: _once((None, h, D_MODEL), lambda b: (layer, r, 0))

    xn, h = pl.pallas_call(
        functools.partial(_lat_attn_kernel, layer=layer),
        grid=(DEC_BATCH,),
        in_specs=[rows, _full((DEPTH, D_MODEL)),
                  _mod_chunk(layer, 3), _mod_chunk(layer, 4), _mod_chunk(layer, 5),
                  win_cols(4 * GROUP_W, 0), wout_rows(2 * GROUP_W, 0),
                  cache_spec, cache_spec, cache_spec, cache_spec,
                  gain, gain, gain, gain,
                  pl.BlockSpec(memory_space=pltpu.SMEM),
                  table, table, table],
        out_specs=[rows, pl.BlockSpec((DEC_SEQ, D_MODEL), lambda b: (b, 0), pipeline_mode=pl.Buffered(1))],
        out_shape=[out_shape, jax.ShapeDtypeStruct((N_LAT_ROWS, D_MODEL), BF16)],
        input_output_aliases={0: 0},
        scratch_shapes=[pltpu.VMEM((DEC_SEQ, 4 * GROUP_W), F32), pltpu.VMEM((DEC_SEQ, 2 * GROUP_W), BF16)],
        compiler_params=_cparams("arbitrary"),
        name="lat_attention",
    )(x, norm2_g, mod, mod, mod, w_in, w_out, *caches, a_qn, a_kn, b_qn, b_kn, a_sink, *rope)

    common = [rows, h_rows, _mod_chunk(layer, 5)]
    scr = pltpu.VMEM((DEC_SEQ, GROUP_W), F32)
    xn = pl.pallas_call(
        functools.partial(_lat_rglru_kernel, layer=layer),
        grid=(DEC_BATCH,),
        in_specs=common + [
            win_cols(2 * GROUP_W, COL_CX // (2 * GROUP_W)), wout_rows(GROUP_W, 2),
            pl.BlockSpec((None, None, 2, GROUP_W), lambda b: (b, layer, 0, 0)),
            _layer_block((4, GROUP_W), layer), _full((DEPTH, GROUP_W)),
            _layer_block((2, N_HEADS, HEAD_DIM, HEAD_DIM), layer), _layer_block((2, GROUP_W), layer),
            _layer_block((2, N_HEADS, HEAD_DIM, HEAD_DIM), layer), _layer_block((2, GROUP_W), layer),
            _layer_block((2, GROUP_W), layer)],
        out_specs=rows,
        out_shape=out_shape,
        input_output_aliases={0: 0},
        scratch_shapes=[pltpu.VMEM((4, GROUP_W, GROUP_W), BF16)] + [scr] * 6,
        compiler_params=_cparams("arbitrary"),
        name="lat_rglru",
    )(xn, h, mod, w_in, w_out, state_c, c_conv_w, c_conv_b, c_wa, c_ba, c_wx, c_bx, c_lambda)

    xn = pl.pallas_call(
        functools.partial(_lat_retention_kernel, layer=layer),
        grid=(DEC_BATCH,),
        in_specs=common + [
            win_cols(2 * GROUP_W, COL_DQ // (2 * GROUP_W)), win_cols(2 * GROUP_W, COL_DV // (2 * GROUP_W)),
            wout_rows(GROUP_W, 3),
            pl.BlockSpec((None, None, 2, N_HEADS, HEAD_DIM, HEAD_DIM), lambda b: (b, layer, 0, 0, 0, 0)),
            _layer_block((2, N_HEADS), layer), _full((DEPTH, GROUP_W))],
        out_specs=rows,
        out_shape=out_shape,
        input_output_aliases={0: 0},
        scratch_shapes=[pltpu.VMEM((DEC_SEQ, GROUP_W), BF16)],
        compiler_params=_cparams("arbitrary"),
        name="lat_retention",
    )(xn, h, mod, w_in, w_in, w_out, state_d, d_theta, d_norm_g)
    return xn


def _rope_tables():
    t = np.arange(DEC_SEQ)
    row = (t // GRID_W).astype(np.float64)[:, None]
    col = (t % GRID_W).astype(np.float64)[:, None]
    half = HEAD_DIM // 2
    inv = 1.0 / (ROPE_BASE ** (np.arange(0, half, 2, dtype=np.float64) / half))
    j = np.arange(128) % HEAD_DIM
    ang = np.where((j < half)[None, :], row, col) * inv[j % (half // 2)][None, :]
    first = ((j % half) < half // 2)[None, :]
    cos, sin = np.cos(ang), np.sin(ang)
    return tuple(jnp.asarray(a, F32) for a in (cos, np.where(first, -sin, 0.0), np.where(first, 0.0, sin)))


def kernel(x_prompt, x_sample, cache_a_k, cache_a_v, cache_b_k, cache_b_v, state_c, state_d, c, c_ctx, norm1_g, norm2_g, norm3_g, w_mod, b_mod, ffn1_wg, ffn1_wu, ffn1_wd, ffn2_wg, ffn2_wu, ffn2_wd, w_in, w_out, a_qn, a_kn, a_sink, b_qn, b_kn, c_conv_w, c_conv_b, c_wa, c_ba, c_wx, c_bx, c_lambda, d_theta, d_norm_g):
    mod = _modulation(c_ctx, c, w_mod, b_mod)
    rope = _rope_tables()
    caches = tuple(t.reshape(DEC_BATCH, DEPTH, PAST_LEN, 128) for t in (cache_a_k, cache_a_v, cache_b_k, cache_b_v))
    mixer_params = (a_qn, a_kn, a_sink, b_qn, b_kn, c_conv_w, c_conv_b, c_wa, c_ba, c_wx, c_bx,
                    c_lambda, d_theta, d_norm_g)
    xs = (x_prompt.reshape(N_CTX_ROWS, D_MODEL), x_sample.reshape(N_LAT_ROWS, D_MODEL))
    states = []
    for l in range(DEPTH):
        (x,) = _ffn(xs, mod, l, 0, norm1_g, ffn1_wg, ffn1_wu, ffn1_wd)
        x, layer_states = _ctx_mixers(x, mod, l, norm2_g, w_in, w_out, *mixer_params)
        x = _lat_mixers(x, mod, l, caches, state_c, state_d, rope, norm2_g, w_in, w_out, *mixer_params)
        xs = _ffn((x,), mod, l, 6, norm3_g, ffn2_wg, ffn2_wu, ffn2_wd, split_out=(l == DEPTH - 1))
        states.append(layer_states)
    y_p, y_s = xs
    ka, va, kb, vb, st_c, st_d = (jnp.stack(per_layer, axis=1) for per_layer in zip(*states))
    kv_shape = (BATCH, DEPTH, SEQ, 2, HEAD_DIM)
    return (y_p.reshape(BATCH, SEQ, D_MODEL), y_s.reshape(DEC_BATCH, DEC_SEQ, D_MODEL),
            ka.reshape(kv_shape), va.reshape(kv_shape), kb.reshape(kv_shape), vb.reshape(kv_shape),
            st_c, st_d)
```

```python
import functools
import math

import numpy as np
import jax
import jax.numpy as jnp
from jax import lax
from jax.experimental import pallas as pl
from jax.experimental.pallas import tpu as pltpu

F32 = jnp.float32
BF16 = jnp.bfloat16

D_MODEL = 1024
BATCH = 16
SEQ = 256
DEPTH = 2
DEC_BATCH = 2
DEC_SEQ = 1024
PAST_LEN = 512
GRID_W = 64
HEAD_DIM = 64
HEAD_SHIFT = 6
N_HEADS = 4
GROUP_W = 256
WINDOW = 128
ATT_BLOCK = 128
ROPE_BASE = 10000.0
LRU_C = 8.0
D_FF = 2816
N_MOD = 9
EPS = 1e-6
NEG_INF = -1e30
IN_WIDTH = 2560

N_CTX_ROWS = BATCH * SEQ
N_LAT_ROWS = DEC_BATCH * DEC_SEQ
N_ROWS = N_CTX_ROWS + N_LAT_ROWS
MOD_ROWS = 8
MOD_GROUP = 1024

VMEM_LIMIT_BYTES = 56 * 1024 * 1024

COL_AQ, COL_AK, COL_AV = 0, 256, 384
COL_BQ, COL_BK, COL_BV = 512, 768, 896
COL_CX, COL_CY = 1024, 1280
COL_DQ, COL_DK, COL_DV, COL_DG = 1536, 1792, 2048, 2304


def _cparams(*sem):
    return pltpu.CompilerParams(dimension_semantics=sem, vmem_limit_bytes=VMEM_LIMIT_BYTES)


def _dot(a, b):
    return jnp.dot(a, b, preferred_element_type=F32)


def _dot_nt(a, b):
    return lax.dot_general(a, b, (((1,), (1,)), ((), ())), preferred_element_type=F32)


def _dot_tn(a, b):
    return lax.dot_general(a, b, (((0,), (0,)), ((), ())), preferred_element_type=F32)


def _silu(x):
    return x * jax.nn.sigmoid(x)


def _gelu_tanh(x):
    return 0.5 * x * (1.0 + jnp.tanh(math.sqrt(2.0 / math.pi) * (x + 0.044715 * (x * x * x))))


def _mod_row(i, tm, s):
    if tm >= MOD_GROUP:
        block_index = i * (tm // MOD_GROUP) + s
    else:
        block_index = i >> int(math.log2(MOD_GROUP // tm))
    return jnp.maximum(block_index - (N_CTX_ROWS // MOD_GROUP - 1), 0)


def _norm_mod(x, g, sc, sh):
    ms = jnp.mean(x * x, axis=-1, keepdims=True)
    return (x * lax.rsqrt(ms + EPS) * g) * (1.0 + sc) + sh


def _full(shape):
    return pl.BlockSpec(shape, lambda *_: (0,) * len(shape))


def _layer_block(shape, layer):
    return pl.BlockSpec((None,) + shape, lambda *_: (layer,) + (0,) * len(shape))


MOD_TN = 1024


def _mod_kernel(cc_ref, c_ref, w_ref, b_ref, o_ref):
    l = pl.program_id(0)
    pad = jnp.zeros((MOD_ROWS - 1 - DEC_BATCH, D_MODEL), F32)
    cond = jnp.concatenate([cc_ref[...], c_ref[...], pad], axis=0)
    o_ref[...] = _dot(_silu(cond).astype(BF16), w_ref[...].astype(BF16)) + b_ref[pl.ds(l, 1), :]


def _modulation(c_ctx, c, w_mod, b_mod):
    n = N_MOD * D_MODEL
    return pl.pallas_call(
        _mod_kernel,
        grid=(DEPTH, n // MOD_TN),
        in_specs=[
            pl.BlockSpec((1, D_MODEL), lambda l, j: (0, 0)),
            pl.BlockSpec((DEC_BATCH, D_MODEL), lambda l, j: (0, 0)),
            pl.BlockSpec((None, D_MODEL, MOD_TN), lambda l, j: (l, 0, j)),
            pl.BlockSpec((DEPTH, MOD_TN), lambda l, j: (0, j)),
        ],
        out_specs=pl.BlockSpec((None, MOD_ROWS, MOD_TN), lambda l, j: (l, 0, j)),
        out_shape=jax.ShapeDtypeStruct((DEPTH, MOD_ROWS, n), F32),
        compiler_params=_cparams("arbitrary", "arbitrary"),
        name="modulation",
    )(c_ctx.reshape(1, D_MODEL), c, w_mod, b_mod)


FFN_TM = 1024
FFN_TF = 256
N_CTX_TILES = N_CTX_ROWS // FFN_TM


FFN_NJ = D_FF // FFN_TF
N_FFN_TILES = N_ROWS // FFN_TM
N_FFN_STEPS = FFN_NJ + N_FFN_TILES - 1


def _ffn_tile(step):
    return jnp.maximum(step - (FFN_NJ - 1), 0)


def _on_stream_part(tile, x_refs, o_refs, fn):
    if len(x_refs) == 1 and len(o_refs) == 1:
        fn(x_refs[0], o_refs[0])
    else:
        pl.when(tile < N_CTX_TILES)(lambda: fn(x_refs[0], o_refs[0]))
        pl.when(tile >= N_CTX_TILES)(lambda: fn(x_refs[-1], o_refs[-1]))


def _ffn_kernel(*refs, layer, n_in, n_out):
    x_refs = refs[:n_in]
    n_ref, sh_ref, sc_ref, g_ref, wg_ref, wu_ref, wd_ref = refs[n_in:n_in + 7]
    o_refs = refs[n_in + 7:n_in + 7 + n_out]
    h_ref, a_ref, wg_s, wu_s, wd_s = refs[n_in + 7 + n_out:]
    nj, tf = FFN_NJ, FFN_TF
    s = pl.program_id(0)
    tile = _ffn_tile(s)
    r = _mod_row(tile, FFN_TM, 0)

    def load_tile():
        def init(x_ref, _):
            h = _norm_mod(x_ref[...], n_ref[layer:layer + 1, :], sc_ref[pl.ds(r, 1), :], sh_ref[pl.ds(r, 1), :])
            h_ref[...] = h.astype(BF16)
        _on_stream_part(tile, x_refs, o_refs, init)

    def up_chunk(j, cols):
        h = h_ref[...]
        a_ref[:, cols] = (_silu(_dot(h, wg_s[j])) * _dot(h, wu_s[j])).astype(BF16)

    def down_and_store():
        y = (0.5 * g_ref[pl.ds(r, 1), :]) * _dot(a_ref[...], wd_s[...])

        def store(x_ref, o_ref):
            o_ref[...] = x_ref[...] + y
        _on_stream_part(tile, x_refs, o_refs, store)

    @pl.when(s == 0)
    def _():
        load_tile()

    @pl.when(s < nj)
    def _():
        lo = pl.multiple_of(s * tf, tf)
        wg_s[s] = wg_ref[...].astype(BF16)
        wu_s[s] = wu_ref[...].astype(BF16)
        wd_s[pl.ds(lo, tf), :] = wd_ref[...].astype(BF16)
        up_chunk(s, pl.ds(lo, tf))

    @pl.when(s == nj - 1)
    def _():
        down_and_store()

    @pl.when(s >= nj)
    def _():
        load_tile()
        for j in range(nj):
            up_chunk(j, slice(j * tf, (j + 1) * tf))
        down_and_store()


def _stream_specs(split, buffered_once):
    tm = FFN_TM
    kw = {"pipeline_mode": pl.Buffered(1)} if buffered_once else {}
    if not split:
        return [pl.BlockSpec((tm, D_MODEL), lambda s: (_ffn_tile(s), 0), **kw)]
    last_ctx = N_CTX_TILES - 1
    return [pl.BlockSpec((tm, D_MODEL), lambda s: (jnp.minimum(_ffn_tile(s), last_ctx), 0), **kw),
            pl.BlockSpec((tm, D_MODEL), lambda s: (jnp.maximum(_ffn_tile(s) - N_CTX_TILES, 0), 0), **kw)]


def _ffn(xs, mod, layer, chunk0, norm_g, wg, wu, wd, split_out=False):
    tm, tf, nj = FFN_TM, FFN_TF, FFN_NJ
    split_in = len(xs) == 2
    mod_spec = lambda c: pl.BlockSpec((None, MOD_ROWS, D_MODEL), lambda s: (layer, 0, c))
    w_col = lambda s: (layer, 0, jnp.minimum(s, nj - 1))
    w_row = lambda s: (layer, jnp.minimum(s, nj - 1), 0)
    if split_out:
        out_shape = [jax.ShapeDtypeStruct((N_CTX_ROWS, D_MODEL), F32),
                     jax.ShapeDtypeStruct((N_LAT_ROWS, D_MODEL), F32)]
    else:
        out_shape = [jax.ShapeDtypeStruct((N_ROWS, D_MODEL), F32)]
    out = pl.pallas_call(
        functools.partial(_ffn_kernel, layer=layer, n_in=len(xs), n_out=len(out_shape)),
        grid=(N_FFN_STEPS,),
        in_specs=_stream_specs(split_in, split_in) + [
            _full((DEPTH, D_MODEL)),
            mod_spec(chunk0), mod_spec(chunk0 + 1), mod_spec(chunk0 + 2),
            pl.BlockSpec((None, D_MODEL, tf), w_col),
            pl.BlockSpec((None, D_MODEL, tf), w_col),
            pl.BlockSpec((None, tf, D_MODEL), w_row),
        ],
        out_specs=_stream_specs(split_out, True),
        out_shape=out_shape,
        scratch_shapes=[pltpu.VMEM((tm, D_MODEL), BF16),
                        pltpu.VMEM((tm, D_FF), BF16),
                        pltpu.VMEM((nj, D_MODEL, tf), BF16),
                        pltpu.VMEM((nj, D_MODEL, tf), BF16),
                        pltpu.VMEM((D_FF, D_MODEL), BF16)],
        compiler_params=_cparams("arbitrary"),
        name="ffn",
    )(*xs, norm_g, mod, mod, mod, wg, wu, wd)
    return tuple(out)


def _once(shape, index_map):
    return pl.BlockSpec(shape, index_map, pipeline_mode=pl.Buffered(1))


def _mod_chunk(layer, c):
    return pl.BlockSpec((None, MOD_ROWS, D_MODEL), lambda *_: (layer, 0, c))


def _project_in(x, n_row, sc_row, sh_row, w_bf16):
    return _dot(_norm_mod(x, n_row, sc_row, sh_row).astype(BF16), w_bf16)

def _head_mean_square(x):
    n = x.shape[-1]
    x2 = x * x
    hi = x2.astype(BF16)
    lo = (x2 - hi.astype(F32)).astype(BF16)
    r = lax.broadcasted_iota(jnp.int32, (n, n), 0) >> HEAD_SHIFT
    c = lax.broadcasted_iota(jnp.int32, (n, n), 1) >> HEAD_SHIFT
    ones_bd = jnp.where(r == c, 1.0, 0.0).astype(BF16)
    return (_dot(hi, ones_bd) + _dot(lo, ones_bd)) * (1.0 / HEAD_DIM)


def _head_norm(x, head_gain):
    gain_row = jnp.concatenate([head_gain] * (x.shape[-1] // HEAD_DIM), axis=-1)
    return x * lax.rsqrt(_head_mean_square(x) + EPS) * gain_row


def _head_cols(x, h):
    return x[:, h * HEAD_DIM:(h + 1) * HEAD_DIM].astype(BF16)


def _rope(x, cos, sin_lo, sin_hi):
    cols = []
    for c in range(x.shape[-1] // 128):
        xc = x[:, c * 128:(c + 1) * 128]
        cols.append(xc * cos + pltpu.roll(xc, 112, 1) * sin_lo + pltpu.roll(xc, 16, 1) * sin_hi)
    return cols[0] if len(cols) == 1 else jnp.concatenate(cols, axis=-1)


def _softmax_pv(scores, values, sink):
    m = jnp.max(scores[0], axis=-1, keepdims=True)
    for s in scores[1:]:
        m = jnp.maximum(m, jnp.max(s, axis=-1, keepdims=True))
    if sink is not None:
        m = jnp.maximum(m, sink)
    denom = None
    acc = None
    for s, v in zip(scores, values):
        p = jnp.exp(s - m)
        d = jnp.sum(p, axis=-1, keepdims=True)
        o = _dot(p.astype(BF16), v)
        denom = d if denom is None else denom + d
        acc = o if acc is None else acc + o
    if sink is not None:
        denom = denom + jnp.exp(sink - m)
    return acc / denom


def _block_diag(blocks):
    n = len(blocks)
    w = blocks[0].shape[0]
    rows = []
    for k, blk in enumerate(blocks):
        parts = []
        if k > 0:
            parts.append(jnp.zeros((w, k * w), F32))
        parts.append(blk)
        if k < n - 1:
            parts.append(jnp.zeros((w, (n - 1 - k) * w), F32))
        rows.append(jnp.concatenate(parts, axis=-1))
    return jnp.concatenate(rows, axis=0).astype(BF16)


def _rglru_gates(xc, wa, ba, wx, bx, lam):
    xb = xc.astype(BF16)
    r = jax.nn.sigmoid(_dot(xb, wa) + ba)
    i = jax.nn.sigmoid(_dot(xb, wx) + bx)
    softplus = jnp.maximum(-lam, 0.0) + jnp.log1p(jnp.exp(-jnp.abs(lam)))
    log_a = (-LRU_C) * r * softplus
    a = jnp.exp(log_a)
    b = jnp.sqrt(-jnp.tanh(log_a) * (a * a + 1.0)) * (i * xc)
    return a, b


def _block_prefix(a, b, reverse):
    t = a.shape[0]
    row = lax.broadcasted_iota(jnp.int32, a.shape, 0) & 7
    for d in (1, 2, 4):
        if reverse:
            a_s = pltpu.roll(a, t - d, 0)
            b_s = pltpu.roll(b, t - d, 0)
            ok = row < 8 - d
        else:
            a_s = pltpu.roll(a, d, 0)
            b_s = pltpu.roll(b, d, 0)
            ok = row >= d
        b = jnp.where(ok, a * b_s + b, b)
        a = jnp.where(ok, a * a_s, a)
    return a, b


def _conv4(x, w_ref, b_row):
    t = x.shape[0]
    row = lax.broadcasted_iota(jnp.int32, x.shape, 0)
    xm2 = jnp.where(row >= 2, pltpu.roll(x, 2, 0), 0.0)
    xm1 = jnp.where(row >= 1, pltpu.roll(x, 1, 0), 0.0)
    xp1 = jnp.where(row < t - 1, pltpu.roll(x, t - 1, 0), 0.0)
    return (xm2 * w_ref[0:1, :] + xm1 * w_ref[1:2, :] + x * w_ref[2:3, :] + xp1 * w_ref[3:4, :]) + b_row


def _rglru_mixer(cx, cy, conv_w_ref, conv_b, gate_w_ref, ba_ref, bx_ref, lam_ref, h0f, h0b,
                 af_ref, bf_ref, ab_ref, bb_ref, hf_ref, hb_ref):
    t = cx.shape[0]
    xc = _conv4(cx, conv_w_ref, conv_b)
    a, b = _rglru_gates(xc, gate_w_ref[0], ba_ref[0:1, :], gate_w_ref[1], bx_ref[0:1, :], lam_ref[0:1, :])
    a, b = _block_prefix(a, b, reverse=False)
    af_ref[...] = a
    bf_ref[...] = b
    a, b = _rglru_gates(xc, gate_w_ref[2], ba_ref[1:2, :], gate_w_ref[3], bx_ref[1:2, :], lam_ref[1:2, :])
    a, b = _block_prefix(a, b, reverse=True)
    ab_ref[...] = a
    bb_ref[...] = b
    nblk = t // 8

    def body(k, carry):
        cf, cb = carry
        rf = pl.ds(pl.multiple_of(k * 8, 8), 8)
        hf = bf_ref[rf, :] + af_ref[rf, :] * cf
        hf_ref[rf, :] = hf
        rb = pl.ds(pl.multiple_of((nblk - 1 - k) * 8, 8), 8)
        hb = bb_ref[rb, :] + ab_ref[rb, :] * cb
        hb_ref[rb, :] = hb
        return hf[7:8, :], hb[0:1, :]

    cf, cb = lax.fori_loop(0, nblk, body, (h0f, h0b))
    oc = (hf_ref[...] + hb_ref[...]) * _gelu_tanh(cy)
    return oc, cf, cb


def _store_gate_weights(gate_w_ref, wa_ref, wx_ref):
    for d in range(2):
        gate_w_ref[2 * d] = _block_diag([wa_ref[d, n] for n in range(N_HEADS)])
        gate_w_ref[2 * d + 1] = _block_diag([wx_ref[d, n] for n in range(N_HEADS)])


def _lane_head_masks(n):
    lane = lax.broadcasted_iota(jnp.int32, (1, n), 1) >> HEAD_SHIFT
    return [jnp.where(lane == h, 1.0, 0.0) for h in range(n // HEAD_DIM)]


def _log_decays(theta_ref, masks):
    theta = theta_ref[...]
    lanes = theta[:, 0:1] * masks[0]
    for h in range(1, N_HEADS):
        lanes = lanes + theta[:, h:h + 1] * masks[h]
    lg = jnp.log1p(-jnp.exp(lanes))
    return lg[0:1, :], lg[1:2, :]


def _retention_block(q, k8b, vb, q0, lgf, lgb, masks):
    tq, w = q.shape
    s_len = k8b.shape[0]
    nh = w // HEAD_DIM
    q_stack = jnp.concatenate([(q * masks[h]).astype(BF16) for h in range(nh)], axis=0)
    raw = _dot_nt(q_stack, k8b)
    qi = lax.broadcasted_iota(jnp.int32, (tq, s_len), 0) + q0
    si = lax.broadcasted_iota(jnp.int32, (tq, s_len), 1)
    rel = (qi - si).astype(F32)
    decs = []
    for h in range(nh):
        gf = lgf[:, h * HEAD_DIM:h * HEAD_DIM + 1]
        gb = lgb[:, h * HEAD_DIM:h * HEAD_DIM + 1]
        e = jnp.exp(jnp.where(rel >= 0, gf * rel, gb * (-rel)))
        decs.append(jnp.where(rel == 0, 2.0, e))
    inner = (raw * jnp.concatenate(decs, axis=0)).astype(BF16)
    out = _dot(inner, vb)
    o = out[0:tq, :] * masks[0]
    for h in range(1, nh):
        o = o + out[h * tq:(h + 1) * tq, :] * masks[h]
    return o


def _ctx_mixer_kernel(*refs, layer, n_prev):
    prev_refs = refs[:n_prev]
    (x_ref, n2_ref, sh_ref, sc_ref, g2_ref, win_ref, wout_ref,
     aqn_ref, akn_ref, bqn_ref, bkn_ref, sink_ref,
     convw_ref, convb_ref, wa_ref, ba_ref, wx_ref, bx_ref, lam_ref, theta_ref, dn_ref,
     xn_ref, *state_refs) = refs[n_prev:n_prev + 28]
    (win_s, wout_s, u_ref, mixed_ref,
     gate_w_ref, af_ref, bf_ref, ab_ref, bb_ref, hf_ref, hb_ref) = refs[n_prev + 28:]
    t = SEQ
    lrow = slice(layer, layer + 1)
    for prev_ref, state_ref in zip(prev_refs, state_refs):
        for earlier in range(layer):
            state_ref[earlier] = prev_ref[earlier]
    ka_ref, va_ref, kb_ref, vb_ref, stc_ref, std_ref = (ref.at[layer] for ref in state_refs)

    @pl.when(pl.program_id(0) == 0)
    def _():
        for c in range(IN_WIDTH // 512):
            win_s[:, c * 512:(c + 1) * 512] = win_ref[:, c * 512:(c + 1) * 512].astype(BF16)
        wout_s[...] = wout_ref[...].astype(BF16)
        _store_gate_weights(gate_w_ref, wa_ref, wx_ref)

    x = x_ref[...]
    u_ref[...] = _project_in(x, n2_ref[lrow, :], sc_ref[0:1, :], sh_ref[0:1, :], win_s[...])

    for (cq, ck, cv, qn_ref, kn_ref, k_out, v_out, col0, use_sink) in (
            (COL_AQ, COL_AK, COL_AV, aqn_ref, akn_ref, ka_ref, va_ref, 0, True),
            (COL_BQ, COL_BK, COL_BV, bqn_ref, bkn_ref, kb_ref, vb_ref, GROUP_W, False)):
        q = _head_norm(u_ref[:, cq:cq + 256], qn_ref[lrow, :])
        k = _head_norm(u_ref[:, ck:ck + 128], kn_ref[lrow, :])
        v = u_ref[:, cv:cv + 128]
        k_out[...] = k
        v_out[...] = v
        qs = q * (HEAD_DIM ** -0.5)
        heads = []
        for h in range(N_HEADS):
            kv = h // 2
            s = _dot_nt(_head_cols(qs, h), _head_cols(k, kv))
            sink = jnp.full((t, 1), sink_ref[layer, h], F32) if use_sink else None
            heads.append(_softmax_pv([s], [_head_cols(v, kv)], sink))
        mixed_ref[:, col0:col0 + GROUP_W] = jnp.concatenate(heads, axis=-1).astype(BF16)

    zero = jnp.zeros((1, GROUP_W), F32)
    oc, cf, cb = _rglru_mixer(u_ref[:, COL_CX:COL_CX + 256], u_ref[:, COL_CY:COL_CY + 256],
                              convw_ref, convb_ref[lrow, :], gate_w_ref, ba_ref, bx_ref, lam_ref,
                              zero, zero, af_ref, bf_ref, ab_ref, bb_ref, hf_ref, hb_ref)
    mixed_ref[:, 2 * GROUP_W:3 * GROUP_W] = oc.astype(BF16)
    stc_ref[0:1, :] = cf
    stc_ref[1:2, :] = cb

    masks = _lane_head_masks(GROUP_W)
    lgf, lgb = _log_decays(theta_ref, masks)
    q = u_ref[:, COL_DQ:COL_DQ + 256]
    k8 = u_ref[:, COL_DK:COL_DK + 256] * (HEAD_DIM ** -0.5)
    vb = u_ref[:, COL_DV:COL_DV + 256].astype(BF16)
    o = _retention_block(q, k8.astype(BF16), vb, 0, lgf, lgb, masks)
    dn = dn_ref[lrow, :]
    o = o * lax.rsqrt(_head_mean_square(o) + EPS) * dn * _silu(u_ref[:, COL_DG:COL_DG + 256])
    mixed_ref[:, 3 * GROUP_W:4 * GROUP_W] = o.astype(BF16)
    pos = lax.broadcasted_iota(jnp.int32, (t, GROUP_W), 0).astype(F32)
    for d, (lg, expo) in enumerate(((lgf, (t - 1.0) - pos), (lgb, pos))):
        s_full = _dot_tn((k8 * jnp.exp(lg * expo)).astype(BF16), vb)
        for h in range(N_HEADS):
            std_ref[d, h] = s_full[h * 64:(h + 1) * 64, h * 64:(h + 1) * 64]

    xn_ref[...] = x + g2_ref[0:1, :] * _dot(mixed_ref[...], wout_s[...])


def _ctx_mixers(x, mod, layer, prev, norm2_g, w_in, w_out,
                a_qn, a_kn, a_sink, b_qn, b_kn, c_conv_w, c_conv_b, c_wa, c_ba, c_wx, c_bx,
                c_lambda, d_theta, d_norm_g):
    per_request = lambda slots, shape: pl.BlockSpec((None, slots) + shape, lambda b: (b,) + (0,) * (1 + len(shape)))
    state_dims = [(SEQ, 128)] * 4 + [(2, GROUP_W), (2, N_HEADS, HEAD_DIM, HEAD_DIM)]
    scr = pltpu.VMEM((SEQ, GROUP_W), F32)
    out = pl.pallas_call(
        functools.partial(_ctx_mixer_kernel, layer=layer, n_prev=len(prev)),
        grid=(BATCH,),
        in_specs=[per_request(layer, dims) for dims in state_dims[:len(prev)]] + [
            pl.BlockSpec((SEQ, D_MODEL), lambda b: (b, 0)),
            _full((DEPTH, D_MODEL)),
            _mod_chunk(layer, 3), _mod_chunk(layer, 4), _mod_chunk(layer, 5),
            _once((None, D_MODEL, IN_WIDTH), lambda b: (layer, 0, 0)),
            _once((None, D_MODEL, D_MODEL), lambda b: (layer, 0, 0)),
            _full((DEPTH, HEAD_DIM)), _full((DEPTH, HEAD_DIM)), _full((DEPTH, HEAD_DIM)), _full((DEPTH, HEAD_DIM)),
            pl.BlockSpec(memory_space=pltpu.SMEM),
            _layer_block((4, GROUP_W), layer), _full((DEPTH, GROUP_W)),
            _layer_block((2, N_HEADS, HEAD_DIM, HEAD_DIM), layer), _layer_block((2, GROUP_W), layer),
            _layer_block((2, N_HEADS, HEAD_DIM, HEAD_DIM), layer), _layer_block((2, GROUP_W), layer),
            _layer_block((2, GROUP_W), layer),
            _layer_block((2, N_HEADS), layer), _full((DEPTH, GROUP_W)),
        ],
        out_specs=[pl.BlockSpec((SEQ, D_MODEL), lambda b: (b, 0))] + [
            per_request(layer + 1, dims) for dims in state_dims],
        out_shape=[jax.ShapeDtypeStruct((N_ROWS, D_MODEL), F32)] + [
            jax.ShapeDtypeStruct((BATCH, layer + 1) + dims, F32) for dims in state_dims],
        input_output_aliases={len(prev): 0},
        scratch_shapes=[pltpu.VMEM((D_MODEL, IN_WIDTH), BF16), pltpu.VMEM((D_MODEL, D_MODEL), BF16),
                        pltpu.VMEM((SEQ, IN_WIDTH), F32), pltpu.VMEM((SEQ, D_MODEL), BF16),
                        pltpu.VMEM((4, GROUP_W, GROUP_W), BF16)] + [scr] * 6,
        compiler_params=_cparams("arbitrary"),
        name="ctx_mixers",
    )(*prev, x, norm2_g, mod, mod, mod, w_in, w_out,
      a_qn, a_kn, b_qn, b_kn, a_sink, c_conv_w, c_conv_b, c_wa, c_ba, c_wx, c_bx,
      c_lambda, d_theta, d_norm_g)
    return out[0], tuple(out[1:])


LAT_BLOCK0 = N_CTX_ROWS // DEC_SEQ


def _lat_attn_kernel(x_ref, n2_ref, sh_ref, sc_ref, g2_ref, win_ref, wout_ref,
                     kca_ref, vca_ref, kcb_ref, vcb_ref,
                     aqn_ref, akn_ref, bqn_ref, bkn_ref, sink_ref, cos_ref, sinl_ref, sinh_ref,
                     xn_ref, h_ref, u_ref, o_ref, *, layer):
    t = DEC_SEQ
    lrow = slice(layer, layer + 1)
    mrow = pl.ds(1 + pl.program_id(0), 1)
    cos, sin_lo, sin_hi = cos_ref[...], sinl_ref[...], sinh_ref[...]
    scale = HEAD_DIM ** -0.5
    x = x_ref[...]
    h_ref[...] = _norm_mod(x, n2_ref[lrow, :], sc_ref[mrow, :], sh_ref[mrow, :]).astype(BF16)
    u_ref[...] = _dot(h_ref[...], win_ref[...].astype(BF16))

    q = _rope(_head_norm(u_ref[:, COL_AQ:COL_AQ + 256], aqn_ref[lrow, :]), cos, sin_lo, sin_hi)
    k = _rope(_head_norm(u_ref[:, COL_AK:COL_AK + 128], akn_ref[lrow, :]), cos, sin_lo, sin_hi)
    qh = [_head_cols(q * scale, h) for h in range(4)]
    v = u_ref[:, COL_AV:COL_AV + 128]
    kh = [_head_cols(k, kv) for kv in range(2)]
    vh = [_head_cols(v, kv) for kv in range(2)]
    kch = [_head_cols(kca_ref[...], kv) for kv in range(2)]
    vch = [_head_cols(vca_ref[...], kv) for kv in range(2)]
    w = ATT_BLOCK
    span = 3 * w
    for n in range(t // w):
        start = min(max((n - 1) * w, 0), t - span)
        rows = slice(n * w, (n + 1) * w)
        band = slice(start, start + span)
        qpos = (lax.broadcasted_iota(jnp.int32, (2 * w, span), 0) & (w - 1)) + n * w
        kpos = lax.broadcasted_iota(jnp.int32, (2 * w, span), 1) + start
        valid = jnp.abs(qpos - kpos) <= WINDOW
        heads = []
        for kv in range(2):
            qp = jnp.concatenate([qh[2 * kv][rows, :], qh[2 * kv + 1][rows, :]], axis=0)
            s_ctx = _dot_nt(qp, kch[kv])
            s_band = jnp.where(valid, _dot_nt(qp, kh[kv][band, :]), NEG_INF)
            row = lax.broadcasted_iota(jnp.int32, (2 * w, 1), 0)
            sink = jnp.where(row < w, sink_ref[layer, 2 * kv], sink_ref[layer, 2 * kv + 1])
            o = _softmax_pv([s_ctx, s_band], [vch[kv], vh[kv][band, :]], sink)
            heads += [o[0:w, :], o[w:2 * w, :]]
        o_ref[rows, 0:GROUP_W] = jnp.concatenate(heads, axis=-1).astype(BF16)

    q = _rope(_head_norm(u_ref[:, COL_BQ:COL_BQ + 256], bqn_ref[lrow, :]), cos, sin_lo, sin_hi)
    k = _rope(_head_norm(u_ref[:, COL_BK:COL_BK + 128], bkn_ref[lrow, :]), cos, sin_lo, sin_hi)
    qh = [_head_cols(q * scale, h) for h in range(4)]
    v = u_ref[:, COL_BV:COL_BV + 128]
    kh = [_head_cols(k, kv) for kv in range(2)]
    vh = [_head_cols(v, kv) for kv in range(2)]
    kch = [_head_cols(kcb_ref[...], kv) for kv in range(2)]
    vch = [_head_cols(vcb_ref[...], kv) for kv in range(2)]
    tq = 256
    for n in range(t // tq):
        rows = slice(n * tq, (n + 1) * tq)
        heads = []
        for kv in range(2):
            qp = jnp.concatenate([qh[2 * kv][rows, :], qh[2 * kv + 1][rows, :]], axis=0)
            o = _softmax_pv([_dot_nt(qp, kch[kv]), _dot_nt(qp, kh[kv])], [vch[kv], vh[kv]], None)
            heads += [o[0:tq, :], o[tq:2 * tq, :]]
        o_ref[rows, GROUP_W:2 * GROUP_W] = jnp.concatenate(heads, axis=-1).astype(BF16)

    xn_ref[...] = x + g2_ref[mrow, :] * _dot(o_ref[...], wout_ref[...].astype(BF16))


def _lat_rglru_kernel(xn_in_ref, h_ref, g2_ref, win_ref, wout_ref, h0_ref,
                      convw_ref, convb_ref, wa_ref, ba_ref, wx_ref, bx_ref, lam_ref,
                      xn_ref, gate_w_ref, af_ref, bf_ref, ab_ref, bb_ref, hf_ref, hb_ref, *, layer):
    lrow = slice(layer, layer + 1)
    mrow = pl.ds(1 + pl.program_id(0), 1)

    @pl.when(pl.program_id(0) == 0)
    def _():
        _store_gate_weights(gate_w_ref, wa_ref, wx_ref)

    u = _dot(h_ref[...], win_ref[...].astype(BF16))
    oc, _, _ = _rglru_mixer(u[:, 0:GROUP_W], u[:, GROUP_W:2 * GROUP_W], convw_ref, convb_ref[lrow, :],
                            gate_w_ref, ba_ref, bx_ref, lam_ref,
                            h0_ref[0:1, :], h0_ref[1:2, :],
                            af_ref, bf_ref, ab_ref, bb_ref, hf_ref, hb_ref)
    xn_ref[...] = xn_in_ref[...] + g2_ref[mrow, :] * _dot(oc.astype(BF16), wout_ref[...].astype(BF16))


def _lat_retention_kernel(xn_in_ref, h_ref, g2_ref, wqk_ref, wvg_ref, wout_ref,
                          s0_ref, theta_ref, dn_ref, xn_ref, o_ref, *, layer):
    t = DEC_SEQ
    lrow = slice(layer, layer + 1)
    mrow = pl.ds(1 + pl.program_id(0), 1)
    h = h_ref[...]
    uqk = _dot(h, wqk_ref[...].astype(BF16))
    uvg = _dot(h, wvg_ref[...].astype(BF16))
    masks = _lane_head_masks(GROUP_W)
    lgf, lgb = _log_decays(theta_ref, masks)
    k8b = (uqk[:, GROUP_W:2 * GROUP_W] * (HEAD_DIM ** -0.5)).astype(BF16)
    vb = uvg[:, 0:GROUP_W].astype(BF16)
    s0f = _block_diag([s0_ref[0, hd] for hd in range(N_HEADS)])
    s0b = _block_diag([s0_ref[1, hd] for hd in range(N_HEADS)])
    dn = dn_ref[lrow, :]
    tq = 256
    for n in range(t // tq):
        rows = slice(n * tq, (n + 1) * tq)
        q = uqk[rows, 0:GROUP_W]
        o = _retention_block(q, k8b, vb, n * tq, lgf, lgb, masks)
        pos = lax.broadcasted_iota(jnp.int32, (tq, GROUP_W), 0).astype(F32) + float(n * tq)
        o = o + _dot((q * jnp.exp(lgf * (pos + 1.0))).astype(BF16), s0f)
        o = o + _dot((q * jnp.exp(lgb * (float(t) - pos))).astype(BF16), s0b)
        o = o * lax.rsqrt(_head_mean_square(o) + EPS) * dn * _silu(uvg[rows, GROUP_W:2 * GROUP_W])
        o_ref[rows, :] = o.astype(BF16)
    xn_ref[...] = xn_in_ref[...] + g2_ref[mrow, :] * _dot(o_ref[...], wout_ref[...].astype(BF16))


def _lat_mixers(x, mod, layer, caches, state_c, state_d, rope, norm2_g, w_in, w_out,
                a_qn, a_kn, a_sink, b_qn, b_kn, c_conv_w, c_conv_b, c_wa, c_ba, c_wx, c_bx,
                c_lambda, d_theta, d_norm_g):
    rows = pl.BlockSpec((DEC_SEQ, D_MODEL), lambda b: (LAT_BLOCK0 + b, 0))
    h_rows = pl.BlockSpec((DEC_SEQ, D_MODEL), lambda b: (b, 0))
    cache_spec = pl.BlockSpec((None, None, PAST_LEN, 128), lambda b: (b, layer, 0, 0))
    gain = _full((DEPTH, HEAD_DIM))
    table = _once((DEC_SEQ, 128), lambda b: (0, 0))
    out_shape = jax.ShapeDtypeStruct((N_ROWS, D_MODEL), F32)
    win_cols = lambda w, c: _once((None, D_MODEL, w), lambda b: (layer, 0, c))
    wout_rows = lambda h, r: _once((None, h, D_MODEL), lambda b: (layer, r, 0))

    xn, h = pl.pallas_call(
        functools.partial(_lat_attn_kernel, layer=layer),
        grid=(DEC_BATCH,),
        in_specs=[rows, _full((DEPTH, D_MODEL)),
                  _mod_chunk(layer, 3), _mod_chunk(layer, 4), _mod_chunk(layer, 5),
                  win_cols(4 * GROUP_W, 0), wout_rows(2 * GROUP_W, 0),
                  cache_spec, cache_spec, cache_spec, cache_spec,
                  gain, gain, gain, gain,
                  pl.BlockSpec(memory_space=pltpu.SMEM),
                  table, table, table],
        out_specs=[rows, pl.BlockSpec((DEC_SEQ, D_MODEL), lambda b: (b, 0), pipeline_mode=pl.Buffered(1))],
        out_shape=[out_shape, jax.ShapeDtypeStruct((N_LAT_ROWS, D_MODEL), BF16)],
        input_output_aliases={0: 0},
        scratch_shapes=[pltpu.VMEM((DEC_SEQ, 4 * GROUP_W), F32), pltpu.VMEM((DEC_SEQ, 2 * GROUP_W), BF16)],
        compiler_params=_cparams("arbitrary"),
        name="lat_attention",
    )(x, norm2_g, mod, mod, mod, w_in, w_out, *caches, a_qn, a_kn, b_qn, b_kn, a_sink, *rope)

    common = [rows, h_rows, _mod_chunk(layer, 5)]
    scr = pltpu.VMEM((DEC_SEQ, GROUP_W), F32)
    xn = pl.pallas_call(
        functools.partial(_lat_rglru_kernel, layer=layer),
        grid=(DEC_BATCH,),
        in_specs=common + [
            win_cols(2 * GROUP_W, COL_CX // (2 * GROUP_W)), wout_rows(GROUP_W, 2),
            pl.BlockSpec((None, None, 2, GROUP_W), lambda b: (b, layer, 0, 0)),
            _layer_block((4, GROUP_W), layer), _full((DEPTH, GROUP_W)),
            _layer_block((2, N_HEADS, HEAD_DIM, HEAD_DIM), layer), _layer_block((2, GROUP_W), layer),
            _layer_block((2, N_HEADS, HEAD_DIM, HEAD_DIM), layer), _layer_block((2, GROUP_W), layer),
            _layer_block((2, GROUP_W), layer)],
        out_specs=rows,
        out_shape=out_shape,
        input_output_aliases={0: 0},
        scratch_shapes=[pltpu.VMEM((4, GROUP_W, GROUP_W), BF16)] + [scr] * 6,
        compiler_params=_cparams("arbitrary"),
        name="lat_rglru",
    )(xn, h, mod, w_in, w_out, state_c, c_conv_w, c_conv_b, c_wa, c_ba, c_wx, c_bx, c_lambda)

    xn = pl.pallas_call(
        functools.partial(_lat_retention_kernel, layer=layer),
        grid=(DEC_BATCH,),
        in_specs=common + [
            win_cols(2 * GROUP_W, COL_DQ // (2 * GROUP_W)), win_cols(2 * GROUP_W, COL_DV // (2 * GROUP_W)),
            wout_rows(GROUP_W, 3),
            pl.BlockSpec((None, None, 2, N_HEADS, HEAD_DIM, HEAD_DIM), lambda b: (b, layer, 0, 0, 0, 0)),
            _layer_block((2, N_HEADS), layer), _full((DEPTH, GROUP_W))],
        out_specs=rows,
        out_shape=out_shape,
        input_output_aliases={0: 0},
        scratch_shapes=[pltpu.VMEM((DEC_SEQ, GROUP_W), BF16)],
        compiler_params=_cparams("arbitrary"),
        name="lat_retention",
    )(xn, h, mod, w_in, w_in, w_out, state_d, d_theta, d_norm_g)
    return xn


def _rope_tables():
    t = np.arange(DEC_SEQ)
    row = (t // GRID_W).astype(np.float64)[:, None]
    col = (t % GRID_W).astype(np.float64)[:, None]
    half = HEAD_DIM // 2
    inv = 1.0 / (ROPE_BASE ** (np.arange(0, half, 2, dtype=np.float64) / half))
    j = np.arange(128) % HEAD_DIM
    ang = np.where((j < half)[None, :], row, col) * inv[j % (half // 2)][None, :]
    first = ((j % half) < half // 2)[None, :]
    cos, sin = np.cos(ang), np.sin(ang)
    return tuple(jnp.asarray(a, F32) for a in (cos, np.where(first, -sin, 0.0), np.where(first, 0.0, sin)))


def kernel(x_prompt, x_sample, cache_a_k, cache_a_v, cache_b_k, cache_b_v, state_c, state_d, c, c_ctx, norm1_g, norm2_g, norm3_g, w_mod, b_mod, ffn1_wg, ffn1_wu, ffn1_wd, ffn2_wg, ffn2_wu, ffn2_wd, w_in, w_out, a_qn, a_kn, a_sink, b_qn, b_kn, c_conv_w, c_conv_b, c_wa, c_ba, c_wx, c_bx, c_lambda, d_theta, d_norm_g):
    mod = _modulation(c_ctx, c, w_mod, b_mod)
    rope = _rope_tables()
    caches = tuple(t.reshape(DEC_BATCH, DEPTH, PAST_LEN, 128) for t in (cache_a_k, cache_a_v, cache_b_k, cache_b_v))
    mixer_params = (a_qn, a_kn, a_sink, b_qn, b_kn, c_conv_w, c_conv_b, c_wa, c_ba, c_wx, c_bx,
                    c_lambda, d_theta, d_norm_g)
    xs = (x_prompt.reshape(N_CTX_ROWS, D_MODEL), x_sample.reshape(N_LAT_ROWS, D_MODEL))
    states = ()
    for l in range(DEPTH):
        (x,) = _ffn(xs, mod, l, 0, norm1_g, ffn1_wg, ffn1_wu, ffn1_wd)
        x, states = _ctx_mixers(x, mod, l, states, norm2_g, w_in, w_out, *mixer_params)
        x = _lat_mixers(x, mod, l, caches, state_c, state_d, rope, norm2_g, w_in, w_out, *mixer_params)
        xs = _ffn((x,), mod, l, 6, norm3_g, ffn2_wg, ffn2_wu, ffn2_wd, split_out=(l == DEPTH - 1))
    y_p, y_s = xs
    ka, va, kb, vb, st_c, st_d = states
    kv_shape = (BATCH, DEPTH, SEQ, 2, HEAD_DIM)
    return (y_p.reshape(BATCH, SEQ, D_MODEL), y_s.reshape(DEC_BATCH, DEC_SEQ, D_MODEL),
            ka.reshape(kv_shape), va.reshape(kv_shape), kb.reshape(kv_shape), vb.reshape(kv_shape),
            st_c, st_d)
```

```python
import functools
import math

import numpy as np
import jax
import jax.numpy as jnp
from jax import lax
from jax.experimental import pallas as pl
from jax.experimental.pallas import tpu as pltpu

F32 = jnp.float32
BF16 = jnp.bfloat16

D_MODEL = 1024
BATCH = 16
SEQ = 256
DEPTH = 2
DEC_BATCH = 2
DEC_SEQ = 1024
PAST_LEN = 512
GRID_W = 64
HEAD_DIM = 64
HEAD_SHIFT = 6
N_HEADS = 4
GROUP_W = 256
WINDOW = 128
ATT_BLOCK = 128
ROPE_BASE = 10000.0
LRU_C = 8.0
D_FF = 2816
N_MOD = 9
EPS = 1e-6
NEG_INF = -1e30
IN_WIDTH = 2560

N_CTX_ROWS = BATCH * SEQ
N_LAT_ROWS = DEC_BATCH * DEC_SEQ
N_ROWS = N_CTX_ROWS + N_LAT_ROWS
MOD_ROWS = 8
MOD_GROUP = 1024

VMEM_LIMIT_BYTES = 56 * 1024 * 1024

COL_AQ, COL_AK, COL_AV = 0, 256, 384
COL_BQ, COL_BK, COL_BV = 512, 768, 896
COL_CX, COL_CY = 1024, 1280
COL_DQ, COL_DK, COL_DV, COL_DG = 1536, 1792, 2048, 2304


def _cparams(*sem):
    return pltpu.CompilerParams(dimension_semantics=sem, vmem_limit_bytes=VMEM_LIMIT_BYTES)


def _dot(a, b):
    return jnp.dot(a, b, preferred_element_type=F32)


def _dot_nt(a, b):
    return lax.dot_general(a, b, (((1,), (1,)), ((), ())), preferred_element_type=F32)


def _dot_tn(a, b):
    return lax.dot_general(a, b, (((0,), (0,)), ((), ())), preferred_element_type=F32)


def _silu(x):
    return x * jax.nn.sigmoid(x)


def _gelu_tanh(x):
    return 0.5 * x * (1.0 + jnp.tanh(math.sqrt(2.0 / math.pi) * (x + 0.044715 * (x * x * x))))


def _mod_row(i, tm, s):
    if tm >= MOD_GROUP:
        block_index = i * (tm // MOD_GROUP) + s
    else:
        block_index = i >> int(math.log2(MOD_GROUP // tm))
    return jnp.maximum(block_index - (N_CTX_ROWS // MOD_GROUP - 1), 0)


def _norm_mod(x, g, sc, sh):
    ms = jnp.mean(x * x, axis=-1, keepdims=True)
    return (x * lax.rsqrt(ms + EPS) * g) * (1.0 + sc) + sh


def _full(shape):
    return pl.BlockSpec(shape, lambda *_: (0,) * len(shape))


def _layer_block(shape, layer):
    return pl.BlockSpec((None,) + shape, lambda *_: (layer,) + (0,) * len(shape))


MOD_TN = 1024


def _mod_kernel(cc_ref, c_ref, w_ref, b_ref, o_ref):
    l = pl.program_id(0)
    pad = jnp.zeros((MOD_ROWS - 1 - DEC_BATCH, D_MODEL), F32)
    cond = jnp.concatenate([cc_ref[...], c_ref[...], pad], axis=0)
    o_ref[...] = _dot(_silu(cond).astype(BF16), w_ref[...].astype(BF16)) + b_ref[pl.ds(l, 1), :]


def _modulation(c_ctx, c, w_mod, b_mod):
    n = N_MOD * D_MODEL
    return pl.pallas_call(
        _mod_kernel,
        grid=(DEPTH, n // MOD_TN),
        in_specs=[
            pl.BlockSpec((1, D_MODEL), lambda l, j: (0, 0)),
            pl.BlockSpec((DEC_BATCH, D_MODEL), lambda l, j: (0, 0)),
            pl.BlockSpec((None, D_MODEL, MOD_TN), lambda l, j: (l, 0, j)),
            pl.BlockSpec((DEPTH, MOD_TN), lambda l, j: (0, j)),
        ],
        out_specs=pl.BlockSpec((None, MOD_ROWS, MOD_TN), lambda l, j: (l, 0, j)),
        out_shape=jax.ShapeDtypeStruct((DEPTH, MOD_ROWS, n), F32),
        compiler_params=_cparams("arbitrary", "arbitrary"),
        name="modulation",
    )(c_ctx.reshape(1, D_MODEL), c, w_mod, b_mod)


FFN_TM = 1024
FFN_TF = 256
N_CTX_TILES = N_CTX_ROWS // FFN_TM


FFN_NJ = D_FF // FFN_TF
N_FFN_TILES = N_ROWS // FFN_TM
N_FFN_STEPS = FFN_NJ + N_FFN_TILES - 1


def _ffn_tile(step):
    return jnp.maximum(step - (FFN_NJ - 1), 0)


def _on_stream_part(tile, x_refs, o_refs, fn):
    if len(x_refs) == 1 and len(o_refs) == 1:
        fn(x_refs[0], o_refs[0])
    else:
        pl.when(tile < N_CTX_TILES)(lambda: fn(x_refs[0], o_refs[0]))
        pl.when(tile >= N_CTX_TILES)(lambda: fn(x_refs[-1], o_refs[-1]))


def _ffn_kernel(*refs, layer, n_in, n_out):
    x_refs = refs[:n_in]
    n_ref, sh_ref, sc_ref, g_ref, wg_ref, wu_ref, wd_ref = refs[n_in:n_in + 7]
    o_refs = refs[n_in + 7:n_in + 7 + n_out]
    h_ref, a_ref, wg_s, wu_s, wd_s = refs[n_in + 7 + n_out:]
    nj, tf = FFN_NJ, FFN_TF
    s = pl.program_id(0)
    tile = _ffn_tile(s)
    r = _mod_row(tile, FFN_TM, 0)

    def load_tile():
        def init(x_ref, _):
            h = _norm_mod(x_ref[...], n_ref[layer:layer + 1, :], sc_ref[pl.ds(r, 1), :], sh_ref[pl.ds(r, 1), :])
            h_ref[...] = h.astype(BF16)
        _on_stream_part(tile, x_refs, o_refs, init)

    def up_chunk(j, cols):
        h = h_ref[...]
        a_ref[:, cols] = (_silu(_dot(h, wg_s[j])) * _dot(h, wu_s[j])).astype(BF16)

    def down_and_store():
        y = (0.5 * g_ref[pl.ds(r, 1), :]) * _dot(a_ref[...], wd_s[...])

        def store(x_ref, o_ref):
            o_ref[...] = x_ref[...] + y
        _on_stream_part(tile, x_refs, o_refs, store)

    @pl.when(s == 0)
    def _():
        load_tile()

    @pl.when(s < nj)
    def _():
        lo = pl.multiple_of(s * tf, tf)
        wg_s[s] = wg_ref[...].astype(BF16)
        wu_s[s] = wu_ref[...].astype(BF16)
        wd_s[pl.ds(lo, tf), :] = wd_ref[...].astype(BF16)
        up_chunk(s, pl.ds(lo, tf))

    @pl.when(s == nj - 1)
    def _():
        down_and_store()

    @pl.when(s >= nj)
    def _():
        load_tile()
        for j in range(nj):
            up_chunk(j, slice(j * tf, (j + 1) * tf))
        down_and_store()


def _stream_specs(split, buffered_once):
    tm = FFN_TM
    kw = {"pipeline_mode": pl.Buffered(1)} if buffered_once else {}
    if not split:
        return [pl.BlockSpec((tm, D_MODEL), lambda s: (_ffn_tile(s), 0), **kw)]
    last_ctx = N_CTX_TILES - 1
    return [pl.BlockSpec((tm, D_MODEL), lambda s: (jnp.minimum(_ffn_tile(s), last_ctx), 0), **kw),
            pl.BlockSpec((tm, D_MODEL), lambda s: (jnp.maximum(_ffn_tile(s) - N_CTX_TILES, 0), 0), **kw)]


def _ffn(xs, mod, layer, chunk0, norm_g, wg, wu, wd, split_out=False):
    tm, tf, nj = FFN_TM, FFN_TF, FFN_NJ
    split_in = len(xs) == 2
    mod_spec = lambda c: pl.BlockSpec((None, MOD_ROWS, D_MODEL), lambda s: (layer, 0, c))
    w_col = lambda s: (layer, 0, jnp.minimum(s, nj - 1))
    w_row = lambda s: (layer, jnp.minimum(s, nj - 1), 0)
    if split_out:
        out_shape = [jax.ShapeDtypeStruct((N_CTX_ROWS, D_MODEL), F32),
                     jax.ShapeDtypeStruct((N_LAT_ROWS, D_MODEL), F32)]
    else:
        out_shape = [jax.ShapeDtypeStruct((N_ROWS, D_MODEL), F32)]
    out = pl.pallas_call(
        functools.partial(_ffn_kernel, layer=layer, n_in=len(xs), n_out=len(out_shape)),
        grid=(N_FFN_STEPS,),
        in_specs=_stream_specs(split_in, split_in) + [
            _full((DEPTH, D_MODEL)),
            mod_spec(chunk0), mod_spec(chunk0 + 1), mod_spec(chunk0 + 2),
            pl.BlockSpec((None, D_MODEL, tf), w_col),
            pl.BlockSpec((None, D_MODEL, tf), w_col),
            pl.BlockSpec((None, tf, D_MODEL), w_row),
        ],
        out_specs=_stream_specs(split_out, True),
        out_shape=out_shape,
        scratch_shapes=[pltpu.VMEM((tm, D_MODEL), BF16),
                        pltpu.VMEM((tm, D_FF), BF16),
                        pltpu.VMEM((nj, D_MODEL, tf), BF16),
                        pltpu.VMEM((nj, D_MODEL, tf), BF16),
                        pltpu.VMEM((D_FF, D_MODEL), BF16)],
        compiler_params=_cparams("arbitrary"),
        name="ffn",
    )(*xs, norm_g, mod, mod, mod, wg, wu, wd)
    return tuple(out)


def _once(shape, index_map):
    return pl.BlockSpec(shape, index_map, pipeline_mode=pl.Buffered(1))


def _mod_chunk(layer, c):
    return pl.BlockSpec((None, MOD_ROWS, D_MODEL), lambda *_: (layer, 0, c))


def _project_in(x, n_row, sc_row, sh_row, w_bf16):
    return _dot(_norm_mod(x, n_row, sc_row, sh_row).astype(BF16), w_bf16)

def _head_mean_square(x):
    n = x.shape[-1]
    x2 = x * x
    hi = x2.astype(BF16)
    lo = (x2 - hi.astype(F32)).astype(BF16)
    r = lax.broadcasted_iota(jnp.int32, (n, n), 0) >> HEAD_SHIFT
    c = lax.broadcasted_iota(jnp.int32, (n, n), 1) >> HEAD_SHIFT
    ones_bd = jnp.where(r == c, 1.0, 0.0).astype(BF16)
    return (_dot(hi, ones_bd) + _dot(lo, ones_bd)) * (1.0 / HEAD_DIM)


def _head_norm(x, head_gain):
    gain_row = jnp.concatenate([head_gain] * (x.shape[-1] // HEAD_DIM), axis=-1)
    return x * lax.rsqrt(_head_mean_square(x) + EPS) * gain_row


def _head_cols(x, h):
    return x[:, h * HEAD_DIM:(h + 1) * HEAD_DIM].astype(BF16)


def _rope(x, cos, sin_lo, sin_hi):
    cols = []
    for c in range(x.shape[-1] // 128):
        xc = x[:, c * 128:(c + 1) * 128]
        cols.append(xc * cos + pltpu.roll(xc, 112, 1) * sin_lo + pltpu.roll(xc, 16, 1) * sin_hi)
    return cols[0] if len(cols) == 1 else jnp.concatenate(cols, axis=-1)


def _softmax_pv(scores, values, sink):
    m = jnp.max(scores[0], axis=-1, keepdims=True)
    for s in scores[1:]:
        m = jnp.maximum(m, jnp.max(s, axis=-1, keepdims=True))
    if sink is not None:
        m = jnp.maximum(m, sink)
    denom = None
    acc = None
    for s, v in zip(scores, values):
        p = jnp.exp(s - m)
        d = jnp.sum(p, axis=-1, keepdims=True)
        o = _dot(p.astype(BF16), v)
        denom = d if denom is None else denom + d
        acc = o if acc is None else acc + o
    if sink is not None:
        denom = denom + jnp.exp(sink - m)
    return acc / denom


def _block_diag(blocks):
    n = len(blocks)
    w = blocks[0].shape[0]
    rows = []
    for k, blk in enumerate(blocks):
        parts = []
        if k > 0:
            parts.append(jnp.zeros((w, k * w), F32))
        parts.append(blk)
        if k < n - 1:
            parts.append(jnp.zeros((w, (n - 1 - k) * w), F32))
        rows.append(jnp.concatenate(parts, axis=-1))
    return jnp.concatenate(rows, axis=0)


def _rglru_gates(xc, wa, ba, wx, bx, lam):
    xb = xc.astype(BF16)
    r = jax.nn.sigmoid(_dot(xb, wa) + ba)
    i = jax.nn.sigmoid(_dot(xb, wx) + bx)
    softplus = jnp.maximum(-lam, 0.0) + jnp.log1p(jnp.exp(-jnp.abs(lam)))
    log_a = (-LRU_C) * r * softplus
    a = jnp.exp(log_a)
    b = jnp.sqrt(1.0 - a * a) * (i * xc)
    return a, b


def _block_prefix(a, b, reverse):
    t = a.shape[0]
    row = lax.broadcasted_iota(jnp.int32, a.shape, 0) & 7
    for d in (1, 2, 4):
        if reverse:
            a_s = pltpu.roll(a, t - d, 0)
            b_s = pltpu.roll(b, t - d, 0)
            ok = row < 8 - d
        else:
            a_s = pltpu.roll(a, d, 0)
            b_s = pltpu.roll(b, d, 0)
            ok = row >= d
        b = jnp.where(ok, a * b_s + b, b)
        a = jnp.where(ok, a * a_s, a)
    return a, b


def _conv4(x, w_ref, b_row):
    t = x.shape[0]
    row = lax.broadcasted_iota(jnp.int32, x.shape, 0)
    xm2 = jnp.where(row >= 2, pltpu.roll(x, 2, 0), 0.0)
    xm1 = jnp.where(row >= 1, pltpu.roll(x, 1, 0), 0.0)
    xp1 = jnp.where(row < t - 1, pltpu.roll(x, t - 1, 0), 0.0)
    return (xm2 * w_ref[0:1, :] + xm1 * w_ref[1:2, :] + x * w_ref[2:3, :] + xp1 * w_ref[3:4, :]) + b_row


def _rglru_mixer(cx, cy, conv_w_ref, conv_b, gate_w_ref, ba_ref, bx_ref, lam_ref, h0f, h0b,
                 af_ref, bf_ref, ab_ref, bb_ref, hf_ref, hb_ref):
    t = cx.shape[0]
    xc = _conv4(cx, conv_w_ref, conv_b)
    a, b = _rglru_gates(xc, gate_w_ref[0], ba_ref[0:1, :], gate_w_ref[1], bx_ref[0:1, :], lam_ref[0:1, :])
    a, b = _block_prefix(a, b, reverse=False)
    af_ref[...] = a
    bf_ref[...] = b
    a, b = _rglru_gates(xc, gate_w_ref[2], ba_ref[1:2, :], gate_w_ref[3], bx_ref[1:2, :], lam_ref[1:2, :])
    a, b = _block_prefix(a, b, reverse=True)
    ab_ref[...] = a
    bb_ref[...] = b
    nblk = t // 8

    def body(k, carry):
        cf, cb = carry
        rf = pl.ds(pl.multiple_of(k * 8, 8), 8)
        hf = bf_ref[rf, :] + af_ref[rf, :] * cf
        hf_ref[rf, :] = hf
        rb = pl.ds(pl.multiple_of((nblk - 1 - k) * 8, 8), 8)
        hb = bb_ref[rb, :] + ab_ref[rb, :] * cb
        hb_ref[rb, :] = hb
        return hf[7:8, :], hb[0:1, :]

    cf, cb = lax.fori_loop(0, nblk, body, (h0f, h0b))
    oc = (hf_ref[...] + hb_ref[...]) * _gelu_tanh(cy)
    return oc, cf, cb


def _store_gate_weights(gate_w_ref, wa_ref, wx_ref):
    for d in range(2):
        gate_w_ref[2 * d] = _block_diag([wa_ref[d, n] for n in range(N_HEADS)]).astype(BF16)
        gate_w_ref[2 * d + 1] = _block_diag([wx_ref[d, n] for n in range(N_HEADS)]).astype(BF16)


def _lane_head_masks(n):
    lane = lax.broadcasted_iota(jnp.int32, (1, n), 1) >> HEAD_SHIFT
    return [jnp.where(lane == h, 1.0, 0.0) for h in range(n // HEAD_DIM)]


def _log_decays(theta_ref, masks):
    theta = theta_ref[...]
    lanes = theta[:, 0:1] * masks[0]
    for h in range(1, N_HEADS):
        lanes = lanes + theta[:, h:h + 1] * masks[h]
    lg = jnp.log1p(-jnp.exp(lanes))
    return lg[0:1, :], lg[1:2, :]


RET_BLOCK = 256


def _retention(q, k8, vb, s0, lgf, lgb, masks, o_ref):
    t, w = q.shape
    c = RET_BLOCK
    nh = w // HEAD_DIM
    pos = lax.broadcasted_iota(jnp.int32, (c, w), 0).astype(F32)
    q_dec = (jnp.exp(lgf * (pos + 1.0)), jnp.exp(lgb * (float(c) - pos)))
    k_dec = (jnp.exp(lgf * (float(c - 1) - pos)), jnp.exp(lgb * pos))
    chunk_dec = (jnp.exp(lgf * float(c)), jnp.exp(lgb * float(c)))
    rel = (lax.broadcasted_iota(jnp.int32, (c, c), 0) - lax.broadcasted_iota(jnp.int32, (c, c), 1)).astype(F32)
    decs = []
    for h in range(nh):
        gf = lgf[:, h * HEAD_DIM:h * HEAD_DIM + 1]
        gb = lgb[:, h * HEAD_DIM:h * HEAD_DIM + 1]
        e = jnp.exp(jnp.where(rel >= 0, gf * rel, gb * (-rel)))
        decs.append(jnp.where(rel == 0, 2.0, e))
    dec = jnp.concatenate(decs, axis=0)
    r_head = lax.broadcasted_iota(jnp.int32, (w, w), 0) >> HEAD_SHIFT
    c_head = lax.broadcasted_iota(jnp.int32, (w, w), 1) >> HEAD_SHIFT
    same_head = jnp.where(r_head == c_head, 1.0, 0.0)
    states = [None, None] if s0 is None else list(s0)

    def carry(d, rows, o):
        if states[d] is not None:
            o = o + _dot((q[rows, :] * q_dec[d]).astype(BF16), states[d].astype(BF16))
        upd = _dot_tn((k8[rows, :] * k_dec[d]).astype(BF16), vb[rows, :]) * same_head
        states[d] = upd if states[d] is None else states[d] * chunk_dec[d] + upd
        return o

    for ci in range(t // c):
        rows = slice(ci * c, (ci + 1) * c)
        qc = q[rows, :]
        q_stack = jnp.concatenate([(qc * masks[h]).astype(BF16) for h in range(nh)], axis=0)
        inner = (_dot_nt(q_stack, k8[rows, :].astype(BF16)) * dec).astype(BF16)
        out = _dot(inner, vb[rows, :])
        o = out[0:c, :] * masks[0]
        for h in range(1, nh):
            o = o + out[h * c:(h + 1) * c, :] * masks[h]
        o_ref[rows, :] = carry(0, rows, o)
    for ci in reversed(range(t // c)):
        rows = slice(ci * c, (ci + 1) * c)
        if states[1] is not None:
            o_ref[rows, :] = carry(1, rows, o_ref[rows, :])
        else:
            carry(1, rows, None)
    return states[0], states[1]


def _ctx_mixer_kernel(*refs, layer, n_prev):
    prev_refs = refs[:n_prev]
    (x_ref, n2_ref, sh_ref, sc_ref, g2_ref, win_ref, wout_ref,
     aqn_ref, akn_ref, bqn_ref, bkn_ref, sink_ref,
     convw_ref, convb_ref, wa_ref, ba_ref, wx_ref, bx_ref, lam_ref, theta_ref, dn_ref,
     xn_ref, *state_refs) = refs[n_prev:n_prev + 28]
    (win_s, wout_s, u_ref, mixed_ref,
     gate_w_ref, af_ref, bf_ref, ab_ref, bb_ref, hf_ref, hb_ref, ret_ref) = refs[n_prev + 28:]
    t = SEQ
    lrow = slice(layer, layer + 1)
    for prev_ref, state_ref in zip(prev_refs, state_refs):
        for earlier in range(layer):
            state_ref[earlier] = prev_ref[earlier]
    ka_ref, va_ref, kb_ref, vb_ref, stc_ref, std_ref = (ref.at[layer] for ref in state_refs)

    @pl.when(pl.program_id(0) == 0)
    def _():
        for c in range(IN_WIDTH // 512):
            win_s[:, c * 512:(c + 1) * 512] = win_ref[:, c * 512:(c + 1) * 512].astype(BF16)
        wout_s[...] = wout_ref[...].astype(BF16)
        _store_gate_weights(gate_w_ref, wa_ref, wx_ref)

    x = x_ref[...]
    u_ref[...] = _project_in(x, n2_ref[lrow, :], sc_ref[0:1, :], sh_ref[0:1, :], win_s[...])

    for (cq, ck, cv, qn_ref, kn_ref, k_out, v_out, col0, use_sink) in (
            (COL_AQ, COL_AK, COL_AV, aqn_ref, akn_ref, ka_ref, va_ref, 0, True),
            (COL_BQ, COL_BK, COL_BV, bqn_ref, bkn_ref, kb_ref, vb_ref, GROUP_W, False)):
        q = _head_norm(u_ref[:, cq:cq + 256], qn_ref[lrow, :])
        k = _head_norm(u_ref[:, ck:ck + 128], kn_ref[lrow, :])
        v = u_ref[:, cv:cv + 128]
        k_out[...] = k
        v_out[...] = v
        qs = q * (HEAD_DIM ** -0.5)
        heads = []
        for h in range(N_HEADS):
            kv = h // 2
            s = _dot_nt(_head_cols(qs, h), _head_cols(k, kv))
            sink = jnp.full((t, 1), sink_ref[layer, h], F32) if use_sink else None
            heads.append(_softmax_pv([s], [_head_cols(v, kv)], sink))
        mixed_ref[:, col0:col0 + GROUP_W] = jnp.concatenate(heads, axis=-1).astype(BF16)

    zero = jnp.zeros((1, GROUP_W), F32)
    oc, cf, cb = _rglru_mixer(u_ref[:, COL_CX:COL_CX + 256], u_ref[:, COL_CY:COL_CY + 256],
                              convw_ref, convb_ref[lrow, :], gate_w_ref, ba_ref, bx_ref, lam_ref,
                              zero, zero, af_ref, bf_ref, ab_ref, bb_ref, hf_ref, hb_ref)
    mixed_ref[:, 2 * GROUP_W:3 * GROUP_W] = oc.astype(BF16)
    stc_ref[0:1, :] = cf
    stc_ref[1:2, :] = cb

    masks = _lane_head_masks(GROUP_W)
    lgf, lgb = _log_decays(theta_ref, masks)
    q = u_ref[:, COL_DQ:COL_DQ + 256]
    k8 = u_ref[:, COL_DK:COL_DK + 256] * (HEAD_DIM ** -0.5)
    vb = u_ref[:, COL_DV:COL_DV + 256].astype(BF16)
    final_states = _retention(q, k8, vb, None, lgf, lgb, masks, ret_ref)
    o = ret_ref[...]
    dn = dn_ref[lrow, :]
    o = o * lax.rsqrt(_head_mean_square(o) + EPS) * dn * _silu(u_ref[:, COL_DG:COL_DG + 256])
    mixed_ref[:, 3 * GROUP_W:4 * GROUP_W] = o.astype(BF16)
    for d, s_full in enumerate(final_states):
        for h in range(N_HEADS):
            std_ref[d, h] = s_full[h * 64:(h + 1) * 64, h * 64:(h + 1) * 64]

    xn_ref[...] = x + g2_ref[0:1, :] * _dot(mixed_ref[...], wout_s[...])


def _ctx_mixers(x, mod, layer, prev, norm2_g, w_in, w_out,
                a_qn, a_kn, a_sink, b_qn, b_kn, c_conv_w, c_conv_b, c_wa, c_ba, c_wx, c_bx,
                c_lambda, d_theta, d_norm_g):
    per_request = lambda slots, shape: pl.BlockSpec((None, slots) + shape, lambda b: (b,) + (0,) * (1 + len(shape)))
    state_dims = [(SEQ, 128)] * 4 + [(2, GROUP_W), (2, N_HEADS, HEAD_DIM, HEAD_DIM)]
    scr = pltpu.VMEM((SEQ, GROUP_W), F32)
    out = pl.pallas_call(
        functools.partial(_ctx_mixer_kernel, layer=layer, n_prev=len(prev)),
        grid=(BATCH,),
        in_specs=[per_request(layer, dims) for dims in state_dims[:len(prev)]] + [
            pl.BlockSpec((SEQ, D_MODEL), lambda b: (b, 0)),
            _full((DEPTH, D_MODEL)),
            _mod_chunk(layer, 3), _mod_chunk(layer, 4), _mod_chunk(layer, 5),
            _once((None, D_MODEL, IN_WIDTH), lambda b: (layer, 0, 0)),
            _once((None, D_MODEL, D_MODEL), lambda b: (layer, 0, 0)),
            _full((DEPTH, HEAD_DIM)), _full((DEPTH, HEAD_DIM)), _full((DEPTH, HEAD_DIM)), _full((DEPTH, HEAD_DIM)),
            pl.BlockSpec(memory_space=pltpu.SMEM),
            _layer_block((4, GROUP_W), layer), _full((DEPTH, GROUP_W)),
            _layer_block((2, N_HEADS, HEAD_DIM, HEAD_DIM), layer), _layer_block((2, GROUP_W), layer),
            _layer_block((2, N_HEADS, HEAD_DIM, HEAD_DIM), layer), _layer_block((2, GROUP_W), layer),
            _layer_block((2, GROUP_W), layer),
            _layer_block((2, N_HEADS), layer), _full((DEPTH, GROUP_W)),
        ],
        out_specs=[pl.BlockSpec((SEQ, D_MODEL), lambda b: (b, 0))] + [
            per_request(layer + 1, dims) for dims in state_dims],
        out_shape=[jax.ShapeDtypeStruct((N_ROWS, D_MODEL), F32)] + [
            jax.ShapeDtypeStruct((BATCH, layer + 1) + dims, F32) for dims in state_dims],
        input_output_aliases={len(prev): 0},
        scratch_shapes=[pltpu.VMEM((D_MODEL, IN_WIDTH), BF16), pltpu.VMEM((D_MODEL, D_MODEL), BF16),
                        pltpu.VMEM((SEQ, IN_WIDTH), F32), pltpu.VMEM((SEQ, D_MODEL), BF16),
                        pltpu.VMEM((4, GROUP_W, GROUP_W), BF16)] + [scr] * 7,
        compiler_params=_cparams("arbitrary"),
        name="ctx_mixers",
    )(*prev, x, norm2_g, mod, mod, mod, w_in, w_out,
      a_qn, a_kn, b_qn, b_kn, a_sink, c_conv_w, c_conv_b, c_wa, c_ba, c_wx, c_bx,
      c_lambda, d_theta, d_norm_g)
    return out[0], tuple(out[1:])


LAT_BLOCK0 = N_CTX_ROWS // DEC_SEQ


def _lat_attn_kernel(x_ref, n2_ref, sh_ref, sc_ref, g2_ref, win_ref, wout_ref,
                     kca_ref, vca_ref, kcb_ref, vcb_ref,
                     aqn_ref, akn_ref, bqn_ref, bkn_ref, sink_ref, cos_ref, sinl_ref, sinh_ref,
                     xn_ref, h_ref, u_ref, o_ref, *, layer):
    t = DEC_SEQ
    lrow = slice(layer, layer + 1)
    mrow = pl.ds(1 + pl.program_id(0), 1)
    cos, sin_lo, sin_hi = cos_ref[...], sinl_ref[...], sinh_ref[...]
    scale = HEAD_DIM ** -0.5
    x = x_ref[...]
    h_ref[...] = _norm_mod(x, n2_ref[lrow, :], sc_ref[mrow, :], sh_ref[mrow, :]).astype(BF16)
    u_ref[...] = _dot(h_ref[...], win_ref[...].astype(BF16))

    q = _rope(_head_norm(u_ref[:, COL_AQ:COL_AQ + 256], aqn_ref[lrow, :]), cos, sin_lo, sin_hi)
    k = _rope(_head_norm(u_ref[:, COL_AK:COL_AK + 128], akn_ref[lrow, :]), cos, sin_lo, sin_hi)
    qh = [_head_cols(q * scale, h) for h in range(4)]
    v = u_ref[:, COL_AV:COL_AV + 128]
    kh = [_head_cols(k, kv) for kv in range(2)]
    vh = [_head_cols(v, kv) for kv in range(2)]
    kch = [_head_cols(kca_ref[...], kv) for kv in range(2)]
    vch = [_head_cols(vca_ref[...], kv) for kv in range(2)]
    w = ATT_BLOCK
    span = 3 * w
    for n in range(t // w):
        start = min(max((n - 1) * w, 0), t - span)
        rows = slice(n * w, (n + 1) * w)
        band = slice(start, start + span)
        qpos = (lax.broadcasted_iota(jnp.int32, (2 * w, span), 0) & (w - 1)) + n * w
        kpos = lax.broadcasted_iota(jnp.int32, (2 * w, span), 1) + start
        valid = jnp.abs(qpos - kpos) <= WINDOW
        heads = []
        for kv in range(2):
            qp = jnp.concatenate([qh[2 * kv][rows, :], qh[2 * kv + 1][rows, :]], axis=0)
            s_ctx = _dot_nt(qp, kch[kv])
            s_band = jnp.where(valid, _dot_nt(qp, kh[kv][band, :]), NEG_INF)
            row = lax.broadcasted_iota(jnp.int32, (2 * w, 1), 0)
            sink = jnp.where(row < w, sink_ref[layer, 2 * kv], sink_ref[layer, 2 * kv + 1])
            o = _softmax_pv([s_ctx, s_band], [vch[kv], vh[kv][band, :]], sink)
            heads += [o[0:w, :], o[w:2 * w, :]]
        o_ref[rows, 0:GROUP_W] = jnp.concatenate(heads, axis=-1).astype(BF16)

    q = _rope(_head_norm(u_ref[:, COL_BQ:COL_BQ + 256], bqn_ref[lrow, :]), cos, sin_lo, sin_hi)
    k = _rope(_head_norm(u_ref[:, COL_BK:COL_BK + 128], bkn_ref[lrow, :]), cos, sin_lo, sin_hi)
    qh = [_head_cols(q * scale, h) for h in range(4)]
    v = u_ref[:, COL_BV:COL_BV + 128]
    kh = [_head_cols(k, kv) for kv in range(2)]
    vh = [_head_cols(v, kv) for kv in range(2)]
    kch = [_head_cols(kcb_ref[...], kv) for kv in range(2)]
    vch = [_head_cols(vcb_ref[...], kv) for kv in range(2)]
    tq = 256
    for n in range(t // tq):
        rows = slice(n * tq, (n + 1) * tq)
        heads = []
        for kv in range(2):
            qp = jnp.concatenate([qh[2 * kv][rows, :], qh[2 * kv + 1][rows, :]], axis=0)
            o = _softmax_pv([_dot_nt(qp, kch[kv]), _dot_nt(qp, kh[kv])], [vch[kv], vh[kv]], None)
            heads += [o[0:tq, :], o[tq:2 * tq, :]]
        o_ref[rows, GROUP_W:2 * GROUP_W] = jnp.concatenate(heads, axis=-1).astype(BF16)

    xn_ref[...] = x + g2_ref[mrow, :] * _dot(o_ref[...], wout_ref[...].astype(BF16))


def _lat_rglru_kernel(xn_in_ref, h_ref, g2_ref, win_ref, wout_ref, h0_ref,
                      convw_ref, convb_ref, wa_ref, ba_ref, wx_ref, bx_ref, lam_ref,
                      xn_ref, gate_w_ref, af_ref, bf_ref, ab_ref, bb_ref, hf_ref, hb_ref, *, layer):
    lrow = slice(layer, layer + 1)
    mrow = pl.ds(1 + pl.program_id(0), 1)

    @pl.when(pl.program_id(0) == 0)
    def _():
        _store_gate_weights(gate_w_ref, wa_ref, wx_ref)

    u = _dot(h_ref[...], win_ref[...].astype(BF16))
    oc, _, _ = _rglru_mixer(u[:, 0:GROUP_W], u[:, GROUP_W:2 * GROUP_W], convw_ref, convb_ref[lrow, :],
                            gate_w_ref, ba_ref, bx_ref, lam_ref,
                            h0_ref[0:1, :], h0_ref[1:2, :],
                            af_ref, bf_ref, ab_ref, bb_ref, hf_ref, hb_ref)
    xn_ref[...] = xn_in_ref[...] + g2_ref[mrow, :] * _dot(oc.astype(BF16), wout_ref[...].astype(BF16))


def _lat_retention_kernel(xn_in_ref, h_ref, g2_ref, wqk_ref, wvg_ref, wout_ref,
                          s0_ref, theta_ref, dn_ref, xn_ref, o_ref, *, layer):
    lrow = slice(layer, layer + 1)
    mrow = pl.ds(1 + pl.program_id(0), 1)
    h = h_ref[...]
    uqk = _dot(h, wqk_ref[...].astype(BF16))
    uvg = _dot(h, wvg_ref[...].astype(BF16))
    masks = _lane_head_masks(GROUP_W)
    lgf, lgb = _log_decays(theta_ref, masks)
    s0 = tuple(_block_diag([s0_ref[d, hd] for hd in range(N_HEADS)]) for d in range(2))
    _retention(uqk[:, 0:GROUP_W], uqk[:, GROUP_W:2 * GROUP_W] * (HEAD_DIM ** -0.5),
               uvg[:, 0:GROUP_W].astype(BF16), s0, lgf, lgb, masks, o_ref)
    o = o_ref[...]
    o = o * lax.rsqrt(_head_mean_square(o) + EPS) * dn_ref[lrow, :] * _silu(uvg[:, GROUP_W:2 * GROUP_W])
    xn_ref[...] = xn_in_ref[...] + g2_ref[mrow, :] * _dot(o.astype(BF16), wout_ref[...].astype(BF16))


def _lat_mixers(x, mod, layer, caches, state_c, state_d, rope, norm2_g, w_in, w_out,
                a_qn, a_kn, a_sink, b_qn, b_kn, c_conv_w, c_conv_b, c_wa, c_ba, c_wx, c_bx,
                c_lambda, d_theta, d_norm_g):
    rows = pl.BlockSpec((DEC_SEQ, D_MODEL), lambda b: (LAT_BLOCK0 + b, 0))
    h_rows = pl.BlockSpec((DEC_SEQ, D_MODEL), lambda b: (b, 0))
    cache_spec = pl.BlockSpec((None, None, PAST_LEN, 128), lambda b: (b, layer, 0, 0))
    gain = _full((DEPTH, HEAD_DIM))
    table = _once((DEC_SEQ, 128), lambda b: (0, 0))
    out_shape = jax.ShapeDtypeStruct((N_ROWS, D_MODEL), F32)
    win_cols = lambda w, c: _once((None, D_MODEL, w), lambda b: (layer, 0, c))
    wout_rows = lambda h, r: _once((None, h, D_MODEL), lambda b: (layer, r, 0))

    xn, h = pl.pallas_call(
        functools.partial(_lat_attn_kernel, layer=layer),
        grid=(DEC_BATCH,),
        in_specs=[rows, _full((DEPTH, D_MODEL)),
                  _mod_chunk(layer, 3), _mod_chunk(layer, 4), _mod_chunk(layer, 5),
                  win_cols(4 * GROUP_W, 0), wout_rows(2 * GROUP_W, 0),
                  cache_spec, cache_spec, cache_spec, cache_spec,
                  gain, gain, gain, gain,
                  pl.BlockSpec(memory_space=pltpu.SMEM),
                  table, table, table],
        out_specs=[rows, pl.BlockSpec((DEC_SEQ, D_MODEL), lambda b: (b, 0), pipeline_mode=pl.Buffered(1))],
        out_shape=[out_shape, jax.ShapeDtypeStruct((N_LAT_ROWS, D_MODEL), BF16)],
        input_output_aliases={0: 0},
        scratch_shapes=[pltpu.VMEM((DEC_SEQ, 4 * GROUP_W), F32), pltpu.VMEM((DEC_SEQ, 2 * GROUP_W), BF16)],
        compiler_params=_cparams("arbitrary"),
        name="lat_attention",
    )(x, norm2_g, mod, mod, mod, w_in, w_out, *caches, a_qn, a_kn, b_qn, b_kn, a_sink, *rope)

    common = [rows, h_rows, _mod_chunk(layer, 5)]
    scr = pltpu.VMEM((DEC_SEQ, GROUP_W), F32)
    xn = pl.pallas_call(
        functools.partial(_lat_rglru_kernel, layer=layer),
        grid=(DEC_BATCH,),
        in_specs=common + [
            win_cols(2 * GROUP_W, COL_CX // (2 * GROUP_W)), wout_rows(GROUP_W, 2),
            pl.BlockSpec((None, None, 2, GROUP_W), lambda b: (b, layer, 0, 0)),
            _layer_block((4, GROUP_W), layer), _full((DEPTH, GROUP_W)),
            _layer_block((2, N_HEADS, HEAD_DIM, HEAD_DIM), layer), _layer_block((2, GROUP_W), layer),
            _layer_block((2, N_HEADS, HEAD_DIM, HEAD_DIM), layer), _layer_block((2, GROUP_W), layer),
            _layer_block((2, GROUP_W), layer)],
        out_specs=rows,
        out_shape=out_shape,
        input_output_aliases={0: 0},
        scratch_shapes=[pltpu.VMEM((4, GROUP_W, GROUP_W), BF16)] + [scr] * 6,
        compiler_params=_cparams("arbitrary"),
        name="lat_rglru",
    )(xn, h, mod, w_in, w_out, state_c, c_conv_w, c_conv_b, c_wa, c_ba, c_wx, c_bx, c_lambda)

    xn = pl.pallas_call(
        functools.partial(_lat_retention_kernel, layer=layer),
        grid=(DEC_BATCH,),
        in_specs=common + [
            win_cols(2 * GROUP_W, COL_DQ // (2 * GROUP_W)), win_cols(2 * GROUP_W, COL_DV // (2 * GROUP_W)),
            wout_rows(GROUP_W, 3),
            pl.BlockSpec((None, None, 2, N_HEADS, HEAD_DIM, HEAD_DIM), lambda b: (b, layer, 0, 0, 0, 0)),
            _layer_block((2, N_HEADS), layer), _full((DEPTH, GROUP_W))],
        out_specs=rows,
        out_shape=out_shape,
        input_output_aliases={0: 0},
        scratch_shapes=[pltpu.VMEM((DEC_SEQ, GROUP_W), F32)],
        compiler_params=_cparams("arbitrary"),
        name="lat_retention",
    )(xn, h, mod, w_in, w_in, w_out, state_d, d_theta, d_norm_g)
    return xn


def _rope_tables():
    t = np.arange(DEC_SEQ)
    row = (t // GRID_W).astype(np.float64)[:, None]
    col = (t % GRID_W).astype(np.float64)[:, None]
    half = HEAD_DIM // 2
    inv = 1.0 / (ROPE_BASE ** (np.arange(0, half, 2, dtype=np.float64) / half))
    j = np.arange(128) % HEAD_DIM
    ang = np.where((j < half)[None, :], row, col) * inv[j % (half // 2)][None, :]
    first = ((j % half) < half // 2)[None, :]
    cos, sin = np.cos(ang), np.sin(ang)
    return tuple(jnp.asarray(a, F32) for a in (cos, np.where(first, -sin, 0.0), np.where(first, 0.0, sin)))


def kernel(x_prompt, x_sample, cache_a_k, cache_a_v, cache_b_k, cache_b_v, state_c, state_d, c, c_ctx, norm1_g, norm2_g, norm3_g, w_mod, b_mod, ffn1_wg, ffn1_wu, ffn1_wd, ffn2_wg, ffn2_wu, ffn2_wd, w_in, w_out, a_qn, a_kn, a_sink, b_qn, b_kn, c_conv_w, c_conv_b, c_wa, c_ba, c_wx, c_bx, c_lambda, d_theta, d_norm_g):
    mod = _modulation(c_ctx, c, w_mod, b_mod)
    rope = _rope_tables()
    caches = tuple(t.reshape(DEC_BATCH, DEPTH, PAST_LEN, 128) for t in (cache_a_k, cache_a_v, cache_b_k, cache_b_v))
    mixer_params = (a_qn, a_kn, a_sink, b_qn, b_kn, c_conv_w, c_conv_b, c_wa, c_ba, c_wx, c_bx,
                    c_lambda, d_theta, d_norm_g)
    xs = (x_prompt.reshape(N_CTX_ROWS, D_MODEL), x_sample.reshape(N_LAT_ROWS, D_MODEL))
    states = ()
    for l in range(DEPTH):
        (x,) = _ffn(xs, mod, l, 0, norm1_g, ffn1_wg, ffn1_wu, ffn1_wd)
        x, states = _ctx_mixers(x, mod, l, states, norm2_g, w_in, w_out, *mixer_params)
        x = _lat_mixers(x, mod, l, caches, state_c, state_d, rope, norm2_g, w_in, w_out, *mixer_params)
        xs = _ffn((x,), mod, l, 6, norm3_g, ffn2_wg, ffn2_wu, ffn2_wd, split_out=(l == DEPTH - 1))
    y_p, y_s = xs
    ka, va, kb, vb, st_c, st_d = states
    kv_shape = (BATCH, DEPTH, SEQ, 2, HEAD_DIM)
    return (y_p.reshape(BATCH, SEQ, D_MODEL), y_s.reshape(DEC_BATCH, DEC_SEQ, D_MODEL),
            ka.reshape(kv_shape), va.reshape(kv_shape), kb.reshape(kv_shape), vb.reshape(kv_shape),
            st_c, st_d)
```

```python
import functools
import math

import numpy as np
import jax
import jax.numpy as jnp
from jax import lax
from jax.experimental import pallas as pl
from jax.experimental.pallas import tpu as pltpu

F32 = jnp.float32
BF16 = jnp.bfloat16

D_MODEL = 1024
BATCH = 16
SEQ = 256
DEPTH = 2
DEC_BATCH = 2
DEC_SEQ = 1024
PAST_LEN = 512
GRID_W = 64
HEAD_DIM = 64
HEAD_SHIFT = 6
N_HEADS = 4
GROUP_W = 256
WINDOW = 128
ATT_BLOCK = 128
ROPE_BASE = 10000.0
LRU_C = 8.0
D_FF = 2816
N_MOD = 9
EPS = 1e-6
NEG_INF = -1e30
IN_WIDTH = 2560

N_CTX_ROWS = BATCH * SEQ
N_LAT_ROWS = DEC_BATCH * DEC_SEQ
N_ROWS = N_CTX_ROWS + N_LAT_ROWS
MOD_ROWS = 8
MOD_GROUP = 1024

VMEM_LIMIT_BYTES = 56 * 1024 * 1024

COL_AQ, COL_AK, COL_AV = 0, 256, 384
COL_BQ, COL_BK, COL_BV = 512, 768, 896
COL_CX, COL_CY = 1024, 1280
COL_DQ, COL_DK, COL_DV, COL_DG = 1536, 1792, 2048, 2304


def _cparams(*sem):
    return pltpu.CompilerParams(dimension_semantics=sem, vmem_limit_bytes=VMEM_LIMIT_BYTES)


def _dot(a, b):
    return jnp.dot(a, b, preferred_element_type=F32)


def _dot_nt(a, b):
    return lax.dot_general(a, b, (((1,), (1,)), ((), ())), preferred_element_type=F32)


def _dot_tn(a, b):
    return lax.dot_general(a, b, (((0,), (0,)), ((), ())), preferred_element_type=F32)


def _silu(x):
    return x * jax.nn.sigmoid(x)


def _gelu_tanh(x):
    return 0.5 * x * (1.0 + jnp.tanh(math.sqrt(2.0 / math.pi) * (x + 0.044715 * (x * x * x))))


def _mod_row(i, tm, s):
    if tm >= MOD_GROUP:
        block_index = i * (tm // MOD_GROUP) + s
    else:
        block_index = i >> int(math.log2(MOD_GROUP // tm))
    return jnp.maximum(block_index - (N_CTX_ROWS // MOD_GROUP - 1), 0)


def _norm_mod(x, g, sc, sh):
    ms = jnp.mean(x * x, axis=-1, keepdims=True)
    return (x * lax.rsqrt(ms + EPS) * g) * (1.0 + sc) + sh


def _full(shape):
    return pl.BlockSpec(shape, lambda *_: (0,) * len(shape))


def _layer_block(shape, layer):
    return pl.BlockSpec((None,) + shape, lambda *_: (layer,) + (0,) * len(shape))


MOD_TN = 1024


def _mod_kernel(cc_ref, c_ref, w_ref, b_ref, o_ref):
    l = pl.program_id(0)
    pad = jnp.zeros((MOD_ROWS - 1 - DEC_BATCH, D_MODEL), F32)
    cond = jnp.concatenate([cc_ref[...], c_ref[...], pad], axis=0)
    o_ref[...] = _dot(_silu(cond).astype(BF16), w_ref[...].astype(BF16)) + b_ref[pl.ds(l, 1), :]


def _modulation(c_ctx, c, w_mod, b_mod):
    n = N_MOD * D_MODEL
    return pl.pallas_call(
        _mod_kernel,
        grid=(DEPTH, n // MOD_TN),
        in_specs=[
            pl.BlockSpec((1, D_MODEL), lambda l, j: (0, 0)),
            pl.BlockSpec((DEC_BATCH, D_MODEL), lambda l, j: (0, 0)),
            pl.BlockSpec((None, D_MODEL, MOD_TN), lambda l, j: (l, 0, j)),
            pl.BlockSpec((DEPTH, MOD_TN), lambda l, j: (0, j)),
        ],
        out_specs=pl.BlockSpec((None, MOD_ROWS, MOD_TN), lambda l, j: (l, 0, j)),
        out_shape=jax.ShapeDtypeStruct((DEPTH, MOD_ROWS, n), F32),
        compiler_params=_cparams("arbitrary", "arbitrary"),
        name="modulation",
    )(c_ctx.reshape(1, D_MODEL), c, w_mod, b_mod)


FFN_TM = 1024
FFN_TF = 256
N_CTX_TILES = N_CTX_ROWS // FFN_TM


FFN_NJ = D_FF // FFN_TF
N_FFN_TILES = N_ROWS // FFN_TM
N_FFN_STEPS = FFN_NJ + N_FFN_TILES - 1


def _ffn_tile(step):
    return jnp.maximum(step - (FFN_NJ - 1), 0)


def _on_stream_part(tile, x_refs, o_refs, fn):
    if len(x_refs) == 1 and len(o_refs) == 1:
        fn(x_refs[0], o_refs[0])
    else:
        pl.when(tile < N_CTX_TILES)(lambda: fn(x_refs[0], o_refs[0]))
        pl.when(tile >= N_CTX_TILES)(lambda: fn(x_refs[-1], o_refs[-1]))


def _ffn_kernel(*refs, layer, n_in, n_out):
    x_refs = refs[:n_in]
    n_ref, sh_ref, sc_ref, g_ref, wg_ref, wu_ref, wd_ref = refs[n_in:n_in + 7]
    o_refs = refs[n_in + 7:n_in + 7 + n_out]
    h_ref, a_ref, wg_s, wu_s, wd_s = refs[n_in + 7 + n_out:]
    nj, tf = FFN_NJ, FFN_TF
    s = pl.program_id(0)
    tile = _ffn_tile(s)
    r = _mod_row(tile, FFN_TM, 0)

    def load_tile():
        def init(x_ref, _):
            h = _norm_mod(x_ref[...], n_ref[layer:layer + 1, :], sc_ref[pl.ds(r, 1), :], sh_ref[pl.ds(r, 1), :])
            h_ref[...] = h.astype(BF16)
        _on_stream_part(tile, x_refs, o_refs, init)

    def up_chunk(j, cols):
        h = h_ref[...]
        a_ref[:, cols] = (_silu(_dot(h, wg_s[j])) * _dot(h, wu_s[j])).astype(BF16)

    def down_and_store():
        y = (0.5 * g_ref[pl.ds(r, 1), :]) * _dot(a_ref[...], wd_s[...])

        def store(x_ref, o_ref):
            o_ref[...] = x_ref[...] + y
        _on_stream_part(tile, x_refs, o_refs, store)

    @pl.when(s == 0)
    def _():
        load_tile()

    @pl.when(s < nj)
    def _():
        lo = pl.multiple_of(s * tf, tf)
        wg_s[s] = wg_ref[...].astype(BF16)
        wu_s[s] = wu_ref[...].astype(BF16)
        wd_s[pl.ds(lo, tf), :] = wd_ref[...].astype(BF16)
        up_chunk(s, pl.ds(lo, tf))

    @pl.when(s == nj - 1)
    def _():
        down_and_store()

    @pl.when(s >= nj)
    def _():
        load_tile()
        for j in range(nj):
            up_chunk(j, slice(j * tf, (j + 1) * tf))
        down_and_store()


def _stream_specs(split, buffered_once):
    tm = FFN_TM
    kw = {"pipeline_mode": pl.Buffered(1)} if buffered_once else {}
    if not split:
        return [pl.BlockSpec((tm, D_MODEL), lambda s: (_ffn_tile(s), 0), **kw)]
    last_ctx = N_CTX_TILES - 1
    return [pl.BlockSpec((tm, D_MODEL), lambda s: (jnp.minimum(_ffn_tile(s), last_ctx), 0), **kw),
            pl.BlockSpec((tm, D_MODEL), lambda s: (jnp.maximum(_ffn_tile(s) - N_CTX_TILES, 0), 0), **kw)]


def _ffn(xs, mod, layer, chunk0, norm_g, wg, wu, wd, split_out=False):
    tm, tf, nj = FFN_TM, FFN_TF, FFN_NJ
    split_in = len(xs) == 2
    mod_spec = lambda c: pl.BlockSpec((None, MOD_ROWS, D_MODEL), lambda s: (layer, 0, c))
    w_col = lambda s: (layer, 0, jnp.minimum(s, nj - 1))
    w_row = lambda s: (layer, jnp.minimum(s, nj - 1), 0)
    if split_out:
        out_shape = [jax.ShapeDtypeStruct((N_CTX_ROWS, D_MODEL), F32),
                     jax.ShapeDtypeStruct((N_LAT_ROWS, D_MODEL), F32)]
    else:
        out_shape = [jax.ShapeDtypeStruct((N_ROWS, D_MODEL), F32)]
    out = pl.pallas_call(
        functools.partial(_ffn_kernel, layer=layer, n_in=len(xs), n_out=len(out_shape)),
        grid=(N_FFN_STEPS,),
        in_specs=_stream_specs(split_in, split_in) + [
            _full((DEPTH, D_MODEL)),
            mod_spec(chunk0), mod_spec(chunk0 + 1), mod_spec(chunk0 + 2),
            pl.BlockSpec((None, D_MODEL, tf), w_col),
            pl.BlockSpec((None, D_MODEL, tf), w_col),
            pl.BlockSpec((None, tf, D_MODEL), w_row),
        ],
        out_specs=_stream_specs(split_out, True),
        out_shape=out_shape,
        scratch_shapes=[pltpu.VMEM((tm, D_MODEL), BF16),
                        pltpu.VMEM((tm, D_FF), BF16),
                        pltpu.VMEM((nj, D_MODEL, tf), BF16),
                        pltpu.VMEM((nj, D_MODEL, tf), BF16),
                        pltpu.VMEM((D_FF, D_MODEL), BF16)],
        compiler_params=_cparams("arbitrary"),
        name="ffn",
    )(*xs, norm_g, mod, mod, mod, wg, wu, wd)
    return tuple(out)


def _once(shape, index_map):
    return pl.BlockSpec(shape, index_map, pipeline_mode=pl.Buffered(1))


def _mod_chunk(layer, c):
    return pl.BlockSpec((None, MOD_ROWS, D_MODEL), lambda *_: (layer, 0, c))


def _project_in(x, n_row, sc_row, sh_row, w_bf16):
    return _dot(_norm_mod(x, n_row, sc_row, sh_row).astype(BF16), w_bf16)

def _head_mean_square(x):
    n = x.shape[-1]
    x2 = x * x
    hi = x2.astype(BF16)
    lo = (x2 - hi.astype(F32)).astype(BF16)
    r = lax.broadcasted_iota(jnp.int32, (n, n), 0) >> HEAD_SHIFT
    c = lax.broadcasted_iota(jnp.int32, (n, n), 1) >> HEAD_SHIFT
    ones_bd = jnp.where(r == c, 1.0, 0.0).astype(BF16)
    return (_dot(hi, ones_bd) + _dot(lo, ones_bd)) * (1.0 / HEAD_DIM)


def _head_norm(x, head_gain):
    gain_row = jnp.concatenate([head_gain] * (x.shape[-1] // HEAD_DIM), axis=-1)
    return x * lax.rsqrt(_head_mean_square(x) + EPS) * gain_row


def _head_cols(x, h):
    return x[:, h * HEAD_DIM:(h + 1) * HEAD_DIM].astype(BF16)


def _rope(x, cos, sin_lo, sin_hi):
    cols = []
    for c in range(x.shape[-1] // 128):
        xc = x[:, c * 128:(c + 1) * 128]
        cols.append(xc * cos + pltpu.roll(xc, 112, 1) * sin_lo + pltpu.roll(xc, 16, 1) * sin_hi)
    return cols[0] if len(cols) == 1 else jnp.concatenate(cols, axis=-1)


def _softmax_pv(scores, values, sink):
    m = jnp.max(scores[0], axis=-1, keepdims=True)
    for s in scores[1:]:
        m = jnp.maximum(m, jnp.max(s, axis=-1, keepdims=True))
    if sink is not None:
        m = jnp.maximum(m, sink)
    denom = None
    acc = None
    for s, v in zip(scores, values):
        p = jnp.exp(s - m)
        d = jnp.sum(p, axis=-1, keepdims=True)
        o = _dot(p.astype(BF16), v)
        denom = d if denom is None else denom + d
        acc = o if acc is None else acc + o
    if sink is not None:
        denom = denom + jnp.exp(sink - m)
    return acc / denom


def _block_diag(blocks):
    n = len(blocks)
    w = blocks[0].shape[0]
    rows = []
    for k, blk in enumerate(blocks):
        parts = []
        if k > 0:
            parts.append(jnp.zeros((w, k * w), F32))
        parts.append(blk)
        if k < n - 1:
            parts.append(jnp.zeros((w, (n - 1 - k) * w), F32))
        rows.append(jnp.concatenate(parts, axis=-1))
    return jnp.concatenate(rows, axis=0)


def _rglru_gates(xc, wa, ba, wx, bx, lam):
    xb = xc.astype(BF16)
    r = jax.nn.sigmoid(_dot(xb, wa) + ba)
    i = jax.nn.sigmoid(_dot(xb, wx) + bx)
    softplus = jnp.maximum(-lam, 0.0) + jnp.log1p(jnp.exp(-jnp.abs(lam)))
    log_a = (-LRU_C) * r * softplus
    a = jnp.exp(log_a)
    b = jnp.sqrt(1.0 - a * a) * (i * xc)
    return a, b


def _block_prefix(a, b, reverse):
    t = a.shape[0]
    row = lax.broadcasted_iota(jnp.int32, a.shape, 0) & 7
    for d in (1, 2, 4):
        if reverse:
            a_s = pltpu.roll(a, t - d, 0)
            b_s = pltpu.roll(b, t - d, 0)
            ok = row < 8 - d
        else:
            a_s = pltpu.roll(a, d, 0)
            b_s = pltpu.roll(b, d, 0)
            ok = row >= d
        b = jnp.where(ok, a * b_s + b, b)
        a = jnp.where(ok, a * a_s, a)
    return a, b


def _conv4(x, w_ref, b_row):
    t = x.shape[0]
    row = lax.broadcasted_iota(jnp.int32, x.shape, 0)
    xm2 = jnp.where(row >= 2, pltpu.roll(x, 2, 0), 0.0)
    xm1 = jnp.where(row >= 1, pltpu.roll(x, 1, 0), 0.0)
    xp1 = jnp.where(row < t - 1, pltpu.roll(x, t - 1, 0), 0.0)
    return (xm2 * w_ref[0:1, :] + xm1 * w_ref[1:2, :] + x * w_ref[2:3, :] + xp1 * w_ref[3:4, :]) + b_row


def _rglru_prepare(cx, cy, conv_w_ref, conv_b, gate_w_ref, ba_ref, bx_ref, lam_ref,
                   af_ref, bf_ref, ab_ref, bb_ref, gel_ref):
    xc = _conv4(cx, conv_w_ref, conv_b)
    a, b = _rglru_gates(xc, gate_w_ref[0], ba_ref[0:1, :], gate_w_ref[1], bx_ref[0:1, :], lam_ref[0:1, :])
    a, b = _block_prefix(a, b, reverse=False)
    af_ref[...] = a
    bf_ref[...] = b
    a, b = _rglru_gates(xc, gate_w_ref[2], ba_ref[1:2, :], gate_w_ref[3], bx_ref[1:2, :], lam_ref[1:2, :])
    a, b = _block_prefix(a, b, reverse=True)
    ab_ref[...] = a
    bb_ref[...] = b
    gel_ref[...] = _gelu_tanh(cy)


def _rglru_finish(h0f, h0b, af_ref, bf_ref, ab_ref, bb_ref, hf_ref, hb_ref, gel_ref):
    nblk = af_ref.shape[0] // 8

    def body(k, carry):
        cf, cb = carry
        rf = pl.ds(pl.multiple_of(k * 8, 8), 8)
        hf = bf_ref[rf, :] + af_ref[rf, :] * cf
        hf_ref[rf, :] = hf
        rb = pl.ds(pl.multiple_of((nblk - 1 - k) * 8, 8), 8)
        hb = bb_ref[rb, :] + ab_ref[rb, :] * cb
        hb_ref[rb, :] = hb
        return hf[7:8, :], hb[0:1, :]

    cf, cb = lax.fori_loop(0, nblk, body, (h0f, h0b))
    oc = (hf_ref[...] + hb_ref[...]) * gel_ref[...]
    return oc, cf, cb


def _store_gate_weights(gate_w_ref, wa_ref, wx_ref):
    for d in range(2):
        gate_w_ref[2 * d] = _block_diag([wa_ref[d, n] for n in range(N_HEADS)]).astype(BF16)
        gate_w_ref[2 * d + 1] = _block_diag([wx_ref[d, n] for n in range(N_HEADS)]).astype(BF16)


def _lane_head_masks(n):
    lane = lax.broadcasted_iota(jnp.int32, (1, n), 1) >> HEAD_SHIFT
    return [jnp.where(lane == h, 1.0, 0.0) for h in range(n // HEAD_DIM)]


def _log_decays(theta_ref, masks):
    theta = theta_ref[...]
    lanes = theta[:, 0:1] * masks[0]
    for h in range(1, N_HEADS):
        lanes = lanes + theta[:, h:h + 1] * masks[h]
    lg = jnp.log1p(-jnp.exp(lanes))
    return lg[0:1, :], lg[1:2, :]


RET_BLOCK = 256


def _retention(q, k8, vb, s0, lgf, lgb, masks, o_ref):
    t, w = q.shape
    c = RET_BLOCK
    nh = w // HEAD_DIM
    pos = lax.broadcasted_iota(jnp.int32, (c, w), 0).astype(F32)
    q_dec = (jnp.exp(lgf * (pos + 1.0)), jnp.exp(lgb * (float(c) - pos)))
    k_dec = (jnp.exp(lgf * (float(c - 1) - pos)), jnp.exp(lgb * pos))
    chunk_dec = (jnp.exp(lgf * float(c)), jnp.exp(lgb * float(c)))
    rel = (lax.broadcasted_iota(jnp.int32, (c, c), 0) - lax.broadcasted_iota(jnp.int32, (c, c), 1)).astype(F32)
    decs = []
    for h in range(nh):
        gf = lgf[:, h * HEAD_DIM:h * HEAD_DIM + 1]
        gb = lgb[:, h * HEAD_DIM:h * HEAD_DIM + 1]
        e = jnp.exp(jnp.where(rel >= 0, gf * rel, gb * (-rel)))
        decs.append(jnp.where(rel == 0, 2.0, e))
    dec = jnp.concatenate(decs, axis=0)
    r_head = lax.broadcasted_iota(jnp.int32, (w, w), 0) >> HEAD_SHIFT
    c_head = lax.broadcasted_iota(jnp.int32, (w, w), 1) >> HEAD_SHIFT
    same_head = jnp.where(r_head == c_head, 1.0, 0.0)
    states = [None, None] if s0 is None else list(s0)

    def carry(d, rows, o):
        if states[d] is not None:
            o = o + _dot((q[rows, :] * q_dec[d]).astype(BF16), states[d].astype(BF16))
        upd = _dot_tn((k8[rows, :] * k_dec[d]).astype(BF16), vb[rows, :]) * same_head
        states[d] = upd if states[d] is None else states[d] * chunk_dec[d] + upd
        return o

    for ci in range(t // c):
        rows = slice(ci * c, (ci + 1) * c)
        qc = q[rows, :]
        q_stack = jnp.concatenate([(qc * masks[h]).astype(BF16) for h in range(nh)], axis=0)
        inner = (_dot_nt(q_stack, k8[rows, :].astype(BF16)) * dec).astype(BF16)
        out = _dot(inner, vb[rows, :])
        o = out[0:c, :] * masks[0]
        for h in range(1, nh):
            o = o + out[h * c:(h + 1) * c, :] * masks[h]
        o_ref[rows, :] = carry(0, rows, o)
    for ci in reversed(range(t // c)):
        rows = slice(ci * c, (ci + 1) * c)
        if states[1] is not None:
            o_ref[rows, :] = carry(1, rows, o_ref[rows, :])
        else:
            carry(1, rows, None)
    return states[0], states[1]


def _ctx_mixer_kernel(*refs, layer, n_prev):
    prev_refs = refs[:n_prev]
    (x_ref, n2_ref, sh_ref, sc_ref, g2_ref, win_ref, wout_ref,
     aqn_ref, akn_ref, bqn_ref, bkn_ref, sink_ref,
     convw_ref, convb_ref, wa_ref, ba_ref, wx_ref, bx_ref, lam_ref, theta_ref, dn_ref,
     xn_ref, *state_refs) = refs[n_prev:n_prev + 28]
    (win_s, wout_s, gate_w_ref, af_ref, bf_ref, ab_ref, bb_ref, hf_ref, hb_ref, gel_ref, ret_ref) = refs[n_prev + 28:]
    t = SEQ
    lrow = slice(layer, layer + 1)
    for prev_ref, state_ref in zip(prev_refs, state_refs):
        for earlier in range(layer):
            state_ref[earlier] = prev_ref[earlier]
    ka_ref, va_ref, kb_ref, vb_ref, stc_ref, std_ref = (ref.at[layer] for ref in state_refs)

    @pl.when(pl.program_id(0) == 0)
    def _():
        for c in range(IN_WIDTH // 512):
            win_s[:, c * 512:(c + 1) * 512] = win_ref[:, c * 512:(c + 1) * 512].astype(BF16)
        wout_s[...] = wout_ref[...].astype(BF16)
        _store_gate_weights(gate_w_ref, wa_ref, wx_ref)

    x = x_ref[...]
    h = _norm_mod(x, n2_ref[lrow, :], sc_ref[0:1, :], sh_ref[0:1, :]).astype(BF16)

    u = _dot(h, win_s[:, COL_CX:COL_CX + 2 * GROUP_W])
    _rglru_prepare(u[:, 0:GROUP_W], u[:, GROUP_W:2 * GROUP_W], convw_ref, convb_ref[lrow, :],
                   gate_w_ref, ba_ref, bx_ref, lam_ref, af_ref, bf_ref, ab_ref, bb_ref, gel_ref)

    u = _dot(h, win_s[:, 0:4 * GROUP_W])
    groups = []
    for (cq, ck, cv, qn_ref, kn_ref, k_out, v_out, use_sink) in (
            (COL_AQ, COL_AK, COL_AV, aqn_ref, akn_ref, ka_ref, va_ref, True),
            (COL_BQ, COL_BK, COL_BV, bqn_ref, bkn_ref, kb_ref, vb_ref, False)):
        q = _head_norm(u[:, cq:cq + 256], qn_ref[lrow, :])
        k = _head_norm(u[:, ck:ck + 128], kn_ref[lrow, :])
        v = u[:, cv:cv + 128]
        k_out[...] = k
        v_out[...] = v
        qs = q * (HEAD_DIM ** -0.5)
        for hd in range(N_HEADS):
            kv = hd // 2
            s = _dot_nt(_head_cols(qs, hd), _head_cols(k, kv))
            sink = jnp.full((t, 1), sink_ref[layer, hd], F32) if use_sink else None
            groups.append(_softmax_pv([s], [_head_cols(v, kv)], sink))
    y = _dot(jnp.concatenate(groups, axis=-1).astype(BF16), wout_s[0:2 * GROUP_W, :])

    u = _dot(h, win_s[:, COL_DQ:COL_DQ + 4 * GROUP_W])
    masks = _lane_head_masks(GROUP_W)
    lgf, lgb = _log_decays(theta_ref, masks)
    k8 = u[:, GROUP_W:2 * GROUP_W] * (HEAD_DIM ** -0.5)
    vb = u[:, 2 * GROUP_W:3 * GROUP_W].astype(BF16)
    final_states = _retention(u[:, 0:GROUP_W], k8, vb, None, lgf, lgb, masks, ret_ref)
    o = ret_ref[...]
    o = o * lax.rsqrt(_head_mean_square(o) + EPS) * dn_ref[lrow, :] * _silu(u[:, 3 * GROUP_W:4 * GROUP_W])
    y = y + _dot(o.astype(BF16), wout_s[3 * GROUP_W:4 * GROUP_W, :])
    for d, s_full in enumerate(final_states):
        for hd in range(N_HEADS):
            std_ref[d, hd] = s_full[hd * 64:(hd + 1) * 64, hd * 64:(hd + 1) * 64]

    zero = jnp.zeros((1, GROUP_W), F32)
    oc, cf, cb = _rglru_finish(zero, zero, af_ref, bf_ref, ab_ref, bb_ref, hf_ref, hb_ref, gel_ref)
    stc_ref[0:1, :] = cf
    stc_ref[1:2, :] = cb
    y = y + _dot(oc.astype(BF16), wout_s[2 * GROUP_W:3 * GROUP_W, :])
    xn_ref[...] = x + g2_ref[0:1, :] * y


def _ctx_mixers(x, mod, layer, prev, norm2_g, w_in, w_out,
                a_qn, a_kn, a_sink, b_qn, b_kn, c_conv_w, c_conv_b, c_wa, c_ba, c_wx, c_bx,
                c_lambda, d_theta, d_norm_g):
    per_request = lambda slots, shape: pl.BlockSpec((None, slots) + shape, lambda b: (b,) + (0,) * (1 + len(shape)))
    state_dims = [(SEQ, 128)] * 4 + [(2, GROUP_W), (2, N_HEADS, HEAD_DIM, HEAD_DIM)]
    scr = pltpu.VMEM((SEQ, GROUP_W), F32)
    out = pl.pallas_call(
        functools.partial(_ctx_mixer_kernel, layer=layer, n_prev=len(prev)),
        grid=(BATCH,),
        in_specs=[per_request(layer, dims) for dims in state_dims[:len(prev)]] + [
            pl.BlockSpec((SEQ, D_MODEL), lambda b: (b, 0)),
            _full((DEPTH, D_MODEL)),
            _mod_chunk(layer, 3), _mod_chunk(layer, 4), _mod_chunk(layer, 5),
            _once((None, D_MODEL, IN_WIDTH), lambda b: (layer, 0, 0)),
            _once((None, D_MODEL, D_MODEL), lambda b: (layer, 0, 0)),
            _full((DEPTH, HEAD_DIM)), _full((DEPTH, HEAD_DIM)), _full((DEPTH, HEAD_DIM)), _full((DEPTH, HEAD_DIM)),
            pl.BlockSpec(memory_space=pltpu.SMEM),
            _layer_block((4, GROUP_W), layer), _full((DEPTH, GROUP_W)),
            _layer_block((2, N_HEADS, HEAD_DIM, HEAD_DIM), layer), _layer_block((2, GROUP_W), layer),
            _layer_block((2, N_HEADS, HEAD_DIM, HEAD_DIM), layer), _layer_block((2, GROUP_W), layer),
            _layer_block((2, GROUP_W), layer),
            _layer_block((2, N_HEADS), layer), _full((DEPTH, GROUP_W)),
        ],
        out_specs=[pl.BlockSpec((SEQ, D_MODEL), lambda b: (b, 0))] + [
            per_request(layer + 1, dims) for dims in state_dims],
        out_shape=[jax.ShapeDtypeStruct((N_ROWS, D_MODEL), F32)] + [
            jax.ShapeDtypeStruct((BATCH, layer + 1) + dims, F32) for dims in state_dims],
        input_output_aliases={len(prev): 0},
        scratch_shapes=[pltpu.VMEM((D_MODEL, IN_WIDTH), BF16), pltpu.VMEM((D_MODEL, D_MODEL), BF16),
                        pltpu.VMEM((4, GROUP_W, GROUP_W), BF16)] + [scr] * 8,
        compiler_params=_cparams("arbitrary"),
        name="ctx_mixers",
    )(*prev, x, norm2_g, mod, mod, mod, w_in, w_out,
      a_qn, a_kn, b_qn, b_kn, a_sink, c_conv_w, c_conv_b, c_wa, c_ba, c_wx, c_bx,
      c_lambda, d_theta, d_norm_g)
    return out[0], tuple(out[1:])


LAT_BLOCK0 = N_CTX_ROWS // DEC_SEQ


def _lat_attn_kernel(x_ref, n2_ref, sh_ref, sc_ref, g2_ref, win_ref, wout_ref,
                     kca_ref, vca_ref, kcb_ref, vcb_ref,
                     aqn_ref, akn_ref, bqn_ref, bkn_ref, sink_ref, cos_ref, sinl_ref, sinh_ref,
                     xn_ref, h_ref, u_ref, o_ref, *, layer):
    t = DEC_SEQ
    lrow = slice(layer, layer + 1)
    mrow = pl.ds(1 + pl.program_id(0), 1)
    cos, sin_lo, sin_hi = cos_ref[...], sinl_ref[...], sinh_ref[...]
    scale = HEAD_DIM ** -0.5
    x = x_ref[...]
    h_ref[...] = _norm_mod(x, n2_ref[lrow, :], sc_ref[mrow, :], sh_ref[mrow, :]).astype(BF16)
    u_ref[...] = _dot(h_ref[...], win_ref[...].astype(BF16))

    q = _rope(_head_norm(u_ref[:, COL_AQ:COL_AQ + 256], aqn_ref[lrow, :]), cos, sin_lo, sin_hi)
    k = _rope(_head_norm(u_ref[:, COL_AK:COL_AK + 128], akn_ref[lrow, :]), cos, sin_lo, sin_hi)
    qh = [_head_cols(q * scale, h) for h in range(4)]
    v = u_ref[:, COL_AV:COL_AV + 128]
    kh = [_head_cols(k, kv) for kv in range(2)]
    vh = [_head_cols(v, kv) for kv in range(2)]
    kch = [_head_cols(kca_ref[...], kv) for kv in range(2)]
    vch = [_head_cols(vca_ref[...], kv) for kv in range(2)]
    w = ATT_BLOCK
    span = 3 * w
    for n in range(t // w):
        start = min(max((n - 1) * w, 0), t - span)
        rows = slice(n * w, (n + 1) * w)
        band = slice(start, start + span)
        qpos = (lax.broadcasted_iota(jnp.int32, (2 * w, span), 0) & (w - 1)) + n * w
        kpos = lax.broadcasted_iota(jnp.int32, (2 * w, span), 1) + start
        valid = jnp.abs(qpos - kpos) <= WINDOW
        heads = []
        for kv in range(2):
            qp = jnp.concatenate([qh[2 * kv][rows, :], qh[2 * kv + 1][rows, :]], axis=0)
            s_ctx = _dot_nt(qp, kch[kv])
            s_band = jnp.where(valid, _dot_nt(qp, kh[kv][band, :]), NEG_INF)
            row = lax.broadcasted_iota(jnp.int32, (2 * w, 1), 0)
            sink = jnp.where(row < w, sink_ref[layer, 2 * kv], sink_ref[layer, 2 * kv + 1])
            o = _softmax_pv([s_ctx, s_band], [vch[kv], vh[kv][band, :]], sink)
            heads += [o[0:w, :], o[w:2 * w, :]]
        o_ref[rows, 0:GROUP_W] = jnp.concatenate(heads, axis=-1).astype(BF16)

    q = _rope(_head_norm(u_ref[:, COL_BQ:COL_BQ + 256], bqn_ref[lrow, :]), cos, sin_lo, sin_hi)
    k = _rope(_head_norm(u_ref[:, COL_BK:COL_BK + 128], bkn_ref[lrow, :]), cos, sin_lo, sin_hi)
    qh = [_head_cols(q * scale, h) for h in range(4)]
    v = u_ref[:, COL_BV:COL_BV + 128]
    kh = [_head_cols(k, kv) for kv in range(2)]
    vh = [_head_cols(v, kv) for kv in range(2)]
    kch = [_head_cols(kcb_ref[...], kv) for kv in range(2)]
    vch = [_head_cols(vcb_ref[...], kv) for kv in range(2)]
    tq = 256
    for n in range(t // tq):
        rows = slice(n * tq, (n + 1) * tq)
        heads = []
        for kv in range(2):
            qp = jnp.concatenate([qh[2 * kv][rows, :], qh[2 * kv + 1][rows, :]], axis=0)
            o = _softmax_pv([_dot_nt(qp, kch[kv]), _dot_nt(qp, kh[kv])], [vch[kv], vh[kv]], None)
            heads += [o[0:tq, :], o[tq:2 * tq, :]]
        o_ref[rows, GROUP_W:2 * GROUP_W] = jnp.concatenate(heads, axis=-1).astype(BF16)

    xn_ref[...] = x + g2_ref[mrow, :] * _dot(o_ref[...], wout_ref[...].astype(BF16))


def _lat_rglru_kernel(xn_in_ref, h_ref, g2_ref, win_ref, wout_ref, h0_ref,
                      convw_ref, convb_ref, wa_ref, ba_ref, wx_ref, bx_ref, lam_ref,
                      xn_ref, gate_w_ref, af_ref, bf_ref, ab_ref, bb_ref, hf_ref, hb_ref, gel_ref, *, layer):
    lrow = slice(layer, layer + 1)
    mrow = pl.ds(1 + pl.program_id(0), 1)

    @pl.when(pl.program_id(0) == 0)
    def _():
        _store_gate_weights(gate_w_ref, wa_ref, wx_ref)

    u = _dot(h_ref[...], win_ref[...].astype(BF16))
    _rglru_prepare(u[:, 0:GROUP_W], u[:, GROUP_W:2 * GROUP_W], convw_ref, convb_ref[lrow, :],
                   gate_w_ref, ba_ref, bx_ref, lam_ref, af_ref, bf_ref, ab_ref, bb_ref, gel_ref)
    oc, _, _ = _rglru_finish(h0_ref[0:1, :], h0_ref[1:2, :],
                             af_ref, bf_ref, ab_ref, bb_ref, hf_ref, hb_ref, gel_ref)
    xn_ref[...] = xn_in_ref[...] + g2_ref[mrow, :] * _dot(oc.astype(BF16), wout_ref[...].astype(BF16))


def _lat_retention_kernel(xn_in_ref, h_ref, g2_ref, wqk_ref, wvg_ref, wout_ref,
                          s0_ref, theta_ref, dn_ref, xn_ref, o_ref, *, layer):
    lrow = slice(layer, layer + 1)
    mrow = pl.ds(1 + pl.program_id(0), 1)
    h = h_ref[...]
    uqk = _dot(h, wqk_ref[...].astype(BF16))
    uvg = _dot(h, wvg_ref[...].astype(BF16))
    masks = _lane_head_masks(GROUP_W)
    lgf, lgb = _log_decays(theta_ref, masks)
    s0 = tuple(_block_diag([s0_ref[d, hd] for hd in range(N_HEADS)]) for d in range(2))
    _retention(uqk[:, 0:GROUP_W], uqk[:, GROUP_W:2 * GROUP_W] * (HEAD_DIM ** -0.5),
               uvg[:, 0:GROUP_W].astype(BF16), s0, lgf, lgb, masks, o_ref)
    o = o_ref[...]
    o = o * lax.rsqrt(_head_mean_square(o) + EPS) * dn_ref[lrow, :] * _silu(uvg[:, GROUP_W:2 * GROUP_W])
    xn_ref[...] = xn_in_ref[...] + g2_ref[mrow, :] * _dot(o.astype(BF16), wout_ref[...].astype(BF16))


def _lat_mixers(x, mod, layer, caches, state_c, state_d, rope, norm2_g, w_in, w_out,
                a_qn, a_kn, a_sink, b_qn, b_kn, c_conv_w, c_conv_b, c_wa, c_ba, c_wx, c_bx,
                c_lambda, d_theta, d_norm_g):
    rows = pl.BlockSpec((DEC_SEQ, D_MODEL), lambda b: (LAT_BLOCK0 + b, 0))
    h_rows = pl.BlockSpec((DEC_SEQ, D_MODEL), lambda b: (b, 0))
    cache_spec = pl.BlockSpec((None, None, PAST_LEN, 128), lambda b: (b, layer, 0, 0))
    gain = _full((DEPTH, HEAD_DIM))
    table = _once((DEC_SEQ, 128), lambda b: (0, 0))
    out_shape = jax.ShapeDtypeStruct((N_ROWS, D_MODEL), F32)
    win_cols = lambda w, c: _once((None, D_MODEL, w), lambda b: (layer, 0, c))
    wout_rows = lambda h, r: _once((None, h, D_MODEL), lambda b: (layer, r, 0))

    xn, h = pl.pallas_call(
        functools.partial(_lat_attn_kernel, layer=layer),
        grid=(DEC_BATCH,),
        in_specs=[rows, _full((DEPTH, D_MODEL)),
                  _mod_chunk(layer, 3), _mod_chunk(layer, 4), _mod_chunk(layer, 5),
                  win_cols(4 * GROUP_W, 0), wout_rows(2 * GROUP_W, 0),
                  cache_spec, cache_spec, cache_spec, cache_spec,
                  gain, gain, gain, gain,
                  pl.BlockSpec(memory_space=pltpu.SMEM),
                  table, table, table],
        out_specs=[rows, pl.BlockSpec((DEC_SEQ, D_MODEL), lambda b: (b, 0), pipeline_mode=pl.Buffered(1))],
        out_shape=[out_shape, jax.ShapeDtypeStruct((N_LAT_ROWS, D_MODEL), BF16)],
        input_output_aliases={0: 0},
        scratch_shapes=[pltpu.VMEM((DEC_SEQ, 4 * GROUP_W), F32), pltpu.VMEM((DEC_SEQ, 2 * GROUP_W), BF16)],
        compiler_params=_cparams("arbitrary"),
        name="lat_attention",
    )(x, norm2_g, mod, mod, mod, w_in, w_out, *caches, a_qn, a_kn, b_qn, b_kn, a_sink, *rope)

    common = [rows, h_rows, _mod_chunk(layer, 5)]
    scr = pltpu.VMEM((DEC_SEQ, GROUP_W), F32)
    xn = pl.pallas_call(
        functools.partial(_lat_rglru_kernel, layer=layer),
        grid=(DEC_BATCH,),
        in_specs=common + [
            win_cols(2 * GROUP_W, COL_CX // (2 * GROUP_W)), wout_rows(GROUP_W, 2),
            pl.BlockSpec((None, None, 2, GROUP_W), lambda b: (b, layer, 0, 0)),
            _layer_block((4, GROUP_W), layer), _full((DEPTH, GROUP_W)),
            _layer_block((2, N_HEADS, HEAD_DIM, HEAD_DIM), layer), _layer_block((2, GROUP_W), layer),
            _layer_block((2, N_HEADS, HEAD_DIM, HEAD_DIM), layer), _layer_block((2, GROUP_W), layer),
            _layer_block((2, GROUP_W), layer)],
        out_specs=rows,
        out_shape=out_shape,
        input_output_aliases={0: 0},
        scratch_shapes=[pltpu.VMEM((4, GROUP_W, GROUP_W), BF16)] + [scr] * 7,
        compiler_params=_cparams("arbitrary"),
        name="lat_rglru",
    )(xn, h, mod, w_in, w_out, state_c, c_conv_w, c_conv_b, c_wa, c_ba, c_wx, c_bx, c_lambda)

    xn = pl.pallas_call(
        functools.partial(_lat_retention_kernel, layer=layer),
        grid=(DEC_BATCH,),
        in_specs=common + [
            win_cols(2 * GROUP_W, COL_DQ // (2 * GROUP_W)), win_cols(2 * GROUP_W, COL_DV // (2 * GROUP_W)),
            wout_rows(GROUP_W, 3),
            pl.BlockSpec((None, None, 2, N_HEADS, HEAD_DIM, HEAD_DIM), lambda b: (b, layer, 0, 0, 0, 0)),
            _layer_block((2, N_HEADS), layer), _full((DEPTH, GROUP_W))],
        out_specs=rows,
        out_shape=out_shape,
        input_output_aliases={0: 0},
        scratch_shapes=[pltpu.VMEM((DEC_SEQ, GROUP_W), F32)],
        compiler_params=_cparams("arbitrary"),
        name="lat_retention",
    )(xn, h, mod, w_in, w_in, w_out, state_d, d_theta, d_norm_g)
    return xn


def _rope_tables():
    t = np.arange(DEC_SEQ)
    row = (t // GRID_W).astype(np.float64)[:, None]
    col = (t % GRID_W).astype(np.float64)[:, None]
    half = HEAD_DIM // 2
    inv = 1.0 / (ROPE_BASE ** (np.arange(0, half, 2, dtype=np.float64) / half))
    j = np.arange(128) % HEAD_DIM
    ang = np.where((j < half)[None, :], row, col) * inv[j % (half // 2)][None, :]
    first = ((j % half) < half // 2)[None, :]
    cos, sin = np.cos(ang), np.sin(ang)
    return tuple(jnp.asarray(a, F32) for a in (cos, np.where(first, -sin, 0.0), np.where(first, 0.0, sin)))


def kernel(x_prompt, x_sample, cache_a_k, cache_a_v, cache_b_k, cache_b_v, state_c, state_d, c, c_ctx, norm1_g, norm2_g, norm3_g, w_mod, b_mod, ffn1_wg, ffn1_wu, ffn1_wd, ffn2_wg, ffn2_wu, ffn2_wd, w_in, w_out, a_qn, a_kn, a_sink, b_qn, b_kn, c_conv_w, c_conv_b, c_wa, c_ba, c_wx, c_bx, c_lambda, d_theta, d_norm_g):
    mod = _modulation(c_ctx, c, w_mod, b_mod)
    rope = _rope_tables()
    caches = tuple(t.reshape(DEC_BATCH, DEPTH, PAST_LEN, 128) for t in (cache_a_k, cache_a_v, cache_b_k, cache_b_v))
    mixer_params = (a_qn, a_kn, a_sink, b_qn, b_kn, c_conv_w, c_conv_b, c_wa, c_ba, c_wx, c_bx,
                    c_lambda, d_theta, d_norm_g)
    xs = (x_prompt.reshape(N_CTX_ROWS, D_MODEL), x_sample.reshape(N_LAT_ROWS, D_MODEL))
    states = ()
    for l in range(DEPTH):
        (x,) = _ffn(xs, mod, l, 0, norm1_g, ffn1_wg, ffn1_wu, ffn1_wd)
        x, states = _ctx_mixers(x, mod, l, states, norm2_g, w_in, w_out, *mixer_params)
        x = _lat_mixers(x, mod, l, caches, state_c, state_d, rope, norm2_g, w_in, w_out, *mixer_params)
        xs = _ffn((x,), mod, l, 6, norm3_g, ffn2_wg, ffn2_wu, ffn2_wd, split_out=(l == DEPTH - 1))
    y_p, y_s = xs
    ka, va, kb, vb, st_c, st_d = states
    kv_shape = (BATCH, DEPTH, SEQ, 2, HEAD_DIM)
    return (y_p.reshape(BATCH, SEQ, D_MODEL), y_s.reshape(DEC_BATCH, DEC_SEQ, D_MODEL),
            ka.reshape(kv_shape), va.reshape(kv_shape), kb.reshape(kv_shape), vb.reshape(kv_shape),
            st_c, st_d)
```

```python
import functools
import math

import numpy as np
import jax
import jax.numpy as jnp
from jax import lax
from jax.experimental import pallas as pl
from jax.experimental.pallas import tpu as pltpu

F32 = jnp.float32
BF16 = jnp.bfloat16

D_MODEL = 1024
BATCH = 16
SEQ = 256
DEPTH = 2
DEC_BATCH = 2
DEC_SEQ = 1024
PAST_LEN = 512
GRID_W = 64
HEAD_DIM = 64
HEAD_SHIFT = 6
N_HEADS = 4
GROUP_W = 256
WINDOW = 128
ATT_BLOCK = 128
ROPE_BASE = 10000.0
LRU_C = 8.0
D_FF = 2816
N_MOD = 9
EPS = 1e-6
NEG_INF = -1e30
IN_WIDTH = 2560

N_CTX_ROWS = BATCH * SEQ
N_LAT_ROWS = DEC_BATCH * DEC_SEQ
N_ROWS = N_CTX_ROWS + N_LAT_ROWS
MOD_ROWS = 8
MOD_GROUP = 1024

VMEM_LIMIT_BYTES = 56 * 1024 * 1024

COL_AQ, COL_AK, COL_AV = 0, 256, 384
COL_BQ, COL_BK, COL_BV = 512, 768, 896
COL_CX, COL_CY = 1024, 1280
COL_DQ, COL_DK, COL_DV, COL_DG = 1536, 1792, 2048, 2304


def _cparams(*sem):
    return pltpu.CompilerParams(dimension_semantics=sem, vmem_limit_bytes=VMEM_LIMIT_BYTES)


def _dot(a, b):
    return jnp.dot(a, b, preferred_element_type=F32)


def _dot_nt(a, b):
    return lax.dot_general(a, b, (((1,), (1,)), ((), ())), preferred_element_type=F32)


def _dot_tn(a, b):
    return lax.dot_general(a, b, (((0,), (0,)), ((), ())), preferred_element_type=F32)


def _silu(x):
    return x * jax.nn.sigmoid(x)


def _gelu_tanh(x):
    return 0.5 * x * (1.0 + jnp.tanh(math.sqrt(2.0 / math.pi) * (x + 0.044715 * (x * x * x))))


def _mod_row(i, tm, s):
    if tm >= MOD_GROUP:
        block_index = i * (tm // MOD_GROUP) + s
    else:
        block_index = i >> int(math.log2(MOD_GROUP // tm))
    return jnp.maximum(block_index - (N_CTX_ROWS // MOD_GROUP - 1), 0)


def _norm_mod(x, g, sc, sh):
    ms = jnp.mean(x * x, axis=-1, keepdims=True)
    return (x * lax.rsqrt(ms + EPS) * g) * (1.0 + sc) + sh


def _full(shape):
    return pl.BlockSpec(shape, lambda *_: (0,) * len(shape))


def _layer_block(shape, layer):
    return pl.BlockSpec((None,) + shape, lambda *_: (layer,) + (0,) * len(shape))


MOD_TN = 3072


def _mod_kernel(cc_ref, c_ref, w_ref, b_ref, o_ref):
    l = pl.program_id(0)
    pad = jnp.zeros((MOD_ROWS - 1 - DEC_BATCH, D_MODEL), F32)
    cond = jnp.concatenate([cc_ref[...], c_ref[...], pad], axis=0)
    o_ref[...] = _dot(_silu(cond).astype(BF16), w_ref[...].astype(BF16)) + b_ref[pl.ds(l, 1), :]


def _modulation(c_ctx, c, w_mod, b_mod):
    n = N_MOD * D_MODEL
    return pl.pallas_call(
        _mod_kernel,
        grid=(DEPTH, n // MOD_TN),
        in_specs=[
            pl.BlockSpec((1, D_MODEL), lambda l, j: (0, 0)),
            pl.BlockSpec((DEC_BATCH, D_MODEL), lambda l, j: (0, 0)),
            pl.BlockSpec((None, D_MODEL, MOD_TN), lambda l, j: (l, 0, j)),
            pl.BlockSpec((DEPTH, MOD_TN), lambda l, j: (0, j)),
        ],
        out_specs=pl.BlockSpec((None, MOD_ROWS, MOD_TN), lambda l, j: (l, 0, j)),
        out_shape=jax.ShapeDtypeStruct((DEPTH, MOD_ROWS, n), F32),
        compiler_params=_cparams("arbitrary", "arbitrary"),
        name="modulation",
    )(c_ctx.reshape(1, D_MODEL), c, w_mod, b_mod)


FFN_TM = 1024
FFN_TF = 256
N_CTX_TILES = N_CTX_ROWS // FFN_TM


FFN_NJ = D_FF // FFN_TF
N_FFN_TILES = N_ROWS // FFN_TM
N_FFN_STEPS = FFN_NJ + N_FFN_TILES


def _ffn_tile(step):
    return jnp.maximum(step - FFN_NJ, 0)


def _on_stream_part(tile, x_refs, o_refs, fn):
    if len(x_refs) == 1 and len(o_refs) == 1:
        fn(x_refs[0], o_refs[0])
    else:
        pl.when(tile < N_CTX_TILES)(lambda: fn(x_refs[0], o_refs[0]))
        pl.when(tile >= N_CTX_TILES)(lambda: fn(x_refs[-1], o_refs[-1]))


def _ffn_kernel(*refs, layer, n_in, n_out):
    x_refs = refs[:n_in]
    n_ref, sh_ref, sc_ref, g_ref, wg_ref, wu_ref, wd_ref = refs[n_in:n_in + 7]
    o_refs = refs[n_in + 7:n_in + 7 + n_out]
    h_ref, a_ref, wg_s, wu_s, wd_s = refs[n_in + 7 + n_out:]
    nj, tf = FFN_NJ, FFN_TF
    s = pl.program_id(0)
    tile = _ffn_tile(s)
    r = _mod_row(tile, FFN_TM, 0)

    def load_tile():
        def init(x_ref, _):
            h = _norm_mod(x_ref[...], n_ref[layer:layer + 1, :], sc_ref[pl.ds(r, 1), :], sh_ref[pl.ds(r, 1), :])
            h_ref[...] = h.astype(BF16)
        _on_stream_part(tile, x_refs, o_refs, init)

    def up_chunk(j, cols):
        h = h_ref[...]
        a_ref[:, cols] = (_silu(_dot(h, wg_s[j])) * _dot(h, wu_s[j])).astype(BF16)

    def down_and_store():
        y = (0.5 * g_ref[pl.ds(r, 1), :]) * _dot(a_ref[...], wd_s[...])

        def store(x_ref, o_ref):
            o_ref[...] = x_ref[...] + y
        _on_stream_part(tile, x_refs, o_refs, store)

    def keep_arrived_chunk():
        wg_s[s] = wg_ref[...].astype(BF16)
        wu_s[s] = wu_ref[...].astype(BF16)
        wd_s[pl.ds(pl.multiple_of(s * tf, tf), tf), :] = wd_ref[...].astype(BF16)

    def up_previous_chunk():
        up_chunk(s - 1, pl.ds(pl.multiple_of((s - 1) * tf, tf), tf))

    @pl.when(s == 0)
    def _():
        load_tile()
        keep_arrived_chunk()

    @pl.when((s > 0) & (s < nj))
    def _():
        up_previous_chunk()
        keep_arrived_chunk()

    @pl.when(s == nj)
    def _():
        up_previous_chunk()
        down_and_store()

    @pl.when(s > nj)
    def _():
        load_tile()
        for j in range(nj):
            up_chunk(j, slice(j * tf, (j + 1) * tf))
        down_and_store()


def _stream_specs(split, buffered_once):
    tm = FFN_TM
    kw = {"pipeline_mode": pl.Buffered(1)} if buffered_once else {}
    if not split:
        return [pl.BlockSpec((tm, D_MODEL), lambda s: (_ffn_tile(s), 0), **kw)]
    last_ctx = N_CTX_TILES - 1
    return [pl.BlockSpec((tm, D_MODEL), lambda s: (jnp.minimum(_ffn_tile(s), last_ctx), 0), **kw),
            pl.BlockSpec((tm, D_MODEL), lambda s: (jnp.maximum(_ffn_tile(s) - N_CTX_TILES, 0), 0), **kw)]


def _ffn(xs, mod, layer, chunk0, norm_g, wg, wu, wd, split_out=False):
    tm, tf, nj = FFN_TM, FFN_TF, FFN_NJ
    split_in = len(xs) == 2
    mod_spec = lambda c: pl.BlockSpec((None, MOD_ROWS, D_MODEL), lambda s: (layer, 0, c))
    w_col = lambda s: (layer, 0, jnp.minimum(s, nj - 1))
    w_row = lambda s: (layer, jnp.minimum(s, nj - 1), 0)
    if split_out:
        out_shape = [jax.ShapeDtypeStruct((N_CTX_ROWS, D_MODEL), F32),
                     jax.ShapeDtypeStruct((N_LAT_ROWS, D_MODEL), F32)]
    else:
        out_shape = [jax.ShapeDtypeStruct((N_ROWS, D_MODEL), F32)]
    out = pl.pallas_call(
        functools.partial(_ffn_kernel, layer=layer, n_in=len(xs), n_out=len(out_shape)),
        grid=(N_FFN_STEPS,),
        in_specs=_stream_specs(split_in, split_in) + [
            _full((DEPTH, D_MODEL)),
            mod_spec(chunk0), mod_spec(chunk0 + 1), mod_spec(chunk0 + 2),
            pl.BlockSpec((None, D_MODEL, tf), w_col),
            pl.BlockSpec((None, D_MODEL, tf), w_col),
            pl.BlockSpec((None, tf, D_MODEL), w_row),
        ],
        out_specs=_stream_specs(split_out, True),
        out_shape=out_shape,
        scratch_shapes=[pltpu.VMEM((tm, D_MODEL), BF16),
                        pltpu.VMEM((tm, D_FF), BF16),
                        pltpu.VMEM((nj, D_MODEL, tf), BF16),
                        pltpu.VMEM((nj, D_MODEL, tf), BF16),
                        pltpu.VMEM((D_FF, D_MODEL), BF16)],
        compiler_params=_cparams("arbitrary"),
        name="ffn",
    )(*xs, norm_g, mod, mod, mod, wg, wu, wd)
    return tuple(out)


def _once(shape, index_map):
    return pl.BlockSpec(shape, index_map, pipeline_mode=pl.Buffered(1))


def _mod_chunk(layer, c):
    return pl.BlockSpec((None, MOD_ROWS, D_MODEL), lambda *_: (layer, 0, c))


def _head_mean_square(x):
    n = x.shape[-1]
    x2 = x * x
    hi = x2.astype(BF16)
    lo = (x2 - hi.astype(F32)).astype(BF16)
    r = lax.broadcasted_iota(jnp.int32, (n, n), 0) >> HEAD_SHIFT
    c = lax.broadcasted_iota(jnp.int32, (n, n), 1) >> HEAD_SHIFT
    ones_bd = jnp.where(r == c, 1.0, 0.0).astype(BF16)
    return (_dot(hi, ones_bd) + _dot(lo, ones_bd)) * (1.0 / HEAD_DIM)


def _head_norm(x, head_gain):
    gain_row = jnp.concatenate([head_gain] * (x.shape[-1] // HEAD_DIM), axis=-1)
    return x * lax.rsqrt(_head_mean_square(x) + EPS) * gain_row


def _head_cols(x, h):
    return x[:, h * HEAD_DIM:(h + 1) * HEAD_DIM].astype(BF16)


def _rope(x, cos, sin_lo, sin_hi):
    cols = []
    for c in range(x.shape[-1] // 128):
        xc = x[:, c * 128:(c + 1) * 128]
        cols.append(xc * cos + pltpu.roll(xc, 112, 1) * sin_lo + pltpu.roll(xc, 16, 1) * sin_hi)
    return cols[0] if len(cols) == 1 else jnp.concatenate(cols, axis=-1)


def _softmax_pv(scores, values, sink):
    m = jnp.max(scores[0], axis=-1, keepdims=True)
    for s in scores[1:]:
        m = jnp.maximum(m, jnp.max(s, axis=-1, keepdims=True))
    if sink is not None:
        m = jnp.maximum(m, sink)
    denom = None
    acc = None
    for s, v in zip(scores, values):
        p = jnp.exp(s - m)
        d = jnp.sum(p, axis=-1, keepdims=True)
        o = _dot(p.astype(BF16), v)
        denom = d if denom is None else denom + d
        acc = o if acc is None else acc + o
    if sink is not None:
        denom = denom + jnp.exp(sink - m)
    return acc / denom


def _block_diag(blocks):
    n = len(blocks)
    w = blocks[0].shape[0]
    rows = []
    for k, blk in enumerate(blocks):
        parts = []
        if k > 0:
            parts.append(jnp.zeros((w, k * w), F32))
        parts.append(blk)
        if k < n - 1:
            parts.append(jnp.zeros((w, (n - 1 - k) * w), F32))
        rows.append(jnp.concatenate(parts, axis=-1))
    return jnp.concatenate(rows, axis=0)


def _rglru_gates(xc, wa, ba, wx, bx, lam):
    xb = xc.astype(BF16)
    r = jax.nn.sigmoid(_dot(xb, wa) + ba)
    i = jax.nn.sigmoid(_dot(xb, wx) + bx)
    softplus = jnp.maximum(-lam, 0.0) + jnp.log1p(jnp.exp(-jnp.abs(lam)))
    log_a = (-LRU_C) * r * softplus
    a = jnp.exp(log_a)
    b = jnp.sqrt(1.0 - a * a) * (i * xc)
    return a, b


def _block_prefix(a, b, reverse):
    t = a.shape[0]
    row = lax.broadcasted_iota(jnp.int32, a.shape, 0) & 7
    for d in (1, 2, 4):
        if reverse:
            a_s = pltpu.roll(a, t - d, 0)
            b_s = pltpu.roll(b, t - d, 0)
            ok = row < 8 - d
        else:
            a_s = pltpu.roll(a, d, 0)
            b_s = pltpu.roll(b, d, 0)
            ok = row >= d
        b = jnp.where(ok, a * b_s + b, b)
        a = jnp.where(ok, a * a_s, a)
    return a, b


def _conv4(x, w_ref, b_row):
    t = x.shape[0]
    row = lax.broadcasted_iota(jnp.int32, x.shape, 0)
    xm2 = jnp.where(row >= 2, pltpu.roll(x, 2, 0), 0.0)
    xm1 = jnp.where(row >= 1, pltpu.roll(x, 1, 0), 0.0)
    xp1 = jnp.where(row < t - 1, pltpu.roll(x, t - 1, 0), 0.0)
    return (xm2 * w_ref[0:1, :] + xm1 * w_ref[1:2, :] + x * w_ref[2:3, :] + xp1 * w_ref[3:4, :]) + b_row


def _rglru_prepare(cx, cy, conv_w_ref, conv_b, gate_w_ref, ba_ref, bx_ref, lam_ref,
                   af_ref, bf_ref, ab_ref, bb_ref, gel_ref):
    xc = _conv4(cx, conv_w_ref, conv_b)
    a, b = _rglru_gates(xc, gate_w_ref[0], ba_ref[0:1, :], gate_w_ref[1], bx_ref[0:1, :], lam_ref[0:1, :])
    a, b = _block_prefix(a, b, reverse=False)
    af_ref[...] = a
    bf_ref[...] = b
    a, b = _rglru_gates(xc, gate_w_ref[2], ba_ref[1:2, :], gate_w_ref[3], bx_ref[1:2, :], lam_ref[1:2, :])
    a, b = _block_prefix(a, b, reverse=True)
    ab_ref[...] = a
    bb_ref[...] = b
    gel_ref[...] = _gelu_tanh(cy)


def _rglru_finish(h0f, h0b, af_ref, bf_ref, ab_ref, bb_ref, hf_ref, hb_ref, gel_ref):
    nblk = af_ref.shape[0] // 8

    def body(k, carry):
        cf, cb = carry
        rf = pl.ds(pl.multiple_of(k * 8, 8), 8)
        hf = bf_ref[rf, :] + af_ref[rf, :] * cf
        hf_ref[rf, :] = hf
        rb = pl.ds(pl.multiple_of((nblk - 1 - k) * 8, 8), 8)
        hb = bb_ref[rb, :] + ab_ref[rb, :] * cb
        hb_ref[rb, :] = hb
        return hf[7:8, :], hb[0:1, :]

    cf, cb = lax.fori_loop(0, nblk, body, (h0f, h0b))
    oc = (hf_ref[...] + hb_ref[...]) * gel_ref[...]
    return oc, cf, cb


def _store_gate_weights(gate_w_ref, wa_ref, wx_ref):
    for d in range(2):
        gate_w_ref[2 * d] = _block_diag([wa_ref[d, n] for n in range(N_HEADS)]).astype(BF16)
        gate_w_ref[2 * d + 1] = _block_diag([wx_ref[d, n] for n in range(N_HEADS)]).astype(BF16)


def _lane_head_masks(n):
    lane = lax.broadcasted_iota(jnp.int32, (1, n), 1) >> HEAD_SHIFT
    return [jnp.where(lane == h, 1.0, 0.0) for h in range(n // HEAD_DIM)]


def _log_decays(theta_ref, masks):
    theta = theta_ref[...]
    lanes = theta[:, 0:1] * masks[0]
    for h in range(1, N_HEADS):
        lanes = lanes + theta[:, h:h + 1] * masks[h]
    lg = jnp.log1p(-jnp.exp(lanes))
    return lg[0:1, :], lg[1:2, :]


RET_BLOCK = 256


def _retention(q, k8, vb, s0, lgf, lgb, masks, o_ref):
    t, w = q.shape
    c = RET_BLOCK
    nh = w // HEAD_DIM
    pos = lax.broadcasted_iota(jnp.int32, (c, w), 0).astype(F32)
    q_dec = (jnp.exp(lgf * (pos + 1.0)), jnp.exp(lgb * (float(c) - pos)))
    k_dec = (jnp.exp(lgf * (float(c - 1) - pos)), jnp.exp(lgb * pos))
    chunk_dec = (jnp.exp(lgf * float(c)), jnp.exp(lgb * float(c)))
    rel = (lax.broadcasted_iota(jnp.int32, (c, c), 0) - lax.broadcasted_iota(jnp.int32, (c, c), 1)).astype(F32)
    decs = []
    for h in range(nh):
        gf = lgf[:, h * HEAD_DIM:h * HEAD_DIM + 1]
        gb = lgb[:, h * HEAD_DIM:h * HEAD_DIM + 1]
        e = jnp.exp(jnp.where(rel >= 0, gf * rel, gb * (-rel)))
        decs.append(jnp.where(rel == 0, 2.0, e))
    dec = jnp.concatenate(decs, axis=0)
    r_head = lax.broadcasted_iota(jnp.int32, (w, w), 0) >> HEAD_SHIFT
    c_head = lax.broadcasted_iota(jnp.int32, (w, w), 1) >> HEAD_SHIFT
    same_head = jnp.where(r_head == c_head, 1.0, 0.0)
    states = [None, None] if s0 is None else list(s0)

    def carry(d, rows, o):
        if states[d] is not None:
            o = o + _dot((q[rows, :] * q_dec[d]).astype(BF16), states[d].astype(BF16))
        upd = _dot_tn((k8[rows, :] * k_dec[d]).astype(BF16), vb[rows, :]) * same_head
        states[d] = upd if states[d] is None else states[d] * chunk_dec[d] + upd
        return o

    for ci in range(t // c):
        rows = slice(ci * c, (ci + 1) * c)
        qc = q[rows, :]
        q_stack = jnp.concatenate([(qc * masks[h]).astype(BF16) for h in range(nh)], axis=0)
        inner = (_dot_nt(q_stack, k8[rows, :].astype(BF16)) * dec).astype(BF16)
        out = _dot(inner, vb[rows, :])
        o = out[0:c, :] * masks[0]
        for h in range(1, nh):
            o = o + out[h * c:(h + 1) * c, :] * masks[h]
        o_ref[rows, :] = carry(0, rows, o)
    for ci in reversed(range(t // c)):
        rows = slice(ci * c, (ci + 1) * c)
        if states[1] is not None:
            o_ref[rows, :] = carry(1, rows, o_ref[rows, :])
        else:
            carry(1, rows, None)
    return states[0], states[1]


def _ctx_mixer_kernel(*refs, layer, n_prev):
    prev_refs = refs[:n_prev]
    (x_ref, n2_ref, sh_ref, sc_ref, g2_ref, win_ref, wout_ref,
     aqn_ref, akn_ref, bqn_ref, bkn_ref, sink_ref,
     convw_ref, convb_ref, wa_ref, ba_ref, wx_ref, bx_ref, lam_ref, theta_ref, dn_ref,
     xn_ref, *state_refs) = refs[n_prev:n_prev + 28]
    (win_s, wout_s, u_ref, mixed_ref,
     gate_w_ref, af_ref, bf_ref, ab_ref, bb_ref, hf_ref, hb_ref, gel_ref, ret_ref) = refs[n_prev + 28:]
    t = SEQ
    lrow = slice(layer, layer + 1)
    for prev_ref, state_ref in zip(prev_refs, state_refs):
        for earlier in range(layer):
            state_ref[earlier] = prev_ref[earlier]
    ka_ref, va_ref, kb_ref, vb_ref, stc_ref, std_ref = (ref.at[layer] for ref in state_refs)

    @pl.when(pl.program_id(0) == 0)
    def _():
        for c in range(IN_WIDTH // 512):
            win_s[:, c * 512:(c + 1) * 512] = win_ref[:, c * 512:(c + 1) * 512].astype(BF16)
        wout_s[...] = wout_ref[...].astype(BF16)
        _store_gate_weights(gate_w_ref, wa_ref, wx_ref)

    x = x_ref[...]
    h = _norm_mod(x, n2_ref[lrow, :], sc_ref[0:1, :], sh_ref[0:1, :]).astype(BF16)
    u_ref[...] = _dot(h, win_s[...])

    for (cq, ck, cv, qn_ref, kn_ref, k_out, v_out, col0, use_sink) in (
            (COL_AQ, COL_AK, COL_AV, aqn_ref, akn_ref, ka_ref, va_ref, 0, True),
            (COL_BQ, COL_BK, COL_BV, bqn_ref, bkn_ref, kb_ref, vb_ref, GROUP_W, False)):
        q = _head_norm(u_ref[:, cq:cq + 256], qn_ref[lrow, :])
        k = _head_norm(u_ref[:, ck:ck + 128], kn_ref[lrow, :])
        v = u_ref[:, cv:cv + 128]
        k_out[...] = k
        v_out[...] = v
        qs = q * (HEAD_DIM ** -0.5)
        heads = []
        for hd in range(N_HEADS):
            kv = hd // 2
            s = _dot_nt(_head_cols(qs, hd), _head_cols(k, kv))
            sink = jnp.full((t, 1), sink_ref[layer, hd], F32) if use_sink else None
            heads.append(_softmax_pv([s], [_head_cols(v, kv)], sink))
        mixed_ref[:, col0:col0 + GROUP_W] = jnp.concatenate(heads, axis=-1).astype(BF16)

    _rglru_prepare(u_ref[:, COL_CX:COL_CX + GROUP_W], u_ref[:, COL_CY:COL_CY + GROUP_W],
                   convw_ref, convb_ref[lrow, :], gate_w_ref, ba_ref, bx_ref, lam_ref,
                   af_ref, bf_ref, ab_ref, bb_ref, gel_ref)
    zero = jnp.zeros((1, GROUP_W), F32)
    oc, cf, cb = _rglru_finish(zero, zero, af_ref, bf_ref, ab_ref, bb_ref, hf_ref, hb_ref, gel_ref)
    mixed_ref[:, 2 * GROUP_W:3 * GROUP_W] = oc.astype(BF16)
    stc_ref[0:1, :] = cf
    stc_ref[1:2, :] = cb

    masks = _lane_head_masks(GROUP_W)
    lgf, lgb = _log_decays(theta_ref, masks)
    k8 = u_ref[:, COL_DK:COL_DK + GROUP_W] * (HEAD_DIM ** -0.5)
    vb = u_ref[:, COL_DV:COL_DV + GROUP_W].astype(BF16)
    final_states = _retention(u_ref[:, COL_DQ:COL_DQ + GROUP_W], k8, vb, None, lgf, lgb, masks, ret_ref)
    o = ret_ref[...]
    o = o * lax.rsqrt(_head_mean_square(o) + EPS) * dn_ref[lrow, :] * _silu(u_ref[:, COL_DG:COL_DG + GROUP_W])
    mixed_ref[:, 3 * GROUP_W:4 * GROUP_W] = o.astype(BF16)
    for d, s_full in enumerate(final_states):
        for hd in range(N_HEADS):
            std_ref[d, hd] = s_full[hd * 64:(hd + 1) * 64, hd * 64:(hd + 1) * 64]

    xn_ref[...] = x + g2_ref[0:1, :] * _dot(mixed_ref[...], wout_s[...])


def _ctx_mixers(x, mod, layer, prev, norm2_g, w_in, w_out,
                a_qn, a_kn, a_sink, b_qn, b_kn, c_conv_w, c_conv_b, c_wa, c_ba, c_wx, c_bx,
                c_lambda, d_theta, d_norm_g):
    per_request = lambda slots, shape: pl.BlockSpec((None, slots) + shape, lambda b: (b,) + (0,) * (1 + len(shape)))
    state_dims = [(SEQ, 128)] * 4 + [(2, GROUP_W), (2, N_HEADS, HEAD_DIM, HEAD_DIM)]
    scr = pltpu.VMEM((SEQ, GROUP_W), F32)
    out = pl.pallas_call(
        functools.partial(_ctx_mixer_kernel, layer=layer, n_prev=len(prev)),
        grid=(BATCH,),
        in_specs=[per_request(layer, dims) for dims in state_dims[:len(prev)]] + [
            pl.BlockSpec((SEQ, D_MODEL), lambda b: (b, 0)),
            _full((DEPTH, D_MODEL)),
            _mod_chunk(layer, 3), _mod_chunk(layer, 4), _mod_chunk(layer, 5),
            _once((None, D_MODEL, IN_WIDTH), lambda b: (layer, 0, 0)),
            _once((None, D_MODEL, D_MODEL), lambda b: (layer, 0, 0)),
            _full((DEPTH, HEAD_DIM)), _full((DEPTH, HEAD_DIM)), _full((DEPTH, HEAD_DIM)), _full((DEPTH, HEAD_DIM)),
            pl.BlockSpec(memory_space=pltpu.SMEM),
            _layer_block((4, GROUP_W), layer), _full((DEPTH, GROUP_W)),
            _layer_block((2, N_HEADS, HEAD_DIM, HEAD_DIM), layer), _layer_block((2, GROUP_W), layer),
            _layer_block((2, N_HEADS, HEAD_DIM, HEAD_DIM), layer), _layer_block((2, GROUP_W), layer),
            _layer_block((2, GROUP_W), layer),
            _layer_block((2, N_HEADS), layer), _full((DEPTH, GROUP_W)),
        ],
        out_specs=[pl.BlockSpec((SEQ, D_MODEL), lambda b: (b, 0))] + [
            per_request(layer + 1, dims) for dims in state_dims],
        out_shape=[jax.ShapeDtypeStruct((N_ROWS, D_MODEL), F32)] + [
            jax.ShapeDtypeStruct((BATCH, layer + 1) + dims, F32) for dims in state_dims],
        input_output_aliases={len(prev): 0},
        scratch_shapes=[pltpu.VMEM((D_MODEL, IN_WIDTH), BF16), pltpu.VMEM((D_MODEL, D_MODEL), BF16),
                        pltpu.VMEM((SEQ, IN_WIDTH), F32), pltpu.VMEM((SEQ, D_MODEL), BF16),
                        pltpu.VMEM((4, GROUP_W, GROUP_W), BF16)] + [scr] * 8,
        compiler_params=_cparams("arbitrary"),
        name="ctx_mixers",
    )(*prev, x, norm2_g, mod, mod, mod, w_in, w_out,
      a_qn, a_kn, b_qn, b_kn, a_sink, c_conv_w, c_conv_b, c_wa, c_ba, c_wx, c_bx,
      c_lambda, d_theta, d_norm_g)
    return out[0], tuple(out[1:])


LAT_BLOCK0 = N_CTX_ROWS // DEC_SEQ


def _lat_attn_kernel(x_ref, n2_ref, sh_ref, sc_ref, g2_ref, win_ref, wout_ref,
                     kca_ref, vca_ref, kcb_ref, vcb_ref,
                     aqn_ref, akn_ref, bqn_ref, bkn_ref, sink_ref, cos_ref, sinl_ref, sinh_ref,
                     xn_ref, h_ref, u_ref, o_ref, *, layer):
    t = DEC_SEQ
    lrow = slice(layer, layer + 1)
    mrow = pl.ds(1 + pl.program_id(0), 1)
    cos, sin_lo, sin_hi = cos_ref[...], sinl_ref[...], sinh_ref[...]
    scale = HEAD_DIM ** -0.5
    x = x_ref[...]
    h_ref[...] = _norm_mod(x, n2_ref[lrow, :], sc_ref[mrow, :], sh_ref[mrow, :]).astype(BF16)
    u_ref[...] = _dot(h_ref[...], win_ref[...].astype(BF16))

    q = _rope(_head_norm(u_ref[:, COL_AQ:COL_AQ + 256], aqn_ref[lrow, :]), cos, sin_lo, sin_hi)
    k = _rope(_head_norm(u_ref[:, COL_AK:COL_AK + 128], akn_ref[lrow, :]), cos, sin_lo, sin_hi)
    qh = [_head_cols(q * scale, h) for h in range(4)]
    v = u_ref[:, COL_AV:COL_AV + 128]
    kh = [_head_cols(k, kv) for kv in range(2)]
    vh = [_head_cols(v, kv) for kv in range(2)]
    kch = [_head_cols(kca_ref[...], kv) for kv in range(2)]
    vch = [_head_cols(vca_ref[...], kv) for kv in range(2)]
    w = ATT_BLOCK
    span = 3 * w
    for n in range(t // w):
        start = min(max((n - 1) * w, 0), t - span)
        rows = slice(n * w, (n + 1) * w)
        band = slice(start, start + span)
        qpos = (lax.broadcasted_iota(jnp.int32, (2 * w, span), 0) & (w - 1)) + n * w
        kpos = lax.broadcasted_iota(jnp.int32, (2 * w, span), 1) + start
        valid = jnp.abs(qpos - kpos) <= WINDOW
        heads = []
        for kv in range(2):
            qp = jnp.concatenate([qh[2 * kv][rows, :], qh[2 * kv + 1][rows, :]], axis=0)
            s_ctx = _dot_nt(qp, kch[kv])
            s_band = jnp.where(valid, _dot_nt(qp, kh[kv][band, :]), NEG_INF)
            row = lax.broadcasted_iota(jnp.int32, (2 * w, 1), 0)
            sink = jnp.where(row < w, sink_ref[layer, 2 * kv], sink_ref[layer, 2 * kv + 1])
            o = _softmax_pv([s_ctx, s_band], [vch[kv], vh[kv][band, :]], sink)
            heads += [o[0:w, :], o[w:2 * w, :]]
        o_ref[rows, 0:GROUP_W] = jnp.concatenate(heads, axis=-1).astype(BF16)

    q = _rope(_head_norm(u_ref[:, COL_BQ:COL_BQ + 256], bqn_ref[lrow, :]), cos, sin_lo, sin_hi)
    k = _rope(_head_norm(u_ref[:, COL_BK:COL_BK + 128], bkn_ref[lrow, :]), cos, sin_lo, sin_hi)
    qh = [_head_cols(q * scale, h) for h in range(4)]
    v = u_ref[:, COL_BV:COL_BV + 128]
    kh = [_head_cols(k, kv) for kv in range(2)]
    vh = [_head_cols(v, kv) for kv in range(2)]
    kch = [_head_cols(kcb_ref[...], kv) for kv in range(2)]
    vch = [_head_cols(vcb_ref[...], kv) for kv in range(2)]
    tq = 256
    for n in range(t // tq):
        rows = slice(n * tq, (n + 1) * tq)
        heads = []
        for kv in range(2):
            qp = jnp.concatenate([qh[2 * kv][rows, :], qh[2 * kv + 1][rows, :]], axis=0)
            o = _softmax_pv([_dot_nt(qp, kch[kv]), _dot_nt(qp, kh[kv])], [vch[kv], vh[kv]], None)
            heads += [o[0:tq, :], o[tq:2 * tq, :]]
        o_ref[rows, GROUP_W:2 * GROUP_W] = jnp.concatenate(heads, axis=-1).astype(BF16)

    xn_ref[...] = x + g2_ref[mrow, :] * _dot(o_ref[...], wout_ref[...].astype(BF16))


def _lat_recurrent_kernel(xn_in_ref, h_ref, g2_ref, wc_ref, wqk_ref, wvg_ref, wout_ref, h0_ref,
                          convw_ref, convb_ref, wa_ref, ba_ref, wx_ref, bx_ref, lam_ref,
                          s0_ref, theta_ref, dn_ref,
                          xn_ref, gate_w_ref, af_ref, bf_ref, ab_ref, bb_ref, hf_ref, hb_ref, gel_ref, ret_ref,
                          *, layer):
    lrow = slice(layer, layer + 1)
    mrow = pl.ds(1 + pl.program_id(0), 1)

    @pl.when(pl.program_id(0) == 0)
    def _():
        _store_gate_weights(gate_w_ref, wa_ref, wx_ref)

    h = h_ref[...]
    u = _dot(h, wc_ref[...].astype(BF16))
    _rglru_prepare(u[:, 0:GROUP_W], u[:, GROUP_W:2 * GROUP_W], convw_ref, convb_ref[lrow, :],
                   gate_w_ref, ba_ref, bx_ref, lam_ref, af_ref, bf_ref, ab_ref, bb_ref, gel_ref)
    oc, _, _ = _rglru_finish(h0_ref[0:1, :], h0_ref[1:2, :],
                             af_ref, bf_ref, ab_ref, bb_ref, hf_ref, hb_ref, gel_ref)
    y = _dot(oc.astype(BF16), wout_ref[0:GROUP_W, :].astype(BF16))

    uqk = _dot(h, wqk_ref[...].astype(BF16))
    uvg = _dot(h, wvg_ref[...].astype(BF16))
    masks = _lane_head_masks(GROUP_W)
    lgf, lgb = _log_decays(theta_ref, masks)
    s0 = tuple(_block_diag([s0_ref[d, hd] for hd in range(N_HEADS)]) for d in range(2))
    _retention(uqk[:, 0:GROUP_W], uqk[:, GROUP_W:2 * GROUP_W] * (HEAD_DIM ** -0.5),
               uvg[:, 0:GROUP_W].astype(BF16), s0, lgf, lgb, masks, ret_ref)
    o = ret_ref[...]
    o = o * lax.rsqrt(_head_mean_square(o) + EPS) * dn_ref[lrow, :] * _silu(uvg[:, GROUP_W:2 * GROUP_W])
    y = y + _dot(o.astype(BF16), wout_ref[GROUP_W:2 * GROUP_W, :].astype(BF16))
    xn_ref[...] = xn_in_ref[...] + g2_ref[mrow, :] * y


def _lat_mixers(x, mod, layer, caches, state_c, state_d, rope, norm2_g, w_in, w_out,
                a_qn, a_kn, a_sink, b_qn, b_kn, c_conv_w, c_conv_b, c_wa, c_ba, c_wx, c_bx,
                c_lambda, d_theta, d_norm_g):
    rows = pl.BlockSpec((DEC_SEQ, D_MODEL), lambda b: (LAT_BLOCK0 + b, 0))
    h_rows = pl.BlockSpec((DEC_SEQ, D_MODEL), lambda b: (b, 0))
    cache_spec = pl.BlockSpec((None, None, PAST_LEN, 128), lambda b: (b, layer, 0, 0))
    gain = _full((DEPTH, HEAD_DIM))
    table = _once((DEC_SEQ, 128), lambda b: (0, 0))
    out_shape = jax.ShapeDtypeStruct((N_ROWS, D_MODEL), F32)
    win_cols = lambda w, c: _once((None, D_MODEL, w), lambda b: (layer, 0, c))
    wout_rows = lambda h, r: _once((None, h, D_MODEL), lambda b: (layer, r, 0))

    xn, h = pl.pallas_call(
        functools.partial(_lat_attn_kernel, layer=layer),
        grid=(DEC_BATCH,),
        in_specs=[rows, _full((DEPTH, D_MODEL)),
                  _mod_chunk(layer, 3), _mod_chunk(layer, 4), _mod_chunk(layer, 5),
                  win_cols(4 * GROUP_W, 0), wout_rows(2 * GROUP_W, 0),
                  cache_spec, cache_spec, cache_spec, cache_spec,
                  gain, gain, gain, gain,
                  pl.BlockSpec(memory_space=pltpu.SMEM),
                  table, table, table],
        out_specs=[rows, pl.BlockSpec((DEC_SEQ, D_MODEL), lambda b: (b, 0), pipeline_mode=pl.Buffered(1))],
        out_shape=[out_shape, jax.ShapeDtypeStruct((N_LAT_ROWS, D_MODEL), BF16)],
        input_output_aliases={0: 0},
        scratch_shapes=[pltpu.VMEM((DEC_SEQ, 4 * GROUP_W), F32), pltpu.VMEM((DEC_SEQ, 2 * GROUP_W), BF16)],
        compiler_params=_cparams("arbitrary"),
        name="lat_attention",
    )(x, norm2_g, mod, mod, mod, w_in, w_out, *caches, a_qn, a_kn, b_qn, b_kn, a_sink, *rope)

    scr = pltpu.VMEM((DEC_SEQ, GROUP_W), F32)
    xn = pl.pallas_call(
        functools.partial(_lat_recurrent_kernel, layer=layer),
        grid=(DEC_BATCH,),
        in_specs=[
            rows, h_rows, _mod_chunk(layer, 5),
            win_cols(2 * GROUP_W, COL_CX // (2 * GROUP_W)),
            win_cols(2 * GROUP_W, COL_DQ // (2 * GROUP_W)), win_cols(2 * GROUP_W, COL_DV // (2 * GROUP_W)),
            wout_rows(2 * GROUP_W, 1),
            pl.BlockSpec((None, None, 2, GROUP_W), lambda b: (b, layer, 0, 0)),
            _layer_block((4, GROUP_W), layer), _full((DEPTH, GROUP_W)),
            _layer_block((2, N_HEADS, HEAD_DIM, HEAD_DIM), layer), _layer_block((2, GROUP_W), layer),
            _layer_block((2, N_HEADS, HEAD_DIM, HEAD_DIM), layer), _layer_block((2, GROUP_W), layer),
            _layer_block((2, GROUP_W), layer),
            pl.BlockSpec((None, None, 2, N_HEADS, HEAD_DIM, HEAD_DIM), lambda b: (b, layer, 0, 0, 0, 0)),
            _layer_block((2, N_HEADS), layer), _full((DEPTH, GROUP_W))],
        out_specs=rows,
        out_shape=out_shape,
        input_output_aliases={0: 0},
        scratch_shapes=[pltpu.VMEM((4, GROUP_W, GROUP_W), BF16)] + [scr] * 8,
        compiler_params=_cparams("arbitrary"),
        name="lat_recurrent",
    )(xn, h, mod, w_in, w_in, w_in, w_out, state_c, c_conv_w, c_conv_b, c_wa, c_ba, c_wx, c_bx, c_lambda,
      state_d, d_theta, d_norm_g)
    return xn


def _rope_tables():
    t = np.arange(DEC_SEQ)
    row = (t // GRID_W).astype(np.float64)[:, None]
    col = (t % GRID_W).astype(np.float64)[:, None]
    half = HEAD_DIM // 2
    inv = 1.0 / (ROPE_BASE ** (np.arange(0, half, 2, dtype=np.float64) / half))
    j = np.arange(128) % HEAD_DIM
    ang = np.where((j < half)[None, :], row, col) * inv[j % (half // 2)][None, :]
    first = ((j % half) < half // 2)[None, :]
    cos, sin = np.cos(ang), np.sin(ang)
    return tuple(jnp.asarray(a, F32) for a in (cos, np.where(first, -sin, 0.0), np.where(first, 0.0, sin)))


def kernel(x_prompt, x_sample, cache_a_k, cache_a_v, cache_b_k, cache_b_v, state_c, state_d, c, c_ctx, norm1_g, norm2_g, norm3_g, w_mod, b_mod, ffn1_wg, ffn1_wu, ffn1_wd, ffn2_wg, ffn2_wu, ffn2_wd, w_in, w_out, a_qn, a_kn, a_sink, b_qn, b_kn, c_conv_w, c_conv_b, c_wa, c_ba, c_wx, c_bx, c_lambda, d_theta, d_norm_g):
    mod = _modulation(c_ctx, c, w_mod, b_mod)
    rope = _rope_tables()
    caches = tuple(t.reshape(DEC_BATCH, DEPTH, PAST_LEN, 128) for t in (cache_a_k, cache_a_v, cache_b_k, cache_b_v))
    mixer_params = (a_qn, a_kn, a_sink, b_qn, b_kn, c_conv_w, c_conv_b, c_wa, c_ba, c_wx, c_bx,
                    c_lambda, d_theta, d_norm_g)
    xs = (x_prompt.reshape(N_CTX_ROWS, D_MODEL), x_sample.reshape(N_LAT_ROWS, D_MODEL))
    states = ()
    for l in range(DEPTH):
        (x,) = _ffn(xs, mod, l, 0, norm1_g, ffn1_wg, ffn1_wu, ffn1_wd)
        x, states = _ctx_mixers(x, mod, l, states, norm2_g, w_in, w_out, *mixer_params)
        x = _lat_mixers(x, mod, l, caches, state_c, state_d, rope, norm2_g, w_in, w_out, *mixer_params)
        xs = _ffn((x,), mod, l, 6, norm3_g, ffn2_wg, ffn2_wu, ffn2_wd, split_out=(l == DEPTH - 1))
    y_p, y_s = xs
    ka, va, kb, vb, st_c, st_d = states
    kv_shape = (BATCH, DEPTH, SEQ, 2, HEAD_DIM)
    return (y_p.reshape(BATCH, SEQ, D_MODEL), y_s.reshape(DEC_BATCH, DEC_SEQ, D_MODEL),
            ka.reshape(kv_shape), va.reshape(kv_shape), kb.reshape(kv_shape), vb.reshape(kv_shape),
            st_c, st_d)
```

```python
import functools
import math

import numpy as np
import jax
import jax.numpy as jnp
from jax import lax
from jax.experimental import pallas as pl
from jax.experimental.pallas import tpu as pltpu

F32 = jnp.float32
BF16 = jnp.bfloat16

D_MODEL = 1024
BATCH = 16
SEQ = 256
DEPTH = 2
DEC_BATCH = 2
DEC_SEQ = 1024
PAST_LEN = 512
GRID_W = 64
HEAD_DIM = 64
HEAD_SHIFT = 6
N_HEADS = 4
GROUP_W = 256
WINDOW = 128
ATT_BLOCK = 128
ROPE_BASE = 10000.0
LRU_C = 8.0
D_FF = 2816
N_MOD = 9
EPS = 1e-6
NEG_INF = -1e30
IN_WIDTH = 2560

N_CTX_ROWS = BATCH * SEQ
N_LAT_ROWS = DEC_BATCH * DEC_SEQ
N_ROWS = N_CTX_ROWS + N_LAT_ROWS
MOD_ROWS = 8
MOD_GROUP = 1024

VMEM_LIMIT_BYTES = 56 * 1024 * 1024

COL_AQ, COL_AK, COL_AV = 0, 256, 384
COL_BQ, COL_BK, COL_BV = 512, 768, 896
COL_CX, COL_CY = 1024, 1280
COL_DQ, COL_DK, COL_DV, COL_DG = 1536, 1792, 2048, 2304


def _cparams(*sem):
    return pltpu.CompilerParams(dimension_semantics=sem, vmem_limit_bytes=VMEM_LIMIT_BYTES)


def _dot(a, b):
    return jnp.dot(a, b, preferred_element_type=F32)


def _dot_nt(a, b):
    return lax.dot_general(a, b, (((1,), (1,)), ((), ())), preferred_element_type=F32)


def _dot_tn(a, b):
    return lax.dot_general(a, b, (((0,), (0,)), ((), ())), preferred_element_type=F32)


def _silu(x):
    return x * jax.nn.sigmoid(x)


def _gelu_tanh(x):
    return 0.5 * x * (1.0 + jnp.tanh(math.sqrt(2.0 / math.pi) * (x + 0.044715 * (x * x * x))))


def _mod_row(i, tm, s):
    if tm >= MOD_GROUP:
        block_index = i * (tm // MOD_GROUP) + s
    else:
        block_index = i >> int(math.log2(MOD_GROUP // tm))
    return jnp.maximum(block_index - (N_CTX_ROWS // MOD_GROUP - 1), 0)


def _norm_mod(x, g, sc, sh):
    ms = jnp.mean(x * x, axis=-1, keepdims=True)
    return (x * lax.rsqrt(ms + EPS) * g) * (1.0 + sc) + sh


def _full(shape):
    return pl.BlockSpec(shape, lambda *_: (0,) * len(shape))


def _layer_block(shape, layer):
    return pl.BlockSpec((None,) + shape, lambda *_: (layer,) + (0,) * len(shape))


MOD_TN = 3072


def _mod_kernel(cc_ref, c_ref, w_ref, b_ref, o_ref):
    l = pl.program_id(0)
    pad = jnp.zeros((MOD_ROWS - 1 - DEC_BATCH, D_MODEL), F32)
    cond = jnp.concatenate([cc_ref[...], c_ref[...], pad], axis=0)
    o_ref[...] = _dot(_silu(cond).astype(BF16), w_ref[...].astype(BF16)) + b_ref[pl.ds(l, 1), :]


def _modulation(c_ctx, c, w_mod, b_mod):
    n = N_MOD * D_MODEL
    return pl.pallas_call(
        _mod_kernel,
        grid=(DEPTH, n // MOD_TN),
        in_specs=[
            pl.BlockSpec((1, D_MODEL), lambda l, j: (0, 0)),
            pl.BlockSpec((DEC_BATCH, D_MODEL), lambda l, j: (0, 0)),
            pl.BlockSpec((None, D_MODEL, MOD_TN), lambda l, j: (l, 0, j)),
            pl.BlockSpec((DEPTH, MOD_TN), lambda l, j: (0, j)),
        ],
        out_specs=pl.BlockSpec((None, MOD_ROWS, MOD_TN), lambda l, j: (l, 0, j)),
        out_shape=jax.ShapeDtypeStruct((DEPTH, MOD_ROWS, n), F32),
        compiler_params=_cparams("arbitrary", "arbitrary"),
        name="modulation",
    )(c_ctx.reshape(1, D_MODEL), c, w_mod, b_mod)


FFN_TM = 1024
FFN_TF = 256
N_CTX_TILES = N_CTX_ROWS // FFN_TM


FFN_NJ = D_FF // FFN_TF
N_FFN_TILES = N_ROWS // FFN_TM
N_FFN_STEPS = FFN_NJ + N_FFN_TILES


def _ffn_tile(step):
    return jnp.maximum(step - FFN_NJ, 0)


def _on_stream_part(tile, x_refs, o_refs, fn):
    if len(x_refs) == 1 and len(o_refs) == 1:
        fn(x_refs[0], o_refs[0])
    else:
        pl.when(tile < N_CTX_TILES)(lambda: fn(x_refs[0], o_refs[0]))
        pl.when(tile >= N_CTX_TILES)(lambda: fn(x_refs[-1], o_refs[-1]))


def _ffn_kernel(*refs, layer, n_in, n_out):
    x_refs = refs[:n_in]
    n_ref, sh_ref, sc_ref, g_ref, wg_ref, wu_ref, wd_ref = refs[n_in:n_in + 7]
    o_refs = refs[n_in + 7:n_in + 7 + n_out]
    h_ref, a_ref, wg_s, wu_s, wd_s = refs[n_in + 7 + n_out:]
    nj, tf = FFN_NJ, FFN_TF
    s = pl.program_id(0)
    tile = _ffn_tile(s)
    r = _mod_row(tile, FFN_TM, 0)

    def load_tile():
        def init(x_ref, _):
            h = _norm_mod(x_ref[...], n_ref[layer:layer + 1, :], sc_ref[pl.ds(r, 1), :], sh_ref[pl.ds(r, 1), :])
            h_ref[...] = h.astype(BF16)
        _on_stream_part(tile, x_refs, o_refs, init)

    def up_chunk(j, cols):
        h = h_ref[...]
        a_ref[:, cols] = (_silu(_dot(h, wg_s[j])) * _dot(h, wu_s[j])).astype(BF16)

    def down_and_store():
        y = (0.5 * g_ref[pl.ds(r, 1), :]) * _dot(a_ref[...], wd_s[...])

        def store(x_ref, o_ref):
            o_ref[...] = x_ref[...] + y
        _on_stream_part(tile, x_refs, o_refs, store)

    def keep_arrived_chunk():
        wg_s[s] = wg_ref[...].astype(BF16)
        wu_s[s] = wu_ref[...].astype(BF16)
        wd_s[pl.ds(pl.multiple_of(s * tf, tf), tf), :] = wd_ref[...].astype(BF16)

    def up_previous_chunk():
        up_chunk(s - 1, pl.ds(pl.multiple_of((s - 1) * tf, tf), tf))

    @pl.when(s == 0)
    def _():
        load_tile()
        keep_arrived_chunk()

    @pl.when((s > 0) & (s < nj))
    def _():
        up_previous_chunk()
        keep_arrived_chunk()

    @pl.when(s == nj)
    def _():
        up_previous_chunk()
        down_and_store()

    @pl.when(s > nj)
    def _():
        load_tile()
        for j in range(nj):
            up_chunk(j, slice(j * tf, (j + 1) * tf))
        down_and_store()


def _stream_specs(split, buffered_once):
    tm = FFN_TM
    kw = {"pipeline_mode": pl.Buffered(1)} if buffered_once else {}
    if not split:
        return [pl.BlockSpec((tm, D_MODEL), lambda s: (_ffn_tile(s), 0), **kw)]
    last_ctx = N_CTX_TILES - 1
    return [pl.BlockSpec((tm, D_MODEL), lambda s: (jnp.minimum(_ffn_tile(s), last_ctx), 0), **kw),
            pl.BlockSpec((tm, D_MODEL), lambda s: (jnp.maximum(_ffn_tile(s) - N_CTX_TILES, 0), 0), **kw)]


def _ffn(xs, mod, layer, chunk0, norm_g, wg, wu, wd, split_out=False):
    tm, tf, nj = FFN_TM, FFN_TF, FFN_NJ
    split_in = len(xs) == 2
    mod_spec = lambda c: pl.BlockSpec((None, MOD_ROWS, D_MODEL), lambda s: (layer, 0, c))
    w_col = lambda s: (layer, 0, jnp.minimum(s, nj - 1))
    w_row = lambda s: (layer, jnp.minimum(s, nj - 1), 0)
    if split_out:
        out_shape = [jax.ShapeDtypeStruct((N_CTX_ROWS, D_MODEL), F32),
                     jax.ShapeDtypeStruct((N_LAT_ROWS, D_MODEL), F32)]
    else:
        out_shape = [jax.ShapeDtypeStruct((N_ROWS, D_MODEL), F32)]
    out = pl.pallas_call(
        functools.partial(_ffn_kernel, layer=layer, n_in=len(xs), n_out=len(out_shape)),
        grid=(N_FFN_STEPS,),
        in_specs=_stream_specs(split_in, split_in) + [
            _full((DEPTH, D_MODEL)),
            mod_spec(chunk0), mod_spec(chunk0 + 1), mod_spec(chunk0 + 2),
            pl.BlockSpec((None, D_MODEL, tf), w_col),
            pl.BlockSpec((None, D_MODEL, tf), w_col),
            pl.BlockSpec((None, tf, D_MODEL), w_row),
        ],
        out_specs=_stream_specs(split_out, True),
        out_shape=out_shape,
        scratch_shapes=[pltpu.VMEM((tm, D_MODEL), BF16),
                        pltpu.VMEM((tm, D_FF), BF16),
                        pltpu.VMEM((nj, D_MODEL, tf), BF16),
                        pltpu.VMEM((nj, D_MODEL, tf), BF16),
                        pltpu.VMEM((D_FF, D_MODEL), BF16)],
        compiler_params=_cparams("arbitrary"),
        name="ffn",
    )(*xs, norm_g, mod, mod, mod, wg, wu, wd)
    return tuple(out)


def _once(shape, index_map):
    return pl.BlockSpec(shape, index_map, pipeline_mode=pl.Buffered(1))


def _mod_chunk(layer, c):
    return pl.BlockSpec((None, MOD_ROWS, D_MODEL), lambda *_: (layer, 0, c))


def _head_mean_square(x):
    n = x.shape[-1]
    x2 = x * x
    hi = x2.astype(BF16)
    lo = (x2 - hi.astype(F32)).astype(BF16)
    r = lax.broadcasted_iota(jnp.int32, (n, n), 0) >> HEAD_SHIFT
    c = lax.broadcasted_iota(jnp.int32, (n, n), 1) >> HEAD_SHIFT
    ones_bd = jnp.where(r == c, 1.0, 0.0).astype(BF16)
    return (_dot(hi, ones_bd) + _dot(lo, ones_bd)) * (1.0 / HEAD_DIM)


def _head_norm(x, head_gain):
    gain_row = jnp.concatenate([head_gain] * (x.shape[-1] // HEAD_DIM), axis=-1)
    return x * lax.rsqrt(_head_mean_square(x) + EPS) * gain_row


def _gqa_attention(q, keys, values, valid, sinks):
    tq = q.shape[0]
    low = lax.broadcasted_iota(jnp.int32, (1, 2 * HEAD_DIM), 1) < HEAD_DIM
    qa, qb = q[:, 0:2 * HEAD_DIM], q[:, 2 * HEAD_DIM:4 * HEAD_DIM]
    stacked = jnp.concatenate([
        jnp.where(low, qa, 0.0), jnp.where(low, pltpu.roll(qa, HEAD_DIM, 1), 0.0),
        jnp.where(low, 0.0, pltpu.roll(qb, HEAD_DIM, 1)), jnp.where(low, 0.0, qb)], axis=0).astype(BF16)
    scores = []
    for k, ok in zip(keys, valid):
        s = _dot_nt(stacked, k)
        if ok is not None:
            s = jnp.where(jnp.concatenate([ok] * N_HEADS, axis=0), s, NEG_INF)
        scores.append(s)
    m = jnp.max(scores[0], axis=-1, keepdims=True)
    for s in scores[1:]:
        m = jnp.maximum(m, jnp.max(s, axis=-1, keepdims=True))
    sink = None
    if sinks is not None:
        sink = jnp.concatenate([jnp.full((tq, 1), sk, F32) for sk in sinks], axis=0)
        m = jnp.maximum(m, sink)
    denom = None
    acc = None
    for s, v in zip(scores, values):
        p = jnp.exp(s - m)
        d = jnp.sum(p, axis=-1, keepdims=True)
        o = _dot(p.astype(BF16), v)
        denom = d if denom is None else denom + d
        acc = o if acc is None else acc + o
    if sink is not None:
        denom = denom + jnp.exp(sink - m)
    out = acc / denom
    o0, o1, o2, o3 = (out[hd * tq:(hd + 1) * tq, :] for hd in range(N_HEADS))
    return jnp.concatenate([jnp.where(low, o0, pltpu.roll(o1, HEAD_DIM, 1)),
                            jnp.where(low, pltpu.roll(o2, HEAD_DIM, 1), o3)], axis=-1)


def _rope(x, cos, sin_lo, sin_hi):
    cols = []
    for c in range(x.shape[-1] // 128):
        xc = x[:, c * 128:(c + 1) * 128]
        cols.append(xc * cos + pltpu.roll(xc, 112, 1) * sin_lo + pltpu.roll(xc, 16, 1) * sin_hi)
    return cols[0] if len(cols) == 1 else jnp.concatenate(cols, axis=-1)


def _block_diag(blocks):
    n = len(blocks)
    w = blocks[0].shape[0]
    rows = []
    for k, blk in enumerate(blocks):
        parts = []
        if k > 0:
            parts.append(jnp.zeros((w, k * w), F32))
        parts.append(blk)
        if k < n - 1:
            parts.append(jnp.zeros((w, (n - 1 - k) * w), F32))
        rows.append(jnp.concatenate(parts, axis=-1))
    return jnp.concatenate(rows, axis=0)


def _rglru_gates(xc, wa, ba, wx, bx, lam):
    xb = xc.astype(BF16)
    r = jax.nn.sigmoid(_dot(xb, wa) + ba)
    i = jax.nn.sigmoid(_dot(xb, wx) + bx)
    softplus = jnp.maximum(-lam, 0.0) + jnp.log1p(jnp.exp(-jnp.abs(lam)))
    log_a = (-LRU_C) * r * softplus
    a = jnp.exp(log_a)
    b = jnp.sqrt(1.0 - a * a) * (i * xc)
    return a, b


def _block_prefix(a, b, reverse):
    t = a.shape[0]
    row = lax.broadcasted_iota(jnp.int32, a.shape, 0) & 7
    for d in (1, 2, 4):
        if reverse:
            a_s = pltpu.roll(a, t - d, 0)
            b_s = pltpu.roll(b, t - d, 0)
            ok = row < 8 - d
        else:
            a_s = pltpu.roll(a, d, 0)
            b_s = pltpu.roll(b, d, 0)
            ok = row >= d
        b = jnp.where(ok, a * b_s + b, b)
        a = jnp.where(ok, a * a_s, a)
    return a, b


def _conv4(x, w_ref, b_row):
    t = x.shape[0]
    row = lax.broadcasted_iota(jnp.int32, x.shape, 0)
    xm2 = jnp.where(row >= 2, pltpu.roll(x, 2, 0), 0.0)
    xm1 = jnp.where(row >= 1, pltpu.roll(x, 1, 0), 0.0)
    xp1 = jnp.where(row < t - 1, pltpu.roll(x, t - 1, 0), 0.0)
    return (xm2 * w_ref[0:1, :] + xm1 * w_ref[1:2, :] + x * w_ref[2:3, :] + xp1 * w_ref[3:4, :]) + b_row


def _rglru_prepare(cx, cy, conv_w_ref, conv_b, gate_w_ref, ba_ref, bx_ref, lam_ref,
                   af_ref, bf_ref, ab_ref, bb_ref, gel_ref):
    xc = _conv4(cx, conv_w_ref, conv_b)
    a, b = _rglru_gates(xc, gate_w_ref[0], ba_ref[0:1, :], gate_w_ref[1], bx_ref[0:1, :], lam_ref[0:1, :])
    a, b = _block_prefix(a, b, reverse=False)
    af_ref[...] = a
    bf_ref[...] = b
    a, b = _rglru_gates(xc, gate_w_ref[2], ba_ref[1:2, :], gate_w_ref[3], bx_ref[1:2, :], lam_ref[1:2, :])
    a, b = _block_prefix(a, b, reverse=True)
    ab_ref[...] = a
    bb_ref[...] = b
    gel_ref[...] = _gelu_tanh(cy)


def _rglru_finish(h0f, h0b, af_ref, bf_ref, ab_ref, bb_ref, hf_ref, hb_ref, gel_ref):
    nblk = af_ref.shape[0] // 8

    def body(k, carry):
        cf, cb = carry
        rf = pl.ds(pl.multiple_of(k * 8, 8), 8)
        hf = bf_ref[rf, :] + af_ref[rf, :] * cf
        hf_ref[rf, :] = hf
        rb = pl.ds(pl.multiple_of((nblk - 1 - k) * 8, 8), 8)
        hb = bb_ref[rb, :] + ab_ref[rb, :] * cb
        hb_ref[rb, :] = hb
        return hf[7:8, :], hb[0:1, :]

    cf, cb = lax.fori_loop(0, nblk, body, (h0f, h0b))
    oc = (hf_ref[...] + hb_ref[...]) * gel_ref[...]
    return oc, cf, cb


def _store_gate_weights(gate_w_ref, wa_ref, wx_ref):
    for d in range(2):
        gate_w_ref[2 * d] = _block_diag([wa_ref[d, n] for n in range(N_HEADS)]).astype(BF16)
        gate_w_ref[2 * d + 1] = _block_diag([wx_ref[d, n] for n in range(N_HEADS)]).astype(BF16)


def _lane_head_masks(n):
    lane = lax.broadcasted_iota(jnp.int32, (1, n), 1) >> HEAD_SHIFT
    return [jnp.where(lane == h, 1.0, 0.0) for h in range(n // HEAD_DIM)]


def _log_decays(theta_ref, masks):
    theta = theta_ref[...]
    lanes = theta[:, 0:1] * masks[0]
    for h in range(1, N_HEADS):
        lanes = lanes + theta[:, h:h + 1] * masks[h]
    lg = jnp.log1p(-jnp.exp(lanes))
    return lg[0:1, :], lg[1:2, :]


RET_BLOCK = 256


def _retention(q, k8, vb, s0, lgf, lgb, masks, o_ref):
    t, w = q.shape
    c = RET_BLOCK
    nh = w // HEAD_DIM
    pos = lax.broadcasted_iota(jnp.int32, (c, w), 0).astype(F32)
    q_dec = (jnp.exp(lgf * (pos + 1.0)), jnp.exp(lgb * (float(c) - pos)))
    k_dec = (jnp.exp(lgf * (float(c - 1) - pos)), jnp.exp(lgb * pos))
    chunk_dec = (jnp.exp(lgf * float(c)), jnp.exp(lgb * float(c)))
    rel = (lax.broadcasted_iota(jnp.int32, (c, c), 0) - lax.broadcasted_iota(jnp.int32, (c, c), 1)).astype(F32)
    decs = []
    for h in range(nh):
        gf = lgf[:, h * HEAD_DIM:h * HEAD_DIM + 1]
        gb = lgb[:, h * HEAD_DIM:h * HEAD_DIM + 1]
        e = jnp.exp(jnp.where(rel >= 0, gf * rel, gb * (-rel)))
        decs.append(jnp.where(rel == 0, 2.0, e))
    dec = jnp.concatenate(decs, axis=0)
    r_head = lax.broadcasted_iota(jnp.int32, (w, w), 0) >> HEAD_SHIFT
    c_head = lax.broadcasted_iota(jnp.int32, (w, w), 1) >> HEAD_SHIFT
    same_head = jnp.where(r_head == c_head, 1.0, 0.0)
    states = [None, None] if s0 is None else list(s0)

    def carry(d, rows, o):
        if states[d] is not None:
            o = o + _dot((q[rows, :] * q_dec[d]).astype(BF16), states[d].astype(BF16))
        upd = _dot_tn((k8[rows, :] * k_dec[d]).astype(BF16), vb[rows, :]) * same_head
        states[d] = upd if states[d] is None else states[d] * chunk_dec[d] + upd
        return o

    for ci in range(t // c):
        rows = slice(ci * c, (ci + 1) * c)
        qc = q[rows, :]
        q_stack = jnp.concatenate([(qc * masks[h]).astype(BF16) for h in range(nh)], axis=0)
        inner = (_dot_nt(q_stack, k8[rows, :].astype(BF16)) * dec).astype(BF16)
        out = _dot(inner, vb[rows, :])
        o = out[0:c, :] * masks[0]
        for h in range(1, nh):
            o = o + out[h * c:(h + 1) * c, :] * masks[h]
        o_ref[rows, :] = carry(0, rows, o)
    for ci in reversed(range(t // c)):
        rows = slice(ci * c, (ci + 1) * c)
        if states[1] is not None:
            o_ref[rows, :] = carry(1, rows, o_ref[rows, :])
        else:
            carry(1, rows, None)
    return states[0], states[1]


def _ctx_mixer_kernel(*refs, layer, n_prev):
    prev_refs = refs[:n_prev]
    (x_ref, n2_ref, sh_ref, sc_ref, g2_ref, win_ref, wout_ref,
     aqn_ref, akn_ref, bqn_ref, bkn_ref, sink_ref,
     convw_ref, convb_ref, wa_ref, ba_ref, wx_ref, bx_ref, lam_ref, theta_ref, dn_ref,
     xn_ref, *state_refs) = refs[n_prev:n_prev + 28]
    (win_s, wout_s, u_ref, mixed_ref,
     gate_w_ref, af_ref, bf_ref, ab_ref, bb_ref, hf_ref, hb_ref, gel_ref, ret_ref) = refs[n_prev + 28:]
    t = SEQ
    lrow = slice(layer, layer + 1)
    for prev_ref, state_ref in zip(prev_refs, state_refs):
        for earlier in range(layer):
            state_ref[earlier] = prev_ref[earlier]
    ka_ref, va_ref, kb_ref, vb_ref, stc_ref, std_ref = (ref.at[layer] for ref in state_refs)

    @pl.when(pl.program_id(0) == 0)
    def _():
        for c in range(IN_WIDTH // 512):
            win_s[:, c * 512:(c + 1) * 512] = win_ref[:, c * 512:(c + 1) * 512].astype(BF16)
        wout_s[...] = wout_ref[...].astype(BF16)
        _store_gate_weights(gate_w_ref, wa_ref, wx_ref)

    x = x_ref[...]
    h = _norm_mod(x, n2_ref[lrow, :], sc_ref[0:1, :], sh_ref[0:1, :]).astype(BF16)
    u_ref[...] = _dot(h, win_s[...])

    for (cq, ck, cv, qn_ref, kn_ref, k_out, v_out, col0, use_sink) in (
            (COL_AQ, COL_AK, COL_AV, aqn_ref, akn_ref, ka_ref, va_ref, 0, True),
            (COL_BQ, COL_BK, COL_BV, bqn_ref, bkn_ref, kb_ref, vb_ref, GROUP_W, False)):
        q = _head_norm(u_ref[:, cq:cq + 256], qn_ref[lrow, :])
        k = _head_norm(u_ref[:, ck:ck + 128], kn_ref[lrow, :])
        v = u_ref[:, cv:cv + 128]
        k_out[...] = k
        v_out[...] = v
        sinks = [sink_ref[layer, hd] for hd in range(N_HEADS)] if use_sink else None
        o = _gqa_attention(q * (HEAD_DIM ** -0.5), [k.astype(BF16)], [v.astype(BF16)], [None], sinks)
        mixed_ref[:, col0:col0 + GROUP_W] = o.astype(BF16)

    _rglru_prepare(u_ref[:, COL_CX:COL_CX + GROUP_W], u_ref[:, COL_CY:COL_CY + GROUP_W],
                   convw_ref, convb_ref[lrow, :], gate_w_ref, ba_ref, bx_ref, lam_ref,
                   af_ref, bf_ref, ab_ref, bb_ref, gel_ref)
    zero = jnp.zeros((1, GROUP_W), F32)
    oc, cf, cb = _rglru_finish(zero, zero, af_ref, bf_ref, ab_ref, bb_ref, hf_ref, hb_ref, gel_ref)
    mixed_ref[:, 2 * GROUP_W:3 * GROUP_W] = oc.astype(BF16)
    stc_ref[0:1, :] = cf
    stc_ref[1:2, :] = cb

    masks = _lane_head_masks(GROUP_W)
    lgf, lgb = _log_decays(theta_ref, masks)
    k8 = u_ref[:, COL_DK:COL_DK + GROUP_W] * (HEAD_DIM ** -0.5)
    vb = u_ref[:, COL_DV:COL_DV + GROUP_W].astype(BF16)
    final_states = _retention(u_ref[:, COL_DQ:COL_DQ + GROUP_W], k8, vb, None, lgf, lgb, masks, ret_ref)
    o = ret_ref[...]
    o = o * lax.rsqrt(_head_mean_square(o) + EPS) * dn_ref[lrow, :] * _silu(u_ref[:, COL_DG:COL_DG + GROUP_W])
    mixed_ref[:, 3 * GROUP_W:4 * GROUP_W] = o.astype(BF16)
    for d, s_full in enumerate(final_states):
        for hd in range(N_HEADS):
            std_ref[d, hd] = s_full[hd * 64:(hd + 1) * 64, hd * 64:(hd + 1) * 64]

    xn_ref[...] = x + g2_ref[0:1, :] * _dot(mixed_ref[...], wout_s[...])


def _ctx_mixers(x, mod, layer, prev, norm2_g, w_in, w_out,
                a_qn, a_kn, a_sink, b_qn, b_kn, c_conv_w, c_conv_b, c_wa, c_ba, c_wx, c_bx,
                c_lambda, d_theta, d_norm_g):
    per_request = lambda slots, shape: pl.BlockSpec((None, slots) + shape, lambda b: (b,) + (0,) * (1 + len(shape)))
    state_dims = [(SEQ, 128)] * 4 + [(2, GROUP_W), (2, N_HEADS, HEAD_DIM, HEAD_DIM)]
    scr = pltpu.VMEM((SEQ, GROUP_W), F32)
    out = pl.pallas_call(
        functools.partial(_ctx_mixer_kernel, layer=layer, n_prev=len(prev)),
        grid=(BATCH,),
        in_specs=[per_request(layer, dims) for dims in state_dims[:len(prev)]] + [
            pl.BlockSpec((SEQ, D_MODEL), lambda b: (b, 0)),
            _full((DEPTH, D_MODEL)),
            _mod_chunk(layer, 3), _mod_chunk(layer, 4), _mod_chunk(layer, 5),
            _once((None, D_MODEL, IN_WIDTH), lambda b: (layer, 0, 0)),
            _once((None, D_MODEL, D_MODEL), lambda b: (layer, 0, 0)),
            _full((DEPTH, HEAD_DIM)), _full((DEPTH, HEAD_DIM)), _full((DEPTH, HEAD_DIM)), _full((DEPTH, HEAD_DIM)),
            pl.BlockSpec(memory_space=pltpu.SMEM),
            _layer_block((4, GROUP_W), layer), _full((DEPTH, GROUP_W)),
            _layer_block((2, N_HEADS, HEAD_DIM, HEAD_DIM), layer), _layer_block((2, GROUP_W), layer),
            _layer_block((2, N_HEADS, HEAD_DIM, HEAD_DIM), layer), _layer_block((2, GROUP_W), layer),
            _layer_block((2, GROUP_W), layer),
            _layer_block((2, N_HEADS), layer), _full((DEPTH, GROUP_W)),
        ],
        out_specs=[pl.BlockSpec((SEQ, D_MODEL), lambda b: (b, 0))] + [
            per_request(layer + 1, dims) for dims in state_dims],
        out_shape=[jax.ShapeDtypeStruct((N_ROWS, D_MODEL), F32)] + [
            jax.ShapeDtypeStruct((BATCH, layer + 1) + dims, F32) for dims in state_dims],
        input_output_aliases={len(prev): 0},
        scratch_shapes=[pltpu.VMEM((D_MODEL, IN_WIDTH), BF16), pltpu.VMEM((D_MODEL, D_MODEL), BF16),
                        pltpu.VMEM((SEQ, IN_WIDTH), F32), pltpu.VMEM((SEQ, D_MODEL), BF16),
                        pltpu.VMEM((4, GROUP_W, GROUP_W), BF16)] + [scr] * 8,
        compiler_params=_cparams("arbitrary"),
        name="ctx_mixers",
    )(*prev, x, norm2_g, mod, mod, mod, w_in, w_out,
      a_qn, a_kn, b_qn, b_kn, a_sink, c_conv_w, c_conv_b, c_wa, c_ba, c_wx, c_bx,
      c_lambda, d_theta, d_norm_g)
    return out[0], tuple(out[1:])


LAT_BLOCK0 = N_CTX_ROWS // DEC_SEQ


def _lat_attn_kernel(x_ref, n2_ref, sh_ref, sc_ref, g2_ref, win_ref, wout_ref,
                     kca_ref, vca_ref, kcb_ref, vcb_ref,
                     aqn_ref, akn_ref, bqn_ref, bkn_ref, sink_ref, cos_ref, sinl_ref, sinh_ref,
                     xn_ref, h_ref, u_ref, o_ref, kv_ref, *, layer):
    t = DEC_SEQ
    lrow = slice(layer, layer + 1)
    mrow = pl.ds(1 + pl.program_id(0), 1)
    cos, sin_lo, sin_hi = cos_ref[...], sinl_ref[...], sinh_ref[...]
    scale = HEAD_DIM ** -0.5
    x = x_ref[...]
    h_ref[...] = _norm_mod(x, n2_ref[lrow, :], sc_ref[mrow, :], sh_ref[mrow, :]).astype(BF16)
    u_ref[...] = _dot(h_ref[...], win_ref[...].astype(BF16))

    q = _rope(_head_norm(u_ref[:, COL_AQ:COL_AQ + 256], aqn_ref[lrow, :]), cos, sin_lo, sin_hi)
    k = _rope(_head_norm(u_ref[:, COL_AK:COL_AK + 128], akn_ref[lrow, :]), cos, sin_lo, sin_hi)
    u_ref[:, COL_AQ:COL_AQ + 256] = q * scale
    kv_ref[0] = k.astype(BF16)
    kv_ref[1] = u_ref[:, COL_AV:COL_AV + 128].astype(BF16)
    kc = kca_ref[...].astype(BF16)
    vc = vca_ref[...].astype(BF16)
    sinks = [sink_ref[layer, hd] for hd in range(N_HEADS)]
    w = ATT_BLOCK
    span = 3 * w

    def window_block(n, carry):
        rows = pl.ds(pl.multiple_of(n * w, w), w)
        start = pl.multiple_of(jnp.clip((n - 1) * w, 0, t - span), w)
        band = pl.ds(start, span)
        qpos = lax.broadcasted_iota(jnp.int32, (w, span), 0) + n * w
        kpos = lax.broadcasted_iota(jnp.int32, (w, span), 1) + start
        valid = jnp.abs(qpos - kpos) <= WINDOW
        o = _gqa_attention(u_ref[rows, COL_AQ:COL_AQ + 256], [kc, kv_ref[0, band, :]],
                           [vc, kv_ref[1, band, :]], [None, valid], sinks)
        o_ref[rows, 0:GROUP_W] = o.astype(BF16)
        return carry

    lax.fori_loop(0, t // w, window_block, 0)

    q = _rope(_head_norm(u_ref[:, COL_BQ:COL_BQ + 256], bqn_ref[lrow, :]), cos, sin_lo, sin_hi)
    k = _rope(_head_norm(u_ref[:, COL_BK:COL_BK + 128], bkn_ref[lrow, :]), cos, sin_lo, sin_hi)
    u_ref[:, COL_BQ:COL_BQ + 256] = q * scale
    kv_ref[2] = k.astype(BF16)
    kv_ref[3] = u_ref[:, COL_BV:COL_BV + 128].astype(BF16)
    kc = kcb_ref[...].astype(BF16)
    vc = vcb_ref[...].astype(BF16)

    def dense_block(n, carry):
        rows = pl.ds(pl.multiple_of(n * w, w), w)
        o = _gqa_attention(u_ref[rows, COL_BQ:COL_BQ + 256], [kc, kv_ref[2]], [vc, kv_ref[3]],
                           [None, None], None)
        o_ref[rows, GROUP_W:2 * GROUP_W] = o.astype(BF16)
        return carry

    lax.fori_loop(0, t // w, dense_block, 0)

    xn_ref[...] = x + g2_ref[mrow, :] * _dot(o_ref[...], wout_ref[...].astype(BF16))


def _lat_recurrent_kernel(xn_in_ref, h_ref, g2_ref, wc_ref, wqk_ref, wvg_ref, wout_ref, h0_ref,
                          convw_ref, convb_ref, wa_ref, ba_ref, wx_ref, bx_ref, lam_ref,
                          s0_ref, theta_ref, dn_ref,
                          xn_ref, gate_w_ref, af_ref, bf_ref, ab_ref, bb_ref, hf_ref, hb_ref, gel_ref, ret_ref,
                          *, layer):
    lrow = slice(layer, layer + 1)
    mrow = pl.ds(1 + pl.program_id(0), 1)

    @pl.when(pl.program_id(0) == 0)
    def _():
        _store_gate_weights(gate_w_ref, wa_ref, wx_ref)

    h = h_ref[...]
    u = _dot(h, wc_ref[...].astype(BF16))
    _rglru_prepare(u[:, 0:GROUP_W], u[:, GROUP_W:2 * GROUP_W], convw_ref, convb_ref[lrow, :],
                   gate_w_ref, ba_ref, bx_ref, lam_ref, af_ref, bf_ref, ab_ref, bb_ref, gel_ref)
    oc, _, _ = _rglru_finish(h0_ref[0:1, :], h0_ref[1:2, :],
                             af_ref, bf_ref, ab_ref, bb_ref, hf_ref, hb_ref, gel_ref)
    y = _dot(oc.astype(BF16), wout_ref[0:GROUP_W, :].astype(BF16))

    uqk = _dot(h, wqk_ref[...].astype(BF16))
    uvg = _dot(h, wvg_ref[...].astype(BF16))
    masks = _lane_head_masks(GROUP_W)
    lgf, lgb = _log_decays(theta_ref, masks)
    s0 = tuple(_block_diag([s0_ref[d, hd] for hd in range(N_HEADS)]) for d in range(2))
    _retention(uqk[:, 0:GROUP_W], uqk[:, GROUP_W:2 * GROUP_W] * (HEAD_DIM ** -0.5),
               uvg[:, 0:GROUP_W].astype(BF16), s0, lgf, lgb, masks, ret_ref)
    o = ret_ref[...]
    o = o * lax.rsqrt(_head_mean_square(o) + EPS) * dn_ref[lrow, :] * _silu(uvg[:, GROUP_W:2 * GROUP_W])
    y = y + _dot(o.astype(BF16), wout_ref[GROUP_W:2 * GROUP_W, :].astype(BF16))
    xn_ref[...] = xn_in_ref[...] + g2_ref[mrow, :] * y


def _lat_mixers(x, mod, layer, caches, state_c, state_d, rope, norm2_g, w_in, w_out,
                a_qn, a_kn, a_sink, b_qn, b_kn, c_conv_w, c_conv_b, c_wa, c_ba, c_wx, c_bx,
                c_lambda, d_theta, d_norm_g):
    rows = pl.BlockSpec((DEC_SEQ, D_MODEL), lambda b: (LAT_BLOCK0 + b, 0))
    h_rows = pl.BlockSpec((DEC_SEQ, D_MODEL), lambda b: (b, 0))
    cache_spec = pl.BlockSpec((None, None, PAST_LEN, 128), lambda b: (b, layer, 0, 0))
    gain = _full((DEPTH, HEAD_DIM))
    table = _once((DEC_SEQ, 128), lambda b: (0, 0))
    out_shape = jax.ShapeDtypeStruct((N_ROWS, D_MODEL), F32)
    win_cols = lambda w, c: _once((None, D_MODEL, w), lambda b: (layer, 0, c))
    wout_rows = lambda h, r: _once((None, h, D_MODEL), lambda b: (layer, r, 0))

    xn, h = pl.pallas_call(
        functools.partial(_lat_attn_kernel, layer=layer),
        grid=(DEC_BATCH,),
        in_specs=[rows, _full((DEPTH, D_MODEL)),
                  _mod_chunk(layer, 3), _mod_chunk(layer, 4), _mod_chunk(layer, 5),
                  win_cols(4 * GROUP_W, 0), wout_rows(2 * GROUP_W, 0),
                  cache_spec, cache_spec, cache_spec, cache_spec,
                  gain, gain, gain, gain,
                  pl.BlockSpec(memory_space=pltpu.SMEM),
                  table, table, table],
        out_specs=[rows, pl.BlockSpec((DEC_SEQ, D_MODEL), lambda b: (b, 0), pipeline_mode=pl.Buffered(1))],
        out_shape=[out_shape, jax.ShapeDtypeStruct((N_LAT_ROWS, D_MODEL), BF16)],
        input_output_aliases={0: 0},
        scratch_shapes=[pltpu.VMEM((DEC_SEQ, 4 * GROUP_W), F32), pltpu.VMEM((DEC_SEQ, 2 * GROUP_W), BF16),
                        pltpu.VMEM((4, DEC_SEQ, 2 * HEAD_DIM), BF16)],
        compiler_params=_cparams("arbitrary"),
        name="lat_attention",
    )(x, norm2_g, mod, mod, mod, w_in, w_out, *caches, a_qn, a_kn, b_qn, b_kn, a_sink, *rope)

    scr = pltpu.VMEM((DEC_SEQ, GROUP_W), F32)
    xn = pl.pallas_call(
        functools.partial(_lat_recurrent_kernel, layer=layer),
        grid=(DEC_BATCH,),
        in_specs=[
            rows, h_rows, _mod_chunk(layer, 5),
            win_cols(2 * GROUP_W, COL_CX // (2 * GROUP_W)),
            win_cols(2 * GROUP_W, COL_DQ // (2 * GROUP_W)), win_cols(2 * GROUP_W, COL_DV // (2 * GROUP_W)),
            wout_rows(2 * GROUP_W, 1),
            pl.BlockSpec((None, None, 2, GROUP_W), lambda b: (b, layer, 0, 0)),
            _layer_block((4, GROUP_W), layer), _full((DEPTH, GROUP_W)),
            _layer_block((2, N_HEADS, HEAD_DIM, HEAD_DIM), layer), _layer_block((2, GROUP_W), layer),
            _layer_block((2, N_HEADS, HEAD_DIM, HEAD_DIM), layer), _layer_block((2, GROUP_W), layer),
            _layer_block((2, GROUP_W), layer),
            pl.BlockSpec((None, None, 2, N_HEADS, HEAD_DIM, HEAD_DIM), lambda b: (b, layer, 0, 0, 0, 0)),
            _layer_block((2, N_HEADS), layer), _full((DEPTH, GROUP_W))],
        out_specs=rows,
        out_shape=out_shape,
        input_output_aliases={0: 0},
        scratch_shapes=[pltpu.VMEM((4, GROUP_W, GROUP_W), BF16)] + [scr] * 8,
        compiler_params=_cparams("arbitrary"),
        name="lat_recurrent",
    )(xn, h, mod, w_in, w_in, w_in, w_out, state_c, c_conv_w, c_conv_b, c_wa, c_ba, c_wx, c_bx, c_lambda,
      state_d, d_theta, d_norm_g)
    return xn


def _rope_tables():
    t = np.arange(DEC_SEQ)
    row = (t // GRID_W).astype(np.float64)[:, None]
    col = (t % GRID_W).astype(np.float64)[:, None]
    half = HEAD_DIM // 2
    inv = 1.0 / (ROPE_BASE ** (np.arange(0, half, 2, dtype=np.float64) / half))
    j = np.arange(128) % HEAD_DIM
    ang = np.where((j < half)[None, :], row, col) * inv[j % (half // 2)][None, :]
    first = ((j % half) < half // 2)[None, :]
    cos, sin = np.cos(ang), np.sin(ang)
    return tuple(jnp.asarray(a, F32) for a in (cos, np.where(first, -sin, 0.0), np.where(first, 0.0, sin)))


def kernel(x_prompt, x_sample, cache_a_k, cache_a_v, cache_b_k, cache_b_v, state_c, state_d, c, c_ctx, norm1_g, norm2_g, norm3_g, w_mod, b_mod, ffn1_wg, ffn1_wu, ffn1_wd, ffn2_wg, ffn2_wu, ffn2_wd, w_in, w_out, a_qn, a_kn, a_sink, b_qn, b_kn, c_conv_w, c_conv_b, c_wa, c_ba, c_wx, c_bx, c_lambda, d_theta, d_norm_g):
    mod = _modulation(c_ctx, c, w_mod, b_mod)
    rope = _rope_tables()
    caches = tuple(t.reshape(DEC_BATCH, DEPTH, PAST_LEN, 128) for t in (cache_a_k, cache_a_v, cache_b_k, cache_b_v))
    mixer_params = (a_qn, a_kn, a_sink, b_qn, b_kn, c_conv_w, c_conv_b, c_wa, c_ba, c_wx, c_bx,
                    c_lambda, d_theta, d_norm_g)
    xs = (x_prompt.reshape(N_CTX_ROWS, D_MODEL), x_sample.reshape(N_LAT_ROWS, D_MODEL))
    states = ()
    for l in range(DEPTH):
        (x,) = _ffn(xs, mod, l, 0, norm1_g, ffn1_wg, ffn1_wu, ffn1_wd)
        x, states = _ctx_mixers(x, mod, l, states, norm2_g, w_in, w_out, *mixer_params)
        x = _lat_mixers(x, mod, l, caches, state_c, state_d, rope, norm2_g, w_in, w_out, *mixer_params)
        xs = _ffn((x,), mod, l, 6, norm3_g, ffn2_wg, ffn2_wu, ffn2_wd, split_out=(l == DEPTH - 1))
    y_p, y_s = xs
    ka, va, kb, vb, st_c, st_d = states
    kv_shape = (BATCH, DEPTH, SEQ, 2, HEAD_DIM)
    return (y_p.reshape(BATCH, SEQ, D_MODEL), y_s.reshape(DEC_BATCH, DEC_SEQ, D_MODEL),
            ka.reshape(kv_shape), va.reshape(kv_shape), kb.reshape(kv_shape), vb.reshape(kv_shape),
            st_c, st_d)
```

```python
import functools
import math

import numpy as np
import jax
import jax.numpy as jnp
from jax import lax
from jax.experimental import pallas as pl
from jax.experimental.pallas import tpu as pltpu

F32 = jnp.float32
BF16 = jnp.bfloat16

D_MODEL = 1024
BATCH = 16
SEQ = 256
DEPTH = 2
DEC_BATCH = 2
DEC_SEQ = 1024
PAST_LEN = 512
GRID_W = 64
HEAD_DIM = 64
HEAD_SHIFT = 6
N_HEADS = 4
GROUP_W = 256
WINDOW = 128
ATT_BLOCK = 128
ROPE_BASE = 10000.0
LRU_C = 8.0
D_FF = 2816
N_MOD = 9
EPS = 1e-6
NEG_INF = -1e30
IN_WIDTH = 2560

N_CTX_ROWS = BATCH * SEQ
N_LAT_ROWS = DEC_BATCH * DEC_SEQ
N_ROWS = N_CTX_ROWS + N_LAT_ROWS
MOD_ROWS = 8
MOD_GROUP = 1024

VMEM_LIMIT_BYTES = 56 * 1024 * 1024

COL_AQ, COL_AK, COL_AV = 0, 256, 384
COL_BQ, COL_BK, COL_BV = 512, 768, 896
COL_CX, COL_CY = 1024, 1280
COL_DQ, COL_DK, COL_DV, COL_DG = 1536, 1792, 2048, 2304


def _cparams(*sem):
    return pltpu.CompilerParams(dimension_semantics=sem, vmem_limit_bytes=VMEM_LIMIT_BYTES)


def _dot(a, b):
    return jnp.dot(a, b, preferred_element_type=F32)


def _dot_nt(a, b):
    return lax.dot_general(a, b, (((1,), (1,)), ((), ())), preferred_element_type=F32)


def _dot_tn(a, b):
    return lax.dot_general(a, b, (((0,), (0,)), ((), ())), preferred_element_type=F32)


def _sigmoid(x):
    return 0.5 * jnp.tanh(0.5 * x) + 0.5


def _silu(x):
    return x * _sigmoid(x)


def _gelu_tanh(x):
    return 0.5 * x * (1.0 + jnp.tanh(math.sqrt(2.0 / math.pi) * (x + 0.044715 * (x * x * x))))


def _mod_row(i, tm, s):
    if tm >= MOD_GROUP:
        block_index = i * (tm // MOD_GROUP) + s
    else:
        block_index = i >> int(math.log2(MOD_GROUP // tm))
    return jnp.maximum(block_index - (N_CTX_ROWS // MOD_GROUP - 1), 0)


def _norm_mod(x, g, sc, sh):
    ms = jnp.mean(x * x, axis=-1, keepdims=True)
    return (x * lax.rsqrt(ms + EPS) * g) * (1.0 + sc) + sh


def _full(shape):
    return pl.BlockSpec(shape, lambda *_: (0,) * len(shape))


def _layer_block(shape, layer):
    return pl.BlockSpec((None,) + shape, lambda *_: (layer,) + (0,) * len(shape))


MOD_TN = 3072


def _mod_kernel(cc_ref, c_ref, w_ref, b_ref, o_ref):
    l = pl.program_id(0)
    pad = jnp.zeros((MOD_ROWS - 1 - DEC_BATCH, D_MODEL), F32)
    cond = jnp.concatenate([cc_ref[...], c_ref[...], pad], axis=0)
    o_ref[...] = _dot(_silu(cond).astype(BF16), w_ref[...].astype(BF16)) + b_ref[pl.ds(l, 1), :]


def _modulation(c_ctx, c, w_mod, b_mod):
    n = N_MOD * D_MODEL
    return pl.pallas_call(
        _mod_kernel,
        grid=(DEPTH, n // MOD_TN),
        in_specs=[
            pl.BlockSpec((1, D_MODEL), lambda l, j: (0, 0)),
            pl.BlockSpec((DEC_BATCH, D_MODEL), lambda l, j: (0, 0)),
            pl.BlockSpec((None, D_MODEL, MOD_TN), lambda l, j: (l, 0, j)),
            pl.BlockSpec((DEPTH, MOD_TN), lambda l, j: (0, j)),
        ],
        out_specs=pl.BlockSpec((None, MOD_ROWS, MOD_TN), lambda l, j: (l, 0, j)),
        out_shape=jax.ShapeDtypeStruct((DEPTH, MOD_ROWS, n), F32),
        compiler_params=_cparams("arbitrary", "arbitrary"),
        name="modulation",
    )(c_ctx.reshape(1, D_MODEL), c, w_mod, b_mod)


FFN_TM = 1024
FFN_TF = 256
N_CTX_TILES = N_CTX_ROWS // FFN_TM


FFN_NJ = D_FF // FFN_TF
N_FFN_TILES = N_ROWS // FFN_TM
N_FFN_STEPS = FFN_NJ + N_FFN_TILES


def _ffn_tile(step):
    return jnp.maximum(step - FFN_NJ, 0)


def _on_stream_part(tile, x_refs, o_refs, fn):
    if len(x_refs) == 1 and len(o_refs) == 1:
        fn(x_refs[0], o_refs[0])
    else:
        pl.when(tile < N_CTX_TILES)(lambda: fn(x_refs[0], o_refs[0]))
        pl.when(tile >= N_CTX_TILES)(lambda: fn(x_refs[-1], o_refs[-1]))


def _ffn_kernel(*refs, layer, n_in, n_out):
    x_refs = refs[:n_in]
    n_ref, sh_ref, sc_ref, g_ref, wg_ref, wu_ref, wd_ref = refs[n_in:n_in + 7]
    o_refs = refs[n_in + 7:n_in + 7 + n_out]
    h_ref, a_ref, wg_s, wu_s, wd_s = refs[n_in + 7 + n_out:]
    nj, tf = FFN_NJ, FFN_TF
    s = pl.program_id(0)
    tile = _ffn_tile(s)
    r = _mod_row(tile, FFN_TM, 0)

    def load_tile():
        def init(x_ref, _):
            h = _norm_mod(x_ref[...], n_ref[layer:layer + 1, :], sc_ref[pl.ds(r, 1), :], sh_ref[pl.ds(r, 1), :])
            h_ref[...] = h.astype(BF16)
        _on_stream_part(tile, x_refs, o_refs, init)

    def up_chunk(j, cols):
        h = h_ref[...]
        a_ref[:, cols] = (_silu(_dot(h, wg_s[j])) * _dot(h, wu_s[j])).astype(BF16)

    def down_and_store():
        y = (0.5 * g_ref[pl.ds(r, 1), :]) * _dot(a_ref[...], wd_s[...])

        def store(x_ref, o_ref):
            o_ref[...] = x_ref[...] + y
        _on_stream_part(tile, x_refs, o_refs, store)

    def keep_arrived_chunk():
        wg_s[s] = wg_ref[...].astype(BF16)
        wu_s[s] = wu_ref[...].astype(BF16)
        wd_s[pl.ds(pl.multiple_of(s * tf, tf), tf), :] = wd_ref[...].astype(BF16)

    def up_previous_chunk():
        up_chunk(s - 1, pl.ds(pl.multiple_of((s - 1) * tf, tf), tf))

    @pl.when(s == 0)
    def _():
        load_tile()
        keep_arrived_chunk()

    @pl.when((s > 0) & (s < nj))
    def _():
        up_previous_chunk()
        keep_arrived_chunk()

    @pl.when(s == nj)
    def _():
        up_previous_chunk()
        down_and_store()

    @pl.when(s > nj)
    def _():
        load_tile()
        for j in range(nj):
            up_chunk(j, slice(j * tf, (j + 1) * tf))
        down_and_store()


def _stream_specs(split, buffered_once):
    tm = FFN_TM
    kw = {"pipeline_mode": pl.Buffered(1)} if buffered_once else {}
    if not split:
        return [pl.BlockSpec((tm, D_MODEL), lambda s: (_ffn_tile(s), 0), **kw)]
    last_ctx = N_CTX_TILES - 1
    return [pl.BlockSpec((tm, D_MODEL), lambda s: (jnp.minimum(_ffn_tile(s), last_ctx), 0), **kw),
            pl.BlockSpec((tm, D_MODEL), lambda s: (jnp.maximum(_ffn_tile(s) - N_CTX_TILES, 0), 0), **kw)]


def _ffn(xs, mod, layer, chunk0, norm_g, wg, wu, wd, split_out=False):
    tm, tf, nj = FFN_TM, FFN_TF, FFN_NJ
    split_in = len(xs) == 2
    mod_spec = lambda c: pl.BlockSpec((None, MOD_ROWS, D_MODEL), lambda s: (layer, 0, c))
    w_col = lambda s: (layer, 0, jnp.minimum(s, nj - 1))
    w_row = lambda s: (layer, jnp.minimum(s, nj - 1), 0)
    if split_out:
        out_shape = [jax.ShapeDtypeStruct((N_CTX_ROWS, D_MODEL), F32),
                     jax.ShapeDtypeStruct((N_LAT_ROWS, D_MODEL), F32)]
    else:
        out_shape = [jax.ShapeDtypeStruct((N_ROWS, D_MODEL), F32)]
    out = pl.pallas_call(
        functools.partial(_ffn_kernel, layer=layer, n_in=len(xs), n_out=len(out_shape)),
        grid=(N_FFN_STEPS,),
        in_specs=_stream_specs(split_in, False) + [
            _full((DEPTH, D_MODEL)),
            mod_spec(chunk0), mod_spec(chunk0 + 1), mod_spec(chunk0 + 2),
            pl.BlockSpec((None, D_MODEL, tf), w_col),
            pl.BlockSpec((None, D_MODEL, tf), w_col),
            pl.BlockSpec((None, tf, D_MODEL), w_row),
        ],
        out_specs=_stream_specs(split_out, True),
        out_shape=out_shape,
        scratch_shapes=[pltpu.VMEM((tm, D_MODEL), BF16),
                        pltpu.VMEM((tm, D_FF), BF16),
                        pltpu.VMEM((nj, D_MODEL, tf), BF16),
                        pltpu.VMEM((nj, D_MODEL, tf), BF16),
                        pltpu.VMEM((D_FF, D_MODEL), BF16)],
        compiler_params=_cparams("arbitrary"),
        name="ffn",
    )(*xs, norm_g, mod, mod, mod, wg, wu, wd)
    return tuple(out)


def _once(shape, index_map):
    return pl.BlockSpec(shape, index_map, pipeline_mode=pl.Buffered(1))


def _mod_chunk(layer, c):
    return pl.BlockSpec((None, MOD_ROWS, D_MODEL), lambda *_: (layer, 0, c))


def _head_mean_square(x):
    n = x.shape[-1]
    x2 = x * x
    hi = x2.astype(BF16)
    lo = (x2 - hi.astype(F32)).astype(BF16)
    r = lax.broadcasted_iota(jnp.int32, (n, n), 0) >> HEAD_SHIFT
    c = lax.broadcasted_iota(jnp.int32, (n, n), 1) >> HEAD_SHIFT
    ones_bd = jnp.where(r == c, 1.0, 0.0).astype(BF16)
    return (_dot(hi, ones_bd) + _dot(lo, ones_bd)) * (1.0 / HEAD_DIM)


def _head_norm(x, head_gain):
    gain_row = jnp.concatenate([head_gain] * (x.shape[-1] // HEAD_DIM), axis=-1)
    return x * lax.rsqrt(_head_mean_square(x) + EPS) * gain_row


def _head_cols(x, h):
    return x[:, h * HEAD_DIM:(h + 1) * HEAD_DIM].astype(BF16)


def _softmax_pv(scores, values, sink):
    m = jnp.max(scores[0], axis=-1, keepdims=True)
    for s in scores[1:]:
        m = jnp.maximum(m, jnp.max(s, axis=-1, keepdims=True))
    if sink is not None:
        m = jnp.maximum(m, sink)
    denom = None
    acc = None
    for s, v in zip(scores, values):
        p = jnp.exp(s - m)
        d = jnp.sum(p, axis=-1, keepdims=True)
        o = _dot(p.astype(BF16), v)
        denom = d if denom is None else denom + d
        acc = o if acc is None else acc + o
    if sink is not None:
        denom = denom + jnp.exp(sink - m)
    return acc / denom


def _rope(x, cos, sin_lo, sin_hi):
    cols = []
    for c in range(x.shape[-1] // 128):
        xc = x[:, c * 128:(c + 1) * 128]
        cols.append(xc * cos + pltpu.roll(xc, 112, 1) * sin_lo + pltpu.roll(xc, 16, 1) * sin_hi)
    return cols[0] if len(cols) == 1 else jnp.concatenate(cols, axis=-1)


def _block_diag(blocks):
    n = len(blocks)
    w = blocks[0].shape[0]
    rows = []
    for k, blk in enumerate(blocks):
        parts = []
        if k > 0:
            parts.append(jnp.zeros((w, k * w), F32))
        parts.append(blk)
        if k < n - 1:
            parts.append(jnp.zeros((w, (n - 1 - k) * w), F32))
        rows.append(jnp.concatenate(parts, axis=-1))
    return jnp.concatenate(rows, axis=0)


def _rglru_gates(xc, wa, ba, wx, bx, lam):
    xb = xc.astype(BF16)
    r = _sigmoid(_dot(xb, wa) + ba)
    i = _sigmoid(_dot(xb, wx) + bx)
    softplus = jnp.maximum(-lam, 0.0) + jnp.log1p(jnp.exp(-jnp.abs(lam)))
    log_a = (-LRU_C) * r * softplus
    a = jnp.exp(log_a)
    b = jnp.sqrt(1.0 - a * a) * (i * xc)
    return a, b


def _block_prefix(a, b, reverse):
    t = a.shape[0]
    row = lax.broadcasted_iota(jnp.int32, a.shape, 0) & 7
    for d in (1, 2, 4):
        if reverse:
            a_s = pltpu.roll(a, t - d, 0)
            b_s = pltpu.roll(b, t - d, 0)
            ok = row < 8 - d
        else:
            a_s = pltpu.roll(a, d, 0)
            b_s = pltpu.roll(b, d, 0)
            ok = row >= d
        b = jnp.where(ok, a * b_s + b, b)
        a = jnp.where(ok, a * a_s, a)
    return a, b


def _conv4(x, w_ref, b_row):
    t = x.shape[0]
    row = lax.broadcasted_iota(jnp.int32, x.shape, 0)
    xm2 = jnp.where(row >= 2, pltpu.roll(x, 2, 0), 0.0)
    xm1 = jnp.where(row >= 1, pltpu.roll(x, 1, 0), 0.0)
    xp1 = jnp.where(row < t - 1, pltpu.roll(x, t - 1, 0), 0.0)
    return (xm2 * w_ref[0:1, :] + xm1 * w_ref[1:2, :] + x * w_ref[2:3, :] + xp1 * w_ref[3:4, :]) + b_row


def _rglru_prepare(cx, cy, conv_w_ref, conv_b, gate_w_ref, ba_ref, bx_ref, lam_ref,
                   af_ref, bf_ref, ab_ref, bb_ref, gel_ref):
    xc = _conv4(cx, conv_w_ref, conv_b)
    a, b = _rglru_gates(xc, gate_w_ref[0], ba_ref[0:1, :], gate_w_ref[1], bx_ref[0:1, :], lam_ref[0:1, :])
    a, b = _block_prefix(a, b, reverse=False)
    af_ref[...] = a
    bf_ref[...] = b
    a, b = _rglru_gates(xc, gate_w_ref[2], ba_ref[1:2, :], gate_w_ref[3], bx_ref[1:2, :], lam_ref[1:2, :])
    a, b = _block_prefix(a, b, reverse=True)
    ab_ref[...] = a
    bb_ref[...] = b
    gel_ref[...] = _gelu_tanh(cy)


def _rglru_finish(h0f, h0b, af_ref, bf_ref, ab_ref, bb_ref, hf_ref, hb_ref, gel_ref):
    nblk = af_ref.shape[0] // 8

    def body(k, carry):
        cf, cb = carry
        rf = pl.ds(pl.multiple_of(k * 8, 8), 8)
        hf = bf_ref[rf, :] + af_ref[rf, :] * cf
        hf_ref[rf, :] = hf
        rb = pl.ds(pl.multiple_of((nblk - 1 - k) * 8, 8), 8)
        hb = bb_ref[rb, :] + ab_ref[rb, :] * cb
        hb_ref[rb, :] = hb
        return hf[7:8, :], hb[0:1, :]

    cf, cb = lax.fori_loop(0, nblk, body, (h0f, h0b))
    oc = (hf_ref[...] + hb_ref[...]) * gel_ref[...]
    return oc, cf, cb


def _store_gate_weights(gate_w_ref, wa_ref, wx_ref):
    for d in range(2):
        gate_w_ref[2 * d] = _block_diag([wa_ref[d, n] for n in range(N_HEADS)]).astype(BF16)
        gate_w_ref[2 * d + 1] = _block_diag([wx_ref[d, n] for n in range(N_HEADS)]).astype(BF16)


def _lane_head_masks(n):
    lane = lax.broadcasted_iota(jnp.int32, (1, n), 1) >> HEAD_SHIFT
    return [jnp.where(lane == h, 1.0, 0.0) for h in range(n // HEAD_DIM)]


def _log_decays(theta_ref, masks):
    theta = theta_ref[...]
    lanes = theta[:, 0:1] * masks[0]
    for h in range(1, N_HEADS):
        lanes = lanes + theta[:, h:h + 1] * masks[h]
    lg = jnp.log1p(-jnp.exp(lanes))
    return lg[0:1, :], lg[1:2, :]


RET_BLOCK = 256


def _retention(q, k8, vb, s0, lgf, lgb, masks, o_ref):
    t, w = q.shape
    c = RET_BLOCK
    nh = w // HEAD_DIM
    pos = lax.broadcasted_iota(jnp.int32, (c, w), 0).astype(F32)
    q_dec = (jnp.exp(lgf * (pos + 1.0)), jnp.exp(lgb * (float(c) - pos)))
    k_dec = (jnp.exp(lgf * (float(c - 1) - pos)), jnp.exp(lgb * pos))
    chunk_dec = (jnp.exp(lgf * float(c)), jnp.exp(lgb * float(c)))
    rel = (lax.broadcasted_iota(jnp.int32, (c, c), 0) - lax.broadcasted_iota(jnp.int32, (c, c), 1)).astype(F32)
    decs = []
    for h in range(nh):
        gf = lgf[:, h * HEAD_DIM:h * HEAD_DIM + 1]
        gb = lgb[:, h * HEAD_DIM:h * HEAD_DIM + 1]
        e = jnp.exp(jnp.where(rel >= 0, gf * rel, gb * (-rel)))
        decs.append(jnp.where(rel == 0, 2.0, e))
    dec = jnp.concatenate(decs, axis=0)
    r_head = lax.broadcasted_iota(jnp.int32, (w, w), 0) >> HEAD_SHIFT
    c_head = lax.broadcasted_iota(jnp.int32, (w, w), 1) >> HEAD_SHIFT
    same_head = jnp.where(r_head == c_head, 1.0, 0.0)
    states = [None, None] if s0 is None else list(s0)

    def carry(d, rows, o):
        if states[d] is not None:
            o = o + _dot((q[rows, :] * q_dec[d]).astype(BF16), states[d].astype(BF16))
        upd = _dot_tn((k8[rows, :] * k_dec[d]).astype(BF16), vb[rows, :]) * same_head
        states[d] = upd if states[d] is None else states[d] * chunk_dec[d] + upd
        return o

    for ci in range(t // c):
        rows = slice(ci * c, (ci + 1) * c)
        qc = q[rows, :]
        q_stack = jnp.concatenate([(qc * masks[h]).astype(BF16) for h in range(nh)], axis=0)
        inner = (_dot_nt(q_stack, k8[rows, :].astype(BF16)) * dec).astype(BF16)
        out = _dot(inner, vb[rows, :])
        o = out[0:c, :] * masks[0]
        for h in range(1, nh):
            o = o + out[h * c:(h + 1) * c, :] * masks[h]
        o_ref[rows, :] = carry(0, rows, o)
    for ci in reversed(range(t // c)):
        rows = slice(ci * c, (ci + 1) * c)
        if states[1] is not None:
            o_ref[rows, :] = carry(1, rows, o_ref[rows, :])
        else:
            carry(1, rows, None)
    return states[0], states[1]


def _ctx_mixer_kernel(*refs, layer, n_prev):
    prev_refs = refs[:n_prev]
    (x_ref, n2_ref, sh_ref, sc_ref, g2_ref, win_ref, wout_ref,
     aqn_ref, akn_ref, bqn_ref, bkn_ref, sink_ref,
     convw_ref, convb_ref, wa_ref, ba_ref, wx_ref, bx_ref, lam_ref, theta_ref, dn_ref,
     xn_ref, *state_refs) = refs[n_prev:n_prev + 28]
    (win_s, wout_s, u_ref, mixed_ref,
     gate_w_ref, af_ref, bf_ref, ab_ref, bb_ref, hf_ref, hb_ref, gel_ref, ret_ref) = refs[n_prev + 28:]
    t = SEQ
    lrow = slice(layer, layer + 1)
    for prev_ref, state_ref in zip(prev_refs, state_refs):
        for earlier in range(layer):
            state_ref[earlier] = prev_ref[earlier]
    ka_ref, va_ref, kb_ref, vb_ref, stc_ref, std_ref = (ref.at[layer] for ref in state_refs)

    @pl.when(pl.program_id(0) == 0)
    def _():
        for c in range(IN_WIDTH // 512):
            win_s[:, c * 512:(c + 1) * 512] = win_ref[:, c * 512:(c + 1) * 512].astype(BF16)
        wout_s[...] = wout_ref[...].astype(BF16)
        _store_gate_weights(gate_w_ref, wa_ref, wx_ref)

    x = x_ref[...]
    h = _norm_mod(x, n2_ref[lrow, :], sc_ref[0:1, :], sh_ref[0:1, :]).astype(BF16)
    u_ref[...] = _dot(h, win_s[...])

    for (cq, ck, cv, qn_ref, kn_ref, k_out, v_out, col0, use_sink) in (
            (COL_AQ, COL_AK, COL_AV, aqn_ref, akn_ref, ka_ref, va_ref, 0, True),
            (COL_BQ, COL_BK, COL_BV, bqn_ref, bkn_ref, kb_ref, vb_ref, GROUP_W, False)):
        q = _head_norm(u_ref[:, cq:cq + 256], qn_ref[lrow, :])
        k = _head_norm(u_ref[:, ck:ck + 128], kn_ref[lrow, :])
        v = u_ref[:, cv:cv + 128]
        k_out[...] = k
        v_out[...] = v
        qs = q * (HEAD_DIM ** -0.5)
        heads = []
        for hd in range(N_HEADS):
            kv = hd // 2
            s = _dot_nt(_head_cols(qs, hd), _head_cols(k, kv))
            sink = jnp.full((t, 1), sink_ref[layer, hd], F32) if use_sink else None
            heads.append(_softmax_pv([s], [_head_cols(v, kv)], sink))
        mixed_ref[:, col0:col0 + GROUP_W] = jnp.concatenate(heads, axis=-1).astype(BF16)

    _rglru_prepare(u_ref[:, COL_CX:COL_CX + GROUP_W], u_ref[:, COL_CY:COL_CY + GROUP_W],
                   convw_ref, convb_ref[lrow, :], gate_w_ref, ba_ref, bx_ref, lam_ref,
                   af_ref, bf_ref, ab_ref, bb_ref, gel_ref)
    zero = jnp.zeros((1, GROUP_W), F32)
    oc, cf, cb = _rglru_finish(zero, zero, af_ref, bf_ref, ab_ref, bb_ref, hf_ref, hb_ref, gel_ref)
    mixed_ref[:, 2 * GROUP_W:3 * GROUP_W] = oc.astype(BF16)
    stc_ref[0:1, :] = cf
    stc_ref[1:2, :] = cb

    masks = _lane_head_masks(GROUP_W)
    lgf, lgb = _log_decays(theta_ref, masks)
    k8 = u_ref[:, COL_DK:COL_DK + GROUP_W] * (HEAD_DIM ** -0.5)
    vb = u_ref[:, COL_DV:COL_DV + GROUP_W].astype(BF16)
    final_states = _retention(u_ref[:, COL_DQ:COL_DQ + GROUP_W], k8, vb, None, lgf, lgb, masks, ret_ref)
    o = ret_ref[...]
    o = o * lax.rsqrt(_head_mean_square(o) + EPS) * dn_ref[lrow, :] * _silu(u_ref[:, COL_DG:COL_DG + GROUP_W])
    mixed_ref[:, 3 * GROUP_W:4 * GROUP_W] = o.astype(BF16)
    for d, s_full in enumerate(final_states):
        for hd in range(N_HEADS):
            std_ref[d, hd] = s_full[hd * 64:(hd + 1) * 64, hd * 64:(hd + 1) * 64]

    xn_ref[...] = x + g2_ref[0:1, :] * _dot(mixed_ref[...], wout_s[...])


def _ctx_mixers(x, mod, layer, prev, norm2_g, w_in, w_out,
                a_qn, a_kn, a_sink, b_qn, b_kn, c_conv_w, c_conv_b, c_wa, c_ba, c_wx, c_bx,
                c_lambda, d_theta, d_norm_g):
    per_request = lambda slots, shape: pl.BlockSpec((None, slots) + shape, lambda b: (b,) + (0,) * (1 + len(shape)))
    state_dims = [(SEQ, 128)] * 4 + [(2, GROUP_W), (2, N_HEADS, HEAD_DIM, HEAD_DIM)]
    scr = pltpu.VMEM((SEQ, GROUP_W), F32)
    out = pl.pallas_call(
        functools.partial(_ctx_mixer_kernel, layer=layer, n_prev=len(prev)),
        grid=(BATCH,),
        in_specs=[per_request(layer, dims) for dims in state_dims[:len(prev)]] + [
            pl.BlockSpec((SEQ, D_MODEL), lambda b: (b, 0)),
            _full((DEPTH, D_MODEL)),
            _mod_chunk(layer, 3), _mod_chunk(layer, 4), _mod_chunk(layer, 5),
            _once((None, D_MODEL, IN_WIDTH), lambda b: (layer, 0, 0)),
            _once((None, D_MODEL, D_MODEL), lambda b: (layer, 0, 0)),
            _full((DEPTH, HEAD_DIM)), _full((DEPTH, HEAD_DIM)), _full((DEPTH, HEAD_DIM)), _full((DEPTH, HEAD_DIM)),
            pl.BlockSpec(memory_space=pltpu.SMEM),
            _layer_block((4, GROUP_W), layer), _full((DEPTH, GROUP_W)),
            _layer_block((2, N_HEADS, HEAD_DIM, HEAD_DIM), layer), _layer_block((2, GROUP_W), layer),
            _layer_block((2, N_HEADS, HEAD_DIM, HEAD_DIM), layer), _layer_block((2, GROUP_W), layer),
            _layer_block((2, GROUP_W), layer),
            _layer_block((2, N_HEADS), layer), _full((DEPTH, GROUP_W)),
        ],
        out_specs=[pl.BlockSpec((SEQ, D_MODEL), lambda b: (b, 0))] + [
            per_request(layer + 1, dims) for dims in state_dims],
        out_shape=[jax.ShapeDtypeStruct((N_ROWS, D_MODEL), F32)] + [
            jax.ShapeDtypeStruct((BATCH, layer + 1) + dims, F32) for dims in state_dims],
        input_output_aliases={len(prev): 0},
        scratch_shapes=[pltpu.VMEM((D_MODEL, IN_WIDTH), BF16), pltpu.VMEM((D_MODEL, D_MODEL), BF16),
                        pltpu.VMEM((SEQ, IN_WIDTH), F32), pltpu.VMEM((SEQ, D_MODEL), BF16),
                        pltpu.VMEM((4, GROUP_W, GROUP_W), BF16)] + [scr] * 8,
        compiler_params=_cparams("arbitrary"),
        name="ctx_mixers",
    )(*prev, x, norm2_g, mod, mod, mod, w_in, w_out,
      a_qn, a_kn, b_qn, b_kn, a_sink, c_conv_w, c_conv_b, c_wa, c_ba, c_wx, c_bx,
      c_lambda, d_theta, d_norm_g)
    return out[0], tuple(out[1:])


LAT_BLOCK0 = N_CTX_ROWS // DEC_SEQ


def _lat_attn_kernel(x_ref, n2_ref, sh_ref, sc_ref, g2_ref, win_ref, wout_ref,
                     kca_ref, vca_ref, kcb_ref, vcb_ref,
                     aqn_ref, akn_ref, bqn_ref, bkn_ref, sink_ref, cos_ref, sinl_ref, sinh_ref,
                     xn_ref, h_ref, u_ref, o_ref, *, layer):
    t = DEC_SEQ
    lrow = slice(layer, layer + 1)
    mrow = pl.ds(1 + pl.program_id(0), 1)
    cos, sin_lo, sin_hi = cos_ref[...], sinl_ref[...], sinh_ref[...]
    scale = HEAD_DIM ** -0.5
    x = x_ref[...]
    h_ref[...] = _norm_mod(x, n2_ref[lrow, :], sc_ref[mrow, :], sh_ref[mrow, :]).astype(BF16)
    u_ref[...] = _dot(h_ref[...], win_ref[...].astype(BF16))

    q = _rope(_head_norm(u_ref[:, COL_AQ:COL_AQ + 256], aqn_ref[lrow, :]), cos, sin_lo, sin_hi)
    k = _rope(_head_norm(u_ref[:, COL_AK:COL_AK + 128], akn_ref[lrow, :]), cos, sin_lo, sin_hi)
    qh = [_head_cols(q * scale, h) for h in range(4)]
    v = u_ref[:, COL_AV:COL_AV + 128]
    kh = [_head_cols(k, kv) for kv in range(2)]
    vh = [_head_cols(v, kv) for kv in range(2)]
    kch = [_head_cols(kca_ref[...], kv) for kv in range(2)]
    vch = [_head_cols(vca_ref[...], kv) for kv in range(2)]
    w = ATT_BLOCK
    span = 3 * w
    for n in range(t // w):
        start = min(max((n - 1) * w, 0), t - span)
        rows = slice(n * w, (n + 1) * w)
        band = slice(start, start + span)
        qpos = (lax.broadcasted_iota(jnp.int32, (2 * w, span), 0) & (w - 1)) + n * w
        kpos = lax.broadcasted_iota(jnp.int32, (2 * w, span), 1) + start
        valid = jnp.abs(qpos - kpos) <= WINDOW
        heads = []
        for kv in range(2):
            qp = jnp.concatenate([qh[2 * kv][rows, :], qh[2 * kv + 1][rows, :]], axis=0)
            s_ctx = _dot_nt(qp, kch[kv])
            s_band = jnp.where(valid, _dot_nt(qp, kh[kv][band, :]), NEG_INF)
            row = lax.broadcasted_iota(jnp.int32, (2 * w, 1), 0)
            sink = jnp.where(row < w, sink_ref[layer, 2 * kv], sink_ref[layer, 2 * kv + 1])
            o = _softmax_pv([s_ctx, s_band], [vch[kv], vh[kv][band, :]], sink)
            heads += [o[0:w, :], o[w:2 * w, :]]
        o_ref[rows, 0:GROUP_W] = jnp.concatenate(heads, axis=-1).astype(BF16)

    q = _rope(_head_norm(u_ref[:, COL_BQ:COL_BQ + 256], bqn_ref[lrow, :]), cos, sin_lo, sin_hi)
    k = _rope(_head_norm(u_ref[:, COL_BK:COL_BK + 128], bkn_ref[lrow, :]), cos, sin_lo, sin_hi)
    qh = [_head_cols(q * scale, h) for h in range(4)]
    v = u_ref[:, COL_BV:COL_BV + 128]
    kh = [_head_cols(k, kv) for kv in range(2)]
    vh = [_head_cols(v, kv) for kv in range(2)]
    kch = [_head_cols(kcb_ref[...], kv) for kv in range(2)]
    vch = [_head_cols(vcb_ref[...], kv) for kv in range(2)]
    tq = 256
    for n in range(t // tq):
        rows = slice(n * tq, (n + 1) * tq)
        heads = []
        for kv in range(2):
            qp = jnp.concatenate([qh[2 * kv][rows, :], qh[2 * kv + 1][rows, :]], axis=0)
            o = _softmax_pv([_dot_nt(qp, kch[kv]), _dot_nt(qp, kh[kv])], [vch[kv], vh[kv]], None)
            heads += [o[0:tq, :], o[tq:2 * tq, :]]
        o_ref[rows, GROUP_W:2 * GROUP_W] = jnp.concatenate(heads, axis=-1).astype(BF16)

    xn_ref[...] = x + g2_ref[mrow, :] * _dot(o_ref[...], wout_ref[...].astype(BF16))


def _lat_recurrent_kernel(xn_in_ref, h_ref, g2_ref, wc_ref, wqk_ref, wvg_ref, wout_ref, h0_ref,
                          convw_ref, convb_ref, wa_ref, ba_ref, wx_ref, bx_ref, lam_ref,
                          s0_ref, theta_ref, dn_ref,
                          xn_ref, gate_w_ref, af_ref, bf_ref, ab_ref, bb_ref, hf_ref, hb_ref, gel_ref, ret_ref,
                          *, layer):
    lrow = slice(layer, layer + 1)
    mrow = pl.ds(1 + pl.program_id(0), 1)

    @pl.when(pl.program_id(0) == 0)
    def _():
        _store_gate_weights(gate_w_ref, wa_ref, wx_ref)

    h = h_ref[...]
    u = _dot(h, wc_ref[...].astype(BF16))
    _rglru_prepare(u[:, 0:GROUP_W], u[:, GROUP_W:2 * GROUP_W], convw_ref, convb_ref[lrow, :],
                   gate_w_ref, ba_ref, bx_ref, lam_ref, af_ref, bf_ref, ab_ref, bb_ref, gel_ref)
    oc, _, _ = _rglru_finish(h0_ref[0:1, :], h0_ref[1:2, :],
                             af_ref, bf_ref, ab_ref, bb_ref, hf_ref, hb_ref, gel_ref)
    y = _dot(oc.astype(BF16), wout_ref[0:GROUP_W, :].astype(BF16))

    uqk = _dot(h, wqk_ref[...].astype(BF16))
    uvg = _dot(h, wvg_ref[...].astype(BF16))
    masks = _lane_head_masks(GROUP_W)
    lgf, lgb = _log_decays(theta_ref, masks)
    s0 = tuple(_block_diag([s0_ref[d, hd] for hd in range(N_HEADS)]) for d in range(2))
    _retention(uqk[:, 0:GROUP_W], uqk[:, GROUP_W:2 * GROUP_W] * (HEAD_DIM ** -0.5),
               uvg[:, 0:GROUP_W].astype(BF16), s0, lgf, lgb, masks, ret_ref)
    o = ret_ref[...]
    o = o * lax.rsqrt(_head_mean_square(o) + EPS) * dn_ref[lrow, :] * _silu(uvg[:, GROUP_W:2 * GROUP_W])
    y = y + _dot(o.astype(BF16), wout_ref[GROUP_W:2 * GROUP_W, :].astype(BF16))
    xn_ref[...] = xn_in_ref[...] + g2_ref[mrow, :] * y


def _lat_mixers(x, mod, layer, caches, state_c, state_d, rope, norm2_g, w_in, w_out,
                a_qn, a_kn, a_sink, b_qn, b_kn, c_conv_w, c_conv_b, c_wa, c_ba, c_wx, c_bx,
                c_lambda, d_theta, d_norm_g):
    rows = pl.BlockSpec((DEC_SEQ, D_MODEL), lambda b: (LAT_BLOCK0 + b, 0))
    h_rows = pl.BlockSpec((DEC_SEQ, D_MODEL), lambda b: (b, 0))
    cache_spec = pl.BlockSpec((None, None, PAST_LEN, 128), lambda b: (b, layer, 0, 0))
    gain = _full((DEPTH, HEAD_DIM))
    table = _once((DEC_SEQ, 128), lambda b: (0, 0))
    out_shape = jax.ShapeDtypeStruct((N_ROWS, D_MODEL), F32)
    win_cols = lambda w, c: _once((None, D_MODEL, w), lambda b: (layer, 0, c))
    wout_rows = lambda h, r: _once((None, h, D_MODEL), lambda b: (layer, r, 0))

    xn, h = pl.pallas_call(
        functools.partial(_lat_attn_kernel, layer=layer),
        grid=(DEC_BATCH,),
        in_specs=[rows, _full((DEPTH, D_MODEL)),
                  _mod_chunk(layer, 3), _mod_chunk(layer, 4), _mod_chunk(layer, 5),
                  win_cols(4 * GROUP_W, 0), wout_rows(2 * GROUP_W, 0),
                  cache_spec, cache_spec, cache_spec, cache_spec,
                  gain, gain, gain, gain,
                  pl.BlockSpec(memory_space=pltpu.SMEM),
                  table, table, table],
        out_specs=[rows, pl.BlockSpec((DEC_SEQ, D_MODEL), lambda b: (b, 0), pipeline_mode=pl.Buffered(1))],
        out_shape=[out_shape, jax.ShapeDtypeStruct((N_LAT_ROWS, D_MODEL), BF16)],
        input_output_aliases={0: 0},
        scratch_shapes=[pltpu.VMEM((DEC_SEQ, 4 * GROUP_W), F32), pltpu.VMEM((DEC_SEQ, 2 * GROUP_W), BF16)],
        compiler_params=_cparams("arbitrary"),
        name="lat_attention",
    )(x, norm2_g, mod, mod, mod, w_in, w_out, *caches, a_qn, a_kn, b_qn, b_kn, a_sink, *rope)

    scr = pltpu.VMEM((DEC_SEQ, GROUP_W), F32)
    xn = pl.pallas_call(
        functools.partial(_lat_recurrent_kernel, layer=layer),
        grid=(DEC_BATCH,),
        in_specs=[
            rows, h_rows, _mod_chunk(layer, 5),
            win_cols(2 * GROUP_W, COL_CX // (2 * GROUP_W)),
            win_cols(2 * GROUP_W, COL_DQ // (2 * GROUP_W)), win_cols(2 * GROUP_W, COL_DV // (2 * GROUP_W)),
            wout_rows(2 * GROUP_W, 1),
            pl.BlockSpec((None, None, 2, GROUP_W), lambda b: (b, layer, 0, 0)),
            _layer_block((4, GROUP_W), layer), _full((DEPTH, GROUP_W)),
            _layer_block((2, N_HEADS, HEAD_DIM, HEAD_DIM), layer), _layer_block((2, GROUP_W), layer),
            _layer_block((2, N_HEADS, HEAD_DIM, HEAD_DIM), layer), _layer_block((2, GROUP_W), layer),
            _layer_block((2, GROUP_W), layer),
            pl.BlockSpec((None, None, 2, N_HEADS, HEAD_DIM, HEAD_DIM), lambda b: (b, layer, 0, 0, 0, 0)),
            _layer_block((2, N_HEADS), layer), _full((DEPTH, GROUP_W))],
        out_specs=rows,
        out_shape=out_shape,
        input_output_aliases={0: 0},
        scratch_shapes=[pltpu.VMEM((4, GROUP_W, GROUP_W), BF16)] + [scr] * 8,
        compiler_params=_cparams("arbitrary"),
        name="lat_recurrent",
    )(xn, h, mod, w_in, w_in, w_in, w_out, state_c, c_conv_w, c_conv_b, c_wa, c_ba, c_wx, c_bx, c_lambda,
      state_d, d_theta, d_norm_g)
    return xn


def _rope_tables():
    t = np.arange(DEC_SEQ)
    row = (t // GRID_W).astype(np.float64)[:, None]
    col = (t % GRID_W).astype(np.float64)[:, None]
    half = HEAD_DIM // 2
    inv = 1.0 / (ROPE_BASE ** (np.arange(0, half, 2, dtype=np.float64) / half))
    j = np.arange(128) % HEAD_DIM
    ang = np.where((j < half)[None, :], row, col) * inv[j % (half // 2)][None, :]
    first = ((j % half) < half // 2)[None, :]
    cos, sin = np.cos(ang), np.sin(ang)
    return tuple(jnp.asarray(a, F32) for a in (cos, np.where(first, -sin, 0.0), np.where(first, 0.0, sin)))


def kernel(x_prompt, x_sample, cache_a_k, cache_a_v, cache_b_k, cache_b_v, state_c, state_d, c, c_ctx, norm1_g, norm2_g, norm3_g, w_mod, b_mod, ffn1_wg, ffn1_wu, ffn1_wd, ffn2_wg, ffn2_wu, ffn2_wd, w_in, w_out, a_qn, a_kn, a_sink, b_qn, b_kn, c_conv_w, c_conv_b, c_wa, c_ba, c_wx, c_bx, c_lambda, d_theta, d_norm_g):
    mod = _modulation(c_ctx, c, w_mod, b_mod)
    rope = _rope_tables()
    caches = tuple(t.reshape(DEC_BATCH, DEPTH, PAST_LEN, 128) for t in (cache_a_k, cache_a_v, cache_b_k, cache_b_v))
    mixer_params = (a_qn, a_kn, a_sink, b_qn, b_kn, c_conv_w, c_conv_b, c_wa, c_ba, c_wx, c_bx,
                    c_lambda, d_theta, d_norm_g)
    xs = (x_prompt.reshape(N_CTX_ROWS, D_MODEL), x_sample.reshape(N_LAT_ROWS, D_MODEL))
    states = ()
    for l in range(DEPTH):
        (x,) = _ffn(xs, mod, l, 0, norm1_g, ffn1_wg, ffn1_wu, ffn1_wd)
        x, states = _ctx_mixers(x, mod, l, states, norm2_g, w_in, w_out, *mixer_params)
        x = _lat_mixers(x, mod, l, caches, state_c, state_d, rope, norm2_g, w_in, w_out, *mixer_params)
        xs = _ffn((x,), mod, l, 6, norm3_g, ffn2_wg, ffn2_wu, ffn2_wd, split_out=(l == DEPTH - 1))
    y_p, y_s = xs
    ka, va, kb, vb, st_c, st_d = states
    kv_shape = (BATCH, DEPTH, SEQ, 2, HEAD_DIM)
    return (y_p.reshape(BATCH, SEQ, D_MODEL), y_s.reshape(DEC_BATCH, DEC_SEQ, D_MODEL),
            ka.reshape(kv_shape), va.reshape(kv_shape), kb.reshape(kv_shape), vb.reshape(kv_shape),
            st_c, st_d)
```

```python
import functools
import math

import numpy as np
import jax
import jax.numpy as jnp
from jax import lax
from jax.experimental import pallas as pl
from jax.experimental.pallas import tpu as pltpu

F32 = jnp.float32
BF16 = jnp.bfloat16

D_MODEL = 1024
BATCH = 16
SEQ = 256
DEPTH = 2
DEC_BATCH = 2
DEC_SEQ = 1024
PAST_LEN = 512
GRID_W = 64
HEAD_DIM = 64
HEAD_SHIFT = 6
N_HEADS = 4
GROUP_W = 256
WINDOW = 128
ATT_BLOCK = 128
ROPE_BASE = 10000.0
LRU_C = 8.0
D_FF = 2816
N_MOD = 9
EPS = 1e-6
NEG_INF = -1e30
IN_WIDTH = 2560

N_CTX_ROWS = BATCH * SEQ
N_LAT_ROWS = DEC_BATCH * DEC_SEQ
N_ROWS = N_CTX_ROWS + N_LAT_ROWS
MOD_ROWS = 8
MOD_GROUP = 1024

VMEM_LIMIT_BYTES = 56 * 1024 * 1024

COL_AQ, COL_AK, COL_AV = 0, 256, 384
COL_BQ, COL_BK, COL_BV = 512, 768, 896
COL_CX, COL_CY = 1024, 1280
COL_DQ, COL_DK, COL_DV, COL_DG = 1536, 1792, 2048, 2304


def _cparams(*sem):
    return pltpu.CompilerParams(dimension_semantics=sem, vmem_limit_bytes=VMEM_LIMIT_BYTES)


def _dot(a, b):
    return jnp.dot(a, b, preferred_element_type=F32)


def _dot_nt(a, b):
    return lax.dot_general(a, b, (((1,), (1,)), ((), ())), preferred_element_type=F32)


def _dot_tn(a, b):
    return lax.dot_general(a, b, (((0,), (0,)), ((), ())), preferred_element_type=F32)


def _sigmoid(x):
    return 0.5 * jnp.tanh(0.5 * x) + 0.5


def _silu(x):
    return x * _sigmoid(x)


def _gelu_tanh(x):
    return 0.5 * x * (1.0 + jnp.tanh(math.sqrt(2.0 / math.pi) * (x + 0.044715 * (x * x * x))))


def _mod_row(i, tm, s):
    if tm >= MOD_GROUP:
        block_index = i * (tm // MOD_GROUP) + s
    else:
        block_index = i >> int(math.log2(MOD_GROUP // tm))
    return jnp.maximum(block_index - (N_CTX_ROWS // MOD_GROUP - 1), 0)


def _norm_mod(x, g, sc, sh):
    ms = jnp.mean(x * x, axis=-1, keepdims=True)
    return (x * lax.rsqrt(ms + EPS) * g) * (1.0 + sc) + sh


def _full(shape):
    return pl.BlockSpec(shape, lambda *_: (0,) * len(shape))


def _layer_block(shape, layer):
    return pl.BlockSpec((None,) + shape, lambda *_: (layer,) + (0,) * len(shape))


MOD_TN = 3072


def _mod_kernel(cc_ref, c_ref, w_ref, b_ref, o_ref):
    l = pl.program_id(0)
    pad = jnp.zeros((MOD_ROWS - 1 - DEC_BATCH, D_MODEL), F32)
    cond = jnp.concatenate([cc_ref[...], c_ref[...], pad], axis=0)
    o_ref[...] = _dot(_silu(cond).astype(BF16), w_ref[...].astype(BF16)) + b_ref[pl.ds(l, 1), :]


def _modulation(c_ctx, c, w_mod, b_mod):
    n = N_MOD * D_MODEL
    return pl.pallas_call(
        _mod_kernel,
        grid=(DEPTH, n // MOD_TN),
        in_specs=[
            pl.BlockSpec((1, D_MODEL), lambda l, j: (0, 0)),
            pl.BlockSpec((DEC_BATCH, D_MODEL), lambda l, j: (0, 0)),
            pl.BlockSpec((None, D_MODEL, MOD_TN), lambda l, j: (l, 0, j)),
            pl.BlockSpec((DEPTH, MOD_TN), lambda l, j: (0, j)),
        ],
        out_specs=pl.BlockSpec((None, MOD_ROWS, MOD_TN), lambda l, j: (l, 0, j)),
        out_shape=jax.ShapeDtypeStruct((DEPTH, MOD_ROWS, n), F32),
        compiler_params=_cparams("arbitrary", "arbitrary"),
        name="modulation",
    )(c_ctx.reshape(1, D_MODEL), c, w_mod, b_mod)


FFN_TM = 1024
FFN_TF = 256
N_CTX_TILES = N_CTX_ROWS // FFN_TM


FFN_NJ = D_FF // FFN_TF
N_FFN_TILES = N_ROWS // FFN_TM
N_FFN_STEPS = FFN_NJ + N_FFN_TILES


def _ffn_tile(step):
    return jnp.maximum(step - FFN_NJ, 0)


def _on_stream_part(tile, x_refs, o_refs, fn):
    if len(x_refs) == 1 and len(o_refs) == 1:
        fn(x_refs[0], o_refs[0])
    else:
        pl.when(tile < N_CTX_TILES)(lambda: fn(x_refs[0], o_refs[0]))
        pl.when(tile >= N_CTX_TILES)(lambda: fn(x_refs[-1], o_refs[-1]))


def _ffn_kernel(*refs, layer, n_in, n_out):
    x_refs = refs[:n_in]
    n_ref, sh_ref, sc_ref, g_ref, wg_ref, wu_ref, wd_ref = refs[n_in:n_in + 7]
    o_refs = refs[n_in + 7:n_in + 7 + n_out]
    h_ref, a_ref, wg_s, wu_s, wd_s = refs[n_in + 7 + n_out:]
    nj, tf = FFN_NJ, FFN_TF
    s = pl.program_id(0)
    tile = _ffn_tile(s)
    r = _mod_row(tile, FFN_TM, 0)

    def load_tile():
        def init(x_ref, _):
            h = _norm_mod(x_ref[...], n_ref[layer:layer + 1, :], sc_ref[pl.ds(r, 1), :], sh_ref[pl.ds(r, 1), :])
            h_ref[...] = h.astype(BF16)
        _on_stream_part(tile, x_refs, o_refs, init)

    def up_chunk(j, cols):
        h = h_ref[...]
        a_ref[:, cols] = (_silu(_dot(h, wg_s[j])) * _dot(h, wu_s[j])).astype(BF16)

    def down_and_store():
        y = (0.5 * g_ref[pl.ds(r, 1), :]) * _dot(a_ref[...], wd_s[...])

        def store(x_ref, o_ref):
            o_ref[...] = x_ref[...] + y
        _on_stream_part(tile, x_refs, o_refs, store)

    def keep_arrived_chunk():
        wg_s[s] = wg_ref[...].astype(BF16)
        wu_s[s] = wu_ref[...].astype(BF16)
        wd_s[pl.ds(pl.multiple_of(s * tf, tf), tf), :] = wd_ref[...].astype(BF16)

    def up_previous_chunk():
        up_chunk(s - 1, pl.ds(pl.multiple_of((s - 1) * tf, tf), tf))

    @pl.when(s == 0)
    def _():
        load_tile()
        keep_arrived_chunk()

    @pl.when((s > 0) & (s < nj))
    def _():
        up_previous_chunk()
        keep_arrived_chunk()

    @pl.when(s == nj)
    def _():
        up_previous_chunk()
        down_and_store()

    @pl.when(s > nj)
    def _():
        load_tile()
        for j in range(nj):
            up_chunk(j, slice(j * tf, (j + 1) * tf))
        down_and_store()


def _stream_specs(split, buffered_once):
    tm = FFN_TM
    kw = {"pipeline_mode": pl.Buffered(1)} if buffered_once else {}
    if not split:
        return [pl.BlockSpec((tm, D_MODEL), lambda s: (_ffn_tile(s), 0), **kw)]
    last_ctx = N_CTX_TILES - 1
    return [pl.BlockSpec((tm, D_MODEL), lambda s: (jnp.minimum(_ffn_tile(s), last_ctx), 0), **kw),
            pl.BlockSpec((tm, D_MODEL), lambda s: (jnp.maximum(_ffn_tile(s) - N_CTX_TILES, 0), 0), **kw)]


def _ffn(xs, mod, layer, chunk0, norm_g, wg, wu, wd, split_out=False):
    tm, tf, nj = FFN_TM, FFN_TF, FFN_NJ
    split_in = len(xs) == 2
    mod_spec = lambda c: pl.BlockSpec((None, MOD_ROWS, D_MODEL), lambda s: (layer, 0, c))
    w_col = lambda s: (layer, 0, jnp.minimum(s, nj - 1))
    w_row = lambda s: (layer, jnp.minimum(s, nj - 1), 0)
    if split_out:
        out_shape = [jax.ShapeDtypeStruct((N_CTX_ROWS, D_MODEL), F32),
                     jax.ShapeDtypeStruct((N_LAT_ROWS, D_MODEL), F32)]
    else:
        out_shape = [jax.ShapeDtypeStruct((N_ROWS, D_MODEL), F32)]
    out = pl.pallas_call(
        functools.partial(_ffn_kernel, layer=layer, n_in=len(xs), n_out=len(out_shape)),
        grid=(N_FFN_STEPS,),
        in_specs=_stream_specs(split_in, False) + [
            _full((DEPTH, D_MODEL)),
            mod_spec(chunk0), mod_spec(chunk0 + 1), mod_spec(chunk0 + 2),
            pl.BlockSpec((None, D_MODEL, tf), w_col),
            pl.BlockSpec((None, D_MODEL, tf), w_col),
            pl.BlockSpec((None, tf, D_MODEL), w_row),
        ],
        out_specs=_stream_specs(split_out, True),
        out_shape=out_shape,
        scratch_shapes=[pltpu.VMEM((tm, D_MODEL), BF16),
                        pltpu.VMEM((tm, D_FF), BF16),
                        pltpu.VMEM((nj, D_MODEL, tf), BF16),
                        pltpu.VMEM((nj, D_MODEL, tf), BF16),
                        pltpu.VMEM((D_FF, D_MODEL), BF16)],
        compiler_params=_cparams("arbitrary"),
        name="ffn",
    )(*xs, norm_g, mod, mod, mod, wg, wu, wd)
    return tuple(out)


def _once(shape, index_map):
    return pl.BlockSpec(shape, index_map, pipeline_mode=pl.Buffered(1))


def _mod_chunk(layer, c):
    return pl.BlockSpec((None, MOD_ROWS, D_MODEL), lambda *_: (layer, 0, c))


def _head_ones(n):
    r = lax.broadcasted_iota(jnp.int32, (n, n), 0) >> HEAD_SHIFT
    c = lax.broadcasted_iota(jnp.int32, (n, n), 1) >> HEAD_SHIFT
    return jnp.where(r == c, 1.0, 0.0).astype(BF16)


def _head_mean_square(x, ones_ref=None):
    n = x.shape[-1]
    x2 = x * x
    hi = x2.astype(BF16)
    lo = (x2 - hi.astype(F32)).astype(BF16)
    ones_bd = _head_ones(n) if ones_ref is None else ones_ref[0:n, 0:n]
    return (_dot(hi, ones_bd) + _dot(lo, ones_bd)) * (1.0 / HEAD_DIM)


def _head_norm(x, head_gain, ones_ref=None):
    gain_row = jnp.concatenate([head_gain] * (x.shape[-1] // HEAD_DIM), axis=-1)
    return x * lax.rsqrt(_head_mean_square(x, ones_ref) + EPS) * gain_row


def _head_cols(x, h):
    return x[:, h * HEAD_DIM:(h + 1) * HEAD_DIM].astype(BF16)


def _softmax_pv(scores, values, sink):
    m = jnp.max(scores[0], axis=-1, keepdims=True)
    for s in scores[1:]:
        m = jnp.maximum(m, jnp.max(s, axis=-1, keepdims=True))
    if sink is not None:
        m = jnp.maximum(m, sink)
    denom = None
    acc = None
    for s, v in zip(scores, values):
        p = jnp.exp(s - m)
        d = jnp.sum(p, axis=-1, keepdims=True)
        o = _dot(p.astype(BF16), v)
        denom = d if denom is None else denom + d
        acc = o if acc is None else acc + o
    if sink is not None:
        denom = denom + jnp.exp(sink - m)
    return acc / denom


def _rope(x, cos, sin_lo, sin_hi):
    cols = []
    for c in range(x.shape[-1] // 128):
        xc = x[:, c * 128:(c + 1) * 128]
        cols.append(xc * cos + pltpu.roll(xc, 112, 1) * sin_lo + pltpu.roll(xc, 16, 1) * sin_hi)
    return cols[0] if len(cols) == 1 else jnp.concatenate(cols, axis=-1)


def _block_diag(blocks):
    n = len(blocks)
    w = blocks[0].shape[0]
    rows = []
    for k, blk in enumerate(blocks):
        parts = []
        if k > 0:
            parts.append(jnp.zeros((w, k * w), F32))
        parts.append(blk)
        if k < n - 1:
            parts.append(jnp.zeros((w, (n - 1 - k) * w), F32))
        rows.append(jnp.concatenate(parts, axis=-1))
    return jnp.concatenate(rows, axis=0)


def _rglru_gates(xc, wa, ba, wx, bx, lam):
    xb = xc.astype(BF16)
    r = _sigmoid(_dot(xb, wa) + ba)
    i = _sigmoid(_dot(xb, wx) + bx)
    softplus = jnp.maximum(-lam, 0.0) + jnp.log1p(jnp.exp(-jnp.abs(lam)))
    log_a = (-LRU_C) * r * softplus
    a = jnp.exp(log_a)
    b = jnp.sqrt(1.0 - a * a) * (i * xc)
    return a, b


def _block_prefix(a, b, reverse):
    t = a.shape[0]
    row = lax.broadcasted_iota(jnp.int32, a.shape, 0) & 7
    for d in (1, 2, 4):
        if reverse:
            a_s = pltpu.roll(a, t - d, 0)
            b_s = pltpu.roll(b, t - d, 0)
            ok = row < 8 - d
        else:
            a_s = pltpu.roll(a, d, 0)
            b_s = pltpu.roll(b, d, 0)
            ok = row >= d
        b = jnp.where(ok, a * b_s + b, b)
        a = jnp.where(ok, a * a_s, a)
    return a, b


def _conv4(x, w_ref, b_row):
    t = x.shape[0]
    row = lax.broadcasted_iota(jnp.int32, x.shape, 0)
    xm2 = jnp.where(row >= 2, pltpu.roll(x, 2, 0), 0.0)
    xm1 = jnp.where(row >= 1, pltpu.roll(x, 1, 0), 0.0)
    xp1 = jnp.where(row < t - 1, pltpu.roll(x, t - 1, 0), 0.0)
    return (xm2 * w_ref[0:1, :] + xm1 * w_ref[1:2, :] + x * w_ref[2:3, :] + xp1 * w_ref[3:4, :]) + b_row


def _rglru_prepare(cx, cy, conv_w_ref, conv_b, gate_w_ref, ba_ref, bx_ref, lam_ref,
                   af_ref, bf_ref, ab_ref, bb_ref, gel_ref):
    xc = _conv4(cx, conv_w_ref, conv_b)
    a, b = _rglru_gates(xc, gate_w_ref[0], ba_ref[0:1, :], gate_w_ref[1], bx_ref[0:1, :], lam_ref[0:1, :])
    a, b = _block_prefix(a, b, reverse=False)
    af_ref[...] = a
    bf_ref[...] = b
    a, b = _rglru_gates(xc, gate_w_ref[2], ba_ref[1:2, :], gate_w_ref[3], bx_ref[1:2, :], lam_ref[1:2, :])
    a, b = _block_prefix(a, b, reverse=True)
    ab_ref[...] = a
    bb_ref[...] = b
    gel_ref[...] = _gelu_tanh(cy)


def _rglru_finish(h0f, h0b, af_ref, bf_ref, ab_ref, bb_ref, hf_ref, hb_ref, gel_ref):
    nblk = af_ref.shape[0] // 8

    def body(k, carry):
        cf, cb = carry
        rf = pl.ds(pl.multiple_of(k * 8, 8), 8)
        hf = bf_ref[rf, :] + af_ref[rf, :] * cf
        hf_ref[rf, :] = hf
        rb = pl.ds(pl.multiple_of((nblk - 1 - k) * 8, 8), 8)
        hb = bb_ref[rb, :] + ab_ref[rb, :] * cb
        hb_ref[rb, :] = hb
        return hf[7:8, :], hb[0:1, :]

    cf, cb = lax.fori_loop(0, nblk, body, (h0f, h0b))
    oc = (hf_ref[...] + hb_ref[...]) * gel_ref[...]
    return oc, cf, cb


def _store_gate_weights(gate_w_ref, wa_ref, wx_ref):
    for d in range(2):
        gate_w_ref[2 * d] = _block_diag([wa_ref[d, n] for n in range(N_HEADS)]).astype(BF16)
        gate_w_ref[2 * d + 1] = _block_diag([wx_ref[d, n] for n in range(N_HEADS)]).astype(BF16)


def _lane_head_masks(n):
    lane = lax.broadcasted_iota(jnp.int32, (1, n), 1) >> HEAD_SHIFT
    return [jnp.where(lane == h, 1.0, 0.0) for h in range(n // HEAD_DIM)]


def _log_decays(theta_ref, masks):
    theta = theta_ref[...]
    lanes = theta[:, 0:1] * masks[0]
    for h in range(1, N_HEADS):
        lanes = lanes + theta[:, h:h + 1] * masks[h]
    lg = jnp.log1p(-jnp.exp(lanes))
    return lg[0:1, :], lg[1:2, :]


RET_BLOCK = 256


RET_TAB_DEC = 0
RET_TAB_QDEC = RET_TAB_DEC + N_HEADS * RET_BLOCK
RET_TAB_KDEC = RET_TAB_QDEC + 2 * RET_BLOCK
RET_TAB_CHUNK = RET_TAB_KDEC + 2 * RET_BLOCK
RET_TAB_ROWS = RET_TAB_CHUNK + 16


def _store_retention_tables(theta_ref, tab_ref):
    c, w = RET_BLOCK, GROUP_W
    lgf, lgb = _log_decays(theta_ref, _lane_head_masks(w))
    rel = (lax.broadcasted_iota(jnp.int32, (c, c), 0) - lax.broadcasted_iota(jnp.int32, (c, c), 1)).astype(F32)
    for h in range(N_HEADS):
        gf = lgf[:, h * HEAD_DIM:h * HEAD_DIM + 1]
        gb = lgb[:, h * HEAD_DIM:h * HEAD_DIM + 1]
        e = jnp.exp(jnp.where(rel >= 0, gf * rel, gb * (-rel)))
        tab_ref[RET_TAB_DEC + h * c:RET_TAB_DEC + (h + 1) * c, :] = jnp.where(rel == 0, 2.0, e)
    pos = lax.broadcasted_iota(jnp.int32, (c, w), 0).astype(F32)
    tab_ref[RET_TAB_QDEC:RET_TAB_QDEC + c, :] = jnp.exp(lgf * (pos + 1.0))
    tab_ref[RET_TAB_QDEC + c:RET_TAB_QDEC + 2 * c, :] = jnp.exp(lgb * (float(c) - pos))
    tab_ref[RET_TAB_KDEC:RET_TAB_KDEC + c, :] = jnp.exp(lgf * (float(c - 1) - pos))
    tab_ref[RET_TAB_KDEC + c:RET_TAB_KDEC + 2 * c, :] = jnp.exp(lgb * pos)
    tab_ref[RET_TAB_CHUNK:RET_TAB_CHUNK + 8, :] = jnp.broadcast_to(jnp.exp(lgf * float(c)), (8, w))
    tab_ref[RET_TAB_CHUNK + 8:RET_TAB_CHUNK + 16, :] = jnp.broadcast_to(jnp.exp(lgb * float(c)), (8, w))


def _retention(q, k8, vb, s0, tab_ref, masks, o_ref):
    t, w = q.shape
    c = RET_BLOCK
    nh = w // HEAD_DIM
    q_dec = [tab_ref[RET_TAB_QDEC + d * c:RET_TAB_QDEC + (d + 1) * c, :] for d in range(2)]
    k_dec = [tab_ref[RET_TAB_KDEC + d * c:RET_TAB_KDEC + (d + 1) * c, :] for d in range(2)]
    chunk_dec = [tab_ref[RET_TAB_CHUNK + 8 * d:RET_TAB_CHUNK + 8 * d + 1, :] for d in range(2)]
    dec = tab_ref[RET_TAB_DEC:RET_TAB_DEC + nh * c, :]
    r_head = lax.broadcasted_iota(jnp.int32, (w, w), 0) >> HEAD_SHIFT
    c_head = lax.broadcasted_iota(jnp.int32, (w, w), 1) >> HEAD_SHIFT
    same_head = jnp.where(r_head == c_head, 1.0, 0.0)
    states = [None, None] if s0 is None else list(s0)

    def carry(d, rows, o):
        if states[d] is not None:
            o = o + _dot((q[rows, :] * q_dec[d]).astype(BF16), states[d].astype(BF16))
        upd = _dot_tn((k8[rows, :] * k_dec[d]).astype(BF16), vb[rows, :]) * same_head
        states[d] = upd if states[d] is None else states[d] * chunk_dec[d] + upd
        return o

    for ci in range(t // c):
        rows = slice(ci * c, (ci + 1) * c)
        qc = q[rows, :]
        q_stack = jnp.concatenate([(qc * masks[h]).astype(BF16) for h in range(nh)], axis=0)
        inner = (_dot_nt(q_stack, k8[rows, :].astype(BF16)) * dec).astype(BF16)
        out = _dot(inner, vb[rows, :])
        o = out[0:c, :] * masks[0]
        for h in range(1, nh):
            o = o + out[h * c:(h + 1) * c, :] * masks[h]
        o_ref[rows, :] = carry(0, rows, o)
    for ci in reversed(range(t // c)):
        rows = slice(ci * c, (ci + 1) * c)
        if states[1] is not None:
            o_ref[rows, :] = carry(1, rows, o_ref[rows, :])
        else:
            carry(1, rows, None)
    return states[0], states[1]


def _ctx_mixer_kernel(*refs, layer, n_prev):
    prev_refs = refs[:n_prev]
    (x_ref, n2_ref, sh_ref, sc_ref, g2_ref, win_ref, wout_ref,
     aqn_ref, akn_ref, bqn_ref, bkn_ref, sink_ref,
     convw_ref, convb_ref, wa_ref, ba_ref, wx_ref, bx_ref, lam_ref, theta_ref, dn_ref,
     xn_ref, *state_refs) = refs[n_prev:n_prev + 28]
    (win_s, wout_s, u_ref, mixed_ref,
     gate_w_ref, af_ref, bf_ref, ab_ref, bb_ref, hf_ref, hb_ref, gel_ref, ret_ref, tab_ref, ones_ref) = refs[n_prev + 28:]
    t = SEQ
    lrow = slice(layer, layer + 1)
    for prev_ref, state_ref in zip(prev_refs, state_refs):
        for earlier in range(layer):
            state_ref[earlier] = prev_ref[earlier]
    ka_ref, va_ref, kb_ref, vb_ref, stc_ref, std_ref = (ref.at[layer] for ref in state_refs)

    @pl.when(pl.program_id(0) == 0)
    def _():
        for c in range(IN_WIDTH // 512):
            win_s[:, c * 512:(c + 1) * 512] = win_ref[:, c * 512:(c + 1) * 512].astype(BF16)
        wout_s[...] = wout_ref[...].astype(BF16)
        _store_gate_weights(gate_w_ref, wa_ref, wx_ref)
        _store_retention_tables(theta_ref, tab_ref)
        ones_ref[...] = _head_ones(GROUP_W)

    x = x_ref[...]
    h = _norm_mod(x, n2_ref[lrow, :], sc_ref[0:1, :], sh_ref[0:1, :]).astype(BF16)
    c_cols = slice(COL_CX, COL_CX + 2 * GROUP_W)
    u_ref[:, c_cols] = _dot(h, win_s[:, c_cols])
    _rglru_prepare(u_ref[:, COL_CX:COL_CX + GROUP_W], u_ref[:, COL_CY:COL_CY + GROUP_W],
                   convw_ref, convb_ref[lrow, :], gate_w_ref, ba_ref, bx_ref, lam_ref,
                   af_ref, bf_ref, ab_ref, bb_ref, gel_ref)
    u_ref[:, 0:COL_CX] = _dot(h, win_s[:, 0:COL_CX])
    u_ref[:, COL_DQ:IN_WIDTH] = _dot(h, win_s[:, COL_DQ:IN_WIDTH])

    for (cq, ck, cv, qn_ref, kn_ref, k_out, v_out, col0, use_sink) in (
            (COL_AQ, COL_AK, COL_AV, aqn_ref, akn_ref, ka_ref, va_ref, 0, True),
            (COL_BQ, COL_BK, COL_BV, bqn_ref, bkn_ref, kb_ref, vb_ref, GROUP_W, False)):
        q = _head_norm(u_ref[:, cq:cq + 256], qn_ref[lrow, :], ones_ref)
        k = _head_norm(u_ref[:, ck:ck + 128], kn_ref[lrow, :], ones_ref)
        v = u_ref[:, cv:cv + 128]
        k_out[...] = k
        v_out[...] = v
        qs = q * (HEAD_DIM ** -0.5)
        heads = []
        for hd in range(N_HEADS):
            kv = hd // 2
            s = _dot_nt(_head_cols(qs, hd), _head_cols(k, kv))
            sink = jnp.full((t, 1), sink_ref[layer, hd], F32) if use_sink else None
            heads.append(_softmax_pv([s], [_head_cols(v, kv)], sink))
        mixed_ref[:, col0:col0 + GROUP_W] = jnp.concatenate(heads, axis=-1).astype(BF16)

    zero = jnp.zeros((1, GROUP_W), F32)
    oc, cf, cb = _rglru_finish(zero, zero, af_ref, bf_ref, ab_ref, bb_ref, hf_ref, hb_ref, gel_ref)
    mixed_ref[:, 2 * GROUP_W:3 * GROUP_W] = oc.astype(BF16)
    stc_ref[0:1, :] = cf
    stc_ref[1:2, :] = cb

    masks = _lane_head_masks(GROUP_W)
    k8 = u_ref[:, COL_DK:COL_DK + GROUP_W] * (HEAD_DIM ** -0.5)
    vb = u_ref[:, COL_DV:COL_DV + GROUP_W].astype(BF16)
    final_states = _retention(u_ref[:, COL_DQ:COL_DQ + GROUP_W], k8, vb, None, tab_ref, masks, ret_ref)
    o = ret_ref[...]
    o = (o * lax.rsqrt(_head_mean_square(o, ones_ref) + EPS) * dn_ref[lrow, :]
         * _silu(u_ref[:, COL_DG:COL_DG + GROUP_W]))
    mixed_ref[:, 3 * GROUP_W:4 * GROUP_W] = o.astype(BF16)
    for d, s_full in enumerate(final_states):
        for hd in range(N_HEADS):
            std_ref[d, hd] = s_full[hd * 64:(hd + 1) * 64, hd * 64:(hd + 1) * 64]

    xn_ref[...] = x + g2_ref[0:1, :] * _dot(mixed_ref[...], wout_s[...])


def _ctx_mixers(x, mod, layer, prev, norm2_g, w_in, w_out,
                a_qn, a_kn, a_sink, b_qn, b_kn, c_conv_w, c_conv_b, c_wa, c_ba, c_wx, c_bx,
                c_lambda, d_theta, d_norm_g):
    per_request = lambda slots, shape: pl.BlockSpec((None, slots) + shape, lambda b: (b,) + (0,) * (1 + len(shape)))
    state_dims = [(SEQ, 128)] * 4 + [(2, GROUP_W), (2, N_HEADS, HEAD_DIM, HEAD_DIM)]
    scr = pltpu.VMEM((SEQ, GROUP_W), F32)
    out = pl.pallas_call(
        functools.partial(_ctx_mixer_kernel, layer=layer, n_prev=len(prev)),
        grid=(BATCH,),
        in_specs=[per_request(layer, dims) for dims in state_dims[:len(prev)]] + [
            pl.BlockSpec((SEQ, D_MODEL), lambda b: (b, 0)),
            _full((DEPTH, D_MODEL)),
            _mod_chunk(layer, 3), _mod_chunk(layer, 4), _mod_chunk(layer, 5),
            _once((None, D_MODEL, IN_WIDTH), lambda b: (layer, 0, 0)),
            _once((None, D_MODEL, D_MODEL), lambda b: (layer, 0, 0)),
            _full((DEPTH, HEAD_DIM)), _full((DEPTH, HEAD_DIM)), _full((DEPTH, HEAD_DIM)), _full((DEPTH, HEAD_DIM)),
            pl.BlockSpec(memory_space=pltpu.SMEM),
            _layer_block((4, GROUP_W), layer), _full((DEPTH, GROUP_W)),
            _layer_block((2, N_HEADS, HEAD_DIM, HEAD_DIM), layer), _layer_block((2, GROUP_W), layer),
            _layer_block((2, N_HEADS, HEAD_DIM, HEAD_DIM), layer), _layer_block((2, GROUP_W), layer),
            _layer_block((2, GROUP_W), layer),
            _layer_block((2, N_HEADS), layer), _full((DEPTH, GROUP_W)),
        ],
        out_specs=[pl.BlockSpec((SEQ, D_MODEL), lambda b: (b, 0))] + [
            per_request(layer + 1, dims) for dims in state_dims],
        out_shape=[jax.ShapeDtypeStruct((N_ROWS, D_MODEL), F32)] + [
            jax.ShapeDtypeStruct((BATCH, layer + 1) + dims, F32) for dims in state_dims],
        input_output_aliases={len(prev): 0},
        scratch_shapes=[pltpu.VMEM((D_MODEL, IN_WIDTH), BF16), pltpu.VMEM((D_MODEL, D_MODEL), BF16),
                        pltpu.VMEM((SEQ, IN_WIDTH), F32), pltpu.VMEM((SEQ, D_MODEL), BF16),
                        pltpu.VMEM((4, GROUP_W, GROUP_W), BF16)] + [scr] * 8 + [
                            pltpu.VMEM((RET_TAB_ROWS, GROUP_W), F32), pltpu.VMEM((GROUP_W, GROUP_W), BF16)],
        compiler_params=_cparams("arbitrary"),
        name="ctx_mixers",
    )(*prev, x, norm2_g, mod, mod, mod, w_in, w_out,
      a_qn, a_kn, b_qn, b_kn, a_sink, c_conv_w, c_conv_b, c_wa, c_ba, c_wx, c_bx,
      c_lambda, d_theta, d_norm_g)
    return out[0], tuple(out[1:])


LAT_BLOCK0 = N_CTX_ROWS // DEC_SEQ


def _lat_attn_kernel(x_ref, n2_ref, sh_ref, sc_ref, g2_ref, win_ref, wout_ref,
                     kca_ref, vca_ref, kcb_ref, vcb_ref,
                     aqn_ref, akn_ref, bqn_ref, bkn_ref, sink_ref, cos_ref, sinl_ref, sinh_ref,
                     xn_ref, h_ref, u_ref, o_ref, *, layer):
    t = DEC_SEQ
    lrow = slice(layer, layer + 1)
    mrow = pl.ds(1 + pl.program_id(0), 1)
    cos, sin_lo, sin_hi = cos_ref[...], sinl_ref[...], sinh_ref[...]
    scale = HEAD_DIM ** -0.5
    x = x_ref[...]
    h_ref[...] = _norm_mod(x, n2_ref[lrow, :], sc_ref[mrow, :], sh_ref[mrow, :]).astype(BF16)
    u_ref[...] = _dot(h_ref[...], win_ref[...].astype(BF16))

    q = _rope(_head_norm(u_ref[:, COL_AQ:COL_AQ + 256], aqn_ref[lrow, :]), cos, sin_lo, sin_hi)
    k = _rope(_head_norm(u_ref[:, COL_AK:COL_AK + 128], akn_ref[lrow, :]), cos, sin_lo, sin_hi)
    qh = [_head_cols(q * scale, h) for h in range(4)]
    v = u_ref[:, COL_AV:COL_AV + 128]
    kh = [_head_cols(k, kv) for kv in range(2)]
    vh = [_head_cols(v, kv) for kv in range(2)]
    kch = [_head_cols(kca_ref[...], kv) for kv in range(2)]
    vch = [_head_cols(vca_ref[...], kv) for kv in range(2)]
    w = ATT_BLOCK
    span = 3 * w
    for n in range(t // w):
        start = min(max((n - 1) * w, 0), t - span)
        rows = slice(n * w, (n + 1) * w)
        band = slice(start, start + span)
        qpos = (lax.broadcasted_iota(jnp.int32, (2 * w, span), 0) & (w - 1)) + n * w
        kpos = lax.broadcasted_iota(jnp.int32, (2 * w, span), 1) + start
        valid = jnp.abs(qpos - kpos) <= WINDOW
        heads = []
        for kv in range(2):
            qp = jnp.concatenate([qh[2 * kv][rows, :], qh[2 * kv + 1][rows, :]], axis=0)
            s_ctx = _dot_nt(qp, kch[kv])
            s_band = jnp.where(valid, _dot_nt(qp, kh[kv][band, :]), NEG_INF)
            row = lax.broadcasted_iota(jnp.int32, (2 * w, 1), 0)
            sink = jnp.where(row < w, sink_ref[layer, 2 * kv], sink_ref[layer, 2 * kv + 1])
            o = _softmax_pv([s_ctx, s_band], [vch[kv], vh[kv][band, :]], sink)
            heads += [o[0:w, :], o[w:2 * w, :]]
        o_ref[rows, 0:GROUP_W] = jnp.concatenate(heads, axis=-1).astype(BF16)

    q = _rope(_head_norm(u_ref[:, COL_BQ:COL_BQ + 256], bqn_ref[lrow, :]), cos, sin_lo, sin_hi)
    k = _rope(_head_norm(u_ref[:, COL_BK:COL_BK + 128], bkn_ref[lrow, :]), cos, sin_lo, sin_hi)
    qh = [_head_cols(q * scale, h) for h in range(4)]
    v = u_ref[:, COL_BV:COL_BV + 128]
    kh = [_head_cols(k, kv) for kv in range(2)]
    vh = [_head_cols(v, kv) for kv in range(2)]
    kch = [_head_cols(kcb_ref[...], kv) for kv in range(2)]
    vch = [_head_cols(vcb_ref[...], kv) for kv in range(2)]
    tq = 256
    for n in range(t // tq):
        rows = slice(n * tq, (n + 1) * tq)
        heads = []
        for kv in range(2):
            qp = jnp.concatenate([qh[2 * kv][rows, :], qh[2 * kv + 1][rows, :]], axis=0)
            o = _softmax_pv([_dot_nt(qp, kch[kv]), _dot_nt(qp, kh[kv])], [vch[kv], vh[kv]], None)
            heads += [o[0:tq, :], o[tq:2 * tq, :]]
        o_ref[rows, GROUP_W:2 * GROUP_W] = jnp.concatenate(heads, axis=-1).astype(BF16)

    xn_ref[...] = x + g2_ref[mrow, :] * _dot(o_ref[...], wout_ref[...].astype(BF16))


def _lat_recurrent_kernel(xn_in_ref, h_ref, g2_ref, wc_ref, wqk_ref, wvg_ref, wout_ref, h0_ref,
                          convw_ref, convb_ref, wa_ref, ba_ref, wx_ref, bx_ref, lam_ref,
                          s0_ref, theta_ref, dn_ref,
                          xn_ref, gate_w_ref, af_ref, bf_ref, ab_ref, bb_ref, hf_ref, hb_ref, gel_ref, ret_ref,
                          tab_ref, *, layer):
    lrow = slice(layer, layer + 1)
    mrow = pl.ds(1 + pl.program_id(0), 1)

    @pl.when(pl.program_id(0) == 0)
    def _():
        _store_gate_weights(gate_w_ref, wa_ref, wx_ref)
        _store_retention_tables(theta_ref, tab_ref)

    h = h_ref[...]
    u = _dot(h, wc_ref[...].astype(BF16))
    _rglru_prepare(u[:, 0:GROUP_W], u[:, GROUP_W:2 * GROUP_W], convw_ref, convb_ref[lrow, :],
                   gate_w_ref, ba_ref, bx_ref, lam_ref, af_ref, bf_ref, ab_ref, bb_ref, gel_ref)
    oc, _, _ = _rglru_finish(h0_ref[0:1, :], h0_ref[1:2, :],
                             af_ref, bf_ref, ab_ref, bb_ref, hf_ref, hb_ref, gel_ref)
    y = _dot(oc.astype(BF16), wout_ref[0:GROUP_W, :].astype(BF16))

    uqk = _dot(h, wqk_ref[...].astype(BF16))
    uvg = _dot(h, wvg_ref[...].astype(BF16))
    masks = _lane_head_masks(GROUP_W)
    s0 = tuple(_block_diag([s0_ref[d, hd] for hd in range(N_HEADS)]) for d in range(2))
    _retention(uqk[:, 0:GROUP_W], uqk[:, GROUP_W:2 * GROUP_W] * (HEAD_DIM ** -0.5),
               uvg[:, 0:GROUP_W].astype(BF16), s0, tab_ref, masks, ret_ref)
    o = ret_ref[...]
    o = o * lax.rsqrt(_head_mean_square(o) + EPS) * dn_ref[lrow, :] * _silu(uvg[:, GROUP_W:2 * GROUP_W])
    y = y + _dot(o.astype(BF16), wout_ref[GROUP_W:2 * GROUP_W, :].astype(BF16))
    xn_ref[...] = xn_in_ref[...] + g2_ref[mrow, :] * y


def _lat_mixers(x, mod, layer, caches, state_c, state_d, rope, norm2_g, w_in, w_out,
                a_qn, a_kn, a_sink, b_qn, b_kn, c_conv_w, c_conv_b, c_wa, c_ba, c_wx, c_bx,
                c_lambda, d_theta, d_norm_g):
    rows = pl.BlockSpec((DEC_SEQ, D_MODEL), lambda b: (LAT_BLOCK0 + b, 0))
    h_rows = pl.BlockSpec((DEC_SEQ, D_MODEL), lambda b: (b, 0))
    cache_spec = pl.BlockSpec((None, None, PAST_LEN, 128), lambda b: (b, layer, 0, 0))
    gain = _full((DEPTH, HEAD_DIM))
    table = _once((DEC_SEQ, 128), lambda b: (0, 0))
    out_shape = jax.ShapeDtypeStruct((N_ROWS, D_MODEL), F32)
    win_cols = lambda w, c: _once((None, D_MODEL, w), lambda b: (layer, 0, c))
    wout_rows = lambda h, r: _once((None, h, D_MODEL), lambda b: (layer, r, 0))

    xn, h = pl.pallas_call(
        functools.partial(_lat_attn_kernel, layer=layer),
        grid=(DEC_BATCH,),
        in_specs=[rows, _full((DEPTH, D_MODEL)),
                  _mod_chunk(layer, 3), _mod_chunk(layer, 4), _mod_chunk(layer, 5),
                  win_cols(4 * GROUP_W, 0), wout_rows(2 * GROUP_W, 0),
                  cache_spec, cache_spec, cache_spec, cache_spec,
                  gain, gain, gain, gain,
                  pl.BlockSpec(memory_space=pltpu.SMEM),
                  table, table, table],
        out_specs=[rows, pl.BlockSpec((DEC_SEQ, D_MODEL), lambda b: (b, 0), pipeline_mode=pl.Buffered(1))],
        out_shape=[out_shape, jax.ShapeDtypeStruct((N_LAT_ROWS, D_MODEL), BF16)],
        input_output_aliases={0: 0},
        scratch_shapes=[pltpu.VMEM((DEC_SEQ, 4 * GROUP_W), F32), pltpu.VMEM((DEC_SEQ, 2 * GROUP_W), BF16)],
        compiler_params=_cparams("arbitrary"),
        name="lat_attention",
    )(x, norm2_g, mod, mod, mod, w_in, w_out, *caches, a_qn, a_kn, b_qn, b_kn, a_sink, *rope)

    scr = pltpu.VMEM((DEC_SEQ, GROUP_W), F32)
    xn = pl.pallas_call(
        functools.partial(_lat_recurrent_kernel, layer=layer),
        grid=(DEC_BATCH,),
        in_specs=[
            rows, h_rows, _mod_chunk(layer, 5),
            win_cols(2 * GROUP_W, COL_CX // (2 * GROUP_W)),
            win_cols(2 * GROUP_W, COL_DQ // (2 * GROUP_W)), win_cols(2 * GROUP_W, COL_DV // (2 * GROUP_W)),
            wout_rows(2 * GROUP_W, 1),
            pl.BlockSpec((None, None, 2, GROUP_W), lambda b: (b, layer, 0, 0)),
            _layer_block((4, GROUP_W), layer), _full((DEPTH, GROUP_W)),
            _layer_block((2, N_HEADS, HEAD_DIM, HEAD_DIM), layer), _layer_block((2, GROUP_W), layer),
            _layer_block((2, N_HEADS, HEAD_DIM, HEAD_DIM), layer), _layer_block((2, GROUP_W), layer),
            _layer_block((2, GROUP_W), layer),
            pl.BlockSpec((None, None, 2, N_HEADS, HEAD_DIM, HEAD_DIM), lambda b: (b, layer, 0, 0, 0, 0)),
            _layer_block((2, N_HEADS), layer), _full((DEPTH, GROUP_W))],
        out_specs=rows,
        out_shape=out_shape,
        input_output_aliases={0: 0},
        scratch_shapes=[pltpu.VMEM((4, GROUP_W, GROUP_W), BF16)] + [scr] * 8 + [
            pltpu.VMEM((RET_TAB_ROWS, GROUP_W), F32)],
        compiler_params=_cparams("arbitrary"),
        name="lat_recurrent",
    )(xn, h, mod, w_in, w_in, w_in, w_out, state_c, c_conv_w, c_conv_b, c_wa, c_ba, c_wx, c_bx, c_lambda,
      state_d, d_theta, d_norm_g)
    return xn


def _rope_tables():
    t = np.arange(DEC_SEQ)
    row = (t // GRID_W).astype(np.float64)[:, None]
    col = (t % GRID_W).astype(np.float64)[:, None]
    half = HEAD_DIM // 2
    inv = 1.0 / (ROPE_BASE ** (np.arange(0, half, 2, dtype=np.float64) / half))
    j = np.arange(128) % HEAD_DIM
    ang = np.where((j < half)[None, :], row, col) * inv[j % (half // 2)][None, :]
    first = ((j % half) < half // 2)[None, :]
    cos, sin = np.cos(ang), np.sin(ang)
    return tuple(jnp.asarray(a, F32) for a in (cos, np.where(first, -sin, 0.0), np.where(first, 0.0, sin)))


def kernel(x_prompt, x_sample, cache_a_k, cache_a_v, cache_b_k, cache_b_v, state_c, state_d, c, c_ctx, norm1_g, norm2_g, norm3_g, w_mod, b_mod, ffn1_wg, ffn1_wu, ffn1_wd, ffn2_wg, ffn2_wu, ffn2_wd, w_in, w_out, a_qn, a_kn, a_sink, b_qn, b_kn, c_conv_w, c_conv_b, c_wa, c_ba, c_wx, c_bx, c_lambda, d_theta, d_norm_g):
    mod = _modulation(c_ctx, c, w_mod, b_mod)
    rope = _rope_tables()
    caches = tuple(t.reshape(DEC_BATCH, DEPTH, PAST_LEN, 128) for t in (cache_a_k, cache_a_v, cache_b_k, cache_b_v))
    mixer_params = (a_qn, a_kn, a_sink, b_qn, b_kn, c_conv_w, c_conv_b, c_wa, c_ba, c_wx, c_bx,
                    c_lambda, d_theta, d_norm_g)
    xs = (x_prompt.reshape(N_CTX_ROWS, D_MODEL), x_sample.reshape(N_LAT_ROWS, D_MODEL))
    states = ()
    for l in range(DEPTH):
        (x,) = _ffn(xs, mod, l, 0, norm1_g, ffn1_wg, ffn1_wu, ffn1_wd)
        x, states = _ctx_mixers(x, mod, l, states, norm2_g, w_in, w_out, *mixer_params)
        x = _lat_mixers(x, mod, l, caches, state_c, state_d, rope, norm2_g, w_in, w_out, *mixer_params)
        xs = _ffn((x,), mod, l, 6, norm3_g, ffn2_wg, ffn2_wu, ffn2_wd, split_out=(l == DEPTH - 1))
    y_p, y_s = xs
    ka, va, kb, vb, st_c, st_d = states
    kv_shape = (BATCH, DEPTH, SEQ, 2, HEAD_DIM)
    return (y_p.reshape(BATCH, SEQ, D_MODEL), y_s.reshape(DEC_BATCH, DEC_SEQ, D_MODEL),
            ka.reshape(kv_shape), va.reshape(kv_shape), kb.reshape(kv_shape), vb.reshape(kv_shape),
            st_c, st_d)
```

```python
import functools
import math

import numpy as np
import jax
import jax.numpy as jnp
from jax import lax
from jax.experimental import pallas as pl
from jax.experimental.pallas import tpu as pltpu

F32 = jnp.float32
BF16 = jnp.bfloat16

D_MODEL = 1024
BATCH = 16
SEQ = 256
DEPTH = 2
DEC_BATCH = 2
DEC_SEQ = 1024
PAST_LEN = 512
GRID_W = 64
HEAD_DIM = 64
HEAD_SHIFT = 6
N_HEADS = 4
GROUP_W = 256
WINDOW = 128
ATT_BLOCK = 128
ROPE_BASE = 10000.0
LRU_C = 8.0
D_FF = 2816
N_MOD = 9
EPS = 1e-6
NEG_INF = -1e30
IN_WIDTH = 2560

N_CTX_ROWS = BATCH * SEQ
N_LAT_ROWS = DEC_BATCH * DEC_SEQ
N_ROWS = N_CTX_ROWS + N_LAT_ROWS
MOD_ROWS = 8
MOD_GROUP = 1024

VMEM_LIMIT_BYTES = 56 * 1024 * 1024

COL_AQ, COL_AK, COL_AV = 0, 256, 384
COL_BQ, COL_BK, COL_BV = 512, 768, 896
COL_CX, COL_CY = 1024, 1280
COL_DQ, COL_DK, COL_DV, COL_DG = 1536, 1792, 2048, 2304


def _cparams(*sem):
    return pltpu.CompilerParams(dimension_semantics=sem, vmem_limit_bytes=VMEM_LIMIT_BYTES)


def _dot(a, b):
    return jnp.dot(a, b, preferred_element_type=F32)


def _dot_nt(a, b):
    return lax.dot_general(a, b, (((1,), (1,)), ((), ())), preferred_element_type=F32)


def _dot_tn(a, b):
    return lax.dot_general(a, b, (((0,), (0,)), ((), ())), preferred_element_type=F32)


def _sigmoid(x):
    return 0.5 * jnp.tanh(0.5 * x) + 0.5


def _silu(x):
    return x * _sigmoid(x)


def _gelu_tanh(x):
    return 0.5 * x * (1.0 + jnp.tanh(math.sqrt(2.0 / math.pi) * (x + 0.044715 * (x * x * x))))


def _mod_row(i, tm, s):
    if tm >= MOD_GROUP:
        block_index = i * (tm // MOD_GROUP) + s
    else:
        block_index = i >> int(math.log2(MOD_GROUP // tm))
    return jnp.maximum(block_index - (N_CTX_ROWS // MOD_GROUP - 1), 0)


def _norm_mod(x, g, sc, sh):
    ms = jnp.mean(x * x, axis=-1, keepdims=True)
    return (x * lax.rsqrt(ms + EPS) * g) * (1.0 + sc) + sh


def _full(shape):
    return pl.BlockSpec(shape, lambda *_: (0,) * len(shape))


def _layer_block(shape, layer):
    return pl.BlockSpec((None,) + shape, lambda *_: (layer,) + (0,) * len(shape))


MOD_TN = 3072


def _mod_kernel(cc_ref, c_ref, w_ref, b_ref, o_ref):
    l = pl.program_id(0)
    pad = jnp.zeros((MOD_ROWS - 1 - DEC_BATCH, D_MODEL), F32)
    cond = jnp.concatenate([cc_ref[...], c_ref[...], pad], axis=0)
    o_ref[...] = _dot(_silu(cond).astype(BF16), w_ref[...].astype(BF16)) + b_ref[pl.ds(l, 1), :]


def _modulation(c_ctx, c, w_mod, b_mod):
    n = N_MOD * D_MODEL
    return pl.pallas_call(
        _mod_kernel,
        grid=(DEPTH, n // MOD_TN),
        in_specs=[
            pl.BlockSpec((1, D_MODEL), lambda l, j: (0, 0)),
            pl.BlockSpec((DEC_BATCH, D_MODEL), lambda l, j: (0, 0)),
            pl.BlockSpec((None, D_MODEL, MOD_TN), lambda l, j: (l, 0, j)),
            pl.BlockSpec((DEPTH, MOD_TN), lambda l, j: (0, j)),
        ],
        out_specs=pl.BlockSpec((None, MOD_ROWS, MOD_TN), lambda l, j: (l, 0, j)),
        out_shape=jax.ShapeDtypeStruct((DEPTH, MOD_ROWS, n), F32),
        compiler_params=_cparams("arbitrary", "arbitrary"),
        name="modulation",
    )(c_ctx.reshape(1, D_MODEL), c, w_mod, b_mod)


FFN_TM = 1024
FFN_TF = 256
N_CTX_TILES = N_CTX_ROWS // FFN_TM


FFN_NJ = D_FF // FFN_TF
N_FFN_TILES = N_ROWS // FFN_TM
N_FFN_STEPS = FFN_NJ + N_FFN_TILES


def _ffn_tile(step):
    return jnp.maximum(step - FFN_NJ, 0)


def _on_stream_part(tile, x_refs, o_refs, fn):
    if len(x_refs) == 1 and len(o_refs) == 1:
        fn(x_refs[0], o_refs[0])
    else:
        pl.when(tile < N_CTX_TILES)(lambda: fn(x_refs[0], o_refs[0]))
        pl.when(tile >= N_CTX_TILES)(lambda: fn(x_refs[-1], o_refs[-1]))


def _ffn_kernel(*refs, layer, n_in, n_out):
    x_refs = refs[:n_in]
    n_ref, sh_ref, sc_ref, g_ref, wg_ref, wu_ref, wd_ref = refs[n_in:n_in + 7]
    o_refs = refs[n_in + 7:n_in + 7 + n_out]
    h_ref, a_ref, wg_s, wu_s, wd_s = refs[n_in + 7 + n_out:]
    nj, tf = FFN_NJ, FFN_TF
    s = pl.program_id(0)
    tile = _ffn_tile(s)
    r = _mod_row(tile, FFN_TM, 0)

    def load_tile():
        def init(x_ref, _):
            h = _norm_mod(x_ref[...], n_ref[layer:layer + 1, :], sc_ref[pl.ds(r, 1), :], sh_ref[pl.ds(r, 1), :])
            h_ref[...] = h.astype(BF16)
        _on_stream_part(tile, x_refs, o_refs, init)

    def up_chunk(j, cols):
        h = h_ref[...]
        a_ref[:, cols] = (_silu(_dot(h, wg_s[j])) * _dot(h, wu_s[j])).astype(BF16)

    def down_and_store():
        y = (0.5 * g_ref[pl.ds(r, 1), :]) * _dot(a_ref[...], wd_s[...])

        def store(x_ref, o_ref):
            o_ref[...] = x_ref[...] + y
        _on_stream_part(tile, x_refs, o_refs, store)

    def keep_arrived_chunk():
        wg_s[s] = wg_ref[...].astype(BF16)
        wu_s[s] = wu_ref[...].astype(BF16)
        wd_s[pl.ds(pl.multiple_of(s * tf, tf), tf), :] = wd_ref[...].astype(BF16)

    def up_previous_chunk():
        up_chunk(s - 1, pl.ds(pl.multiple_of((s - 1) * tf, tf), tf))

    @pl.when(s == 0)
    def _():
        load_tile()
        keep_arrived_chunk()

    @pl.when((s > 0) & (s < nj))
    def _():
        up_previous_chunk()
        keep_arrived_chunk()

    @pl.when(s == nj)
    def _():
        up_previous_chunk()
        down_and_store()

    @pl.when(s > nj)
    def _():
        load_tile()
        for j in range(nj):
            up_chunk(j, slice(j * tf, (j + 1) * tf))
        down_and_store()


def _stream_specs(split, buffered_once):
    tm = FFN_TM
    kw = {"pipeline_mode": pl.Buffered(1)} if buffered_once else {}
    if not split:
        return [pl.BlockSpec((tm, D_MODEL), lambda s: (_ffn_tile(s), 0), **kw)]
    last_ctx = N_CTX_TILES - 1
    return [pl.BlockSpec((tm, D_MODEL), lambda s: (jnp.minimum(_ffn_tile(s), last_ctx), 0), **kw),
            pl.BlockSpec((tm, D_MODEL), lambda s: (jnp.maximum(_ffn_tile(s) - N_CTX_TILES, 0), 0), **kw)]


def _ffn(xs, mod, layer, chunk0, norm_g, wg, wu, wd, split_out=False):
    tm, tf, nj = FFN_TM, FFN_TF, FFN_NJ
    split_in = len(xs) == 2
    mod_spec = lambda c: pl.BlockSpec((None, MOD_ROWS, D_MODEL), lambda s: (layer, 0, c))
    w_col = lambda s: (layer, 0, jnp.minimum(s, nj - 1))
    w_row = lambda s: (layer, jnp.minimum(s, nj - 1), 0)
    if split_out:
        out_shape = [jax.ShapeDtypeStruct((N_CTX_ROWS, D_MODEL), F32),
                     jax.ShapeDtypeStruct((N_LAT_ROWS, D_MODEL), F32)]
    else:
        out_shape = [jax.ShapeDtypeStruct((N_ROWS, D_MODEL), F32)]
    out = pl.pallas_call(
        functools.partial(_ffn_kernel, layer=layer, n_in=len(xs), n_out=len(out_shape)),
        grid=(N_FFN_STEPS,),
        in_specs=_stream_specs(split_in, False) + [
            _full((DEPTH, D_MODEL)),
            mod_spec(chunk0), mod_spec(chunk0 + 1), mod_spec(chunk0 + 2),
            pl.BlockSpec((None, D_MODEL, tf), w_col),
            pl.BlockSpec((None, D_MODEL, tf), w_col),
            pl.BlockSpec((None, tf, D_MODEL), w_row),
        ],
        out_specs=_stream_specs(split_out, True),
        out_shape=out_shape,
        scratch_shapes=[pltpu.VMEM((tm, D_MODEL), BF16),
                        pltpu.VMEM((tm, D_FF), BF16),
                        pltpu.VMEM((nj, D_MODEL, tf), BF16),
                        pltpu.VMEM((nj, D_MODEL, tf), BF16),
                        pltpu.VMEM((D_FF, D_MODEL), BF16)],
        compiler_params=_cparams("arbitrary"),
        name="ffn",
    )(*xs, norm_g, mod, mod, mod, wg, wu, wd)
    return tuple(out)


def _once(shape, index_map):
    return pl.BlockSpec(shape, index_map, pipeline_mode=pl.Buffered(1))


def _mod_chunk(layer, c):
    return pl.BlockSpec((None, MOD_ROWS, D_MODEL), lambda *_: (layer, 0, c))


def _head_mean_square(x):
    n = x.shape[-1]
    r = lax.broadcasted_iota(jnp.int32, (n, n), 0) >> HEAD_SHIFT
    c = lax.broadcasted_iota(jnp.int32, (n, n), 1) >> HEAD_SHIFT
    ones_bd = jnp.where(r == c, 1.0, 0.0).astype(BF16)
    return _dot((x * x).astype(BF16), ones_bd) * (1.0 / HEAD_DIM)


def _head_norm(x, head_gain):
    gain_row = jnp.concatenate([head_gain] * (x.shape[-1] // HEAD_DIM), axis=-1)
    return x * lax.rsqrt(_head_mean_square(x) + EPS) * gain_row


def _head_cols(x, h):
    return x[:, h * HEAD_DIM:(h + 1) * HEAD_DIM].astype(BF16)


def _softmax_pv(scores, values, sink):
    m = jnp.max(scores[0], axis=-1, keepdims=True)
    for s in scores[1:]:
        m = jnp.maximum(m, jnp.max(s, axis=-1, keepdims=True))
    if sink is not None:
        m = jnp.maximum(m, sink)
    denom = None
    acc = None
    for s, v in zip(scores, values):
        p = jnp.exp(s - m)
        d = jnp.sum(p, axis=-1, keepdims=True)
        o = _dot(p.astype(BF16), v)
        denom = d if denom is None else denom + d
        acc = o if acc is None else acc + o
    if sink is not None:
        denom = denom + jnp.exp(sink - m)
    return acc / denom


def _rope(x, cos, sin_lo, sin_hi):
    cols = []
    for c in range(x.shape[-1] // 128):
        xc = x[:, c * 128:(c + 1) * 128]
        cols.append(xc * cos + pltpu.roll(xc, 112, 1) * sin_lo + pltpu.roll(xc, 16, 1) * sin_hi)
    return cols[0] if len(cols) == 1 else jnp.concatenate(cols, axis=-1)


def _block_diag(blocks):
    n = len(blocks)
    w = blocks[0].shape[0]
    rows = []
    for k, blk in enumerate(blocks):
        parts = []
        if k > 0:
            parts.append(jnp.zeros((w, k * w), F32))
        parts.append(blk)
        if k < n - 1:
            parts.append(jnp.zeros((w, (n - 1 - k) * w), F32))
        rows.append(jnp.concatenate(parts, axis=-1))
    return jnp.concatenate(rows, axis=0)


def _rglru_gates(xc, wa, ba, wx, bx, lam):
    xb = xc.astype(BF16)
    r = _sigmoid(_dot(xb, wa) + ba)
    i = _sigmoid(_dot(xb, wx) + bx)
    softplus = jnp.maximum(-lam, 0.0) + jnp.log1p(jnp.exp(-jnp.abs(lam)))
    log_a = (-LRU_C) * r * softplus
    a = jnp.exp(log_a)
    b = jnp.sqrt(1.0 - a * a) * (i * xc)
    return a, b


def _block_prefix(a, b, reverse):
    t = a.shape[0]
    row = lax.broadcasted_iota(jnp.int32, a.shape, 0) & 7
    for d in (1, 2, 4):
        if reverse:
            a_s = pltpu.roll(a, t - d, 0)
            b_s = pltpu.roll(b, t - d, 0)
            ok = row < 8 - d
        else:
            a_s = pltpu.roll(a, d, 0)
            b_s = pltpu.roll(b, d, 0)
            ok = row >= d
        b = jnp.where(ok, a * b_s + b, b)
        a = jnp.where(ok, a * a_s, a)
    return a, b


def _conv4(x, w_ref, b_row):
    t = x.shape[0]
    row = lax.broadcasted_iota(jnp.int32, x.shape, 0)
    xm2 = jnp.where(row >= 2, pltpu.roll(x, 2, 0), 0.0)
    xm1 = jnp.where(row >= 1, pltpu.roll(x, 1, 0), 0.0)
    xp1 = jnp.where(row < t - 1, pltpu.roll(x, t - 1, 0), 0.0)
    return (xm2 * w_ref[0:1, :] + xm1 * w_ref[1:2, :] + x * w_ref[2:3, :] + xp1 * w_ref[3:4, :]) + b_row


def _rglru_prepare(cx, cy, conv_w_ref, conv_b, gate_w_ref, ba_ref, bx_ref, lam_ref,
                   af_ref, bf_ref, ab_ref, bb_ref, gel_ref):
    xc = _conv4(cx, conv_w_ref, conv_b)
    a, b = _rglru_gates(xc, gate_w_ref[0], ba_ref[0:1, :], gate_w_ref[1], bx_ref[0:1, :], lam_ref[0:1, :])
    a, b = _block_prefix(a, b, reverse=False)
    af_ref[...] = a
    bf_ref[...] = b
    a, b = _rglru_gates(xc, gate_w_ref[2], ba_ref[1:2, :], gate_w_ref[3], bx_ref[1:2, :], lam_ref[1:2, :])
    a, b = _block_prefix(a, b, reverse=True)
    ab_ref[...] = a
    bb_ref[...] = b
    gel_ref[...] = _gelu_tanh(cy)


def _rglru_finish(h0f, h0b, af_ref, bf_ref, ab_ref, bb_ref, hf_ref, hb_ref, gel_ref):
    nblk = af_ref.shape[0] // 8

    def body(k, carry):
        cf, cb = carry
        rf = pl.ds(pl.multiple_of(k * 8, 8), 8)
        hf = bf_ref[rf, :] + af_ref[rf, :] * cf
        hf_ref[rf, :] = hf
        rb = pl.ds(pl.multiple_of((nblk - 1 - k) * 8, 8), 8)
        hb = bb_ref[rb, :] + ab_ref[rb, :] * cb
        hb_ref[rb, :] = hb
        return hf[7:8, :], hb[0:1, :]

    cf, cb = lax.fori_loop(0, nblk, body, (h0f, h0b))
    oc = (hf_ref[...] + hb_ref[...]) * gel_ref[...]
    return oc, cf, cb


def _store_gate_weights(gate_w_ref, wa_ref, wx_ref):
    for d in range(2):
        gate_w_ref[2 * d] = _block_diag([wa_ref[d, n] for n in range(N_HEADS)]).astype(BF16)
        gate_w_ref[2 * d + 1] = _block_diag([wx_ref[d, n] for n in range(N_HEADS)]).astype(BF16)


def _lane_head_masks(n):
    lane = lax.broadcasted_iota(jnp.int32, (1, n), 1) >> HEAD_SHIFT
    return [jnp.where(lane == h, 1.0, 0.0) for h in range(n // HEAD_DIM)]


def _log_decays(theta_ref, masks):
    theta = theta_ref[...]
    lanes = theta[:, 0:1] * masks[0]
    for h in range(1, N_HEADS):
        lanes = lanes + theta[:, h:h + 1] * masks[h]
    lg = jnp.log1p(-jnp.exp(lanes))
    return lg[0:1, :], lg[1:2, :]


RET_BLOCK = 256


def _retention(q, k8, vb, s0, lgf, lgb, masks, o_ref):
    t, w = q.shape
    c = RET_BLOCK
    nh = w // HEAD_DIM
    pos = lax.broadcasted_iota(jnp.int32, (c, w), 0).astype(F32)
    q_dec = (jnp.exp(lgf * (pos + 1.0)), jnp.exp(lgb * (float(c) - pos)))
    k_dec = (jnp.exp(lgf * (float(c - 1) - pos)), jnp.exp(lgb * pos))
    chunk_dec = (jnp.exp(lgf * float(c)), jnp.exp(lgb * float(c)))
    rel = (lax.broadcasted_iota(jnp.int32, (c, c), 0) - lax.broadcasted_iota(jnp.int32, (c, c), 1)).astype(F32)
    decs = []
    for h in range(nh):
        gf = lgf[:, h * HEAD_DIM:h * HEAD_DIM + 1]
        gb = lgb[:, h * HEAD_DIM:h * HEAD_DIM + 1]
        e = jnp.exp(jnp.where(rel >= 0, gf * rel, gb * (-rel)))
        decs.append(jnp.where(rel == 0, 2.0, e))
    dec = jnp.concatenate(decs, axis=0)
    r_head = lax.broadcasted_iota(jnp.int32, (w, w), 0) >> HEAD_SHIFT
    c_head = lax.broadcasted_iota(jnp.int32, (w, w), 1) >> HEAD_SHIFT
    same_head = jnp.where(r_head == c_head, 1.0, 0.0)
    states = [None, None] if s0 is None else list(s0)

    def carry(d, rows, o):
        if states[d] is not None:
            o = o + _dot((q[rows, :] * q_dec[d]).astype(BF16), states[d].astype(BF16))
        upd = _dot_tn((k8[rows, :] * k_dec[d]).astype(BF16), vb[rows, :]) * same_head
        states[d] = upd if states[d] is None else states[d] * chunk_dec[d] + upd
        return o

    for ci in range(t // c):
        rows = slice(ci * c, (ci + 1) * c)
        qc = q[rows, :]
        q_stack = jnp.concatenate([(qc * masks[h]).astype(BF16) for h in range(nh)], axis=0)
        inner = (_dot_nt(q_stack, k8[rows, :].astype(BF16)) * dec).astype(BF16)
        out = _dot(inner, vb[rows, :])
        o = out[0:c, :] * masks[0]
        for h in range(1, nh):
            o = o + out[h * c:(h + 1) * c, :] * masks[h]
        o_ref[rows, :] = carry(0, rows, o)
    for ci in reversed(range(t // c)):
        rows = slice(ci * c, (ci + 1) * c)
        if states[1] is not None:
            o_ref[rows, :] = carry(1, rows, o_ref[rows, :])
        else:
            carry(1, rows, None)
    return states[0], states[1]


def _ctx_mixer_kernel(*refs, layer, n_prev):
    prev_refs = refs[:n_prev]
    (x_ref, n2_ref, sh_ref, sc_ref, g2_ref, win_ref, wout_ref,
     aqn_ref, akn_ref, bqn_ref, bkn_ref, sink_ref,
     convw_ref, convb_ref, wa_ref, ba_ref, wx_ref, bx_ref, lam_ref, theta_ref, dn_ref,
     xn_ref, *state_refs) = refs[n_prev:n_prev + 28]
    (win_s, wout_s, u_ref, mixed_ref,
     gate_w_ref, af_ref, bf_ref, ab_ref, bb_ref, hf_ref, hb_ref, gel_ref, ret_ref) = refs[n_prev + 28:]
    t = SEQ
    lrow = slice(layer, layer + 1)
    for prev_ref, state_ref in zip(prev_refs, state_refs):
        for earlier in range(layer):
            state_ref[earlier] = prev_ref[earlier]
    ka_ref, va_ref, kb_ref, vb_ref, stc_ref, std_ref = (ref.at[layer] for ref in state_refs)

    @pl.when(pl.program_id(0) == 0)
    def _():
        for c in range(IN_WIDTH // 512):
            win_s[:, c * 512:(c + 1) * 512] = win_ref[:, c * 512:(c + 1) * 512].astype(BF16)
        wout_s[...] = wout_ref[...].astype(BF16)
        _store_gate_weights(gate_w_ref, wa_ref, wx_ref)

    x = x_ref[...]
    h = _norm_mod(x, n2_ref[lrow, :], sc_ref[0:1, :], sh_ref[0:1, :]).astype(BF16)
    c_cols = slice(COL_CX, COL_CX + 2 * GROUP_W)
    u_ref[:, c_cols] = _dot(h, win_s[:, c_cols])
    _rglru_prepare(u_ref[:, COL_CX:COL_CX + GROUP_W], u_ref[:, COL_CY:COL_CY + GROUP_W],
                   convw_ref, convb_ref[lrow, :], gate_w_ref, ba_ref, bx_ref, lam_ref,
                   af_ref, bf_ref, ab_ref, bb_ref, gel_ref)
    u_ref[:, 0:COL_CX] = _dot(h, win_s[:, 0:COL_CX])
    u_ref[:, COL_DQ:IN_WIDTH] = _dot(h, win_s[:, COL_DQ:IN_WIDTH])

    for (cq, ck, cv, qn_ref, kn_ref, k_out, v_out, col0, use_sink) in (
            (COL_AQ, COL_AK, COL_AV, aqn_ref, akn_ref, ka_ref, va_ref, 0, True),
            (COL_BQ, COL_BK, COL_BV, bqn_ref, bkn_ref, kb_ref, vb_ref, GROUP_W, False)):
        q = _head_norm(u_ref[:, cq:cq + 256], qn_ref[lrow, :])
        k = _head_norm(u_ref[:, ck:ck + 128], kn_ref[lrow, :])
        v = u_ref[:, cv:cv + 128]
        k_out[...] = k
        v_out[...] = v
        qs = q * (HEAD_DIM ** -0.5)
        heads = []
        for hd in range(N_HEADS):
            kv = hd // 2
            s = _dot_nt(_head_cols(qs, hd), _head_cols(k, kv))
            sink = jnp.full((t, 1), sink_ref[layer, hd], F32) if use_sink else None
            heads.append(_softmax_pv([s], [_head_cols(v, kv)], sink))
        mixed_ref[:, col0:col0 + GROUP_W] = jnp.concatenate(heads, axis=-1).astype(BF16)

    zero = jnp.zeros((1, GROUP_W), F32)
    oc, cf, cb = _rglru_finish(zero, zero, af_ref, bf_ref, ab_ref, bb_ref, hf_ref, hb_ref, gel_ref)
    mixed_ref[:, 2 * GROUP_W:3 * GROUP_W] = oc.astype(BF16)
    stc_ref[0:1, :] = cf
    stc_ref[1:2, :] = cb

    masks = _lane_head_masks(GROUP_W)
    lgf, lgb = _log_decays(theta_ref, masks)
    k8 = u_ref[:, COL_DK:COL_DK + GROUP_W] * (HEAD_DIM ** -0.5)
    vb = u_ref[:, COL_DV:COL_DV + GROUP_W].astype(BF16)
    final_states = _retention(u_ref[:, COL_DQ:COL_DQ + GROUP_W], k8, vb, None, lgf, lgb, masks, ret_ref)
    o = ret_ref[...]
    o = o * lax.rsqrt(_head_mean_square(o) + EPS) * dn_ref[lrow, :] * _silu(u_ref[:, COL_DG:COL_DG + GROUP_W])
    mixed_ref[:, 3 * GROUP_W:4 * GROUP_W] = o.astype(BF16)
    for d, s_full in enumerate(final_states):
        for hd in range(N_HEADS):
            std_ref[d, hd] = s_full[hd * 64:(hd + 1) * 64, hd * 64:(hd + 1) * 64]

    xn_ref[...] = x + g2_ref[0:1, :] * _dot(mixed_ref[...], wout_s[...])


def _ctx_mixers(x, mod, layer, prev, norm2_g, w_in, w_out,
                a_qn, a_kn, a_sink, b_qn, b_kn, c_conv_w, c_conv_b, c_wa, c_ba, c_wx, c_bx,
                c_lambda, d_theta, d_norm_g):
    per_request = lambda slots, shape: pl.BlockSpec((None, slots) + shape, lambda b: (b,) + (0,) * (1 + len(shape)))
    state_dims = [(SEQ, 128)] * 4 + [(2, GROUP_W), (2, N_HEADS, HEAD_DIM, HEAD_DIM)]
    scr = pltpu.VMEM((SEQ, GROUP_W), F32)
    out = pl.pallas_call(
        functools.partial(_ctx_mixer_kernel, layer=layer, n_prev=len(prev)),
        grid=(BATCH,),
        in_specs=[per_request(layer, dims) for dims in state_dims[:len(prev)]] + [
            pl.BlockSpec((SEQ, D_MODEL), lambda b: (b, 0)),
            _full((DEPTH, D_MODEL)),
            _mod_chunk(layer, 3), _mod_chunk(layer, 4), _mod_chunk(layer, 5),
            _once((None, D_MODEL, IN_WIDTH), lambda b: (layer, 0, 0)),
            _once((None, D_MODEL, D_MODEL), lambda b: (layer, 0, 0)),
            _full((DEPTH, HEAD_DIM)), _full((DEPTH, HEAD_DIM)), _full((DEPTH, HEAD_DIM)), _full((DEPTH, HEAD_DIM)),
            pl.BlockSpec(memory_space=pltpu.SMEM),
            _layer_block((4, GROUP_W), layer), _full((DEPTH, GROUP_W)),
            _layer_block((2, N_HEADS, HEAD_DIM, HEAD_DIM), layer), _layer_block((2, GROUP_W), layer),
            _layer_block((2, N_HEADS, HEAD_DIM, HEAD_DIM), layer), _layer_block((2, GROUP_W), layer),
            _layer_block((2, GROUP_W), layer),
            _layer_block((2, N_HEADS), layer), _full((DEPTH, GROUP_W)),
        ],
        out_specs=[pl.BlockSpec((SEQ, D_MODEL), lambda b: (b, 0))] + [
            per_request(layer + 1, dims) for dims in state_dims],
        out_shape=[jax.ShapeDtypeStruct((N_ROWS, D_MODEL), F32)] + [
            jax.ShapeDtypeStruct((BATCH, layer + 1) + dims, F32) for dims in state_dims],
        input_output_aliases={len(prev): 0},
        scratch_shapes=[pltpu.VMEM((D_MODEL, IN_WIDTH), BF16), pltpu.VMEM((D_MODEL, D_MODEL), BF16),
                        pltpu.VMEM((SEQ, IN_WIDTH), F32), pltpu.VMEM((SEQ, D_MODEL), BF16),
                        pltpu.VMEM((4, GROUP_W, GROUP_W), BF16)] + [scr] * 8,
        compiler_params=_cparams("arbitrary"),
        name="ctx_mixers",
    )(*prev, x, norm2_g, mod, mod, mod, w_in, w_out,
      a_qn, a_kn, b_qn, b_kn, a_sink, c_conv_w, c_conv_b, c_wa, c_ba, c_wx, c_bx,
      c_lambda, d_theta, d_norm_g)
    return out[0], tuple(out[1:])


LAT_BLOCK0 = N_CTX_ROWS // DEC_SEQ


def _lat_attn_kernel(x_ref, n2_ref, sh_ref, sc_ref, g2_ref, win_ref, wout_ref,
                     kca_ref, vca_ref, kcb_ref, vcb_ref,
                     aqn_ref, akn_ref, bqn_ref, bkn_ref, sink_ref, cos_ref, sinl_ref, sinh_ref,
                     xn_ref, h_ref, u_ref, o_ref, *, layer):
    t = DEC_SEQ
    lrow = slice(layer, layer + 1)
    mrow = pl.ds(1 + pl.program_id(0), 1)
    cos, sin_lo, sin_hi = cos_ref[...], sinl_ref[...], sinh_ref[...]
    scale = HEAD_DIM ** -0.5
    x = x_ref[...]
    h_ref[...] = _norm_mod(x, n2_ref[lrow, :], sc_ref[mrow, :], sh_ref[mrow, :]).astype(BF16)
    u_ref[...] = _dot(h_ref[...], win_ref[...].astype(BF16))

    q = _rope(_head_norm(u_ref[:, COL_AQ:COL_AQ + 256], aqn_ref[lrow, :]), cos, sin_lo, sin_hi)
    k = _rope(_head_norm(u_ref[:, COL_AK:COL_AK + 128], akn_ref[lrow, :]), cos, sin_lo, sin_hi)
    qh = [_head_cols(q * scale, h) for h in range(4)]
    v = u_ref[:, COL_AV:COL_AV + 128]
    kh = [_head_cols(k, kv) for kv in range(2)]
    vh = [_head_cols(v, kv) for kv in range(2)]
    kch = [_head_cols(kca_ref[...], kv) for kv in range(2)]
    vch = [_head_cols(vca_ref[...], kv) for kv in range(2)]
    w = ATT_BLOCK
    span = 3 * w
    for n in range(t // w):
        start = min(max((n - 1) * w, 0), t - span)
        rows = slice(n * w, (n + 1) * w)
        band = slice(start, start + span)
        qpos = (lax.broadcasted_iota(jnp.int32, (2 * w, span), 0) & (w - 1)) + n * w
        kpos = lax.broadcasted_iota(jnp.int32, (2 * w, span), 1) + start
        valid = jnp.abs(qpos - kpos) <= WINDOW
        heads = []
        for kv in range(2):
            qp = jnp.concatenate([qh[2 * kv][rows, :], qh[2 * kv + 1][rows, :]], axis=0)
            s_ctx = _dot_nt(qp, kch[kv])
            s_band = jnp.where(valid, _dot_nt(qp, kh[kv][band, :]), NEG_INF)
            row = lax.broadcasted_iota(jnp.int32, (2 * w, 1), 0)
            sink = jnp.where(row < w, sink_ref[layer, 2 * kv], sink_ref[layer, 2 * kv + 1])
            o = _softmax_pv([s_ctx, s_band], [vch[kv], vh[kv][band, :]], sink)
            heads += [o[0:w, :], o[w:2 * w, :]]
        o_ref[rows, 0:GROUP_W] = jnp.concatenate(heads, axis=-1).astype(BF16)

    q = _rope(_head_norm(u_ref[:, COL_BQ:COL_BQ + 256], bqn_ref[lrow, :]), cos, sin_lo, sin_hi)
    k = _rope(_head_norm(u_ref[:, COL_BK:COL_BK + 128], bkn_ref[lrow, :]), cos, sin_lo, sin_hi)
    qh = [_head_cols(q * scale, h) for h in range(4)]
    v = u_ref[:, COL_BV:COL_BV + 128]
    kh = [_head_cols(k, kv) for kv in range(2)]
    vh = [_head_cols(v, kv) for kv in range(2)]
    kch = [_head_cols(kcb_ref[...], kv) for kv in range(2)]
    vch = [_head_cols(vcb_ref[...], kv) for kv in range(2)]
    tq = 256
    for n in range(t // tq):
        rows = slice(n * tq, (n + 1) * tq)
        heads = []
        for kv in range(2):
            qp = jnp.concatenate([qh[2 * kv][rows, :], qh[2 * kv + 1][rows, :]], axis=0)
            o = _softmax_pv([_dot_nt(qp, kch[kv]), _dot_nt(qp, kh[kv])], [vch[kv], vh[kv]], None)
            heads += [o[0:tq, :], o[tq:2 * tq, :]]
        o_ref[rows, GROUP_W:2 * GROUP_W] = jnp.concatenate(heads, axis=-1).astype(BF16)

    xn_ref[...] = x + g2_ref[mrow, :] * _dot(o_ref[...], wout_ref[...].astype(BF16))


def _lat_recurrent_kernel(xn_in_ref, h_ref, g2_ref, wc_ref, wqk_ref, wvg_ref, wout_ref, h0_ref,
                          convw_ref, convb_ref, wa_ref, ba_ref, wx_ref, bx_ref, lam_ref,
                          s0_ref, theta_ref, dn_ref,
                          xn_ref, gate_w_ref, af_ref, bf_ref, ab_ref, bb_ref, hf_ref, hb_ref, gel_ref, ret_ref,
                          *, layer):
    lrow = slice(layer, layer + 1)
    mrow = pl.ds(1 + pl.program_id(0), 1)

    @pl.when(pl.program_id(0) == 0)
    def _():
        _store_gate_weights(gate_w_ref, wa_ref, wx_ref)

    h = h_ref[...]
    u = _dot(h, wc_ref[...].astype(BF16))
    _rglru_prepare(u[:, 0:GROUP_W], u[:, GROUP_W:2 * GROUP_W], convw_ref, convb_ref[lrow, :],
                   gate_w_ref, ba_ref, bx_ref, lam_ref, af_ref, bf_ref, ab_ref, bb_ref, gel_ref)
    oc, _, _ = _rglru_finish(h0_ref[0:1, :], h0_ref[1:2, :],
                             af_ref, bf_ref, ab_ref, bb_ref, hf_ref, hb_ref, gel_ref)
    y = _dot(oc.astype(BF16), wout_ref[0:GROUP_W, :].astype(BF16))

    uqk = _dot(h, wqk_ref[...].astype(BF16))
    uvg = _dot(h, wvg_ref[...].astype(BF16))
    masks = _lane_head_masks(GROUP_W)
    lgf, lgb = _log_decays(theta_ref, masks)
    s0 = tuple(_block_diag([s0_ref[d, hd] for hd in range(N_HEADS)]) for d in range(2))
    _retention(uqk[:, 0:GROUP_W], uqk[:, GROUP_W:2 * GROUP_W] * (HEAD_DIM ** -0.5),
               uvg[:, 0:GROUP_W].astype(BF16), s0, lgf, lgb, masks, ret_ref)
    o = ret_ref[...]
    o = o * lax.rsqrt(_head_mean_square(o) + EPS) * dn_ref[lrow, :] * _silu(uvg[:, GROUP_W:2 * GROUP_W])
    y = y + _dot(o.astype(BF16), wout_ref[GROUP_W:2 * GROUP_W, :].astype(BF16))
    xn_ref[...] = xn_in_ref[...] + g2_ref[mrow, :] * y


def _lat_mixers(x, mod, layer, caches, state_c, state_d, rope, norm2_g, w_in, w_out,
                a_qn, a_kn, a_sink, b_qn, b_kn, c_conv_w, c_conv_b, c_wa, c_ba, c_wx, c_bx,
                c_lambda, d_theta, d_norm_g):
    rows = pl.BlockSpec((DEC_SEQ, D_MODEL), lambda b: (LAT_BLOCK0 + b, 0))
    h_rows = pl.BlockSpec((DEC_SEQ, D_MODEL), lambda b: (b, 0))
    cache_spec = pl.BlockSpec((None, None, PAST_LEN, 128), lambda b: (b, layer, 0, 0))
    gain = _full((DEPTH, HEAD_DIM))
    table = _once((DEC_SEQ, 128), lambda b: (0, 0))
    out_shape = jax.ShapeDtypeStruct((N_ROWS, D_MODEL), F32)
    win_cols = lambda w, c: _once((None, D_MODEL, w), lambda b: (layer, 0, c))
    wout_rows = lambda h, r: _once((None, h, D_MODEL), lambda b: (layer, r, 0))

    xn, h = pl.pallas_call(
        functools.partial(_lat_attn_kernel, layer=layer),
        grid=(DEC_BATCH,),
        in_specs=[rows, _full((DEPTH, D_MODEL)),
                  _mod_chunk(layer, 3), _mod_chunk(layer, 4), _mod_chunk(layer, 5),
                  win_cols(4 * GROUP_W, 0), wout_rows(2 * GROUP_W, 0),
                  cache_spec, cache_spec, cache_spec, cache_spec,
                  gain, gain, gain, gain,
                  pl.BlockSpec(memory_space=pltpu.SMEM),
                  table, table, table],
        out_specs=[rows, pl.BlockSpec((DEC_SEQ, D_MODEL), lambda b: (b, 0), pipeline_mode=pl.Buffered(1))],
        out_shape=[out_shape, jax.ShapeDtypeStruct((N_LAT_ROWS, D_MODEL), BF16)],
        input_output_aliases={0: 0},
        scratch_shapes=[pltpu.VMEM((DEC_SEQ, 4 * GROUP_W), F32), pltpu.VMEM((DEC_SEQ, 2 * GROUP_W), BF16)],
        compiler_params=_cparams("arbitrary"),
        name="lat_attention",
    )(x, norm2_g, mod, mod, mod, w_in, w_out, *caches, a_qn, a_kn, b_qn, b_kn, a_sink, *rope)

    scr = pltpu.VMEM((DEC_SEQ, GROUP_W), F32)
    xn = pl.pallas_call(
        functools.partial(_lat_recurrent_kernel, layer=layer),
        grid=(DEC_BATCH,),
        in_specs=[
            rows, h_rows, _mod_chunk(layer, 5),
            win_cols(2 * GROUP_W, COL_CX // (2 * GROUP_W)),
            win_cols(2 * GROUP_W, COL_DQ // (2 * GROUP_W)), win_cols(2 * GROUP_W, COL_DV // (2 * GROUP_W)),
            wout_rows(2 * GROUP_W, 1),
            pl.BlockSpec((None, None, 2, GROUP_W), lambda b: (b, layer, 0, 0)),
            _layer_block((4, GROUP_W), layer), _full((DEPTH, GROUP_W)),
            _layer_block((2, N_HEADS, HEAD_DIM, HEAD_DIM), layer), _layer_block((2, GROUP_W), layer),
            _layer_block((2, N_HEADS, HEAD_DIM, HEAD_DIM), layer), _layer_block((2, GROUP_W), layer),
            _layer_block((2, GROUP_W), layer),
            pl.BlockSpec((None, None, 2, N_HEADS, HEAD_DIM, HEAD_DIM), lambda b: (b, layer, 0, 0, 0, 0)),
            _layer_block((2, N_HEADS), layer), _full((DEPTH, GROUP_W))],
        out_specs=rows,
        out_shape=out_shape,
        input_output_aliases={0: 0},
        scratch_shapes=[pltpu.VMEM((4, GROUP_W, GROUP_W), BF16)] + [scr] * 8,
        compiler_params=_cparams("arbitrary"),
        name="lat_recurrent",
    )(xn, h, mod, w_in, w_in, w_in, w_out, state_c, c_conv_w, c_conv_b, c_wa, c_ba, c_wx, c_bx, c_lambda,
      state_d, d_theta, d_norm_g)
    return xn


def _rope_tables():
    t = np.arange(DEC_SEQ)
    row = (t // GRID_W).astype(np.float64)[:, None]
    col = (t % GRID_W).astype(np.float64)[:, None]
    half = HEAD_DIM // 2
    inv = 1.0 / (ROPE_BASE ** (np.arange(0, half, 2, dtype=np.float64) / half))
    j = np.arange(128) % HEAD_DIM
    ang = np.where((j < half)[None, :], row, col) * inv[j % (half // 2)][None, :]
    first = ((j % half) < half // 2)[None, :]
    cos, sin = np.cos(ang), np.sin(ang)
    return tuple(jnp.asarray(a, F32) for a in (cos, np.where(first, -sin, 0.0), np.where(first, 0.0, sin)))


def kernel(x_prompt, x_sample, cache_a_k, cache_a_v, cache_b_k, cache_b_v, state_c, state_d, c, c_ctx, norm1_g, norm2_g, norm3_g, w_mod, b_mod, ffn1_wg, ffn1_wu, ffn1_wd, ffn2_wg, ffn2_wu, ffn2_wd, w_in, w_out, a_qn, a_kn, a_sink, b_qn, b_kn, c_conv_w, c_conv_b, c_wa, c_ba, c_wx, c_bx, c_lambda, d_theta, d_norm_g):
    mod = _modulation(c_ctx, c, w_mod, b_mod)
    rope = _rope_tables()
    caches = tuple(t.reshape(DEC_BATCH, DEPTH, PAST_LEN, 128) for t in (cache_a_k, cache_a_v, cache_b_k, cache_b_v))
    mixer_params = (a_qn, a_kn, a_sink, b_qn, b_kn, c_conv_w, c_conv_b, c_wa, c_ba, c_wx, c_bx,
                    c_lambda, d_theta, d_norm_g)
    xs = (x_prompt.reshape(N_CTX_ROWS, D_MODEL), x_sample.reshape(N_LAT_ROWS, D_MODEL))
    states = ()
    for l in range(DEPTH):
        (x,) = _ffn(xs, mod, l, 0, norm1_g, ffn1_wg, ffn1_wu, ffn1_wd)
        x, states = _ctx_mixers(x, mod, l, states, norm2_g, w_in, w_out, *mixer_params)
        x = _lat_mixers(x, mod, l, caches, state_c, state_d, rope, norm2_g, w_in, w_out, *mixer_params)
        xs = _ffn((x,), mod, l, 6, norm3_g, ffn2_wg, ffn2_wu, ffn2_wd, split_out=(l == DEPTH - 1))
    y_p, y_s = xs
    ka, va, kb, vb, st_c, st_d = states
    kv_shape = (BATCH, DEPTH, SEQ, 2, HEAD_DIM)
    return (y_p.reshape(BATCH, SEQ, D_MODEL), y_s.reshape(DEC_BATCH, DEC_SEQ, D_MODEL),
            ka.reshape(kv_shape), va.reshape(kv_shape), kb.reshape(kv_shape), vb.reshape(kv_shape),
            st_c, st_d)
```

```python
import functools
import math

import numpy as np
import jax
import jax.numpy as jnp
from jax import lax
from jax.experimental import pallas as pl
from jax.experimental.pallas import tpu as pltpu

F32 = jnp.float32
BF16 = jnp.bfloat16

D_MODEL = 1024
BATCH = 16
SEQ = 256
DEPTH = 2
DEC_BATCH = 2
DEC_SEQ = 1024
PAST_LEN = 512
GRID_W = 64
HEAD_DIM = 64
HEAD_SHIFT = 6
N_HEADS = 4
GROUP_W = 256
WINDOW = 128
ATT_BLOCK = 128
ROPE_BASE = 10000.0
LRU_C = 8.0
D_FF = 2816
N_MOD = 9
EPS = 1e-6
NEG_INF = -1e30
IN_WIDTH = 2560

N_CTX_ROWS = BATCH * SEQ
N_LAT_ROWS = DEC_BATCH * DEC_SEQ
N_ROWS = N_CTX_ROWS + N_LAT_ROWS
MOD_ROWS = 8
MOD_GROUP = 1024

VMEM_LIMIT_BYTES = 56 * 1024 * 1024

COL_AQ, COL_AK, COL_AV = 0, 256, 384
COL_BQ, COL_BK, COL_BV = 512, 768, 896
COL_CX, COL_CY = 1024, 1280
COL_DQ, COL_DK, COL_DV, COL_DG = 1536, 1792, 2048, 2304


def _cparams(*sem):
    return pltpu.CompilerParams(dimension_semantics=sem, vmem_limit_bytes=VMEM_LIMIT_BYTES)


def _dot(a, b):
    return jnp.dot(a, b, preferred_element_type=F32)


def _dot_nt(a, b):
    return lax.dot_general(a, b, (((1,), (1,)), ((), ())), preferred_element_type=F32)


def _dot_tn(a, b):
    return lax.dot_general(a, b, (((0,), (0,)), ((), ())), preferred_element_type=F32)


def _sigmoid(x):
    return 0.5 * jnp.tanh(0.5 * x) + 0.5


def _silu(x):
    return x * _sigmoid(x)


def _gelu_tanh(x):
    return 0.5 * x * (1.0 + jnp.tanh(math.sqrt(2.0 / math.pi) * (x + 0.044715 * (x * x * x))))


def _mod_row(i, tm, s):
    if tm >= MOD_GROUP:
        block_index = i * (tm // MOD_GROUP) + s
    else:
        block_index = i >> int(math.log2(MOD_GROUP // tm))
    return jnp.maximum(block_index - (N_CTX_ROWS // MOD_GROUP - 1), 0)


def _norm_mod(x, g, sc, sh):
    ms = jnp.mean(x * x, axis=-1, keepdims=True)
    return (x * lax.rsqrt(ms + EPS) * g) * (1.0 + sc) + sh


def _full(shape):
    return pl.BlockSpec(shape, lambda *_: (0,) * len(shape))


def _layer_block(shape, layer):
    return pl.BlockSpec((None,) + shape, lambda *_: (layer,) + (0,) * len(shape))


MOD_TN = 3072


def _mod_kernel(cc_ref, c_ref, w_ref, b_ref, o_ref):
    l = pl.program_id(0)
    pad = jnp.zeros((MOD_ROWS - 1 - DEC_BATCH, D_MODEL), F32)
    cond = jnp.concatenate([cc_ref[...], c_ref[...], pad], axis=0)
    o_ref[...] = _dot(_silu(cond).astype(BF16), w_ref[...].astype(BF16)) + b_ref[pl.ds(l, 1), :]


def _modulation(c_ctx, c, w_mod, b_mod):
    n = N_MOD * D_MODEL
    return pl.pallas_call(
        _mod_kernel,
        grid=(DEPTH, n // MOD_TN),
        in_specs=[
            pl.BlockSpec((1, D_MODEL), lambda l, j: (0, 0)),
            pl.BlockSpec((DEC_BATCH, D_MODEL), lambda l, j: (0, 0)),
            pl.BlockSpec((None, D_MODEL, MOD_TN), lambda l, j: (l, 0, j)),
            pl.BlockSpec((DEPTH, MOD_TN), lambda l, j: (0, j)),
        ],
        out_specs=pl.BlockSpec((None, MOD_ROWS, MOD_TN), lambda l, j: (l, 0, j)),
        out_shape=jax.ShapeDtypeStruct((DEPTH, MOD_ROWS, n), F32),
        compiler_params=_cparams("arbitrary", "arbitrary"),
        name="modulation",
    )(c_ctx.reshape(1, D_MODEL), c, w_mod, b_mod)


FFN_TM = 1024
FFN_TF = 256
N_CTX_TILES = N_CTX_ROWS // FFN_TM


FFN_NJ = D_FF // FFN_TF
N_FFN_TILES = N_ROWS // FFN_TM
N_FFN_STEPS = FFN_NJ + N_FFN_TILES


def _ffn_tile(step):
    return jnp.maximum(step - FFN_NJ, 0)


def _on_stream_part(tile, x_refs, o_refs, fn):
    if len(x_refs) == 1 and len(o_refs) == 1:
        fn(x_refs[0], o_refs[0])
    else:
        pl.when(tile < N_CTX_TILES)(lambda: fn(x_refs[0], o_refs[0]))
        pl.when(tile >= N_CTX_TILES)(lambda: fn(x_refs[-1], o_refs[-1]))


def _ffn_kernel(*refs, layer, n_in, n_out):
    x_refs = refs[:n_in]
    n_ref, sh_ref, sc_ref, g_ref, wg_ref, wu_ref, wd_ref = refs[n_in:n_in + 7]
    o_refs = refs[n_in + 7:n_in + 7 + n_out]
    h_ref, a_ref, wg_s, wu_s, wd_s = refs[n_in + 7 + n_out:]
    nj, tf = FFN_NJ, FFN_TF
    s = pl.program_id(0)
    tile = _ffn_tile(s)
    r = _mod_row(tile, FFN_TM, 0)

    def load_tile():
        def init(x_ref, _):
            h = _norm_mod(x_ref[...], n_ref[layer:layer + 1, :], sc_ref[pl.ds(r, 1), :], sh_ref[pl.ds(r, 1), :])
            h_ref[...] = h.astype(BF16)
        _on_stream_part(tile, x_refs, o_refs, init)

    def up_chunk(j, cols):
        h = h_ref[...]
        a_ref[:, cols] = (_silu(_dot(h, wg_s[j])) * _dot(h, wu_s[j])).astype(BF16)

    def down_and_store():
        y = (0.5 * g_ref[pl.ds(r, 1), :]) * _dot(a_ref[...], wd_s[...])

        def store(x_ref, o_ref):
            o_ref[...] = x_ref[...] + y
        _on_stream_part(tile, x_refs, o_refs, store)

    def keep_arrived_chunk():
        wg_s[s] = wg_ref[...].astype(BF16)
        wu_s[s] = wu_ref[...].astype(BF16)
        wd_s[pl.ds(pl.multiple_of(s * tf, tf), tf), :] = wd_ref[...].astype(BF16)

    def up_previous_chunk():
        up_chunk(s - 1, pl.ds(pl.multiple_of((s - 1) * tf, tf), tf))

    @pl.when(s == 0)
    def _():
        load_tile()
        keep_arrived_chunk()

    @pl.when((s > 0) & (s < nj))
    def _():
        up_previous_chunk()
        keep_arrived_chunk()

    @pl.when(s == nj)
    def _():
        up_previous_chunk()
        down_and_store()

    @pl.when(s > nj)
    def _():
        load_tile()
        for j in range(nj):
            up_chunk(j, slice(j * tf, (j + 1) * tf))
        down_and_store()


def _stream_specs(split, buffered_once):
    tm = FFN_TM
    kw = {"pipeline_mode": pl.Buffered(1)} if buffered_once else {}
    if not split:
        return [pl.BlockSpec((tm, D_MODEL), lambda s: (_ffn_tile(s), 0), **kw)]
    last_ctx = N_CTX_TILES - 1
    return [pl.BlockSpec((tm, D_MODEL), lambda s: (jnp.minimum(_ffn_tile(s), last_ctx), 0), **kw),
            pl.BlockSpec((tm, D_MODEL), lambda s: (jnp.maximum(_ffn_tile(s) - N_CTX_TILES, 0), 0), **kw)]


def _ffn(xs, mod, layer, chunk0, norm_g, wg, wu, wd, split_out=False):
    tm, tf, nj = FFN_TM, FFN_TF, FFN_NJ
    split_in = len(xs) == 2
    mod_spec = lambda c: pl.BlockSpec((None, MOD_ROWS, D_MODEL), lambda s: (layer, 0, c))
    w_col = lambda s: (layer, 0, jnp.minimum(s, nj - 1))
    w_row = lambda s: (layer, jnp.minimum(s, nj - 1), 0)
    if split_out:
        out_shape = [jax.ShapeDtypeStruct((N_CTX_ROWS, D_MODEL), F32),
                     jax.ShapeDtypeStruct((N_LAT_ROWS, D_MODEL), F32)]
    else:
        out_shape = [jax.ShapeDtypeStruct((N_ROWS, D_MODEL), F32)]
    out = pl.pallas_call(
        functools.partial(_ffn_kernel, layer=layer, n_in=len(xs), n_out=len(out_shape)),
        grid=(N_FFN_STEPS,),
        in_specs=_stream_specs(split_in, False) + [
            _full((DEPTH, D_MODEL)),
            mod_spec(chunk0), mod_spec(chunk0 + 1), mod_spec(chunk0 + 2),
            pl.BlockSpec((None, D_MODEL, tf), w_col),
            pl.BlockSpec((None, D_MODEL, tf), w_col),
            pl.BlockSpec((None, tf, D_MODEL), w_row),
        ],
        out_specs=_stream_specs(split_out, True),
        out_shape=out_shape,
        scratch_shapes=[pltpu.VMEM((tm, D_MODEL), BF16),
                        pltpu.VMEM((tm, D_FF), BF16),
                        pltpu.VMEM((nj, D_MODEL, tf), BF16),
                        pltpu.VMEM((nj, D_MODEL, tf), BF16),
                        pltpu.VMEM((D_FF, D_MODEL), BF16)],
        compiler_params=_cparams("arbitrary"),
        name="ffn",
    )(*xs, norm_g, mod, mod, mod, wg, wu, wd)
    return tuple(out)


def _once(shape, index_map):
    return pl.BlockSpec(shape, index_map, pipeline_mode=pl.Buffered(1))


def _mod_chunk(layer, c):
    return pl.BlockSpec((None, MOD_ROWS, D_MODEL), lambda *_: (layer, 0, c))


def _head_mean_square(x):
    n = x.shape[-1]
    r = lax.broadcasted_iota(jnp.int32, (n, n), 0) >> HEAD_SHIFT
    c = lax.broadcasted_iota(jnp.int32, (n, n), 1) >> HEAD_SHIFT
    ones_bd = jnp.where(r == c, 1.0, 0.0).astype(BF16)
    return _dot((x * x).astype(BF16), ones_bd) * (1.0 / HEAD_DIM)


def _head_norm(x, head_gain):
    gain_row = jnp.concatenate([head_gain] * (x.shape[-1] // HEAD_DIM), axis=-1)
    return x * lax.rsqrt(_head_mean_square(x) + EPS) * gain_row


def _head_cols(x, h):
    return x[:, h * HEAD_DIM:(h + 1) * HEAD_DIM].astype(BF16)


def _softmax_pv(scores, values, sink):
    m = jnp.max(scores[0], axis=-1, keepdims=True)
    for s in scores[1:]:
        m = jnp.maximum(m, jnp.max(s, axis=-1, keepdims=True))
    if sink is not None:
        m = jnp.maximum(m, sink)
    denom = None
    acc = None
    for s, v in zip(scores, values):
        p = jnp.exp(s - m)
        d = jnp.sum(p, axis=-1, keepdims=True)
        o = _dot(p.astype(BF16), v)
        denom = d if denom is None else denom + d
        acc = o if acc is None else acc + o
    if sink is not None:
        denom = denom + jnp.exp(sink - m)
    return acc / denom


def _rope(x, cos, sin_lo, sin_hi):
    cols = []
    for c in range(x.shape[-1] // 128):
        xc = x[:, c * 128:(c + 1) * 128]
        cols.append(xc * cos + pltpu.roll(xc, 112, 1) * sin_lo + pltpu.roll(xc, 16, 1) * sin_hi)
    return cols[0] if len(cols) == 1 else jnp.concatenate(cols, axis=-1)


def _block_diag(blocks):
    n = len(blocks)
    w = blocks[0].shape[0]
    rows = []
    for k, blk in enumerate(blocks):
        parts = []
        if k > 0:
            parts.append(jnp.zeros((w, k * w), F32))
        parts.append(blk)
        if k < n - 1:
            parts.append(jnp.zeros((w, (n - 1 - k) * w), F32))
        rows.append(jnp.concatenate(parts, axis=-1))
    return jnp.concatenate(rows, axis=0)


def _rglru_gates(xc, wa, ba, wx, bx, lam):
    xb = xc.astype(BF16)
    r = _sigmoid(_dot(xb, wa) + ba)
    i = _sigmoid(_dot(xb, wx) + bx)
    softplus = jnp.maximum(-lam, 0.0) + jnp.log1p(jnp.exp(-jnp.abs(lam)))
    log_a = (-LRU_C) * r * softplus
    a = jnp.exp(log_a)
    y = 1.0 - a * a
    b = jnp.where(y > 0.0, y * lax.rsqrt(y), 0.0) * (i * xc)
    return a, b


def _block_prefix(a, b, reverse):
    t, w = a.shape
    a = a.reshape(t // 8, 8, w)
    b = b.reshape(t // 8, 8, w)
    row = lax.broadcasted_iota(jnp.int32, a.shape, 1)
    for d in (1, 2, 4):
        if reverse:
            a_s = pltpu.roll(a, 8 - d, 1)
            b_s = pltpu.roll(b, 8 - d, 1)
            ok = row < 8 - d
        else:
            a_s = pltpu.roll(a, d, 1)
            b_s = pltpu.roll(b, d, 1)
            ok = row >= d
        b = jnp.where(ok, a * b_s + b, b)
        a = jnp.where(ok, a * a_s, a)
    return a.reshape(t, w), b.reshape(t, w)


def _conv4(x, w_ref, b_row):
    t = x.shape[0]
    row = lax.broadcasted_iota(jnp.int32, x.shape, 0)
    xm2 = jnp.where(row >= 2, pltpu.roll(x, 2, 0), 0.0)
    xm1 = jnp.where(row >= 1, pltpu.roll(x, 1, 0), 0.0)
    xp1 = jnp.where(row < t - 1, pltpu.roll(x, t - 1, 0), 0.0)
    return (xm2 * w_ref[0:1, :] + xm1 * w_ref[1:2, :] + x * w_ref[2:3, :] + xp1 * w_ref[3:4, :]) + b_row


def _rglru_prepare(cx, cy, conv_w_ref, conv_b, gate_w_ref, ba_ref, bx_ref, lam_ref,
                   af_ref, bf_ref, ab_ref, bb_ref, gel_ref):
    xc = _conv4(cx, conv_w_ref, conv_b)
    a, b = _rglru_gates(xc, gate_w_ref[0], ba_ref[0:1, :], gate_w_ref[1], bx_ref[0:1, :], lam_ref[0:1, :])
    a, b = _block_prefix(a, b, reverse=False)
    af_ref[...] = a
    bf_ref[...] = b
    a, b = _rglru_gates(xc, gate_w_ref[2], ba_ref[1:2, :], gate_w_ref[3], bx_ref[1:2, :], lam_ref[1:2, :])
    a, b = _block_prefix(a, b, reverse=True)
    ab_ref[...] = a
    bb_ref[...] = b
    gel_ref[...] = _gelu_tanh(cy)


def _rglru_finish(h0f, h0b, af_ref, bf_ref, ab_ref, bb_ref, hf_ref, hb_ref, gel_ref):
    nblk = af_ref.shape[0] // 8

    def body(k, carry):
        cf, cb = carry
        rf = pl.ds(pl.multiple_of(k * 8, 8), 8)
        hf = bf_ref[rf, :] + af_ref[rf, :] * cf
        hf_ref[rf, :] = hf
        rb = pl.ds(pl.multiple_of((nblk - 1 - k) * 8, 8), 8)
        hb = bb_ref[rb, :] + ab_ref[rb, :] * cb
        hb_ref[rb, :] = hb
        return hf[7:8, :], hb[0:1, :]

    cf, cb = lax.fori_loop(0, nblk, body, (h0f, h0b))
    oc = (hf_ref[...] + hb_ref[...]) * gel_ref[...]
    return oc, cf, cb


def _store_gate_weights(gate_w_ref, wa_ref, wx_ref):
    for d in range(2):
        gate_w_ref[2 * d] = _block_diag([wa_ref[d, n] for n in range(N_HEADS)]).astype(BF16)
        gate_w_ref[2 * d + 1] = _block_diag([wx_ref[d, n] for n in range(N_HEADS)]).astype(BF16)


def _lane_head_masks(n):
    lane = lax.broadcasted_iota(jnp.int32, (1, n), 1) >> HEAD_SHIFT
    return [jnp.where(lane == h, 1.0, 0.0) for h in range(n // HEAD_DIM)]


def _log_decays(theta_ref, masks):
    theta = theta_ref[...]
    lanes = theta[:, 0:1] * masks[0]
    for h in range(1, N_HEADS):
        lanes = lanes + theta[:, h:h + 1] * masks[h]
    lg = jnp.log1p(-jnp.exp(lanes))
    return lg[0:1, :], lg[1:2, :]


RET_BLOCK = 256


def _retention(q, k8, vb, s0, lgf, lgb, masks, o_ref):
    t, w = q.shape
    c = RET_BLOCK
    nh = w // HEAD_DIM
    pos = lax.broadcasted_iota(jnp.int32, (c, w), 0).astype(F32)
    q_dec = (jnp.exp(lgf * (pos + 1.0)), jnp.exp(lgb * (float(c) - pos)))
    k_dec = (jnp.exp(lgf * (float(c - 1) - pos)), jnp.exp(lgb * pos))
    chunk_dec = (jnp.exp(lgf * float(c)), jnp.exp(lgb * float(c)))
    rel = (lax.broadcasted_iota(jnp.int32, (c, c), 0) - lax.broadcasted_iota(jnp.int32, (c, c), 1)).astype(F32)
    decs = []
    for h in range(nh):
        gf = lgf[:, h * HEAD_DIM:h * HEAD_DIM + 1]
        gb = lgb[:, h * HEAD_DIM:h * HEAD_DIM + 1]
        e = jnp.exp(jnp.where(rel >= 0, gf * rel, gb * (-rel)))
        decs.append(jnp.where(rel == 0, 2.0, e))
    dec = jnp.concatenate(decs, axis=0)
    r_head = lax.broadcasted_iota(jnp.int32, (w, w), 0) >> HEAD_SHIFT
    c_head = lax.broadcasted_iota(jnp.int32, (w, w), 1) >> HEAD_SHIFT
    same_head = jnp.where(r_head == c_head, 1.0, 0.0)
    states = [None, None] if s0 is None else list(s0)

    def carry(d, rows, o):
        if states[d] is not None:
            o = o + _dot((q[rows, :] * q_dec[d]).astype(BF16), states[d].astype(BF16))
        upd = _dot_tn((k8[rows, :] * k_dec[d]).astype(BF16), vb[rows, :]) * same_head
        states[d] = upd if states[d] is None else states[d] * chunk_dec[d] + upd
        return o

    for ci in range(t // c):
        rows = slice(ci * c, (ci + 1) * c)
        qc = q[rows, :]
        q_stack = jnp.concatenate([(qc * masks[h]).astype(BF16) for h in range(nh)], axis=0)
        inner = (_dot_nt(q_stack, k8[rows, :].astype(BF16)) * dec).astype(BF16)
        out = _dot(inner, vb[rows, :])
        o = out[0:c, :] * masks[0]
        for h in range(1, nh):
            o = o + out[h * c:(h + 1) * c, :] * masks[h]
        o_ref[rows, :] = carry(0, rows, o)
    for ci in reversed(range(t // c)):
        rows = slice(ci * c, (ci + 1) * c)
        if states[1] is not None:
            o_ref[rows, :] = carry(1, rows, o_ref[rows, :])
        else:
            carry(1, rows, None)
    return states[0], states[1]


def _ctx_mixer_kernel(*refs, layer, n_prev):
    prev_refs = refs[:n_prev]
    (x_ref, n2_ref, sh_ref, sc_ref, g2_ref, win_ref, wout_ref,
     aqn_ref, akn_ref, bqn_ref, bkn_ref, sink_ref,
     convw_ref, convb_ref, wa_ref, ba_ref, wx_ref, bx_ref, lam_ref, theta_ref, dn_ref,
     xn_ref, *state_refs) = refs[n_prev:n_prev + 28]
    (win_s, wout_s, u_ref, mixed_ref,
     gate_w_ref, af_ref, bf_ref, ab_ref, bb_ref, hf_ref, hb_ref, gel_ref, ret_ref) = refs[n_prev + 28:]
    t = SEQ
    lrow = slice(layer, layer + 1)
    for prev_ref, state_ref in zip(prev_refs, state_refs):
        for earlier in range(layer):
            state_ref[earlier] = prev_ref[earlier]
    ka_ref, va_ref, kb_ref, vb_ref, stc_ref, std_ref = (ref.at[layer] for ref in state_refs)

    @pl.when(pl.program_id(0) == 0)
    def _():
        for c in range(IN_WIDTH // 512):
            win_s[:, c * 512:(c + 1) * 512] = win_ref[:, c * 512:(c + 1) * 512].astype(BF16)
        wout_s[...] = wout_ref[...].astype(BF16)
        _store_gate_weights(gate_w_ref, wa_ref, wx_ref)

    x = x_ref[...]
    h = _norm_mod(x, n2_ref[lrow, :], sc_ref[0:1, :], sh_ref[0:1, :]).astype(BF16)
    c_cols = slice(COL_CX, COL_CX + 2 * GROUP_W)
    u_ref[:, c_cols] = _dot(h, win_s[:, c_cols])
    _rglru_prepare(u_ref[:, COL_CX:COL_CX + GROUP_W], u_ref[:, COL_CY:COL_CY + GROUP_W],
                   convw_ref, convb_ref[lrow, :], gate_w_ref, ba_ref, bx_ref, lam_ref,
                   af_ref, bf_ref, ab_ref, bb_ref, gel_ref)
    u_ref[:, 0:COL_CX] = _dot(h, win_s[:, 0:COL_CX])
    u_ref[:, COL_DQ:IN_WIDTH] = _dot(h, win_s[:, COL_DQ:IN_WIDTH])

    for (cq, ck, cv, qn_ref, kn_ref, k_out, v_out, col0, use_sink) in (
            (COL_AQ, COL_AK, COL_AV, aqn_ref, akn_ref, ka_ref, va_ref, 0, True),
            (COL_BQ, COL_BK, COL_BV, bqn_ref, bkn_ref, kb_ref, vb_ref, GROUP_W, False)):
        q = _head_norm(u_ref[:, cq:cq + 256], qn_ref[lrow, :])
        k = _head_norm(u_ref[:, ck:ck + 128], kn_ref[lrow, :])
        v = u_ref[:, cv:cv + 128]
        k_out[...] = k
        v_out[...] = v
        qs = q * (HEAD_DIM ** -0.5)
        heads = []
        for hd in range(N_HEADS):
            kv = hd // 2
            s = _dot_nt(_head_cols(qs, hd), _head_cols(k, kv))
            sink = jnp.full((t, 1), sink_ref[layer, hd], F32) if use_sink else None
            heads.append(_softmax_pv([s], [_head_cols(v, kv)], sink))
        mixed_ref[:, col0:col0 + GROUP_W] = jnp.concatenate(heads, axis=-1).astype(BF16)

    zero = jnp.zeros((1, GROUP_W), F32)
    oc, cf, cb = _rglru_finish(zero, zero, af_ref, bf_ref, ab_ref, bb_ref, hf_ref, hb_ref, gel_ref)
    mixed_ref[:, 2 * GROUP_W:3 * GROUP_W] = oc.astype(BF16)
    stc_ref[0:1, :] = cf
    stc_ref[1:2, :] = cb

    masks = _lane_head_masks(GROUP_W)
    lgf, lgb = _log_decays(theta_ref, masks)
    k8 = u_ref[:, COL_DK:COL_DK + GROUP_W] * (HEAD_DIM ** -0.5)
    vb = u_ref[:, COL_DV:COL_DV + GROUP_W].astype(BF16)
    final_states = _retention(u_ref[:, COL_DQ:COL_DQ + GROUP_W], k8, vb, None, lgf, lgb, masks, ret_ref)
    o = ret_ref[...]
    o = o * lax.rsqrt(_head_mean_square(o) + EPS) * dn_ref[lrow, :] * _silu(u_ref[:, COL_DG:COL_DG + GROUP_W])
    mixed_ref[:, 3 * GROUP_W:4 * GROUP_W] = o.astype(BF16)
    for d, s_full in enumerate(final_states):
        for hd in range(N_HEADS):
            std_ref[d, hd] = s_full[hd * 64:(hd + 1) * 64, hd * 64:(hd + 1) * 64]

    xn_ref[...] = x + g2_ref[0:1, :] * _dot(mixed_ref[...], wout_s[...])


def _ctx_mixers(x, mod, layer, prev, norm2_g, w_in, w_out,
                a_qn, a_kn, a_sink, b_qn, b_kn, c_conv_w, c_conv_b, c_wa, c_ba, c_wx, c_bx,
                c_lambda, d_theta, d_norm_g):
    per_request = lambda slots, shape: pl.BlockSpec((None, slots) + shape, lambda b: (b,) + (0,) * (1 + len(shape)))
    state_dims = [(SEQ, 128)] * 4 + [(2, GROUP_W), (2, N_HEADS, HEAD_DIM, HEAD_DIM)]
    scr = pltpu.VMEM((SEQ, GROUP_W), F32)
    out = pl.pallas_call(
        functools.partial(_ctx_mixer_kernel, layer=layer, n_prev=len(prev)),
        grid=(BATCH,),
        in_specs=[per_request(layer, dims) for dims in state_dims[:len(prev)]] + [
            pl.BlockSpec((SEQ, D_MODEL), lambda b: (b, 0)),
            _full((DEPTH, D_MODEL)),
            _mod_chunk(layer, 3), _mod_chunk(layer, 4), _mod_chunk(layer, 5),
            _once((None, D_MODEL, IN_WIDTH), lambda b: (layer, 0, 0)),
            _once((None, D_MODEL, D_MODEL), lambda b: (layer, 0, 0)),
            _full((DEPTH, HEAD_DIM)), _full((DEPTH, HEAD_DIM)), _full((DEPTH, HEAD_DIM)), _full((DEPTH, HEAD_DIM)),
            pl.BlockSpec(memory_space=pltpu.SMEM),
            _layer_block((4, GROUP_W), layer), _full((DEPTH, GROUP_W)),
            _layer_block((2, N_HEADS, HEAD_DIM, HEAD_DIM), layer), _layer_block((2, GROUP_W), layer),
            _layer_block((2, N_HEADS, HEAD_DIM, HEAD_DIM), layer), _layer_block((2, GROUP_W), layer),
            _layer_block((2, GROUP_W), layer),
            _layer_block((2, N_HEADS), layer), _full((DEPTH, GROUP_W)),
        ],
        out_specs=[pl.BlockSpec((SEQ, D_MODEL), lambda b: (b, 0))] + [
            per_request(layer + 1, dims) for dims in state_dims],
        out_shape=[jax.ShapeDtypeStruct((N_ROWS, D_MODEL), F32)] + [
            jax.ShapeDtypeStruct((BATCH, layer + 1) + dims, F32) for dims in state_dims],
        input_output_aliases={len(prev): 0},
        scratch_shapes=[pltpu.VMEM((D_MODEL, IN_WIDTH), BF16), pltpu.VMEM((D_MODEL, D_MODEL), BF16),
                        pltpu.VMEM((SEQ, IN_WIDTH), F32), pltpu.VMEM((SEQ, D_MODEL), BF16),
                        pltpu.VMEM((4, GROUP_W, GROUP_W), BF16)] + [scr] * 8,
        compiler_params=_cparams("arbitrary"),
        name="ctx_mixers",
    )(*prev, x, norm2_g, mod, mod, mod, w_in, w_out,
      a_qn, a_kn, b_qn, b_kn, a_sink, c_conv_w, c_conv_b, c_wa, c_ba, c_wx, c_bx,
      c_lambda, d_theta, d_norm_g)
    return out[0], tuple(out[1:])


LAT_BLOCK0 = N_CTX_ROWS // DEC_SEQ


def _lat_attn_kernel(x_ref, n2_ref, sh_ref, sc_ref, g2_ref, win_ref, wout_ref,
                     kca_ref, vca_ref, kcb_ref, vcb_ref,
                     aqn_ref, akn_ref, bqn_ref, bkn_ref, sink_ref, cos_ref, sinl_ref, sinh_ref,
                     xn_ref, h_ref, u_ref, o_ref, *, layer):
    t = DEC_SEQ
    lrow = slice(layer, layer + 1)
    mrow = pl.ds(1 + pl.program_id(0), 1)
    cos, sin_lo, sin_hi = cos_ref[...], sinl_ref[...], sinh_ref[...]
    scale = HEAD_DIM ** -0.5
    x = x_ref[...]
    h_ref[...] = _norm_mod(x, n2_ref[lrow, :], sc_ref[mrow, :], sh_ref[mrow, :]).astype(BF16)
    u_ref[...] = _dot(h_ref[...], win_ref[...].astype(BF16))

    q = _rope(_head_norm(u_ref[:, COL_AQ:COL_AQ + 256], aqn_ref[lrow, :]), cos, sin_lo, sin_hi)
    k = _rope(_head_norm(u_ref[:, COL_AK:COL_AK + 128], akn_ref[lrow, :]), cos, sin_lo, sin_hi)
    qh = [_head_cols(q * scale, h) for h in range(4)]
    v = u_ref[:, COL_AV:COL_AV + 128]
    kh = [_head_cols(k, kv) for kv in range(2)]
    vh = [_head_cols(v, kv) for kv in range(2)]
    kch = [_head_cols(kca_ref[...], kv) for kv in range(2)]
    vch = [_head_cols(vca_ref[...], kv) for kv in range(2)]
    w = ATT_BLOCK
    span = 3 * w
    for n in range(t // w):
        start = min(max((n - 1) * w, 0), t - span)
        rows = slice(n * w, (n + 1) * w)
        band = slice(start, start + span)
        qpos = (lax.broadcasted_iota(jnp.int32, (2 * w, span), 0) & (w - 1)) + n * w
        kpos = lax.broadcasted_iota(jnp.int32, (2 * w, span), 1) + start
        valid = jnp.abs(qpos - kpos) <= WINDOW
        heads = []
        for kv in range(2):
            qp = jnp.concatenate([qh[2 * kv][rows, :], qh[2 * kv + 1][rows, :]], axis=0)
            s_ctx = _dot_nt(qp, kch[kv])
            s_band = jnp.where(valid, _dot_nt(qp, kh[kv][band, :]), NEG_INF)
            row = lax.broadcasted_iota(jnp.int32, (2 * w, 1), 0)
            sink = jnp.where(row < w, sink_ref[layer, 2 * kv], sink_ref[layer, 2 * kv + 1])
            o = _softmax_pv([s_ctx, s_band], [vch[kv], vh[kv][band, :]], sink)
            heads += [o[0:w, :], o[w:2 * w, :]]
        o_ref[rows, 0:GROUP_W] = jnp.concatenate(heads, axis=-1).astype(BF16)

    q = _rope(_head_norm(u_ref[:, COL_BQ:COL_BQ + 256], bqn_ref[lrow, :]), cos, sin_lo, sin_hi)
    k = _rope(_head_norm(u_ref[:, COL_BK:COL_BK + 128], bkn_ref[lrow, :]), cos, sin_lo, sin_hi)
    qh = [_head_cols(q * scale, h) for h in range(4)]
    v = u_ref[:, COL_BV:COL_BV + 128]
    kh = [_head_cols(k, kv) for kv in range(2)]
    vh = [_head_cols(v, kv) for kv in range(2)]
    kch = [_head_cols(kcb_ref[...], kv) for kv in range(2)]
    vch = [_head_cols(vcb_ref[...], kv) for kv in range(2)]
    tq = 256
    for n in range(t // tq):
        rows = slice(n * tq, (n + 1) * tq)
        heads = []
        for kv in range(2):
            qp = jnp.concatenate([qh[2 * kv][rows, :], qh[2 * kv + 1][rows, :]], axis=0)
            o = _softmax_pv([_dot_nt(qp, kch[kv]), _dot_nt(qp, kh[kv])], [vch[kv], vh[kv]], None)
            heads += [o[0:tq, :], o[tq:2 * tq, :]]
        o_ref[rows, GROUP_W:2 * GROUP_W] = jnp.concatenate(heads, axis=-1).astype(BF16)

    xn_ref[...] = x + g2_ref[mrow, :] * _dot(o_ref[...], wout_ref[...].astype(BF16))


def _lat_recurrent_kernel(xn_in_ref, h_ref, g2_ref, wc_ref, wqk_ref, wvg_ref, wout_ref, h0_ref,
                          convw_ref, convb_ref, wa_ref, ba_ref, wx_ref, bx_ref, lam_ref,
                          s0_ref, theta_ref, dn_ref,
                          xn_ref, gate_w_ref, af_ref, bf_ref, ab_ref, bb_ref, hf_ref, hb_ref, gel_ref, ret_ref,
                          *, layer):
    lrow = slice(layer, layer + 1)
    mrow = pl.ds(1 + pl.program_id(0), 1)

    @pl.when(pl.program_id(0) == 0)
    def _():
        _store_gate_weights(gate_w_ref, wa_ref, wx_ref)

    h = h_ref[...]
    u = _dot(h, wc_ref[...].astype(BF16))
    _rglru_prepare(u[:, 0:GROUP_W], u[:, GROUP_W:2 * GROUP_W], convw_ref, convb_ref[lrow, :],
                   gate_w_ref, ba_ref, bx_ref, lam_ref, af_ref, bf_ref, ab_ref, bb_ref, gel_ref)
    oc, _, _ = _rglru_finish(h0_ref[0:1, :], h0_ref[1:2, :],
                             af_ref, bf_ref, ab_ref, bb_ref, hf_ref, hb_ref, gel_ref)
    y = _dot(oc.astype(BF16), wout_ref[0:GROUP_W, :].astype(BF16))

    uqk = _dot(h, wqk_ref[...].astype(BF16))
    uvg = _dot(h, wvg_ref[...].astype(BF16))
    masks = _lane_head_masks(GROUP_W)
    lgf, lgb = _log_decays(theta_ref, masks)
    s0 = tuple(_block_diag([s0_ref[d, hd] for hd in range(N_HEADS)]) for d in range(2))
    _retention(uqk[:, 0:GROUP_W], uqk[:, GROUP_W:2 * GROUP_W] * (HEAD_DIM ** -0.5),
               uvg[:, 0:GROUP_W].astype(BF16), s0, lgf, lgb, masks, ret_ref)
    o = ret_ref[...]
    o = o * lax.rsqrt(_head_mean_square(o) + EPS) * dn_ref[lrow, :] * _silu(uvg[:, GROUP_W:2 * GROUP_W])
    y = y + _dot(o.astype(BF16), wout_ref[GROUP_W:2 * GROUP_W, :].astype(BF16))
    xn_ref[...] = xn_in_ref[...] + g2_ref[mrow, :] * y


def _lat_mixers(x, mod, layer, caches, state_c, state_d, rope, norm2_g, w_in, w_out,
                a_qn, a_kn, a_sink, b_qn, b_kn, c_conv_w, c_conv_b, c_wa, c_ba, c_wx, c_bx,
                c_lambda, d_theta, d_norm_g):
    rows = pl.BlockSpec((DEC_SEQ, D_MODEL), lambda b: (LAT_BLOCK0 + b, 0))
    h_rows = pl.BlockSpec((DEC_SEQ, D_MODEL), lambda b: (b, 0))
    cache_spec = pl.BlockSpec((None, None, PAST_LEN, 128), lambda b: (b, layer, 0, 0))
    gain = _full((DEPTH, HEAD_DIM))
    table = _once((DEC_SEQ, 128), lambda b: (0, 0))
    out_shape = jax.ShapeDtypeStruct((N_ROWS, D_MODEL), F32)
    win_cols = lambda w, c: _once((None, D_MODEL, w), lambda b: (layer, 0, c))
    wout_rows = lambda h, r: _once((None, h, D_MODEL), lambda b: (layer, r, 0))

    xn, h = pl.pallas_call(
        functools.partial(_lat_attn_kernel, layer=layer),
        grid=(DEC_BATCH,),
        in_specs=[rows, _full((DEPTH, D_MODEL)),
                  _mod_chunk(layer, 3), _mod_chunk(layer, 4), _mod_chunk(layer, 5),
                  win_cols(4 * GROUP_W, 0), wout_rows(2 * GROUP_W, 0),
                  cache_spec, cache_spec, cache_spec, cache_spec,
                  gain, gain, gain, gain,
                  pl.BlockSpec(memory_space=pltpu.SMEM),
                  table, table, table],
        out_specs=[rows, pl.BlockSpec((DEC_SEQ, D_MODEL), lambda b: (b, 0), pipeline_mode=pl.Buffered(1))],
        out_shape=[out_shape, jax.ShapeDtypeStruct((N_LAT_ROWS, D_MODEL), BF16)],
        input_output_aliases={0: 0},
        scratch_shapes=[pltpu.VMEM((DEC_SEQ, 4 * GROUP_W), F32), pltpu.VMEM((DEC_SEQ, 2 * GROUP_W), BF16)],
        compiler_params=_cparams("arbitrary"),
        name="lat_attention",
    )(x, norm2_g, mod, mod, mod, w_in, w_out, *caches, a_qn, a_kn, b_qn, b_kn, a_sink, *rope)

    scr = pltpu.VMEM((DEC_SEQ, GROUP_W), F32)
    xn = pl.pallas_call(
        functools.partial(_lat_recurrent_kernel, layer=layer),
        grid=(DEC_BATCH,),
        in_specs=[
            rows, h_rows, _mod_chunk(layer, 5),
            win_cols(2 * GROUP_W, COL_CX // (2 * GROUP_W)),
            win_cols(2 * GROUP_W, COL_DQ // (2 * GROUP_W)), win_cols(2 * GROUP_W, COL_DV // (2 * GROUP_W)),
            wout_rows(2 * GROUP_W, 1),
            pl.BlockSpec((None, None, 2, GROUP_W), lambda b: (b, layer, 0, 0)),
            _layer_block((4, GROUP_W), layer), _full((DEPTH, GROUP_W)),
            _layer_block((2, N_HEADS, HEAD_DIM, HEAD_DIM), layer), _layer_block((2, GROUP_W), layer),
            _layer_block((2, N_HEADS, HEAD_DIM, HEAD_DIM), layer), _layer_block((2, GROUP_W), layer),
            _layer_block((2, GROUP_W), layer),
            pl.BlockSpec((None, None, 2, N_HEADS, HEAD_DIM, HEAD_DIM), lambda b: (b, layer, 0, 0, 0, 0)),
            _layer_block((2, N_HEADS), layer), _full((DEPTH, GROUP_W))],
        out_specs=rows,
        out_shape=out_shape,
        input_output_aliases={0: 0},
        scratch_shapes=[pltpu.VMEM((4, GROUP_W, GROUP_W), BF16)] + [scr] * 8,
        compiler_params=_cparams("arbitrary"),
        name="lat_recurrent",
    )(xn, h, mod, w_in, w_in, w_in, w_out, state_c, c_conv_w, c_conv_b, c_wa, c_ba, c_wx, c_bx, c_lambda,
      state_d, d_theta, d_norm_g)
    return xn


def _rope_tables():
    t = np.arange(DEC_SEQ)
    row = (t // GRID_W).astype(np.float64)[:, None]
    col = (t % GRID_W).astype(np.float64)[:, None]
    half = HEAD_DIM // 2
    inv = 1.0 / (ROPE_BASE ** (np.arange(0, half, 2, dtype=np.float64) / half))
    j = np.arange(128) % HEAD_DIM
    ang = np.where((j < half)[None, :], row, col) * inv[j % (half // 2)][None, :]
    first = ((j % half) < half // 2)[None, :]
    cos, sin = np.cos(ang), np.sin(ang)
    return tuple(jnp.asarray(a, F32) for a in (cos, np.where(first, -sin, 0.0), np.where(first, 0.0, sin)))


def kernel(x_prompt, x_sample, cache_a_k, cache_a_v, cache_b_k, cache_b_v, state_c, state_d, c, c_ctx, norm1_g, norm2_g, norm3_g, w_mod, b_mod, ffn1_wg, ffn1_wu, ffn1_wd, ffn2_wg, ffn2_wu, ffn2_wd, w_in, w_out, a_qn, a_kn, a_sink, b_qn, b_kn, c_conv_w, c_conv_b, c_wa, c_ba, c_wx, c_bx, c_lambda, d_theta, d_norm_g):
    mod = _modulation(c_ctx, c, w_mod, b_mod)
    rope = _rope_tables()
    caches = tuple(t.reshape(DEC_BATCH, DEPTH, PAST_LEN, 128) for t in (cache_a_k, cache_a_v, cache_b_k, cache_b_v))
    mixer_params = (a_qn, a_kn, a_sink, b_qn, b_kn, c_conv_w, c_conv_b, c_wa, c_ba, c_wx, c_bx,
                    c_lambda, d_theta, d_norm_g)
    xs = (x_prompt.reshape(N_CTX_ROWS, D_MODEL), x_sample.reshape(N_LAT_ROWS, D_MODEL))
    states = ()
    for l in range(DEPTH):
        (x,) = _ffn(xs, mod, l, 0, norm1_g, ffn1_wg, ffn1_wu, ffn1_wd)
        x, states = _ctx_mixers(x, mod, l, states, norm2_g, w_in, w_out, *mixer_params)
        x = _lat_mixers(x, mod, l, caches, state_c, state_d, rope, norm2_g, w_in, w_out, *mixer_params)
        xs = _ffn((x,), mod, l, 6, norm3_g, ffn2_wg, ffn2_wu, ffn2_wd, split_out=(l == DEPTH - 1))
    y_p, y_s = xs
    ka, va, kb, vb, st_c, st_d = states
    kv_shape = (BATCH, DEPTH, SEQ, 2, HEAD_DIM)
    return (y_p.reshape(BATCH, SEQ, D_MODEL), y_s.reshape(DEC_BATCH, DEC_SEQ, D_MODEL),
            ka.reshape(kv_shape), va.reshape(kv_shape), kb.reshape(kv_shape), vb.reshape(kv_shape),
            st_c, st_d)
```

```python
import functools
import math

import numpy as np
import jax
import jax.numpy as jnp
from jax import lax
from jax.experimental import pallas as pl
from jax.experimental.pallas import tpu as pltpu

F32 = jnp.float32
BF16 = jnp.bfloat16

D_MODEL = 1024
BATCH = 16
SEQ = 256
DEPTH = 2
DEC_BATCH = 2
DEC_SEQ = 1024
PAST_LEN = 512
GRID_W = 64
HEAD_DIM = 64
HEAD_SHIFT = 6
N_HEADS = 4
GROUP_W = 256
WINDOW = 128
ATT_BLOCK = 128
ROPE_BASE = 10000.0
LRU_C = 8.0
D_FF = 2816
N_MOD = 9
EPS = 1e-6
NEG_INF = -1e30
IN_WIDTH = 2560

N_CTX_ROWS = BATCH * SEQ
N_LAT_ROWS = DEC_BATCH * DEC_SEQ
N_ROWS = N_CTX_ROWS + N_LAT_ROWS
MOD_ROWS = 8
MOD_GROUP = 1024

VMEM_LIMIT_BYTES = 56 * 1024 * 1024

COL_AQ, COL_AK, COL_AV = 0, 256, 384
COL_BQ, COL_BK, COL_BV = 512, 768, 896
COL_CX, COL_CY = 1024, 1280
COL_DQ, COL_DK, COL_DV, COL_DG = 1536, 1792, 2048, 2304


def _cparams(*sem):
    return pltpu.CompilerParams(dimension_semantics=sem, vmem_limit_bytes=VMEM_LIMIT_BYTES)


def _dot(a, b):
    return jnp.dot(a, b, preferred_element_type=F32)


def _dot_nt(a, b):
    return lax.dot_general(a, b, (((1,), (1,)), ((), ())), preferred_element_type=F32)


def _dot_tn(a, b):
    return lax.dot_general(a, b, (((0,), (0,)), ((), ())), preferred_element_type=F32)


def _sigmoid(x):
    return 0.5 * jnp.tanh(0.5 * x) + 0.5


def _silu(x):
    return x * _sigmoid(x)


def _gelu_tanh(x):
    return 0.5 * x * (1.0 + jnp.tanh(math.sqrt(2.0 / math.pi) * (x + 0.044715 * (x * x * x))))


def _mod_row(i, tm, s):
    if tm >= MOD_GROUP:
        block_index = i * (tm // MOD_GROUP) + s
    else:
        block_index = i >> int(math.log2(MOD_GROUP // tm))
    return jnp.maximum(block_index - (N_CTX_ROWS // MOD_GROUP - 1), 0)


def _norm_mod(x, g, sc, sh):
    ms = jnp.mean(x * x, axis=-1, keepdims=True)
    return (x * lax.rsqrt(ms + EPS) * g) * (1.0 + sc) + sh


def _full(shape):
    return pl.BlockSpec(shape, lambda *_: (0,) * len(shape))


def _layer_block(shape, layer):
    return pl.BlockSpec((None,) + shape, lambda *_: (layer,) + (0,) * len(shape))


MOD_TN = 3072


def _mod_kernel(cc_ref, c_ref, w_ref, b_ref, o_ref):
    l = pl.program_id(0)
    pad = jnp.zeros((MOD_ROWS - 1 - DEC_BATCH, D_MODEL), F32)
    cond = jnp.concatenate([cc_ref[...], c_ref[...], pad], axis=0)
    o_ref[...] = _dot(_silu(cond).astype(BF16), w_ref[...].astype(BF16)) + b_ref[pl.ds(l, 1), :]


def _modulation(c_ctx, c, w_mod, b_mod):
    n = N_MOD * D_MODEL
    return pl.pallas_call(
        _mod_kernel,
        grid=(DEPTH, n // MOD_TN),
        in_specs=[
            pl.BlockSpec((1, D_MODEL), lambda l, j: (0, 0)),
            pl.BlockSpec((DEC_BATCH, D_MODEL), lambda l, j: (0, 0)),
            pl.BlockSpec((None, D_MODEL, MOD_TN), lambda l, j: (l, 0, j)),
            pl.BlockSpec((DEPTH, MOD_TN), lambda l, j: (0, j)),
        ],
        out_specs=pl.BlockSpec((None, MOD_ROWS, MOD_TN), lambda l, j: (l, 0, j)),
        out_shape=jax.ShapeDtypeStruct((DEPTH, MOD_ROWS, n), F32),
        compiler_params=_cparams("arbitrary", "arbitrary"),
        name="modulation",
    )(c_ctx.reshape(1, D_MODEL), c, w_mod, b_mod)


FFN_TM = 1024
FFN_TF = 256
N_CTX_TILES = N_CTX_ROWS // FFN_TM


FFN_NJ = D_FF // FFN_TF
N_FFN_TILES = N_ROWS // FFN_TM
N_FFN_STEPS = FFN_NJ + N_FFN_TILES


def _ffn_tile(step):
    return jnp.maximum(step - FFN_NJ, 0)


def _on_stream_part(tile, x_refs, o_refs, fn):
    if len(x_refs) == 1 and len(o_refs) == 1:
        fn(x_refs[0], o_refs[0])
    else:
        pl.when(tile < N_CTX_TILES)(lambda: fn(x_refs[0], o_refs[0]))
        pl.when(tile >= N_CTX_TILES)(lambda: fn(x_refs[-1], o_refs[-1]))


def _ffn_kernel(*refs, layer, n_in, n_out):
    x_refs = refs[:n_in]
    n_ref, sh_ref, sc_ref, g_ref, wg_ref, wu_ref, wd_ref = refs[n_in:n_in + 7]
    o_refs = refs[n_in + 7:n_in + 7 + n_out]
    h_ref, a_ref, wg_s, wu_s, wd_s = refs[n_in + 7 + n_out:]
    nj, tf = FFN_NJ, FFN_TF
    s = pl.program_id(0)
    tile = _ffn_tile(s)
    r = _mod_row(tile, FFN_TM, 0)

    def load_tile():
        def init(x_ref, _):
            h = _norm_mod(x_ref[...], n_ref[layer:layer + 1, :], sc_ref[pl.ds(r, 1), :], sh_ref[pl.ds(r, 1), :])
            h_ref[...] = h.astype(BF16)
        _on_stream_part(tile, x_refs, o_refs, init)

    def up_chunk(j, cols):
        h = h_ref[...]
        a_ref[:, cols] = (_silu(_dot(h, wg_s[j])) * _dot(h, wu_s[j])).astype(BF16)

    def down_and_store():
        y = (0.5 * g_ref[pl.ds(r, 1), :]) * _dot(a_ref[...], wd_s[...])

        def store(x_ref, o_ref):
            o_ref[...] = x_ref[...] + y
        _on_stream_part(tile, x_refs, o_refs, store)

    def keep_arrived_chunk():
        wg_s[s] = wg_ref[...].astype(BF16)
        wu_s[s] = wu_ref[...].astype(BF16)
        wd_s[pl.ds(pl.multiple_of(s * tf, tf), tf), :] = wd_ref[...].astype(BF16)

    def up_previous_chunk():
        up_chunk(s - 1, pl.ds(pl.multiple_of((s - 1) * tf, tf), tf))

    @pl.when(s == 0)
    def _():
        load_tile()
        keep_arrived_chunk()

    @pl.when((s > 0) & (s < nj))
    def _():
        up_previous_chunk()
        keep_arrived_chunk()

    @pl.when(s == nj)
    def _():
        up_previous_chunk()
        down_and_store()

    @pl.when(s > nj)
    def _():
        load_tile()
        for j in range(nj):
            up_chunk(j, slice(j * tf, (j + 1) * tf))
        down_and_store()


def _stream_specs(split, buffered_once):
    tm = FFN_TM
    kw = {"pipeline_mode": pl.Buffered(1)} if buffered_once else {}
    if not split:
        return [pl.BlockSpec((tm, D_MODEL), lambda s: (_ffn_tile(s), 0), **kw)]
    last_ctx = N_CTX_TILES - 1
    return [pl.BlockSpec((tm, D_MODEL), lambda s: (jnp.minimum(_ffn_tile(s), last_ctx), 0), **kw),
            pl.BlockSpec((tm, D_MODEL), lambda s: (jnp.maximum(_ffn_tile(s) - N_CTX_TILES, 0), 0), **kw)]


def _ffn(xs, mod, layer, chunk0, norm_g, wg, wu, wd, split_out=False):
    tm, tf, nj = FFN_TM, FFN_TF, FFN_NJ
    split_in = len(xs) == 2
    mod_spec = lambda c: pl.BlockSpec((None, MOD_ROWS, D_MODEL), lambda s: (layer, 0, c))
    w_col = lambda s: (layer, 0, jnp.minimum(s, nj - 1))
    w_row = lambda s: (layer, jnp.minimum(s, nj - 1), 0)
    if split_out:
        out_shape = [jax.ShapeDtypeStruct((N_CTX_ROWS, D_MODEL), F32),
                     jax.ShapeDtypeStruct((N_LAT_ROWS, D_MODEL), F32)]
    else:
        out_shape = [jax.ShapeDtypeStruct((N_ROWS, D_MODEL), F32)]
    out = pl.pallas_call(
        functools.partial(_ffn_kernel, layer=layer, n_in=len(xs), n_out=len(out_shape)),
        grid=(N_FFN_STEPS,),
        in_specs=_stream_specs(split_in, False) + [
            _full((DEPTH, D_MODEL)),
            mod_spec(chunk0), mod_spec(chunk0 + 1), mod_spec(chunk0 + 2),
            pl.BlockSpec((None, D_MODEL, tf), w_col),
            pl.BlockSpec((None, D_MODEL, tf), w_col),
            pl.BlockSpec((None, tf, D_MODEL), w_row),
        ],
        out_specs=_stream_specs(split_out, True),
        out_shape=out_shape,
        scratch_shapes=[pltpu.VMEM((tm, D_MODEL), BF16),
                        pltpu.VMEM((tm, D_FF), BF16),
                        pltpu.VMEM((nj, D_MODEL, tf), BF16),
                        pltpu.VMEM((nj, D_MODEL, tf), BF16),
                        pltpu.VMEM((D_FF, D_MODEL), BF16)],
        compiler_params=_cparams("arbitrary"),
        name="ffn",
    )(*xs, norm_g, mod, mod, mod, wg, wu, wd)
    return tuple(out)


def _once(shape, index_map):
    return pl.BlockSpec(shape, index_map, pipeline_mode=pl.Buffered(1))


def _mod_chunk(layer, c):
    return pl.BlockSpec((None, MOD_ROWS, D_MODEL), lambda *_: (layer, 0, c))


def _head_mean_square(x):
    n = x.shape[-1]
    r = lax.broadcasted_iota(jnp.int32, (n, n), 0) >> HEAD_SHIFT
    c = lax.broadcasted_iota(jnp.int32, (n, n), 1) >> HEAD_SHIFT
    ones_bd = jnp.where(r == c, 1.0, 0.0).astype(BF16)
    return _dot((x * x).astype(BF16), ones_bd) * (1.0 / HEAD_DIM)


def _head_norm(x, head_gain):
    gain_row = jnp.concatenate([head_gain] * (x.shape[-1] // HEAD_DIM), axis=-1)
    return x * lax.rsqrt(_head_mean_square(x) + EPS) * gain_row


def _head_cols(x, h):
    return x[:, h * HEAD_DIM:(h + 1) * HEAD_DIM].astype(BF16)


def _softmax_pv(scores, values, sink):
    m = jnp.max(scores[0], axis=-1, keepdims=True)
    for s in scores[1:]:
        m = jnp.maximum(m, jnp.max(s, axis=-1, keepdims=True))
    if sink is not None:
        m = jnp.maximum(m, sink)
    denom = None
    acc = None
    for s, v in zip(scores, values):
        p = jnp.exp(s - m)
        d = jnp.sum(p, axis=-1, keepdims=True)
        o = _dot(p.astype(BF16), v)
        denom = d if denom is None else denom + d
        acc = o if acc is None else acc + o
    if sink is not None:
        denom = denom + jnp.exp(sink - m)
    return acc / denom


def _rope(x, cos, sin_lo, sin_hi):
    cols = []
    for c in range(x.shape[-1] // 128):
        xc = x[:, c * 128:(c + 1) * 128]
        cols.append(xc * cos + pltpu.roll(xc, 112, 1) * sin_lo + pltpu.roll(xc, 16, 1) * sin_hi)
    return cols[0] if len(cols) == 1 else jnp.concatenate(cols, axis=-1)


def _block_diag(blocks):
    n = len(blocks)
    w = blocks[0].shape[0]
    rows = []
    for k, blk in enumerate(blocks):
        parts = []
        if k > 0:
            parts.append(jnp.zeros((w, k * w), F32))
        parts.append(blk)
        if k < n - 1:
            parts.append(jnp.zeros((w, (n - 1 - k) * w), F32))
        rows.append(jnp.concatenate(parts, axis=-1))
    return jnp.concatenate(rows, axis=0)


def _rglru_gates(xc, wa, ba, wx, bx, lam):
    xb = xc.astype(BF16)
    r = _sigmoid(_dot(xb, wa) + ba)
    i = _sigmoid(_dot(xb, wx) + bx)
    softplus = jnp.maximum(-lam, 0.0) + jnp.log1p(jnp.exp(-jnp.abs(lam)))
    log_a = (-LRU_C) * r * softplus
    a = jnp.exp(log_a)
    b = jnp.sqrt(1.0 - a * a) * (i * xc)
    return a, b


def _block_prefix(a, b, reverse):
    t, w = a.shape
    a = a.reshape(t // 8, 8, w)
    b = b.reshape(t // 8, 8, w)
    row = lax.broadcasted_iota(jnp.int32, a.shape, 1)
    for d in (1, 2, 4):
        if reverse:
            a_s = pltpu.roll(a, 8 - d, 1)
            b_s = pltpu.roll(b, 8 - d, 1)
            ok = row < 8 - d
        else:
            a_s = pltpu.roll(a, d, 1)
            b_s = pltpu.roll(b, d, 1)
            ok = row >= d
        b = jnp.where(ok, a * b_s + b, b)
        a = jnp.where(ok, a * a_s, a)
    return a.reshape(t, w), b.reshape(t, w)


def _conv4(x, w_ref, b_row):
    t = x.shape[0]
    row = lax.broadcasted_iota(jnp.int32, x.shape, 0)
    xm2 = jnp.where(row >= 2, pltpu.roll(x, 2, 0), 0.0)
    xm1 = jnp.where(row >= 1, pltpu.roll(x, 1, 0), 0.0)
    xp1 = jnp.where(row < t - 1, pltpu.roll(x, t - 1, 0), 0.0)
    return (xm2 * w_ref[0:1, :] + xm1 * w_ref[1:2, :] + x * w_ref[2:3, :] + xp1 * w_ref[3:4, :]) + b_row


def _rglru_prepare(cx, cy, conv_w_ref, conv_b, gate_w_ref, ba_ref, bx_ref, lam_ref,
                   af_ref, bf_ref, ab_ref, bb_ref, gel_ref):
    xc = _conv4(cx, conv_w_ref, conv_b)
    a, b = _rglru_gates(xc, gate_w_ref[0], ba_ref[0:1, :], gate_w_ref[1], bx_ref[0:1, :], lam_ref[0:1, :])
    a, b = _block_prefix(a, b, reverse=False)
    af_ref[...] = a
    bf_ref[...] = b
    a, b = _rglru_gates(xc, gate_w_ref[2], ba_ref[1:2, :], gate_w_ref[3], bx_ref[1:2, :], lam_ref[1:2, :])
    a, b = _block_prefix(a, b, reverse=True)
    ab_ref[...] = a
    bb_ref[...] = b
    gel_ref[...] = _gelu_tanh(cy)


def _rglru_finish(h0f, h0b, af_ref, bf_ref, ab_ref, bb_ref, hf_ref, hb_ref, gel_ref):
    nblk = af_ref.shape[0] // 8

    def body(k, carry):
        cf, cb = carry
        rf = pl.ds(pl.multiple_of(k * 8, 8), 8)
        hf = bf_ref[rf, :] + af_ref[rf, :] * cf
        hf_ref[rf, :] = hf
        rb = pl.ds(pl.multiple_of((nblk - 1 - k) * 8, 8), 8)
        hb = bb_ref[rb, :] + ab_ref[rb, :] * cb
        hb_ref[rb, :] = hb
        return hf[7:8, :], hb[0:1, :]

    cf, cb = lax.fori_loop(0, nblk, body, (h0f, h0b))
    oc = (hf_ref[...] + hb_ref[...]) * gel_ref[...]
    return oc, cf, cb


def _store_gate_weights(gate_w_ref, wa_ref, wx_ref):
    for d in range(2):
        gate_w_ref[2 * d] = _block_diag([wa_ref[d, n] for n in range(N_HEADS)]).astype(BF16)
        gate_w_ref[2 * d + 1] = _block_diag([wx_ref[d, n] for n in range(N_HEADS)]).astype(BF16)


def _lane_head_masks(n):
    lane = lax.broadcasted_iota(jnp.int32, (1, n), 1) >> HEAD_SHIFT
    return [jnp.where(lane == h, 1.0, 0.0) for h in range(n // HEAD_DIM)]


def _log_decays(theta_ref, masks):
    theta = theta_ref[...]
    lanes = theta[:, 0:1] * masks[0]
    for h in range(1, N_HEADS):
        lanes = lanes + theta[:, h:h + 1] * masks[h]
    lg = jnp.log1p(-jnp.exp(lanes))
    return lg[0:1, :], lg[1:2, :]


RET_BLOCK = 256


def _retention(q, k8, vb, s0, lgf, lgb, masks, o_ref):
    t, w = q.shape
    c = RET_BLOCK
    nh = w // HEAD_DIM
    pos = lax.broadcasted_iota(jnp.int32, (c, w), 0).astype(F32)
    q_dec = (jnp.exp(lgf * (pos + 1.0)), jnp.exp(lgb * (float(c) - pos)))
    k_dec = (jnp.exp(lgf * (float(c - 1) - pos)), jnp.exp(lgb * pos))
    chunk_dec = (jnp.exp(lgf * float(c)), jnp.exp(lgb * float(c)))
    rel = (lax.broadcasted_iota(jnp.int32, (c, c), 0) - lax.broadcasted_iota(jnp.int32, (c, c), 1)).astype(F32)
    decs = []
    for h in range(nh):
        gf = lgf[:, h * HEAD_DIM:h * HEAD_DIM + 1]
        gb = lgb[:, h * HEAD_DIM:h * HEAD_DIM + 1]
        e = jnp.exp(jnp.where(rel >= 0, gf * rel, gb * (-rel)))
        decs.append(jnp.where(rel == 0, 2.0, e))
    dec = jnp.concatenate(decs, axis=0)
    r_head = lax.broadcasted_iota(jnp.int32, (w, w), 0) >> HEAD_SHIFT
    c_head = lax.broadcasted_iota(jnp.int32, (w, w), 1) >> HEAD_SHIFT
    same_head = jnp.where(r_head == c_head, 1.0, 0.0)
    states = [None, None] if s0 is None else list(s0)

    def carry(d, rows, o):
        if states[d] is not None:
            o = o + _dot((q[rows, :] * q_dec[d]).astype(BF16), states[d].astype(BF16))
        upd = _dot_tn((k8[rows, :] * k_dec[d]).astype(BF16), vb[rows, :]) * same_head
        states[d] = upd if states[d] is None else states[d] * chunk_dec[d] + upd
        return o

    for ci in range(t // c):
        rows = slice(ci * c, (ci + 1) * c)
        qc = q[rows, :]
        q_stack = jnp.concatenate([(qc * masks[h]).astype(BF16) for h in range(nh)], axis=0)
        inner = (_dot_nt(q_stack, k8[rows, :].astype(BF16)) * dec).astype(BF16)
        out = _dot(inner, vb[rows, :])
        o = out[0:c, :] * masks[0]
        for h in range(1, nh):
            o = o + out[h * c:(h + 1) * c, :] * masks[h]
        o_ref[rows, :] = carry(0, rows, o)
    for ci in reversed(range(t // c)):
        rows = slice(ci * c, (ci + 1) * c)
        if states[1] is not None:
            o_ref[rows, :] = carry(1, rows, o_ref[rows, :])
        else:
            carry(1, rows, None)
    return states[0], states[1]


def _ctx_mixer_kernel(*refs, layer, n_prev):
    prev_refs = refs[:n_prev]
    (x_ref, n2_ref, sh_ref, sc_ref, g2_ref, win_ref, wout_ref,
     aqn_ref, akn_ref, bqn_ref, bkn_ref, sink_ref,
     convw_ref, convb_ref, wa_ref, ba_ref, wx_ref, bx_ref, lam_ref, theta_ref, dn_ref,
     xn_ref, *state_refs) = refs[n_prev:n_prev + 28]
    (win_s, wout_s, u_ref, mixed_ref,
     gate_w_ref, af_ref, bf_ref, ab_ref, bb_ref, hf_ref, hb_ref, gel_ref, ret_ref) = refs[n_prev + 28:]
    t = SEQ
    lrow = slice(layer, layer + 1)
    for prev_ref, state_ref in zip(prev_refs, state_refs):
        for earlier in range(layer):
            state_ref[earlier] = prev_ref[earlier]
    ka_ref, va_ref, kb_ref, vb_ref, stc_ref, std_ref = (ref.at[layer] for ref in state_refs)

    @pl.when(pl.program_id(0) == 0)
    def _():
        for c in range(IN_WIDTH // 512):
            win_s[:, c * 512:(c + 1) * 512] = win_ref[:, c * 512:(c + 1) * 512].astype(BF16)
        wout_s[...] = wout_ref[...].astype(BF16)
        _store_gate_weights(gate_w_ref, wa_ref, wx_ref)

    x = x_ref[...]
    h = _norm_mod(x, n2_ref[lrow, :], sc_ref[0:1, :], sh_ref[0:1, :]).astype(BF16)
    c_cols = slice(COL_CX, COL_CX + 2 * GROUP_W)
    u_ref[:, c_cols] = _dot(h, win_s[:, c_cols])
    _rglru_prepare(u_ref[:, COL_CX:COL_CX + GROUP_W], u_ref[:, COL_CY:COL_CY + GROUP_W],
                   convw_ref, convb_ref[lrow, :], gate_w_ref, ba_ref, bx_ref, lam_ref,
                   af_ref, bf_ref, ab_ref, bb_ref, gel_ref)
    u_ref[:, 0:COL_CX] = _dot(h, win_s[:, 0:COL_CX])
    u_ref[:, COL_DQ:IN_WIDTH] = _dot(h, win_s[:, COL_DQ:IN_WIDTH])

    for (cq, ck, cv, qn_ref, kn_ref, k_out, v_out, col0, use_sink) in (
            (COL_AQ, COL_AK, COL_AV, aqn_ref, akn_ref, ka_ref, va_ref, 0, True),
            (COL_BQ, COL_BK, COL_BV, bqn_ref, bkn_ref, kb_ref, vb_ref, GROUP_W, False)):
        q = _head_norm(u_ref[:, cq:cq + 256], qn_ref[lrow, :])
        k = _head_norm(u_ref[:, ck:ck + 128], kn_ref[lrow, :])
        v = u_ref[:, cv:cv + 128]
        k_out[...] = k
        v_out[...] = v
        qs = q * (HEAD_DIM ** -0.5)
        heads = []
        for hd in range(N_HEADS):
            kv = hd // 2
            s = _dot_nt(_head_cols(qs, hd), _head_cols(k, kv))
            sink = jnp.full((t, 1), sink_ref[layer, hd], F32) if use_sink else None
            heads.append(_softmax_pv([s], [_head_cols(v, kv)], sink))
        mixed_ref[:, col0:col0 + GROUP_W] = jnp.concatenate(heads, axis=-1).astype(BF16)

    zero = jnp.zeros((1, GROUP_W), F32)
    oc, cf, cb = _rglru_finish(zero, zero, af_ref, bf_ref, ab_ref, bb_ref, hf_ref, hb_ref, gel_ref)
    mixed_ref[:, 2 * GROUP_W:3 * GROUP_W] = oc.astype(BF16)
    stc_ref[0:1, :] = cf
    stc_ref[1:2, :] = cb

    masks = _lane_head_masks(GROUP_W)
    lgf, lgb = _log_decays(theta_ref, masks)
    k8 = u_ref[:, COL_DK:COL_DK + GROUP_W] * (HEAD_DIM ** -0.5)
    vb = u_ref[:, COL_DV:COL_DV + GROUP_W].astype(BF16)
    final_states = _retention(u_ref[:, COL_DQ:COL_DQ + GROUP_W], k8, vb, None, lgf, lgb, masks, ret_ref)
    o = ret_ref[...]
    o = o * lax.rsqrt(_head_mean_square(o) + EPS) * dn_ref[lrow, :] * _silu(u_ref[:, COL_DG:COL_DG + GROUP_W])
    mixed_ref[:, 3 * GROUP_W:4 * GROUP_W] = o.astype(BF16)
    for d, s_full in enumerate(final_states):
        for hd in range(N_HEADS):
            std_ref[d, hd] = s_full[hd * 64:(hd + 1) * 64, hd * 64:(hd + 1) * 64]

    xn_ref[...] = x + g2_ref[0:1, :] * _dot(mixed_ref[...], wout_s[...])


def _ctx_mixers(x, mod, layer, prev, norm2_g, w_in, w_out,
                a_qn, a_kn, a_sink, b_qn, b_kn, c_conv_w, c_conv_b, c_wa, c_ba, c_wx, c_bx,
                c_lambda, d_theta, d_norm_g):
    per_request = lambda slots, shape: pl.BlockSpec((None, slots) + shape, lambda b: (b,) + (0,) * (1 + len(shape)))
    state_dims = [(SEQ, 128)] * 4 + [(2, GROUP_W), (2, N_HEADS, HEAD_DIM, HEAD_DIM)]
    scr = pltpu.VMEM((SEQ, GROUP_W), F32)
    out = pl.pallas_call(
        functools.partial(_ctx_mixer_kernel, layer=layer, n_prev=len(prev)),
        grid=(BATCH,),
        in_specs=[per_request(layer, dims) for dims in state_dims[:len(prev)]] + [
            pl.BlockSpec((SEQ, D_MODEL), lambda b: (b, 0)),
            _full((DEPTH, D_MODEL)),
            _mod_chunk(layer, 3), _mod_chunk(layer, 4), _mod_chunk(layer, 5),
            _once((None, D_MODEL, IN_WIDTH), lambda b: (layer, 0, 0)),
            _once((None, D_MODEL, D_MODEL), lambda b: (layer, 0, 0)),
            _full((DEPTH, HEAD_DIM)), _full((DEPTH, HEAD_DIM)), _full((DEPTH, HEAD_DIM)), _full((DEPTH, HEAD_DIM)),
            pl.BlockSpec(memory_space=pltpu.SMEM),
            _layer_block((4, GROUP_W), layer), _full((DEPTH, GROUP_W)),
            _layer_block((2, N_HEADS, HEAD_DIM, HEAD_DIM), layer), _layer_block((2, GROUP_W), layer),
            _layer_block((2, N_HEADS, HEAD_DIM, HEAD_DIM), layer), _layer_block((2, GROUP_W), layer),
            _layer_block((2, GROUP_W), layer),
            _layer_block((2, N_HEADS), layer), _full((DEPTH, GROUP_W)),
        ],
        out_specs=[pl.BlockSpec((SEQ, D_MODEL), lambda b: (b, 0))] + [
            per_request(layer + 1, dims) for dims in state_dims],
        out_shape=[jax.ShapeDtypeStruct((N_ROWS, D_MODEL), F32)] + [
            jax.ShapeDtypeStruct((BATCH, layer + 1) + dims, F32) for dims in state_dims],
        input_output_aliases={len(prev): 0},
        scratch_shapes=[pltpu.VMEM((D_MODEL, IN_WIDTH), BF16), pltpu.VMEM((D_MODEL, D_MODEL), BF16),
                        pltpu.VMEM((SEQ, IN_WIDTH), F32), pltpu.VMEM((SEQ, D_MODEL), BF16),
                        pltpu.VMEM((4, GROUP_W, GROUP_W), BF16)] + [scr] * 8,
        compiler_params=_cparams("arbitrary"),
        name="ctx_mixers",
    )(*prev, x, norm2_g, mod, mod, mod, w_in, w_out,
      a_qn, a_kn, b_qn, b_kn, a_sink, c_conv_w, c_conv_b, c_wa, c_ba, c_wx, c_bx,
      c_lambda, d_theta, d_norm_g)
    return out[0], tuple(out[1:])


LAT_BLOCK0 = N_CTX_ROWS // DEC_SEQ


def _lat_attn_kernel(x_ref, n2_ref, sh_ref, sc_ref, g2_ref, win_ref, wout_ref,
                     kca_ref, vca_ref, kcb_ref, vcb_ref,
                     aqn_ref, akn_ref, bqn_ref, bkn_ref, sink_ref, cos_ref, sinl_ref, sinh_ref,
                     xn_ref, h_ref, u_ref, o_ref, *, layer):
    t = DEC_SEQ
    lrow = slice(layer, layer + 1)
    mrow = pl.ds(1 + pl.program_id(0), 1)
    cos, sin_lo, sin_hi = cos_ref[...], sinl_ref[...], sinh_ref[...]
    scale = HEAD_DIM ** -0.5
    x = x_ref[...]
    h_ref[...] = _norm_mod(x, n2_ref[lrow, :], sc_ref[mrow, :], sh_ref[mrow, :]).astype(BF16)
    u_ref[...] = _dot(h_ref[...], win_ref[...].astype(BF16))

    q = _rope(_head_norm(u_ref[:, COL_AQ:COL_AQ + 256], aqn_ref[lrow, :]), cos, sin_lo, sin_hi)
    k = _rope(_head_norm(u_ref[:, COL_AK:COL_AK + 128], akn_ref[lrow, :]), cos, sin_lo, sin_hi)
    qh = [_head_cols(q * scale, h) for h in range(4)]
    v = u_ref[:, COL_AV:COL_AV + 128]
    kh = [_head_cols(k, kv) for kv in range(2)]
    vh = [_head_cols(v, kv) for kv in range(2)]
    kch = [_head_cols(kca_ref[...], kv) for kv in range(2)]
    vch = [_head_cols(vca_ref[...], kv) for kv in range(2)]
    w = ATT_BLOCK
    span = 3 * w
    for n in range(t // w):
        start = min(max((n - 1) * w, 0), t - span)
        rows = slice(n * w, (n + 1) * w)
        band = slice(start, start + span)
        qpos = (lax.broadcasted_iota(jnp.int32, (2 * w, span), 0) & (w - 1)) + n * w
        kpos = lax.broadcasted_iota(jnp.int32, (2 * w, span), 1) + start
        valid = jnp.abs(qpos - kpos) <= WINDOW
        heads = []
        for kv in range(2):
            qp = jnp.concatenate([qh[2 * kv][rows, :], qh[2 * kv + 1][rows, :]], axis=0)
            s_ctx = _dot_nt(qp, kch[kv])
            s_band = jnp.where(valid, _dot_nt(qp, kh[kv][band, :]), NEG_INF)
            row = lax.broadcasted_iota(jnp.int32, (2 * w, 1), 0)
            sink = jnp.where(row < w, sink_ref[layer, 2 * kv], sink_ref[layer, 2 * kv + 1])
            o = _softmax_pv([s_ctx, s_band], [vch[kv], vh[kv][band, :]], sink)
            heads += [o[0:w, :], o[w:2 * w, :]]
        o_ref[rows, 0:GROUP_W] = jnp.concatenate(heads, axis=-1).astype(BF16)

    q = _rope(_head_norm(u_ref[:, COL_BQ:COL_BQ + 256], bqn_ref[lrow, :]), cos, sin_lo, sin_hi)
    k = _rope(_head_norm(u_ref[:, COL_BK:COL_BK + 128], bkn_ref[lrow, :]), cos, sin_lo, sin_hi)
    qh = [_head_cols(q * scale, h) for h in range(4)]
    v = u_ref[:, COL_BV:COL_BV + 128]
    kh = [_head_cols(k, kv) for kv in range(2)]
    vh = [_head_cols(v, kv) for kv in range(2)]
    kch = [_head_cols(kcb_ref[...], kv) for kv in range(2)]
    vch = [_head_cols(vcb_ref[...], kv) for kv in range(2)]
    tq = 256
    for n in range(t // tq):
        rows = slice(n * tq, (n + 1) * tq)
        heads = []
        for kv in range(2):
            qp = jnp.concatenate([qh[2 * kv][rows, :], qh[2 * kv + 1][rows, :]], axis=0)
            o = _softmax_pv([_dot_nt(qp, kch[kv]), _dot_nt(qp, kh[kv])], [vch[kv], vh[kv]], None)
            heads += [o[0:tq, :], o[tq:2 * tq, :]]
        o_ref[rows, GROUP_W:2 * GROUP_W] = jnp.concatenate(heads, axis=-1).astype(BF16)

    xn_ref[...] = x + g2_ref[mrow, :] * _dot(o_ref[...], wout_ref[...].astype(BF16))


def _lat_recurrent_kernel(xn_in_ref, h_ref, g2_ref, wc_ref, wqk_ref, wvg_ref, wout_ref, h0_ref,
                          convw_ref, convb_ref, wa_ref, ba_ref, wx_ref, bx_ref, lam_ref,
                          s0_ref, theta_ref, dn_ref,
                          xn_ref, gate_w_ref, af_ref, bf_ref, ab_ref, bb_ref, hf_ref, hb_ref, gel_ref, ret_ref,
                          *, layer):
    lrow = slice(layer, layer + 1)
    mrow = pl.ds(1 + pl.program_id(0), 1)

    @pl.when(pl.program_id(0) == 0)
    def _():
        _store_gate_weights(gate_w_ref, wa_ref, wx_ref)

    h = h_ref[...]
    u = _dot(h, wc_ref[...].astype(BF16))
    _rglru_prepare(u[:, 0:GROUP_W], u[:, GROUP_W:2 * GROUP_W], convw_ref, convb_ref[lrow, :],
                   gate_w_ref, ba_ref, bx_ref, lam_ref, af_ref, bf_ref, ab_ref, bb_ref, gel_ref)
    oc, _, _ = _rglru_finish(h0_ref[0:1, :], h0_ref[1:2, :],
                             af_ref, bf_ref, ab_ref, bb_ref, hf_ref, hb_ref, gel_ref)
    y = _dot(oc.astype(BF16), wout_ref[0:GROUP_W, :].astype(BF16))

    uqk = _dot(h, wqk_ref[...].astype(BF16))
    uvg = _dot(h, wvg_ref[...].astype(BF16))
    masks = _lane_head_masks(GROUP_W)
    lgf, lgb = _log_decays(theta_ref, masks)
    s0 = tuple(_block_diag([s0_ref[d, hd] for hd in range(N_HEADS)]) for d in range(2))
    _retention(uqk[:, 0:GROUP_W], uqk[:, GROUP_W:2 * GROUP_W] * (HEAD_DIM ** -0.5),
               uvg[:, 0:GROUP_W].astype(BF16), s0, lgf, lgb, masks, ret_ref)
    o = ret_ref[...]
    o = o * lax.rsqrt(_head_mean_square(o) + EPS) * dn_ref[lrow, :] * _silu(uvg[:, GROUP_W:2 * GROUP_W])
    y = y + _dot(o.astype(BF16), wout_ref[GROUP_W:2 * GROUP_W, :].astype(BF16))
    xn_ref[...] = xn_in_ref[...] + g2_ref[mrow, :] * y


def _lat_mixers(x, mod, layer, caches, state_c, state_d, rope, norm2_g, w_in, w_out,
                a_qn, a_kn, a_sink, b_qn, b_kn, c_conv_w, c_conv_b, c_wa, c_ba, c_wx, c_bx,
                c_lambda, d_theta, d_norm_g):
    rows = pl.BlockSpec((DEC_SEQ, D_MODEL), lambda b: (LAT_BLOCK0 + b, 0))
    h_rows = pl.BlockSpec((DEC_SEQ, D_MODEL), lambda b: (b, 0))
    cache_spec = pl.BlockSpec((None, None, PAST_LEN, 128), lambda b: (b, layer, 0, 0))
    gain = _full((DEPTH, HEAD_DIM))
    table = _once((DEC_SEQ, 128), lambda b: (0, 0))
    out_shape = jax.ShapeDtypeStruct((N_ROWS, D_MODEL), F32)
    win_cols = lambda w, c: _once((None, D_MODEL, w), lambda b: (layer, 0, c))
    wout_rows = lambda h, r: _once((None, h, D_MODEL), lambda b: (layer, r, 0))

    xn, h = pl.pallas_call(
        functools.partial(_lat_attn_kernel, layer=layer),
        grid=(DEC_BATCH,),
        in_specs=[rows, _full((DEPTH, D_MODEL)),
                  _mod_chunk(layer, 3), _mod_chunk(layer, 4), _mod_chunk(layer, 5),
                  win_cols(4 * GROUP_W, 0), wout_rows(2 * GROUP_W, 0),
                  cache_spec, cache_spec, cache_spec, cache_spec,
                  gain, gain, gain, gain,
                  pl.BlockSpec(memory_space=pltpu.SMEM),
                  table, table, table],
        out_specs=[rows, pl.BlockSpec((DEC_SEQ, D_MODEL), lambda b: (b, 0), pipeline_mode=pl.Buffered(1))],
        out_shape=[out_shape, jax.ShapeDtypeStruct((N_LAT_ROWS, D_MODEL), BF16)],
        input_output_aliases={0: 0},
        scratch_shapes=[pltpu.VMEM((DEC_SEQ, 4 * GROUP_W), F32), pltpu.VMEM((DEC_SEQ, 2 * GROUP_W), BF16)],
        compiler_params=_cparams("arbitrary"),
        name="lat_attention",
    )(x, norm2_g, mod, mod, mod, w_in, w_out, *caches, a_qn, a_kn, b_qn, b_kn, a_sink, *rope)

    scr = pltpu.VMEM((DEC_SEQ, GROUP_W), F32)
    xn = pl.pallas_call(
        functools.partial(_lat_recurrent_kernel, layer=layer),
        grid=(DEC_BATCH,),
        in_specs=[
            rows, h_rows, _mod_chunk(layer, 5),
            win_cols(2 * GROUP_W, COL_CX // (2 * GROUP_W)),
            win_cols(2 * GROUP_W, COL_DQ // (2 * GROUP_W)), win_cols(2 * GROUP_W, COL_DV // (2 * GROUP_W)),
            wout_rows(2 * GROUP_W, 1),
            pl.BlockSpec((None, None, 2, GROUP_W), lambda b: (b, layer, 0, 0)),
            _layer_block((4, GROUP_W), layer), _full((DEPTH, GROUP_W)),
            _layer_block((2, N_HEADS, HEAD_DIM, HEAD_DIM), layer), _layer_block((2, GROUP_W), layer),
            _layer_block((2, N_HEADS, HEAD_DIM, HEAD_DIM), layer), _layer_block((2, GROUP_W), layer),
            _layer_block((2, GROUP_W), layer),
            pl.BlockSpec((None, None, 2, N_HEADS, HEAD_DIM, HEAD_DIM), lambda b: (b, layer, 0, 0, 0, 0)),
            _layer_block((2, N_HEADS), layer), _full((DEPTH, GROUP_W))],
        out_specs=rows,
        out_shape=out_shape,
        input_output_aliases={0: 0},
        scratch_shapes=[pltpu.VMEM((4, GROUP_W, GROUP_W), BF16)] + [scr] * 8,
        compiler_params=_cparams("arbitrary"),
        name="lat_recurrent",
    )(xn, h, mod, w_in, w_in, w_in, w_out, state_c, c_conv_w, c_conv_b, c_wa, c_ba, c_wx, c_bx, c_lambda,
      state_d, d_theta, d_norm_g)
    return xn


def _rope_tables():
    t = np.arange(DEC_SEQ)
    row = (t // GRID_W).astype(np.float64)[:, None]
    col = (t % GRID_W).astype(np.float64)[:, None]
    half = HEAD_DIM // 2
    inv = 1.0 / (ROPE_BASE ** (np.arange(0, half, 2, dtype=np.float64) / half))
    j = np.arange(128) % HEAD_DIM
    ang = np.where((j < half)[None, :], row, col) * inv[j % (half // 2)][None, :]
    first = ((j % half) < half // 2)[None, :]
    cos, sin = np.cos(ang), np.sin(ang)
    return tuple(jnp.asarray(a, F32) for a in (cos, np.where(first, -sin, 0.0), np.where(first, 0.0, sin)))


def kernel(x_prompt, x_sample, cache_a_k, cache_a_v, cache_b_k, cache_b_v, state_c, state_d, c, c_ctx, norm1_g, norm2_g, norm3_g, w_mod, b_mod, ffn1_wg, ffn1_wu, ffn1_wd, ffn2_wg, ffn2_wu, ffn2_wd, w_in, w_out, a_qn, a_kn, a_sink, b_qn, b_kn, c_conv_w, c_conv_b, c_wa, c_ba, c_wx, c_bx, c_lambda, d_theta, d_norm_g):
    mod = _modulation(c_ctx, c, w_mod, b_mod)
    rope = _rope_tables()
    caches = tuple(t.reshape(DEC_BATCH, DEPTH, PAST_LEN, 128) for t in (cache_a_k, cache_a_v, cache_b_k, cache_b_v))
    mixer_params = (a_qn, a_kn, a_sink, b_qn, b_kn, c_conv_w, c_conv_b, c_wa, c_ba, c_wx, c_bx,
                    c_lambda, d_theta, d_norm_g)
    xs = (x_prompt.reshape(N_CTX_ROWS, D_MODEL), x_sample.reshape(N_LAT_ROWS, D_MODEL))
    states = ()
    for l in range(DEPTH):
        (x,) = _ffn(xs, mod, l, 0, norm1_g, ffn1_wg, ffn1_wu, ffn1_wd)
        x, states = _ctx_mixers(x, mod, l, states, norm2_g, w_in, w_out, *mixer_params)
        x = _lat_mixers(x, mod, l, caches, state_c, state_d, rope, norm2_g, w_in, w_out, *mixer_params)
        xs = _ffn((x,), mod, l, 6, norm3_g, ffn2_wg, ffn2_wu, ffn2_wd, split_out=(l == DEPTH - 1))
    y_p, y_s = xs
    ka, va, kb, vb, st_c, st_d = states
    kv_shape = (BATCH, DEPTH, SEQ, 2, HEAD_DIM)
    return (y_p.reshape(BATCH, SEQ, D_MODEL), y_s.reshape(DEC_BATCH, DEC_SEQ, D_MODEL),
            ka.reshape(kv_shape), va.reshape(kv_shape), kb.reshape(kv_shape), vb.reshape(kv_shape),
            st_c, st_d)
```

```python
import functools
import math

import numpy as np
import jax
import jax.numpy as jnp
from jax import lax
from jax.experimental import pallas as pl
from jax.experimental.pallas import tpu as pltpu

F32 = jnp.float32
BF16 = jnp.bfloat16

D_MODEL = 1024
BATCH = 16
SEQ = 256
DEPTH = 2
DEC_BATCH = 2
DEC_SEQ = 1024
PAST_LEN = 512
GRID_W = 64
HEAD_DIM = 64
HEAD_SHIFT = 6
N_HEADS = 4
GROUP_W = 256
WINDOW = 128
ATT_BLOCK = 128
ROPE_BASE = 10000.0
LRU_C = 8.0
D_FF = 2816
N_MOD = 9
EPS = 1e-6
NEG_INF = -1e30
IN_WIDTH = 2560

N_CTX_ROWS = BATCH * SEQ
N_LAT_ROWS = DEC_BATCH * DEC_SEQ
N_ROWS = N_CTX_ROWS + N_LAT_ROWS
MOD_ROWS = 8
MOD_GROUP = 1024

VMEM_LIMIT_BYTES = 56 * 1024 * 1024

COL_AQ, COL_AK, COL_AV = 0, 256, 384
COL_BQ, COL_BK, COL_BV = 512, 768, 896
COL_CX, COL_CY = 1024, 1280
COL_DQ, COL_DK, COL_DV, COL_DG = 1536, 1792, 2048, 2304


def _cparams(*sem):
    return pltpu.CompilerParams(dimension_semantics=sem, vmem_limit_bytes=VMEM_LIMIT_BYTES)


def _dot(a, b):
    return jnp.dot(a, b, preferred_element_type=F32)


def _dot_nt(a, b):
    return lax.dot_general(a, b, (((1,), (1,)), ((), ())), preferred_element_type=F32)


def _dot_tn(a, b):
    return lax.dot_general(a, b, (((0,), (0,)), ((), ())), preferred_element_type=F32)


def _sigmoid(x):
    return 0.5 * jnp.tanh(0.5 * x) + 0.5


def _silu(x):
    return x * _sigmoid(x)


def _gelu_tanh(x):
    return 0.5 * x * (1.0 + jnp.tanh(math.sqrt(2.0 / math.pi) * (x + 0.044715 * (x * x * x))))


def _mod_row(i, tm, s):
    if tm >= MOD_GROUP:
        block_index = i * (tm // MOD_GROUP) + s
    else:
        block_index = i >> int(math.log2(MOD_GROUP // tm))
    return jnp.maximum(block_index - (N_CTX_ROWS // MOD_GROUP - 1), 0)


def _norm_mod(x, g, sc, sh):
    ms = jnp.mean(x * x, axis=-1, keepdims=True)
    return (x * lax.rsqrt(ms + EPS) * g) * (1.0 + sc) + sh


def _full(shape):
    return pl.BlockSpec(shape, lambda *_: (0,) * len(shape))


def _layer_block(shape, layer):
    return pl.BlockSpec((None,) + shape, lambda *_: (layer,) + (0,) * len(shape))


MOD_TN = 3072


def _mod_kernel(cc_ref, c_ref, w_ref, b_ref, o_ref):
    l = pl.program_id(0)
    pad = jnp.zeros((MOD_ROWS - 1 - DEC_BATCH, D_MODEL), F32)
    cond = jnp.concatenate([cc_ref[...], c_ref[...], pad], axis=0)
    o_ref[...] = _dot(_silu(cond).astype(BF16), w_ref[...].astype(BF16)) + b_ref[pl.ds(l, 1), :]


def _modulation(c_ctx, c, w_mod, b_mod):
    n = N_MOD * D_MODEL
    return pl.pallas_call(
        _mod_kernel,
        grid=(DEPTH, n // MOD_TN),
        in_specs=[
            pl.BlockSpec((1, D_MODEL), lambda l, j: (0, 0)),
            pl.BlockSpec((DEC_BATCH, D_MODEL), lambda l, j: (0, 0)),
            pl.BlockSpec((None, D_MODEL, MOD_TN), lambda l, j: (l, 0, j)),
            pl.BlockSpec((DEPTH, MOD_TN), lambda l, j: (0, j)),
        ],
        out_specs=pl.BlockSpec((None, MOD_ROWS, MOD_TN), lambda l, j: (l, 0, j)),
        out_shape=jax.ShapeDtypeStruct((DEPTH, MOD_ROWS, n), F32),
        compiler_params=_cparams("arbitrary", "arbitrary"),
        name="modulation",
    )(c_ctx.reshape(1, D_MODEL), c, w_mod, b_mod)


FFN_TM = 1024
FFN_TF = 256
N_CTX_TILES = N_CTX_ROWS // FFN_TM


FFN_NJ = D_FF // FFN_TF
N_FFN_TILES = N_ROWS // FFN_TM
N_FFN_STEPS = FFN_NJ + N_FFN_TILES


def _ffn_tile(step):
    return jnp.maximum(step - FFN_NJ, 0)


def _on_stream_part(tile, x_refs, o_refs, fn):
    if len(x_refs) == 1 and len(o_refs) == 1:
        fn(x_refs[0], o_refs[0])
    else:
        pl.when(tile < N_CTX_TILES)(lambda: fn(x_refs[0], o_refs[0]))
        pl.when(tile >= N_CTX_TILES)(lambda: fn(x_refs[-1], o_refs[-1]))


def _ffn_kernel(*refs, layer, n_in, n_out):
    x_refs = refs[:n_in]
    n_ref, sh_ref, sc_ref, g_ref, wg_ref, wu_ref, wd_ref = refs[n_in:n_in + 7]
    o_refs = refs[n_in + 7:n_in + 7 + n_out]
    h_ref, a_ref, wg_s, wu_s, wd_s = refs[n_in + 7 + n_out:]
    nj, tf = FFN_NJ, FFN_TF
    s = pl.program_id(0)
    tile = _ffn_tile(s)
    r = _mod_row(tile, FFN_TM, 0)

    def load_tile():
        def init(x_ref, _):
            h = _norm_mod(x_ref[...], n_ref[layer:layer + 1, :], sc_ref[pl.ds(r, 1), :], sh_ref[pl.ds(r, 1), :])
            h_ref[...] = h.astype(BF16)
        _on_stream_part(tile, x_refs, o_refs, init)

    def up_chunk(j, cols):
        h = h_ref[...]
        a_ref[:, cols] = (_silu(_dot(h, wg_s[j])) * _dot(h, wu_s[j])).astype(BF16)

    def down_and_store():
        y = (0.5 * g_ref[pl.ds(r, 1), :]) * _dot(a_ref[...], wd_s[...])

        def store(x_ref, o_ref):
            o_ref[...] = x_ref[...] + y
        _on_stream_part(tile, x_refs, o_refs, store)

    def keep_arrived_chunk():
        wg_s[s] = wg_ref[...].astype(BF16)
        wu_s[s] = wu_ref[...].astype(BF16)
        wd_s[pl.ds(pl.multiple_of(s * tf, tf), tf), :] = wd_ref[...].astype(BF16)

    def up_previous_chunk():
        up_chunk(s - 1, pl.ds(pl.multiple_of((s - 1) * tf, tf), tf))

    @pl.when(s == 0)
    def _():
        load_tile()
        keep_arrived_chunk()

    @pl.when((s > 0) & (s < nj))
    def _():
        up_previous_chunk()
        keep_arrived_chunk()

    @pl.when(s == nj)
    def _():
        up_previous_chunk()
        down_and_store()

    @pl.when(s > nj)
    def _():
        load_tile()
        for j in range(nj):
            up_chunk(j, slice(j * tf, (j + 1) * tf))
        down_and_store()


def _stream_specs(split, buffered_once):
    tm = FFN_TM
    kw = {"pipeline_mode": pl.Buffered(1)} if buffered_once else {}
    if not split:
        return [pl.BlockSpec((tm, D_MODEL), lambda s: (_ffn_tile(s), 0), **kw)]
    last_ctx = N_CTX_TILES - 1
    return [pl.BlockSpec((tm, D_MODEL), lambda s: (jnp.minimum(_ffn_tile(s), last_ctx), 0), **kw),
            pl.BlockSpec((tm, D_MODEL), lambda s: (jnp.maximum(_ffn_tile(s) - N_CTX_TILES, 0), 0), **kw)]


def _ffn(xs, mod, layer, chunk0, norm_g, wg, wu, wd, split_out=False):
    tm, tf, nj = FFN_TM, FFN_TF, FFN_NJ
    split_in = len(xs) == 2
    mod_spec = lambda c: pl.BlockSpec((None, MOD_ROWS, D_MODEL), lambda s: (layer, 0, c))
    w_col = lambda s: (layer, 0, jnp.minimum(s, nj - 1))
    w_row = lambda s: (layer, jnp.minimum(s, nj - 1), 0)
    if split_out:
        out_shape = [jax.ShapeDtypeStruct((N_CTX_ROWS, D_MODEL), F32),
                     jax.ShapeDtypeStruct((N_LAT_ROWS, D_MODEL), F32)]
    else:
        out_shape = [jax.ShapeDtypeStruct((N_ROWS, D_MODEL), F32)]
    out = pl.pallas_call(
        functools.partial(_ffn_kernel, layer=layer, n_in=len(xs), n_out=len(out_shape)),
        grid=(N_FFN_STEPS,),
        in_specs=_stream_specs(split_in, False) + [
            _full((DEPTH, D_MODEL)),
            mod_spec(chunk0), mod_spec(chunk0 + 1), mod_spec(chunk0 + 2),
            pl.BlockSpec((None, D_MODEL, tf), w_col),
            pl.BlockSpec((None, D_MODEL, tf), w_col),
            pl.BlockSpec((None, tf, D_MODEL), w_row),
        ],
        out_specs=_stream_specs(split_out, True),
        out_shape=out_shape,
        scratch_shapes=[pltpu.VMEM((tm, D_MODEL), BF16),
                        pltpu.VMEM((tm, D_FF), BF16),
                        pltpu.VMEM((nj, D_MODEL, tf), BF16),
                        pltpu.VMEM((nj, D_MODEL, tf), BF16),
                        pltpu.VMEM((D_FF, D_MODEL), BF16)],
        compiler_params=_cparams("arbitrary"),
        name="ffn",
    )(*xs, norm_g, mod, mod, mod, wg, wu, wd)
    return tuple(out)


def _once(shape, index_map):
    return pl.BlockSpec(shape, index_map, pipeline_mode=pl.Buffered(1))


def _mod_chunk(layer, c):
    return pl.BlockSpec((None, MOD_ROWS, D_MODEL), lambda *_: (layer, 0, c))


def _head_mean_square(x):
    n = x.shape[-1]
    r = lax.broadcasted_iota(jnp.int32, (n, n), 0) >> HEAD_SHIFT
    c = lax.broadcasted_iota(jnp.int32, (n, n), 1) >> HEAD_SHIFT
    ones_bd = jnp.where(r == c, 1.0, 0.0).astype(BF16)
    return _dot((x * x).astype(BF16), ones_bd) * (1.0 / HEAD_DIM)


def _head_norm(x, head_gain):
    gain_row = jnp.concatenate([head_gain] * (x.shape[-1] // HEAD_DIM), axis=-1)
    return x * lax.rsqrt(_head_mean_square(x) + EPS) * gain_row


def _head_cols(x, h):
    return x[:, h * HEAD_DIM:(h + 1) * HEAD_DIM].astype(BF16)


def _softmax_pv(scores, values, sink):
    m = jnp.max(scores[0], axis=-1, keepdims=True)
    for s in scores[1:]:
        m = jnp.maximum(m, jnp.max(s, axis=-1, keepdims=True))
    if sink is not None:
        m = jnp.maximum(m, sink)
    denom = None
    acc = None
    for s, v in zip(scores, values):
        p = jnp.exp(s - m)
        d = jnp.sum(p, axis=-1, keepdims=True)
        o = _dot(p.astype(BF16), v)
        denom = d if denom is None else denom + d
        acc = o if acc is None else acc + o
    if sink is not None:
        denom = denom + jnp.exp(sink - m)
    return acc / denom


def _rope(x, cos, sin_lo, sin_hi):
    cols = []
    for c in range(x.shape[-1] // 128):
        xc = x[:, c * 128:(c + 1) * 128]
        cols.append(xc * cos + pltpu.roll(xc, 112, 1) * sin_lo + pltpu.roll(xc, 16, 1) * sin_hi)
    return cols[0] if len(cols) == 1 else jnp.concatenate(cols, axis=-1)


def _block_diag(blocks):
    n = len(blocks)
    w = blocks[0].shape[0]
    rows = []
    for k, blk in enumerate(blocks):
        parts = []
        if k > 0:
            parts.append(jnp.zeros((w, k * w), F32))
        parts.append(blk)
        if k < n - 1:
            parts.append(jnp.zeros((w, (n - 1 - k) * w), F32))
        rows.append(jnp.concatenate(parts, axis=-1))
    return jnp.concatenate(rows, axis=0)


def _rglru_gates(xc, wa, ba, wx, bx, lam):
    xb = xc.astype(BF16)
    r = _sigmoid(_dot(xb, wa) + ba)
    i = _sigmoid(_dot(xb, wx) + bx)
    softplus = jnp.maximum(-lam, 0.0) + jnp.log1p(jnp.exp(-jnp.abs(lam)))
    log_a = (-LRU_C) * r * softplus
    a = jnp.exp(log_a)
    b = jnp.sqrt(1.0 - a * a) * (i * xc)
    return a, b


def _block_prefix(a, b, reverse):
    t = a.shape[0]
    row = lax.broadcasted_iota(jnp.int32, a.shape, 0) & 7
    for d in (1, 2, 4):
        if reverse:
            a_s = pltpu.roll(a, t - d, 0)
            b_s = pltpu.roll(b, t - d, 0)
            ok = row < 8 - d
        else:
            a_s = pltpu.roll(a, d, 0)
            b_s = pltpu.roll(b, d, 0)
            ok = row >= d
        b = jnp.where(ok, a * b_s + b, b)
        a = jnp.where(ok, a * a_s, a)
    return a, b


def _conv4(x, w_ref, b_row):
    t = x.shape[0]
    row = lax.broadcasted_iota(jnp.int32, x.shape, 0)
    xm2 = jnp.where(row >= 2, pltpu.roll(x, 2, 0), 0.0)
    xm1 = jnp.where(row >= 1, pltpu.roll(x, 1, 0), 0.0)
    xp1 = jnp.where(row < t - 1, pltpu.roll(x, t - 1, 0), 0.0)
    return (xm2 * w_ref[0:1, :] + xm1 * w_ref[1:2, :] + x * w_ref[2:3, :] + xp1 * w_ref[3:4, :]) + b_row


def _rglru_prepare(cx, cy, conv_w_ref, conv_b, gate_w_ref, ba_ref, bx_ref, lam_ref,
                   af_ref, bf_ref, ab_ref, bb_ref, gel_ref):
    xc = _conv4(cx, conv_w_ref, conv_b)
    a, b = _rglru_gates(xc, gate_w_ref[0], ba_ref[0:1, :], gate_w_ref[1], bx_ref[0:1, :], lam_ref[0:1, :])
    a, b = _block_prefix(a, b, reverse=False)
    af_ref[...] = a
    bf_ref[...] = b
    a, b = _rglru_gates(xc, gate_w_ref[2], ba_ref[1:2, :], gate_w_ref[3], bx_ref[1:2, :], lam_ref[1:2, :])
    a, b = _block_prefix(a, b, reverse=True)
    ab_ref[...] = a
    bb_ref[...] = b
    gel_ref[...] = _gelu_tanh(cy)


def _rglru_finish(h0f, h0b, af_ref, bf_ref, ab_ref, bb_ref, hf_ref, hb_ref, gel_ref):
    nblk = af_ref.shape[0] // 8

    def body(k, carry):
        cf, cb = carry
        rf = pl.ds(pl.multiple_of(k * 8, 8), 8)
        hf = bf_ref[rf, :] + af_ref[rf, :] * cf
        hf_ref[rf, :] = hf
        rb = pl.ds(pl.multiple_of((nblk - 1 - k) * 8, 8), 8)
        hb = bb_ref[rb, :] + ab_ref[rb, :] * cb
        hb_ref[rb, :] = hb
        return hf[7:8, :], hb[0:1, :]

    cf, cb = lax.fori_loop(0, nblk, body, (h0f, h0b), unroll=8)
    oc = (hf_ref[...] + hb_ref[...]) * gel_ref[...]
    return oc, cf, cb


def _store_gate_weights(gate_w_ref, wa_ref, wx_ref):
    for d in range(2):
        gate_w_ref[2 * d] = _block_diag([wa_ref[d, n] for n in range(N_HEADS)]).astype(BF16)
        gate_w_ref[2 * d + 1] = _block_diag([wx_ref[d, n] for n in range(N_HEADS)]).astype(BF16)


def _lane_head_masks(n):
    lane = lax.broadcasted_iota(jnp.int32, (1, n), 1) >> HEAD_SHIFT
    return [jnp.where(lane == h, 1.0, 0.0) for h in range(n // HEAD_DIM)]


def _log_decays(theta_ref, masks):
    theta = theta_ref[...]
    lanes = theta[:, 0:1] * masks[0]
    for h in range(1, N_HEADS):
        lanes = lanes + theta[:, h:h + 1] * masks[h]
    lg = jnp.log1p(-jnp.exp(lanes))
    return lg[0:1, :], lg[1:2, :]


RET_BLOCK = 256


def _retention(q, k8, vb, s0, lgf, lgb, masks, o_ref):
    t, w = q.shape
    c = RET_BLOCK
    nh = w // HEAD_DIM
    pos = lax.broadcasted_iota(jnp.int32, (c, w), 0).astype(F32)
    q_dec = (jnp.exp(lgf * (pos + 1.0)), jnp.exp(lgb * (float(c) - pos)))
    k_dec = (jnp.exp(lgf * (float(c - 1) - pos)), jnp.exp(lgb * pos))
    chunk_dec = (jnp.exp(lgf * float(c)), jnp.exp(lgb * float(c)))
    rel = (lax.broadcasted_iota(jnp.int32, (c, c), 0) - lax.broadcasted_iota(jnp.int32, (c, c), 1)).astype(F32)
    decs = []
    for h in range(nh):
        gf = lgf[:, h * HEAD_DIM:h * HEAD_DIM + 1]
        gb = lgb[:, h * HEAD_DIM:h * HEAD_DIM + 1]
        e = jnp.exp(jnp.where(rel >= 0, gf * rel, gb * (-rel)))
        decs.append(jnp.where(rel == 0, 2.0, e))
    dec = jnp.concatenate(decs, axis=0)
    r_head = lax.broadcasted_iota(jnp.int32, (w, w), 0) >> HEAD_SHIFT
    c_head = lax.broadcasted_iota(jnp.int32, (w, w), 1) >> HEAD_SHIFT
    same_head = jnp.where(r_head == c_head, 1.0, 0.0)
    states = [None, None] if s0 is None else list(s0)

    def carry(d, rows, o):
        if states[d] is not None:
            o = o + _dot((q[rows, :] * q_dec[d]).astype(BF16), states[d].astype(BF16))
        upd = _dot_tn((k8[rows, :] * k_dec[d]).astype(BF16), vb[rows, :]) * same_head
        states[d] = upd if states[d] is None else states[d] * chunk_dec[d] + upd
        return o

    for ci in range(t // c):
        rows = slice(ci * c, (ci + 1) * c)
        qc = q[rows, :]
        q_stack = jnp.concatenate([(qc * masks[h]).astype(BF16) for h in range(nh)], axis=0)
        inner = (_dot_nt(q_stack, k8[rows, :].astype(BF16)) * dec).astype(BF16)
        out = _dot(inner, vb[rows, :])
        o = out[0:c, :] * masks[0]
        for h in range(1, nh):
            o = o + out[h * c:(h + 1) * c, :] * masks[h]
        o_ref[rows, :] = carry(0, rows, o)
    for ci in reversed(range(t // c)):
        rows = slice(ci * c, (ci + 1) * c)
        if states[1] is not None:
            o_ref[rows, :] = carry(1, rows, o_ref[rows, :])
        else:
            carry(1, rows, None)
    return states[0], states[1]


def _ctx_mixer_kernel(*refs, layer, n_prev):
    prev_refs = refs[:n_prev]
    (x_ref, n2_ref, sh_ref, sc_ref, g2_ref, win_ref, wout_ref,
     aqn_ref, akn_ref, bqn_ref, bkn_ref, sink_ref,
     convw_ref, convb_ref, wa_ref, ba_ref, wx_ref, bx_ref, lam_ref, theta_ref, dn_ref,
     xn_ref, *state_refs) = refs[n_prev:n_prev + 28]
    (win_s, wout_s, u_ref, mixed_ref,
     gate_w_ref, af_ref, bf_ref, ab_ref, bb_ref, hf_ref, hb_ref, gel_ref, ret_ref) = refs[n_prev + 28:]
    t = SEQ
    lrow = slice(layer, layer + 1)
    for prev_ref, state_ref in zip(prev_refs, state_refs):
        for earlier in range(layer):
            state_ref[earlier] = prev_ref[earlier]
    ka_ref, va_ref, kb_ref, vb_ref, stc_ref, std_ref = (ref.at[layer] for ref in state_refs)

    @pl.when(pl.program_id(0) == 0)
    def _():
        for c in range(IN_WIDTH // 512):
            win_s[:, c * 512:(c + 1) * 512] = win_ref[:, c * 512:(c + 1) * 512].astype(BF16)
        wout_s[...] = wout_ref[...].astype(BF16)
        _store_gate_weights(gate_w_ref, wa_ref, wx_ref)

    x = x_ref[...]
    h = _norm_mod(x, n2_ref[lrow, :], sc_ref[0:1, :], sh_ref[0:1, :]).astype(BF16)
    c_cols = slice(COL_CX, COL_CX + 2 * GROUP_W)
    u_ref[:, c_cols] = _dot(h, win_s[:, c_cols])
    _rglru_prepare(u_ref[:, COL_CX:COL_CX + GROUP_W], u_ref[:, COL_CY:COL_CY + GROUP_W],
                   convw_ref, convb_ref[lrow, :], gate_w_ref, ba_ref, bx_ref, lam_ref,
                   af_ref, bf_ref, ab_ref, bb_ref, gel_ref)
    u_ref[:, 0:COL_CX] = _dot(h, win_s[:, 0:COL_CX])
    u_ref[:, COL_DQ:IN_WIDTH] = _dot(h, win_s[:, COL_DQ:IN_WIDTH])

    for (cq, ck, cv, qn_ref, kn_ref, k_out, v_out, col0, use_sink) in (
            (COL_AQ, COL_AK, COL_AV, aqn_ref, akn_ref, ka_ref, va_ref, 0, True),
            (COL_BQ, COL_BK, COL_BV, bqn_ref, bkn_ref, kb_ref, vb_ref, GROUP_W, False)):
        q = _head_norm(u_ref[:, cq:cq + 256], qn_ref[lrow, :])
        k = _head_norm(u_ref[:, ck:ck + 128], kn_ref[lrow, :])
        v = u_ref[:, cv:cv + 128]
        k_out[...] = k
        v_out[...] = v
        qs = q * (HEAD_DIM ** -0.5)
        heads = []
        for hd in range(N_HEADS):
            kv = hd // 2
            s = _dot_nt(_head_cols(qs, hd), _head_cols(k, kv))
            sink = jnp.full((t, 1), sink_ref[layer, hd], F32) if use_sink else None
            heads.append(_softmax_pv([s], [_head_cols(v, kv)], sink))
        mixed_ref[:, col0:col0 + GROUP_W] = jnp.concatenate(heads, axis=-1).astype(BF16)

    zero = jnp.zeros((1, GROUP_W), F32)
    oc, cf, cb = _rglru_finish(zero, zero, af_ref, bf_ref, ab_ref, bb_ref, hf_ref, hb_ref, gel_ref)
    mixed_ref[:, 2 * GROUP_W:3 * GROUP_W] = oc.astype(BF16)
    stc_ref[0:1, :] = cf
    stc_ref[1:2, :] = cb

    masks = _lane_head_masks(GROUP_W)
    lgf, lgb = _log_decays(theta_ref, masks)
    k8 = u_ref[:, COL_DK:COL_DK + GROUP_W] * (HEAD_DIM ** -0.5)
    vb = u_ref[:, COL_DV:COL_DV + GROUP_W].astype(BF16)
    final_states = _retention(u_ref[:, COL_DQ:COL_DQ + GROUP_W], k8, vb, None, lgf, lgb, masks, ret_ref)
    o = ret_ref[...]
    o = o * lax.rsqrt(_head_mean_square(o) + EPS) * dn_ref[lrow, :] * _silu(u_ref[:, COL_DG:COL_DG + GROUP_W])
    mixed_ref[:, 3 * GROUP_W:4 * GROUP_W] = o.astype(BF16)
    for d, s_full in enumerate(final_states):
        for hd in range(N_HEADS):
            std_ref[d, hd] = s_full[hd * 64:(hd + 1) * 64, hd * 64:(hd + 1) * 64]

    xn_ref[...] = x + g2_ref[0:1, :] * _dot(mixed_ref[...], wout_s[...])


def _ctx_mixers(x, mod, layer, prev, norm2_g, w_in, w_out,
                a_qn, a_kn, a_sink, b_qn, b_kn, c_conv_w, c_conv_b, c_wa, c_ba, c_wx, c_bx,
                c_lambda, d_theta, d_norm_g):
    per_request = lambda slots, shape: pl.BlockSpec((None, slots) + shape, lambda b: (b,) + (0,) * (1 + len(shape)))
    state_dims = [(SEQ, 128)] * 4 + [(2, GROUP_W), (2, N_HEADS, HEAD_DIM, HEAD_DIM)]
    scr = pltpu.VMEM((SEQ, GROUP_W), F32)
    out = pl.pallas_call(
        functools.partial(_ctx_mixer_kernel, layer=layer, n_prev=len(prev)),
        grid=(BATCH,),
        in_specs=[per_request(layer, dims) for dims in state_dims[:len(prev)]] + [
            pl.BlockSpec((SEQ, D_MODEL), lambda b: (b, 0)),
            _full((DEPTH, D_MODEL)),
            _mod_chunk(layer, 3), _mod_chunk(layer, 4), _mod_chunk(layer, 5),
            _once((None, D_MODEL, IN_WIDTH), lambda b: (layer, 0, 0)),
            _once((None, D_MODEL, D_MODEL), lambda b: (layer, 0, 0)),
            _full((DEPTH, HEAD_DIM)), _full((DEPTH, HEAD_DIM)), _full((DEPTH, HEAD_DIM)), _full((DEPTH, HEAD_DIM)),
            pl.BlockSpec(memory_space=pltpu.SMEM),
            _layer_block((4, GROUP_W), layer), _full((DEPTH, GROUP_W)),
            _layer_block((2, N_HEADS, HEAD_DIM, HEAD_DIM), layer), _layer_block((2, GROUP_W), layer),
            _layer_block((2, N_HEADS, HEAD_DIM, HEAD_DIM), layer), _layer_block((2, GROUP_W), layer),
            _layer_block((2, GROUP_W), layer),
            _layer_block((2, N_HEADS), layer), _full((DEPTH, GROUP_W)),
        ],
        out_specs=[pl.BlockSpec((SEQ, D_MODEL), lambda b: (b, 0))] + [
            per_request(layer + 1, dims) for dims in state_dims],
        out_shape=[jax.ShapeDtypeStruct((N_ROWS, D_MODEL), F32)] + [
            jax.ShapeDtypeStruct((BATCH, layer + 1) + dims, F32) for dims in state_dims],
        input_output_aliases={len(prev): 0},
        scratch_shapes=[pltpu.VMEM((D_MODEL, IN_WIDTH), BF16), pltpu.VMEM((D_MODEL, D_MODEL), BF16),
                        pltpu.VMEM((SEQ, IN_WIDTH), F32), pltpu.VMEM((SEQ, D_MODEL), BF16),
                        pltpu.VMEM((4, GROUP_W, GROUP_W), BF16)] + [scr] * 8,
        compiler_params=_cparams("arbitrary"),
        name="ctx_mixers",
    )(*prev, x, norm2_g, mod, mod, mod, w_in, w_out,
      a_qn, a_kn, b_qn, b_kn, a_sink, c_conv_w, c_conv_b, c_wa, c_ba, c_wx, c_bx,
      c_lambda, d_theta, d_norm_g)
    return out[0], tuple(out[1:])


LAT_BLOCK0 = N_CTX_ROWS // DEC_SEQ


def _lat_attn_kernel(x_ref, n2_ref, sh_ref, sc_ref, g2_ref, win_ref, wout_ref,
                     kca_ref, vca_ref, kcb_ref, vcb_ref,
                     aqn_ref, akn_ref, bqn_ref, bkn_ref, sink_ref, cos_ref, sinl_ref, sinh_ref,
                     xn_ref, h_ref, u_ref, o_ref, *, layer):
    t = DEC_SEQ
    lrow = slice(layer, layer + 1)
    mrow = pl.ds(1 + pl.program_id(0), 1)
    cos, sin_lo, sin_hi = cos_ref[...], sinl_ref[...], sinh_ref[...]
    scale = HEAD_DIM ** -0.5
    x = x_ref[...]
    h_ref[...] = _norm_mod(x, n2_ref[lrow, :], sc_ref[mrow, :], sh_ref[mrow, :]).astype(BF16)
    u_ref[...] = _dot(h_ref[...], win_ref[...].astype(BF16))

    q = _rope(_head_norm(u_ref[:, COL_AQ:COL_AQ + 256], aqn_ref[lrow, :]), cos, sin_lo, sin_hi)
    k = _rope(_head_norm(u_ref[:, COL_AK:COL_AK + 128], akn_ref[lrow, :]), cos, sin_lo, sin_hi)
    qh = [_head_cols(q * scale, h) for h in range(4)]
    v = u_ref[:, COL_AV:COL_AV + 128]
    kh = [_head_cols(k, kv) for kv in range(2)]
    vh = [_head_cols(v, kv) for kv in range(2)]
    kch = [_head_cols(kca_ref[...], kv) for kv in range(2)]
    vch = [_head_cols(vca_ref[...], kv) for kv in range(2)]
    w = ATT_BLOCK
    span = 3 * w
    for n in range(t // w):
        start = min(max((n - 1) * w, 0), t - span)
        rows = slice(n * w, (n + 1) * w)
        band = slice(start, start + span)
        qpos = (lax.broadcasted_iota(jnp.int32, (2 * w, span), 0) & (w - 1)) + n * w
        kpos = lax.broadcasted_iota(jnp.int32, (2 * w, span), 1) + start
        valid = jnp.abs(qpos - kpos) <= WINDOW
        heads = []
        for kv in range(2):
            qp = jnp.concatenate([qh[2 * kv][rows, :], qh[2 * kv + 1][rows, :]], axis=0)
            s_ctx = _dot_nt(qp, kch[kv])
            s_band = jnp.where(valid, _dot_nt(qp, kh[kv][band, :]), NEG_INF)
            row = lax.broadcasted_iota(jnp.int32, (2 * w, 1), 0)
            sink = jnp.where(row < w, sink_ref[layer, 2 * kv], sink_ref[layer, 2 * kv + 1])
            o = _softmax_pv([s_ctx, s_band], [vch[kv], vh[kv][band, :]], sink)
            heads += [o[0:w, :], o[w:2 * w, :]]
        o_ref[rows, 0:GROUP_W] = jnp.concatenate(heads, axis=-1).astype(BF16)

    q = _rope(_head_norm(u_ref[:, COL_BQ:COL_BQ + 256], bqn_ref[lrow, :]), cos, sin_lo, sin_hi)
    k = _rope(_head_norm(u_ref[:, COL_BK:COL_BK + 128], bkn_ref[lrow, :]), cos, sin_lo, sin_hi)
    qh = [_head_cols(q * scale, h) for h in range(4)]
    v = u_ref[:, COL_BV:COL_BV + 128]
    kh = [_head_cols(k, kv) for kv in range(2)]
    vh = [_head_cols(v, kv) for kv in range(2)]
    kch = [_head_cols(kcb_ref[...], kv) for kv in range(2)]
    vch = [_head_cols(vcb_ref[...], kv) for kv in range(2)]
    tq = 256
    for n in range(t // tq):
        rows = slice(n * tq, (n + 1) * tq)
        heads = []
        for kv in range(2):
            qp = jnp.concatenate([qh[2 * kv][rows, :], qh[2 * kv + 1][rows, :]], axis=0)
            o = _softmax_pv([_dot_nt(qp, kch[kv]), _dot_nt(qp, kh[kv])], [vch[kv], vh[kv]], None)
            heads += [o[0:tq, :], o[tq:2 * tq, :]]
        o_ref[rows, GROUP_W:2 * GROUP_W] = jnp.concatenate(heads, axis=-1).astype(BF16)

    xn_ref[...] = x + g2_ref[mrow, :] * _dot(o_ref[...], wout_ref[...].astype(BF16))


def _lat_recurrent_kernel(xn_in_ref, h_ref, g2_ref, wc_ref, wqk_ref, wvg_ref, wout_ref, h0_ref,
                          convw_ref, convb_ref, wa_ref, ba_ref, wx_ref, bx_ref, lam_ref,
                          s0_ref, theta_ref, dn_ref,
                          xn_ref, gate_w_ref, af_ref, bf_ref, ab_ref, bb_ref, hf_ref, hb_ref, gel_ref, ret_ref,
                          *, layer):
    lrow = slice(layer, layer + 1)
    mrow = pl.ds(1 + pl.program_id(0), 1)

    @pl.when(pl.program_id(0) == 0)
    def _():
        _store_gate_weights(gate_w_ref, wa_ref, wx_ref)

    h = h_ref[...]
    u = _dot(h, wc_ref[...].astype(BF16))
    _rglru_prepare(u[:, 0:GROUP_W], u[:, GROUP_W:2 * GROUP_W], convw_ref, convb_ref[lrow, :],
                   gate_w_ref, ba_ref, bx_ref, lam_ref, af_ref, bf_ref, ab_ref, bb_ref, gel_ref)
    oc, _, _ = _rglru_finish(h0_ref[0:1, :], h0_ref[1:2, :],
                             af_ref, bf_ref, ab_ref, bb_ref, hf_ref, hb_ref, gel_ref)
    y = _dot(oc.astype(BF16), wout_ref[0:GROUP_W, :].astype(BF16))

    uqk = _dot(h, wqk_ref[...].astype(BF16))
    uvg = _dot(h, wvg_ref[...].astype(BF16))
    masks = _lane_head_masks(GROUP_W)
    lgf, lgb = _log_decays(theta_ref, masks)
    s0 = tuple(_block_diag([s0_ref[d, hd] for hd in range(N_HEADS)]) for d in range(2))
    _retention(uqk[:, 0:GROUP_W], uqk[:, GROUP_W:2 * GROUP_W] * (HEAD_DIM ** -0.5),
               uvg[:, 0:GROUP_W].astype(BF16), s0, lgf, lgb, masks, ret_ref)
    o = ret_ref[...]
    o = o * lax.rsqrt(_head_mean_square(o) + EPS) * dn_ref[lrow, :] * _silu(uvg[:, GROUP_W:2 * GROUP_W])
    y = y + _dot(o.astype(BF16), wout_ref[GROUP_W:2 * GROUP_W, :].astype(BF16))
    xn_ref[...] = xn_in_ref[...] + g2_ref[mrow, :] * y


def _lat_mixers(x, mod, layer, caches, state_c, state_d, rope, norm2_g, w_in, w_out,
                a_qn, a_kn, a_sink, b_qn, b_kn, c_conv_w, c_conv_b, c_wa, c_ba, c_wx, c_bx,
                c_lambda, d_theta, d_norm_g):
    rows = pl.BlockSpec((DEC_SEQ, D_MODEL), lambda b: (LAT_BLOCK0 + b, 0))
    h_rows = pl.BlockSpec((DEC_SEQ, D_MODEL), lambda b: (b, 0))
    cache_spec = pl.BlockSpec((None, None, PAST_LEN, 128), lambda b: (b, layer, 0, 0))
    gain = _full((DEPTH, HEAD_DIM))
    table = _once((DEC_SEQ, 128), lambda b: (0, 0))
    out_shape = jax.ShapeDtypeStruct((N_ROWS, D_MODEL), F32)
    win_cols = lambda w, c: _once((None, D_MODEL, w), lambda b: (layer, 0, c))
    wout_rows = lambda h, r: _once((None, h, D_MODEL), lambda b: (layer, r, 0))

    xn, h = pl.pallas_call(
        functools.partial(_lat_attn_kernel, layer=layer),
        grid=(DEC_BATCH,),
        in_specs=[rows, _full((DEPTH, D_MODEL)),
                  _mod_chunk(layer, 3), _mod_chunk(layer, 4), _mod_chunk(layer, 5),
                  win_cols(4 * GROUP_W, 0), wout_rows(2 * GROUP_W, 0),
                  cache_spec, cache_spec, cache_spec, cache_spec,
                  gain, gain, gain, gain,
                  pl.BlockSpec(memory_space=pltpu.SMEM),
                  table, table, table],
        out_specs=[rows, pl.BlockSpec((DEC_SEQ, D_MODEL), lambda b: (b, 0), pipeline_mode=pl.Buffered(1))],
        out_shape=[out_shape, jax.ShapeDtypeStruct((N_LAT_ROWS, D_MODEL), BF16)],
        input_output_aliases={0: 0},
        scratch_shapes=[pltpu.VMEM((DEC_SEQ, 4 * GROUP_W), F32), pltpu.VMEM((DEC_SEQ, 2 * GROUP_W), BF16)],
        compiler_params=_cparams("arbitrary"),
        name="lat_attention",
    )(x, norm2_g, mod, mod, mod, w_in, w_out, *caches, a_qn, a_kn, b_qn, b_kn, a_sink, *rope)

    scr = pltpu.VMEM((DEC_SEQ, GROUP_W), F32)
    xn = pl.pallas_call(
        functools.partial(_lat_recurrent_kernel, layer=layer),
        grid=(DEC_BATCH,),
        in_specs=[
            rows, h_rows, _mod_chunk(layer, 5),
            win_cols(2 * GROUP_W, COL_CX // (2 * GROUP_W)),
            win_cols(2 * GROUP_W, COL_DQ // (2 * GROUP_W)), win_cols(2 * GROUP_W, COL_DV // (2 * GROUP_W)),
            wout_rows(2 * GROUP_W, 1),
            pl.BlockSpec((None, None, 2, GROUP_W), lambda b: (b, layer, 0, 0)),
            _layer_block((4, GROUP_W), layer), _full((DEPTH, GROUP_W)),
            _layer_block((2, N_HEADS, HEAD_DIM, HEAD_DIM), layer), _layer_block((2, GROUP_W), layer),
            _layer_block((2, N_HEADS, HEAD_DIM, HEAD_DIM), layer), _layer_block((2, GROUP_W), layer),
            _layer_block((2, GROUP_W), layer),
            pl.BlockSpec((None, None, 2, N_HEADS, HEAD_DIM, HEAD_DIM), lambda b: (b, layer, 0, 0, 0, 0)),
            _layer_block((2, N_HEADS), layer), _full((DEPTH, GROUP_W))],
        out_specs=rows,
        out_shape=out_shape,
        input_output_aliases={0: 0},
        scratch_shapes=[pltpu.VMEM((4, GROUP_W, GROUP_W), BF16)] + [scr] * 8,
        compiler_params=_cparams("arbitrary"),
        name="lat_recurrent",
    )(xn, h, mod, w_in, w_in, w_in, w_out, state_c, c_conv_w, c_conv_b, c_wa, c_ba, c_wx, c_bx, c_lambda,
      state_d, d_theta, d_norm_g)
    return xn


def _rope_tables():
    t = np.arange(DEC_SEQ)
    row = (t // GRID_W).astype(np.float64)[:, None]
    col = (t % GRID_W).astype(np.float64)[:, None]
    half = HEAD_DIM // 2
    inv = 1.0 / (ROPE_BASE ** (np.arange(0, half, 2, dtype=np.float64) / half))
    j = np.arange(128) % HEAD_DIM
    ang = np.where((j < half)[None, :], row, col) * inv[j % (half // 2)][None, :]
    first = ((j % half) < half // 2)[None, :]
    cos, sin = np.cos(ang), np.sin(ang)
    return tuple(jnp.asarray(a, F32) for a in (cos, np.where(first, -sin, 0.0), np.where(first, 0.0, sin)))


def kernel(x_prompt, x_sample, cache_a_k, cache_a_v, cache_b_k, cache_b_v, state_c, state_d, c, c_ctx, norm1_g, norm2_g, norm3_g, w_mod, b_mod, ffn1_wg, ffn1_wu, ffn1_wd, ffn2_wg, ffn2_wu, ffn2_wd, w_in, w_out, a_qn, a_kn, a_sink, b_qn, b_kn, c_conv_w, c_conv_b, c_wa, c_ba, c_wx, c_bx, c_lambda, d_theta, d_norm_g):
    mod = _modulation(c_ctx, c, w_mod, b_mod)
    rope = _rope_tables()
    caches = tuple(t.reshape(DEC_BATCH, DEPTH, PAST_LEN, 128) for t in (cache_a_k, cache_a_v, cache_b_k, cache_b_v))
    mixer_params = (a_qn, a_kn, a_sink, b_qn, b_kn, c_conv_w, c_conv_b, c_wa, c_ba, c_wx, c_bx,
                    c_lambda, d_theta, d_norm_g)
    xs = (x_prompt.reshape(N_CTX_ROWS, D_MODEL), x_sample.reshape(N_LAT_ROWS, D_MODEL))
    states = ()
    for l in range(DEPTH):
        (x,) = _ffn(xs, mod, l, 0, norm1_g, ffn1_wg, ffn1_wu, ffn1_wd)
        x, states = _ctx_mixers(x, mod, l, states, norm2_g, w_in, w_out, *mixer_params)
        x = _lat_mixers(x, mod, l, caches, state_c, state_d, rope, norm2_g, w_in, w_out, *mixer_params)
        xs = _ffn((x,), mod, l, 6, norm3_g, ffn2_wg, ffn2_wu, ffn2_wd, split_out=(l == DEPTH - 1))
    y_p, y_s = xs
    ka, va, kb, vb, st_c, st_d = states
    kv_shape = (BATCH, DEPTH, SEQ, 2, HEAD_DIM)
    return (y_p.reshape(BATCH, SEQ, D_MODEL), y_s.reshape(DEC_BATCH, DEC_SEQ, D_MODEL),
            ka.reshape(kv_shape), va.reshape(kv_shape), kb.reshape(kv_shape), vb.reshape(kv_shape),
            st_c, st_d)
```

```python
import functools
import math

import numpy as np
import jax
import jax.numpy as jnp
from jax import lax
from jax.experimental import pallas as pl
from jax.experimental.pallas import tpu as pltpu

F32 = jnp.float32
BF16 = jnp.bfloat16

D_MODEL = 1024
BATCH = 16
SEQ = 256
DEPTH = 2
DEC_BATCH = 2
DEC_SEQ = 1024
PAST_LEN = 512
GRID_W = 64
HEAD_DIM = 64
HEAD_SHIFT = 6
N_HEADS = 4
GROUP_W = 256
WINDOW = 128
ATT_BLOCK = 128
ROPE_BASE = 10000.0
LRU_C = 8.0
D_FF = 2816
N_MOD = 9
EPS = 1e-6
NEG_INF = -1e30
IN_WIDTH = 2560

N_CTX_ROWS = BATCH * SEQ
N_LAT_ROWS = DEC_BATCH * DEC_SEQ
N_ROWS = N_CTX_ROWS + N_LAT_ROWS
MOD_ROWS = 8
MOD_GROUP = 1024

VMEM_LIMIT_BYTES = 56 * 1024 * 1024

COL_AQ, COL_AK, COL_AV = 0, 256, 384
COL_BQ, COL_BK, COL_BV = 512, 768, 896
COL_CX, COL_CY = 1024, 1280
COL_DQ, COL_DK, COL_DV, COL_DG = 1536, 1792, 2048, 2304


def _cparams(*sem):
    return pltpu.CompilerParams(dimension_semantics=sem, vmem_limit_bytes=VMEM_LIMIT_BYTES)


def _dot(a, b):
    return jnp.dot(a, b, preferred_element_type=F32)


def _dot_nt(a, b):
    return lax.dot_general(a, b, (((1,), (1,)), ((), ())), preferred_element_type=F32)


def _dot_tn(a, b):
    return lax.dot_general(a, b, (((0,), (0,)), ((), ())), preferred_element_type=F32)


def _sigmoid(x):
    return 0.5 * jnp.tanh(0.5 * x) + 0.5


def _silu(x):
    return x * _sigmoid(x)


def _gelu_tanh(x):
    return 0.5 * x * (1.0 + jnp.tanh(math.sqrt(2.0 / math.pi) * (x + 0.044715 * (x * x * x))))


def _mod_row(i, tm, s):
    if tm >= MOD_GROUP:
        block_index = i * (tm // MOD_GROUP) + s
    else:
        block_index = i >> int(math.log2(MOD_GROUP // tm))
    return jnp.maximum(block_index - (N_CTX_ROWS // MOD_GROUP - 1), 0)


def _norm_mod(x, g, sc, sh):
    ms = jnp.mean(x * x, axis=-1, keepdims=True)
    return (x * lax.rsqrt(ms + EPS) * g) * (1.0 + sc) + sh


def _full(shape):
    return pl.BlockSpec(shape, lambda *_: (0,) * len(shape))


def _layer_block(shape, layer):
    return pl.BlockSpec((None,) + shape, lambda *_: (layer,) + (0,) * len(shape))


MOD_TN = 3072


def _mod_kernel(cc_ref, c_ref, w_ref, b_ref, o_ref):
    l = pl.program_id(0)
    pad = jnp.zeros((MOD_ROWS - 1 - DEC_BATCH, D_MODEL), F32)
    cond = jnp.concatenate([cc_ref[...], c_ref[...], pad], axis=0)
    o_ref[...] = _dot(_silu(cond).astype(BF16), w_ref[...].astype(BF16)) + b_ref[pl.ds(l, 1), :]


def _modulation(c_ctx, c, w_mod, b_mod):
    n = N_MOD * D_MODEL
    return pl.pallas_call(
        _mod_kernel,
        grid=(DEPTH, n // MOD_TN),
        in_specs=[
            pl.BlockSpec((1, D_MODEL), lambda l, j: (0, 0)),
            pl.BlockSpec((DEC_BATCH, D_MODEL), lambda l, j: (0, 0)),
            pl.BlockSpec((None, D_MODEL, MOD_TN), lambda l, j: (l, 0, j)),
            pl.BlockSpec((DEPTH, MOD_TN), lambda l, j: (0, j)),
        ],
        out_specs=pl.BlockSpec((None, MOD_ROWS, MOD_TN), lambda l, j: (l, 0, j)),
        out_shape=jax.ShapeDtypeStruct((DEPTH, MOD_ROWS, n), F32),
        compiler_params=_cparams("arbitrary", "arbitrary"),
        name="modulation",
    )(c_ctx.reshape(1, D_MODEL), c, w_mod, b_mod)


FFN_TM = 1024
FFN_TF = 256
N_CTX_TILES = N_CTX_ROWS // FFN_TM


FFN_NJ = D_FF // FFN_TF
N_FFN_TILES = N_ROWS // FFN_TM
N_FFN_STEPS = FFN_NJ + N_FFN_TILES


def _ffn_tile(step):
    return jnp.maximum(step - FFN_NJ, 0)


def _on_stream_part(tile, x_refs, o_refs, fn):
    if len(x_refs) == 1 and len(o_refs) == 1:
        fn(x_refs[0], o_refs[0])
    else:
        pl.when(tile < N_CTX_TILES)(lambda: fn(x_refs[0], o_refs[0]))
        pl.when(tile >= N_CTX_TILES)(lambda: fn(x_refs[-1], o_refs[-1]))


def _ffn_kernel(*refs, layer, n_in, n_out):
    x_refs = refs[:n_in]
    n_ref, sh_ref, sc_ref, g_ref, wg_ref, wu_ref, wd_ref = refs[n_in:n_in + 7]
    o_refs = refs[n_in + 7:n_in + 7 + n_out]
    h_ref, a_ref, wg_s, wu_s, wd_s = refs[n_in + 7 + n_out:]
    nj, tf = FFN_NJ, FFN_TF
    s = pl.program_id(0)
    tile = _ffn_tile(s)
    r = _mod_row(tile, FFN_TM, 0)

    def load_tile():
        def init(x_ref, _):
            h = _norm_mod(x_ref[...], n_ref[layer:layer + 1, :], sc_ref[pl.ds(r, 1), :], sh_ref[pl.ds(r, 1), :])
            h_ref[...] = h.astype(BF16)
        _on_stream_part(tile, x_refs, o_refs, init)

    def up_chunk(j, cols):
        h = h_ref[...]
        a_ref[:, cols] = (_silu(_dot(h, wg_s[j])) * _dot(h, wu_s[j])).astype(BF16)

    def down_and_store():
        y = (0.5 * g_ref[pl.ds(r, 1), :]) * _dot(a_ref[...], wd_s[...])

        def store(x_ref, o_ref):
            o_ref[...] = x_ref[...] + y
        _on_stream_part(tile, x_refs, o_refs, store)

    def keep_arrived_chunk():
        wg_s[s] = wg_ref[...].astype(BF16)
        wu_s[s] = wu_ref[...].astype(BF16)
        wd_s[pl.ds(pl.multiple_of(s * tf, tf), tf), :] = wd_ref[...].astype(BF16)

    def up_previous_chunk():
        up_chunk(s - 1, pl.ds(pl.multiple_of((s - 1) * tf, tf), tf))

    @pl.when(s == 0)
    def _():
        load_tile()
        keep_arrived_chunk()

    @pl.when((s > 0) & (s < nj))
    def _():
        up_previous_chunk()
        keep_arrived_chunk()

    @pl.when(s == nj)
    def _():
        up_previous_chunk()
        down_and_store()

    @pl.when(s > nj)
    def _():
        load_tile()
        for j in range(nj):
            up_chunk(j, slice(j * tf, (j + 1) * tf))
        down_and_store()


def _stream_specs(split, buffered_once):
    tm = FFN_TM
    kw = {"pipeline_mode": pl.Buffered(1)} if buffered_once else {}
    if not split:
        return [pl.BlockSpec((tm, D_MODEL), lambda s: (_ffn_tile(s), 0), **kw)]
    last_ctx = N_CTX_TILES - 1
    return [pl.BlockSpec((tm, D_MODEL), lambda s: (jnp.minimum(_ffn_tile(s), last_ctx), 0), **kw),
            pl.BlockSpec((tm, D_MODEL), lambda s: (jnp.maximum(_ffn_tile(s) - N_CTX_TILES, 0), 0), **kw)]


def _ffn(xs, mod, layer, chunk0, norm_g, wg, wu, wd, split_out=False):
    tm, tf, nj = FFN_TM, FFN_TF, FFN_NJ
    split_in = len(xs) == 2
    mod_spec = lambda c: pl.BlockSpec((None, MOD_ROWS, D_MODEL), lambda s: (layer, 0, c))
    w_col = lambda s: (layer, 0, jnp.minimum(s, nj - 1))
    w_row = lambda s: (layer, jnp.minimum(s, nj - 1), 0)
    if split_out:
        out_shape = [jax.ShapeDtypeStruct((N_CTX_ROWS, D_MODEL), F32),
                     jax.ShapeDtypeStruct((N_LAT_ROWS, D_MODEL), F32)]
    else:
        out_shape = [jax.ShapeDtypeStruct((N_ROWS, D_MODEL), F32)]
    out = pl.pallas_call(
        functools.partial(_ffn_kernel, layer=layer, n_in=len(xs), n_out=len(out_shape)),
        grid=(N_FFN_STEPS,),
        in_specs=_stream_specs(split_in, False) + [
            _full((DEPTH, D_MODEL)),
            mod_spec(chunk0), mod_spec(chunk0 + 1), mod_spec(chunk0 + 2),
            pl.BlockSpec((None, D_MODEL, tf), w_col),
            pl.BlockSpec((None, D_MODEL, tf), w_col),
            pl.BlockSpec((None, tf, D_MODEL), w_row),
        ],
        out_specs=_stream_specs(split_out, True),
        out_shape=out_shape,
        scratch_shapes=[pltpu.VMEM((tm, D_MODEL), BF16),
                        pltpu.VMEM((tm, D_FF), BF16),
                        pltpu.VMEM((nj, D_MODEL, tf), BF16),
                        pltpu.VMEM((nj, D_MODEL, tf), BF16),
                        pltpu.VMEM((D_FF, D_MODEL), BF16)],
        compiler_params=_cparams("arbitrary"),
        name="ffn",
    )(*xs, norm_g, mod, mod, mod, wg, wu, wd)
    return tuple(out)


def _once(shape, index_map):
    return pl.BlockSpec(shape, index_map, pipeline_mode=pl.Buffered(1))


def _mod_chunk(layer, c):
    return pl.BlockSpec((None, MOD_ROWS, D_MODEL), lambda *_: (layer, 0, c))


def _head_mean_square(x):
    n = x.shape[-1]
    r = lax.broadcasted_iota(jnp.int32, (n, n), 0) >> HEAD_SHIFT
    c = lax.broadcasted_iota(jnp.int32, (n, n), 1) >> HEAD_SHIFT
    ones_bd = jnp.where(r == c, 1.0, 0.0).astype(BF16)
    return _dot((x * x).astype(BF16), ones_bd) * (1.0 / HEAD_DIM)


def _head_norm(x, head_gain):
    gain_row = jnp.concatenate([head_gain] * (x.shape[-1] // HEAD_DIM), axis=-1)
    return x * lax.rsqrt(_head_mean_square(x) + EPS) * gain_row


def _head_cols(x, h):
    return x[:, h * HEAD_DIM:(h + 1) * HEAD_DIM].astype(BF16)


def _softmax_pv(scores, values, sink):
    m = jnp.max(scores[0], axis=-1, keepdims=True)
    for s in scores[1:]:
        m = jnp.maximum(m, jnp.max(s, axis=-1, keepdims=True))
    if sink is not None:
        m = jnp.maximum(m, sink)
    denom = None
    acc = None
    for s, v in zip(scores, values):
        p = jnp.exp(s - m)
        d = jnp.sum(p, axis=-1, keepdims=True)
        o = _dot(p.astype(BF16), v)
        denom = d if denom is None else denom + d
        acc = o if acc is None else acc + o
    if sink is not None:
        denom = denom + jnp.exp(sink - m)
    return acc / denom


def _rope(x, cos, sin_lo, sin_hi):
    cols = []
    for c in range(x.shape[-1] // 128):
        xc = x[:, c * 128:(c + 1) * 128]
        cols.append(xc * cos + pltpu.roll(xc, 112, 1) * sin_lo + pltpu.roll(xc, 16, 1) * sin_hi)
    return cols[0] if len(cols) == 1 else jnp.concatenate(cols, axis=-1)


def _block_diag(blocks):
    n = len(blocks)
    w = blocks[0].shape[0]
    rows = []
    for k, blk in enumerate(blocks):
        parts = []
        if k > 0:
            parts.append(jnp.zeros((w, k * w), F32))
        parts.append(blk)
        if k < n - 1:
            parts.append(jnp.zeros((w, (n - 1 - k) * w), F32))
        rows.append(jnp.concatenate(parts, axis=-1))
    return jnp.concatenate(rows, axis=0)


def _rglru_gates(xc, wa, ba, wx, bx, lam):
    xb = xc.astype(BF16)
    r = _sigmoid(_dot(xb, wa) + ba)
    i = _sigmoid(_dot(xb, wx) + bx)
    softplus = jnp.maximum(-lam, 0.0) + jnp.log1p(jnp.exp(-jnp.abs(lam)))
    log_a = (-LRU_C) * r * softplus
    a = jnp.exp(log_a)
    b = jnp.sqrt(1.0 - a * a) * (i * xc)
    return a, b


def _block_prefix(a, b, reverse):
    t = a.shape[0]
    row = lax.broadcasted_iota(jnp.int32, a.shape, 0) & 7
    for d in (1, 2, 4):
        if reverse:
            a_s = pltpu.roll(a, t - d, 0)
            b_s = pltpu.roll(b, t - d, 0)
            ok = row < 8 - d
        else:
            a_s = pltpu.roll(a, d, 0)
            b_s = pltpu.roll(b, d, 0)
            ok = row >= d
        b = jnp.where(ok, a * b_s + b, b)
        a = jnp.where(ok, a * a_s, a)
    return a, b


def _conv4(x, w_ref, b_row):
    t = x.shape[0]
    row = lax.broadcasted_iota(jnp.int32, x.shape, 0)
    xm2 = jnp.where(row >= 2, pltpu.roll(x, 2, 0), 0.0)
    xm1 = jnp.where(row >= 1, pltpu.roll(x, 1, 0), 0.0)
    xp1 = jnp.where(row < t - 1, pltpu.roll(x, t - 1, 0), 0.0)
    return (xm2 * w_ref[0:1, :] + xm1 * w_ref[1:2, :] + x * w_ref[2:3, :] + xp1 * w_ref[3:4, :]) + b_row


def _rglru_prepare(cx, cy, conv_w_ref, conv_b, gate_w_ref, ba_ref, bx_ref, lam_ref,
                   af_ref, bf_ref, ab_ref, bb_ref, gel_ref):
    xc = _conv4(cx, conv_w_ref, conv_b)
    a, b = _rglru_gates(xc, gate_w_ref[0], ba_ref[0:1, :], gate_w_ref[1], bx_ref[0:1, :], lam_ref[0:1, :])
    a, b = _block_prefix(a, b, reverse=False)
    af_ref[...] = a
    bf_ref[...] = b
    a, b = _rglru_gates(xc, gate_w_ref[2], ba_ref[1:2, :], gate_w_ref[3], bx_ref[1:2, :], lam_ref[1:2, :])
    a, b = _block_prefix(a, b, reverse=True)
    ab_ref[...] = a
    bb_ref[...] = b
    gel_ref[...] = _gelu_tanh(cy)


def _rglru_finish(h0f, h0b, af_ref, bf_ref, ab_ref, bb_ref, hf_ref, hb_ref, gel_ref):
    nblk = af_ref.shape[0] // 8

    def body(k, carry):
        cf, cb = carry
        rf = pl.ds(pl.multiple_of(k * 8, 8), 8)
        hf = bf_ref[rf, :] + af_ref[rf, :] * cf
        hf_ref[rf, :] = hf
        rb = pl.ds(pl.multiple_of((nblk - 1 - k) * 8, 8), 8)
        hb = bb_ref[rb, :] + ab_ref[rb, :] * cb
        hb_ref[rb, :] = hb
        return hf[7:8, :], hb[0:1, :]

    cf, cb = lax.fori_loop(0, nblk, body, (h0f, h0b), unroll=32)
    oc = (hf_ref[...] + hb_ref[...]) * gel_ref[...]
    return oc, cf, cb


def _store_gate_weights(gate_w_ref, wa_ref, wx_ref):
    for d in range(2):
        gate_w_ref[2 * d] = _block_diag([wa_ref[d, n] for n in range(N_HEADS)]).astype(BF16)
        gate_w_ref[2 * d + 1] = _block_diag([wx_ref[d, n] for n in range(N_HEADS)]).astype(BF16)


def _lane_head_masks(n):
    lane = lax.broadcasted_iota(jnp.int32, (1, n), 1) >> HEAD_SHIFT
    return [jnp.where(lane == h, 1.0, 0.0) for h in range(n // HEAD_DIM)]


def _log_decays(theta_ref, masks):
    theta = theta_ref[...]
    lanes = theta[:, 0:1] * masks[0]
    for h in range(1, N_HEADS):
        lanes = lanes + theta[:, h:h + 1] * masks[h]
    lg = jnp.log1p(-jnp.exp(lanes))
    return lg[0:1, :], lg[1:2, :]


RET_BLOCK = 256


def _retention(q, k8, vb, s0, lgf, lgb, masks, o_ref):
    t, w = q.shape
    c = RET_BLOCK
    nh = w // HEAD_DIM
    pos = lax.broadcasted_iota(jnp.int32, (c, w), 0).astype(F32)
    q_dec = (jnp.exp(lgf * (pos + 1.0)), jnp.exp(lgb * (float(c) - pos)))
    k_dec = (jnp.exp(lgf * (float(c - 1) - pos)), jnp.exp(lgb * pos))
    chunk_dec = (jnp.exp(lgf * float(c)), jnp.exp(lgb * float(c)))
    rel = (lax.broadcasted_iota(jnp.int32, (c, c), 0) - lax.broadcasted_iota(jnp.int32, (c, c), 1)).astype(F32)
    decs = []
    for h in range(nh):
        gf = lgf[:, h * HEAD_DIM:h * HEAD_DIM + 1]
        gb = lgb[:, h * HEAD_DIM:h * HEAD_DIM + 1]
        e = jnp.exp(jnp.where(rel >= 0, gf * rel, gb * (-rel)))
        decs.append(jnp.where(rel == 0, 2.0, e))
    dec = jnp.concatenate(decs, axis=0)
    r_head = lax.broadcasted_iota(jnp.int32, (w, w), 0) >> HEAD_SHIFT
    c_head = lax.broadcasted_iota(jnp.int32, (w, w), 1) >> HEAD_SHIFT
    same_head = jnp.where(r_head == c_head, 1.0, 0.0)
    states = [None, None] if s0 is None else list(s0)

    def carry(d, rows, o):
        if states[d] is not None:
            o = o + _dot((q[rows, :] * q_dec[d]).astype(BF16), states[d].astype(BF16))
        upd = _dot_tn((k8[rows, :] * k_dec[d]).astype(BF16), vb[rows, :]) * same_head
        states[d] = upd if states[d] is None else states[d] * chunk_dec[d] + upd
        return o

    for ci in range(t // c):
        rows = slice(ci * c, (ci + 1) * c)
        qc = q[rows, :]
        q_stack = jnp.concatenate([(qc * masks[h]).astype(BF16) for h in range(nh)], axis=0)
        inner = (_dot_nt(q_stack, k8[rows, :].astype(BF16)) * dec).astype(BF16)
        out = _dot(inner, vb[rows, :])
        o = out[0:c, :] * masks[0]
        for h in range(1, nh):
            o = o + out[h * c:(h + 1) * c, :] * masks[h]
        o_ref[rows, :] = carry(0, rows, o)
    for ci in reversed(range(t // c)):
        rows = slice(ci * c, (ci + 1) * c)
        if states[1] is not None:
            o_ref[rows, :] = carry(1, rows, o_ref[rows, :])
        else:
            carry(1, rows, None)
    return states[0], states[1]


def _ctx_mixer_kernel(*refs, layer, n_prev):
    prev_refs = refs[:n_prev]
    (x_ref, n2_ref, sh_ref, sc_ref, g2_ref, win_ref, wout_ref,
     aqn_ref, akn_ref, bqn_ref, bkn_ref, sink_ref,
     convw_ref, convb_ref, wa_ref, ba_ref, wx_ref, bx_ref, lam_ref, theta_ref, dn_ref,
     xn_ref, *state_refs) = refs[n_prev:n_prev + 28]
    (win_s, wout_s, u_ref, mixed_ref,
     gate_w_ref, af_ref, bf_ref, ab_ref, bb_ref, hf_ref, hb_ref, gel_ref, ret_ref) = refs[n_prev + 28:]
    t = SEQ
    lrow = slice(layer, layer + 1)
    for prev_ref, state_ref in zip(prev_refs, state_refs):
        for earlier in range(layer):
            state_ref[earlier] = prev_ref[earlier]
    ka_ref, va_ref, kb_ref, vb_ref, stc_ref, std_ref = (ref.at[layer] for ref in state_refs)

    @pl.when(pl.program_id(0) == 0)
    def _():
        for c in range(IN_WIDTH // 512):
            win_s[:, c * 512:(c + 1) * 512] = win_ref[:, c * 512:(c + 1) * 512].astype(BF16)
        wout_s[...] = wout_ref[...].astype(BF16)
        _store_gate_weights(gate_w_ref, wa_ref, wx_ref)

    x = x_ref[...]
    h = _norm_mod(x, n2_ref[lrow, :], sc_ref[0:1, :], sh_ref[0:1, :]).astype(BF16)
    c_cols = slice(COL_CX, COL_CX + 2 * GROUP_W)
    u_ref[:, c_cols] = _dot(h, win_s[:, c_cols])
    _rglru_prepare(u_ref[:, COL_CX:COL_CX + GROUP_W], u_ref[:, COL_CY:COL_CY + GROUP_W],
                   convw_ref, convb_ref[lrow, :], gate_w_ref, ba_ref, bx_ref, lam_ref,
                   af_ref, bf_ref, ab_ref, bb_ref, gel_ref)
    u_ref[:, 0:COL_CX] = _dot(h, win_s[:, 0:COL_CX])
    u_ref[:, COL_DQ:IN_WIDTH] = _dot(h, win_s[:, COL_DQ:IN_WIDTH])

    for (cq, ck, cv, qn_ref, kn_ref, k_out, v_out, col0, use_sink) in (
            (COL_AQ, COL_AK, COL_AV, aqn_ref, akn_ref, ka_ref, va_ref, 0, True),
            (COL_BQ, COL_BK, COL_BV, bqn_ref, bkn_ref, kb_ref, vb_ref, GROUP_W, False)):
        q = _head_norm(u_ref[:, cq:cq + 256], qn_ref[lrow, :])
        k = _head_norm(u_ref[:, ck:ck + 128], kn_ref[lrow, :])
        v = u_ref[:, cv:cv + 128]
        k_out[...] = k
        v_out[...] = v
        qs = q * (HEAD_DIM ** -0.5)
        heads = []
        for hd in range(N_HEADS):
            kv = hd // 2
            s = _dot_nt(_head_cols(qs, hd), _head_cols(k, kv))
            sink = jnp.full((t, 1), sink_ref[layer, hd], F32) if use_sink else None
            heads.append(_softmax_pv([s], [_head_cols(v, kv)], sink))
        mixed_ref[:, col0:col0 + GROUP_W] = jnp.concatenate(heads, axis=-1).astype(BF16)

    zero = jnp.zeros((1, GROUP_W), F32)
    oc, cf, cb = _rglru_finish(zero, zero, af_ref, bf_ref, ab_ref, bb_ref, hf_ref, hb_ref, gel_ref)
    mixed_ref[:, 2 * GROUP_W:3 * GROUP_W] = oc.astype(BF16)
    stc_ref[0:1, :] = cf
    stc_ref[1:2, :] = cb

    masks = _lane_head_masks(GROUP_W)
    lgf, lgb = _log_decays(theta_ref, masks)
    k8 = u_ref[:, COL_DK:COL_DK + GROUP_W] * (HEAD_DIM ** -0.5)
    vb = u_ref[:, COL_DV:COL_DV + GROUP_W].astype(BF16)
    final_states = _retention(u_ref[:, COL_DQ:COL_DQ + GROUP_W], k8, vb, None, lgf, lgb, masks, ret_ref)
    o = ret_ref[...]
    o = o * lax.rsqrt(_head_mean_square(o) + EPS) * dn_ref[lrow, :] * _silu(u_ref[:, COL_DG:COL_DG + GROUP_W])
    mixed_ref[:, 3 * GROUP_W:4 * GROUP_W] = o.astype(BF16)
    for d, s_full in enumerate(final_states):
        for hd in range(N_HEADS):
            std_ref[d, hd] = s_full[hd * 64:(hd + 1) * 64, hd * 64:(hd + 1) * 64]

    xn_ref[...] = x + g2_ref[0:1, :] * _dot(mixed_ref[...], wout_s[...])


def _ctx_mixers(x, mod, layer, prev, norm2_g, w_in, w_out,
                a_qn, a_kn, a_sink, b_qn, b_kn, c_conv_w, c_conv_b, c_wa, c_ba, c_wx, c_bx,
                c_lambda, d_theta, d_norm_g):
    per_request = lambda slots, shape: pl.BlockSpec((None, slots) + shape, lambda b: (b,) + (0,) * (1 + len(shape)))
    state_dims = [(SEQ, 128)] * 4 + [(2, GROUP_W), (2, N_HEADS, HEAD_DIM, HEAD_DIM)]
    scr = pltpu.VMEM((SEQ, GROUP_W), F32)
    out = pl.pallas_call(
        functools.partial(_ctx_mixer_kernel, layer=layer, n_prev=len(prev)),
        grid=(BATCH,),
        in_specs=[per_request(layer, dims) for dims in state_dims[:len(prev)]] + [
            pl.BlockSpec((SEQ, D_MODEL), lambda b: (b, 0)),
            _full((DEPTH, D_MODEL)),
            _mod_chunk(layer, 3), _mod_chunk(layer, 4), _mod_chunk(layer, 5),
            _once((None, D_MODEL, IN_WIDTH), lambda b: (layer, 0, 0)),
            _once((None, D_MODEL, D_MODEL), lambda b: (layer, 0, 0)),
            _full((DEPTH, HEAD_DIM)), _full((DEPTH, HEAD_DIM)), _full((DEPTH, HEAD_DIM)), _full((DEPTH, HEAD_DIM)),
            pl.BlockSpec(memory_space=pltpu.SMEM),
            _layer_block((4, GROUP_W), layer), _full((DEPTH, GROUP_W)),
            _layer_block((2, N_HEADS, HEAD_DIM, HEAD_DIM), layer), _layer_block((2, GROUP_W), layer),
            _layer_block((2, N_HEADS, HEAD_DIM, HEAD_DIM), layer), _layer_block((2, GROUP_W), layer),
            _layer_block((2, GROUP_W), layer),
            _layer_block((2, N_HEADS), layer), _full((DEPTH, GROUP_W)),
        ],
        out_specs=[pl.BlockSpec((SEQ, D_MODEL), lambda b: (b, 0))] + [
            per_request(layer + 1, dims) for dims in state_dims],
        out_shape=[jax.ShapeDtypeStruct((N_ROWS, D_MODEL), F32)] + [
            jax.ShapeDtypeStruct((BATCH, layer + 1) + dims, F32) for dims in state_dims],
        input_output_aliases={len(prev): 0},
        scratch_shapes=[pltpu.VMEM((D_MODEL, IN_WIDTH), BF16), pltpu.VMEM((D_MODEL, D_MODEL), BF16),
                        pltpu.VMEM((SEQ, IN_WIDTH), F32), pltpu.VMEM((SEQ, D_MODEL), BF16),
                        pltpu.VMEM((4, GROUP_W, GROUP_W), BF16)] + [scr] * 8,
        compiler_params=_cparams("arbitrary"),
        name="ctx_mixers",
    )(*prev, x, norm2_g, mod, mod, mod, w_in, w_out,
      a_qn, a_kn, b_qn, b_kn, a_sink, c_conv_w, c_conv_b, c_wa, c_ba, c_wx, c_bx,
      c_lambda, d_theta, d_norm_g)
    return out[0], tuple(out[1:])


LAT_BLOCK0 = N_CTX_ROWS // DEC_SEQ


def _lat_attn_kernel(x_ref, n2_ref, sh_ref, sc_ref, g2_ref, win_ref, wout_ref,
                     kca_ref, vca_ref, kcb_ref, vcb_ref,
                     aqn_ref, akn_ref, bqn_ref, bkn_ref, sink_ref, cos_ref, sinl_ref, sinh_ref,
                     xn_ref, h_ref, u_ref, o_ref, *, layer):
    t = DEC_SEQ
    lrow = slice(layer, layer + 1)
    mrow = pl.ds(1 + pl.program_id(0), 1)
    cos, sin_lo, sin_hi = cos_ref[...], sinl_ref[...], sinh_ref[...]
    scale = HEAD_DIM ** -0.5
    x = x_ref[...]
    h_ref[...] = _norm_mod(x, n2_ref[lrow, :], sc_ref[mrow, :], sh_ref[mrow, :]).astype(BF16)
    u_ref[...] = _dot(h_ref[...], win_ref[...].astype(BF16))

    q = _rope(_head_norm(u_ref[:, COL_AQ:COL_AQ + 256], aqn_ref[lrow, :]), cos, sin_lo, sin_hi)
    k = _rope(_head_norm(u_ref[:, COL_AK:COL_AK + 128], akn_ref[lrow, :]), cos, sin_lo, sin_hi)
    qh = [_head_cols(q * scale, h) for h in range(4)]
    v = u_ref[:, COL_AV:COL_AV + 128]
    kh = [_head_cols(k, kv) for kv in range(2)]
    vh = [_head_cols(v, kv) for kv in range(2)]
    kch = [_head_cols(kca_ref[...], kv) for kv in range(2)]
    vch = [_head_cols(vca_ref[...], kv) for kv in range(2)]
    w = ATT_BLOCK
    span = 3 * w
    for n in range(t // w):
        start = min(max((n - 1) * w, 0), t - span)
        rows = slice(n * w, (n + 1) * w)
        band = slice(start, start + span)
        qpos = (lax.broadcasted_iota(jnp.int32, (2 * w, span), 0) & (w - 1)) + n * w
        kpos = lax.broadcasted_iota(jnp.int32, (2 * w, span), 1) + start
        valid = jnp.abs(qpos - kpos) <= WINDOW
        heads = []
        for kv in range(2):
            qp = jnp.concatenate([qh[2 * kv][rows, :], qh[2 * kv + 1][rows, :]], axis=0)
            s_ctx = _dot_nt(qp, kch[kv])
            s_band = jnp.where(valid, _dot_nt(qp, kh[kv][band, :]), NEG_INF)
            row = lax.broadcasted_iota(jnp.int32, (2 * w, 1), 0)
            sink = jnp.where(row < w, sink_ref[layer, 2 * kv], sink_ref[layer, 2 * kv + 1])
            o = _softmax_pv([s_ctx, s_band], [vch[kv], vh[kv][band, :]], sink)
            heads += [o[0:w, :], o[w:2 * w, :]]
        o_ref[rows, 0:GROUP_W] = jnp.concatenate(heads, axis=-1).astype(BF16)

    q = _rope(_head_norm(u_ref[:, COL_BQ:COL_BQ + 256], bqn_ref[lrow, :]), cos, sin_lo, sin_hi)
    k = _rope(_head_norm(u_ref[:, COL_BK:COL_BK + 128], bkn_ref[lrow, :]), cos, sin_lo, sin_hi)
    qh = [_head_cols(q * scale, h) for h in range(4)]
    v = u_ref[:, COL_BV:COL_BV + 128]
    kh = [_head_cols(k, kv) for kv in range(2)]
    vh = [_head_cols(v, kv) for kv in range(2)]
    kch = [_head_cols(kcb_ref[...], kv) for kv in range(2)]
    vch = [_head_cols(vcb_ref[...], kv) for kv in range(2)]
    tq = 256
    for n in range(t // tq):
        rows = slice(n * tq, (n + 1) * tq)
        heads = []
        for kv in range(2):
            qp = jnp.concatenate([qh[2 * kv][rows, :], qh[2 * kv + 1][rows, :]], axis=0)
            o = _softmax_pv([_dot_nt(qp, kch[kv]), _dot_nt(qp, kh[kv])], [vch[kv], vh[kv]], None)
            heads += [o[0:tq, :], o[tq:2 * tq, :]]
        o_ref[rows, GROUP_W:2 * GROUP_W] = jnp.concatenate(heads, axis=-1).astype(BF16)

    xn_ref[...] = x + g2_ref[mrow, :] * _dot(o_ref[...], wout_ref[...].astype(BF16))


def _lat_recurrent_kernel(xn_in_ref, h_ref, g2_ref, wc_ref, wqk_ref, wvg_ref, wout_ref, h0_ref,
                          convw_ref, convb_ref, wa_ref, ba_ref, wx_ref, bx_ref, lam_ref,
                          s0_ref, theta_ref, dn_ref,
                          xn_ref, gate_w_ref, af_ref, bf_ref, ab_ref, bb_ref, hf_ref, hb_ref, gel_ref, ret_ref,
                          *, layer):
    lrow = slice(layer, layer + 1)
    mrow = pl.ds(1 + pl.program_id(0), 1)

    @pl.when(pl.program_id(0) == 0)
    def _():
        _store_gate_weights(gate_w_ref, wa_ref, wx_ref)

    h = h_ref[...]
    u = _dot(h, wc_ref[...].astype(BF16))
    _rglru_prepare(u[:, 0:GROUP_W], u[:, GROUP_W:2 * GROUP_W], convw_ref, convb_ref[lrow, :],
                   gate_w_ref, ba_ref, bx_ref, lam_ref, af_ref, bf_ref, ab_ref, bb_ref, gel_ref)
    oc, _, _ = _rglru_finish(h0_ref[0:1, :], h0_ref[1:2, :],
                             af_ref, bf_ref, ab_ref, bb_ref, hf_ref, hb_ref, gel_ref)
    y = _dot(oc.astype(BF16), wout_ref[0:GROUP_W, :].astype(BF16))

    uqk = _dot(h, wqk_ref[...].astype(BF16))
    uvg = _dot(h, wvg_ref[...].astype(BF16))
    masks = _lane_head_masks(GROUP_W)
    lgf, lgb = _log_decays(theta_ref, masks)
    s0 = tuple(_block_diag([s0_ref[d, hd] for hd in range(N_HEADS)]) for d in range(2))
    _retention(uqk[:, 0:GROUP_W], uqk[:, GROUP_W:2 * GROUP_W] * (HEAD_DIM ** -0.5),
               uvg[:, 0:GROUP_W].astype(BF16), s0, lgf, lgb, masks, ret_ref)
    o = ret_ref[...]
    o = o * lax.rsqrt(_head_mean_square(o) + EPS) * dn_ref[lrow, :] * _silu(uvg[:, GROUP_W:2 * GROUP_W])
    y = y + _dot(o.astype(BF16), wout_ref[GROUP_W:2 * GROUP_W, :].astype(BF16))
    xn_ref[...] = xn_in_ref[...] + g2_ref[mrow, :] * y


def _lat_mixers(x, mod, layer, caches, state_c, state_d, rope, norm2_g, w_in, w_out,
                a_qn, a_kn, a_sink, b_qn, b_kn, c_conv_w, c_conv_b, c_wa, c_ba, c_wx, c_bx,
                c_lambda, d_theta, d_norm_g):
    rows = pl.BlockSpec((DEC_SEQ, D_MODEL), lambda b: (LAT_BLOCK0 + b, 0))
    h_rows = pl.BlockSpec((DEC_SEQ, D_MODEL), lambda b: (b, 0))
    cache_spec = pl.BlockSpec((None, None, PAST_LEN, 128), lambda b: (b, layer, 0, 0))
    gain = _full((DEPTH, HEAD_DIM))
    table = _once((DEC_SEQ, 128), lambda b: (0, 0))
    out_shape = jax.ShapeDtypeStruct((N_ROWS, D_MODEL), F32)
    win_cols = lambda w, c: _once((None, D_MODEL, w), lambda b: (layer, 0, c))
    wout_rows = lambda h, r: _once((None, h, D_MODEL), lambda b: (layer, r, 0))

    xn, h = pl.pallas_call(
        functools.partial(_lat_attn_kernel, layer=layer),
        grid=(DEC_BATCH,),
        in_specs=[rows, _full((DEPTH, D_MODEL)),
                  _mod_chunk(layer, 3), _mod_chunk(layer, 4), _mod_chunk(layer, 5),
                  win_cols(4 * GROUP_W, 0), wout_rows(2 * GROUP_W, 0),
                  cache_spec, cache_spec, cache_spec, cache_spec,
                  gain, gain, gain, gain,
                  pl.BlockSpec(memory_space=pltpu.SMEM),
                  table, table, table],
        out_specs=[rows, pl.BlockSpec((DEC_SEQ, D_MODEL), lambda b: (b, 0), pipeline_mode=pl.Buffered(1))],
        out_shape=[out_shape, jax.ShapeDtypeStruct((N_LAT_ROWS, D_MODEL), BF16)],
        input_output_aliases={0: 0},
        scratch_shapes=[pltpu.VMEM((DEC_SEQ, 4 * GROUP_W), F32), pltpu.VMEM((DEC_SEQ, 2 * GROUP_W), BF16)],
        compiler_params=_cparams("arbitrary"),
        name="lat_attention",
    )(x, norm2_g, mod, mod, mod, w_in, w_out, *caches, a_qn, a_kn, b_qn, b_kn, a_sink, *rope)

    scr = pltpu.VMEM((DEC_SEQ, GROUP_W), F32)
    xn = pl.pallas_call(
        functools.partial(_lat_recurrent_kernel, layer=layer),
        grid=(DEC_BATCH,),
        in_specs=[
            rows, h_rows, _mod_chunk(layer, 5),
            win_cols(2 * GROUP_W, COL_CX // (2 * GROUP_W)),
            win_cols(2 * GROUP_W, COL_DQ // (2 * GROUP_W)), win_cols(2 * GROUP_W, COL_DV // (2 * GROUP_W)),
            wout_rows(2 * GROUP_W, 1),
            pl.BlockSpec((None, None, 2, GROUP_W), lambda b: (b, layer, 0, 0)),
            _layer_block((4, GROUP_W), layer), _full((DEPTH, GROUP_W)),
            _layer_block((2, N_HEADS, HEAD_DIM, HEAD_DIM), layer), _layer_block((2, GROUP_W), layer),
            _layer_block((2, N_HEADS, HEAD_DIM, HEAD_DIM), layer), _layer_block((2, GROUP_W), layer),
            _layer_block((2, GROUP_W), layer),
            pl.BlockSpec((None, None, 2, N_HEADS, HEAD_DIM, HEAD_DIM), lambda b: (b, layer, 0, 0, 0, 0)),
            _layer_block((2, N_HEADS), layer), _full((DEPTH, GROUP_W))],
        out_specs=rows,
        out_shape=out_shape,
        input_output_aliases={0: 0},
        scratch_shapes=[pltpu.VMEM((4, GROUP_W, GROUP_W), BF16)] + [scr] * 8,
        compiler_params=_cparams("arbitrary"),
        name="lat_recurrent",
    )(xn, h, mod, w_in, w_in, w_in, w_out, state_c, c_conv_w, c_conv_b, c_wa, c_ba, c_wx, c_bx, c_lambda,
      state_d, d_theta, d_norm_g)
    return xn


def _rope_tables():
    t = np.arange(DEC_SEQ)
    row = (t // GRID_W).astype(np.float64)[:, None]
    col = (t % GRID_W).astype(np.float64)[:, None]
    half = HEAD_DIM // 2
    inv = 1.0 / (ROPE_BASE ** (np.arange(0, half, 2, dtype=np.float64) / half))
    j = np.arange(128) % HEAD_DIM
    ang = np.where((j < half)[None, :], row, col) * inv[j % (half // 2)][None, :]
    first = ((j % half) < half // 2)[None, :]
    cos, sin = np.cos(ang), np.sin(ang)
    return tuple(jnp.asarray(a, F32) for a in (cos, np.where(first, -sin, 0.0), np.where(first, 0.0, sin)))


def kernel(x_prompt, x_sample, cache_a_k, cache_a_v, cache_b_k, cache_b_v, state_c, state_d, c, c_ctx, norm1_g, norm2_g, norm3_g, w_mod, b_mod, ffn1_wg, ffn1_wu, ffn1_wd, ffn2_wg, ffn2_wu, ffn2_wd, w_in, w_out, a_qn, a_kn, a_sink, b_qn, b_kn, c_conv_w, c_conv_b, c_wa, c_ba, c_wx, c_bx, c_lambda, d_theta, d_norm_g):
    mod = _modulation(c_ctx, c, w_mod, b_mod)
    rope = _rope_tables()
    caches = tuple(t.reshape(DEC_BATCH, DEPTH, PAST_LEN, 128) for t in (cache_a_k, cache_a_v, cache_b_k, cache_b_v))
    mixer_params = (a_qn, a_kn, a_sink, b_qn, b_kn, c_conv_w, c_conv_b, c_wa, c_ba, c_wx, c_bx,
                    c_lambda, d_theta, d_norm_g)
    xs = (x_prompt.reshape(N_CTX_ROWS, D_MODEL), x_sample.reshape(N_LAT_ROWS, D_MODEL))
    states = ()
    for l in range(DEPTH):
        (x,) = _ffn(xs, mod, l, 0, norm1_g, ffn1_wg, ffn1_wu, ffn1_wd)
        x, states = _ctx_mixers(x, mod, l, states, norm2_g, w_in, w_out, *mixer_params)
        x = _lat_mixers(x, mod, l, caches, state_c, state_d, rope, norm2_g, w_in, w_out, *mixer_params)
        xs = _ffn((x,), mod, l, 6, norm3_g, ffn2_wg, ffn2_wu, ffn2_wd, split_out=(l == DEPTH - 1))
    y_p, y_s = xs
    ka, va, kb, vb, st_c, st_d = states
    kv_shape = (BATCH, DEPTH, SEQ, 2, HEAD_DIM)
    return (y_p.reshape(BATCH, SEQ, D_MODEL), y_s.reshape(DEC_BATCH, DEC_SEQ, D_MODEL),
            ka.reshape(kv_shape), va.reshape(kv_shape), kb.reshape(kv_shape), vb.reshape(kv_shape),
            st_c, st_d)
```

```python
import functools
import math

import numpy as np
import jax
import jax.numpy as jnp
from jax import lax
from jax.experimental import pallas as pl
from jax.experimental.pallas import tpu as pltpu

F32 = jnp.float32
BF16 = jnp.bfloat16

D_MODEL = 1024
BATCH = 16
SEQ = 256
DEPTH = 2
DEC_BATCH = 2
DEC_SEQ = 1024
PAST_LEN = 512
GRID_W = 64
HEAD_DIM = 64
HEAD_SHIFT = 6
N_HEADS = 4
GROUP_W = 256
WINDOW = 128
ATT_BLOCK = 128
ROPE_BASE = 10000.0
LRU_C = 8.0
D_FF = 2816
N_MOD = 9
EPS = 1e-6
NEG_INF = -1e30
IN_WIDTH = 2560

N_CTX_ROWS = BATCH * SEQ
N_LAT_ROWS = DEC_BATCH * DEC_SEQ
N_ROWS = N_CTX_ROWS + N_LAT_ROWS
MOD_ROWS = 8
MOD_GROUP = 1024

VMEM_LIMIT_BYTES = 56 * 1024 * 1024

COL_AQ, COL_AK, COL_AV = 0, 256, 384
COL_BQ, COL_BK, COL_BV = 512, 768, 896
COL_CX, COL_CY = 1024, 1280
COL_DQ, COL_DK, COL_DV, COL_DG = 1536, 1792, 2048, 2304


def _cparams(*sem):
    return pltpu.CompilerParams(dimension_semantics=sem, vmem_limit_bytes=VMEM_LIMIT_BYTES)


def _dot(a, b):
    return jnp.dot(a, b, preferred_element_type=F32)


def _dot_nt(a, b):
    return lax.dot_general(a, b, (((1,), (1,)), ((), ())), preferred_element_type=F32)


def _dot_tn(a, b):
    return lax.dot_general(a, b, (((0,), (0,)), ((), ())), preferred_element_type=F32)


def _sigmoid(x):
    return 0.5 * jnp.tanh(0.5 * x) + 0.5


def _silu(x):
    return x * _sigmoid(x)


def _gelu_tanh(x):
    return 0.5 * x * (1.0 + jnp.tanh(math.sqrt(2.0 / math.pi) * (x + 0.044715 * (x * x * x))))


def _mod_row(i, tm, s):
    if tm >= MOD_GROUP:
        block_index = i * (tm // MOD_GROUP) + s
    else:
        block_index = i >> int(math.log2(MOD_GROUP // tm))
    return jnp.maximum(block_index - (N_CTX_ROWS // MOD_GROUP - 1), 0)


def _norm_mod(x, g, sc, sh):
    ms = jnp.mean(x * x, axis=-1, keepdims=True)
    return (x * lax.rsqrt(ms + EPS) * g) * (1.0 + sc) + sh


def _full(shape):
    return pl.BlockSpec(shape, lambda *_: (0,) * len(shape))


def _layer_block(shape, layer):
    return pl.BlockSpec((None,) + shape, lambda *_: (layer,) + (0,) * len(shape))


MOD_TN = 3072


def _mod_kernel(cc_ref, c_ref, w_ref, b_ref, o_ref):
    l = pl.program_id(0)
    pad = jnp.zeros((MOD_ROWS - 1 - DEC_BATCH, D_MODEL), F32)
    cond = jnp.concatenate([cc_ref[...], c_ref[...], pad], axis=0)
    o_ref[...] = _dot(_silu(cond).astype(BF16), w_ref[...].astype(BF16)) + b_ref[pl.ds(l, 1), :]


def _modulation(c_ctx, c, w_mod, b_mod):
    n = N_MOD * D_MODEL
    return pl.pallas_call(
        _mod_kernel,
        grid=(DEPTH, n // MOD_TN),
        in_specs=[
            pl.BlockSpec((1, D_MODEL), lambda l, j: (0, 0)),
            pl.BlockSpec((DEC_BATCH, D_MODEL), lambda l, j: (0, 0)),
            pl.BlockSpec((None, D_MODEL, MOD_TN), lambda l, j: (l, 0, j)),
            pl.BlockSpec((DEPTH, MOD_TN), lambda l, j: (0, j)),
        ],
        out_specs=pl.BlockSpec((None, MOD_ROWS, MOD_TN), lambda l, j: (l, 0, j)),
        out_shape=jax.ShapeDtypeStruct((DEPTH, MOD_ROWS, n), F32),
        compiler_params=_cparams("arbitrary", "arbitrary"),
        name="modulation",
    )(c_ctx.reshape(1, D_MODEL), c, w_mod, b_mod)


FFN_TM = 1024
FFN_TF = 256
N_CTX_TILES = N_CTX_ROWS // FFN_TM


FFN_NJ = D_FF // FFN_TF
N_FFN_TILES = N_ROWS // FFN_TM
N_FFN_STEPS = FFN_NJ + N_FFN_TILES


def _ffn_tile(step):
    return jnp.maximum(step - FFN_NJ, 0)


def _read_stream_tile(tile, x_refs):
    if len(x_refs) == 1:
        return x_refs[0][...]
    return jnp.where(tile < N_CTX_TILES, x_refs[0][...], x_refs[1][...])


def _write_stream_tile(tile, o_refs, value):
    if len(o_refs) == 1:
        o_refs[0][...] = value
    else:
        pl.when(tile < N_CTX_TILES)(lambda: o_refs[0].__setitem__(Ellipsis, value))
        pl.when(tile >= N_CTX_TILES)(lambda: o_refs[1].__setitem__(Ellipsis, value))


def _ffn_kernel(*refs, layer, n_in, n_out):
    x_refs = refs[:n_in]
    n_ref, sh_ref, sc_ref, g_ref, wg_ref, wu_ref, wd_ref = refs[n_in:n_in + 7]
    o_refs = refs[n_in + 7:n_in + 7 + n_out]
    h_ref, a_ref, wg_s, wu_s, wd_s = refs[n_in + 7 + n_out:]
    nj, tf = FFN_NJ, FFN_TF
    s = pl.program_id(0)
    tile = _ffn_tile(s)
    r = _mod_row(tile, FFN_TM, 0)

    def load_tile():
        x = _read_stream_tile(tile, x_refs)
        h = _norm_mod(x, n_ref[layer:layer + 1, :], sc_ref[pl.ds(r, 1), :], sh_ref[pl.ds(r, 1), :])
        h_ref[...] = h.astype(BF16)

    def up_chunk(j, cols):
        h = h_ref[...]
        a_ref[:, cols] = (_silu(_dot(h, wg_s[j])) * _dot(h, wu_s[j])).astype(BF16)

    def down_and_store():
        y = (0.5 * g_ref[pl.ds(r, 1), :]) * _dot(a_ref[...], wd_s[...])
        _write_stream_tile(tile, o_refs, _read_stream_tile(tile, x_refs) + y)

    def keep_arrived_chunk():
        wg_s[s] = wg_ref[...].astype(BF16)
        wu_s[s] = wu_ref[...].astype(BF16)
        wd_s[pl.ds(pl.multiple_of(s * tf, tf), tf), :] = wd_ref[...].astype(BF16)

    def up_previous_chunk():
        up_chunk(s - 1, pl.ds(pl.multiple_of((s - 1) * tf, tf), tf))

    @pl.when(s == 0)
    def _():
        load_tile()
        keep_arrived_chunk()

    @pl.when((s > 0) & (s < nj))
    def _():
        up_previous_chunk()
        keep_arrived_chunk()

    @pl.when(s == nj)
    def _():
        up_previous_chunk()
        down_and_store()

    @pl.when(s > nj)
    def _():
        load_tile()
        for j in range(nj):
            up_chunk(j, slice(j * tf, (j + 1) * tf))
        down_and_store()


def _stream_specs(split, buffered_once):
    tm = FFN_TM
    kw = {"pipeline_mode": pl.Buffered(1)} if buffered_once else {}
    if not split:
        return [pl.BlockSpec((tm, D_MODEL), lambda s: (_ffn_tile(s), 0), **kw)]
    last_ctx = N_CTX_TILES - 1
    return [pl.BlockSpec((tm, D_MODEL), lambda s: (jnp.minimum(_ffn_tile(s), last_ctx), 0), **kw),
            pl.BlockSpec((tm, D_MODEL), lambda s: (jnp.maximum(_ffn_tile(s) - N_CTX_TILES, 0), 0), **kw)]


def _ffn(xs, mod, layer, chunk0, norm_g, wg, wu, wd, split_out=False):
    tm, tf, nj = FFN_TM, FFN_TF, FFN_NJ
    split_in = len(xs) == 2
    mod_spec = lambda c: pl.BlockSpec((None, MOD_ROWS, D_MODEL), lambda s: (layer, 0, c))
    w_col = lambda s: (layer, 0, jnp.minimum(s, nj - 1))
    w_row = lambda s: (layer, jnp.minimum(s, nj - 1), 0)
    if split_out:
        out_shape = [jax.ShapeDtypeStruct((N_CTX_ROWS, D_MODEL), F32),
                     jax.ShapeDtypeStruct((N_LAT_ROWS, D_MODEL), F32)]
    else:
        out_shape = [jax.ShapeDtypeStruct((N_ROWS, D_MODEL), F32)]
    out = pl.pallas_call(
        functools.partial(_ffn_kernel, layer=layer, n_in=len(xs), n_out=len(out_shape)),
        grid=(N_FFN_STEPS,),
        in_specs=_stream_specs(split_in, False) + [
            _full((DEPTH, D_MODEL)),
            mod_spec(chunk0), mod_spec(chunk0 + 1), mod_spec(chunk0 + 2),
            pl.BlockSpec((None, D_MODEL, tf), w_col),
            pl.BlockSpec((None, D_MODEL, tf), w_col),
            pl.BlockSpec((None, tf, D_MODEL), w_row),
        ],
        out_specs=_stream_specs(split_out, True),
        out_shape=out_shape,
        scratch_shapes=[pltpu.VMEM((tm, D_MODEL), BF16),
                        pltpu.VMEM((tm, D_FF), BF16),
                        pltpu.VMEM((nj, D_MODEL, tf), BF16),
                        pltpu.VMEM((nj, D_MODEL, tf), BF16),
                        pltpu.VMEM((D_FF, D_MODEL), BF16)],
        compiler_params=_cparams("arbitrary"),
        name="ffn",
    )(*xs, norm_g, mod, mod, mod, wg, wu, wd)
    return tuple(out)


def _once(shape, index_map):
    return pl.BlockSpec(shape, index_map, pipeline_mode=pl.Buffered(1))


def _mod_chunk(layer, c):
    return pl.BlockSpec((None, MOD_ROWS, D_MODEL), lambda *_: (layer, 0, c))


def _head_mean_square(x):
    n = x.shape[-1]
    r = lax.broadcasted_iota(jnp.int32, (n, n), 0) >> HEAD_SHIFT
    c = lax.broadcasted_iota(jnp.int32, (n, n), 1) >> HEAD_SHIFT
    ones_bd = jnp.where(r == c, 1.0, 0.0).astype(BF16)
    return _dot((x * x).astype(BF16), ones_bd) * (1.0 / HEAD_DIM)


def _head_norm(x, head_gain):
    gain_row = jnp.concatenate([head_gain] * (x.shape[-1] // HEAD_DIM), axis=-1)
    return x * lax.rsqrt(_head_mean_square(x) + EPS) * gain_row


def _head_cols(x, h):
    return x[:, h * HEAD_DIM:(h + 1) * HEAD_DIM].astype(BF16)


def _softmax_pv(scores, values, sink):
    m = jnp.max(scores[0], axis=-1, keepdims=True)
    for s in scores[1:]:
        m = jnp.maximum(m, jnp.max(s, axis=-1, keepdims=True))
    if sink is not None:
        m = jnp.maximum(m, sink)
    denom = None
    acc = None
    for s, v in zip(scores, values):
        p = jnp.exp(s - m)
        d = jnp.sum(p, axis=-1, keepdims=True)
        o = _dot(p.astype(BF16), v)
        denom = d if denom is None else denom + d
        acc = o if acc is None else acc + o
    if sink is not None:
        denom = denom + jnp.exp(sink - m)
    return acc / denom


def _rope(x, cos, sin_lo, sin_hi):
    cols = []
    for c in range(x.shape[-1] // 128):
        xc = x[:, c * 128:(c + 1) * 128]
        cols.append(xc * cos + pltpu.roll(xc, 112, 1) * sin_lo + pltpu.roll(xc, 16, 1) * sin_hi)
    return cols[0] if len(cols) == 1 else jnp.concatenate(cols, axis=-1)


def _block_diag(blocks):
    n = len(blocks)
    w = blocks[0].shape[0]
    rows = []
    for k, blk in enumerate(blocks):
        parts = []
        if k > 0:
            parts.append(jnp.zeros((w, k * w), F32))
        parts.append(blk)
        if k < n - 1:
            parts.append(jnp.zeros((w, (n - 1 - k) * w), F32))
        rows.append(jnp.concatenate(parts, axis=-1))
    return jnp.concatenate(rows, axis=0)


def _rglru_gates(xc, wa, ba, wx, bx, lam):
    xb = xc.astype(BF16)
    r = _sigmoid(_dot(xb, wa) + ba)
    i = _sigmoid(_dot(xb, wx) + bx)
    softplus = jnp.maximum(-lam, 0.0) + jnp.log1p(jnp.exp(-jnp.abs(lam)))
    log_a = (-LRU_C) * r * softplus
    a = jnp.exp(log_a)
    b = jnp.sqrt(1.0 - a * a) * (i * xc)
    return a, b


def _block_prefix(a, b, reverse):
    t = a.shape[0]
    row = lax.broadcasted_iota(jnp.int32, a.shape, 0) & 7
    for d in (1, 2, 4):
        if reverse:
            a_s = pltpu.roll(a, t - d, 0)
            b_s = pltpu.roll(b, t - d, 0)
            ok = row < 8 - d
        else:
            a_s = pltpu.roll(a, d, 0)
            b_s = pltpu.roll(b, d, 0)
            ok = row >= d
        b = jnp.where(ok, a * b_s + b, b)
        a = jnp.where(ok, a * a_s, a)
    return a, b


def _conv4(x, w_ref, b_row):
    t = x.shape[0]
    row = lax.broadcasted_iota(jnp.int32, x.shape, 0)
    xm2 = jnp.where(row >= 2, pltpu.roll(x, 2, 0), 0.0)
    xm1 = jnp.where(row >= 1, pltpu.roll(x, 1, 0), 0.0)
    xp1 = jnp.where(row < t - 1, pltpu.roll(x, t - 1, 0), 0.0)
    return (xm2 * w_ref[0:1, :] + xm1 * w_ref[1:2, :] + x * w_ref[2:3, :] + xp1 * w_ref[3:4, :]) + b_row


def _rglru_prepare(cx, cy, conv_w_ref, conv_b, gate_w_ref, ba_ref, bx_ref, lam_ref,
                   af_ref, bf_ref, ab_ref, bb_ref, gel_ref):
    xc = _conv4(cx, conv_w_ref, conv_b)
    a, b = _rglru_gates(xc, gate_w_ref[0], ba_ref[0:1, :], gate_w_ref[1], bx_ref[0:1, :], lam_ref[0:1, :])
    a, b = _block_prefix(a, b, reverse=False)
    af_ref[...] = a
    bf_ref[...] = b
    a, b = _rglru_gates(xc, gate_w_ref[2], ba_ref[1:2, :], gate_w_ref[3], bx_ref[1:2, :], lam_ref[1:2, :])
    a, b = _block_prefix(a, b, reverse=True)
    ab_ref[...] = a
    bb_ref[...] = b
    gel_ref[...] = _gelu_tanh(cy)


def _rglru_finish(h0f, h0b, af_ref, bf_ref, ab_ref, bb_ref, hf_ref, hb_ref, gel_ref):
    nblk = af_ref.shape[0] // 8

    def body(k, carry):
        cf, cb = carry
        rf = pl.ds(pl.multiple_of(k * 8, 8), 8)
        hf = bf_ref[rf, :] + af_ref[rf, :] * cf
        hf_ref[rf, :] = hf
        rb = pl.ds(pl.multiple_of((nblk - 1 - k) * 8, 8), 8)
        hb = bb_ref[rb, :] + ab_ref[rb, :] * cb
        hb_ref[rb, :] = hb
        return hf[7:8, :], hb[0:1, :]

    cf, cb = lax.fori_loop(0, nblk, body, (h0f, h0b), unroll=8)
    oc = (hf_ref[...] + hb_ref[...]) * gel_ref[...]
    return oc, cf, cb


def _store_gate_weights(gate_w_ref, wa_ref, wx_ref):
    for d in range(2):
        gate_w_ref[2 * d] = _block_diag([wa_ref[d, n] for n in range(N_HEADS)]).astype(BF16)
        gate_w_ref[2 * d + 1] = _block_diag([wx_ref[d, n] for n in range(N_HEADS)]).astype(BF16)


def _lane_head_masks(n):
    lane = lax.broadcasted_iota(jnp.int32, (1, n), 1) >> HEAD_SHIFT
    return [jnp.where(lane == h, 1.0, 0.0) for h in range(n // HEAD_DIM)]


def _log_decays(theta_ref, masks):
    theta = theta_ref[...]
    lanes = theta[:, 0:1] * masks[0]
    for h in range(1, N_HEADS):
        lanes = lanes + theta[:, h:h + 1] * masks[h]
    lg = jnp.log1p(-jnp.exp(lanes))
    return lg[0:1, :], lg[1:2, :]


RET_BLOCK = 256


def _retention(q, k8, vb, s0, lgf, lgb, masks, o_ref):
    t, w = q.shape
    c = RET_BLOCK
    nh = w // HEAD_DIM
    pos = lax.broadcasted_iota(jnp.int32, (c, w), 0).astype(F32)
    q_dec = (jnp.exp(lgf * (pos + 1.0)), jnp.exp(lgb * (float(c) - pos)))
    k_dec = (jnp.exp(lgf * (float(c - 1) - pos)), jnp.exp(lgb * pos))
    chunk_dec = (jnp.exp(lgf * float(c)), jnp.exp(lgb * float(c)))
    rel = (lax.broadcasted_iota(jnp.int32, (c, c), 0) - lax.broadcasted_iota(jnp.int32, (c, c), 1)).astype(F32)
    decs = []
    for h in range(nh):
        gf = lgf[:, h * HEAD_DIM:h * HEAD_DIM + 1]
        gb = lgb[:, h * HEAD_DIM:h * HEAD_DIM + 1]
        e = jnp.exp(jnp.where(rel >= 0, gf * rel, gb * (-rel)))
        decs.append(jnp.where(rel == 0, 2.0, e))
    dec = jnp.concatenate(decs, axis=0)
    r_head = lax.broadcasted_iota(jnp.int32, (w, w), 0) >> HEAD_SHIFT
    c_head = lax.broadcasted_iota(jnp.int32, (w, w), 1) >> HEAD_SHIFT
    same_head = jnp.where(r_head == c_head, 1.0, 0.0)
    states = [None, None] if s0 is None else list(s0)

    def carry(d, rows, o):
        if states[d] is not None:
            o = o + _dot((q[rows, :] * q_dec[d]).astype(BF16), states[d].astype(BF16))
        upd = _dot_tn((k8[rows, :] * k_dec[d]).astype(BF16), vb[rows, :]) * same_head
        states[d] = upd if states[d] is None else states[d] * chunk_dec[d] + upd
        return o

    for ci in range(t // c):
        rows = slice(ci * c, (ci + 1) * c)
        qc = q[rows, :]
        q_stack = jnp.concatenate([(qc * masks[h]).astype(BF16) for h in range(nh)], axis=0)
        inner = (_dot_nt(q_stack, k8[rows, :].astype(BF16)) * dec).astype(BF16)
        out = _dot(inner, vb[rows, :])
        o = out[0:c, :] * masks[0]
        for h in range(1, nh):
            o = o + out[h * c:(h + 1) * c, :] * masks[h]
        o_ref[rows, :] = carry(0, rows, o)
    for ci in reversed(range(t // c)):
        rows = slice(ci * c, (ci + 1) * c)
        if states[1] is not None:
            o_ref[rows, :] = carry(1, rows, o_ref[rows, :])
        else:
            carry(1, rows, None)
    return states[0], states[1]


def _ctx_mixer_kernel(*refs, layer, n_prev):
    prev_refs = refs[:n_prev]
    (x_ref, n2_ref, sh_ref, sc_ref, g2_ref, win_ref, wout_ref,
     aqn_ref, akn_ref, bqn_ref, bkn_ref, sink_ref,
     convw_ref, convb_ref, wa_ref, ba_ref, wx_ref, bx_ref, lam_ref, theta_ref, dn_ref,
     xn_ref, *state_refs) = refs[n_prev:n_prev + 28]
    (win_s, wout_s, u_ref, mixed_ref,
     gate_w_ref, af_ref, bf_ref, ab_ref, bb_ref, hf_ref, hb_ref, gel_ref, ret_ref) = refs[n_prev + 28:]
    t = SEQ
    lrow = slice(layer, layer + 1)
    for prev_ref, state_ref in zip(prev_refs, state_refs):
        for earlier in range(layer):
            state_ref[earlier] = prev_ref[earlier]
    ka_ref, va_ref, kb_ref, vb_ref, stc_ref, std_ref = (ref.at[layer] for ref in state_refs)

    @pl.when(pl.program_id(0) == 0)
    def _():
        for c in range(IN_WIDTH // 512):
            win_s[:, c * 512:(c + 1) * 512] = win_ref[:, c * 512:(c + 1) * 512].astype(BF16)
        wout_s[...] = wout_ref[...].astype(BF16)
        _store_gate_weights(gate_w_ref, wa_ref, wx_ref)

    x = x_ref[...]
    h = _norm_mod(x, n2_ref[lrow, :], sc_ref[0:1, :], sh_ref[0:1, :]).astype(BF16)
    c_cols = slice(COL_CX, COL_CX + 2 * GROUP_W)
    u_ref[:, c_cols] = _dot(h, win_s[:, c_cols])
    _rglru_prepare(u_ref[:, COL_CX:COL_CX + GROUP_W], u_ref[:, COL_CY:COL_CY + GROUP_W],
                   convw_ref, convb_ref[lrow, :], gate_w_ref, ba_ref, bx_ref, lam_ref,
                   af_ref, bf_ref, ab_ref, bb_ref, gel_ref)
    u_ref[:, 0:COL_CX] = _dot(h, win_s[:, 0:COL_CX])
    u_ref[:, COL_DQ:IN_WIDTH] = _dot(h, win_s[:, COL_DQ:IN_WIDTH])

    for (cq, ck, cv, qn_ref, kn_ref, k_out, v_out, col0, use_sink) in (
            (COL_AQ, COL_AK, COL_AV, aqn_ref, akn_ref, ka_ref, va_ref, 0, True),
            (COL_BQ, COL_BK, COL_BV, bqn_ref, bkn_ref, kb_ref, vb_ref, GROUP_W, False)):
        q = _head_norm(u_ref[:, cq:cq + 256], qn_ref[lrow, :])
        k = _head_norm(u_ref[:, ck:ck + 128], kn_ref[lrow, :])
        v = u_ref[:, cv:cv + 128]
        k_out[...] = k
        v_out[...] = v
        qs = q * (HEAD_DIM ** -0.5)
        heads = []
        for hd in range(N_HEADS):
            kv = hd // 2
            s = _dot_nt(_head_cols(qs, hd), _head_cols(k, kv))
            sink = jnp.full((t, 1), sink_ref[layer, hd], F32) if use_sink else None
            heads.append(_softmax_pv([s], [_head_cols(v, kv)], sink))
        mixed_ref[:, col0:col0 + GROUP_W] = jnp.concatenate(heads, axis=-1).astype(BF16)

    zero = jnp.zeros((1, GROUP_W), F32)
    oc, cf, cb = _rglru_finish(zero, zero, af_ref, bf_ref, ab_ref, bb_ref, hf_ref, hb_ref, gel_ref)
    mixed_ref[:, 2 * GROUP_W:3 * GROUP_W] = oc.astype(BF16)
    stc_ref[0:1, :] = cf
    stc_ref[1:2, :] = cb

    masks = _lane_head_masks(GROUP_W)
    lgf, lgb = _log_decays(theta_ref, masks)
    k8 = u_ref[:, COL_DK:COL_DK + GROUP_W] * (HEAD_DIM ** -0.5)
    vb = u_ref[:, COL_DV:COL_DV + GROUP_W].astype(BF16)
    final_states = _retention(u_ref[:, COL_DQ:COL_DQ + GROUP_W], k8, vb, None, lgf, lgb, masks, ret_ref)
    o = ret_ref[...]
    o = o * lax.rsqrt(_head_mean_square(o) + EPS) * dn_ref[lrow, :] * _silu(u_ref[:, COL_DG:COL_DG + GROUP_W])
    mixed_ref[:, 3 * GROUP_W:4 * GROUP_W] = o.astype(BF16)
    for d, s_full in enumerate(final_states):
        for hd in range(N_HEADS):
            std_ref[d, hd] = s_full[hd * 64:(hd + 1) * 64, hd * 64:(hd + 1) * 64]

    xn_ref[...] = x + g2_ref[0:1, :] * _dot(mixed_ref[...], wout_s[...])


def _ctx_mixers(x, mod, layer, prev, norm2_g, w_in, w_out,
                a_qn, a_kn, a_sink, b_qn, b_kn, c_conv_w, c_conv_b, c_wa, c_ba, c_wx, c_bx,
                c_lambda, d_theta, d_norm_g):
    per_request = lambda slots, shape: pl.BlockSpec((None, slots) + shape, lambda b: (b,) + (0,) * (1 + len(shape)))
    state_dims = [(SEQ, 128)] * 4 + [(2, GROUP_W), (2, N_HEADS, HEAD_DIM, HEAD_DIM)]
    scr = pltpu.VMEM((SEQ, GROUP_W), F32)
    out = pl.pallas_call(
        functools.partial(_ctx_mixer_kernel, layer=layer, n_prev=len(prev)),
        grid=(BATCH,),
        in_specs=[per_request(layer, dims) for dims in state_dims[:len(prev)]] + [
            pl.BlockSpec((SEQ, D_MODEL), lambda b: (b, 0)),
            _full((DEPTH, D_MODEL)),
            _mod_chunk(layer, 3), _mod_chunk(layer, 4), _mod_chunk(layer, 5),
            _once((None, D_MODEL, IN_WIDTH), lambda b: (layer, 0, 0)),
            _once((None, D_MODEL, D_MODEL), lambda b: (layer, 0, 0)),
            _full((DEPTH, HEAD_DIM)), _full((DEPTH, HEAD_DIM)), _full((DEPTH, HEAD_DIM)), _full((DEPTH, HEAD_DIM)),
            pl.BlockSpec(memory_space=pltpu.SMEM),
            _layer_block((4, GROUP_W), layer), _full((DEPTH, GROUP_W)),
            _layer_block((2, N_HEADS, HEAD_DIM, HEAD_DIM), layer), _layer_block((2, GROUP_W), layer),
            _layer_block((2, N_HEADS, HEAD_DIM, HEAD_DIM), layer), _layer_block((2, GROUP_W), layer),
            _layer_block((2, GROUP_W), layer),
            _layer_block((2, N_HEADS), layer), _full((DEPTH, GROUP_W)),
        ],
        out_specs=[pl.BlockSpec((SEQ, D_MODEL), lambda b: (b, 0))] + [
            per_request(layer + 1, dims) for dims in state_dims],
        out_shape=[jax.ShapeDtypeStruct((N_ROWS, D_MODEL), F32)] + [
            jax.ShapeDtypeStruct((BATCH, layer + 1) + dims, F32) for dims in state_dims],
        input_output_aliases={len(prev): 0},
        scratch_shapes=[pltpu.VMEM((D_MODEL, IN_WIDTH), BF16), pltpu.VMEM((D_MODEL, D_MODEL), BF16),
                        pltpu.VMEM((SEQ, IN_WIDTH), F32), pltpu.VMEM((SEQ, D_MODEL), BF16),
                        pltpu.VMEM((4, GROUP_W, GROUP_W), BF16)] + [scr] * 8,
        compiler_params=_cparams("arbitrary"),
        name="ctx_mixers",
    )(*prev, x, norm2_g, mod, mod, mod, w_in, w_out,
      a_qn, a_kn, b_qn, b_kn, a_sink, c_conv_w, c_conv_b, c_wa, c_ba, c_wx, c_bx,
      c_lambda, d_theta, d_norm_g)
    return out[0], tuple(out[1:])


LAT_BLOCK0 = N_CTX_ROWS // DEC_SEQ


def _lat_attn_kernel(x_ref, n2_ref, sh_ref, sc_ref, g2_ref, win_ref, wout_ref,
                     kca_ref, vca_ref, kcb_ref, vcb_ref,
                     aqn_ref, akn_ref, bqn_ref, bkn_ref, sink_ref, cos_ref, sinl_ref, sinh_ref,
                     xn_ref, h_ref, u_ref, o_ref, *, layer):
    t = DEC_SEQ
    lrow = slice(layer, layer + 1)
    mrow = pl.ds(1 + pl.program_id(0), 1)
    cos, sin_lo, sin_hi = cos_ref[...], sinl_ref[...], sinh_ref[...]
    scale = HEAD_DIM ** -0.5
    x = x_ref[...]
    h_ref[...] = _norm_mod(x, n2_ref[lrow, :], sc_ref[mrow, :], sh_ref[mrow, :]).astype(BF16)
    u_ref[...] = _dot(h_ref[...], win_ref[...].astype(BF16))

    q = _rope(_head_norm(u_ref[:, COL_AQ:COL_AQ + 256], aqn_ref[lrow, :]), cos, sin_lo, sin_hi)
    k = _rope(_head_norm(u_ref[:, COL_AK:COL_AK + 128], akn_ref[lrow, :]), cos, sin_lo, sin_hi)
    qh = [_head_cols(q * scale, h) for h in range(4)]
    v = u_ref[:, COL_AV:COL_AV + 128]
    kh = [_head_cols(k, kv) for kv in range(2)]
    vh = [_head_cols(v, kv) for kv in range(2)]
    kch = [_head_cols(kca_ref[...], kv) for kv in range(2)]
    vch = [_head_cols(vca_ref[...], kv) for kv in range(2)]
    w = ATT_BLOCK
    span = 3 * w
    for n in range(t // w):
        start = min(max((n - 1) * w, 0), t - span)
        rows = slice(n * w, (n + 1) * w)
        band = slice(start, start + span)
        qpos = (lax.broadcasted_iota(jnp.int32, (2 * w, span), 0) & (w - 1)) + n * w
        kpos = lax.broadcasted_iota(jnp.int32, (2 * w, span), 1) + start
        valid = jnp.abs(qpos - kpos) <= WINDOW
        heads = []
        for kv in range(2):
            qp = jnp.concatenate([qh[2 * kv][rows, :], qh[2 * kv + 1][rows, :]], axis=0)
            s_ctx = _dot_nt(qp, kch[kv])
            s_band = jnp.where(valid, _dot_nt(qp, kh[kv][band, :]), NEG_INF)
            row = lax.broadcasted_iota(jnp.int32, (2 * w, 1), 0)
            sink = jnp.where(row < w, sink_ref[layer, 2 * kv], sink_ref[layer, 2 * kv + 1])
            o = _softmax_pv([s_ctx, s_band], [vch[kv], vh[kv][band, :]], sink)
            heads += [o[0:w, :], o[w:2 * w, :]]
        o_ref[rows, 0:GROUP_W] = jnp.concatenate(heads, axis=-1).astype(BF16)

    q = _rope(_head_norm(u_ref[:, COL_BQ:COL_BQ + 256], bqn_ref[lrow, :]), cos, sin_lo, sin_hi)
    k = _rope(_head_norm(u_ref[:, COL_BK:COL_BK + 128], bkn_ref[lrow, :]), cos, sin_lo, sin_hi)
    qh = [_head_cols(q * scale, h) for h in range(4)]
    v = u_ref[:, COL_BV:COL_BV + 128]
    kh = [_head_cols(k, kv) for kv in range(2)]
    vh = [_head_cols(v, kv) for kv in range(2)]
    kch = [_head_cols(kcb_ref[...], kv) for kv in range(2)]
    vch = [_head_cols(vcb_ref[...], kv) for kv in range(2)]
    tq = 256
    for n in range(t // tq):
        rows = slice(n * tq, (n + 1) * tq)
        heads = []
        for kv in range(2):
            qp = jnp.concatenate([qh[2 * kv][rows, :], qh[2 * kv + 1][rows, :]], axis=0)
            o = _softmax_pv([_dot_nt(qp, kch[kv]), _dot_nt(qp, kh[kv])], [vch[kv], vh[kv]], None)
            heads += [o[0:tq, :], o[tq:2 * tq, :]]
        o_ref[rows, GROUP_W:2 * GROUP_W] = jnp.concatenate(heads, axis=-1).astype(BF16)

    xn_ref[...] = x + g2_ref[mrow, :] * _dot(o_ref[...], wout_ref[...].astype(BF16))


def _lat_recurrent_kernel(xn_in_ref, h_ref, g2_ref, wc_ref, wqk_ref, wvg_ref, wout_ref, h0_ref,
                          convw_ref, convb_ref, wa_ref, ba_ref, wx_ref, bx_ref, lam_ref,
                          s0_ref, theta_ref, dn_ref,
                          xn_ref, gate_w_ref, af_ref, bf_ref, ab_ref, bb_ref, hf_ref, hb_ref, gel_ref, ret_ref,
                          *, layer):
    lrow = slice(layer, layer + 1)
    mrow = pl.ds(1 + pl.program_id(0), 1)

    @pl.when(pl.program_id(0) == 0)
    def _():
        _store_gate_weights(gate_w_ref, wa_ref, wx_ref)

    h = h_ref[...]
    u = _dot(h, wc_ref[...].astype(BF16))
    _rglru_prepare(u[:, 0:GROUP_W], u[:, GROUP_W:2 * GROUP_W], convw_ref, convb_ref[lrow, :],
                   gate_w_ref, ba_ref, bx_ref, lam_ref, af_ref, bf_ref, ab_ref, bb_ref, gel_ref)
    oc, _, _ = _rglru_finish(h0_ref[0:1, :], h0_ref[1:2, :],
                             af_ref, bf_ref, ab_ref, bb_ref, hf_ref, hb_ref, gel_ref)
    y = _dot(oc.astype(BF16), wout_ref[0:GROUP_W, :].astype(BF16))

    uqk = _dot(h, wqk_ref[...].astype(BF16))
    uvg = _dot(h, wvg_ref[...].astype(BF16))
    masks = _lane_head_masks(GROUP_W)
    lgf, lgb = _log_decays(theta_ref, masks)
    s0 = tuple(_block_diag([s0_ref[d, hd] for hd in range(N_HEADS)]) for d in range(2))
    _retention(uqk[:, 0:GROUP_W], uqk[:, GROUP_W:2 * GROUP_W] * (HEAD_DIM ** -0.5),
               uvg[:, 0:GROUP_W].astype(BF16), s0, lgf, lgb, masks, ret_ref)
    o = ret_ref[...]
    o = o * lax.rsqrt(_head_mean_square(o) + EPS) * dn_ref[lrow, :] * _silu(uvg[:, GROUP_W:2 * GROUP_W])
    y = y + _dot(o.astype(BF16), wout_ref[GROUP_W:2 * GROUP_W, :].astype(BF16))
    xn_ref[...] = xn_in_ref[...] + g2_ref[mrow, :] * y


def _lat_mixers(x, mod, layer, caches, state_c, state_d, rope, norm2_g, w_in, w_out,
                a_qn, a_kn, a_sink, b_qn, b_kn, c_conv_w, c_conv_b, c_wa, c_ba, c_wx, c_bx,
                c_lambda, d_theta, d_norm_g):
    rows = pl.BlockSpec((DEC_SEQ, D_MODEL), lambda b: (LAT_BLOCK0 + b, 0))
    h_rows = pl.BlockSpec((DEC_SEQ, D_MODEL), lambda b: (b, 0))
    cache_spec = pl.BlockSpec((None, None, PAST_LEN, 128), lambda b: (b, layer, 0, 0))
    gain = _full((DEPTH, HEAD_DIM))
    table = _once((DEC_SEQ, 128), lambda b: (0, 0))
    out_shape = jax.ShapeDtypeStruct((N_ROWS, D_MODEL), F32)
    win_cols = lambda w, c: _once((None, D_MODEL, w), lambda b: (layer, 0, c))
    wout_rows = lambda h, r: _once((None, h, D_MODEL), lambda b: (layer, r, 0))

    xn, h = pl.pallas_call(
        functools.partial(_lat_attn_kernel, layer=layer),
        grid=(DEC_BATCH,),
        in_specs=[rows, _full((DEPTH, D_MODEL)),
                  _mod_chunk(layer, 3), _mod_chunk(layer, 4), _mod_chunk(layer, 5),
                  win_cols(4 * GROUP_W, 0), wout_rows(2 * GROUP_W, 0),
                  cache_spec, cache_spec, cache_spec, cache_spec,
                  gain, gain, gain, gain,
                  pl.BlockSpec(memory_space=pltpu.SMEM),
                  table, table, table],
        out_specs=[rows, pl.BlockSpec((DEC_SEQ, D_MODEL), lambda b: (b, 0), pipeline_mode=pl.Buffered(1))],
        out_shape=[out_shape, jax.ShapeDtypeStruct((N_LAT_ROWS, D_MODEL), BF16)],
        input_output_aliases={0: 0},
        scratch_shapes=[pltpu.VMEM((DEC_SEQ, 4 * GROUP_W), F32), pltpu.VMEM((DEC_SEQ, 2 * GROUP_W), BF16)],
        compiler_params=_cparams("arbitrary"),
        name="lat_attention",
    )(x, norm2_g, mod, mod, mod, w_in, w_out, *caches, a_qn, a_kn, b_qn, b_kn, a_sink, *rope)

    scr = pltpu.VMEM((DEC_SEQ, GROUP_W), F32)
    xn = pl.pallas_call(
        functools.partial(_lat_recurrent_kernel, layer=layer),
        grid=(DEC_BATCH,),
        in_specs=[
            rows, h_rows, _mod_chunk(layer, 5),
            win_cols(2 * GROUP_W, COL_CX // (2 * GROUP_W)),
            win_cols(2 * GROUP_W, COL_DQ // (2 * GROUP_W)), win_cols(2 * GROUP_W, COL_DV // (2 * GROUP_W)),
            wout_rows(2 * GROUP_W, 1),
            pl.BlockSpec((None, None, 2, GROUP_W), lambda b: (b, layer, 0, 0)),
            _layer_block((4, GROUP_W), layer), _full((DEPTH, GROUP_W)),
            _layer_block((2, N_HEADS, HEAD_DIM, HEAD_DIM), layer), _layer_block((2, GROUP_W), layer),
            _layer_block((2, N_HEADS, HEAD_DIM, HEAD_DIM), layer), _layer_block((2, GROUP_W), layer),
            _layer_block((2, GROUP_W), layer),
            pl.BlockSpec((None, None, 2, N_HEADS, HEAD_DIM, HEAD_DIM), lambda b: (b, layer, 0, 0, 0, 0)),
            _layer_block((2, N_HEADS), layer), _full((DEPTH, GROUP_W))],
        out_specs=rows,
        out_shape=out_shape,
        input_output_aliases={0: 0},
        scratch_shapes=[pltpu.VMEM((4, GROUP_W, GROUP_W), BF16)] + [scr] * 8,
        compiler_params=_cparams("arbitrary"),
        name="lat_recurrent",
    )(xn, h, mod, w_in, w_in, w_in, w_out, state_c, c_conv_w, c_conv_b, c_wa, c_ba, c_wx, c_bx, c_lambda,
      state_d, d_theta, d_norm_g)
    return xn


def _rope_tables():
    t = np.arange(DEC_SEQ)
    row = (t // GRID_W).astype(np.float64)[:, None]
    col = (t % GRID_W).astype(np.float64)[:, None]
    half = HEAD_DIM // 2
    inv = 1.0 / (ROPE_BASE ** (np.arange(0, half, 2, dtype=np.float64) / half))
    j = np.arange(128) % HEAD_DIM
    ang = np.where((j < half)[None, :], row, col) * inv[j % (half // 2)][None, :]
    first = ((j % half) < half // 2)[None, :]
    cos, sin = np.cos(ang), np.sin(ang)
    return tuple(jnp.asarray(a, F32) for a in (cos, np.where(first, -sin, 0.0), np.where(first, 0.0, sin)))


def kernel(x_prompt, x_sample, cache_a_k, cache_a_v, cache_b_k, cache_b_v, state_c, state_d, c, c_ctx, norm1_g, norm2_g, norm3_g, w_mod, b_mod, ffn1_wg, ffn1_wu, ffn1_wd, ffn2_wg, ffn2_wu, ffn2_wd, w_in, w_out, a_qn, a_kn, a_sink, b_qn, b_kn, c_conv_w, c_conv_b, c_wa, c_ba, c_wx, c_bx, c_lambda, d_theta, d_norm_g):
    mod = _modulation(c_ctx, c, w_mod, b_mod)
    rope = _rope_tables()
    caches = tuple(t.reshape(DEC_BATCH, DEPTH, PAST_LEN, 128) for t in (cache_a_k, cache_a_v, cache_b_k, cache_b_v))
    mixer_params = (a_qn, a_kn, a_sink, b_qn, b_kn, c_conv_w, c_conv_b, c_wa, c_ba, c_wx, c_bx,
                    c_lambda, d_theta, d_norm_g)
    xs = (x_prompt.reshape(N_CTX_ROWS, D_MODEL), x_sample.reshape(N_LAT_ROWS, D_MODEL))
    states = ()
    for l in range(DEPTH):
        (x,) = _ffn(xs, mod, l, 0, norm1_g, ffn1_wg, ffn1_wu, ffn1_wd)
        x, states = _ctx_mixers(x, mod, l, states, norm2_g, w_in, w_out, *mixer_params)
        x = _lat_mixers(x, mod, l, caches, state_c, state_d, rope, norm2_g, w_in, w_out, *mixer_params)
        xs = _ffn((x,), mod, l, 6, norm3_g, ffn2_wg, ffn2_wu, ffn2_wd, split_out=(l == DEPTH - 1))
    y_p, y_s = xs
    ka, va, kb, vb, st_c, st_d = states
    kv_shape = (BATCH, DEPTH, SEQ, 2, HEAD_DIM)
    return (y_p.reshape(BATCH, SEQ, D_MODEL), y_s.reshape(DEC_BATCH, DEC_SEQ, D_MODEL),
            ka.reshape(kv_shape), va.reshape(kv_shape), kb.reshape(kv_shape), vb.reshape(kv_shape),
            st_c, st_d)
```

```python
import functools
import math

import numpy as np
import jax
import jax.numpy as jnp
from jax import lax
from jax.experimental import pallas as pl
from jax.experimental.pallas import tpu as pltpu

F32 = jnp.float32
BF16 = jnp.bfloat16

D_MODEL = 1024
BATCH = 16
SEQ = 256
DEPTH = 2
DEC_BATCH = 2
DEC_SEQ = 1024
PAST_LEN = 512
GRID_W = 64
HEAD_DIM = 64
HEAD_SHIFT = 6
N_HEADS = 4
GROUP_W = 256
WINDOW = 128
ATT_BLOCK = 128
ROPE_BASE = 10000.0
LRU_C = 8.0
D_FF = 2816
N_MOD = 9
EPS = 1e-6
NEG_INF = -1e30
IN_WIDTH = 2560

N_CTX_ROWS = BATCH * SEQ
N_LAT_ROWS = DEC_BATCH * DEC_SEQ
N_ROWS = N_CTX_ROWS + N_LAT_ROWS
MOD_ROWS = 8
MOD_GROUP = 1024

VMEM_LIMIT_BYTES = 56 * 1024 * 1024

COL_AQ, COL_AK, COL_AV = 0, 256, 384
COL_BQ, COL_BK, COL_BV = 512, 768, 896
COL_CX, COL_CY = 1024, 1280
COL_DQ, COL_DK, COL_DV, COL_DG = 1536, 1792, 2048, 2304


def _cparams(*sem):
    return pltpu.CompilerParams(dimension_semantics=sem, vmem_limit_bytes=VMEM_LIMIT_BYTES)


def _dot(a, b):
    return jnp.dot(a, b, preferred_element_type=F32)


def _dot_nt(a, b):
    return lax.dot_general(a, b, (((1,), (1,)), ((), ())), preferred_element_type=F32)


def _dot_tn(a, b):
    return lax.dot_general(a, b, (((0,), (0,)), ((), ())), preferred_element_type=F32)


def _sigmoid(x):
    return 0.5 * jnp.tanh(0.5 * x) + 0.5


def _silu(x):
    return x * _sigmoid(x)


def _gelu_tanh(x):
    return 0.5 * x * (1.0 + jnp.tanh(math.sqrt(2.0 / math.pi) * (x + 0.044715 * (x * x * x))))


def _mod_row(i, tm, s):
    if tm >= MOD_GROUP:
        block_index = i * (tm // MOD_GROUP) + s
    else:
        block_index = i >> int(math.log2(MOD_GROUP // tm))
    return jnp.maximum(block_index - (N_CTX_ROWS // MOD_GROUP - 1), 0)


def _norm_mod(x, g, sc, sh):
    ms = jnp.mean(x * x, axis=-1, keepdims=True)
    return (x * lax.rsqrt(ms + EPS) * g) * (1.0 + sc) + sh


def _full(shape):
    return pl.BlockSpec(shape, lambda *_: (0,) * len(shape))


def _layer_block(shape, layer):
    return pl.BlockSpec((None,) + shape, lambda *_: (layer,) + (0,) * len(shape))


MOD_TN = 4608


def _mod_kernel(cc_ref, c_ref, w_ref, b_ref, o_ref):
    l = pl.program_id(0)
    pad = jnp.zeros((MOD_ROWS - 1 - DEC_BATCH, D_MODEL), F32)
    cond = jnp.concatenate([cc_ref[...], c_ref[...], pad], axis=0)
    o_ref[...] = _dot(_silu(cond).astype(BF16), w_ref[...].astype(BF16)) + b_ref[pl.ds(l, 1), :]


def _modulation(c_ctx, c, w_mod, b_mod):
    n = N_MOD * D_MODEL
    return pl.pallas_call(
        _mod_kernel,
        grid=(DEPTH, n // MOD_TN),
        in_specs=[
            pl.BlockSpec((1, D_MODEL), lambda l, j: (0, 0)),
            pl.BlockSpec((DEC_BATCH, D_MODEL), lambda l, j: (0, 0)),
            pl.BlockSpec((None, D_MODEL, MOD_TN), lambda l, j: (l, 0, j)),
            pl.BlockSpec((DEPTH, MOD_TN), lambda l, j: (0, j)),
        ],
        out_specs=pl.BlockSpec((None, MOD_ROWS, MOD_TN), lambda l, j: (l, 0, j)),
        out_shape=jax.ShapeDtypeStruct((DEPTH, MOD_ROWS, n), F32),
        compiler_params=_cparams("arbitrary", "arbitrary"),
        name="modulation",
    )(c_ctx.reshape(1, D_MODEL), c, w_mod, b_mod)


FFN_TM = 1024
FFN_TF = 256
N_CTX_TILES = N_CTX_ROWS // FFN_TM


FFN_NJ = D_FF // FFN_TF
N_FFN_TILES = N_ROWS // FFN_TM
N_FFN_STEPS = FFN_NJ + N_FFN_TILES


def _ffn_tile(step):
    return jnp.maximum(step - FFN_NJ, 0)


def _read_stream_tile(tile, x_refs):
    if len(x_refs) == 1:
        return x_refs[0][...]
    return jnp.where(tile < N_CTX_TILES, x_refs[0][...], x_refs[1][...])


def _write_stream_tile(tile, o_refs, value):
    if len(o_refs) == 1:
        o_refs[0][...] = value
    else:
        pl.when(tile < N_CTX_TILES)(lambda: o_refs[0].__setitem__(Ellipsis, value))
        pl.when(tile >= N_CTX_TILES)(lambda: o_refs[1].__setitem__(Ellipsis, value))


def _ffn_kernel(*refs, layer, n_in, n_out):
    x_refs = refs[:n_in]
    n_ref, sh_ref, sc_ref, g_ref, wg_ref, wu_ref, wd_ref = refs[n_in:n_in + 7]
    o_refs = refs[n_in + 7:n_in + 7 + n_out]
    h_ref, a_ref, wg_s, wu_s, wd_s = refs[n_in + 7 + n_out:]
    nj, tf = FFN_NJ, FFN_TF
    s = pl.program_id(0)
    tile = _ffn_tile(s)
    r = _mod_row(tile, FFN_TM, 0)

    def load_tile():
        x = _read_stream_tile(tile, x_refs)
        h = _norm_mod(x, n_ref[layer:layer + 1, :], sc_ref[pl.ds(r, 1), :], sh_ref[pl.ds(r, 1), :])
        h_ref[...] = h.astype(BF16)

    def up_chunk(j, cols):
        h = h_ref[...]
        a_ref[:, cols] = (_silu(_dot(h, wg_s[j])) * _dot(h, wu_s[j])).astype(BF16)

    def down_and_store():
        y = (0.5 * g_ref[pl.ds(r, 1), :]) * _dot(a_ref[...], wd_s[...])
        _write_stream_tile(tile, o_refs, _read_stream_tile(tile, x_refs) + y)

    def keep_arrived_chunk():
        wg_s[s] = wg_ref[...].astype(BF16)
        wu_s[s] = wu_ref[...].astype(BF16)
        wd_s[pl.ds(pl.multiple_of(s * tf, tf), tf), :] = wd_ref[...].astype(BF16)

    def up_previous_chunk():
        up_chunk(s - 1, pl.ds(pl.multiple_of((s - 1) * tf, tf), tf))

    @pl.when(s == 0)
    def _():
        load_tile()
        keep_arrived_chunk()

    @pl.when((s > 0) & (s < nj))
    def _():
        up_previous_chunk()
        keep_arrived_chunk()

    @pl.when(s == nj)
    def _():
        up_previous_chunk()
        down_and_store()

    @pl.when(s > nj)
    def _():
        load_tile()
        for j in range(nj):
            up_chunk(j, slice(j * tf, (j + 1) * tf))
        down_and_store()


def _stream_specs(split, buffered_once):
    tm = FFN_TM
    kw = {"pipeline_mode": pl.Buffered(1)} if buffered_once else {}
    if not split:
        return [pl.BlockSpec((tm, D_MODEL), lambda s: (_ffn_tile(s), 0), **kw)]
    last_ctx = N_CTX_TILES - 1
    return [pl.BlockSpec((tm, D_MODEL), lambda s: (jnp.minimum(_ffn_tile(s), last_ctx), 0), **kw),
            pl.BlockSpec((tm, D_MODEL), lambda s: (jnp.maximum(_ffn_tile(s) - N_CTX_TILES, 0), 0), **kw)]


def _ffn(xs, mod, layer, chunk0, norm_g, wg, wu, wd, split_out=False):
    tm, tf, nj = FFN_TM, FFN_TF, FFN_NJ
    split_in = len(xs) == 2
    mod_spec = lambda c: pl.BlockSpec((None, MOD_ROWS, D_MODEL), lambda s: (layer, 0, c))
    w_col = lambda s: (layer, 0, jnp.minimum(s, nj - 1))
    w_row = lambda s: (layer, jnp.minimum(s, nj - 1), 0)
    if split_out:
        out_shape = [jax.ShapeDtypeStruct((N_CTX_ROWS, D_MODEL), F32),
                     jax.ShapeDtypeStruct((N_LAT_ROWS, D_MODEL), F32)]
    else:
        out_shape = [jax.ShapeDtypeStruct((N_ROWS, D_MODEL), F32)]
    out = pl.pallas_call(
        functools.partial(_ffn_kernel, layer=layer, n_in=len(xs), n_out=len(out_shape)),
        grid=(N_FFN_STEPS,),
        in_specs=_stream_specs(split_in, False) + [
            _full((DEPTH, D_MODEL)),
            mod_spec(chunk0), mod_spec(chunk0 + 1), mod_spec(chunk0 + 2),
            pl.BlockSpec((None, D_MODEL, tf), w_col),
            pl.BlockSpec((None, D_MODEL, tf), w_col),
            pl.BlockSpec((None, tf, D_MODEL), w_row),
        ],
        out_specs=_stream_specs(split_out, True),
        out_shape=out_shape,
        scratch_shapes=[pltpu.VMEM((tm, D_MODEL), BF16),
                        pltpu.VMEM((tm, D_FF), BF16),
                        pltpu.VMEM((nj, D_MODEL, tf), BF16),
                        pltpu.VMEM((nj, D_MODEL, tf), BF16),
                        pltpu.VMEM((D_FF, D_MODEL), BF16)],
        compiler_params=_cparams("arbitrary"),
        name="ffn",
    )(*xs, norm_g, mod, mod, mod, wg, wu, wd)
    return tuple(out)


def _once(shape, index_map):
    return pl.BlockSpec(shape, index_map, pipeline_mode=pl.Buffered(1))


def _mod_chunk(layer, c):
    return pl.BlockSpec((None, MOD_ROWS, D_MODEL), lambda *_: (layer, 0, c))


def _head_mean_square(x):
    n = x.shape[-1]
    r = lax.broadcasted_iota(jnp.int32, (n, n), 0) >> HEAD_SHIFT
    c = lax.broadcasted_iota(jnp.int32, (n, n), 1) >> HEAD_SHIFT
    ones_bd = jnp.where(r == c, 1.0, 0.0).astype(BF16)
    return _dot((x * x).astype(BF16), ones_bd) * (1.0 / HEAD_DIM)


def _head_norm(x, head_gain):
    gain_row = jnp.concatenate([head_gain] * (x.shape[-1] // HEAD_DIM), axis=-1)
    return x * lax.rsqrt(_head_mean_square(x) + EPS) * gain_row


def _head_cols(x, h):
    return x[:, h * HEAD_DIM:(h + 1) * HEAD_DIM].astype(BF16)


def _softmax_pv(scores, values, sink):
    m = jnp.max(scores[0], axis=-1, keepdims=True)
    for s in scores[1:]:
        m = jnp.maximum(m, jnp.max(s, axis=-1, keepdims=True))
    if sink is not None:
        m = jnp.maximum(m, sink)
    denom = None
    acc = None
    for s, v in zip(scores, values):
        p = jnp.exp(s - m)
        d = jnp.sum(p, axis=-1, keepdims=True)
        o = _dot(p.astype(BF16), v)
        denom = d if denom is None else denom + d
        acc = o if acc is None else acc + o
    if sink is not None:
        denom = denom + jnp.exp(sink - m)
    return acc / denom


def _rope(x, cos, sin_lo, sin_hi):
    cols = []
    for c in range(x.shape[-1] // 128):
        xc = x[:, c * 128:(c + 1) * 128]
        cols.append(xc * cos + pltpu.roll(xc, 112, 1) * sin_lo + pltpu.roll(xc, 16, 1) * sin_hi)
    return cols[0] if len(cols) == 1 else jnp.concatenate(cols, axis=-1)


def _block_diag(blocks):
    n = len(blocks)
    w = blocks[0].shape[0]
    rows = []
    for k, blk in enumerate(blocks):
        parts = []
        if k > 0:
            parts.append(jnp.zeros((w, k * w), F32))
        parts.append(blk)
        if k < n - 1:
            parts.append(jnp.zeros((w, (n - 1 - k) * w), F32))
        rows.append(jnp.concatenate(parts, axis=-1))
    return jnp.concatenate(rows, axis=0)


def _rglru_gates(xc, wa, ba, wx, bx, lam):
    xb = xc.astype(BF16)
    r = _sigmoid(_dot(xb, wa) + ba)
    i = _sigmoid(_dot(xb, wx) + bx)
    softplus = jnp.maximum(-lam, 0.0) + jnp.log1p(jnp.exp(-jnp.abs(lam)))
    log_a = (-LRU_C) * r * softplus
    a = jnp.exp(log_a)
    b = jnp.sqrt(1.0 - a * a) * (i * xc)
    return a, b


def _block_prefix(a, b, reverse):
    t = a.shape[0]
    row = lax.broadcasted_iota(jnp.int32, a.shape, 0) & 7
    for d in (1, 2, 4):
        if reverse:
            a_s = pltpu.roll(a, t - d, 0)
            b_s = pltpu.roll(b, t - d, 0)
            ok = row < 8 - d
        else:
            a_s = pltpu.roll(a, d, 0)
            b_s = pltpu.roll(b, d, 0)
            ok = row >= d
        b = jnp.where(ok, a * b_s + b, b)
        a = jnp.where(ok, a * a_s, a)
    return a, b


def _conv4(x, w_ref, b_row):
    t = x.shape[0]
    row = lax.broadcasted_iota(jnp.int32, x.shape, 0)
    xm2 = jnp.where(row >= 2, pltpu.roll(x, 2, 0), 0.0)
    xm1 = jnp.where(row >= 1, pltpu.roll(x, 1, 0), 0.0)
    xp1 = jnp.where(row < t - 1, pltpu.roll(x, t - 1, 0), 0.0)
    return (xm2 * w_ref[0:1, :] + xm1 * w_ref[1:2, :] + x * w_ref[2:3, :] + xp1 * w_ref[3:4, :]) + b_row


def _rglru_prepare(cx, cy, conv_w_ref, conv_b, gate_w_ref, ba_ref, bx_ref, lam_ref,
                   af_ref, bf_ref, ab_ref, bb_ref, gel_ref):
    xc = _conv4(cx, conv_w_ref, conv_b)
    a, b = _rglru_gates(xc, gate_w_ref[0], ba_ref[0:1, :], gate_w_ref[1], bx_ref[0:1, :], lam_ref[0:1, :])
    a, b = _block_prefix(a, b, reverse=False)
    af_ref[...] = a
    bf_ref[...] = b
    a, b = _rglru_gates(xc, gate_w_ref[2], ba_ref[1:2, :], gate_w_ref[3], bx_ref[1:2, :], lam_ref[1:2, :])
    a, b = _block_prefix(a, b, reverse=True)
    ab_ref[...] = a
    bb_ref[...] = b
    gel_ref[...] = _gelu_tanh(cy)


def _rglru_finish(h0f, h0b, af_ref, bf_ref, ab_ref, bb_ref, hf_ref, hb_ref, gel_ref):
    nblk = af_ref.shape[0] // 8

    def body(k, carry):
        cf, cb = carry
        rf = pl.ds(pl.multiple_of(k * 8, 8), 8)
        hf = bf_ref[rf, :] + af_ref[rf, :] * cf
        hf_ref[rf, :] = hf
        rb = pl.ds(pl.multiple_of((nblk - 1 - k) * 8, 8), 8)
        hb = bb_ref[rb, :] + ab_ref[rb, :] * cb
        hb_ref[rb, :] = hb
        return hf[7:8, :], hb[0:1, :]

    cf, cb = lax.fori_loop(0, nblk, body, (h0f, h0b), unroll=8)
    oc = (hf_ref[...] + hb_ref[...]) * gel_ref[...]
    return oc, cf, cb


def _store_gate_weights(gate_w_ref, wa_ref, wx_ref):
    for d in range(2):
        gate_w_ref[2 * d] = _block_diag([wa_ref[d, n] for n in range(N_HEADS)]).astype(BF16)
        gate_w_ref[2 * d + 1] = _block_diag([wx_ref[d, n] for n in range(N_HEADS)]).astype(BF16)


def _lane_head_masks(n):
    lane = lax.broadcasted_iota(jnp.int32, (1, n), 1) >> HEAD_SHIFT
    return [jnp.where(lane == h, 1.0, 0.0) for h in range(n // HEAD_DIM)]


def _log_decays(theta_ref, masks):
    theta = theta_ref[...]
    lanes = theta[:, 0:1] * masks[0]
    for h in range(1, N_HEADS):
        lanes = lanes + theta[:, h:h + 1] * masks[h]
    lg = jnp.log1p(-jnp.exp(lanes))
    return lg[0:1, :], lg[1:2, :]


RET_BLOCK = 256


def _retention(q, k8, vb, s0, lgf, lgb, masks, o_ref):
    t, w = q.shape
    c = RET_BLOCK
    nh = w // HEAD_DIM
    pos = lax.broadcasted_iota(jnp.int32, (c, w), 0).astype(F32)
    q_dec = (jnp.exp(lgf * (pos + 1.0)), jnp.exp(lgb * (float(c) - pos)))
    k_dec = (jnp.exp(lgf * (float(c - 1) - pos)), jnp.exp(lgb * pos))
    chunk_dec = (jnp.exp(lgf * float(c)), jnp.exp(lgb * float(c)))
    rel = (lax.broadcasted_iota(jnp.int32, (c, c), 0) - lax.broadcasted_iota(jnp.int32, (c, c), 1)).astype(F32)
    decs = []
    for h in range(nh):
        gf = lgf[:, h * HEAD_DIM:h * HEAD_DIM + 1]
        gb = lgb[:, h * HEAD_DIM:h * HEAD_DIM + 1]
        e = jnp.exp(jnp.where(rel >= 0, gf * rel, gb * (-rel)))
        decs.append(jnp.where(rel == 0, 2.0, e))
    dec = jnp.concatenate(decs, axis=0)
    r_head = lax.broadcasted_iota(jnp.int32, (w, w), 0) >> HEAD_SHIFT
    c_head = lax.broadcasted_iota(jnp.int32, (w, w), 1) >> HEAD_SHIFT
    same_head = jnp.where(r_head == c_head, 1.0, 0.0)
    states = [None, None] if s0 is None else list(s0)

    def carry(d, rows, o):
        if states[d] is not None:
            o = o + _dot((q[rows, :] * q_dec[d]).astype(BF16), states[d].astype(BF16))
        upd = _dot_tn((k8[rows, :] * k_dec[d]).astype(BF16), vb[rows, :]) * same_head
        states[d] = upd if states[d] is None else states[d] * chunk_dec[d] + upd
        return o

    for ci in range(t // c):
        rows = slice(ci * c, (ci + 1) * c)
        qc = q[rows, :]
        q_stack = jnp.concatenate([(qc * masks[h]).astype(BF16) for h in range(nh)], axis=0)
        inner = (_dot_nt(q_stack, k8[rows, :].astype(BF16)) * dec).astype(BF16)
        out = _dot(inner, vb[rows, :])
        o = out[0:c, :] * masks[0]
        for h in range(1, nh):
            o = o + out[h * c:(h + 1) * c, :] * masks[h]
        o_ref[rows, :] = carry(0, rows, o)
    for ci in reversed(range(t // c)):
        rows = slice(ci * c, (ci + 1) * c)
        if states[1] is not None:
            o_ref[rows, :] = carry(1, rows, o_ref[rows, :])
        else:
            carry(1, rows, None)
    return states[0], states[1]


def _ctx_mixer_kernel(*refs, layer, n_prev):
    prev_refs = refs[:n_prev]
    (x_ref, n2_ref, sh_ref, sc_ref, g2_ref, win_ref, wout_ref,
     aqn_ref, akn_ref, bqn_ref, bkn_ref, sink_ref,
     convw_ref, convb_ref, wa_ref, ba_ref, wx_ref, bx_ref, lam_ref, theta_ref, dn_ref,
     xn_ref, *state_refs) = refs[n_prev:n_prev + 28]
    (win_s, wout_s, u_ref, mixed_ref,
     gate_w_ref, af_ref, bf_ref, ab_ref, bb_ref, hf_ref, hb_ref, gel_ref, ret_ref) = refs[n_prev + 28:]
    t = SEQ
    lrow = slice(layer, layer + 1)
    for prev_ref, state_ref in zip(prev_refs, state_refs):
        for earlier in range(layer):
            state_ref[earlier] = prev_ref[earlier]
    ka_ref, va_ref, kb_ref, vb_ref, stc_ref, std_ref = (ref.at[layer] for ref in state_refs)

    @pl.when(pl.program_id(0) == 0)
    def _():
        for c in range(IN_WIDTH // 512):
            win_s[:, c * 512:(c + 1) * 512] = win_ref[:, c * 512:(c + 1) * 512].astype(BF16)
        wout_s[...] = wout_ref[...].astype(BF16)
        _store_gate_weights(gate_w_ref, wa_ref, wx_ref)

    x = x_ref[...]
    h = _norm_mod(x, n2_ref[lrow, :], sc_ref[0:1, :], sh_ref[0:1, :]).astype(BF16)
    c_cols = slice(COL_CX, COL_CX + 2 * GROUP_W)
    u_ref[:, c_cols] = _dot(h, win_s[:, c_cols])
    _rglru_prepare(u_ref[:, COL_CX:COL_CX + GROUP_W], u_ref[:, COL_CY:COL_CY + GROUP_W],
                   convw_ref, convb_ref[lrow, :], gate_w_ref, ba_ref, bx_ref, lam_ref,
                   af_ref, bf_ref, ab_ref, bb_ref, gel_ref)
    u_ref[:, 0:COL_CX] = _dot(h, win_s[:, 0:COL_CX])
    u_ref[:, COL_DQ:IN_WIDTH] = _dot(h, win_s[:, COL_DQ:IN_WIDTH])

    for (cq, ck, cv, qn_ref, kn_ref, k_out, v_out, col0, use_sink) in (
            (COL_AQ, COL_AK, COL_AV, aqn_ref, akn_ref, ka_ref, va_ref, 0, True),
            (COL_BQ, COL_BK, COL_BV, bqn_ref, bkn_ref, kb_ref, vb_ref, GROUP_W, False)):
        q = _head_norm(u_ref[:, cq:cq + 256], qn_ref[lrow, :])
        k = _head_norm(u_ref[:, ck:ck + 128], kn_ref[lrow, :])
        v = u_ref[:, cv:cv + 128]
        k_out[...] = k
        v_out[...] = v
        qs = q * (HEAD_DIM ** -0.5)
        heads = []
        for hd in range(N_HEADS):
            kv = hd // 2
            s = _dot_nt(_head_cols(qs, hd), _head_cols(k, kv))
            sink = jnp.full((t, 1), sink_ref[layer, hd], F32) if use_sink else None
            heads.append(_softmax_pv([s], [_head_cols(v, kv)], sink))
        mixed_ref[:, col0:col0 + GROUP_W] = jnp.concatenate(heads, axis=-1).astype(BF16)

    zero = jnp.zeros((1, GROUP_W), F32)
    oc, cf, cb = _rglru_finish(zero, zero, af_ref, bf_ref, ab_ref, bb_ref, hf_ref, hb_ref, gel_ref)
    mixed_ref[:, 2 * GROUP_W:3 * GROUP_W] = oc.astype(BF16)
    stc_ref[0:1, :] = cf
    stc_ref[1:2, :] = cb

    masks = _lane_head_masks(GROUP_W)
    lgf, lgb = _log_decays(theta_ref, masks)
    k8 = u_ref[:, COL_DK:COL_DK + GROUP_W] * (HEAD_DIM ** -0.5)
    vb = u_ref[:, COL_DV:COL_DV + GROUP_W].astype(BF16)
    final_states = _retention(u_ref[:, COL_DQ:COL_DQ + GROUP_W], k8, vb, None, lgf, lgb, masks, ret_ref)
    o = ret_ref[...]
    o = o * lax.rsqrt(_head_mean_square(o) + EPS) * dn_ref[lrow, :] * _silu(u_ref[:, COL_DG:COL_DG + GROUP_W])
    mixed_ref[:, 3 * GROUP_W:4 * GROUP_W] = o.astype(BF16)
    for d, s_full in enumerate(final_states):
        for hd in range(N_HEADS):
            std_ref[d, hd] = s_full[hd * 64:(hd + 1) * 64, hd * 64:(hd + 1) * 64]

    xn_ref[...] = x + g2_ref[0:1, :] * _dot(mixed_ref[...], wout_s[...])


def _ctx_mixers(x, mod, layer, prev, norm2_g, w_in, w_out,
                a_qn, a_kn, a_sink, b_qn, b_kn, c_conv_w, c_conv_b, c_wa, c_ba, c_wx, c_bx,
                c_lambda, d_theta, d_norm_g):
    per_request = lambda slots, shape: pl.BlockSpec((None, slots) + shape, lambda b: (b,) + (0,) * (1 + len(shape)))
    state_dims = [(SEQ, 128)] * 4 + [(2, GROUP_W), (2, N_HEADS, HEAD_DIM, HEAD_DIM)]
    scr = pltpu.VMEM((SEQ, GROUP_W), F32)
    out = pl.pallas_call(
        functools.partial(_ctx_mixer_kernel, layer=layer, n_prev=len(prev)),
        grid=(BATCH,),
        in_specs=[per_request(layer, dims) for dims in state_dims[:len(prev)]] + [
            pl.BlockSpec((SEQ, D_MODEL), lambda b: (b, 0)),
            _full((DEPTH, D_MODEL)),
            _mod_chunk(layer, 3), _mod_chunk(layer, 4), _mod_chunk(layer, 5),
            _once((None, D_MODEL, IN_WIDTH), lambda b: (layer, 0, 0)),
            _once((None, D_MODEL, D_MODEL), lambda b: (layer, 0, 0)),
            _full((DEPTH, HEAD_DIM)), _full((DEPTH, HEAD_DIM)), _full((DEPTH, HEAD_DIM)), _full((DEPTH, HEAD_DIM)),
            pl.BlockSpec(memory_space=pltpu.SMEM),
            _layer_block((4, GROUP_W), layer), _full((DEPTH, GROUP_W)),
            _layer_block((2, N_HEADS, HEAD_DIM, HEAD_DIM), layer), _layer_block((2, GROUP_W), layer),
            _layer_block((2, N_HEADS, HEAD_DIM, HEAD_DIM), layer), _layer_block((2, GROUP_W), layer),
            _layer_block((2, GROUP_W), layer),
            _layer_block((2, N_HEADS), layer), _full((DEPTH, GROUP_W)),
        ],
        out_specs=[pl.BlockSpec((SEQ, D_MODEL), lambda b: (b, 0))] + [
            per_request(layer + 1, dims) for dims in state_dims],
        out_shape=[jax.ShapeDtypeStruct((N_ROWS, D_MODEL), F32)] + [
            jax.ShapeDtypeStruct((BATCH, layer + 1) + dims, F32) for dims in state_dims],
        input_output_aliases={len(prev): 0},
        scratch_shapes=[pltpu.VMEM((D_MODEL, IN_WIDTH), BF16), pltpu.VMEM((D_MODEL, D_MODEL), BF16),
                        pltpu.VMEM((SEQ, IN_WIDTH), F32), pltpu.VMEM((SEQ, D_MODEL), BF16),
                        pltpu.VMEM((4, GROUP_W, GROUP_W), BF16)] + [scr] * 8,
        compiler_params=_cparams("arbitrary"),
        name="ctx_mixers",
    )(*prev, x, norm2_g, mod, mod, mod, w_in, w_out,
      a_qn, a_kn, b_qn, b_kn, a_sink, c_conv_w, c_conv_b, c_wa, c_ba, c_wx, c_bx,
      c_lambda, d_theta, d_norm_g)
    return out[0], tuple(out[1:])


LAT_BLOCK0 = N_CTX_ROWS // DEC_SEQ


def _lat_attn_kernel(x_ref, n2_ref, sh_ref, sc_ref, g2_ref, win_ref, wout_ref,
                     kca_ref, vca_ref, kcb_ref, vcb_ref,
                     aqn_ref, akn_ref, bqn_ref, bkn_ref, sink_ref, cos_ref, sinl_ref, sinh_ref,
                     xn_ref, h_ref, u_ref, o_ref, *, layer):
    t = DEC_SEQ
    lrow = slice(layer, layer + 1)
    mrow = pl.ds(1 + pl.program_id(0), 1)
    cos, sin_lo, sin_hi = cos_ref[...], sinl_ref[...], sinh_ref[...]
    scale = HEAD_DIM ** -0.5
    x = x_ref[...]
    h_ref[...] = _norm_mod(x, n2_ref[lrow, :], sc_ref[mrow, :], sh_ref[mrow, :]).astype(BF16)
    u_ref[...] = _dot(h_ref[...], win_ref[...].astype(BF16))

    q = _rope(_head_norm(u_ref[:, COL_AQ:COL_AQ + 256], aqn_ref[lrow, :]), cos, sin_lo, sin_hi)
    k = _rope(_head_norm(u_ref[:, COL_AK:COL_AK + 128], akn_ref[lrow, :]), cos, sin_lo, sin_hi)
    qh = [_head_cols(q * scale, h) for h in range(4)]
    v = u_ref[:, COL_AV:COL_AV + 128]
    kh = [_head_cols(k, kv) for kv in range(2)]
    vh = [_head_cols(v, kv) for kv in range(2)]
    kch = [_head_cols(kca_ref[...], kv) for kv in range(2)]
    vch = [_head_cols(vca_ref[...], kv) for kv in range(2)]
    w = ATT_BLOCK
    span = 3 * w
    for n in range(t // w):
        start = min(max((n - 1) * w, 0), t - span)
        rows = slice(n * w, (n + 1) * w)
        band = slice(start, start + span)
        qpos = (lax.broadcasted_iota(jnp.int32, (2 * w, span), 0) & (w - 1)) + n * w
        kpos = lax.broadcasted_iota(jnp.int32, (2 * w, span), 1) + start
        valid = jnp.abs(qpos - kpos) <= WINDOW
        heads = []
        for kv in range(2):
            qp = jnp.concatenate([qh[2 * kv][rows, :], qh[2 * kv + 1][rows, :]], axis=0)
            s_ctx = _dot_nt(qp, kch[kv])
            s_band = jnp.where(valid, _dot_nt(qp, kh[kv][band, :]), NEG_INF)
            row = lax.broadcasted_iota(jnp.int32, (2 * w, 1), 0)
            sink = jnp.where(row < w, sink_ref[layer, 2 * kv], sink_ref[layer, 2 * kv + 1])
            o = _softmax_pv([s_ctx, s_band], [vch[kv], vh[kv][band, :]], sink)
            heads += [o[0:w, :], o[w:2 * w, :]]
        o_ref[rows, 0:GROUP_W] = jnp.concatenate(heads, axis=-1).astype(BF16)

    q = _rope(_head_norm(u_ref[:, COL_BQ:COL_BQ + 256], bqn_ref[lrow, :]), cos, sin_lo, sin_hi)
    k = _rope(_head_norm(u_ref[:, COL_BK:COL_BK + 128], bkn_ref[lrow, :]), cos, sin_lo, sin_hi)
    qh = [_head_cols(q * scale, h) for h in range(4)]
    v = u_ref[:, COL_BV:COL_BV + 128]
    kh = [_head_cols(k, kv) for kv in range(2)]
    vh = [_head_cols(v, kv) for kv in range(2)]
    kch = [_head_cols(kcb_ref[...], kv) for kv in range(2)]
    vch = [_head_cols(vcb_ref[...], kv) for kv in range(2)]
    tq = 256
    for n in range(t // tq):
        rows = slice(n * tq, (n + 1) * tq)
        heads = []
        for kv in range(2):
            qp = jnp.concatenate([qh[2 * kv][rows, :], qh[2 * kv + 1][rows, :]], axis=0)
            o = _softmax_pv([_dot_nt(qp, kch[kv]), _dot_nt(qp, kh[kv])], [vch[kv], vh[kv]], None)
            heads += [o[0:tq, :], o[tq:2 * tq, :]]
        o_ref[rows, GROUP_W:2 * GROUP_W] = jnp.concatenate(heads, axis=-1).astype(BF16)

    xn_ref[...] = x + g2_ref[mrow, :] * _dot(o_ref[...], wout_ref[...].astype(BF16))


def _lat_recurrent_kernel(xn_in_ref, h_ref, g2_ref, wc_ref, wqk_ref, wvg_ref, wout_ref, h0_ref,
                          convw_ref, convb_ref, wa_ref, ba_ref, wx_ref, bx_ref, lam_ref,
                          s0_ref, theta_ref, dn_ref,
                          xn_ref, gate_w_ref, af_ref, bf_ref, ab_ref, bb_ref, hf_ref, hb_ref, gel_ref, ret_ref,
                          *, layer):
    lrow = slice(layer, layer + 1)
    mrow = pl.ds(1 + pl.program_id(0), 1)

    @pl.when(pl.program_id(0) == 0)
    def _():
        _store_gate_weights(gate_w_ref, wa_ref, wx_ref)

    h = h_ref[...]
    u = _dot(h, wc_ref[...].astype(BF16))
    _rglru_prepare(u[:, 0:GROUP_W], u[:, GROUP_W:2 * GROUP_W], convw_ref, convb_ref[lrow, :],
                   gate_w_ref, ba_ref, bx_ref, lam_ref, af_ref, bf_ref, ab_ref, bb_ref, gel_ref)
    oc, _, _ = _rglru_finish(h0_ref[0:1, :], h0_ref[1:2, :],
                             af_ref, bf_ref, ab_ref, bb_ref, hf_ref, hb_ref, gel_ref)
    y = _dot(oc.astype(BF16), wout_ref[0:GROUP_W, :].astype(BF16))

    uqk = _dot(h, wqk_ref[...].astype(BF16))
    uvg = _dot(h, wvg_ref[...].astype(BF16))
    masks = _lane_head_masks(GROUP_W)
    lgf, lgb = _log_decays(theta_ref, masks)
    s0 = tuple(_block_diag([s0_ref[d, hd] for hd in range(N_HEADS)]) for d in range(2))
    _retention(uqk[:, 0:GROUP_W], uqk[:, GROUP_W:2 * GROUP_W] * (HEAD_DIM ** -0.5),
               uvg[:, 0:GROUP_W].astype(BF16), s0, lgf, lgb, masks, ret_ref)
    o = ret_ref[...]
    o = o * lax.rsqrt(_head_mean_square(o) + EPS) * dn_ref[lrow, :] * _silu(uvg[:, GROUP_W:2 * GROUP_W])
    y = y + _dot(o.astype(BF16), wout_ref[GROUP_W:2 * GROUP_W, :].astype(BF16))
    xn_ref[...] = xn_in_ref[...] + g2_ref[mrow, :] * y


def _lat_mixers(x, mod, layer, caches, state_c, state_d, rope, norm2_g, w_in, w_out,
                a_qn, a_kn, a_sink, b_qn, b_kn, c_conv_w, c_conv_b, c_wa, c_ba, c_wx, c_bx,
                c_lambda, d_theta, d_norm_g):
    rows = pl.BlockSpec((DEC_SEQ, D_MODEL), lambda b: (LAT_BLOCK0 + b, 0))
    h_rows = pl.BlockSpec((DEC_SEQ, D_MODEL), lambda b: (b, 0))
    cache_spec = pl.BlockSpec((None, None, PAST_LEN, 128), lambda b: (b, layer, 0, 0))
    gain = _full((DEPTH, HEAD_DIM))
    table = _once((DEC_SEQ, 128), lambda b: (0, 0))
    out_shape = jax.ShapeDtypeStruct((N_ROWS, D_MODEL), F32)
    win_cols = lambda w, c: _once((None, D_MODEL, w), lambda b: (layer, 0, c))
    wout_rows = lambda h, r: _once((None, h, D_MODEL), lambda b: (layer, r, 0))

    xn, h = pl.pallas_call(
        functools.partial(_lat_attn_kernel, layer=layer),
        grid=(DEC_BATCH,),
        in_specs=[rows, _full((DEPTH, D_MODEL)),
                  _mod_chunk(layer, 3), _mod_chunk(layer, 4), _mod_chunk(layer, 5),
                  win_cols(4 * GROUP_W, 0), wout_rows(2 * GROUP_W, 0),
                  cache_spec, cache_spec, cache_spec, cache_spec,
                  gain, gain, gain, gain,
                  pl.BlockSpec(memory_space=pltpu.SMEM),
                  table, table, table],
        out_specs=[rows, pl.BlockSpec((DEC_SEQ, D_MODEL), lambda b: (b, 0), pipeline_mode=pl.Buffered(1))],
        out_shape=[out_shape, jax.ShapeDtypeStruct((N_LAT_ROWS, D_MODEL), BF16)],
        input_output_aliases={0: 0},
        scratch_shapes=[pltpu.VMEM((DEC_SEQ, 4 * GROUP_W), F32), pltpu.VMEM((DEC_SEQ, 2 * GROUP_W), BF16)],
        compiler_params=_cparams("arbitrary"),
        name="lat_attention",
    )(x, norm2_g, mod, mod, mod, w_in, w_out, *caches, a_qn, a_kn, b_qn, b_kn, a_sink, *rope)

    scr = pltpu.VMEM((DEC_SEQ, GROUP_W), F32)
    xn = pl.pallas_call(
        functools.partial(_lat_recurrent_kernel, layer=layer),
        grid=(DEC_BATCH,),
        in_specs=[
            rows, h_rows, _mod_chunk(layer, 5),
            win_cols(2 * GROUP_W, COL_CX // (2 * GROUP_W)),
            win_cols(2 * GROUP_W, COL_DQ // (2 * GROUP_W)), win_cols(2 * GROUP_W, COL_DV // (2 * GROUP_W)),
            wout_rows(2 * GROUP_W, 1),
            pl.BlockSpec((None, None, 2, GROUP_W), lambda b: (b, layer, 0, 0)),
            _layer_block((4, GROUP_W), layer), _full((DEPTH, GROUP_W)),
            _layer_block((2, N_HEADS, HEAD_DIM, HEAD_DIM), layer), _layer_block((2, GROUP_W), layer),
            _layer_block((2, N_HEADS, HEAD_DIM, HEAD_DIM), layer), _layer_block((2, GROUP_W), layer),
            _layer_block((2, GROUP_W), layer),
            pl.BlockSpec((None, None, 2, N_HEADS, HEAD_DIM, HEAD_DIM), lambda b: (b, layer, 0, 0, 0, 0)),
            _layer_block((2, N_HEADS), layer), _full((DEPTH, GROUP_W))],
        out_specs=rows,
        out_shape=out_shape,
        input_output_aliases={0: 0},
        scratch_shapes=[pltpu.VMEM((4, GROUP_W, GROUP_W), BF16)] + [scr] * 8,
        compiler_params=_cparams("arbitrary"),
        name="lat_recurrent",
    )(xn, h, mod, w_in, w_in, w_in, w_out, state_c, c_conv_w, c_conv_b, c_wa, c_ba, c_wx, c_bx, c_lambda,
      state_d, d_theta, d_norm_g)
    return xn


def _rope_tables():
    t = np.arange(DEC_SEQ)
    row = (t // GRID_W).astype(np.float64)[:, None]
    col = (t % GRID_W).astype(np.float64)[:, None]
    half = HEAD_DIM // 2
    inv = 1.0 / (ROPE_BASE ** (np.arange(0, half, 2, dtype=np.float64) / half))
    j = np.arange(128) % HEAD_DIM
    ang = np.where((j < half)[None, :], row, col) * inv[j % (half // 2)][None, :]
    first = ((j % half) < half // 2)[None, :]
    cos, sin = np.cos(ang), np.sin(ang)
    return tuple(jnp.asarray(a, F32) for a in (cos, np.where(first, -sin, 0.0), np.where(first, 0.0, sin)))


def kernel(x_prompt, x_sample, cache_a_k, cache_a_v, cache_b_k, cache_b_v, state_c, state_d, c, c_ctx, norm1_g, norm2_g, norm3_g, w_mod, b_mod, ffn1_wg, ffn1_wu, ffn1_wd, ffn2_wg, ffn2_wu, ffn2_wd, w_in, w_out, a_qn, a_kn, a_sink, b_qn, b_kn, c_conv_w, c_conv_b, c_wa, c_ba, c_wx, c_bx, c_lambda, d_theta, d_norm_g):
    mod = _modulation(c_ctx, c, w_mod, b_mod)
    rope = _rope_tables()
    caches = tuple(t.reshape(DEC_BATCH, DEPTH, PAST_LEN, 128) for t in (cache_a_k, cache_a_v, cache_b_k, cache_b_v))
    mixer_params = (a_qn, a_kn, a_sink, b_qn, b_kn, c_conv_w, c_conv_b, c_wa, c_ba, c_wx, c_bx,
                    c_lambda, d_theta, d_norm_g)
    xs = (x_prompt.reshape(N_CTX_ROWS, D_MODEL), x_sample.reshape(N_LAT_ROWS, D_MODEL))
    states = ()
    for l in range(DEPTH):
        (x,) = _ffn(xs, mod, l, 0, norm1_g, ffn1_wg, ffn1_wu, ffn1_wd)
        x, states = _ctx_mixers(x, mod, l, states, norm2_g, w_in, w_out, *mixer_params)
        x = _lat_mixers(x, mod, l, caches, state_c, state_d, rope, norm2_g, w_in, w_out, *mixer_params)
        xs = _ffn((x,), mod, l, 6, norm3_g, ffn2_wg, ffn2_wu, ffn2_wd, split_out=(l == DEPTH - 1))
    y_p, y_s = xs
    ka, va, kb, vb, st_c, st_d = states
    kv_shape = (BATCH, DEPTH, SEQ, 2, HEAD_DIM)
    return (y_p.reshape(BATCH, SEQ, D_MODEL), y_s.reshape(DEC_BATCH, DEC_SEQ, D_MODEL),
            ka.reshape(kv_shape), va.reshape(kv_shape), kb.reshape(kv_shape), vb.reshape(kv_shape),
            st_c, st_d)
```

```python
import functools
import math

import numpy as np
import jax
import jax.numpy as jnp
from jax import lax
from jax.experimental import pallas as pl
from jax.experimental.pallas import tpu as pltpu

F32 = jnp.float32
BF16 = jnp.bfloat16

D_MODEL = 1024
BATCH = 16
SEQ = 256
DEPTH = 2
DEC_BATCH = 2
DEC_SEQ = 1024
PAST_LEN = 512
GRID_W = 64
HEAD_DIM = 64
HEAD_SHIFT = 6
N_HEADS = 4
GROUP_W = 256
WINDOW = 128
ATT_BLOCK = 128
ROPE_BASE = 10000.0
LRU_C = 8.0
D_FF = 2816
N_MOD = 9
EPS = 1e-6
NEG_INF = -1e30
IN_WIDTH = 2560

N_CTX_ROWS = BATCH * SEQ
N_LAT_ROWS = DEC_BATCH * DEC_SEQ
N_ROWS = N_CTX_ROWS + N_LAT_ROWS
MOD_ROWS = 8
MOD_GROUP = 1024

VMEM_LIMIT_BYTES = 56 * 1024 * 1024

COL_AQ, COL_AK, COL_AV = 0, 256, 384
COL_BQ, COL_BK, COL_BV = 512, 768, 896
COL_CX, COL_CY = 1024, 1280
COL_DQ, COL_DK, COL_DV, COL_DG = 1536, 1792, 2048, 2304


def _cparams(*sem):
    return pltpu.CompilerParams(dimension_semantics=sem, vmem_limit_bytes=VMEM_LIMIT_BYTES)


def _dot(a, b):
    return jnp.dot(a, b, preferred_element_type=F32)


def _dot_nt(a, b):
    return lax.dot_general(a, b, (((1,), (1,)), ((), ())), preferred_element_type=F32)


def _dot_tn(a, b):
    return lax.dot_general(a, b, (((0,), (0,)), ((), ())), preferred_element_type=F32)


def _sigmoid(x):
    return 0.5 * jnp.tanh(0.5 * x) + 0.5


def _silu(x):
    return x * _sigmoid(x)


def _gelu_tanh(x):
    return 0.5 * x * (1.0 + jnp.tanh(math.sqrt(2.0 / math.pi) * (x + 0.044715 * (x * x * x))))


def _mod_row(i, tm, s):
    if tm >= MOD_GROUP:
        block_index = i * (tm // MOD_GROUP) + s
    else:
        block_index = i >> int(math.log2(MOD_GROUP // tm))
    return jnp.maximum(block_index - (N_CTX_ROWS // MOD_GROUP - 1), 0)


def _norm_mod(x, g, sc, sh):
    ms = jnp.mean(x * x, axis=-1, keepdims=True)
    return (x * lax.rsqrt(ms + EPS) * g) * (1.0 + sc) + sh


def _full(shape):
    return pl.BlockSpec(shape, lambda *_: (0,) * len(shape))


def _layer_block(shape, layer):
    return pl.BlockSpec((None,) + shape, lambda *_: (layer,) + (0,) * len(shape))


MOD_TN = 3072


def _mod_kernel(cc_ref, c_ref, w_ref, b_ref, o_ref):
    l = pl.program_id(0)
    pad = jnp.zeros((MOD_ROWS - 1 - DEC_BATCH, D_MODEL), F32)
    cond = jnp.concatenate([cc_ref[...], c_ref[...], pad], axis=0)
    o_ref[...] = _dot(_silu(cond).astype(BF16), w_ref[...].astype(BF16)) + b_ref[pl.ds(l, 1), :]


def _modulation(c_ctx, c, w_mod, b_mod):
    n = N_MOD * D_MODEL
    return pl.pallas_call(
        _mod_kernel,
        grid=(DEPTH, n // MOD_TN),
        in_specs=[
            pl.BlockSpec((1, D_MODEL), lambda l, j: (0, 0)),
            pl.BlockSpec((DEC_BATCH, D_MODEL), lambda l, j: (0, 0)),
            pl.BlockSpec((None, D_MODEL, MOD_TN), lambda l, j: (l, 0, j)),
            pl.BlockSpec((DEPTH, MOD_TN), lambda l, j: (0, j)),
        ],
        out_specs=pl.BlockSpec((None, MOD_ROWS, MOD_TN), lambda l, j: (l, 0, j)),
        out_shape=jax.ShapeDtypeStruct((DEPTH, MOD_ROWS, n), F32),
        compiler_params=_cparams("arbitrary", "arbitrary"),
        name="modulation",
    )(c_ctx.reshape(1, D_MODEL), c, w_mod, b_mod)


FFN_TM = 1024
FFN_TF = 256
N_CTX_TILES = N_CTX_ROWS // FFN_TM


FFN_NJ = D_FF // FFN_TF
N_FFN_TILES = N_ROWS // FFN_TM
N_FFN_STEPS = FFN_NJ + N_FFN_TILES


def _ffn_tile(step):
    return jnp.maximum(step - FFN_NJ, 0)


def _read_stream_tile(tile, x_refs):
    if len(x_refs) == 1:
        return x_refs[0][...]
    return jnp.where(tile < N_CTX_TILES, x_refs[0][...], x_refs[1][...])


def _write_stream_tile(tile, o_refs, make_value):
    if len(o_refs) == 1:
        o_refs[0][...] = make_value()
    else:
        pl.when(tile < N_CTX_TILES)(lambda: o_refs[0].__setitem__(Ellipsis, make_value()))
        pl.when(tile >= N_CTX_TILES)(lambda: o_refs[1].__setitem__(Ellipsis, make_value()))


def _ffn_kernel(*refs, layer, n_in, n_out):
    x_refs = refs[:n_in]
    n_ref, sh_ref, sc_ref, g_ref, wg_ref, wu_ref, wd_ref = refs[n_in:n_in + 7]
    o_refs = refs[n_in + 7:n_in + 7 + n_out]
    h_ref, a_ref, wg_s, wu_s, wd_s = refs[n_in + 7 + n_out:]
    nj, tf = FFN_NJ, FFN_TF
    s = pl.program_id(0)
    tile = _ffn_tile(s)
    r = _mod_row(tile, FFN_TM, 0)

    def load_tile():
        x = _read_stream_tile(tile, x_refs)
        h = _norm_mod(x, n_ref[layer:layer + 1, :], sc_ref[pl.ds(r, 1), :], sh_ref[pl.ds(r, 1), :])
        h_ref[...] = h.astype(BF16)

    def up_chunk(j, cols):
        h = h_ref[...]
        a_ref[:, cols] = (_silu(_dot(h, wg_s[j])) * _dot(h, wu_s[j])).astype(BF16)

    def down_and_store():
        y = (0.5 * g_ref[pl.ds(r, 1), :]) * _dot(a_ref[...], wd_s[...])
        _write_stream_tile(tile, o_refs, lambda: _read_stream_tile(tile, x_refs) + y)

    def keep_arrived_chunk():
        wg_s[s] = wg_ref[...].astype(BF16)
        wu_s[s] = wu_ref[...].astype(BF16)
        wd_s[pl.ds(pl.multiple_of(s * tf, tf), tf), :] = wd_ref[...].astype(BF16)

    def up_previous_chunk():
        up_chunk(s - 1, pl.ds(pl.multiple_of((s - 1) * tf, tf), tf))

    @pl.when(s == 0)
    def _():
        load_tile()
        keep_arrived_chunk()

    @pl.when((s > 0) & (s < nj))
    def _():
        up_previous_chunk()
        keep_arrived_chunk()

    @pl.when(s == nj)
    def _():
        up_previous_chunk()
        down_and_store()

    @pl.when(s > nj)
    def _():
        load_tile()
        for j in range(nj):
            up_chunk(j, slice(j * tf, (j + 1) * tf))
        down_and_store()


def _stream_specs(split, buffered_once):
    tm = FFN_TM
    kw = {"pipeline_mode": pl.Buffered(1)} if buffered_once else {}
    if not split:
        return [pl.BlockSpec((tm, D_MODEL), lambda s: (_ffn_tile(s), 0), **kw)]
    last_ctx = N_CTX_TILES - 1
    return [pl.BlockSpec((tm, D_MODEL), lambda s: (jnp.minimum(_ffn_tile(s), last_ctx), 0), **kw),
            pl.BlockSpec((tm, D_MODEL), lambda s: (jnp.maximum(_ffn_tile(s) - N_CTX_TILES, 0), 0), **kw)]


def _ffn(xs, mod, layer, chunk0, norm_g, wg, wu, wd, split_out=False):
    tm, tf, nj = FFN_TM, FFN_TF, FFN_NJ
    split_in = len(xs) == 2
    mod_spec = lambda c: pl.BlockSpec((None, MOD_ROWS, D_MODEL), lambda s: (layer, 0, c))
    w_col = lambda s: (layer, 0, jnp.minimum(s, nj - 1))
    w_row = lambda s: (layer, jnp.minimum(s, nj - 1), 0)
    if split_out:
        out_shape = [jax.ShapeDtypeStruct((N_CTX_ROWS, D_MODEL), F32),
                     jax.ShapeDtypeStruct((N_LAT_ROWS, D_MODEL), F32)]
    else:
        out_shape = [jax.ShapeDtypeStruct((N_ROWS, D_MODEL), F32)]
    out = pl.pallas_call(
        functools.partial(_ffn_kernel, layer=layer, n_in=len(xs), n_out=len(out_shape)),
        grid=(N_FFN_STEPS,),
        in_specs=_stream_specs(split_in, False) + [
            _full((DEPTH, D_MODEL)),
            mod_spec(chunk0), mod_spec(chunk0 + 1), mod_spec(chunk0 + 2),
            pl.BlockSpec((None, D_MODEL, tf), w_col),
            pl.BlockSpec((None, D_MODEL, tf), w_col),
            pl.BlockSpec((None, tf, D_MODEL), w_row),
        ],
        out_specs=_stream_specs(split_out, True),
        out_shape=out_shape,
        scratch_shapes=[pltpu.VMEM((tm, D_MODEL), BF16),
                        pltpu.VMEM((tm, D_FF), BF16),
                        pltpu.VMEM((nj, D_MODEL, tf), BF16),
                        pltpu.VMEM((nj, D_MODEL, tf), BF16),
                        pltpu.VMEM((D_FF, D_MODEL), BF16)],
        compiler_params=_cparams("arbitrary"),
        name="ffn",
    )(*xs, norm_g, mod, mod, mod, wg, wu, wd)
    return tuple(out)


def _once(shape, index_map):
    return pl.BlockSpec(shape, index_map, pipeline_mode=pl.Buffered(1))


def _mod_chunk(layer, c):
    return pl.BlockSpec((None, MOD_ROWS, D_MODEL), lambda *_: (layer, 0, c))


def _head_mean_square(x):
    n = x.shape[-1]
    r = lax.broadcasted_iota(jnp.int32, (n, n), 0) >> HEAD_SHIFT
    c = lax.broadcasted_iota(jnp.int32, (n, n), 1) >> HEAD_SHIFT
    ones_bd = jnp.where(r == c, 1.0, 0.0).astype(BF16)
    return _dot((x * x).astype(BF16), ones_bd) * (1.0 / HEAD_DIM)


def _head_norm(x, head_gain):
    gain_row = jnp.concatenate([head_gain] * (x.shape[-1] // HEAD_DIM), axis=-1)
    return x * lax.rsqrt(_head_mean_square(x) + EPS) * gain_row


def _head_cols(x, h):
    return x[:, h * HEAD_DIM:(h + 1) * HEAD_DIM].astype(BF16)


def _softmax_pv(scores, values, sink):
    m = jnp.max(scores[0], axis=-1, keepdims=True)
    for s in scores[1:]:
        m = jnp.maximum(m, jnp.max(s, axis=-1, keepdims=True))
    if sink is not None:
        m = jnp.maximum(m, sink)
    denom = None
    acc = None
    for s, v in zip(scores, values):
        p = jnp.exp(s - m)
        d = jnp.sum(p, axis=-1, keepdims=True)
        o = _dot(p.astype(BF16), v)
        denom = d if denom is None else denom + d
        acc = o if acc is None else acc + o
    if sink is not None:
        denom = denom + jnp.exp(sink - m)
    return acc / denom


def _rope(x, cos, sin_lo, sin_hi):
    cols = []
    for c in range(x.shape[-1] // 128):
        xc = x[:, c * 128:(c + 1) * 128]
        cols.append(xc * cos + pltpu.roll(xc, 112, 1) * sin_lo + pltpu.roll(xc, 16, 1) * sin_hi)
    return cols[0] if len(cols) == 1 else jnp.concatenate(cols, axis=-1)


def _block_diag(blocks):
    n = len(blocks)
    w = blocks[0].shape[0]
    rows = []
    for k, blk in enumerate(blocks):
        parts = []
        if k > 0:
            parts.append(jnp.zeros((w, k * w), F32))
        parts.append(blk)
        if k < n - 1:
            parts.append(jnp.zeros((w, (n - 1 - k) * w), F32))
        rows.append(jnp.concatenate(parts, axis=-1))
    return jnp.concatenate(rows, axis=0)


def _rglru_gates(xc, wa, ba, wx, bx, lam):
    xb = xc.astype(BF16)
    r = _sigmoid(_dot(xb, wa) + ba)
    i = _sigmoid(_dot(xb, wx) + bx)
    softplus = jnp.maximum(-lam, 0.0) + jnp.log1p(jnp.exp(-jnp.abs(lam)))
    log_a = (-LRU_C) * r * softplus
    a = jnp.exp(log_a)
    b = jnp.sqrt(1.0 - a * a) * (i * xc)
    return a, b


def _block_prefix(a, b, reverse):
    t = a.shape[0]
    row = lax.broadcasted_iota(jnp.int32, a.shape, 0) & 7
    for d in (1, 2, 4):
        if reverse:
            a_s = pltpu.roll(a, t - d, 0)
            b_s = pltpu.roll(b, t - d, 0)
            ok = row < 8 - d
        else:
            a_s = pltpu.roll(a, d, 0)
            b_s = pltpu.roll(b, d, 0)
            ok = row >= d
        b = jnp.where(ok, a * b_s + b, b)
        a = jnp.where(ok, a * a_s, a)
    return a, b


def _conv4(x, w_ref, b_row):
    t = x.shape[0]
    row = lax.broadcasted_iota(jnp.int32, x.shape, 0)
    xm2 = jnp.where(row >= 2, pltpu.roll(x, 2, 0), 0.0)
    xm1 = jnp.where(row >= 1, pltpu.roll(x, 1, 0), 0.0)
    xp1 = jnp.where(row < t - 1, pltpu.roll(x, t - 1, 0), 0.0)
    return (xm2 * w_ref[0:1, :] + xm1 * w_ref[1:2, :] + x * w_ref[2:3, :] + xp1 * w_ref[3:4, :]) + b_row


def _rglru_prepare(cx, cy, conv_w_ref, conv_b, gate_w_ref, ba_ref, bx_ref, lam_ref,
                   af_ref, bf_ref, ab_ref, bb_ref, gel_ref):
    xc = _conv4(cx, conv_w_ref, conv_b)
    a, b = _rglru_gates(xc, gate_w_ref[0], ba_ref[0:1, :], gate_w_ref[1], bx_ref[0:1, :], lam_ref[0:1, :])
    a, b = _block_prefix(a, b, reverse=False)
    af_ref[...] = a
    bf_ref[...] = b
    a, b = _rglru_gates(xc, gate_w_ref[2], ba_ref[1:2, :], gate_w_ref[3], bx_ref[1:2, :], lam_ref[1:2, :])
    a, b = _block_prefix(a, b, reverse=True)
    ab_ref[...] = a
    bb_ref[...] = b
    gel_ref[...] = _gelu_tanh(cy)


def _rglru_finish(h0f, h0b, af_ref, bf_ref, ab_ref, bb_ref, hf_ref, hb_ref, gel_ref):
    nblk = af_ref.shape[0] // 8

    def body(k, carry):
        cf, cb = carry
        rf = pl.ds(pl.multiple_of(k * 8, 8), 8)
        hf = bf_ref[rf, :] + af_ref[rf, :] * cf
        hf_ref[rf, :] = hf
        rb = pl.ds(pl.multiple_of((nblk - 1 - k) * 8, 8), 8)
        hb = bb_ref[rb, :] + ab_ref[rb, :] * cb
        hb_ref[rb, :] = hb
        return hf[7:8, :], hb[0:1, :]

    cf, cb = lax.fori_loop(0, nblk, body, (h0f, h0b), unroll=8)
    oc = (hf_ref[...] + hb_ref[...]) * gel_ref[...]
    return oc, cf, cb


def _store_gate_weights(gate_w_ref, wa_ref, wx_ref):
    for d in range(2):
        gate_w_ref[2 * d] = _block_diag([wa_ref[d, n] for n in range(N_HEADS)]).astype(BF16)
        gate_w_ref[2 * d + 1] = _block_diag([wx_ref[d, n] for n in range(N_HEADS)]).astype(BF16)


def _lane_head_masks(n):
    lane = lax.broadcasted_iota(jnp.int32, (1, n), 1) >> HEAD_SHIFT
    return [jnp.where(lane == h, 1.0, 0.0) for h in range(n // HEAD_DIM)]


def _log_decays(theta_ref, masks):
    theta = theta_ref[...]
    lanes = theta[:, 0:1] * masks[0]
    for h in range(1, N_HEADS):
        lanes = lanes + theta[:, h:h + 1] * masks[h]
    lg = jnp.log1p(-jnp.exp(lanes))
    return lg[0:1, :], lg[1:2, :]


RET_BLOCK = 256


def _retention(q, k8, vb, s0, lgf, lgb, masks, o_ref):
    t, w = q.shape
    c = RET_BLOCK
    nh = w // HEAD_DIM
    pos = lax.broadcasted_iota(jnp.int32, (c, w), 0).astype(F32)
    q_dec = (jnp.exp(lgf * (pos + 1.0)), jnp.exp(lgb * (float(c) - pos)))
    k_dec = (jnp.exp(lgf * (float(c - 1) - pos)), jnp.exp(lgb * pos))
    chunk_dec = (jnp.exp(lgf * float(c)), jnp.exp(lgb * float(c)))
    rel = (lax.broadcasted_iota(jnp.int32, (c, c), 0) - lax.broadcasted_iota(jnp.int32, (c, c), 1)).astype(F32)
    decs = []
    for h in range(nh):
        gf = lgf[:, h * HEAD_DIM:h * HEAD_DIM + 1]
        gb = lgb[:, h * HEAD_DIM:h * HEAD_DIM + 1]
        e = jnp.exp(jnp.where(rel >= 0, gf * rel, gb * (-rel)))
        decs.append(jnp.where(rel == 0, 2.0, e))
    dec = jnp.concatenate(decs, axis=0)
    r_head = lax.broadcasted_iota(jnp.int32, (w, w), 0) >> HEAD_SHIFT
    c_head = lax.broadcasted_iota(jnp.int32, (w, w), 1) >> HEAD_SHIFT
    same_head = jnp.where(r_head == c_head, 1.0, 0.0)
    states = [None, None] if s0 is None else list(s0)

    def carry(d, rows, o):
        if states[d] is not None:
            o = o + _dot((q[rows, :] * q_dec[d]).astype(BF16), states[d].astype(BF16))
        upd = _dot_tn((k8[rows, :] * k_dec[d]).astype(BF16), vb[rows, :]) * same_head
        states[d] = upd if states[d] is None else states[d] * chunk_dec[d] + upd
        return o

    for ci in range(t // c):
        rows = slice(ci * c, (ci + 1) * c)
        qc = q[rows, :]
        q_stack = jnp.concatenate([(qc * masks[h]).astype(BF16) for h in range(nh)], axis=0)
        inner = (_dot_nt(q_stack, k8[rows, :].astype(BF16)) * dec).astype(BF16)
        out = _dot(inner, vb[rows, :])
        o = out[0:c, :] * masks[0]
        for h in range(1, nh):
            o = o + out[h * c:(h + 1) * c, :] * masks[h]
        o_ref[rows, :] = carry(0, rows, o)
    for ci in reversed(range(t // c)):
        rows = slice(ci * c, (ci + 1) * c)
        if states[1] is not None:
            o_ref[rows, :] = carry(1, rows, o_ref[rows, :])
        else:
            carry(1, rows, None)
    return states[0], states[1]


def _ctx_mixer_kernel(*refs, layer, n_prev):
    prev_refs = refs[:n_prev]
    (x_ref, n2_ref, sh_ref, sc_ref, g2_ref, win_ref, wout_ref,
     aqn_ref, akn_ref, bqn_ref, bkn_ref, sink_ref,
     convw_ref, convb_ref, wa_ref, ba_ref, wx_ref, bx_ref, lam_ref, theta_ref, dn_ref,
     xn_ref, *state_refs) = refs[n_prev:n_prev + 28]
    (win_s, wout_s, u_ref, mixed_ref,
     gate_w_ref, af_ref, bf_ref, ab_ref, bb_ref, hf_ref, hb_ref, gel_ref, ret_ref) = refs[n_prev + 28:]
    t = SEQ
    lrow = slice(layer, layer + 1)
    for prev_ref, state_ref in zip(prev_refs, state_refs):
        for earlier in range(layer):
            state_ref[earlier] = prev_ref[earlier]
    ka_ref, va_ref, kb_ref, vb_ref, stc_ref, std_ref = (ref.at[layer] for ref in state_refs)

    @pl.when(pl.program_id(0) == 0)
    def _():
        for c in range(IN_WIDTH // 512):
            win_s[:, c * 512:(c + 1) * 512] = win_ref[:, c * 512:(c + 1) * 512].astype(BF16)
        wout_s[...] = wout_ref[...].astype(BF16)
        _store_gate_weights(gate_w_ref, wa_ref, wx_ref)

    x = x_ref[...]
    h = _norm_mod(x, n2_ref[lrow, :], sc_ref[0:1, :], sh_ref[0:1, :]).astype(BF16)
    c_cols = slice(COL_CX, COL_CX + 2 * GROUP_W)
    u_ref[:, c_cols] = _dot(h, win_s[:, c_cols])
    _rglru_prepare(u_ref[:, COL_CX:COL_CX + GROUP_W], u_ref[:, COL_CY:COL_CY + GROUP_W],
                   convw_ref, convb_ref[lrow, :], gate_w_ref, ba_ref, bx_ref, lam_ref,
                   af_ref, bf_ref, ab_ref, bb_ref, gel_ref)
    u_ref[:, 0:COL_CX] = _dot(h, win_s[:, 0:COL_CX])
    u_ref[:, COL_DQ:IN_WIDTH] = _dot(h, win_s[:, COL_DQ:IN_WIDTH])

    for (cq, ck, cv, qn_ref, kn_ref, k_out, v_out, col0, use_sink) in (
            (COL_AQ, COL_AK, COL_AV, aqn_ref, akn_ref, ka_ref, va_ref, 0, True),
            (COL_BQ, COL_BK, COL_BV, bqn_ref, bkn_ref, kb_ref, vb_ref, GROUP_W, False)):
        q = _head_norm(u_ref[:, cq:cq + 256], qn_ref[lrow, :])
        k = _head_norm(u_ref[:, ck:ck + 128], kn_ref[lrow, :])
        v = u_ref[:, cv:cv + 128]
        k_out[...] = k
        v_out[...] = v
        qs = q * (HEAD_DIM ** -0.5)
        heads = []
        for hd in range(N_HEADS):
            kv = hd // 2
            s = _dot_nt(_head_cols(qs, hd), _head_cols(k, kv))
            sink = jnp.full((t, 1), sink_ref[layer, hd], F32) if use_sink else None
            heads.append(_softmax_pv([s], [_head_cols(v, kv)], sink))
        mixed_ref[:, col0:col0 + GROUP_W] = jnp.concatenate(heads, axis=-1).astype(BF16)

    zero = jnp.zeros((1, GROUP_W), F32)
    oc, cf, cb = _rglru_finish(zero, zero, af_ref, bf_ref, ab_ref, bb_ref, hf_ref, hb_ref, gel_ref)
    mixed_ref[:, 2 * GROUP_W:3 * GROUP_W] = oc.astype(BF16)
    stc_ref[0:1, :] = cf
    stc_ref[1:2, :] = cb

    masks = _lane_head_masks(GROUP_W)
    lgf, lgb = _log_decays(theta_ref, masks)
    k8 = u_ref[:, COL_DK:COL_DK + GROUP_W] * (HEAD_DIM ** -0.5)
    vb = u_ref[:, COL_DV:COL_DV + GROUP_W].astype(BF16)
    final_states = _retention(u_ref[:, COL_DQ:COL_DQ + GROUP_W], k8, vb, None, lgf, lgb, masks, ret_ref)
    o = ret_ref[...]
    o = o * lax.rsqrt(_head_mean_square(o) + EPS) * dn_ref[lrow, :] * _silu(u_ref[:, COL_DG:COL_DG + GROUP_W])
    mixed_ref[:, 3 * GROUP_W:4 * GROUP_W] = o.astype(BF16)
    for d, s_full in enumerate(final_states):
        for hd in range(N_HEADS):
            std_ref[d, hd] = s_full[hd * 64:(hd + 1) * 64, hd * 64:(hd + 1) * 64]

    xn_ref[...] = x + g2_ref[0:1, :] * _dot(mixed_ref[...], wout_s[...])


def _ctx_mixers(x, mod, layer, prev, norm2_g, w_in, w_out,
                a_qn, a_kn, a_sink, b_qn, b_kn, c_conv_w, c_conv_b, c_wa, c_ba, c_wx, c_bx,
                c_lambda, d_theta, d_norm_g):
    per_request = lambda slots, shape: pl.BlockSpec((None, slots) + shape, lambda b: (b,) + (0,) * (1 + len(shape)))
    state_dims = [(SEQ, 128)] * 4 + [(2, GROUP_W), (2, N_HEADS, HEAD_DIM, HEAD_DIM)]
    scr = pltpu.VMEM((SEQ, GROUP_W), F32)
    out = pl.pallas_call(
        functools.partial(_ctx_mixer_kernel, layer=layer, n_prev=len(prev)),
        grid=(BATCH,),
        in_specs=[per_request(layer, dims) for dims in state_dims[:len(prev)]] + [
            pl.BlockSpec((SEQ, D_MODEL), lambda b: (b, 0)),
            _full((DEPTH, D_MODEL)),
            _mod_chunk(layer, 3), _mod_chunk(layer, 4), _mod_chunk(layer, 5),
            _once((None, D_MODEL, IN_WIDTH), lambda b: (layer, 0, 0)),
            _once((None, D_MODEL, D_MODEL), lambda b: (layer, 0, 0)),
            _full((DEPTH, HEAD_DIM)), _full((DEPTH, HEAD_DIM)), _full((DEPTH, HEAD_DIM)), _full((DEPTH, HEAD_DIM)),
            pl.BlockSpec(memory_space=pltpu.SMEM),
            _layer_block((4, GROUP_W), layer), _full((DEPTH, GROUP_W)),
            _layer_block((2, N_HEADS, HEAD_DIM, HEAD_DIM), layer), _layer_block((2, GROUP_W), layer),
            _layer_block((2, N_HEADS, HEAD_DIM, HEAD_DIM), layer), _layer_block((2, GROUP_W), layer),
            _layer_block((2, GROUP_W), layer),
            _layer_block((2, N_HEADS), layer), _full((DEPTH, GROUP_W)),
        ],
        out_specs=[pl.BlockSpec((SEQ, D_MODEL), lambda b: (b, 0))] + [
            per_request(layer + 1, dims) for dims in state_dims],
        out_shape=[jax.ShapeDtypeStruct((N_ROWS, D_MODEL), F32)] + [
            jax.ShapeDtypeStruct((BATCH, layer + 1) + dims, F32) for dims in state_dims],
        input_output_aliases={len(prev): 0},
        scratch_shapes=[pltpu.VMEM((D_MODEL, IN_WIDTH), BF16), pltpu.VMEM((D_MODEL, D_MODEL), BF16),
                        pltpu.VMEM((SEQ, IN_WIDTH), F32), pltpu.VMEM((SEQ, D_MODEL), BF16),
                        pltpu.VMEM((4, GROUP_W, GROUP_W), BF16)] + [scr] * 8,
        compiler_params=_cparams("arbitrary"),
        name="ctx_mixers",
    )(*prev, x, norm2_g, mod, mod, mod, w_in, w_out,
      a_qn, a_kn, b_qn, b_kn, a_sink, c_conv_w, c_conv_b, c_wa, c_ba, c_wx, c_bx,
      c_lambda, d_theta, d_norm_g)
    return out[0], tuple(out[1:])


LAT_BLOCK0 = N_CTX_ROWS // DEC_SEQ


def _lat_attn_kernel(x_ref, n2_ref, sh_ref, sc_ref, g2_ref, win_ref, wout_ref,
                     kca_ref, vca_ref, kcb_ref, vcb_ref,
                     aqn_ref, akn_ref, bqn_ref, bkn_ref, sink_ref, cos_ref, sinl_ref, sinh_ref,
                     xn_ref, h_ref, u_ref, o_ref, *, layer):
    t = DEC_SEQ
    lrow = slice(layer, layer + 1)
    mrow = pl.ds(1 + pl.program_id(0), 1)
    cos, sin_lo, sin_hi = cos_ref[...], sinl_ref[...], sinh_ref[...]
    scale = HEAD_DIM ** -0.5
    x = x_ref[...]
    h_ref[...] = _norm_mod(x, n2_ref[lrow, :], sc_ref[mrow, :], sh_ref[mrow, :]).astype(BF16)
    u_ref[...] = _dot(h_ref[...], win_ref[...].astype(BF16))

    q = _rope(_head_norm(u_ref[:, COL_AQ:COL_AQ + 256], aqn_ref[lrow, :]), cos, sin_lo, sin_hi)
    k = _rope(_head_norm(u_ref[:, COL_AK:COL_AK + 128], akn_ref[lrow, :]), cos, sin_lo, sin_hi)
    qh = [_head_cols(q * scale, h) for h in range(4)]
    v = u_ref[:, COL_AV:COL_AV + 128]
    kh = [_head_cols(k, kv) for kv in range(2)]
    vh = [_head_cols(v, kv) for kv in range(2)]
    kch = [_head_cols(kca_ref[...], kv) for kv in range(2)]
    vch = [_head_cols(vca_ref[...], kv) for kv in range(2)]
    w = ATT_BLOCK
    span = 3 * w
    for n in range(t // w):
        start = min(max((n - 1) * w, 0), t - span)
        rows = slice(n * w, (n + 1) * w)
        band = slice(start, start + span)
        qpos = (lax.broadcasted_iota(jnp.int32, (2 * w, span), 0) & (w - 1)) + n * w
        kpos = lax.broadcasted_iota(jnp.int32, (2 * w, span), 1) + start
        valid = jnp.abs(qpos - kpos) <= WINDOW
        heads = []
        for kv in range(2):
            qp = jnp.concatenate([qh[2 * kv][rows, :], qh[2 * kv + 1][rows, :]], axis=0)
            s_ctx = _dot_nt(qp, kch[kv])
            s_band = jnp.where(valid, _dot_nt(qp, kh[kv][band, :]), NEG_INF)
            row = lax.broadcasted_iota(jnp.int32, (2 * w, 1), 0)
            sink = jnp.where(row < w, sink_ref[layer, 2 * kv], sink_ref[layer, 2 * kv + 1])
            o = _softmax_pv([s_ctx, s_band], [vch[kv], vh[kv][band, :]], sink)
            heads += [o[0:w, :], o[w:2 * w, :]]
        o_ref[rows, 0:GROUP_W] = jnp.concatenate(heads, axis=-1).astype(BF16)

    q = _rope(_head_norm(u_ref[:, COL_BQ:COL_BQ + 256], bqn_ref[lrow, :]), cos, sin_lo, sin_hi)
    k = _rope(_head_norm(u_ref[:, COL_BK:COL_BK + 128], bkn_ref[lrow, :]), cos, sin_lo, sin_hi)
    qh = [_head_cols(q * scale, h) for h in range(4)]
    v = u_ref[:, COL_BV:COL_BV + 128]
    kh = [_head_cols(k, kv) for kv in range(2)]
    vh = [_head_cols(v, kv) for kv in range(2)]
    kch = [_head_cols(kcb_ref[...], kv) for kv in range(2)]
    vch = [_head_cols(vcb_ref[...], kv) for kv in range(2)]
    tq = 256
    for n in range(t // tq):
        rows = slice(n * tq, (n + 1) * tq)
        heads = []
        for kv in range(2):
            qp = jnp.concatenate([qh[2 * kv][rows, :], qh[2 * kv + 1][rows, :]], axis=0)
            o = _softmax_pv([_dot_nt(qp, kch[kv]), _dot_nt(qp, kh[kv])], [vch[kv], vh[kv]], None)
            heads += [o[0:tq, :], o[tq:2 * tq, :]]
        o_ref[rows, GROUP_W:2 * GROUP_W] = jnp.concatenate(heads, axis=-1).astype(BF16)

    xn_ref[...] = x + g2_ref[mrow, :] * _dot(o_ref[...], wout_ref[...].astype(BF16))


def _lat_recurrent_kernel(xn_in_ref, h_ref, g2_ref, wc_ref, wqk_ref, wvg_ref, wout_ref, h0_ref,
                          convw_ref, convb_ref, wa_ref, ba_ref, wx_ref, bx_ref, lam_ref,
                          s0_ref, theta_ref, dn_ref,
                          xn_ref, gate_w_ref, af_ref, bf_ref, ab_ref, bb_ref, hf_ref, hb_ref, gel_ref, ret_ref,
                          *, layer):
    lrow = slice(layer, layer + 1)
    mrow = pl.ds(1 + pl.program_id(0), 1)

    @pl.when(pl.program_id(0) == 0)
    def _():
        _store_gate_weights(gate_w_ref, wa_ref, wx_ref)

    h = h_ref[...]
    u = _dot(h, wc_ref[...].astype(BF16))
    _rglru_prepare(u[:, 0:GROUP_W], u[:, GROUP_W:2 * GROUP_W], convw_ref, convb_ref[lrow, :],
                   gate_w_ref, ba_ref, bx_ref, lam_ref, af_ref, bf_ref, ab_ref, bb_ref, gel_ref)
    oc, _, _ = _rglru_finish(h0_ref[0:1, :], h0_ref[1:2, :],
                             af_ref, bf_ref, ab_ref, bb_ref, hf_ref, hb_ref, gel_ref)
    y = _dot(oc.astype(BF16), wout_ref[0:GROUP_W, :].astype(BF16))

    uqk = _dot(h, wqk_ref[...].astype(BF16))
    uvg = _dot(h, wvg_ref[...].astype(BF16))
    masks = _lane_head_masks(GROUP_W)
    lgf, lgb = _log_decays(theta_ref, masks)
    s0 = tuple(_block_diag([s0_ref[d, hd] for hd in range(N_HEADS)]) for d in range(2))
    _retention(uqk[:, 0:GROUP_W], uqk[:, GROUP_W:2 * GROUP_W] * (HEAD_DIM ** -0.5),
               uvg[:, 0:GROUP_W].astype(BF16), s0, lgf, lgb, masks, ret_ref)
    o = ret_ref[...]
    o = o * lax.rsqrt(_head_mean_square(o) + EPS) * dn_ref[lrow, :] * _silu(uvg[:, GROUP_W:2 * GROUP_W])
    y = y + _dot(o.astype(BF16), wout_ref[GROUP_W:2 * GROUP_W, :].astype(BF16))
    xn_ref[...] = xn_in_ref[...] + g2_ref[mrow, :] * y


def _lat_mixers(x, mod, layer, caches, state_c, state_d, rope, norm2_g, w_in, w_out,
                a_qn, a_kn, a_sink, b_qn, b_kn, c_conv_w, c_conv_b, c_wa, c_ba, c_wx, c_bx,
                c_lambda, d_theta, d_norm_g):
    rows = pl.BlockSpec((DEC_SEQ, D_MODEL), lambda b: (LAT_BLOCK0 + b, 0))
    h_rows = pl.BlockSpec((DEC_SEQ, D_MODEL), lambda b: (b, 0))
    cache_spec = pl.BlockSpec((None, None, PAST_LEN, 128), lambda b: (b, layer, 0, 0))
    gain = _full((DEPTH, HEAD_DIM))
    table = _once((DEC_SEQ, 128), lambda b: (0, 0))
    out_shape = jax.ShapeDtypeStruct((N_ROWS, D_MODEL), F32)
    win_cols = lambda w, c: _once((None, D_MODEL, w), lambda b: (layer, 0, c))
    wout_rows = lambda h, r: _once((None, h, D_MODEL), lambda b: (layer, r, 0))

    xn, h = pl.pallas_call(
        functools.partial(_lat_attn_kernel, layer=layer),
        grid=(DEC_BATCH,),
        in_specs=[rows, _full((DEPTH, D_MODEL)),
                  _mod_chunk(layer, 3), _mod_chunk(layer, 4), _mod_chunk(layer, 5),
                  win_cols(4 * GROUP_W, 0), wout_rows(2 * GROUP_W, 0),
                  cache_spec, cache_spec, cache_spec, cache_spec,
                  gain, gain, gain, gain,
                  pl.BlockSpec(memory_space=pltpu.SMEM),
                  table, table, table],
        out_specs=[rows, pl.BlockSpec((DEC_SEQ, D_MODEL), lambda b: (b, 0), pipeline_mode=pl.Buffered(1))],
        out_shape=[out_shape, jax.ShapeDtypeStruct((N_LAT_ROWS, D_MODEL), BF16)],
        input_output_aliases={0: 0},
        scratch_shapes=[pltpu.VMEM((DEC_SEQ, 4 * GROUP_W), F32), pltpu.VMEM((DEC_SEQ, 2 * GROUP_W), BF16)],
        compiler_params=_cparams("arbitrary"),
        name="lat_attention",
    )(x, norm2_g, mod, mod, mod, w_in, w_out, *caches, a_qn, a_kn, b_qn, b_kn, a_sink, *rope)

    scr = pltpu.VMEM((DEC_SEQ, GROUP_W), F32)
    xn = pl.pallas_call(
        functools.partial(_lat_recurrent_kernel, layer=layer),
        grid=(DEC_BATCH,),
        in_specs=[
            rows, h_rows, _mod_chunk(layer, 5),
            win_cols(2 * GROUP_W, COL_CX // (2 * GROUP_W)),
            win_cols(2 * GROUP_W, COL_DQ // (2 * GROUP_W)), win_cols(2 * GROUP_W, COL_DV // (2 * GROUP_W)),
            wout_rows(2 * GROUP_W, 1),
            pl.BlockSpec((None, None, 2, GROUP_W), lambda b: (b, layer, 0, 0)),
            _layer_block((4, GROUP_W), layer), _full((DEPTH, GROUP_W)),
            _layer_block((2, N_HEADS, HEAD_DIM, HEAD_DIM), layer), _layer_block((2, GROUP_W), layer),
            _layer_block((2, N_HEADS, HEAD_DIM, HEAD_DIM), layer), _layer_block((2, GROUP_W), layer),
            _layer_block((2, GROUP_W), layer),
            pl.BlockSpec((None, None, 2, N_HEADS, HEAD_DIM, HEAD_DIM), lambda b: (b, layer, 0, 0, 0, 0)),
            _layer_block((2, N_HEADS), layer), _full((DEPTH, GROUP_W))],
        out_specs=rows,
        out_shape=out_shape,
        input_output_aliases={0: 0},
        scratch_shapes=[pltpu.VMEM((4, GROUP_W, GROUP_W), BF16)] + [scr] * 8,
        compiler_params=_cparams("arbitrary"),
        name="lat_recurrent",
    )(xn, h, mod, w_in, w_in, w_in, w_out, state_c, c_conv_w, c_conv_b, c_wa, c_ba, c_wx, c_bx, c_lambda,
      state_d, d_theta, d_norm_g)
    return xn


def _rope_tables():
    t = np.arange(DEC_SEQ)
    row = (t // GRID_W).astype(np.float64)[:, None]
    col = (t % GRID_W).astype(np.float64)[:, None]
    half = HEAD_DIM // 2
    inv = 1.0 / (ROPE_BASE ** (np.arange(0, half, 2, dtype=np.float64) / half))
    j = np.arange(128) % HEAD_DIM
    ang = np.where((j < half)[None, :], row, col) * inv[j % (half // 2)][None, :]
    first = ((j % half) < half // 2)[None, :]
    cos, sin = np.cos(ang), np.sin(ang)
    return tuple(jnp.asarray(a, F32) for a in (cos, np.where(first, -sin, 0.0), np.where(first, 0.0, sin)))


def kernel(x_prompt, x_sample, cache_a_k, cache_a_v, cache_b_k, cache_b_v, state_c, state_d, c, c_ctx, norm1_g, norm2_g, norm3_g, w_mod, b_mod, ffn1_wg, ffn1_wu, ffn1_wd, ffn2_wg, ffn2_wu, ffn2_wd, w_in, w_out, a_qn, a_kn, a_sink, b_qn, b_kn, c_conv_w, c_conv_b, c_wa, c_ba, c_wx, c_bx, c_lambda, d_theta, d_norm_g):
    mod = _modulation(c_ctx, c, w_mod, b_mod)
    rope = _rope_tables()
    caches = tuple(t.reshape(DEC_BATCH, DEPTH, PAST_LEN, 128) for t in (cache_a_k, cache_a_v, cache_b_k, cache_b_v))
    mixer_params = (a_qn, a_kn, a_sink, b_qn, b_kn, c_conv_w, c_conv_b, c_wa, c_ba, c_wx, c_bx,
                    c_lambda, d_theta, d_norm_g)
    xs = (x_prompt.reshape(N_CTX_ROWS, D_MODEL), x_sample.reshape(N_LAT_ROWS, D_MODEL))
    states = ()
    for l in range(DEPTH):
        (x,) = _ffn(xs, mod, l, 0, norm1_g, ffn1_wg, ffn1_wu, ffn1_wd)
        x, states = _ctx_mixers(x, mod, l, states, norm2_g, w_in, w_out, *mixer_params)
        x = _lat_mixers(x, mod, l, caches, state_c, state_d, rope, norm2_g, w_in, w_out, *mixer_params)
        xs = _ffn((x,), mod, l, 6, norm3_g, ffn2_wg, ffn2_wu, ffn2_wd, split_out=(l == DEPTH - 1))
    y_p, y_s = xs
    ka, va, kb, vb, st_c, st_d = states
    kv_shape = (BATCH, DEPTH, SEQ, 2, HEAD_DIM)
    return (y_p.reshape(BATCH, SEQ, D_MODEL), y_s.reshape(DEC_BATCH, DEC_SEQ, D_MODEL),
            ka.reshape(kv_shape), va.reshape(kv_shape), kb.reshape(kv_shape), vb.reshape(kv_shape),
            st_c, st_d)
```

```python
import functools
import math

import numpy as np
import jax
import jax.numpy as jnp
from jax import lax
from jax.experimental import pallas as pl
from jax.experimental.pallas import tpu as pltpu

F32 = jnp.float32
BF16 = jnp.bfloat16

D_MODEL = 1024
BATCH = 16
SEQ = 256
DEPTH = 2
DEC_BATCH = 2
DEC_SEQ = 1024
PAST_LEN = 512
GRID_W = 64
HEAD_DIM = 64
HEAD_SHIFT = 6
N_HEADS = 4
GROUP_W = 256
KV_W = 2 * HEAD_DIM
LANES = 128
WINDOW = 128
ATT_BLOCK = 128
ROPE_BASE = 10000.0
LRU_C = 8.0
D_FF = 2816
N_MOD = 9
EPS = 1e-6
NEG_INF = -1e30
IN_WIDTH = 2560

N_CTX_ROWS = BATCH * SEQ
N_LAT_ROWS = DEC_BATCH * DEC_SEQ
N_ROWS = N_CTX_ROWS + N_LAT_ROWS
MOD_ROWS = 8
MOD_GROUP = 1024

VMEM_LIMIT_BYTES = 56 * 1024 * 1024

COL_AQ, COL_AK, COL_AV = 0, 256, 384
COL_BQ, COL_BK, COL_BV = 512, 768, 896
COL_CX, COL_CY = 1024, 1280
COL_DQ, COL_DK, COL_DV, COL_DG = 1536, 1792, 2048, 2304


def _cparams(*sem):
    return pltpu.CompilerParams(dimension_semantics=sem, vmem_limit_bytes=VMEM_LIMIT_BYTES)


def _dot(a, b):
    return jnp.dot(a, b, preferred_element_type=F32)


def _dot_nt(a, b):
    return lax.dot_general(a, b, (((1,), (1,)), ((), ())), preferred_element_type=F32)


def _dot_tn(a, b):
    return lax.dot_general(a, b, (((0,), (0,)), ((), ())), preferred_element_type=F32)


def _sigmoid(x):
    return 0.5 * jnp.tanh(0.5 * x) + 0.5


def _silu(x):
    return x * _sigmoid(x)


def _gelu_tanh(x):
    return 0.5 * x * (1.0 + jnp.tanh(math.sqrt(2.0 / math.pi) * (x + 0.044715 * (x * x * x))))


def _mod_row(i, tm, s):
    if tm >= MOD_GROUP:
        block_index = i * (tm // MOD_GROUP) + s
    else:
        block_index = i >> int(math.log2(MOD_GROUP // tm))
    return jnp.maximum(block_index - (N_CTX_ROWS // MOD_GROUP - 1), 0)


def _norm_mod(x, g, sc, sh):
    ms = jnp.mean(x * x, axis=-1, keepdims=True)
    return (x * lax.rsqrt(ms + EPS) * g) * (1.0 + sc) + sh


def _full(shape):
    return pl.BlockSpec(shape, lambda *_: (0,) * len(shape))


def _layer_block(shape, layer):
    return pl.BlockSpec((None,) + shape, lambda *_: (layer,) + (0,) * len(shape))


MOD_TN = 3072


def _mod_kernel(cc_ref, c_ref, w_ref, b_ref, o_ref):
    l = pl.program_id(0)
    pad = jnp.zeros((MOD_ROWS - 1 - DEC_BATCH, D_MODEL), F32)
    cond = jnp.concatenate([cc_ref[...], c_ref[...], pad], axis=0)
    o_ref[...] = _dot(_silu(cond).astype(BF16), w_ref[...].astype(BF16)) + b_ref[pl.ds(l, 1), :]


def _modulation(c_ctx, c, w_mod, b_mod):
    n = N_MOD * D_MODEL
    return pl.pallas_call(
        _mod_kernel,
        grid=(DEPTH, n // MOD_TN),
        in_specs=[
            pl.BlockSpec((1, D_MODEL), lambda l, j: (0, 0)),
            pl.BlockSpec((DEC_BATCH, D_MODEL), lambda l, j: (0, 0)),
            pl.BlockSpec((None, D_MODEL, MOD_TN), lambda l, j: (l, 0, j)),
            pl.BlockSpec((DEPTH, MOD_TN), lambda l, j: (0, j)),
        ],
        out_specs=pl.BlockSpec((None, MOD_ROWS, MOD_TN), lambda l, j: (l, 0, j)),
        out_shape=jax.ShapeDtypeStruct((DEPTH, MOD_ROWS, n), F32),
        compiler_params=_cparams("arbitrary", "arbitrary"),
        name="modulation",
    )(c_ctx.reshape(1, D_MODEL), c, w_mod, b_mod)


FFN_TM = 1024
FFN_TF = 256
N_CTX_TILES = N_CTX_ROWS // FFN_TM


FFN_NJ = D_FF // FFN_TF
N_FFN_TILES = N_ROWS // FFN_TM
N_FFN_STEPS = FFN_NJ + N_FFN_TILES


def _ffn_tile(step):
    return jnp.maximum(step - FFN_NJ, 0)


def _on_stream_part(tile, x_refs, o_refs, fn):
    if len(x_refs) == 1 and len(o_refs) == 1:
        fn(x_refs[0], o_refs[0])
    else:
        pl.when(tile < N_CTX_TILES)(lambda: fn(x_refs[0], o_refs[0]))
        pl.when(tile >= N_CTX_TILES)(lambda: fn(x_refs[-1], o_refs[-1]))


def _ffn_kernel(*refs, layer, n_in, n_out):
    x_refs = refs[:n_in]
    n_ref, sh_ref, sc_ref, g_ref, wg_ref, wu_ref, wd_ref = refs[n_in:n_in + 7]
    o_refs = refs[n_in + 7:n_in + 7 + n_out]
    h_ref, a_ref, wg_s, wu_s, wd_s = refs[n_in + 7 + n_out:]
    nj, tf = FFN_NJ, FFN_TF
    s = pl.program_id(0)
    tile = _ffn_tile(s)
    r = _mod_row(tile, FFN_TM, 0)

    def load_tile():
        def init(x_ref, _):
            h = _norm_mod(x_ref[...], n_ref[layer:layer + 1, :], sc_ref[pl.ds(r, 1), :], sh_ref[pl.ds(r, 1), :])
            h_ref[...] = h.astype(BF16)
        _on_stream_part(tile, x_refs, o_refs, init)

    def up_chunk(j, cols):
        h = h_ref[...]
        a_ref[:, cols] = (_silu(_dot(h, wg_s[j])) * _dot(h, wu_s[j])).astype(BF16)

    def down_and_store():
        y = (0.5 * g_ref[pl.ds(r, 1), :]) * _dot(a_ref[...], wd_s[...])

        def store(x_ref, o_ref):
            o_ref[...] = x_ref[...] + y
        _on_stream_part(tile, x_refs, o_refs, store)

    def keep_arrived_chunk():
        wg_s[s] = wg_ref[...].astype(BF16)
        wu_s[s] = wu_ref[...].astype(BF16)
        wd_s[pl.ds(pl.multiple_of(s * tf, tf), tf), :] = wd_ref[...].astype(BF16)

    def up_previous_chunk():
        up_chunk(s - 1, pl.ds(pl.multiple_of((s - 1) * tf, tf), tf))

    @pl.when(s == 0)
    def _():
        load_tile()
        keep_arrived_chunk()

    @pl.when((s > 0) & (s < nj))
    def _():
        up_previous_chunk()
        keep_arrived_chunk()

    @pl.when(s == nj)
    def _():
        up_previous_chunk()
        down_and_store()

    @pl.when(s > nj)
    def _():
        load_tile()
        for j in range(nj):
            up_chunk(j, slice(j * tf, (j + 1) * tf))
        down_and_store()


def _stream_specs(split, buffered_once):
    tm = FFN_TM
    kw = {"pipeline_mode": pl.Buffered(1)} if buffered_once else {}
    if not split:
        return [pl.BlockSpec((tm, D_MODEL), lambda s: (_ffn_tile(s), 0), **kw)]
    last_ctx = N_CTX_TILES - 1
    return [pl.BlockSpec((tm, D_MODEL), lambda s: (jnp.minimum(_ffn_tile(s), last_ctx), 0), **kw),
            pl.BlockSpec((tm, D_MODEL), lambda s: (jnp.maximum(_ffn_tile(s) - N_CTX_TILES, 0), 0), **kw)]


def _ffn(xs, mod, layer, chunk0, norm_g, wg, wu, wd, split_out=False):
    tm, tf, nj = FFN_TM, FFN_TF, FFN_NJ
    split_in = len(xs) == 2
    mod_spec = lambda c: pl.BlockSpec((None, MOD_ROWS, D_MODEL), lambda s: (layer, 0, c))
    w_col = lambda s: (layer, 0, jnp.minimum(s, nj - 1))
    w_row = lambda s: (layer, jnp.minimum(s, nj - 1), 0)
    if split_out:
        out_shape = [jax.ShapeDtypeStruct((N_CTX_ROWS, D_MODEL), F32),
                     jax.ShapeDtypeStruct((N_LAT_ROWS, D_MODEL), F32)]
    else:
        out_shape = [jax.ShapeDtypeStruct((N_ROWS, D_MODEL), F32)]
    out = pl.pallas_call(
        functools.partial(_ffn_kernel, layer=layer, n_in=len(xs), n_out=len(out_shape)),
        grid=(N_FFN_STEPS,),
        in_specs=_stream_specs(split_in, False) + [
            _full((DEPTH, D_MODEL)),
            mod_spec(chunk0), mod_spec(chunk0 + 1), mod_spec(chunk0 + 2),
            pl.BlockSpec((None, D_MODEL, tf), w_col),
            pl.BlockSpec((None, D_MODEL, tf), w_col),
            pl.BlockSpec((None, tf, D_MODEL), w_row),
        ],
        out_specs=_stream_specs(split_out, True),
        out_shape=out_shape,
        scratch_shapes=[pltpu.VMEM((tm, D_MODEL), BF16),
                        pltpu.VMEM((tm, D_FF), BF16),
                        pltpu.VMEM((nj, D_MODEL, tf), BF16),
                        pltpu.VMEM((nj, D_MODEL, tf), BF16),
                        pltpu.VMEM((D_FF, D_MODEL), BF16)],
        compiler_params=_cparams("arbitrary"),
        name="ffn",
    )(*xs, norm_g, mod, mod, mod, wg, wu, wd)
    return tuple(out)


def _once(shape, index_map):
    return pl.BlockSpec(shape, index_map, pipeline_mode=pl.Buffered(1))


def _mod_chunk(layer, c):
    return pl.BlockSpec((None, MOD_ROWS, D_MODEL), lambda *_: (layer, 0, c))


def _head_mean_square(x):
    n = x.shape[-1]
    r = lax.broadcasted_iota(jnp.int32, (n, n), 0) >> HEAD_SHIFT
    c = lax.broadcasted_iota(jnp.int32, (n, n), 1) >> HEAD_SHIFT
    ones_bd = jnp.where(r == c, 1.0, 0.0).astype(BF16)
    return _dot((x * x).astype(BF16), ones_bd) * (1.0 / HEAD_DIM)


def _head_norm(x, head_gain):
    gain_row = jnp.concatenate([head_gain] * (x.shape[-1] // HEAD_DIM), axis=-1)
    return x * lax.rsqrt(_head_mean_square(x) + EPS) * gain_row


def _head_cols(x, h):
    return x[:, h * HEAD_DIM:(h + 1) * HEAD_DIM].astype(BF16)


def _softmax_pv(scores, values, sink):
    m = jnp.max(scores[0], axis=-1, keepdims=True)
    for s in scores[1:]:
        m = jnp.maximum(m, jnp.max(s, axis=-1, keepdims=True))
    if sink is not None:
        m = jnp.maximum(m, sink)
    denom = None
    acc = None
    for s, v in zip(scores, values):
        p = jnp.exp(s - m)
        d = jnp.sum(p, axis=-1, keepdims=True)
        o = _dot(p.astype(BF16), v)
        denom = d if denom is None else denom + d
        acc = o if acc is None else acc + o
    if sink is not None:
        denom = denom + jnp.exp(sink - m)
    return acc / denom


def _rope(x, cos, sin_lo, sin_hi):
    cols = []
    for c in range(x.shape[-1] // LANES):
        xc = x[:, c * LANES:(c + 1) * LANES]
        cols.append(xc * cos + pltpu.roll(xc, 112, 1) * sin_lo + pltpu.roll(xc, 16, 1) * sin_hi)
    return cols[0] if len(cols) == 1 else jnp.concatenate(cols, axis=-1)


def _block_diag(blocks):
    n = len(blocks)
    w = blocks[0].shape[0]
    rows = []
    for k, blk in enumerate(blocks):
        parts = []
        if k > 0:
            parts.append(jnp.zeros((w, k * w), F32))
        parts.append(blk)
        if k < n - 1:
            parts.append(jnp.zeros((w, (n - 1 - k) * w), F32))
        rows.append(jnp.concatenate(parts, axis=-1))
    return jnp.concatenate(rows, axis=0)


def _rglru_gates(xc, wa, ba, wx, bx, lam):
    xb = xc.astype(BF16)
    r = _sigmoid(_dot(xb, wa) + ba)
    i = _sigmoid(_dot(xb, wx) + bx)
    softplus = jnp.maximum(-lam, 0.0) + jnp.log1p(jnp.exp(-jnp.abs(lam)))
    log_a = (-LRU_C) * r * softplus
    a = jnp.exp(log_a)
    b = jnp.sqrt(1.0 - a * a) * (i * xc)
    return a, b


def _block_prefix(a, b, reverse):
    t = a.shape[0]
    row = lax.broadcasted_iota(jnp.int32, a.shape, 0) & 7
    for d in (1, 2, 4):
        if reverse:
            a_s = pltpu.roll(a, t - d, 0)
            b_s = pltpu.roll(b, t - d, 0)
            ok = row < 8 - d
        else:
            a_s = pltpu.roll(a, d, 0)
            b_s = pltpu.roll(b, d, 0)
            ok = row >= d
        b = jnp.where(ok, a * b_s + b, b)
        a = jnp.where(ok, a * a_s, a)
    return a, b


def _conv4(x, w_ref, b_row):
    t = x.shape[0]
    row = lax.broadcasted_iota(jnp.int32, x.shape, 0)
    xm2 = jnp.where(row >= 2, pltpu.roll(x, 2, 0), 0.0)
    xm1 = jnp.where(row >= 1, pltpu.roll(x, 1, 0), 0.0)
    xp1 = jnp.where(row < t - 1, pltpu.roll(x, t - 1, 0), 0.0)
    return (xm2 * w_ref[0:1, :] + xm1 * w_ref[1:2, :] + x * w_ref[2:3, :] + xp1 * w_ref[3:4, :]) + b_row


def _rglru_prepare(cx, cy, conv_w_ref, conv_b, gate_w_ref, ba_ref, bx_ref, lam_ref,
                   af_ref, bf_ref, ab_ref, bb_ref, gel_ref):
    xc = _conv4(cx, conv_w_ref, conv_b)
    a, b = _rglru_gates(xc, gate_w_ref[0], ba_ref[0:1, :], gate_w_ref[1], bx_ref[0:1, :], lam_ref[0:1, :])
    a, b = _block_prefix(a, b, reverse=False)
    af_ref[...] = a
    bf_ref[...] = b
    a, b = _rglru_gates(xc, gate_w_ref[2], ba_ref[1:2, :], gate_w_ref[3], bx_ref[1:2, :], lam_ref[1:2, :])
    a, b = _block_prefix(a, b, reverse=True)
    ab_ref[...] = a
    bb_ref[...] = b
    gel_ref[...] = _gelu_tanh(cy)


SCAN_UNROLL = 8


def _rglru_finish(h0f, h0b, af_ref, bf_ref, ab_ref, bb_ref, hf_ref, hb_ref, gel_ref):
    nblk = af_ref.shape[0] // 8

    def body(k, carry):
        cf, cb = carry
        rf = pl.ds(pl.multiple_of(k * 8, 8), 8)
        hf = bf_ref[rf, :] + af_ref[rf, :] * cf
        hf_ref[rf, :] = hf
        rb = pl.ds(pl.multiple_of((nblk - 1 - k) * 8, 8), 8)
        hb = bb_ref[rb, :] + ab_ref[rb, :] * cb
        hb_ref[rb, :] = hb
        return hf[7:8, :], hb[0:1, :]

    cf, cb = lax.fori_loop(0, nblk, body, (h0f, h0b), unroll=SCAN_UNROLL)
    oc = (hf_ref[...] + hb_ref[...]) * gel_ref[...]
    return oc, cf, cb


def _store_gate_weights(gate_w_ref, wa_ref, wx_ref):
    for d in range(2):
        gate_w_ref[2 * d] = _block_diag([wa_ref[d, n] for n in range(N_HEADS)]).astype(BF16)
        gate_w_ref[2 * d + 1] = _block_diag([wx_ref[d, n] for n in range(N_HEADS)]).astype(BF16)


def _lane_head_masks(n):
    lane = lax.broadcasted_iota(jnp.int32, (1, n), 1) >> HEAD_SHIFT
    return [jnp.where(lane == h, 1.0, 0.0) for h in range(n // HEAD_DIM)]


def _log_decays(theta_ref, masks):
    theta = theta_ref[...]
    lanes = theta[:, 0:1] * masks[0]
    for h in range(1, N_HEADS):
        lanes = lanes + theta[:, h:h + 1] * masks[h]
    lg = jnp.log1p(-jnp.exp(lanes))
    return lg[0:1, :], lg[1:2, :]


RET_BLOCK = 256


def _retention(q, k8, vb, s0, lgf, lgb, masks, o_ref):
    t, w = q.shape
    c = RET_BLOCK
    nh = w // HEAD_DIM
    pos = lax.broadcasted_iota(jnp.int32, (c, w), 0).astype(F32)
    q_dec = (jnp.exp(lgf * (pos + 1.0)), jnp.exp(lgb * (float(c) - pos)))
    k_dec = (jnp.exp(lgf * (float(c - 1) - pos)), jnp.exp(lgb * pos))
    chunk_dec = (jnp.exp(lgf * float(c)), jnp.exp(lgb * float(c)))
    rel = (lax.broadcasted_iota(jnp.int32, (c, c), 0) - lax.broadcasted_iota(jnp.int32, (c, c), 1)).astype(F32)
    decs = []
    for h in range(nh):
        gf = lgf[:, h * HEAD_DIM:h * HEAD_DIM + 1]
        gb = lgb[:, h * HEAD_DIM:h * HEAD_DIM + 1]
        e = jnp.exp(jnp.where(rel >= 0, gf * rel, gb * (-rel)))
        decs.append(jnp.where(rel == 0, 2.0, e))
    dec = jnp.concatenate(decs, axis=0)
    r_head = lax.broadcasted_iota(jnp.int32, (w, w), 0) >> HEAD_SHIFT
    c_head = lax.broadcasted_iota(jnp.int32, (w, w), 1) >> HEAD_SHIFT
    same_head = jnp.where(r_head == c_head, 1.0, 0.0)
    states = [None, None] if s0 is None else list(s0)

    def carry(d, rows, o):
        if states[d] is not None:
            o = o + _dot((q[rows, :] * q_dec[d]).astype(BF16), states[d].astype(BF16))
        upd = _dot_tn((k8[rows, :] * k_dec[d]).astype(BF16), vb[rows, :]) * same_head
        states[d] = upd if states[d] is None else states[d] * chunk_dec[d] + upd
        return o

    for ci in range(t // c):
        rows = slice(ci * c, (ci + 1) * c)
        qc = q[rows, :]
        q_stack = jnp.concatenate([(qc * masks[h]).astype(BF16) for h in range(nh)], axis=0)
        inner = (_dot_nt(q_stack, k8[rows, :].astype(BF16)) * dec).astype(BF16)
        out = _dot(inner, vb[rows, :])
        o = out[0:c, :] * masks[0]
        for h in range(1, nh):
            o = o + out[h * c:(h + 1) * c, :] * masks[h]
        o_ref[rows, :] = carry(0, rows, o)
    for ci in reversed(range(t // c)):
        rows = slice(ci * c, (ci + 1) * c)
        if states[1] is not None:
            o_ref[rows, :] = carry(1, rows, o_ref[rows, :])
        else:
            carry(1, rows, None)
    return states[0], states[1]


def _ctx_mixer_kernel(*refs, layer, n_prev):
    prev_refs = refs[:n_prev]
    (x_ref, n2_ref, sh_ref, sc_ref, g2_ref, win_ref, wout_ref,
     aqn_ref, akn_ref, bqn_ref, bkn_ref, sink_ref,
     convw_ref, convb_ref, wa_ref, ba_ref, wx_ref, bx_ref, lam_ref, theta_ref, dn_ref,
     xn_ref, *state_refs) = refs[n_prev:n_prev + 28]
    (win_s, wout_s, u_ref, mixed_ref,
     gate_w_ref, af_ref, bf_ref, ab_ref, bb_ref, hf_ref, hb_ref, gel_ref, ret_ref) = refs[n_prev + 28:]
    t = SEQ
    lrow = slice(layer, layer + 1)
    for prev_ref, state_ref in zip(prev_refs, state_refs):
        for earlier in range(layer):
            state_ref[earlier] = prev_ref[earlier]
    ka_ref, va_ref, kb_ref, vb_ref, stc_ref, std_ref = (ref.at[layer] for ref in state_refs)

    @pl.when(pl.program_id(0) == 0)
    def _():
        for c0 in range(0, IN_WIDTH, 2 * GROUP_W):
            win_s[:, c0:c0 + 2 * GROUP_W] = win_ref[:, c0:c0 + 2 * GROUP_W].astype(BF16)
        wout_s[...] = wout_ref[...].astype(BF16)
        _store_gate_weights(gate_w_ref, wa_ref, wx_ref)

    x = x_ref[...]
    h = _norm_mod(x, n2_ref[lrow, :], sc_ref[0:1, :], sh_ref[0:1, :]).astype(BF16)
    c_cols = slice(COL_CX, COL_CX + 2 * GROUP_W)
    u_ref[:, c_cols] = _dot(h, win_s[:, c_cols])
    _rglru_prepare(u_ref[:, COL_CX:COL_CX + GROUP_W], u_ref[:, COL_CY:COL_CY + GROUP_W],
                   convw_ref, convb_ref[lrow, :], gate_w_ref, ba_ref, bx_ref, lam_ref,
                   af_ref, bf_ref, ab_ref, bb_ref, gel_ref)
    u_ref[:, 0:COL_CX] = _dot(h, win_s[:, 0:COL_CX])
    u_ref[:, COL_DQ:IN_WIDTH] = _dot(h, win_s[:, COL_DQ:IN_WIDTH])

    for (cq, ck, cv, qn_ref, kn_ref, k_out, v_out, col0, use_sink) in (
            (COL_AQ, COL_AK, COL_AV, aqn_ref, akn_ref, ka_ref, va_ref, 0, True),
            (COL_BQ, COL_BK, COL_BV, bqn_ref, bkn_ref, kb_ref, vb_ref, GROUP_W, False)):
        q = _head_norm(u_ref[:, cq:cq + GROUP_W], qn_ref[lrow, :])
        k = _head_norm(u_ref[:, ck:ck + KV_W], kn_ref[lrow, :])
        v = u_ref[:, cv:cv + KV_W]
        k_out[...] = k
        v_out[...] = v
        qs = q * (HEAD_DIM ** -0.5)
        heads = []
        for hd in range(N_HEADS):
            kv = hd // 2
            s = _dot_nt(_head_cols(qs, hd), _head_cols(k, kv))
            sink = jnp.full((t, 1), sink_ref[layer, hd], F32) if use_sink else None
            heads.append(_softmax_pv([s], [_head_cols(v, kv)], sink))
        mixed_ref[:, col0:col0 + GROUP_W] = jnp.concatenate(heads, axis=-1).astype(BF16)

    zero = jnp.zeros((1, GROUP_W), F32)
    oc, cf, cb = _rglru_finish(zero, zero, af_ref, bf_ref, ab_ref, bb_ref, hf_ref, hb_ref, gel_ref)
    mixed_ref[:, 2 * GROUP_W:3 * GROUP_W] = oc.astype(BF16)
    stc_ref[0:1, :] = cf
    stc_ref[1:2, :] = cb

    masks = _lane_head_masks(GROUP_W)
    lgf, lgb = _log_decays(theta_ref, masks)
    k8 = u_ref[:, COL_DK:COL_DK + GROUP_W] * (HEAD_DIM ** -0.5)
    vb = u_ref[:, COL_DV:COL_DV + GROUP_W].astype(BF16)
    final_states = _retention(u_ref[:, COL_DQ:COL_DQ + GROUP_W], k8, vb, None, lgf, lgb, masks, ret_ref)
    o = ret_ref[...]
    o = o * lax.rsqrt(_head_mean_square(o) + EPS) * dn_ref[lrow, :] * _silu(u_ref[:, COL_DG:COL_DG + GROUP_W])
    mixed_ref[:, 3 * GROUP_W:4 * GROUP_W] = o.astype(BF16)
    for d, s_full in enumerate(final_states):
        for hd in range(N_HEADS):
            std_ref[d, hd] = s_full[hd * HEAD_DIM:(hd + 1) * HEAD_DIM, hd * HEAD_DIM:(hd + 1) * HEAD_DIM]

    xn_ref[...] = x + g2_ref[0:1, :] * _dot(mixed_ref[...], wout_s[...])


def _ctx_mixers(x, mod, layer, prev, norm2_g, w_in, w_out,
                a_qn, a_kn, a_sink, b_qn, b_kn, c_conv_w, c_conv_b, c_wa, c_ba, c_wx, c_bx,
                c_lambda, d_theta, d_norm_g):
    per_request = lambda slots, shape: pl.BlockSpec((None, slots) + shape, lambda b: (b,) + (0,) * (1 + len(shape)))
    state_dims = [(SEQ, KV_W)] * 4 + [(2, GROUP_W), (2, N_HEADS, HEAD_DIM, HEAD_DIM)]
    scr = pltpu.VMEM((SEQ, GROUP_W), F32)
    out = pl.pallas_call(
        functools.partial(_ctx_mixer_kernel, layer=layer, n_prev=len(prev)),
        grid=(BATCH,),
        in_specs=[per_request(layer, dims) for dims in state_dims[:len(prev)]] + [
            pl.BlockSpec((SEQ, D_MODEL), lambda b: (b, 0)),
            _full((DEPTH, D_MODEL)),
            _mod_chunk(layer, 3), _mod_chunk(layer, 4), _mod_chunk(layer, 5),
            _once((None, D_MODEL, IN_WIDTH), lambda b: (layer, 0, 0)),
            _once((None, D_MODEL, D_MODEL), lambda b: (layer, 0, 0)),
            _full((DEPTH, HEAD_DIM)), _full((DEPTH, HEAD_DIM)), _full((DEPTH, HEAD_DIM)), _full((DEPTH, HEAD_DIM)),
            pl.BlockSpec(memory_space=pltpu.SMEM),
            _layer_block((4, GROUP_W), layer), _full((DEPTH, GROUP_W)),
            _layer_block((2, N_HEADS, HEAD_DIM, HEAD_DIM), layer), _layer_block((2, GROUP_W), layer),
            _layer_block((2, N_HEADS, HEAD_DIM, HEAD_DIM), layer), _layer_block((2, GROUP_W), layer),
            _layer_block((2, GROUP_W), layer),
            _layer_block((2, N_HEADS), layer), _full((DEPTH, GROUP_W)),
        ],
        out_specs=[pl.BlockSpec((SEQ, D_MODEL), lambda b: (b, 0))] + [
            per_request(layer + 1, dims) for dims in state_dims],
        out_shape=[jax.ShapeDtypeStruct((N_ROWS, D_MODEL), F32)] + [
            jax.ShapeDtypeStruct((BATCH, layer + 1) + dims, F32) for dims in state_dims],
        input_output_aliases={len(prev): 0},
        scratch_shapes=[pltpu.VMEM((D_MODEL, IN_WIDTH), BF16), pltpu.VMEM((D_MODEL, D_MODEL), BF16),
                        pltpu.VMEM((SEQ, IN_WIDTH), F32), pltpu.VMEM((SEQ, D_MODEL), BF16),
                        pltpu.VMEM((4, GROUP_W, GROUP_W), BF16)] + [scr] * 8,
        compiler_params=_cparams("arbitrary"),
        name="ctx_mixers",
    )(*prev, x, norm2_g, mod, mod, mod, w_in, w_out,
      a_qn, a_kn, b_qn, b_kn, a_sink, c_conv_w, c_conv_b, c_wa, c_ba, c_wx, c_bx,
      c_lambda, d_theta, d_norm_g)
    return out[0], tuple(out[1:])


LAT_BLOCK0 = N_CTX_ROWS // DEC_SEQ


def _lat_attn_kernel(x_ref, n2_ref, sh_ref, sc_ref, g2_ref, win_ref, wout_ref,
                     kca_ref, vca_ref, kcb_ref, vcb_ref,
                     aqn_ref, akn_ref, bqn_ref, bkn_ref, sink_ref, cos_ref, sinl_ref, sinh_ref,
                     xn_ref, h_ref, u_ref, o_ref, *, layer):
    t = DEC_SEQ
    lrow = slice(layer, layer + 1)
    mrow = pl.ds(1 + pl.program_id(0), 1)
    cos, sin_lo, sin_hi = cos_ref[...], sinl_ref[...], sinh_ref[...]
    scale = HEAD_DIM ** -0.5
    x = x_ref[...]
    h_ref[...] = _norm_mod(x, n2_ref[lrow, :], sc_ref[mrow, :], sh_ref[mrow, :]).astype(BF16)
    u_ref[...] = _dot(h_ref[...], win_ref[...].astype(BF16))

    q = _rope(_head_norm(u_ref[:, COL_AQ:COL_AQ + GROUP_W], aqn_ref[lrow, :]), cos, sin_lo, sin_hi)
    k = _rope(_head_norm(u_ref[:, COL_AK:COL_AK + KV_W], akn_ref[lrow, :]), cos, sin_lo, sin_hi)
    qh = [_head_cols(q * scale, h) for h in range(4)]
    v = u_ref[:, COL_AV:COL_AV + KV_W]
    kh = [_head_cols(k, kv) for kv in range(2)]
    vh = [_head_cols(v, kv) for kv in range(2)]
    kch = [_head_cols(kca_ref[...], kv) for kv in range(2)]
    vch = [_head_cols(vca_ref[...], kv) for kv in range(2)]
    w = ATT_BLOCK
    span = 3 * w
    for n in range(t // w):
        start = min(max((n - 1) * w, 0), t - span)
        rows = slice(n * w, (n + 1) * w)
        band = slice(start, start + span)
        qpos = (lax.broadcasted_iota(jnp.int32, (2 * w, span), 0) & (w - 1)) + n * w
        kpos = lax.broadcasted_iota(jnp.int32, (2 * w, span), 1) + start
        valid = jnp.abs(qpos - kpos) <= WINDOW
        heads = []
        for kv in range(2):
            qp = jnp.concatenate([qh[2 * kv][rows, :], qh[2 * kv + 1][rows, :]], axis=0)
            s_ctx = _dot_nt(qp, kch[kv])
            s_band = jnp.where(valid, _dot_nt(qp, kh[kv][band, :]), NEG_INF)
            row = lax.broadcasted_iota(jnp.int32, (2 * w, 1), 0)
            sink = jnp.where(row < w, sink_ref[layer, 2 * kv], sink_ref[layer, 2 * kv + 1])
            o = _softmax_pv([s_ctx, s_band], [vch[kv], vh[kv][band, :]], sink)
            heads += [o[0:w, :], o[w:2 * w, :]]
        o_ref[rows, 0:GROUP_W] = jnp.concatenate(heads, axis=-1).astype(BF16)

    q = _rope(_head_norm(u_ref[:, COL_BQ:COL_BQ + GROUP_W], bqn_ref[lrow, :]), cos, sin_lo, sin_hi)
    k = _rope(_head_norm(u_ref[:, COL_BK:COL_BK + KV_W], bkn_ref[lrow, :]), cos, sin_lo, sin_hi)
    qh = [_head_cols(q * scale, h) for h in range(4)]
    v = u_ref[:, COL_BV:COL_BV + KV_W]
    kh = [_head_cols(k, kv) for kv in range(2)]
    vh = [_head_cols(v, kv) for kv in range(2)]
    kch = [_head_cols(kcb_ref[...], kv) for kv in range(2)]
    vch = [_head_cols(vcb_ref[...], kv) for kv in range(2)]
    tq = 2 * ATT_BLOCK
    for n in range(t // tq):
        rows = slice(n * tq, (n + 1) * tq)
        heads = []
        for kv in range(2):
            qp = jnp.concatenate([qh[2 * kv][rows, :], qh[2 * kv + 1][rows, :]], axis=0)
            o = _softmax_pv([_dot_nt(qp, kch[kv]), _dot_nt(qp, kh[kv])], [vch[kv], vh[kv]], None)
            heads += [o[0:tq, :], o[tq:2 * tq, :]]
        o_ref[rows, GROUP_W:2 * GROUP_W] = jnp.concatenate(heads, axis=-1).astype(BF16)

    xn_ref[...] = x + g2_ref[mrow, :] * _dot(o_ref[...], wout_ref[...].astype(BF16))


def _lat_recurrent_kernel(xn_in_ref, h_ref, g2_ref, wc_ref, wqk_ref, wvg_ref, wout_ref, h0_ref,
                          convw_ref, convb_ref, wa_ref, ba_ref, wx_ref, bx_ref, lam_ref,
                          s0_ref, theta_ref, dn_ref,
                          xn_ref, gate_w_ref, af_ref, bf_ref, ab_ref, bb_ref, hf_ref, hb_ref, gel_ref, ret_ref,
                          *, layer):
    lrow = slice(layer, layer + 1)
    mrow = pl.ds(1 + pl.program_id(0), 1)

    @pl.when(pl.program_id(0) == 0)
    def _():
        _store_gate_weights(gate_w_ref, wa_ref, wx_ref)

    h = h_ref[...]
    u = _dot(h, wc_ref[...].astype(BF16))
    _rglru_prepare(u[:, 0:GROUP_W], u[:, GROUP_W:2 * GROUP_W], convw_ref, convb_ref[lrow, :],
                   gate_w_ref, ba_ref, bx_ref, lam_ref, af_ref, bf_ref, ab_ref, bb_ref, gel_ref)
    oc, _, _ = _rglru_finish(h0_ref[0:1, :], h0_ref[1:2, :],
                             af_ref, bf_ref, ab_ref, bb_ref, hf_ref, hb_ref, gel_ref)
    y = _dot(oc.astype(BF16), wout_ref[0:GROUP_W, :].astype(BF16))

    uqk = _dot(h, wqk_ref[...].astype(BF16))
    uvg = _dot(h, wvg_ref[...].astype(BF16))
    masks = _lane_head_masks(GROUP_W)
    lgf, lgb = _log_decays(theta_ref, masks)
    s0 = tuple(_block_diag([s0_ref[d, hd] for hd in range(N_HEADS)]) for d in range(2))
    _retention(uqk[:, 0:GROUP_W], uqk[:, GROUP_W:2 * GROUP_W] * (HEAD_DIM ** -0.5),
               uvg[:, 0:GROUP_W].astype(BF16), s0, lgf, lgb, masks, ret_ref)
    o = ret_ref[...]
    o = o * lax.rsqrt(_head_mean_square(o) + EPS) * dn_ref[lrow, :] * _silu(uvg[:, GROUP_W:2 * GROUP_W])
    y = y + _dot(o.astype(BF16), wout_ref[GROUP_W:2 * GROUP_W, :].astype(BF16))
    xn_ref[...] = xn_in_ref[...] + g2_ref[mrow, :] * y


def _lat_mixers(x, mod, layer, caches, state_c, state_d, rope, norm2_g, w_in, w_out,
                a_qn, a_kn, a_sink, b_qn, b_kn, c_conv_w, c_conv_b, c_wa, c_ba, c_wx, c_bx,
                c_lambda, d_theta, d_norm_g):
    rows = pl.BlockSpec((DEC_SEQ, D_MODEL), lambda b: (LAT_BLOCK0 + b, 0))
    h_rows = pl.BlockSpec((DEC_SEQ, D_MODEL), lambda b: (b, 0))
    cache_spec = pl.BlockSpec((None, None, PAST_LEN, KV_W), lambda b: (b, layer, 0, 0))
    gain = _full((DEPTH, HEAD_DIM))
    table = _once((DEC_SEQ, LANES), lambda b: (0, 0))
    out_shape = jax.ShapeDtypeStruct((N_ROWS, D_MODEL), F32)
    win_cols = lambda w, c: _once((None, D_MODEL, w), lambda b: (layer, 0, c))
    wout_rows = lambda h, r: _once((None, h, D_MODEL), lambda b: (layer, r, 0))

    xn, h = pl.pallas_call(
        functools.partial(_lat_attn_kernel, layer=layer),
        grid=(DEC_BATCH,),
        in_specs=[rows, _full((DEPTH, D_MODEL)),
                  _mod_chunk(layer, 3), _mod_chunk(layer, 4), _mod_chunk(layer, 5),
                  win_cols(4 * GROUP_W, 0), wout_rows(2 * GROUP_W, 0),
                  cache_spec, cache_spec, cache_spec, cache_spec,
                  gain, gain, gain, gain,
                  pl.BlockSpec(memory_space=pltpu.SMEM),
                  table, table, table],
        out_specs=[rows, pl.BlockSpec((DEC_SEQ, D_MODEL), lambda b: (b, 0), pipeline_mode=pl.Buffered(1))],
        out_shape=[out_shape, jax.ShapeDtypeStruct((N_LAT_ROWS, D_MODEL), BF16)],
        input_output_aliases={0: 0},
        scratch_shapes=[pltpu.VMEM((DEC_SEQ, 4 * GROUP_W), F32), pltpu.VMEM((DEC_SEQ, 2 * GROUP_W), BF16)],
        compiler_params=_cparams("arbitrary"),
        name="lat_attention",
    )(x, norm2_g, mod, mod, mod, w_in, w_out, *caches, a_qn, a_kn, b_qn, b_kn, a_sink, *rope)

    scr = pltpu.VMEM((DEC_SEQ, GROUP_W), F32)
    xn = pl.pallas_call(
        functools.partial(_lat_recurrent_kernel, layer=layer),
        grid=(DEC_BATCH,),
        in_specs=[
            rows, h_rows, _mod_chunk(layer, 5),
            win_cols(2 * GROUP_W, COL_CX // (2 * GROUP_W)),
            win_cols(2 * GROUP_W, COL_DQ // (2 * GROUP_W)), win_cols(2 * GROUP_W, COL_DV // (2 * GROUP_W)),
            wout_rows(2 * GROUP_W, 1),
            pl.BlockSpec((None, None, 2, GROUP_W), lambda b: (b, layer, 0, 0)),
            _layer_block((4, GROUP_W), layer), _full((DEPTH, GROUP_W)),
            _layer_block((2, N_HEADS, HEAD_DIM, HEAD_DIM), layer), _layer_block((2, GROUP_W), layer),
            _layer_block((2, N_HEADS, HEAD_DIM, HEAD_DIM), layer), _layer_block((2, GROUP_W), layer),
            _layer_block((2, GROUP_W), layer),
            pl.BlockSpec((None, None, 2, N_HEADS, HEAD_DIM, HEAD_DIM), lambda b: (b, layer, 0, 0, 0, 0)),
            _layer_block((2, N_HEADS), layer), _full((DEPTH, GROUP_W))],
        out_specs=rows,
        out_shape=out_shape,
        input_output_aliases={0: 0},
        scratch_shapes=[pltpu.VMEM((4, GROUP_W, GROUP_W), BF16)] + [scr] * 8,
        compiler_params=_cparams("arbitrary"),
        name="lat_recurrent",
    )(xn, h, mod, w_in, w_in, w_in, w_out, state_c, c_conv_w, c_conv_b, c_wa, c_ba, c_wx, c_bx, c_lambda,
      state_d, d_theta, d_norm_g)
    return xn


def _rope_tables():
    t = np.arange(DEC_SEQ)
    row = (t // GRID_W).astype(np.float64)[:, None]
    col = (t % GRID_W).astype(np.float64)[:, None]
    half = HEAD_DIM // 2
    inv = 1.0 / (ROPE_BASE ** (np.arange(0, half, 2, dtype=np.float64) / half))
    j = np.arange(LANES) % HEAD_DIM
    ang = np.where((j < half)[None, :], row, col) * inv[j % (half // 2)][None, :]
    first = ((j % half) < half // 2)[None, :]
    cos, sin = np.cos(ang), np.sin(ang)
    return tuple(jnp.asarray(a, F32) for a in (cos, np.where(first, -sin, 0.0), np.where(first, 0.0, sin)))


def kernel(x_prompt, x_sample, cache_a_k, cache_a_v, cache_b_k, cache_b_v, state_c, state_d, c, c_ctx, norm1_g, norm2_g, norm3_g, w_mod, b_mod, ffn1_wg, ffn1_wu, ffn1_wd, ffn2_wg, ffn2_wu, ffn2_wd, w_in, w_out, a_qn, a_kn, a_sink, b_qn, b_kn, c_conv_w, c_conv_b, c_wa, c_ba, c_wx, c_bx, c_lambda, d_theta, d_norm_g):
    mod = _modulation(c_ctx, c, w_mod, b_mod)
    rope = _rope_tables()
    caches = tuple(t.reshape(DEC_BATCH, DEPTH, PAST_LEN, KV_W) for t in (cache_a_k, cache_a_v, cache_b_k, cache_b_v))
    mixer_params = (a_qn, a_kn, a_sink, b_qn, b_kn, c_conv_w, c_conv_b, c_wa, c_ba, c_wx, c_bx,
                    c_lambda, d_theta, d_norm_g)
    xs = (x_prompt.reshape(N_CTX_ROWS, D_MODEL), x_sample.reshape(N_LAT_ROWS, D_MODEL))
    states = ()
    for l in range(DEPTH):
        (x,) = _ffn(xs, mod, l, 0, norm1_g, ffn1_wg, ffn1_wu, ffn1_wd)
        x, states = _ctx_mixers(x, mod, l, states, norm2_g, w_in, w_out, *mixer_params)
        x = _lat_mixers(x, mod, l, caches, state_c, state_d, rope, norm2_g, w_in, w_out, *mixer_params)
        xs = _ffn((x,), mod, l, 6, norm3_g, ffn2_wg, ffn2_wu, ffn2_wd, split_out=(l == DEPTH - 1))
    y_p, y_s = xs
    ka, va, kb, vb, st_c, st_d = states
    kv_shape = (BATCH, DEPTH, SEQ, 2, HEAD_DIM)
    return (y_p.reshape(BATCH, SEQ, D_MODEL), y_s.reshape(DEC_BATCH, DEC_SEQ, D_MODEL),
            ka.reshape(kv_shape), va.reshape(kv_shape), kb.reshape(kv_shape), vb.reshape(kv_shape),
            st_c, st_d)
```

```python
import functools
import math

import numpy as np
import jax
import jax.numpy as jnp
from jax import lax
from jax.experimental import pallas as pl
from jax.experimental.pallas import tpu as pltpu

F32 = jnp.float32
BF16 = jnp.bfloat16

D_MODEL = 1024
BATCH = 16
SEQ = 256
DEPTH = 2
DEC_BATCH = 2
DEC_SEQ = 1024
PAST_LEN = 512
GRID_W = 64
HEAD_DIM = 64
HEAD_SHIFT = 6
N_HEADS = 4
GROUP_W = 256
KV_W = 2 * HEAD_DIM
LANES = 128
WINDOW = 128
ATT_BLOCK = 128
ROPE_BASE = 10000.0
LRU_C = 8.0
D_FF = 2816
N_MOD = 9
EPS = 1e-6
NEG_INF = -1e30
IN_WIDTH = 2560

N_CTX_ROWS = BATCH * SEQ
N_LAT_ROWS = DEC_BATCH * DEC_SEQ
N_ROWS = N_CTX_ROWS + N_LAT_ROWS
MOD_ROWS = 8
MOD_GROUP = 1024

VMEM_LIMIT_BYTES = 56 * 1024 * 1024

COL_AQ, COL_AK, COL_AV = 0, 256, 384
COL_BQ, COL_BK, COL_BV = 512, 768, 896
COL_CX, COL_CY = 1024, 1280
COL_DQ, COL_DK, COL_DV, COL_DG = 1536, 1792, 2048, 2304


def _cparams(*sem):
    return pltpu.CompilerParams(dimension_semantics=sem, vmem_limit_bytes=VMEM_LIMIT_BYTES)


def _dot(a, b):
    return jnp.dot(a, b, preferred_element_type=F32)


def _dot_nt(a, b):
    return lax.dot_general(a, b, (((1,), (1,)), ((), ())), preferred_element_type=F32)


def _dot_tn(a, b):
    return lax.dot_general(a, b, (((0,), (0,)), ((), ())), preferred_element_type=F32)


def _sigmoid(x):
    return 0.5 * jnp.tanh(0.5 * x) + 0.5


def _silu(x):
    return x * _sigmoid(x)


def _gelu_tanh(x):
    return 0.5 * x * (1.0 + jnp.tanh(math.sqrt(2.0 / math.pi) * (x + 0.044715 * (x * x * x))))


def _mod_row(i, tm, s):
    if tm >= MOD_GROUP:
        block_index = i * (tm // MOD_GROUP) + s
    else:
        block_index = i >> int(math.log2(MOD_GROUP // tm))
    return jnp.maximum(block_index - (N_CTX_ROWS // MOD_GROUP - 1), 0)


def _norm_mod(x, g, sc, sh):
    ms = jnp.mean(x * x, axis=-1, keepdims=True)
    return (x * lax.rsqrt(ms + EPS) * g) * (1.0 + sc) + sh


def _full(shape):
    return pl.BlockSpec(shape, lambda *_: (0,) * len(shape))


def _layer_block(shape, layer):
    return pl.BlockSpec((None,) + shape, lambda *_: (layer,) + (0,) * len(shape))


MOD_TN = 3072


def _mod_kernel(cc_ref, c_ref, w_ref, b_ref, o_ref):
    l = pl.program_id(0)
    pad = jnp.zeros((MOD_ROWS - 1 - DEC_BATCH, D_MODEL), F32)
    cond = jnp.concatenate([cc_ref[...], c_ref[...], pad], axis=0)
    o_ref[...] = _dot(_silu(cond).astype(BF16), w_ref[...].astype(BF16)) + b_ref[pl.ds(l, 1), :]


def _modulation(c_ctx, c, w_mod, b_mod):
    n = N_MOD * D_MODEL
    return pl.pallas_call(
        _mod_kernel,
        grid=(DEPTH, n // MOD_TN),
        in_specs=[
            pl.BlockSpec((1, D_MODEL), lambda l, j: (0, 0)),
            pl.BlockSpec((DEC_BATCH, D_MODEL), lambda l, j: (0, 0)),
            pl.BlockSpec((None, D_MODEL, MOD_TN), lambda l, j: (l, 0, j)),
            pl.BlockSpec((DEPTH, MOD_TN), lambda l, j: (0, j)),
        ],
        out_specs=pl.BlockSpec((None, MOD_ROWS, MOD_TN), lambda l, j: (l, 0, j)),
        out_shape=jax.ShapeDtypeStruct((DEPTH, MOD_ROWS, n), F32),
        compiler_params=_cparams("arbitrary", "arbitrary"),
        name="modulation",
    )(c_ctx.reshape(1, D_MODEL), c, w_mod, b_mod)


FFN_TM = 1024
FFN_TF = 256
N_CTX_TILES = N_CTX_ROWS // FFN_TM


FFN_NJ = D_FF // FFN_TF
N_FFN_TILES = N_ROWS // FFN_TM
N_FFN_STEPS = FFN_NJ + N_FFN_TILES


def _ffn_tile(step):
    return jnp.maximum(step - FFN_NJ, 0)


def _on_stream_part(tile, x_refs, o_refs, fn):
    if len(x_refs) == 1 and len(o_refs) == 1:
        fn(x_refs[0], o_refs[0])
    else:
        pl.when(tile < N_CTX_TILES)(lambda: fn(x_refs[0], o_refs[0]))
        pl.when(tile >= N_CTX_TILES)(lambda: fn(x_refs[-1], o_refs[-1]))


def _ffn_kernel(*refs, layer, n_in, n_out):
    x_refs = refs[:n_in]
    n_ref, sh_ref, sc_ref, g_ref, wg_ref, wu_ref, wd_ref = refs[n_in:n_in + 7]
    o_refs = refs[n_in + 7:n_in + 7 + n_out]
    h_ref, a_ref, wg_s, wu_s, wd_s = refs[n_in + 7 + n_out:]
    nj, tf = FFN_NJ, FFN_TF
    s = pl.program_id(0)
    tile = _ffn_tile(s)
    r = _mod_row(tile, FFN_TM, 0)

    def load_tile():
        def init(x_ref, _):
            h = _norm_mod(x_ref[...], n_ref[layer:layer + 1, :], sc_ref[pl.ds(r, 1), :], sh_ref[pl.ds(r, 1), :])
            h_ref[...] = h.astype(BF16)
        _on_stream_part(tile, x_refs, o_refs, init)

    def up_chunk(j, cols):
        h = h_ref[...]
        a_ref[:, cols] = (_silu(_dot(h, wg_s[j])) * _dot(h, wu_s[j])).astype(BF16)

    def down_and_store():
        y = (0.5 * g_ref[pl.ds(r, 1), :]) * _dot(a_ref[...], wd_s[...])

        def store(x_ref, o_ref):
            o_ref[...] = x_ref[...] + y
        _on_stream_part(tile, x_refs, o_refs, store)

    def keep_arrived_chunk():
        wg_s[s] = wg_ref[...].astype(BF16)
        wu_s[s] = wu_ref[...].astype(BF16)
        wd_s[pl.ds(pl.multiple_of(s * tf, tf), tf), :] = wd_ref[...].astype(BF16)

    def up_previous_chunk():
        up_chunk(s - 1, pl.ds(pl.multiple_of((s - 1) * tf, tf), tf))

    @pl.when(s == 0)
    def _():
        load_tile()
        keep_arrived_chunk()

    @pl.when((s > 0) & (s < nj))
    def _():
        up_previous_chunk()
        keep_arrived_chunk()

    @pl.when(s == nj)
    def _():
        up_previous_chunk()
        down_and_store()

    @pl.when(s > nj)
    def _():
        load_tile()
        for j in range(nj):
            up_chunk(j, slice(j * tf, (j + 1) * tf))
        down_and_store()


def _stream_specs(split, buffered_once):
    tm = FFN_TM
    kw = {"pipeline_mode": pl.Buffered(1)} if buffered_once else {}
    if not split:
        return [pl.BlockSpec((tm, D_MODEL), lambda s: (_ffn_tile(s), 0), **kw)]
    last_ctx = N_CTX_TILES - 1
    return [pl.BlockSpec((tm, D_MODEL), lambda s: (jnp.minimum(_ffn_tile(s), last_ctx), 0), **kw),
            pl.BlockSpec((tm, D_MODEL), lambda s: (jnp.maximum(_ffn_tile(s) - N_CTX_TILES, 0), 0), **kw)]


def _ffn(xs, mod, layer, chunk0, norm_g, wg, wu, wd, split_out=False):
    tm, tf, nj = FFN_TM, FFN_TF, FFN_NJ
    split_in = len(xs) == 2
    mod_spec = lambda c: pl.BlockSpec((None, MOD_ROWS, D_MODEL), lambda s: (layer, 0, c))
    w_col = lambda s: (layer, 0, jnp.minimum(s, nj - 1))
    w_row = lambda s: (layer, jnp.minimum(s, nj - 1), 0)
    if split_out:
        out_shape = [jax.ShapeDtypeStruct((N_CTX_ROWS, D_MODEL), F32),
                     jax.ShapeDtypeStruct((N_LAT_ROWS, D_MODEL), F32)]
    else:
        out_shape = [jax.ShapeDtypeStruct((N_ROWS, D_MODEL), F32)]
    out = pl.pallas_call(
        functools.partial(_ffn_kernel, layer=layer, n_in=len(xs), n_out=len(out_shape)),
        grid=(N_FFN_STEPS,),
        in_specs=_stream_specs(split_in, False) + [
            _full((DEPTH, D_MODEL)),
            mod_spec(chunk0), mod_spec(chunk0 + 1), mod_spec(chunk0 + 2),
            pl.BlockSpec((None, D_MODEL, tf), w_col),
            pl.BlockSpec((None, D_MODEL, tf), w_col),
            pl.BlockSpec((None, tf, D_MODEL), w_row),
        ],
        out_specs=_stream_specs(split_out, True),
        out_shape=out_shape,
        scratch_shapes=[pltpu.VMEM((tm, D_MODEL), BF16),
                        pltpu.VMEM((tm, D_FF), BF16),
                        pltpu.VMEM((nj, D_MODEL, tf), BF16),
                        pltpu.VMEM((nj, D_MODEL, tf), BF16),
                        pltpu.VMEM((D_FF, D_MODEL), BF16)],
        compiler_params=_cparams("arbitrary"),
        name="ffn",
    )(*xs, norm_g, mod, mod, mod, wg, wu, wd)
    return tuple(out)


def _once(shape, index_map):
    return pl.BlockSpec(shape, index_map, pipeline_mode=pl.Buffered(1))


def _mod_chunk(layer, c):
    return pl.BlockSpec((None, MOD_ROWS, D_MODEL), lambda *_: (layer, 0, c))


def _head_mean_square(x):
    n = x.shape[-1]
    r = lax.broadcasted_iota(jnp.int32, (n, n), 0) >> HEAD_SHIFT
    c = lax.broadcasted_iota(jnp.int32, (n, n), 1) >> HEAD_SHIFT
    ones_bd = jnp.where(r == c, 1.0, 0.0).astype(BF16)
    return _dot((x * x).astype(BF16), ones_bd) * (1.0 / HEAD_DIM)


def _head_norm(x, head_gain):
    gain_row = jnp.concatenate([head_gain] * (x.shape[-1] // HEAD_DIM), axis=-1)
    return x * lax.rsqrt(_head_mean_square(x) + EPS) * gain_row


def _head_cols(x, h):
    return x[:, h * HEAD_DIM:(h + 1) * HEAD_DIM].astype(BF16)


def _softmax_pv(scores, values, sink):
    m = jnp.max(scores[0], axis=-1, keepdims=True)
    for s in scores[1:]:
        m = jnp.maximum(m, jnp.max(s, axis=-1, keepdims=True))
    if sink is not None:
        m = jnp.maximum(m, sink)
    denom = None
    acc = None
    for s, v in zip(scores, values):
        p = jnp.exp(s - m)
        d = jnp.sum(p, axis=-1, keepdims=True)
        o = _dot(p.astype(BF16), v)
        denom = d if denom is None else denom + d
        acc = o if acc is None else acc + o
    if sink is not None:
        denom = denom + jnp.exp(sink - m)
    return acc / denom


def _rope(x, cos, sin_lo, sin_hi):
    cols = []
    for c in range(x.shape[-1] // LANES):
        xc = x[:, c * LANES:(c + 1) * LANES]
        cols.append(xc * cos + pltpu.roll(xc, 112, 1) * sin_lo + pltpu.roll(xc, 16, 1) * sin_hi)
    return cols[0] if len(cols) == 1 else jnp.concatenate(cols, axis=-1)


def _block_diag(blocks):
    n = len(blocks)
    w = blocks[0].shape[0]
    rows = []
    for k, blk in enumerate(blocks):
        parts = []
        if k > 0:
            parts.append(jnp.zeros((w, k * w), F32))
        parts.append(blk)
        if k < n - 1:
            parts.append(jnp.zeros((w, (n - 1 - k) * w), F32))
        rows.append(jnp.concatenate(parts, axis=-1))
    return jnp.concatenate(rows, axis=0)


def _rglru_gates(xc, wa, ba, wx, bx, lam):
    xb = xc.astype(BF16)
    r = _sigmoid(_dot(xb, wa) + ba)
    i = _sigmoid(_dot(xb, wx) + bx)
    softplus = jnp.maximum(-lam, 0.0) + jnp.log1p(jnp.exp(-jnp.abs(lam)))
    log_a = (-LRU_C) * r * softplus
    a = jnp.exp(log_a)
    b = jnp.sqrt(1.0 - a * a) * (i * xc)
    return a, b


def _block_prefix(a, b, reverse):
    t = a.shape[0]
    row = lax.broadcasted_iota(jnp.int32, a.shape, 0) & 7
    for d in (1, 2, 4):
        if reverse:
            a_s = pltpu.roll(a, t - d, 0)
            b_s = pltpu.roll(b, t - d, 0)
            ok = row < 8 - d
        else:
            a_s = pltpu.roll(a, d, 0)
            b_s = pltpu.roll(b, d, 0)
            ok = row >= d
        b = jnp.where(ok, a * b_s + b, b)
        a = jnp.where(ok, a * a_s, a)
    return a, b


def _conv4(x, w_ref, b_row):
    t = x.shape[0]
    row = lax.broadcasted_iota(jnp.int32, x.shape, 0)
    xm2 = jnp.where(row >= 2, pltpu.roll(x, 2, 0), 0.0)
    xm1 = jnp.where(row >= 1, pltpu.roll(x, 1, 0), 0.0)
    xp1 = jnp.where(row < t - 1, pltpu.roll(x, t - 1, 0), 0.0)
    return (xm2 * w_ref[0:1, :] + xm1 * w_ref[1:2, :] + x * w_ref[2:3, :] + xp1 * w_ref[3:4, :]) + b_row


def _rglru_prepare(cx, cy, conv_w_ref, conv_b, gate_w_ref, ba_ref, bx_ref, lam_ref,
                   af_ref, bf_ref, ab_ref, bb_ref, gel_ref):
    xc = _conv4(cx, conv_w_ref, conv_b)
    a, b = _rglru_gates(xc, gate_w_ref[0], ba_ref[0:1, :], gate_w_ref[1], bx_ref[0:1, :], lam_ref[0:1, :])
    a, b = _block_prefix(a, b, reverse=False)
    af_ref[...] = a
    bf_ref[...] = b
    a, b = _rglru_gates(xc, gate_w_ref[2], ba_ref[1:2, :], gate_w_ref[3], bx_ref[1:2, :], lam_ref[1:2, :])
    a, b = _block_prefix(a, b, reverse=True)
    ab_ref[...] = a
    bb_ref[...] = b
    gel_ref[...] = _gelu_tanh(cy)


SCAN_UNROLL = 8


def _rglru_finish(h0f, h0b, af_ref, bf_ref, ab_ref, bb_ref, hf_ref, hb_ref, gel_ref):
    nblk = af_ref.shape[0] // 8

    def body(k, carry):
        cf, cb = carry
        rf = pl.ds(pl.multiple_of(k * 8, 8), 8)
        hf = bf_ref[rf, :] + af_ref[rf, :] * cf
        hf_ref[rf, :] = hf
        rb = pl.ds(pl.multiple_of((nblk - 1 - k) * 8, 8), 8)
        hb = bb_ref[rb, :] + ab_ref[rb, :] * cb
        hb_ref[rb, :] = hb
        return hf[7:8, :], hb[0:1, :]

    cf, cb = lax.fori_loop(0, nblk, body, (h0f, h0b), unroll=SCAN_UNROLL)
    oc = (hf_ref[...] + hb_ref[...]) * gel_ref[...]
    return oc, cf, cb


def _store_gate_weights(gate_w_ref, wa_ref, wx_ref):
    for d in range(2):
        gate_w_ref[2 * d] = _block_diag([wa_ref[d, n] for n in range(N_HEADS)]).astype(BF16)
        gate_w_ref[2 * d + 1] = _block_diag([wx_ref[d, n] for n in range(N_HEADS)]).astype(BF16)


def _lane_head_masks(n):
    lane = lax.broadcasted_iota(jnp.int32, (1, n), 1) >> HEAD_SHIFT
    return [jnp.where(lane == h, 1.0, 0.0) for h in range(n // HEAD_DIM)]


def _log_decays(theta_ref, masks):
    theta = theta_ref[...]
    lanes = theta[:, 0:1] * masks[0]
    for h in range(1, N_HEADS):
        lanes = lanes + theta[:, h:h + 1] * masks[h]
    lg = jnp.log1p(-jnp.exp(lanes))
    return lg[0:1, :], lg[1:2, :]


RET_BLOCK = 256


def _retention(q, k8, vb, s0, lgf, lgb, masks, o_ref):
    t, w = q.shape
    c = RET_BLOCK
    nh = w // HEAD_DIM
    pos = lax.broadcasted_iota(jnp.int32, (c, w), 0).astype(F32)
    q_dec = (jnp.exp(lgf * (pos + 1.0)), jnp.exp(lgb * (float(c) - pos)))
    k_dec = (jnp.exp(lgf * (float(c - 1) - pos)), jnp.exp(lgb * pos))
    chunk_dec = (jnp.exp(lgf * float(c)), jnp.exp(lgb * float(c)))
    rel = (lax.broadcasted_iota(jnp.int32, (c, c), 0) - lax.broadcasted_iota(jnp.int32, (c, c), 1)).astype(F32)
    decs = []
    for h in range(nh):
        gf = lgf[:, h * HEAD_DIM:h * HEAD_DIM + 1]
        gb = lgb[:, h * HEAD_DIM:h * HEAD_DIM + 1]
        e = jnp.exp(jnp.where(rel >= 0, gf * rel, gb * (-rel)))
        decs.append(jnp.where(rel == 0, 2.0, e))
    dec = jnp.concatenate(decs, axis=0)
    r_head = lax.broadcasted_iota(jnp.int32, (w, w), 0) >> HEAD_SHIFT
    c_head = lax.broadcasted_iota(jnp.int32, (w, w), 1) >> HEAD_SHIFT
    same_head = jnp.where(r_head == c_head, 1.0, 0.0)
    states = [None, None] if s0 is None else list(s0)

    def carry(d, rows, o):
        if states[d] is not None:
            o = o + _dot((q[rows, :] * q_dec[d]).astype(BF16), states[d].astype(BF16))
        upd = _dot_tn((k8[rows, :] * k_dec[d]).astype(BF16), vb[rows, :]) * same_head
        states[d] = upd if states[d] is None else states[d] * chunk_dec[d] + upd
        return o

    for ci in range(t // c):
        rows = slice(ci * c, (ci + 1) * c)
        qc = q[rows, :]
        q_stack = jnp.concatenate([(qc * masks[h]).astype(BF16) for h in range(nh)], axis=0)
        inner = (_dot_nt(q_stack, k8[rows, :].astype(BF16)) * dec).astype(BF16)
        out = _dot(inner, vb[rows, :])
        o = out[0:c, :] * masks[0]
        for h in range(1, nh):
            o = o + out[h * c:(h + 1) * c, :] * masks[h]
        o_ref[rows, :] = carry(0, rows, o)
    for ci in reversed(range(t // c)):
        rows = slice(ci * c, (ci + 1) * c)
        if states[1] is not None:
            o_ref[rows, :] = carry(1, rows, o_ref[rows, :])
        else:
            carry(1, rows, None)
    return states[0], states[1]


def _ctx_mixer_kernel(*refs, layer, n_prev):
    prev_refs = refs[:n_prev]
    (x_ref, n2_ref, sh_ref, sc_ref, g2_ref, win_ref, wout_ref,
     aqn_ref, akn_ref, bqn_ref, bkn_ref, sink_ref,
     convw_ref, convb_ref, wa_ref, ba_ref, wx_ref, bx_ref, lam_ref, theta_ref, dn_ref,
     xn_ref, *state_refs) = refs[n_prev:n_prev + 28]
    (win_s, wout_s, u_ref, mixed_ref,
     gate_w_ref, af_ref, bf_ref, ab_ref, bb_ref, hf_ref, hb_ref, gel_ref, ret_ref) = refs[n_prev + 28:]
    t = SEQ
    lrow = slice(layer, layer + 1)
    for prev_ref, state_ref in zip(prev_refs, state_refs):
        for earlier in range(layer):
            state_ref[earlier] = prev_ref[earlier]
    ka_ref, va_ref, kb_ref, vb_ref, stc_ref, std_ref = (ref.at[layer] for ref in state_refs)

    @pl.when(pl.program_id(0) == 0)
    def _():
        for c0 in range(0, IN_WIDTH, 2 * GROUP_W):
            win_s[:, c0:c0 + 2 * GROUP_W] = win_ref[:, c0:c0 + 2 * GROUP_W].astype(BF16)
        wout_s[...] = wout_ref[...].astype(BF16)
        _store_gate_weights(gate_w_ref, wa_ref, wx_ref)

    x = x_ref[...]
    h = _norm_mod(x, n2_ref[lrow, :], sc_ref[0:1, :], sh_ref[0:1, :]).astype(BF16)
    c_cols = slice(COL_CX, COL_CX + 2 * GROUP_W)
    u_ref[:, c_cols] = _dot(h, win_s[:, c_cols])
    _rglru_prepare(u_ref[:, COL_CX:COL_CX + GROUP_W], u_ref[:, COL_CY:COL_CY + GROUP_W],
                   convw_ref, convb_ref[lrow, :], gate_w_ref, ba_ref, bx_ref, lam_ref,
                   af_ref, bf_ref, ab_ref, bb_ref, gel_ref)
    u_ref[:, 0:COL_CX] = _dot(h, win_s[:, 0:COL_CX])
    u_ref[:, COL_DQ:IN_WIDTH] = _dot(h, win_s[:, COL_DQ:IN_WIDTH])

    for (cq, ck, cv, qn_ref, kn_ref, k_out, v_out, col0, use_sink) in (
            (COL_AQ, COL_AK, COL_AV, aqn_ref, akn_ref, ka_ref, va_ref, 0, True),
            (COL_BQ, COL_BK, COL_BV, bqn_ref, bkn_ref, kb_ref, vb_ref, GROUP_W, False)):
        q = _head_norm(u_ref[:, cq:cq + GROUP_W], qn_ref[lrow, :])
        k = _head_norm(u_ref[:, ck:ck + KV_W], kn_ref[lrow, :])
        v = u_ref[:, cv:cv + KV_W]
        k_out[...] = k
        v_out[...] = v
        qs = q * (HEAD_DIM ** -0.5)
        heads = []
        for hd in range(N_HEADS):
            kv = hd // 2
            s = _dot_nt(_head_cols(qs, hd), _head_cols(k, kv))
            sink = jnp.full((t, 1), sink_ref[layer, hd], F32) if use_sink else None
            heads.append(_softmax_pv([s], [_head_cols(v, kv)], sink))
        mixed_ref[:, col0:col0 + GROUP_W] = jnp.concatenate(heads, axis=-1).astype(BF16)

    masks = _lane_head_masks(GROUP_W)
    lgf, lgb = _log_decays(theta_ref, masks)
    k8 = u_ref[:, COL_DK:COL_DK + GROUP_W] * (HEAD_DIM ** -0.5)
    vb = u_ref[:, COL_DV:COL_DV + GROUP_W].astype(BF16)
    final_states = _retention(u_ref[:, COL_DQ:COL_DQ + GROUP_W], k8, vb, None, lgf, lgb, masks, ret_ref)
    o = ret_ref[...]
    o = o * lax.rsqrt(_head_mean_square(o) + EPS) * dn_ref[lrow, :] * _silu(u_ref[:, COL_DG:COL_DG + GROUP_W])
    mixed_ref[:, 3 * GROUP_W:4 * GROUP_W] = o.astype(BF16)
    for d, s_full in enumerate(final_states):
        for hd in range(N_HEADS):
            std_ref[d, hd] = s_full[hd * HEAD_DIM:(hd + 1) * HEAD_DIM, hd * HEAD_DIM:(hd + 1) * HEAD_DIM]

    zero = jnp.zeros((1, GROUP_W), F32)
    oc, cf, cb = _rglru_finish(zero, zero, af_ref, bf_ref, ab_ref, bb_ref, hf_ref, hb_ref, gel_ref)
    mixed_ref[:, 2 * GROUP_W:3 * GROUP_W] = oc.astype(BF16)
    stc_ref[0:1, :] = cf
    stc_ref[1:2, :] = cb

    xn_ref[...] = x + g2_ref[0:1, :] * _dot(mixed_ref[...], wout_s[...])


def _ctx_mixers(x, mod, layer, prev, norm2_g, w_in, w_out,
                a_qn, a_kn, a_sink, b_qn, b_kn, c_conv_w, c_conv_b, c_wa, c_ba, c_wx, c_bx,
                c_lambda, d_theta, d_norm_g):
    per_request = lambda slots, shape: pl.BlockSpec((None, slots) + shape, lambda b: (b,) + (0,) * (1 + len(shape)))
    state_dims = [(SEQ, KV_W)] * 4 + [(2, GROUP_W), (2, N_HEADS, HEAD_DIM, HEAD_DIM)]
    scr = pltpu.VMEM((SEQ, GROUP_W), F32)
    out = pl.pallas_call(
        functools.partial(_ctx_mixer_kernel, layer=layer, n_prev=len(prev)),
        grid=(BATCH,),
        in_specs=[per_request(layer, dims) for dims in state_dims[:len(prev)]] + [
            pl.BlockSpec((SEQ, D_MODEL), lambda b: (b, 0)),
            _full((DEPTH, D_MODEL)),
            _mod_chunk(layer, 3), _mod_chunk(layer, 4), _mod_chunk(layer, 5),
            _once((None, D_MODEL, IN_WIDTH), lambda b: (layer, 0, 0)),
            _once((None, D_MODEL, D_MODEL), lambda b: (layer, 0, 0)),
            _full((DEPTH, HEAD_DIM)), _full((DEPTH, HEAD_DIM)), _full((DEPTH, HEAD_DIM)), _full((DEPTH, HEAD_DIM)),
            pl.BlockSpec(memory_space=pltpu.SMEM),
            _layer_block((4, GROUP_W), layer), _full((DEPTH, GROUP_W)),
            _layer_block((2, N_HEADS, HEAD_DIM, HEAD_DIM), layer), _layer_block((2, GROUP_W), layer),
            _layer_block((2, N_HEADS, HEAD_DIM, HEAD_DIM), layer), _layer_block((2, GROUP_W), layer),
            _layer_block((2, GROUP_W), layer),
            _layer_block((2, N_HEADS), layer), _full((DEPTH, GROUP_W)),
        ],
        out_specs=[pl.BlockSpec((SEQ, D_MODEL), lambda b: (b, 0))] + [
            per_request(layer + 1, dims) for dims in state_dims],
        out_shape=[jax.ShapeDtypeStruct((N_ROWS, D_MODEL), F32)] + [
            jax.ShapeDtypeStruct((BATCH, layer + 1) + dims, F32) for dims in state_dims],
        input_output_aliases={len(prev): 0},
        scratch_shapes=[pltpu.VMEM((D_MODEL, IN_WIDTH), BF16), pltpu.VMEM((D_MODEL, D_MODEL), BF16),
                        pltpu.VMEM((SEQ, IN_WIDTH), F32), pltpu.VMEM((SEQ, D_MODEL), BF16),
                        pltpu.VMEM((4, GROUP_W, GROUP_W), BF16)] + [scr] * 8,
        compiler_params=_cparams("arbitrary"),
        name="ctx_mixers",
    )(*prev, x, norm2_g, mod, mod, mod, w_in, w_out,
      a_qn, a_kn, b_qn, b_kn, a_sink, c_conv_w, c_conv_b, c_wa, c_ba, c_wx, c_bx,
      c_lambda, d_theta, d_norm_g)
    return out[0], tuple(out[1:])


LAT_BLOCK0 = N_CTX_ROWS // DEC_SEQ


def _lat_attn_kernel(x_ref, n2_ref, sh_ref, sc_ref, g2_ref, win_ref, wout_ref,
                     kca_ref, vca_ref, kcb_ref, vcb_ref,
                     aqn_ref, akn_ref, bqn_ref, bkn_ref, sink_ref, cos_ref, sinl_ref, sinh_ref,
                     xn_ref, h_ref, u_ref, o_ref, *, layer):
    t = DEC_SEQ
    lrow = slice(layer, layer + 1)
    mrow = pl.ds(1 + pl.program_id(0), 1)
    cos, sin_lo, sin_hi = cos_ref[...], sinl_ref[...], sinh_ref[...]
    scale = HEAD_DIM ** -0.5
    x = x_ref[...]
    h_ref[...] = _norm_mod(x, n2_ref[lrow, :], sc_ref[mrow, :], sh_ref[mrow, :]).astype(BF16)
    u_ref[...] = _dot(h_ref[...], win_ref[...].astype(BF16))

    q = _rope(_head_norm(u_ref[:, COL_AQ:COL_AQ + GROUP_W], aqn_ref[lrow, :]), cos, sin_lo, sin_hi)
    k = _rope(_head_norm(u_ref[:, COL_AK:COL_AK + KV_W], akn_ref[lrow, :]), cos, sin_lo, sin_hi)
    qh = [_head_cols(q * scale, h) for h in range(4)]
    v = u_ref[:, COL_AV:COL_AV + KV_W]
    kh = [_head_cols(k, kv) for kv in range(2)]
    vh = [_head_cols(v, kv) for kv in range(2)]
    kch = [_head_cols(kca_ref[...], kv) for kv in range(2)]
    vch = [_head_cols(vca_ref[...], kv) for kv in range(2)]
    w = ATT_BLOCK
    span = 3 * w
    for n in range(t // w):
        start = min(max((n - 1) * w, 0), t - span)
        rows = slice(n * w, (n + 1) * w)
        band = slice(start, start + span)
        qpos = (lax.broadcasted_iota(jnp.int32, (2 * w, span), 0) & (w - 1)) + n * w
        kpos = lax.broadcasted_iota(jnp.int32, (2 * w, span), 1) + start
        valid = jnp.abs(qpos - kpos) <= WINDOW
        heads = []
        for kv in range(2):
            qp = jnp.concatenate([qh[2 * kv][rows, :], qh[2 * kv + 1][rows, :]], axis=0)
            s_ctx = _dot_nt(qp, kch[kv])
            s_band = jnp.where(valid, _dot_nt(qp, kh[kv][band, :]), NEG_INF)
            row = lax.broadcasted_iota(jnp.int32, (2 * w, 1), 0)
            sink = jnp.where(row < w, sink_ref[layer, 2 * kv], sink_ref[layer, 2 * kv + 1])
            o = _softmax_pv([s_ctx, s_band], [vch[kv], vh[kv][band, :]], sink)
            heads += [o[0:w, :], o[w:2 * w, :]]
        o_ref[rows, 0:GROUP_W] = jnp.concatenate(heads, axis=-1).astype(BF16)

    q = _rope(_head_norm(u_ref[:, COL_BQ:COL_BQ + GROUP_W], bqn_ref[lrow, :]), cos, sin_lo, sin_hi)
    k = _rope(_head_norm(u_ref[:, COL_BK:COL_BK + KV_W], bkn_ref[lrow, :]), cos, sin_lo, sin_hi)
    qh = [_head_cols(q * scale, h) for h in range(4)]
    v = u_ref[:, COL_BV:COL_BV + KV_W]
    kh = [_head_cols(k, kv) for kv in range(2)]
    vh = [_head_cols(v, kv) for kv in range(2)]
    kch = [_head_cols(kcb_ref[...], kv) for kv in range(2)]
    vch = [_head_cols(vcb_ref[...], kv) for kv in range(2)]
    tq = 2 * ATT_BLOCK
    for n in range(t // tq):
        rows = slice(n * tq, (n + 1) * tq)
        heads = []
        for kv in range(2):
            qp = jnp.concatenate([qh[2 * kv][rows, :], qh[2 * kv + 1][rows, :]], axis=0)
            o = _softmax_pv([_dot_nt(qp, kch[kv]), _dot_nt(qp, kh[kv])], [vch[kv], vh[kv]], None)
            heads += [o[0:tq, :], o[tq:2 * tq, :]]
        o_ref[rows, GROUP_W:2 * GROUP_W] = jnp.concatenate(heads, axis=-1).astype(BF16)

    xn_ref[...] = x + g2_ref[mrow, :] * _dot(o_ref[...], wout_ref[...].astype(BF16))


def _lat_recurrent_kernel(xn_in_ref, h_ref, g2_ref, wc_ref, wqk_ref, wvg_ref, wout_ref, h0_ref,
                          convw_ref, convb_ref, wa_ref, ba_ref, wx_ref, bx_ref, lam_ref,
                          s0_ref, theta_ref, dn_ref,
                          xn_ref, gate_w_ref, af_ref, bf_ref, ab_ref, bb_ref, hf_ref, hb_ref, gel_ref, ret_ref,
                          *, layer):
    lrow = slice(layer, layer + 1)
    mrow = pl.ds(1 + pl.program_id(0), 1)

    @pl.when(pl.program_id(0) == 0)
    def _():
        _store_gate_weights(gate_w_ref, wa_ref, wx_ref)

    h = h_ref[...]
    u = _dot(h, wc_ref[...].astype(BF16))
    _rglru_prepare(u[:, 0:GROUP_W], u[:, GROUP_W:2 * GROUP_W], convw_ref, convb_ref[lrow, :],
                   gate_w_ref, ba_ref, bx_ref, lam_ref, af_ref, bf_ref, ab_ref, bb_ref, gel_ref)
    oc, _, _ = _rglru_finish(h0_ref[0:1, :], h0_ref[1:2, :],
                             af_ref, bf_ref, ab_ref, bb_ref, hf_ref, hb_ref, gel_ref)
    y = _dot(oc.astype(BF16), wout_ref[0:GROUP_W, :].astype(BF16))

    uqk = _dot(h, wqk_ref[...].astype(BF16))
    uvg = _dot(h, wvg_ref[...].astype(BF16))
    masks = _lane_head_masks(GROUP_W)
    lgf, lgb = _log_decays(theta_ref, masks)
    s0 = tuple(_block_diag([s0_ref[d, hd] for hd in range(N_HEADS)]) for d in range(2))
    _retention(uqk[:, 0:GROUP_W], uqk[:, GROUP_W:2 * GROUP_W] * (HEAD_DIM ** -0.5),
               uvg[:, 0:GROUP_W].astype(BF16), s0, lgf, lgb, masks, ret_ref)
    o = ret_ref[...]
    o = o * lax.rsqrt(_head_mean_square(o) + EPS) * dn_ref[lrow, :] * _silu(uvg[:, GROUP_W:2 * GROUP_W])
    y = y + _dot(o.astype(BF16), wout_ref[GROUP_W:2 * GROUP_W, :].astype(BF16))
    xn_ref[...] = xn_in_ref[...] + g2_ref[mrow, :] * y


def _lat_mixers(x, mod, layer, caches, state_c, state_d, rope, norm2_g, w_in, w_out,
                a_qn, a_kn, a_sink, b_qn, b_kn, c_conv_w, c_conv_b, c_wa, c_ba, c_wx, c_bx,
                c_lambda, d_theta, d_norm_g):
    rows = pl.BlockSpec((DEC_SEQ, D_MODEL), lambda b: (LAT_BLOCK0 + b, 0))
    h_rows = pl.BlockSpec((DEC_SEQ, D_MODEL), lambda b: (b, 0))
    cache_spec = pl.BlockSpec((None, None, PAST_LEN, KV_W), lambda b: (b, layer, 0, 0))
    gain = _full((DEPTH, HEAD_DIM))
    table = _once((DEC_SEQ, LANES), lambda b: (0, 0))
    out_shape = jax.ShapeDtypeStruct((N_ROWS, D_MODEL), F32)
    win_cols = lambda w, c: _once((None, D_MODEL, w), lambda b: (layer, 0, c))
    wout_rows = lambda h, r: _once((None, h, D_MODEL), lambda b: (layer, r, 0))

    xn, h = pl.pallas_call(
        functools.partial(_lat_attn_kernel, layer=layer),
        grid=(DEC_BATCH,),
        in_specs=[rows, _full((DEPTH, D_MODEL)),
                  _mod_chunk(layer, 3), _mod_chunk(layer, 4), _mod_chunk(layer, 5),
                  win_cols(4 * GROUP_W, 0), wout_rows(2 * GROUP_W, 0),
                  cache_spec, cache_spec, cache_spec, cache_spec,
                  gain, gain, gain, gain,
                  pl.BlockSpec(memory_space=pltpu.SMEM),
                  table, table, table],
        out_specs=[rows, pl.BlockSpec((DEC_SEQ, D_MODEL), lambda b: (b, 0), pipeline_mode=pl.Buffered(1))],
        out_shape=[out_shape, jax.ShapeDtypeStruct((N_LAT_ROWS, D_MODEL), BF16)],
        input_output_aliases={0: 0},
        scratch_shapes=[pltpu.VMEM((DEC_SEQ, 4 * GROUP_W), F32), pltpu.VMEM((DEC_SEQ, 2 * GROUP_W), BF16)],
        compiler_params=_cparams("arbitrary"),
        name="lat_attention",
    )(x, norm2_g, mod, mod, mod, w_in, w_out, *caches, a_qn, a_kn, b_qn, b_kn, a_sink, *rope)

    scr = pltpu.VMEM((DEC_SEQ, GROUP_W), F32)
    xn = pl.pallas_call(
        functools.partial(_lat_recurrent_kernel, layer=layer),
        grid=(DEC_BATCH,),
        in_specs=[
            rows, h_rows, _mod_chunk(layer, 5),
            win_cols(2 * GROUP_W, COL_CX // (2 * GROUP_W)),
            win_cols(2 * GROUP_W, COL_DQ // (2 * GROUP_W)), win_cols(2 * GROUP_W, COL_DV // (2 * GROUP_W)),
            wout_rows(2 * GROUP_W, 1),
            pl.BlockSpec((None, None, 2, GROUP_W), lambda b: (b, layer, 0, 0)),
            _layer_block((4, GROUP_W), layer), _full((DEPTH, GROUP_W)),
            _layer_block((2, N_HEADS, HEAD_DIM, HEAD_DIM), layer), _layer_block((2, GROUP_W), layer),
            _layer_block((2, N_HEADS, HEAD_DIM, HEAD_DIM), layer), _layer_block((2, GROUP_W), layer),
            _layer_block((2, GROUP_W), layer),
            pl.BlockSpec((None, None, 2, N_HEADS, HEAD_DIM, HEAD_DIM), lambda b: (b, layer, 0, 0, 0, 0)),
            _layer_block((2, N_HEADS), layer), _full((DEPTH, GROUP_W))],
        out_specs=rows,
        out_shape=out_shape,
        input_output_aliases={0: 0},
        scratch_shapes=[pltpu.VMEM((4, GROUP_W, GROUP_W), BF16)] + [scr] * 8,
        compiler_params=_cparams("arbitrary"),
        name="lat_recurrent",
    )(xn, h, mod, w_in, w_in, w_in, w_out, state_c, c_conv_w, c_conv_b, c_wa, c_ba, c_wx, c_bx, c_lambda,
      state_d, d_theta, d_norm_g)
    return xn


def _rope_tables():
    t = np.arange(DEC_SEQ)
    row = (t // GRID_W).astype(np.float64)[:, None]
    col = (t % GRID_W).astype(np.float64)[:, None]
    half = HEAD_DIM // 2
    inv = 1.0 / (ROPE_BASE ** (np.arange(0, half, 2, dtype=np.float64) / half))
    j = np.arange(LANES) % HEAD_DIM
    ang = np.where((j < half)[None, :], row, col) * inv[j % (half // 2)][None, :]
    first = ((j % half) < half // 2)[None, :]
    cos, sin = np.cos(ang), np.sin(ang)
    return tuple(jnp.asarray(a, F32) for a in (cos, np.where(first, -sin, 0.0), np.where(first, 0.0, sin)))


def kernel(x_prompt, x_sample, cache_a_k, cache_a_v, cache_b_k, cache_b_v, state_c, state_d, c, c_ctx, norm1_g, norm2_g, norm3_g, w_mod, b_mod, ffn1_wg, ffn1_wu, ffn1_wd, ffn2_wg, ffn2_wu, ffn2_wd, w_in, w_out, a_qn, a_kn, a_sink, b_qn, b_kn, c_conv_w, c_conv_b, c_wa, c_ba, c_wx, c_bx, c_lambda, d_theta, d_norm_g):
    mod = _modulation(c_ctx, c, w_mod, b_mod)
    rope = _rope_tables()
    caches = tuple(t.reshape(DEC_BATCH, DEPTH, PAST_LEN, KV_W) for t in (cache_a_k, cache_a_v, cache_b_k, cache_b_v))
    mixer_params = (a_qn, a_kn, a_sink, b_qn, b_kn, c_conv_w, c_conv_b, c_wa, c_ba, c_wx, c_bx,
                    c_lambda, d_theta, d_norm_g)
    xs = (x_prompt.reshape(N_CTX_ROWS, D_MODEL), x_sample.reshape(N_LAT_ROWS, D_MODEL))
    states = ()
    for l in range(DEPTH):
        (x,) = _ffn(xs, mod, l, 0, norm1_g, ffn1_wg, ffn1_wu, ffn1_wd)
        x, states = _ctx_mixers(x, mod, l, states, norm2_g, w_in, w_out, *mixer_params)
        x = _lat_mixers(x, mod, l, caches, state_c, state_d, rope, norm2_g, w_in, w_out, *mixer_params)
        xs = _ffn((x,), mod, l, 6, norm3_g, ffn2_wg, ffn2_wu, ffn2_wd, split_out=(l == DEPTH - 1))
    y_p, y_s = xs
    ka, va, kb, vb, st_c, st_d = states
    kv_shape = (BATCH, DEPTH, SEQ, 2, HEAD_DIM)
    return (y_p.reshape(BATCH, SEQ, D_MODEL), y_s.reshape(DEC_BATCH, DEC_SEQ, D_MODEL),
            ka.reshape(kv_shape), va.reshape(kv_shape), kb.reshape(kv_shape), vb.reshape(kv_shape),
            st_c, st_d)
```

```python
import functools
import math

import numpy as np
import jax
import jax.numpy as jnp
from jax import lax
from jax.experimental import pallas as pl
from jax.experimental.pallas import tpu as pltpu

F32 = jnp.float32
BF16 = jnp.bfloat16

D_MODEL = 1024
BATCH = 16
SEQ = 256
DEPTH = 2
DEC_BATCH = 2
DEC_SEQ = 1024
PAST_LEN = 512
GRID_W = 64
HEAD_DIM = 64
HEAD_SHIFT = 6
N_HEADS = 4
GROUP_W = 256
KV_W = 2 * HEAD_DIM
LANES = 128
WINDOW = 128
ATT_BLOCK = 128
ROPE_BASE = 10000.0
LRU_C = 8.0
D_FF = 2816
N_MOD = 9
EPS = 1e-6
NEG_INF = -1e30
IN_WIDTH = 2560

N_CTX_ROWS = BATCH * SEQ
N_LAT_ROWS = DEC_BATCH * DEC_SEQ
N_ROWS = N_CTX_ROWS + N_LAT_ROWS
MOD_ROWS = 8
MOD_GROUP = 1024

VMEM_LIMIT_BYTES = 56 * 1024 * 1024

COL_AQ, COL_AK, COL_AV = 0, 256, 384
COL_BQ, COL_BK, COL_BV = 512, 768, 896
COL_CX, COL_CY = 1024, 1280
COL_DQ, COL_DK, COL_DV, COL_DG = 1536, 1792, 2048, 2304


def _cparams(*sem):
    return pltpu.CompilerParams(dimension_semantics=sem, vmem_limit_bytes=VMEM_LIMIT_BYTES)


def _dot(a, b):
    return jnp.dot(a, b, preferred_element_type=F32)


def _dot_nt(a, b):
    return lax.dot_general(a, b, (((1,), (1,)), ((), ())), preferred_element_type=F32)


def _dot_tn(a, b):
    return lax.dot_general(a, b, (((0,), (0,)), ((), ())), preferred_element_type=F32)


def _sigmoid(x):
    return 0.5 * jnp.tanh(0.5 * x) + 0.5


def _silu(x):
    return x * _sigmoid(x)


def _gelu_tanh(x):
    return 0.5 * x * (1.0 + jnp.tanh(math.sqrt(2.0 / math.pi) * (x + 0.044715 * (x * x * x))))


def _mod_row(i, tm, s):
    if tm >= MOD_GROUP:
        block_index = i * (tm // MOD_GROUP) + s
    else:
        block_index = i >> int(math.log2(MOD_GROUP // tm))
    return jnp.maximum(block_index - (N_CTX_ROWS // MOD_GROUP - 1), 0)


def _norm_mod(x, g, sc, sh):
    ms = jnp.mean(x * x, axis=-1, keepdims=True)
    return (x * lax.rsqrt(ms + EPS) * g) * (1.0 + sc) + sh


def _full(shape):
    return pl.BlockSpec(shape, lambda *_: (0,) * len(shape))


def _layer_block(shape, layer):
    return pl.BlockSpec((None,) + shape, lambda *_: (layer,) + (0,) * len(shape))


MOD_TN = 3072


def _mod_kernel(cc_ref, c_ref, w_ref, b_ref, o_ref):
    l = pl.program_id(0)
    pad = jnp.zeros((MOD_ROWS - 1 - DEC_BATCH, D_MODEL), F32)
    cond = jnp.concatenate([cc_ref[...], c_ref[...], pad], axis=0)
    o_ref[...] = _dot(_silu(cond).astype(BF16), w_ref[...].astype(BF16)) + b_ref[pl.ds(l, 1), :]


def _modulation(c_ctx, c, w_mod, b_mod):
    n = N_MOD * D_MODEL
    return pl.pallas_call(
        _mod_kernel,
        grid=(DEPTH, n // MOD_TN),
        in_specs=[
            pl.BlockSpec((1, D_MODEL), lambda l, j: (0, 0)),
            pl.BlockSpec((DEC_BATCH, D_MODEL), lambda l, j: (0, 0)),
            pl.BlockSpec((None, D_MODEL, MOD_TN), lambda l, j: (l, 0, j)),
            pl.BlockSpec((DEPTH, MOD_TN), lambda l, j: (0, j)),
        ],
        out_specs=pl.BlockSpec((None, MOD_ROWS, MOD_TN), lambda l, j: (l, 0, j)),
        out_shape=jax.ShapeDtypeStruct((DEPTH, MOD_ROWS, n), F32),
        compiler_params=_cparams("arbitrary", "arbitrary"),
        name="modulation",
    )(c_ctx.reshape(1, D_MODEL), c, w_mod, b_mod)


FFN_TM = 1024
FFN_TF = 256
N_CTX_TILES = N_CTX_ROWS // FFN_TM


FFN_NJ = D_FF // FFN_TF
N_FFN_TILES = N_ROWS // FFN_TM
N_FFN_STEPS = FFN_NJ + N_FFN_TILES


def _ffn_tile(step):
    return jnp.maximum(step - FFN_NJ, 0)


def _on_stream_part(tile, x_refs, o_refs, fn):
    if len(x_refs) == 1 and len(o_refs) == 1:
        fn(x_refs[0], o_refs[0])
    else:
        pl.when(tile < N_CTX_TILES)(lambda: fn(x_refs[0], o_refs[0]))
        pl.when(tile >= N_CTX_TILES)(lambda: fn(x_refs[-1], o_refs[-1]))


def _ffn_kernel(*refs, layer, n_in, n_out):
    x_refs = refs[:n_in]
    n_ref, sh_ref, sc_ref, g_ref, wg_ref, wu_ref, wd_ref = refs[n_in:n_in + 7]
    o_refs = refs[n_in + 7:n_in + 7 + n_out]
    h_ref, a_ref, wg_s, wu_s, wd_s = refs[n_in + 7 + n_out:]
    nj, tf = FFN_NJ, FFN_TF
    s = pl.program_id(0)
    tile = _ffn_tile(s)
    r = _mod_row(tile, FFN_TM, 0)

    def load_tile():
        def init(x_ref, _):
            h = _norm_mod(x_ref[...], n_ref[layer:layer + 1, :], sc_ref[pl.ds(r, 1), :], sh_ref[pl.ds(r, 1), :])
            h_ref[...] = h.astype(BF16)
        _on_stream_part(tile, x_refs, x_refs, init)

    def up_chunk(j, cols):
        h = h_ref[...]
        a_ref[:, cols] = (_silu(_dot(h, wg_s[j])) * _dot(h, wu_s[j])).astype(BF16)

    def down_and_store():
        y = (0.5 * g_ref[pl.ds(r, 1), :]) * _dot(a_ref[...], wd_s[...])

        def store(x_ref, o_ref):
            o_ref[...] = x_ref[...] + y
        _on_stream_part(tile, x_refs, o_refs, store)

    def keep_arrived_chunk():
        wg_s[s] = wg_ref[...].astype(BF16)
        wu_s[s] = wu_ref[...].astype(BF16)
        wd_s[pl.ds(pl.multiple_of(s * tf, tf), tf), :] = wd_ref[...].astype(BF16)

    def up_previous_chunk():
        up_chunk(s - 1, pl.ds(pl.multiple_of((s - 1) * tf, tf), tf))

    @pl.when(s == 0)
    def _():
        load_tile()
        keep_arrived_chunk()

    @pl.when((s > 0) & (s < nj))
    def _():
        up_previous_chunk()
        keep_arrived_chunk()

    @pl.when(s == nj)
    def _():
        up_previous_chunk()
        down_and_store()

    @pl.when(s > nj)
    def _():
        load_tile()
        for j in range(nj):
            up_chunk(j, slice(j * tf, (j + 1) * tf))
        down_and_store()


def _stream_specs(split, buffered_once):
    tm = FFN_TM
    kw = {"pipeline_mode": pl.Buffered(1)} if buffered_once else {}
    if not split:
        return [pl.BlockSpec((tm, D_MODEL), lambda s: (_ffn_tile(s), 0), **kw)]
    last_ctx = N_CTX_TILES - 1
    return [pl.BlockSpec((tm, D_MODEL), lambda s: (jnp.minimum(_ffn_tile(s), last_ctx), 0), **kw),
            pl.BlockSpec((tm, D_MODEL), lambda s: (jnp.maximum(_ffn_tile(s) - N_CTX_TILES, 0), 0), **kw)]


def _ffn(xs, mod, layer, chunk0, norm_g, wg, wu, wd, split_out=False):
    tm, tf, nj = FFN_TM, FFN_TF, FFN_NJ
    split_in = len(xs) == 2
    mod_spec = lambda c: pl.BlockSpec((None, MOD_ROWS, D_MODEL), lambda s: (layer, 0, c))
    w_col = lambda s: (layer, 0, jnp.minimum(s, nj - 1))
    w_row = lambda s: (layer, jnp.minimum(s, nj - 1), 0)
    if split_out:
        out_shape = [jax.ShapeDtypeStruct((N_CTX_ROWS, D_MODEL), F32),
                     jax.ShapeDtypeStruct((N_LAT_ROWS, D_MODEL), F32)]
    else:
        out_shape = [jax.ShapeDtypeStruct((N_ROWS, D_MODEL), F32)]
    out = pl.pallas_call(
        functools.partial(_ffn_kernel, layer=layer, n_in=len(xs), n_out=len(out_shape)),
        grid=(N_FFN_STEPS,),
        in_specs=_stream_specs(split_in, False) + [
            _full((DEPTH, D_MODEL)),
            mod_spec(chunk0), mod_spec(chunk0 + 1), mod_spec(chunk0 + 2),
            pl.BlockSpec((None, D_MODEL, tf), w_col),
            pl.BlockSpec((None, D_MODEL, tf), w_col),
            pl.BlockSpec((None, tf, D_MODEL), w_row),
        ],
        out_specs=_stream_specs(split_out, True),
        out_shape=out_shape,
        scratch_shapes=[pltpu.VMEM((tm, D_MODEL), BF16),
                        pltpu.VMEM((tm, D_FF), BF16),
                        pltpu.VMEM((nj, D_MODEL, tf), BF16),
                        pltpu.VMEM((nj, D_MODEL, tf), BF16),
                        pltpu.VMEM((D_FF, D_MODEL), BF16)],
        compiler_params=_cparams("arbitrary"),
        name="ffn",
    )(*xs, norm_g, mod, mod, mod, wg, wu, wd)
    return tuple(out)


def _once(shape, index_map):
    return pl.BlockSpec(shape, index_map, pipeline_mode=pl.Buffered(1))


def _mod_chunk(layer, c):
    return pl.BlockSpec((None, MOD_ROWS, D_MODEL), lambda *_: (layer, 0, c))


def _head_mean_square(x):
    n = x.shape[-1]
    r = lax.broadcasted_iota(jnp.int32, (n, n), 0) >> HEAD_SHIFT
    c = lax.broadcasted_iota(jnp.int32, (n, n), 1) >> HEAD_SHIFT
    ones_bd = jnp.where(r == c, 1.0, 0.0).astype(BF16)
    return _dot((x * x).astype(BF16), ones_bd) * (1.0 / HEAD_DIM)


def _head_norm(x, head_gain):
    gain_row = jnp.concatenate([head_gain] * (x.shape[-1] // HEAD_DIM), axis=-1)
    return x * lax.rsqrt(_head_mean_square(x) + EPS) * gain_row


def _head_cols(x, h):
    return x[:, h * HEAD_DIM:(h + 1) * HEAD_DIM].astype(BF16)


def _softmax_pv(scores, values, sink):
    m = jnp.max(scores[0], axis=-1, keepdims=True)
    for s in scores[1:]:
        m = jnp.maximum(m, jnp.max(s, axis=-1, keepdims=True))
    if sink is not None:
        m = jnp.maximum(m, sink)
    denom = None
    acc = None
    for s, v in zip(scores, values):
        p = jnp.exp(s - m)
        d = jnp.sum(p, axis=-1, keepdims=True)
        o = _dot(p.astype(BF16), v)
        denom = d if denom is None else denom + d
        acc = o if acc is None else acc + o
    if sink is not None:
        denom = denom + jnp.exp(sink - m)
    return acc / denom


def _rope(x, cos, sin_lo, sin_hi):
    cols = []
    for c in range(x.shape[-1] // LANES):
        xc = x[:, c * LANES:(c + 1) * LANES]
        cols.append(xc * cos + pltpu.roll(xc, 112, 1) * sin_lo + pltpu.roll(xc, 16, 1) * sin_hi)
    return cols[0] if len(cols) == 1 else jnp.concatenate(cols, axis=-1)


def _block_diag(blocks):
    n = len(blocks)
    w = blocks[0].shape[0]
    rows = []
    for k, blk in enumerate(blocks):
        parts = []
        if k > 0:
            parts.append(jnp.zeros((w, k * w), F32))
        parts.append(blk)
        if k < n - 1:
            parts.append(jnp.zeros((w, (n - 1 - k) * w), F32))
        rows.append(jnp.concatenate(parts, axis=-1))
    return jnp.concatenate(rows, axis=0)


def _rglru_gates(xc, wa, ba, wx, bx, lam):
    xb = xc.astype(BF16)
    r = _sigmoid(_dot(xb, wa) + ba)
    i = _sigmoid(_dot(xb, wx) + bx)
    softplus = jnp.maximum(-lam, 0.0) + jnp.log1p(jnp.exp(-jnp.abs(lam)))
    log_a = (-LRU_C) * r * softplus
    a = jnp.exp(log_a)
    b = jnp.sqrt(1.0 - a * a) * (i * xc)
    return a, b


def _block_prefix(a, b, reverse):
    t = a.shape[0]
    row = lax.broadcasted_iota(jnp.int32, a.shape, 0) & 7
    for d in (1, 2, 4):
        if reverse:
            a_s = pltpu.roll(a, t - d, 0)
            b_s = pltpu.roll(b, t - d, 0)
            ok = row < 8 - d
        else:
            a_s = pltpu.roll(a, d, 0)
            b_s = pltpu.roll(b, d, 0)
            ok = row >= d
        b = jnp.where(ok, a * b_s + b, b)
        a = jnp.where(ok, a * a_s, a)
    return a, b


def _conv4(x, w_ref, b_row):
    t = x.shape[0]
    row = lax.broadcasted_iota(jnp.int32, x.shape, 0)
    xm2 = jnp.where(row >= 2, pltpu.roll(x, 2, 0), 0.0)
    xm1 = jnp.where(row >= 1, pltpu.roll(x, 1, 0), 0.0)
    xp1 = jnp.where(row < t - 1, pltpu.roll(x, t - 1, 0), 0.0)
    return (xm2 * w_ref[0:1, :] + xm1 * w_ref[1:2, :] + x * w_ref[2:3, :] + xp1 * w_ref[3:4, :]) + b_row


def _rglru_prepare(cx, cy, conv_w_ref, conv_b, gate_w_ref, ba_ref, bx_ref, lam_ref,
                   af_ref, bf_ref, ab_ref, bb_ref, gel_ref):
    xc = _conv4(cx, conv_w_ref, conv_b)
    a, b = _rglru_gates(xc, gate_w_ref[0], ba_ref[0:1, :], gate_w_ref[1], bx_ref[0:1, :], lam_ref[0:1, :])
    a, b = _block_prefix(a, b, reverse=False)
    af_ref[...] = a
    bf_ref[...] = b
    a, b = _rglru_gates(xc, gate_w_ref[2], ba_ref[1:2, :], gate_w_ref[3], bx_ref[1:2, :], lam_ref[1:2, :])
    a, b = _block_prefix(a, b, reverse=True)
    ab_ref[...] = a
    bb_ref[...] = b
    gel_ref[...] = _gelu_tanh(cy)


SCAN_UNROLL = 8


def _rglru_finish(h0f, h0b, af_ref, bf_ref, ab_ref, bb_ref, hf_ref, hb_ref, gel_ref):
    nblk = af_ref.shape[0] // 8

    def body(k, carry):
        cf, cb = carry
        rf = pl.ds(pl.multiple_of(k * 8, 8), 8)
        hf = bf_ref[rf, :] + af_ref[rf, :] * cf
        hf_ref[rf, :] = hf
        rb = pl.ds(pl.multiple_of((nblk - 1 - k) * 8, 8), 8)
        hb = bb_ref[rb, :] + ab_ref[rb, :] * cb
        hb_ref[rb, :] = hb
        return hf[7:8, :], hb[0:1, :]

    cf, cb = lax.fori_loop(0, nblk, body, (h0f, h0b), unroll=SCAN_UNROLL)
    oc = (hf_ref[...] + hb_ref[...]) * gel_ref[...]
    return oc, cf, cb


def _store_gate_weights(gate_w_ref, wa_ref, wx_ref):
    for d in range(2):
        gate_w_ref[2 * d] = _block_diag([wa_ref[d, n] for n in range(N_HEADS)]).astype(BF16)
        gate_w_ref[2 * d + 1] = _block_diag([wx_ref[d, n] for n in range(N_HEADS)]).astype(BF16)


def _lane_head_masks(n):
    lane = lax.broadcasted_iota(jnp.int32, (1, n), 1) >> HEAD_SHIFT
    return [jnp.where(lane == h, 1.0, 0.0) for h in range(n // HEAD_DIM)]


def _log_decays(theta_ref, masks):
    theta = theta_ref[...]
    lanes = theta[:, 0:1] * masks[0]
    for h in range(1, N_HEADS):
        lanes = lanes + theta[:, h:h + 1] * masks[h]
    lg = jnp.log1p(-jnp.exp(lanes))
    return lg[0:1, :], lg[1:2, :]


RET_BLOCK = 256


def _retention(q, k8, vb, s0, lgf, lgb, masks, o_ref):
    t, w = q.shape
    c = RET_BLOCK
    nh = w // HEAD_DIM
    pos = lax.broadcasted_iota(jnp.int32, (c, w), 0).astype(F32)
    q_dec = (jnp.exp(lgf * (pos + 1.0)), jnp.exp(lgb * (float(c) - pos)))
    k_dec = (jnp.exp(lgf * (float(c - 1) - pos)), jnp.exp(lgb * pos))
    chunk_dec = (jnp.exp(lgf * float(c)), jnp.exp(lgb * float(c)))
    rel = (lax.broadcasted_iota(jnp.int32, (c, c), 0) - lax.broadcasted_iota(jnp.int32, (c, c), 1)).astype(F32)
    decs = []
    for h in range(nh):
        gf = lgf[:, h * HEAD_DIM:h * HEAD_DIM + 1]
        gb = lgb[:, h * HEAD_DIM:h * HEAD_DIM + 1]
        e = jnp.exp(jnp.where(rel >= 0, gf * rel, gb * (-rel)))
        decs.append(jnp.where(rel == 0, 2.0, e))
    dec = jnp.concatenate(decs, axis=0)
    r_head = lax.broadcasted_iota(jnp.int32, (w, w), 0) >> HEAD_SHIFT
    c_head = lax.broadcasted_iota(jnp.int32, (w, w), 1) >> HEAD_SHIFT
    same_head = jnp.where(r_head == c_head, 1.0, 0.0)
    states = [None, None] if s0 is None else list(s0)

    def carry(d, rows, o):
        if states[d] is not None:
            o = o + _dot((q[rows, :] * q_dec[d]).astype(BF16), states[d].astype(BF16))
        upd = _dot_tn((k8[rows, :] * k_dec[d]).astype(BF16), vb[rows, :]) * same_head
        states[d] = upd if states[d] is None else states[d] * chunk_dec[d] + upd
        return o

    for ci in range(t // c):
        rows = slice(ci * c, (ci + 1) * c)
        qc = q[rows, :]
        q_stack = jnp.concatenate([(qc * masks[h]).astype(BF16) for h in range(nh)], axis=0)
        inner = (_dot_nt(q_stack, k8[rows, :].astype(BF16)) * dec).astype(BF16)
        out = _dot(inner, vb[rows, :])
        o = out[0:c, :] * masks[0]
        for h in range(1, nh):
            o = o + out[h * c:(h + 1) * c, :] * masks[h]
        o_ref[rows, :] = carry(0, rows, o)
    for ci in reversed(range(t // c)):
        rows = slice(ci * c, (ci + 1) * c)
        if states[1] is not None:
            o_ref[rows, :] = carry(1, rows, o_ref[rows, :])
        else:
            carry(1, rows, None)
    return states[0], states[1]


def _ctx_mixer_kernel(*refs, layer, n_prev):
    prev_refs = refs[:n_prev]
    (x_ref, n2_ref, sh_ref, sc_ref, g2_ref, win_ref, wout_ref,
     aqn_ref, akn_ref, bqn_ref, bkn_ref, sink_ref,
     convw_ref, convb_ref, wa_ref, ba_ref, wx_ref, bx_ref, lam_ref, theta_ref, dn_ref,
     xn_ref, *state_refs) = refs[n_prev:n_prev + 28]
    (win_s, wout_s, u_ref, mixed_ref,
     gate_w_ref, af_ref, bf_ref, ab_ref, bb_ref, hf_ref, hb_ref, gel_ref, ret_ref) = refs[n_prev + 28:]
    t = SEQ
    lrow = slice(layer, layer + 1)
    for prev_ref, state_ref in zip(prev_refs, state_refs):
        for earlier in range(layer):
            state_ref[earlier] = prev_ref[earlier]
    ka_ref, va_ref, kb_ref, vb_ref, stc_ref, std_ref = (ref.at[layer] for ref in state_refs)

    @pl.when(pl.program_id(0) == 0)
    def _():
        for c0 in range(0, IN_WIDTH, 2 * GROUP_W):
            win_s[:, c0:c0 + 2 * GROUP_W] = win_ref[:, c0:c0 + 2 * GROUP_W].astype(BF16)
        wout_s[...] = wout_ref[...].astype(BF16)
        _store_gate_weights(gate_w_ref, wa_ref, wx_ref)

    x = x_ref[...]
    h = _norm_mod(x, n2_ref[lrow, :], sc_ref[0:1, :], sh_ref[0:1, :]).astype(BF16)
    c_cols = slice(COL_CX, COL_CX + 2 * GROUP_W)
    u_ref[:, c_cols] = _dot(h, win_s[:, c_cols])
    _rglru_prepare(u_ref[:, COL_CX:COL_CX + GROUP_W], u_ref[:, COL_CY:COL_CY + GROUP_W],
                   convw_ref, convb_ref[lrow, :], gate_w_ref, ba_ref, bx_ref, lam_ref,
                   af_ref, bf_ref, ab_ref, bb_ref, gel_ref)
    u_ref[:, 0:COL_CX] = _dot(h, win_s[:, 0:COL_CX])
    u_ref[:, COL_DQ:IN_WIDTH] = _dot(h, win_s[:, COL_DQ:IN_WIDTH])

    for (cq, ck, cv, qn_ref, kn_ref, k_out, v_out, col0, use_sink) in (
            (COL_AQ, COL_AK, COL_AV, aqn_ref, akn_ref, ka_ref, va_ref, 0, True),
            (COL_BQ, COL_BK, COL_BV, bqn_ref, bkn_ref, kb_ref, vb_ref, GROUP_W, False)):
        q = _head_norm(u_ref[:, cq:cq + GROUP_W], qn_ref[lrow, :])
        k = _head_norm(u_ref[:, ck:ck + KV_W], kn_ref[lrow, :])
        v = u_ref[:, cv:cv + KV_W]
        k_out[...] = k
        v_out[...] = v
        qs = q * (HEAD_DIM ** -0.5)
        heads = []
        for hd in range(N_HEADS):
            kv = hd // 2
            s = _dot_nt(_head_cols(qs, hd), _head_cols(k, kv))
            sink = jnp.full((t, 1), sink_ref[layer, hd], F32) if use_sink else None
            heads.append(_softmax_pv([s], [_head_cols(v, kv)], sink))
        mixed_ref[:, col0:col0 + GROUP_W] = jnp.concatenate(heads, axis=-1).astype(BF16)

    zero = jnp.zeros((1, GROUP_W), F32)
    oc, cf, cb = _rglru_finish(zero, zero, af_ref, bf_ref, ab_ref, bb_ref, hf_ref, hb_ref, gel_ref)
    mixed_ref[:, 2 * GROUP_W:3 * GROUP_W] = oc.astype(BF16)
    stc_ref[0:1, :] = cf
    stc_ref[1:2, :] = cb

    masks = _lane_head_masks(GROUP_W)
    lgf, lgb = _log_decays(theta_ref, masks)
    k8 = u_ref[:, COL_DK:COL_DK + GROUP_W] * (HEAD_DIM ** -0.5)
    vb = u_ref[:, COL_DV:COL_DV + GROUP_W].astype(BF16)
    final_states = _retention(u_ref[:, COL_DQ:COL_DQ + GROUP_W], k8, vb, None, lgf, lgb, masks, ret_ref)
    o = ret_ref[...]
    o = o * lax.rsqrt(_head_mean_square(o) + EPS) * dn_ref[lrow, :] * _silu(u_ref[:, COL_DG:COL_DG + GROUP_W])
    mixed_ref[:, 3 * GROUP_W:4 * GROUP_W] = o.astype(BF16)
    for d, s_full in enumerate(final_states):
        for hd in range(N_HEADS):
            std_ref[d, hd] = s_full[hd * HEAD_DIM:(hd + 1) * HEAD_DIM, hd * HEAD_DIM:(hd + 1) * HEAD_DIM]

    xn_ref[...] = x + g2_ref[0:1, :] * _dot(mixed_ref[...], wout_s[...])


def _ctx_mixers(x, mod, layer, prev, norm2_g, w_in, w_out,
                a_qn, a_kn, a_sink, b_qn, b_kn, c_conv_w, c_conv_b, c_wa, c_ba, c_wx, c_bx,
                c_lambda, d_theta, d_norm_g):
    per_request = lambda slots, shape: pl.BlockSpec((None, slots) + shape, lambda b: (b,) + (0,) * (1 + len(shape)))
    state_dims = [(SEQ, KV_W)] * 4 + [(2, GROUP_W), (2, N_HEADS, HEAD_DIM, HEAD_DIM)]
    scr = pltpu.VMEM((SEQ, GROUP_W), F32)
    out = pl.pallas_call(
        functools.partial(_ctx_mixer_kernel, layer=layer, n_prev=len(prev)),
        grid=(BATCH,),
        in_specs=[per_request(layer, dims) for dims in state_dims[:len(prev)]] + [
            pl.BlockSpec((SEQ, D_MODEL), lambda b: (b, 0)),
            _full((DEPTH, D_MODEL)),
            _mod_chunk(layer, 3), _mod_chunk(layer, 4), _mod_chunk(layer, 5),
            _once((None, D_MODEL, IN_WIDTH), lambda b: (layer, 0, 0)),
            _once((None, D_MODEL, D_MODEL), lambda b: (layer, 0, 0)),
            _full((DEPTH, HEAD_DIM)), _full((DEPTH, HEAD_DIM)), _full((DEPTH, HEAD_DIM)), _full((DEPTH, HEAD_DIM)),
            pl.BlockSpec(memory_space=pltpu.SMEM),
            _layer_block((4, GROUP_W), layer), _full((DEPTH, GROUP_W)),
            _layer_block((2, N_HEADS, HEAD_DIM, HEAD_DIM), layer), _layer_block((2, GROUP_W), layer),
            _layer_block((2, N_HEADS, HEAD_DIM, HEAD_DIM), layer), _layer_block((2, GROUP_W), layer),
            _layer_block((2, GROUP_W), layer),
            _layer_block((2, N_HEADS), layer), _full((DEPTH, GROUP_W)),
        ],
        out_specs=[pl.BlockSpec((SEQ, D_MODEL), lambda b: (b, 0))] + [
            per_request(layer + 1, dims) for dims in state_dims],
        out_shape=[jax.ShapeDtypeStruct((N_ROWS, D_MODEL), F32)] + [
            jax.ShapeDtypeStruct((BATCH, layer + 1) + dims, F32) for dims in state_dims],
        input_output_aliases={len(prev): 0},
        scratch_shapes=[pltpu.VMEM((D_MODEL, IN_WIDTH), BF16), pltpu.VMEM((D_MODEL, D_MODEL), BF16),
                        pltpu.VMEM((SEQ, IN_WIDTH), F32), pltpu.VMEM((SEQ, D_MODEL), BF16),
                        pltpu.VMEM((4, GROUP_W, GROUP_W), BF16)] + [scr] * 8,
        compiler_params=_cparams("arbitrary"),
        name="ctx_mixers",
    )(*prev, x, norm2_g, mod, mod, mod, w_in, w_out,
      a_qn, a_kn, b_qn, b_kn, a_sink, c_conv_w, c_conv_b, c_wa, c_ba, c_wx, c_bx,
      c_lambda, d_theta, d_norm_g)
    return out[0], tuple(out[1:])


LAT_BLOCK0 = N_CTX_ROWS // DEC_SEQ


def _lat_attn_kernel(x_ref, n2_ref, sh_ref, sc_ref, g2_ref, win_ref, wout_ref,
                     kca_ref, vca_ref, kcb_ref, vcb_ref,
                     aqn_ref, akn_ref, bqn_ref, bkn_ref, sink_ref, cos_ref, sinl_ref, sinh_ref,
                     xn_ref, h_ref, u_ref, o_ref, *, layer):
    t = DEC_SEQ
    lrow = slice(layer, layer + 1)
    mrow = pl.ds(1 + pl.program_id(0), 1)
    cos, sin_lo, sin_hi = cos_ref[...], sinl_ref[...], sinh_ref[...]
    scale = HEAD_DIM ** -0.5
    x = x_ref[...]
    h_ref[...] = _norm_mod(x, n2_ref[lrow, :], sc_ref[mrow, :], sh_ref[mrow, :]).astype(BF16)
    u_ref[...] = _dot(h_ref[...], win_ref[...].astype(BF16))

    q = _rope(_head_norm(u_ref[:, COL_AQ:COL_AQ + GROUP_W], aqn_ref[lrow, :]), cos, sin_lo, sin_hi)
    k = _rope(_head_norm(u_ref[:, COL_AK:COL_AK + KV_W], akn_ref[lrow, :]), cos, sin_lo, sin_hi)
    qh = [_head_cols(q * scale, h) for h in range(4)]
    v = u_ref[:, COL_AV:COL_AV + KV_W]
    kh = [_head_cols(k, kv) for kv in range(2)]
    vh = [_head_cols(v, kv) for kv in range(2)]
    kch = [_head_cols(kca_ref[...], kv) for kv in range(2)]
    vch = [_head_cols(vca_ref[...], kv) for kv in range(2)]
    w = ATT_BLOCK
    span = 3 * w
    for n in range(t // w):
        start = min(max((n - 1) * w, 0), t - span)
        rows = slice(n * w, (n + 1) * w)
        band = slice(start, start + span)
        qpos = (lax.broadcasted_iota(jnp.int32, (2 * w, span), 0) & (w - 1)) + n * w
        kpos = lax.broadcasted_iota(jnp.int32, (2 * w, span), 1) + start
        valid = jnp.abs(qpos - kpos) <= WINDOW
        heads = []
        for kv in range(2):
            qp = jnp.concatenate([qh[2 * kv][rows, :], qh[2 * kv + 1][rows, :]], axis=0)
            s_ctx = _dot_nt(qp, kch[kv])
            s_band = jnp.where(valid, _dot_nt(qp, kh[kv][band, :]), NEG_INF)
            row = lax.broadcasted_iota(jnp.int32, (2 * w, 1), 0)
            sink = jnp.where(row < w, sink_ref[layer, 2 * kv], sink_ref[layer, 2 * kv + 1])
            o = _softmax_pv([s_ctx, s_band], [vch[kv], vh[kv][band, :]], sink)
            heads += [o[0:w, :], o[w:2 * w, :]]
        o_ref[rows, 0:GROUP_W] = jnp.concatenate(heads, axis=-1).astype(BF16)

    q = _rope(_head_norm(u_ref[:, COL_BQ:COL_BQ + GROUP_W], bqn_ref[lrow, :]), cos, sin_lo, sin_hi)
    k = _rope(_head_norm(u_ref[:, COL_BK:COL_BK + KV_W], bkn_ref[lrow, :]), cos, sin_lo, sin_hi)
    qh = [_head_cols(q * scale, h) for h in range(4)]
    v = u_ref[:, COL_BV:COL_BV + KV_W]
    kh = [_head_cols(k, kv) for kv in range(2)]
    vh = [_head_cols(v, kv) for kv in range(2)]
    kch = [_head_cols(kcb_ref[...], kv) for kv in range(2)]
    vch = [_head_cols(vcb_ref[...], kv) for kv in range(2)]
    tq = 2 * ATT_BLOCK
    for n in range(t // tq):
        rows = slice(n * tq, (n + 1) * tq)
        heads = []
        for kv in range(2):
            qp = jnp.concatenate([qh[2 * kv][rows, :], qh[2 * kv + 1][rows, :]], axis=0)
            o = _softmax_pv([_dot_nt(qp, kch[kv]), _dot_nt(qp, kh[kv])], [vch[kv], vh[kv]], None)
            heads += [o[0:tq, :], o[tq:2 * tq, :]]
        o_ref[rows, GROUP_W:2 * GROUP_W] = jnp.concatenate(heads, axis=-1).astype(BF16)

    xn_ref[...] = x + g2_ref[mrow, :] * _dot(o_ref[...], wout_ref[...].astype(BF16))


def _lat_recurrent_kernel(xn_in_ref, h_ref, g2_ref, wc_ref, wqk_ref, wvg_ref, wout_ref, h0_ref,
                          convw_ref, convb_ref, wa_ref, ba_ref, wx_ref, bx_ref, lam_ref,
                          s0_ref, theta_ref, dn_ref,
                          xn_ref, gate_w_ref, af_ref, bf_ref, ab_ref, bb_ref, hf_ref, hb_ref, gel_ref, ret_ref,
                          *, layer):
    lrow = slice(layer, layer + 1)
    mrow = pl.ds(1 + pl.program_id(0), 1)

    @pl.when(pl.program_id(0) == 0)
    def _():
        _store_gate_weights(gate_w_ref, wa_ref, wx_ref)

    h = h_ref[...]
    u = _dot(h, wc_ref[...].astype(BF16))
    _rglru_prepare(u[:, 0:GROUP_W], u[:, GROUP_W:2 * GROUP_W], convw_ref, convb_ref[lrow, :],
                   gate_w_ref, ba_ref, bx_ref, lam_ref, af_ref, bf_ref, ab_ref, bb_ref, gel_ref)
    oc, _, _ = _rglru_finish(h0_ref[0:1, :], h0_ref[1:2, :],
                             af_ref, bf_ref, ab_ref, bb_ref, hf_ref, hb_ref, gel_ref)
    y = _dot(oc.astype(BF16), wout_ref[0:GROUP_W, :].astype(BF16))

    uqk = _dot(h, wqk_ref[...].astype(BF16))
    uvg = _dot(h, wvg_ref[...].astype(BF16))
    masks = _lane_head_masks(GROUP_W)
    lgf, lgb = _log_decays(theta_ref, masks)
    s0 = tuple(_block_diag([s0_ref[d, hd] for hd in range(N_HEADS)]) for d in range(2))
    _retention(uqk[:, 0:GROUP_W], uqk[:, GROUP_W:2 * GROUP_W] * (HEAD_DIM ** -0.5),
               uvg[:, 0:GROUP_W].astype(BF16), s0, lgf, lgb, masks, ret_ref)
    o = ret_ref[...]
    o = o * lax.rsqrt(_head_mean_square(o) + EPS) * dn_ref[lrow, :] * _silu(uvg[:, GROUP_W:2 * GROUP_W])
    y = y + _dot(o.astype(BF16), wout_ref[GROUP_W:2 * GROUP_W, :].astype(BF16))
    xn_ref[...] = xn_in_ref[...] + g2_ref[mrow, :] * y


def _lat_mixers(x, mod, layer, caches, state_c, state_d, rope, norm2_g, w_in, w_out,
                a_qn, a_kn, a_sink, b_qn, b_kn, c_conv_w, c_conv_b, c_wa, c_ba, c_wx, c_bx,
                c_lambda, d_theta, d_norm_g):
    rows = pl.BlockSpec((DEC_SEQ, D_MODEL), lambda b: (LAT_BLOCK0 + b, 0))
    h_rows = pl.BlockSpec((DEC_SEQ, D_MODEL), lambda b: (b, 0))
    cache_spec = pl.BlockSpec((None, None, PAST_LEN, KV_W), lambda b: (b, layer, 0, 0))
    gain = _full((DEPTH, HEAD_DIM))
    table = _once((DEC_SEQ, LANES), lambda b: (0, 0))
    out_shape = jax.ShapeDtypeStruct((N_ROWS, D_MODEL), F32)
    win_cols = lambda w, c: _once((None, D_MODEL, w), lambda b: (layer, 0, c))
    wout_rows = lambda h, r: _once((None, h, D_MODEL), lambda b: (layer, r, 0))

    xn, h = pl.pallas_call(
        functools.partial(_lat_attn_kernel, layer=layer),
        grid=(DEC_BATCH,),
        in_specs=[rows, _full((DEPTH, D_MODEL)),
                  _mod_chunk(layer, 3), _mod_chunk(layer, 4), _mod_chunk(layer, 5),
                  win_cols(4 * GROUP_W, 0), wout_rows(2 * GROUP_W, 0),
                  cache_spec, cache_spec, cache_spec, cache_spec,
                  gain, gain, gain, gain,
                  pl.BlockSpec(memory_space=pltpu.SMEM),
                  table, table, table],
        out_specs=[rows, pl.BlockSpec((DEC_SEQ, D_MODEL), lambda b: (b, 0), pipeline_mode=pl.Buffered(1))],
        out_shape=[out_shape, jax.ShapeDtypeStruct((N_LAT_ROWS, D_MODEL), BF16)],
        input_output_aliases={0: 0},
        scratch_shapes=[pltpu.VMEM((DEC_SEQ, 4 * GROUP_W), F32), pltpu.VMEM((DEC_SEQ, 2 * GROUP_W), BF16)],
        compiler_params=_cparams("arbitrary"),
        name="lat_attention",
    )(x, norm2_g, mod, mod, mod, w_in, w_out, *caches, a_qn, a_kn, b_qn, b_kn, a_sink, *rope)

    scr = pltpu.VMEM((DEC_SEQ, GROUP_W), F32)
    xn = pl.pallas_call(
        functools.partial(_lat_recurrent_kernel, layer=layer),
        grid=(DEC_BATCH,),
        in_specs=[
            rows, h_rows, _mod_chunk(layer, 5),
            win_cols(2 * GROUP_W, COL_CX // (2 * GROUP_W)),
            win_cols(2 * GROUP_W, COL_DQ // (2 * GROUP_W)), win_cols(2 * GROUP_W, COL_DV // (2 * GROUP_W)),
            wout_rows(2 * GROUP_W, 1),
            pl.BlockSpec((None, None, 2, GROUP_W), lambda b: (b, layer, 0, 0)),
            _layer_block((4, GROUP_W), layer), _full((DEPTH, GROUP_W)),
            _layer_block((2, N_HEADS, HEAD_DIM, HEAD_DIM), layer), _layer_block((2, GROUP_W), layer),
            _layer_block((2, N_HEADS, HEAD_DIM, HEAD_DIM), layer), _layer_block((2, GROUP_W), layer),
            _layer_block((2, GROUP_W), layer),
            pl.BlockSpec((None, None, 2, N_HEADS, HEAD_DIM, HEAD_DIM), lambda b: (b, layer, 0, 0, 0, 0)),
            _layer_block((2, N_HEADS), layer), _full((DEPTH, GROUP_W))],
        out_specs=rows,
        out_shape=out_shape,
        input_output_aliases={0: 0},
        scratch_shapes=[pltpu.VMEM((4, GROUP_W, GROUP_W), BF16)] + [scr] * 8,
        compiler_params=_cparams("arbitrary"),
        name="lat_recurrent",
    )(xn, h, mod, w_in, w_in, w_in, w_out, state_c, c_conv_w, c_conv_b, c_wa, c_ba, c_wx, c_bx, c_lambda,
      state_d, d_theta, d_norm_g)
    return xn


def _rope_tables():
    t = np.arange(DEC_SEQ)
    row = (t // GRID_W).astype(np.float64)[:, None]
    col = (t % GRID_W).astype(np.float64)[:, None]
    half = HEAD_DIM // 2
    inv = 1.0 / (ROPE_BASE ** (np.arange(0, half, 2, dtype=np.float64) / half))
    j = np.arange(LANES) % HEAD_DIM
    ang = np.where((j < half)[None, :], row, col) * inv[j % (half // 2)][None, :]
    first = ((j % half) < half // 2)[None, :]
    cos, sin = np.cos(ang), np.sin(ang)
    return tuple(jnp.asarray(a, F32) for a in (cos, np.where(first, -sin, 0.0), np.where(first, 0.0, sin)))


def kernel(x_prompt, x_sample, cache_a_k, cache_a_v, cache_b_k, cache_b_v, state_c, state_d, c, c_ctx, norm1_g, norm2_g, norm3_g, w_mod, b_mod, ffn1_wg, ffn1_wu, ffn1_wd, ffn2_wg, ffn2_wu, ffn2_wd, w_in, w_out, a_qn, a_kn, a_sink, b_qn, b_kn, c_conv_w, c_conv_b, c_wa, c_ba, c_wx, c_bx, c_lambda, d_theta, d_norm_g):
    mod = _modulation(c_ctx, c, w_mod, b_mod)
    rope = _rope_tables()
    caches = tuple(t.reshape(DEC_BATCH, DEPTH, PAST_LEN, KV_W) for t in (cache_a_k, cache_a_v, cache_b_k, cache_b_v))
    mixer_params = (a_qn, a_kn, a_sink, b_qn, b_kn, c_conv_w, c_conv_b, c_wa, c_ba, c_wx, c_bx,
                    c_lambda, d_theta, d_norm_g)
    xs = (x_prompt.reshape(N_CTX_ROWS, D_MODEL), x_sample.reshape(N_LAT_ROWS, D_MODEL))
    states = ()
    for l in range(DEPTH):
        (x,) = _ffn(xs, mod, l, 0, norm1_g, ffn1_wg, ffn1_wu, ffn1_wd)
        x, states = _ctx_mixers(x, mod, l, states, norm2_g, w_in, w_out, *mixer_params)
        x = _lat_mixers(x, mod, l, caches, state_c, state_d, rope, norm2_g, w_in, w_out, *mixer_params)
        xs = _ffn((x,), mod, l, 6, norm3_g, ffn2_wg, ffn2_wu, ffn2_wd, split_out=(l == DEPTH - 1))
    y_p, y_s = xs
    ka, va, kb, vb, st_c, st_d = states
    kv_shape = (BATCH, DEPTH, SEQ, 2, HEAD_DIM)
    return (y_p.reshape(BATCH, SEQ, D_MODEL), y_s.reshape(DEC_BATCH, DEC_SEQ, D_MODEL),
            ka.reshape(kv_shape), va.reshape(kv_shape), kb.reshape(kv_shape), vb.reshape(kv_shape),
            st_c, st_d)
```

```python
import functools
import math

import numpy as np
import jax
import jax.numpy as jnp
from jax import lax
from jax.experimental import pallas as pl
from jax.experimental.pallas import tpu as pltpu

F32 = jnp.float32
BF16 = jnp.bfloat16

D_MODEL = 1024
BATCH = 16
SEQ = 256
DEPTH = 2
DEC_BATCH = 2
DEC_SEQ = 1024
PAST_LEN = 512
GRID_W = 64
HEAD_DIM = 64
HEAD_SHIFT = 6
N_HEADS = 4
GROUP_W = 256
KV_W = 2 * HEAD_DIM
LANES = 128
WINDOW = 128
ATT_BLOCK = 128
ROPE_BASE = 10000.0
LRU_C = 8.0
D_FF = 2816
N_MOD = 9
EPS = 1e-6
NEG_INF = -1e30
IN_WIDTH = 2560

N_CTX_ROWS = BATCH * SEQ
N_LAT_ROWS = DEC_BATCH * DEC_SEQ
N_ROWS = N_CTX_ROWS + N_LAT_ROWS
MOD_ROWS = 8
MOD_GROUP = 1024

VMEM_LIMIT_BYTES = 56 * 1024 * 1024

COL_AQ, COL_AK, COL_AV = 0, 256, 384
COL_BQ, COL_BK, COL_BV = 512, 768, 896
COL_CX, COL_CY = 1024, 1280
COL_DQ, COL_DK, COL_DV, COL_DG = 1536, 1792, 2048, 2304


def _cparams(*sem):
    return pltpu.CompilerParams(dimension_semantics=sem, vmem_limit_bytes=VMEM_LIMIT_BYTES)


def _dot(a, b):
    return jnp.dot(a, b, preferred_element_type=F32)


def _dot_nt(a, b):
    return lax.dot_general(a, b, (((1,), (1,)), ((), ())), preferred_element_type=F32)


def _dot_tn(a, b):
    return lax.dot_general(a, b, (((0,), (0,)), ((), ())), preferred_element_type=F32)


def _sigmoid(x):
    return 0.5 * jnp.tanh(0.5 * x) + 0.5


def _silu(x):
    return x * _sigmoid(x)


def _gelu_tanh(x):
    return 0.5 * x * (1.0 + jnp.tanh(math.sqrt(2.0 / math.pi) * (x + 0.044715 * (x * x * x))))


def _mod_row(i, tm, s):
    if tm >= MOD_GROUP:
        block_index = i * (tm // MOD_GROUP) + s
    else:
        block_index = i >> int(math.log2(MOD_GROUP // tm))
    return jnp.maximum(block_index - (N_CTX_ROWS // MOD_GROUP - 1), 0)


def _norm_mod(x, g, sc, sh):
    ms = jnp.mean(x * x, axis=-1, keepdims=True)
    return (x * lax.rsqrt(ms + EPS) * g) * (1.0 + sc) + sh


def _full(shape):
    return pl.BlockSpec(shape, lambda *_: (0,) * len(shape))


def _layer_block(shape, layer):
    return pl.BlockSpec((None,) + shape, lambda *_: (layer,) + (0,) * len(shape))


MOD_TN = 3072


def _mod_kernel(cc_ref, c_ref, w_ref, b_ref, o_ref):
    l = pl.program_id(0)
    pad = jnp.zeros((MOD_ROWS - 1 - DEC_BATCH, D_MODEL), F32)
    cond = jnp.concatenate([cc_ref[...], c_ref[...], pad], axis=0)
    o_ref[...] = _dot(_silu(cond).astype(BF16), w_ref[...].astype(BF16)) + b_ref[pl.ds(l, 1), :]


def _modulation(c_ctx, c, w_mod, b_mod):
    n = N_MOD * D_MODEL
    return pl.pallas_call(
        _mod_kernel,
        grid=(DEPTH, n // MOD_TN),
        in_specs=[
            pl.BlockSpec((1, D_MODEL), lambda l, j: (0, 0)),
            pl.BlockSpec((DEC_BATCH, D_MODEL), lambda l, j: (0, 0)),
            pl.BlockSpec((None, D_MODEL, MOD_TN), lambda l, j: (l, 0, j)),
            pl.BlockSpec((DEPTH, MOD_TN), lambda l, j: (0, j)),
        ],
        out_specs=pl.BlockSpec((None, MOD_ROWS, MOD_TN), lambda l, j: (l, 0, j)),
        out_shape=jax.ShapeDtypeStruct((DEPTH, MOD_ROWS, n), F32),
        compiler_params=_cparams("arbitrary", "arbitrary"),
        name="modulation",
    )(c_ctx.reshape(1, D_MODEL), c, w_mod, b_mod)


FFN_TM = 1024
FFN_TF = 256
N_CTX_TILES = N_CTX_ROWS // FFN_TM


FFN_NJ = D_FF // FFN_TF
N_FFN_TILES = N_ROWS // FFN_TM
N_FFN_STEPS = FFN_NJ + N_FFN_TILES


def _ffn_tile(step):
    return jnp.maximum(step - FFN_NJ, 0)


def _on_stream_part(tile, x_refs, o_refs, fn):
    if len(x_refs) == 1 and len(o_refs) == 1:
        fn(x_refs[0], o_refs[0])
    else:
        pl.when(tile < N_CTX_TILES)(lambda: fn(x_refs[0], o_refs[0]))
        pl.when(tile >= N_CTX_TILES)(lambda: fn(x_refs[-1], o_refs[-1]))


def _ffn_kernel(*refs, layer, n_in, n_out):
    x_refs = refs[:n_in]
    n_ref, sh_ref, sc_ref, g_ref, wg_ref, wu_ref, wd_ref = refs[n_in:n_in + 7]
    o_refs = refs[n_in + 7:n_in + 7 + n_out]
    h_ref, a_ref, wg_s, wu_s, wd_s = refs[n_in + 7 + n_out:]
    nj, tf = FFN_NJ, FFN_TF
    s = pl.program_id(0)
    tile = _ffn_tile(s)
    r = _mod_row(tile, FFN_TM, 0)

    def load_tile():
        def init(x_ref, _):
            h = _norm_mod(x_ref[...], n_ref[layer:layer + 1, :], sc_ref[pl.ds(r, 1), :], sh_ref[pl.ds(r, 1), :])
            h_ref[...] = h.astype(BF16)
        _on_stream_part(tile, x_refs, x_refs, init)

    def up_chunk(j, cols):
        h = h_ref[...]
        a_ref[:, cols] = (_silu(_dot(h, wg_s[j])) * _dot(h, wu_s[j])).astype(BF16)

    def down_and_store():
        y = (0.5 * g_ref[pl.ds(r, 1), :]) * _dot(a_ref[...], wd_s[...])

        def store(x_ref, o_ref):
            o_ref[...] = x_ref[...] + y
        _on_stream_part(tile, x_refs, o_refs, store)

    def keep_arrived_chunk():
        wg_s[s] = wg_ref[...].astype(BF16)
        wu_s[s] = wu_ref[...].astype(BF16)
        wd_s[pl.ds(pl.multiple_of(s * tf, tf), tf), :] = wd_ref[...].astype(BF16)

    def up_previous_chunk():
        up_chunk(s - 1, pl.ds(pl.multiple_of((s - 1) * tf, tf), tf))

    @pl.when(s == 0)
    def _():
        load_tile()
        keep_arrived_chunk()

    @pl.when((s > 0) & (s < nj))
    def _():
        up_previous_chunk()
        keep_arrived_chunk()

    @pl.when(s == nj)
    def _():
        up_previous_chunk()
        down_and_store()

    @pl.when(s > nj)
    def _():
        load_tile()
        for j in range(nj):
            up_chunk(j, slice(j * tf, (j + 1) * tf))
        down_and_store()


def _stream_specs(split, buffered_once):
    tm = FFN_TM
    kw = {"pipeline_mode": pl.Buffered(1)} if buffered_once else {}
    if not split:
        return [pl.BlockSpec((tm, D_MODEL), lambda s: (_ffn_tile(s), 0), **kw)]
    last_ctx = N_CTX_TILES - 1
    return [pl.BlockSpec((tm, D_MODEL), lambda s: (jnp.minimum(_ffn_tile(s), last_ctx), 0), **kw),
            pl.BlockSpec((tm, D_MODEL), lambda s: (jnp.maximum(_ffn_tile(s) - N_CTX_TILES, 0), 0), **kw)]


def _ffn(xs, mod, layer, chunk0, norm_g, wg, wu, wd, split_out=False):
    tm, tf, nj = FFN_TM, FFN_TF, FFN_NJ
    split_in = len(xs) == 2
    mod_spec = lambda c: pl.BlockSpec((None, MOD_ROWS, D_MODEL), lambda s: (layer, 0, c))
    w_col = lambda s: (layer, 0, jnp.minimum(s, nj - 1))
    w_row = lambda s: (layer, jnp.minimum(s, nj - 1), 0)
    if split_out:
        out_shape = [jax.ShapeDtypeStruct((N_CTX_ROWS, D_MODEL), F32),
                     jax.ShapeDtypeStruct((N_LAT_ROWS, D_MODEL), F32)]
    else:
        out_shape = [jax.ShapeDtypeStruct((N_ROWS, D_MODEL), F32)]
    out = pl.pallas_call(
        functools.partial(_ffn_kernel, layer=layer, n_in=len(xs), n_out=len(out_shape)),
        grid=(N_FFN_STEPS,),
        in_specs=_stream_specs(split_in, False) + [
            _full((DEPTH, D_MODEL)),
            mod_spec(chunk0), mod_spec(chunk0 + 1), mod_spec(chunk0 + 2),
            pl.BlockSpec((None, D_MODEL, tf), w_col),
            pl.BlockSpec((None, D_MODEL, tf), w_col),
            pl.BlockSpec((None, tf, D_MODEL), w_row),
        ],
        out_specs=_stream_specs(split_out, True),
        out_shape=out_shape,
        scratch_shapes=[pltpu.VMEM((tm, D_MODEL), BF16),
                        pltpu.VMEM((tm, D_FF), BF16),
                        pltpu.VMEM((nj, D_MODEL, tf), BF16),
                        pltpu.VMEM((nj, D_MODEL, tf), BF16),
                        pltpu.VMEM((D_FF, D_MODEL), BF16)],
        compiler_params=_cparams("arbitrary"),
        name="ffn",
    )(*xs, norm_g, mod, mod, mod, wg, wu, wd)
    return tuple(out)


def _once(shape, index_map):
    return pl.BlockSpec(shape, index_map, pipeline_mode=pl.Buffered(1))


def _mod_chunk(layer, c):
    return pl.BlockSpec((None, MOD_ROWS, D_MODEL), lambda *_: (layer, 0, c))


def _head_mean_square(x):
    n = x.shape[-1]
    r = lax.broadcasted_iota(jnp.int32, (n, n), 0) >> HEAD_SHIFT
    c = lax.broadcasted_iota(jnp.int32, (n, n), 1) >> HEAD_SHIFT
    ones_bd = jnp.where(r == c, 1.0, 0.0).astype(BF16)
    return _dot((x * x).astype(BF16), ones_bd) * (1.0 / HEAD_DIM)


def _head_norm(x, head_gain):
    gain_row = jnp.concatenate([head_gain] * (x.shape[-1] // HEAD_DIM), axis=-1)
    return x * lax.rsqrt(_head_mean_square(x) + EPS) * gain_row


def _head_cols(x, h):
    return x[:, h * HEAD_DIM:(h + 1) * HEAD_DIM].astype(BF16)


def _softmax_pv(scores, values, sink):
    m = jnp.max(scores[0], axis=-1, keepdims=True)
    for s in scores[1:]:
        m = jnp.maximum(m, jnp.max(s, axis=-1, keepdims=True))
    if sink is not None:
        m = jnp.maximum(m, sink)
    denom = None
    acc = None
    for s, v in zip(scores, values):
        p = jnp.exp(s - m)
        d = jnp.sum(p, axis=-1, keepdims=True)
        o = _dot(p.astype(BF16), v)
        denom = d if denom is None else denom + d
        acc = o if acc is None else acc + o
    if sink is not None:
        denom = denom + jnp.exp(sink - m)
    return acc / denom


def _rope(x, cos, sin_lo, sin_hi):
    cols = []
    for c in range(x.shape[-1] // LANES):
        xc = x[:, c * LANES:(c + 1) * LANES]
        cols.append(xc * cos + pltpu.roll(xc, 112, 1) * sin_lo + pltpu.roll(xc, 16, 1) * sin_hi)
    return cols[0] if len(cols) == 1 else jnp.concatenate(cols, axis=-1)


def _block_diag(blocks):
    n = len(blocks)
    w = blocks[0].shape[0]
    rows = []
    for k, blk in enumerate(blocks):
        parts = []
        if k > 0:
            parts.append(jnp.zeros((w, k * w), F32))
        parts.append(blk)
        if k < n - 1:
            parts.append(jnp.zeros((w, (n - 1 - k) * w), F32))
        rows.append(jnp.concatenate(parts, axis=-1))
    return jnp.concatenate(rows, axis=0)


def _rglru_gates(xc, wa, ba, wx, bx, lam):
    xb = xc.astype(BF16)
    r = _sigmoid(_dot(xb, wa) + ba)
    i = _sigmoid(_dot(xb, wx) + bx)
    softplus = jnp.maximum(-lam, 0.0) + jnp.log1p(jnp.exp(-jnp.abs(lam)))
    log_a = (-LRU_C) * r * softplus
    a = jnp.exp(log_a)
    b = jnp.sqrt(1.0 - a * a) * (i * xc)
    return a, b


def _block_prefix(a, b, reverse):
    t = a.shape[0]
    row = lax.broadcasted_iota(jnp.int32, a.shape, 0) & 7
    for d in (1, 2, 4):
        if reverse:
            a_s = pltpu.roll(a, t - d, 0)
            b_s = pltpu.roll(b, t - d, 0)
            ok = row < 8 - d
        else:
            a_s = pltpu.roll(a, d, 0)
            b_s = pltpu.roll(b, d, 0)
            ok = row >= d
        b = jnp.where(ok, a * b_s + b, b)
        a = jnp.where(ok, a * a_s, a)
    return a, b


def _conv4(x, w_ref, b_row):
    t = x.shape[0]
    row = lax.broadcasted_iota(jnp.int32, x.shape, 0)
    xm2 = jnp.where(row >= 2, pltpu.roll(x, 2, 0), 0.0)
    xm1 = jnp.where(row >= 1, pltpu.roll(x, 1, 0), 0.0)
    xp1 = jnp.where(row < t - 1, pltpu.roll(x, t - 1, 0), 0.0)
    return (xm2 * w_ref[0:1, :] + xm1 * w_ref[1:2, :] + x * w_ref[2:3, :] + xp1 * w_ref[3:4, :]) + b_row


def _rglru_prepare(cx, cy, conv_w_ref, conv_b, gate_w_ref, ba_ref, bx_ref, lam_ref,
                   af_ref, bf_ref, ab_ref, bb_ref, gel_ref):
    xc = _conv4(cx, conv_w_ref, conv_b)
    a, b = _rglru_gates(xc, gate_w_ref[0], ba_ref[0:1, :], gate_w_ref[1], bx_ref[0:1, :], lam_ref[0:1, :])
    a, b = _block_prefix(a, b, reverse=False)
    af_ref[...] = a
    bf_ref[...] = b
    a, b = _rglru_gates(xc, gate_w_ref[2], ba_ref[1:2, :], gate_w_ref[3], bx_ref[1:2, :], lam_ref[1:2, :])
    a, b = _block_prefix(a, b, reverse=True)
    ab_ref[...] = a
    bb_ref[...] = b
    gel_ref[...] = _gelu_tanh(cy)


SCAN_UNROLL = 16


def _rglru_finish(h0f, h0b, af_ref, bf_ref, ab_ref, bb_ref, hf_ref, hb_ref, gel_ref):
    nblk = af_ref.shape[0] // 8

    def body(k, carry):
        cf, cb = carry
        rf = pl.ds(pl.multiple_of(k * 8, 8), 8)
        hf = bf_ref[rf, :] + af_ref[rf, :] * cf
        hf_ref[rf, :] = hf
        rb = pl.ds(pl.multiple_of((nblk - 1 - k) * 8, 8), 8)
        hb = bb_ref[rb, :] + ab_ref[rb, :] * cb
        hb_ref[rb, :] = hb
        return hf[7:8, :], hb[0:1, :]

    cf, cb = lax.fori_loop(0, nblk, body, (h0f, h0b), unroll=SCAN_UNROLL)
    oc = (hf_ref[...] + hb_ref[...]) * gel_ref[...]
    return oc, cf, cb


def _store_gate_weights(gate_w_ref, wa_ref, wx_ref):
    for d in range(2):
        gate_w_ref[2 * d] = _block_diag([wa_ref[d, n] for n in range(N_HEADS)]).astype(BF16)
        gate_w_ref[2 * d + 1] = _block_diag([wx_ref[d, n] for n in range(N_HEADS)]).astype(BF16)


def _lane_head_masks(n):
    lane = lax.broadcasted_iota(jnp.int32, (1, n), 1) >> HEAD_SHIFT
    return [jnp.where(lane == h, 1.0, 0.0) for h in range(n // HEAD_DIM)]


def _log_decays(theta_ref, masks):
    theta = theta_ref[...]
    lanes = theta[:, 0:1] * masks[0]
    for h in range(1, N_HEADS):
        lanes = lanes + theta[:, h:h + 1] * masks[h]
    lg = jnp.log1p(-jnp.exp(lanes))
    return lg[0:1, :], lg[1:2, :]


RET_BLOCK = 256


def _retention(q, k8, vb, s0, lgf, lgb, masks, o_ref):
    t, w = q.shape
    c = RET_BLOCK
    nh = w // HEAD_DIM
    pos = lax.broadcasted_iota(jnp.int32, (c, w), 0).astype(F32)
    q_dec = (jnp.exp(lgf * (pos + 1.0)), jnp.exp(lgb * (float(c) - pos)))
    k_dec = (jnp.exp(lgf * (float(c - 1) - pos)), jnp.exp(lgb * pos))
    chunk_dec = (jnp.exp(lgf * float(c)), jnp.exp(lgb * float(c)))
    rel = (lax.broadcasted_iota(jnp.int32, (c, c), 0) - lax.broadcasted_iota(jnp.int32, (c, c), 1)).astype(F32)
    decs = []
    for h in range(nh):
        gf = lgf[:, h * HEAD_DIM:h * HEAD_DIM + 1]
        gb = lgb[:, h * HEAD_DIM:h * HEAD_DIM + 1]
        e = jnp.exp(jnp.where(rel >= 0, gf * rel, gb * (-rel)))
        decs.append(jnp.where(rel == 0, 2.0, e))
    dec = jnp.concatenate(decs, axis=0)
    r_head = lax.broadcasted_iota(jnp.int32, (w, w), 0) >> HEAD_SHIFT
    c_head = lax.broadcasted_iota(jnp.int32, (w, w), 1) >> HEAD_SHIFT
    same_head = jnp.where(r_head == c_head, 1.0, 0.0)
    states = [None, None] if s0 is None else list(s0)

    def carry(d, rows, o):
        if states[d] is not None:
            o = o + _dot((q[rows, :] * q_dec[d]).astype(BF16), states[d].astype(BF16))
        upd = _dot_tn((k8[rows, :] * k_dec[d]).astype(BF16), vb[rows, :]) * same_head
        states[d] = upd if states[d] is None else states[d] * chunk_dec[d] + upd
        return o

    for ci in range(t // c):
        rows = slice(ci * c, (ci + 1) * c)
        qc = q[rows, :]
        q_stack = jnp.concatenate([(qc * masks[h]).astype(BF16) for h in range(nh)], axis=0)
        inner = (_dot_nt(q_stack, k8[rows, :].astype(BF16)) * dec).astype(BF16)
        out = _dot(inner, vb[rows, :])
        o = out[0:c, :] * masks[0]
        for h in range(1, nh):
            o = o + out[h * c:(h + 1) * c, :] * masks[h]
        o_ref[rows, :] = carry(0, rows, o)
    for ci in reversed(range(t // c)):
        rows = slice(ci * c, (ci + 1) * c)
        if states[1] is not None:
            o_ref[rows, :] = carry(1, rows, o_ref[rows, :])
        else:
            carry(1, rows, None)
    return states[0], states[1]


def _ctx_mixer_kernel(*refs, layer, n_prev):
    prev_refs = refs[:n_prev]
    (x_ref, n2_ref, sh_ref, sc_ref, g2_ref, win_ref, wout_ref,
     aqn_ref, akn_ref, bqn_ref, bkn_ref, sink_ref,
     convw_ref, convb_ref, wa_ref, ba_ref, wx_ref, bx_ref, lam_ref, theta_ref, dn_ref,
     xn_ref, *state_refs) = refs[n_prev:n_prev + 28]
    (win_s, wout_s, u_ref, mixed_ref,
     gate_w_ref, af_ref, bf_ref, ab_ref, bb_ref, hf_ref, hb_ref, gel_ref, ret_ref) = refs[n_prev + 28:]
    t = SEQ
    lrow = slice(layer, layer + 1)
    for prev_ref, state_ref in zip(prev_refs, state_refs):
        for earlier in range(layer):
            state_ref[earlier] = prev_ref[earlier]
    ka_ref, va_ref, kb_ref, vb_ref, stc_ref, std_ref = (ref.at[layer] for ref in state_refs)

    @pl.when(pl.program_id(0) == 0)
    def _():
        for c0 in range(0, IN_WIDTH, 2 * GROUP_W):
            win_s[:, c0:c0 + 2 * GROUP_W] = win_ref[:, c0:c0 + 2 * GROUP_W].astype(BF16)
        wout_s[...] = wout_ref[...].astype(BF16)
        _store_gate_weights(gate_w_ref, wa_ref, wx_ref)

    x = x_ref[...]
    h = _norm_mod(x, n2_ref[lrow, :], sc_ref[0:1, :], sh_ref[0:1, :]).astype(BF16)
    c_cols = slice(COL_CX, COL_CX + 2 * GROUP_W)
    u_ref[:, c_cols] = _dot(h, win_s[:, c_cols])
    _rglru_prepare(u_ref[:, COL_CX:COL_CX + GROUP_W], u_ref[:, COL_CY:COL_CY + GROUP_W],
                   convw_ref, convb_ref[lrow, :], gate_w_ref, ba_ref, bx_ref, lam_ref,
                   af_ref, bf_ref, ab_ref, bb_ref, gel_ref)
    u_ref[:, 0:COL_CX] = _dot(h, win_s[:, 0:COL_CX])
    u_ref[:, COL_DQ:IN_WIDTH] = _dot(h, win_s[:, COL_DQ:IN_WIDTH])

    for (cq, ck, cv, qn_ref, kn_ref, k_out, v_out, col0, use_sink) in (
            (COL_AQ, COL_AK, COL_AV, aqn_ref, akn_ref, ka_ref, va_ref, 0, True),
            (COL_BQ, COL_BK, COL_BV, bqn_ref, bkn_ref, kb_ref, vb_ref, GROUP_W, False)):
        q = _head_norm(u_ref[:, cq:cq + GROUP_W], qn_ref[lrow, :])
        k = _head_norm(u_ref[:, ck:ck + KV_W], kn_ref[lrow, :])
        v = u_ref[:, cv:cv + KV_W]
        k_out[...] = k
        v_out[...] = v
        qs = q * (HEAD_DIM ** -0.5)
        heads = []
        for hd in range(N_HEADS):
            kv = hd // 2
            s = _dot_nt(_head_cols(qs, hd), _head_cols(k, kv))
            sink = jnp.full((t, 1), sink_ref[layer, hd], F32) if use_sink else None
            heads.append(_softmax_pv([s], [_head_cols(v, kv)], sink))
        mixed_ref[:, col0:col0 + GROUP_W] = jnp.concatenate(heads, axis=-1).astype(BF16)

    zero = jnp.zeros((1, GROUP_W), F32)
    oc, cf, cb = _rglru_finish(zero, zero, af_ref, bf_ref, ab_ref, bb_ref, hf_ref, hb_ref, gel_ref)
    mixed_ref[:, 2 * GROUP_W:3 * GROUP_W] = oc.astype(BF16)
    stc_ref[0:1, :] = cf
    stc_ref[1:2, :] = cb

    masks = _lane_head_masks(GROUP_W)
    lgf, lgb = _log_decays(theta_ref, masks)
    k8 = u_ref[:, COL_DK:COL_DK + GROUP_W] * (HEAD_DIM ** -0.5)
    vb = u_ref[:, COL_DV:COL_DV + GROUP_W].astype(BF16)
    final_states = _retention(u_ref[:, COL_DQ:COL_DQ + GROUP_W], k8, vb, None, lgf, lgb, masks, ret_ref)
    o = ret_ref[...]
    o = o * lax.rsqrt(_head_mean_square(o) + EPS) * dn_ref[lrow, :] * _silu(u_ref[:, COL_DG:COL_DG + GROUP_W])
    mixed_ref[:, 3 * GROUP_W:4 * GROUP_W] = o.astype(BF16)
    for d, s_full in enumerate(final_states):
        for hd in range(N_HEADS):
            std_ref[d, hd] = s_full[hd * HEAD_DIM:(hd + 1) * HEAD_DIM, hd * HEAD_DIM:(hd + 1) * HEAD_DIM]

    xn_ref[...] = x + g2_ref[0:1, :] * _dot(mixed_ref[...], wout_s[...])


def _ctx_mixers(x, mod, layer, prev, norm2_g, w_in, w_out,
                a_qn, a_kn, a_sink, b_qn, b_kn, c_conv_w, c_conv_b, c_wa, c_ba, c_wx, c_bx,
                c_lambda, d_theta, d_norm_g):
    per_request = lambda slots, shape: pl.BlockSpec((None, slots) + shape, lambda b: (b,) + (0,) * (1 + len(shape)))
    state_dims = [(SEQ, KV_W)] * 4 + [(2, GROUP_W), (2, N_HEADS, HEAD_DIM, HEAD_DIM)]
    scr = pltpu.VMEM((SEQ, GROUP_W), F32)
    out = pl.pallas_call(
        functools.partial(_ctx_mixer_kernel, layer=layer, n_prev=len(prev)),
        grid=(BATCH,),
        in_specs=[per_request(layer, dims) for dims in state_dims[:len(prev)]] + [
            pl.BlockSpec((SEQ, D_MODEL), lambda b: (b, 0)),
            _full((DEPTH, D_MODEL)),
            _mod_chunk(layer, 3), _mod_chunk(layer, 4), _mod_chunk(layer, 5),
            _once((None, D_MODEL, IN_WIDTH), lambda b: (layer, 0, 0)),
            _once((None, D_MODEL, D_MODEL), lambda b: (layer, 0, 0)),
            _full((DEPTH, HEAD_DIM)), _full((DEPTH, HEAD_DIM)), _full((DEPTH, HEAD_DIM)), _full((DEPTH, HEAD_DIM)),
            pl.BlockSpec(memory_space=pltpu.SMEM),
            _layer_block((4, GROUP_W), layer), _full((DEPTH, GROUP_W)),
            _layer_block((2, N_HEADS, HEAD_DIM, HEAD_DIM), layer), _layer_block((2, GROUP_W), layer),
            _layer_block((2, N_HEADS, HEAD_DIM, HEAD_DIM), layer), _layer_block((2, GROUP_W), layer),
            _layer_block((2, GROUP_W), layer),
            _layer_block((2, N_HEADS), layer), _full((DEPTH, GROUP_W)),
        ],
        out_specs=[pl.BlockSpec((SEQ, D_MODEL), lambda b: (b, 0))] + [
            per_request(layer + 1, dims) for dims in state_dims],
        out_shape=[jax.ShapeDtypeStruct((N_ROWS, D_MODEL), F32)] + [
            jax.ShapeDtypeStruct((BATCH, layer + 1) + dims, F32) for dims in state_dims],
        input_output_aliases={len(prev): 0},
        scratch_shapes=[pltpu.VMEM((D_MODEL, IN_WIDTH), BF16), pltpu.VMEM((D_MODEL, D_MODEL), BF16),
                        pltpu.VMEM((SEQ, IN_WIDTH), F32), pltpu.VMEM((SEQ, D_MODEL), BF16),
                        pltpu.VMEM((4, GROUP_W, GROUP_W), BF16)] + [scr] * 8,
        compiler_params=_cparams("arbitrary"),
        name="ctx_mixers",
    )(*prev, x, norm2_g, mod, mod, mod, w_in, w_out,
      a_qn, a_kn, b_qn, b_kn, a_sink, c_conv_w, c_conv_b, c_wa, c_ba, c_wx, c_bx,
      c_lambda, d_theta, d_norm_g)
    return out[0], tuple(out[1:])


LAT_BLOCK0 = N_CTX_ROWS // DEC_SEQ


def _lat_attn_kernel(x_ref, n2_ref, sh_ref, sc_ref, g2_ref, win_ref, wout_ref,
                     kca_ref, vca_ref, kcb_ref, vcb_ref,
                     aqn_ref, akn_ref, bqn_ref, bkn_ref, sink_ref, cos_ref, sinl_ref, sinh_ref,
                     xn_ref, h_ref, u_ref, o_ref, *, layer):
    t = DEC_SEQ
    lrow = slice(layer, layer + 1)
    mrow = pl.ds(1 + pl.program_id(0), 1)
    cos, sin_lo, sin_hi = cos_ref[...], sinl_ref[...], sinh_ref[...]
    scale = HEAD_DIM ** -0.5
    x = x_ref[...]
    h_ref[...] = _norm_mod(x, n2_ref[lrow, :], sc_ref[mrow, :], sh_ref[mrow, :]).astype(BF16)
    u_ref[...] = _dot(h_ref[...], win_ref[...].astype(BF16))

    q = _rope(_head_norm(u_ref[:, COL_AQ:COL_AQ + GROUP_W], aqn_ref[lrow, :]), cos, sin_lo, sin_hi)
    k = _rope(_head_norm(u_ref[:, COL_AK:COL_AK + KV_W], akn_ref[lrow, :]), cos, sin_lo, sin_hi)
    qh = [_head_cols(q * scale, h) for h in range(4)]
    v = u_ref[:, COL_AV:COL_AV + KV_W]
    kh = [_head_cols(k, kv) for kv in range(2)]
    vh = [_head_cols(v, kv) for kv in range(2)]
    kch = [_head_cols(kca_ref[...], kv) for kv in range(2)]
    vch = [_head_cols(vca_ref[...], kv) for kv in range(2)]
    w = ATT_BLOCK
    span = 3 * w
    for n in range(t // w):
        start = min(max((n - 1) * w, 0), t - span)
        rows = slice(n * w, (n + 1) * w)
        band = slice(start, start + span)
        qpos = (lax.broadcasted_iota(jnp.int32, (2 * w, span), 0) & (w - 1)) + n * w
        kpos = lax.broadcasted_iota(jnp.int32, (2 * w, span), 1) + start
        valid = jnp.abs(qpos - kpos) <= WINDOW
        heads = []
        for kv in range(2):
            qp = jnp.concatenate([qh[2 * kv][rows, :], qh[2 * kv + 1][rows, :]], axis=0)
            s_ctx = _dot_nt(qp, kch[kv])
            s_band = jnp.where(valid, _dot_nt(qp, kh[kv][band, :]), NEG_INF)
            row = lax.broadcasted_iota(jnp.int32, (2 * w, 1), 0)
            sink = jnp.where(row < w, sink_ref[layer, 2 * kv], sink_ref[layer, 2 * kv + 1])
            o = _softmax_pv([s_ctx, s_band], [vch[kv], vh[kv][band, :]], sink)
            heads += [o[0:w, :], o[w:2 * w, :]]
        o_ref[rows, 0:GROUP_W] = jnp.concatenate(heads, axis=-1).astype(BF16)

    q = _rope(_head_norm(u_ref[:, COL_BQ:COL_BQ + GROUP_W], bqn_ref[lrow, :]), cos, sin_lo, sin_hi)
    k = _rope(_head_norm(u_ref[:, COL_BK:COL_BK + KV_W], bkn_ref[lrow, :]), cos, sin_lo, sin_hi)
    qh = [_head_cols(q * scale, h) for h in range(4)]
    v = u_ref[:, COL_BV:COL_BV + KV_W]
    kh = [_head_cols(k, kv) for kv in range(2)]
    vh = [_head_cols(v, kv) for kv in range(2)]
    kch = [_head_cols(kcb_ref[...], kv) for kv in range(2)]
    vch = [_head_cols(vcb_ref[...], kv) for kv in range(2)]
    tq = 2 * ATT_BLOCK
    for n in range(t // tq):
        rows = slice(n * tq, (n + 1) * tq)
        heads = []
        for kv in range(2):
            qp = jnp.concatenate([qh[2 * kv][rows, :], qh[2 * kv + 1][rows, :]], axis=0)
            o = _softmax_pv([_dot_nt(qp, kch[kv]), _dot_nt(qp, kh[kv])], [vch[kv], vh[kv]], None)
            heads += [o[0:tq, :], o[tq:2 * tq, :]]
        o_ref[rows, GROUP_W:2 * GROUP_W] = jnp.concatenate(heads, axis=-1).astype(BF16)

    xn_ref[...] = x + g2_ref[mrow, :] * _dot(o_ref[...], wout_ref[...].astype(BF16))


def _lat_recurrent_kernel(xn_in_ref, h_ref, g2_ref, wc_ref, wqk_ref, wvg_ref, wout_ref, h0_ref,
                          convw_ref, convb_ref, wa_ref, ba_ref, wx_ref, bx_ref, lam_ref,
                          s0_ref, theta_ref, dn_ref,
                          xn_ref, gate_w_ref, af_ref, bf_ref, ab_ref, bb_ref, hf_ref, hb_ref, gel_ref, ret_ref,
                          *, layer):
    lrow = slice(layer, layer + 1)
    mrow = pl.ds(1 + pl.program_id(0), 1)

    @pl.when(pl.program_id(0) == 0)
    def _():
        _store_gate_weights(gate_w_ref, wa_ref, wx_ref)

    h = h_ref[...]
    u = _dot(h, wc_ref[...].astype(BF16))
    _rglru_prepare(u[:, 0:GROUP_W], u[:, GROUP_W:2 * GROUP_W], convw_ref, convb_ref[lrow, :],
                   gate_w_ref, ba_ref, bx_ref, lam_ref, af_ref, bf_ref, ab_ref, bb_ref, gel_ref)
    oc, _, _ = _rglru_finish(h0_ref[0:1, :], h0_ref[1:2, :],
                             af_ref, bf_ref, ab_ref, bb_ref, hf_ref, hb_ref, gel_ref)
    y = _dot(oc.astype(BF16), wout_ref[0:GROUP_W, :].astype(BF16))

    uqk = _dot(h, wqk_ref[...].astype(BF16))
    uvg = _dot(h, wvg_ref[...].astype(BF16))
    masks = _lane_head_masks(GROUP_W)
    lgf, lgb = _log_decays(theta_ref, masks)
    s0 = tuple(_block_diag([s0_ref[d, hd] for hd in range(N_HEADS)]) for d in range(2))
    _retention(uqk[:, 0:GROUP_W], uqk[:, GROUP_W:2 * GROUP_W] * (HEAD_DIM ** -0.5),
               uvg[:, 0:GROUP_W].astype(BF16), s0, lgf, lgb, masks, ret_ref)
    o = ret_ref[...]
    o = o * lax.rsqrt(_head_mean_square(o) + EPS) * dn_ref[lrow, :] * _silu(uvg[:, GROUP_W:2 * GROUP_W])
    y = y + _dot(o.astype(BF16), wout_ref[GROUP_W:2 * GROUP_W, :].astype(BF16))
    xn_ref[...] = xn_in_ref[...] + g2_ref[mrow, :] * y


def _lat_mixers(x, mod, layer, caches, state_c, state_d, rope, norm2_g, w_in, w_out,
                a_qn, a_kn, a_sink, b_qn, b_kn, c_conv_w, c_conv_b, c_wa, c_ba, c_wx, c_bx,
                c_lambda, d_theta, d_norm_g):
    rows = pl.BlockSpec((DEC_SEQ, D_MODEL), lambda b: (LAT_BLOCK0 + b, 0))
    h_rows = pl.BlockSpec((DEC_SEQ, D_MODEL), lambda b: (b, 0))
    cache_spec = pl.BlockSpec((None, None, PAST_LEN, KV_W), lambda b: (b, layer, 0, 0))
    gain = _full((DEPTH, HEAD_DIM))
    table = _once((DEC_SEQ, LANES), lambda b: (0, 0))
    out_shape = jax.ShapeDtypeStruct((N_ROWS, D_MODEL), F32)
    win_cols = lambda w, c: _once((None, D_MODEL, w), lambda b: (layer, 0, c))
    wout_rows = lambda h, r: _once((None, h, D_MODEL), lambda b: (layer, r, 0))

    xn, h = pl.pallas_call(
        functools.partial(_lat_attn_kernel, layer=layer),
        grid=(DEC_BATCH,),
        in_specs=[rows, _full((DEPTH, D_MODEL)),
                  _mod_chunk(layer, 3), _mod_chunk(layer, 4), _mod_chunk(layer, 5),
                  win_cols(4 * GROUP_W, 0), wout_rows(2 * GROUP_W, 0),
                  cache_spec, cache_spec, cache_spec, cache_spec,
                  gain, gain, gain, gain,
                  pl.BlockSpec(memory_space=pltpu.SMEM),
                  table, table, table],
        out_specs=[rows, pl.BlockSpec((DEC_SEQ, D_MODEL), lambda b: (b, 0), pipeline_mode=pl.Buffered(1))],
        out_shape=[out_shape, jax.ShapeDtypeStruct((N_LAT_ROWS, D_MODEL), BF16)],
        input_output_aliases={0: 0},
        scratch_shapes=[pltpu.VMEM((DEC_SEQ, 4 * GROUP_W), F32), pltpu.VMEM((DEC_SEQ, 2 * GROUP_W), BF16)],
        compiler_params=_cparams("arbitrary"),
        name="lat_attention",
    )(x, norm2_g, mod, mod, mod, w_in, w_out, *caches, a_qn, a_kn, b_qn, b_kn, a_sink, *rope)

    scr = pltpu.VMEM((DEC_SEQ, GROUP_W), F32)
    xn = pl.pallas_call(
        functools.partial(_lat_recurrent_kernel, layer=layer),
        grid=(DEC_BATCH,),
        in_specs=[
            rows, h_rows, _mod_chunk(layer, 5),
            win_cols(2 * GROUP_W, COL_CX // (2 * GROUP_W)),
            win_cols(2 * GROUP_W, COL_DQ // (2 * GROUP_W)), win_cols(2 * GROUP_W, COL_DV // (2 * GROUP_W)),
            wout_rows(2 * GROUP_W, 1),
            pl.BlockSpec((None, None, 2, GROUP_W), lambda b: (b, layer, 0, 0)),
            _layer_block((4, GROUP_W), layer), _full((DEPTH, GROUP_W)),
            _layer_block((2, N_HEADS, HEAD_DIM, HEAD_DIM), layer), _layer_block((2, GROUP_W), layer),
            _layer_block((2, N_HEADS, HEAD_DIM, HEAD_DIM), layer), _layer_block((2, GROUP_W), layer),
            _layer_block((2, GROUP_W), layer),
            pl.BlockSpec((None, None, 2, N_HEADS, HEAD_DIM, HEAD_DIM), lambda b: (b, layer, 0, 0, 0, 0)),
            _layer_block((2, N_HEADS), layer), _full((DEPTH, GROUP_W))],
        out_specs=rows,
        out_shape=out_shape,
        input_output_aliases={0: 0},
        scratch_shapes=[pltpu.VMEM((4, GROUP_W, GROUP_W), BF16)] + [scr] * 8,
        compiler_params=_cparams("arbitrary"),
        name="lat_recurrent",
    )(xn, h, mod, w_in, w_in, w_in, w_out, state_c, c_conv_w, c_conv_b, c_wa, c_ba, c_wx, c_bx, c_lambda,
      state_d, d_theta, d_norm_g)
    return xn


def _rope_tables():
    t = np.arange(DEC_SEQ)
    row = (t // GRID_W).astype(np.float64)[:, None]
    col = (t % GRID_W).astype(np.float64)[:, None]
    half = HEAD_DIM // 2
    inv = 1.0 / (ROPE_BASE ** (np.arange(0, half, 2, dtype=np.float64) / half))
    j = np.arange(LANES) % HEAD_DIM
    ang = np.where((j < half)[None, :], row, col) * inv[j % (half // 2)][None, :]
    first = ((j % half) < half // 2)[None, :]
    cos, sin = np.cos(ang), np.sin(ang)
    return tuple(jnp.asarray(a, F32) for a in (cos, np.where(first, -sin, 0.0), np.where(first, 0.0, sin)))


def kernel(x_prompt, x_sample, cache_a_k, cache_a_v, cache_b_k, cache_b_v, state_c, state_d, c, c_ctx, norm1_g, norm2_g, norm3_g, w_mod, b_mod, ffn1_wg, ffn1_wu, ffn1_wd, ffn2_wg, ffn2_wu, ffn2_wd, w_in, w_out, a_qn, a_kn, a_sink, b_qn, b_kn, c_conv_w, c_conv_b, c_wa, c_ba, c_wx, c_bx, c_lambda, d_theta, d_norm_g):
    mod = _modulation(c_ctx, c, w_mod, b_mod)
    rope = _rope_tables()
    caches = tuple(t.reshape(DEC_BATCH, DEPTH, PAST_LEN, KV_W) for t in (cache_a_k, cache_a_v, cache_b_k, cache_b_v))
    mixer_params = (a_qn, a_kn, a_sink, b_qn, b_kn, c_conv_w, c_conv_b, c_wa, c_ba, c_wx, c_bx,
                    c_lambda, d_theta, d_norm_g)
    xs = (x_prompt.reshape(N_CTX_ROWS, D_MODEL), x_sample.reshape(N_LAT_ROWS, D_MODEL))
    states = ()
    for l in range(DEPTH):
        (x,) = _ffn(xs, mod, l, 0, norm1_g, ffn1_wg, ffn1_wu, ffn1_wd)
        x, states = _ctx_mixers(x, mod, l, states, norm2_g, w_in, w_out, *mixer_params)
        x = _lat_mixers(x, mod, l, caches, state_c, state_d, rope, norm2_g, w_in, w_out, *mixer_params)
        xs = _ffn((x,), mod, l, 6, norm3_g, ffn2_wg, ffn2_wu, ffn2_wd, split_out=(l == DEPTH - 1))
    y_p, y_s = xs
    ka, va, kb, vb, st_c, st_d = states
    kv_shape = (BATCH, DEPTH, SEQ, 2, HEAD_DIM)
    return (y_p.reshape(BATCH, SEQ, D_MODEL), y_s.reshape(DEC_BATCH, DEC_SEQ, D_MODEL),
            ka.reshape(kv_shape), va.reshape(kv_shape), kb.reshape(kv_shape), vb.reshape(kv_shape),
            st_c, st_d)
```

```python
import functools
import math

import numpy as np
import jax
import jax.numpy as jnp
from jax import lax
from jax.experimental import pallas as pl
from jax.experimental.pallas import tpu as pltpu

F32 = jnp.float32
BF16 = jnp.bfloat16

D_MODEL = 1024
BATCH = 16
SEQ = 256
DEPTH = 2
DEC_BATCH = 2
DEC_SEQ = 1024
PAST_LEN = 512
GRID_W = 64
HEAD_DIM = 64
HEAD_SHIFT = 6
N_HEADS = 4
GROUP_W = 256
KV_W = 2 * HEAD_DIM
LANES = 128
WINDOW = 128
ATT_BLOCK = 128
ROPE_BASE = 10000.0
LRU_C = 8.0
D_FF = 2816
N_MOD = 9
EPS = 1e-6
NEG_INF = -1e30
IN_WIDTH = 2560

N_CTX_ROWS = BATCH * SEQ
N_LAT_ROWS = DEC_BATCH * DEC_SEQ
N_ROWS = N_CTX_ROWS + N_LAT_ROWS
MOD_ROWS = 8
MOD_GROUP = 1024

VMEM_LIMIT_BYTES = 56 * 1024 * 1024

COL_AQ, COL_AK, COL_AV = 0, 256, 384
COL_BQ, COL_BK, COL_BV = 512, 768, 896
COL_CX, COL_CY = 1024, 1280
COL_DQ, COL_DK, COL_DV, COL_DG = 1536, 1792, 2048, 2304


def _cparams(*sem):
    return pltpu.CompilerParams(dimension_semantics=sem, vmem_limit_bytes=VMEM_LIMIT_BYTES)


def _dot(a, b):
    return jnp.dot(a, b, preferred_element_type=F32)


def _dot_nt(a, b):
    return lax.dot_general(a, b, (((1,), (1,)), ((), ())), preferred_element_type=F32)


def _dot_tn(a, b):
    return lax.dot_general(a, b, (((0,), (0,)), ((), ())), preferred_element_type=F32)


def _sigmoid(x):
    return 0.5 * jnp.tanh(0.5 * x) + 0.5


def _silu(x):
    return x * _sigmoid(x)


def _gelu_tanh(x):
    return 0.5 * x * (1.0 + jnp.tanh(math.sqrt(2.0 / math.pi) * (x + 0.044715 * (x * x * x))))


def _mod_row(i, tm, s):
    if tm >= MOD_GROUP:
        block_index = i * (tm // MOD_GROUP) + s
    else:
        block_index = i >> int(math.log2(MOD_GROUP // tm))
    return jnp.maximum(block_index - (N_CTX_ROWS // MOD_GROUP - 1), 0)


def _norm_mod(x, g, sc, sh):
    ms = jnp.mean(x * x, axis=-1, keepdims=True)
    return (x * lax.rsqrt(ms + EPS) * g) * (1.0 + sc) + sh


def _full(shape):
    return pl.BlockSpec(shape, lambda *_: (0,) * len(shape))


def _layer_block(shape, layer):
    return pl.BlockSpec((None,) + shape, lambda *_: (layer,) + (0,) * len(shape))


MOD_TN = 3072


def _mod_kernel(cc_ref, c_ref, w_ref, b_ref, o_ref):
    l = pl.program_id(0)
    pad = jnp.zeros((MOD_ROWS - 1 - DEC_BATCH, D_MODEL), F32)
    cond = jnp.concatenate([cc_ref[...], c_ref[...], pad], axis=0)
    o_ref[...] = _dot(_silu(cond).astype(BF16), w_ref[...].astype(BF16)) + b_ref[pl.ds(l, 1), :]


def _modulation(c_ctx, c, w_mod, b_mod):
    n = N_MOD * D_MODEL
    return pl.pallas_call(
        _mod_kernel,
        grid=(DEPTH, n // MOD_TN),
        in_specs=[
            pl.BlockSpec((1, D_MODEL), lambda l, j: (0, 0)),
            pl.BlockSpec((DEC_BATCH, D_MODEL), lambda l, j: (0, 0)),
            pl.BlockSpec((None, D_MODEL, MOD_TN), lambda l, j: (l, 0, j)),
            pl.BlockSpec((DEPTH, MOD_TN), lambda l, j: (0, j)),
        ],
        out_specs=pl.BlockSpec((None, MOD_ROWS, MOD_TN), lambda l, j: (l, 0, j)),
        out_shape=jax.ShapeDtypeStruct((DEPTH, MOD_ROWS, n), F32),
        compiler_params=_cparams("arbitrary", "arbitrary"),
        name="modulation",
    )(c_ctx.reshape(1, D_MODEL), c, w_mod, b_mod)


FFN_TM = 1024
FFN_TF = 256
N_CTX_TILES = N_CTX_ROWS // FFN_TM


FFN_NJ = D_FF // FFN_TF
N_FFN_TILES = N_ROWS // FFN_TM
N_FFN_STEPS = FFN_NJ + N_FFN_TILES


def _ffn_tile(step):
    return jnp.maximum(step - FFN_NJ, 0)


def _on_stream_part(tile, x_refs, o_refs, fn):
    if len(x_refs) == 1 and len(o_refs) == 1:
        fn(x_refs[0], o_refs[0])
    else:
        pl.when(tile < N_CTX_TILES)(lambda: fn(x_refs[0], o_refs[0]))
        pl.when(tile >= N_CTX_TILES)(lambda: fn(x_refs[-1], o_refs[-1]))


def _ffn_kernel(*refs, layer, n_in, n_out):
    x_refs = refs[:n_in]
    n_ref, sh_ref, sc_ref, g_ref, wg_ref, wu_ref, wd_ref = refs[n_in:n_in + 7]
    o_refs = refs[n_in + 7:n_in + 7 + n_out]
    h_ref, a_ref, wg_s, wu_s, wd_s = refs[n_in + 7 + n_out:]
    nj, tf = FFN_NJ, FFN_TF
    s = pl.program_id(0)
    tile = _ffn_tile(s)
    r = _mod_row(tile, FFN_TM, 0)

    def load_tile():
        def init(x_ref, _):
            h = _norm_mod(x_ref[...], n_ref[layer:layer + 1, :], sc_ref[pl.ds(r, 1), :], sh_ref[pl.ds(r, 1), :])
            h_ref[...] = h.astype(BF16)
        _on_stream_part(tile, x_refs, x_refs, init)

    def up_chunk(j, cols):
        h = h_ref[...]
        a_ref[:, cols] = (_silu(_dot(h, wg_s[j])) * _dot(h, wu_s[j])).astype(BF16)

    def down_and_store():
        y = (0.5 * g_ref[pl.ds(r, 1), :]) * _dot(a_ref[...], wd_s[...])

        def store(x_ref, o_ref):
            o_ref[...] = x_ref[...] + y
        _on_stream_part(tile, x_refs, o_refs, store)

    def keep_arrived_chunk():
        wg_s[s] = wg_ref[...].astype(BF16)
        wu_s[s] = wu_ref[...].astype(BF16)
        wd_s[pl.ds(pl.multiple_of(s * tf, tf), tf), :] = wd_ref[...].astype(BF16)

    def up_previous_chunk():
        up_chunk(s - 1, pl.ds(pl.multiple_of((s - 1) * tf, tf), tf))

    @pl.when(s == 0)
    def _():
        load_tile()
        keep_arrived_chunk()

    @pl.when((s > 0) & (s < nj))
    def _():
        up_previous_chunk()
        keep_arrived_chunk()

    @pl.when(s == nj)
    def _():
        up_previous_chunk()
        down_and_store()

    @pl.when(s > nj)
    def _():
        load_tile()
        for j in range(nj):
            up_chunk(j, slice(j * tf, (j + 1) * tf))
        down_and_store()


def _stream_specs(split, buffered_once):
    tm = FFN_TM
    kw = {"pipeline_mode": pl.Buffered(1)} if buffered_once else {}
    if not split:
        return [pl.BlockSpec((tm, D_MODEL), lambda s: (_ffn_tile(s), 0), **kw)]
    last_ctx = N_CTX_TILES - 1
    return [pl.BlockSpec((tm, D_MODEL), lambda s: (jnp.minimum(_ffn_tile(s), last_ctx), 0), **kw),
            pl.BlockSpec((tm, D_MODEL), lambda s: (jnp.maximum(_ffn_tile(s) - N_CTX_TILES, 0), 0), **kw)]


def _ffn(xs, mod, layer, chunk0, norm_g, wg, wu, wd, split_out=False):
    tm, tf, nj = FFN_TM, FFN_TF, FFN_NJ
    split_in = len(xs) == 2
    mod_spec = lambda c: pl.BlockSpec((None, MOD_ROWS, D_MODEL), lambda s: (layer, 0, c))
    w_col = lambda s: (layer, 0, jnp.minimum(s, nj - 1))
    w_row = lambda s: (layer, jnp.minimum(s, nj - 1), 0)
    if split_out:
        out_shape = [jax.ShapeDtypeStruct((N_CTX_ROWS, D_MODEL), F32),
                     jax.ShapeDtypeStruct((N_LAT_ROWS, D_MODEL), F32)]
    else:
        out_shape = [jax.ShapeDtypeStruct((N_ROWS, D_MODEL), F32)]
    out = pl.pallas_call(
        functools.partial(_ffn_kernel, layer=layer, n_in=len(xs), n_out=len(out_shape)),
        grid=(N_FFN_STEPS,),
        in_specs=_stream_specs(split_in, False) + [
            _full((DEPTH, D_MODEL)),
            mod_spec(chunk0), mod_spec(chunk0 + 1), mod_spec(chunk0 + 2),
            pl.BlockSpec((None, D_MODEL, tf), w_col),
            pl.BlockSpec((None, D_MODEL, tf), w_col),
            pl.BlockSpec((None, tf, D_MODEL), w_row),
        ],
        out_specs=_stream_specs(split_out, True),
        out_shape=out_shape,
        scratch_shapes=[pltpu.VMEM((tm, D_MODEL), BF16),
                        pltpu.VMEM((tm, D_FF), BF16),
                        pltpu.VMEM((nj, D_MODEL, tf), BF16),
                        pltpu.VMEM((nj, D_MODEL, tf), BF16),
                        pltpu.VMEM((D_FF, D_MODEL), BF16)],
        compiler_params=_cparams("arbitrary"),
        name="ffn",
    )(*xs, norm_g, mod, mod, mod, wg, wu, wd)
    return tuple(out)


def _once(shape, index_map):
    return pl.BlockSpec(shape, index_map, pipeline_mode=pl.Buffered(1))


def _mod_chunk(layer, c):
    return pl.BlockSpec((None, MOD_ROWS, D_MODEL), lambda *_: (layer, 0, c))


def _head_mean_square(x):
    n = x.shape[-1]
    r = lax.broadcasted_iota(jnp.int32, (n, n), 0) >> HEAD_SHIFT
    c = lax.broadcasted_iota(jnp.int32, (n, n), 1) >> HEAD_SHIFT
    ones_bd = jnp.where(r == c, 1.0, 0.0).astype(BF16)
    return _dot((x * x).astype(BF16), ones_bd) * (1.0 / HEAD_DIM)


def _head_norm(x, head_gain):
    gain_row = jnp.concatenate([head_gain] * (x.shape[-1] // HEAD_DIM), axis=-1)
    return x * lax.rsqrt(_head_mean_square(x) + EPS) * gain_row


def _head_cols(x, h):
    return x[:, h * HEAD_DIM:(h + 1) * HEAD_DIM].astype(BF16)


def _softmax_pv(scores, values, sink):
    m = jnp.max(scores[0], axis=-1, keepdims=True)
    for s in scores[1:]:
        m = jnp.maximum(m, jnp.max(s, axis=-1, keepdims=True))
    if sink is not None:
        m = jnp.maximum(m, sink)
    denom = None
    acc = None
    for s, v in zip(scores, values):
        p = jnp.exp(s - m)
        d = jnp.sum(p, axis=-1, keepdims=True)
        o = _dot(p.astype(BF16), v)
        denom = d if denom is None else denom + d
        acc = o if acc is None else acc + o
    if sink is not None:
        denom = denom + jnp.exp(sink - m)
    return acc / denom


def _rope(x, cos, sin_lo, sin_hi):
    cols = []
    for c in range(x.shape[-1] // LANES):
        xc = x[:, c * LANES:(c + 1) * LANES]
        cols.append(xc * cos + pltpu.roll(xc, 112, 1) * sin_lo + pltpu.roll(xc, 16, 1) * sin_hi)
    return cols[0] if len(cols) == 1 else jnp.concatenate(cols, axis=-1)


def _block_diag(blocks):
    n = len(blocks)
    w = blocks[0].shape[0]
    rows = []
    for k, blk in enumerate(blocks):
        parts = []
        if k > 0:
            parts.append(jnp.zeros((w, k * w), F32))
        parts.append(blk)
        if k < n - 1:
            parts.append(jnp.zeros((w, (n - 1 - k) * w), F32))
        rows.append(jnp.concatenate(parts, axis=-1))
    return jnp.concatenate(rows, axis=0)


def _rglru_gates(xc, wa, ba, wx, bx, lam):
    xb = xc.astype(BF16)
    r = _sigmoid(_dot(xb, wa) + ba)
    i = _sigmoid(_dot(xb, wx) + bx)
    softplus = jnp.maximum(-lam, 0.0) + jnp.log1p(jnp.exp(-jnp.abs(lam)))
    log_a = (-LRU_C) * r * softplus
    a = jnp.exp(log_a)
    b = jnp.sqrt(1.0 - a * a) * (i * xc)
    return a, b


def _block_prefix(a, b, reverse):
    t = a.shape[0]
    row = lax.broadcasted_iota(jnp.int32, a.shape, 0) & 7
    for d in (1, 2, 4):
        if reverse:
            a_s = pltpu.roll(a, t - d, 0)
            b_s = pltpu.roll(b, t - d, 0)
            ok = row < 8 - d
        else:
            a_s = pltpu.roll(a, d, 0)
            b_s = pltpu.roll(b, d, 0)
            ok = row >= d
        b = jnp.where(ok, a * b_s + b, b)
        a = jnp.where(ok, a * a_s, a)
    return a, b


def _conv4(x, w_ref, b_row):
    t = x.shape[0]
    row = lax.broadcasted_iota(jnp.int32, x.shape, 0)
    xm2 = jnp.where(row >= 2, pltpu.roll(x, 2, 0), 0.0)
    xm1 = jnp.where(row >= 1, pltpu.roll(x, 1, 0), 0.0)
    xp1 = jnp.where(row < t - 1, pltpu.roll(x, t - 1, 0), 0.0)
    return (xm2 * w_ref[0:1, :] + xm1 * w_ref[1:2, :] + x * w_ref[2:3, :] + xp1 * w_ref[3:4, :]) + b_row


def _rglru_prepare(cx, cy, conv_w_ref, conv_b, gate_w_ref, ba_ref, bx_ref, lam_ref,
                   af_ref, bf_ref, ab_ref, bb_ref, gel_ref):
    xc = _conv4(cx, conv_w_ref, conv_b)
    a, b = _rglru_gates(xc, gate_w_ref[0], ba_ref[0:1, :], gate_w_ref[1], bx_ref[0:1, :], lam_ref[0:1, :])
    a, b = _block_prefix(a, b, reverse=False)
    af_ref[...] = a
    bf_ref[...] = b
    a, b = _rglru_gates(xc, gate_w_ref[2], ba_ref[1:2, :], gate_w_ref[3], bx_ref[1:2, :], lam_ref[1:2, :])
    a, b = _block_prefix(a, b, reverse=True)
    ab_ref[...] = a
    bb_ref[...] = b
    gel_ref[...] = _gelu_tanh(cy)


SCAN_UNROLL = 8


def _rglru_finish(h0f, h0b, af_ref, bf_ref, ab_ref, bb_ref, hf_ref, hb_ref, gel_ref):
    nblk = af_ref.shape[0] // 8

    def body(k, carry):
        cf, cb = carry
        rf = pl.ds(pl.multiple_of(k * 8, 8), 8)
        hf = bf_ref[rf, :] + af_ref[rf, :] * cf
        hf_ref[rf, :] = hf
        rb = pl.ds(pl.multiple_of((nblk - 1 - k) * 8, 8), 8)
        hb = bb_ref[rb, :] + ab_ref[rb, :] * cb
        hb_ref[rb, :] = hb
        return hf[7:8, :], hb[0:1, :]

    cf, cb = lax.fori_loop(0, nblk, body, (h0f, h0b), unroll=SCAN_UNROLL)
    oc = (hf_ref[...] + hb_ref[...]) * gel_ref[...]
    return oc, cf, cb


def _store_gate_weights(gate_w_ref, wa_ref, wx_ref):
    for d in range(2):
        gate_w_ref[2 * d] = _block_diag([wa_ref[d, n] for n in range(N_HEADS)]).astype(BF16)
        gate_w_ref[2 * d + 1] = _block_diag([wx_ref[d, n] for n in range(N_HEADS)]).astype(BF16)


def _lane_head_masks(n):
    lane = lax.broadcasted_iota(jnp.int32, (1, n), 1) >> HEAD_SHIFT
    return [jnp.where(lane == h, 1.0, 0.0) for h in range(n // HEAD_DIM)]


def _log_decays(theta_ref, masks):
    theta = theta_ref[...]
    lanes = theta[:, 0:1] * masks[0]
    for h in range(1, N_HEADS):
        lanes = lanes + theta[:, h:h + 1] * masks[h]
    lg = jnp.log1p(-jnp.exp(lanes))
    return lg[0:1, :], lg[1:2, :]


RET_BLOCK = 256


def _retention(q, k8, vb, s0, lgf, lgb, masks, o_ref):
    t, w = q.shape
    c = RET_BLOCK
    nh = w // HEAD_DIM
    pos = lax.broadcasted_iota(jnp.int32, (c, w), 0).astype(F32)
    q_dec = (jnp.exp(lgf * (pos + 1.0)), jnp.exp(lgb * (float(c) - pos)))
    k_dec = (jnp.exp(lgf * (float(c - 1) - pos)), jnp.exp(lgb * pos))
    chunk_dec = (jnp.exp(lgf * float(c)), jnp.exp(lgb * float(c)))
    rel = (lax.broadcasted_iota(jnp.int32, (c, c), 0) - lax.broadcasted_iota(jnp.int32, (c, c), 1)).astype(F32)
    decs = []
    for h in range(nh):
        gf = lgf[:, h * HEAD_DIM:h * HEAD_DIM + 1]
        gb = lgb[:, h * HEAD_DIM:h * HEAD_DIM + 1]
        e = jnp.exp(jnp.where(rel >= 0, gf * rel, gb * (-rel)))
        decs.append(jnp.where(rel == 0, 2.0, e))
    dec = jnp.concatenate(decs, axis=0)
    r_head = lax.broadcasted_iota(jnp.int32, (w, w), 0) >> HEAD_SHIFT
    c_head = lax.broadcasted_iota(jnp.int32, (w, w), 1) >> HEAD_SHIFT
    same_head = jnp.where(r_head == c_head, 1.0, 0.0)
    states = [None, None] if s0 is None else list(s0)

    def carry(d, rows, o):
        if states[d] is not None:
            o = o + _dot((q[rows, :] * q_dec[d]).astype(BF16), states[d].astype(BF16))
        upd = _dot_tn((k8[rows, :] * k_dec[d]).astype(BF16), vb[rows, :]) * same_head
        states[d] = upd if states[d] is None else states[d] * chunk_dec[d] + upd
        return o

    for ci in range(t // c):
        rows = slice(ci * c, (ci + 1) * c)
        qc = q[rows, :]
        q_stack = jnp.concatenate([(qc * masks[h]).astype(BF16) for h in range(nh)], axis=0)
        inner = (_dot_nt(q_stack, k8[rows, :].astype(BF16)) * dec).astype(BF16)
        out = _dot(inner, vb[rows, :])
        o = out[0:c, :] * masks[0]
        for h in range(1, nh):
            o = o + out[h * c:(h + 1) * c, :] * masks[h]
        o_ref[rows, :] = carry(0, rows, o)
    for ci in reversed(range(t // c)):
        rows = slice(ci * c, (ci + 1) * c)
        if states[1] is not None:
            o_ref[rows, :] = carry(1, rows, o_ref[rows, :])
        else:
            carry(1, rows, None)
    return states[0], states[1]


def _ctx_mixer_kernel(*refs, layer, n_prev):
    prev_refs = refs[:n_prev]
    (x_ref, n2_ref, sh_ref, sc_ref, g2_ref, win_ref, wout_ref,
     aqn_ref, akn_ref, bqn_ref, bkn_ref, sink_ref,
     convw_ref, convb_ref, wa_ref, ba_ref, wx_ref, bx_ref, lam_ref, theta_ref, dn_ref,
     xn_ref, *state_refs) = refs[n_prev:n_prev + 28]
    (win_s, wout_s, u_ref, mixed_ref,
     gate_w_ref, af_ref, bf_ref, ab_ref, bb_ref, hf_ref, hb_ref, gel_ref, ret_ref) = refs[n_prev + 28:]
    lrow = slice(layer, layer + 1)

    @pl.when(pl.program_id(0) == 0)
    def _():
        for c0 in range(0, IN_WIDTH, 2 * GROUP_W):
            win_s[:, c0:c0 + 2 * GROUP_W] = win_ref[:, c0:c0 + 2 * GROUP_W].astype(BF16)
        wout_s[...] = wout_ref[...].astype(BF16)
        _store_gate_weights(gate_w_ref, wa_ref, wx_ref)

    for sb in range(CTX_SUB):
        _ctx_request(sb, layer, lrow, prev_refs, state_refs, x_ref, n2_ref, sh_ref, sc_ref, g2_ref,
                     aqn_ref, akn_ref, bqn_ref, bkn_ref, sink_ref, convw_ref, convb_ref, ba_ref, bx_ref,
                     lam_ref, theta_ref, dn_ref, xn_ref, win_s, wout_s, u_ref, mixed_ref,
                     gate_w_ref, af_ref, bf_ref, ab_ref, bb_ref, hf_ref, hb_ref, gel_ref, ret_ref)


def _ctx_request(sb, layer, lrow, prev_refs, state_refs, x_ref, n2_ref, sh_ref, sc_ref, g2_ref,
                 aqn_ref, akn_ref, bqn_ref, bkn_ref, sink_ref, convw_ref, convb_ref, ba_ref, bx_ref,
                 lam_ref, theta_ref, dn_ref, xn_ref, win_s, wout_s, u_ref, mixed_ref,
                 gate_w_ref, af_ref, bf_ref, ab_ref, bb_ref, hf_ref, hb_ref, gel_ref, ret_ref):
    t = SEQ
    rows = slice(sb * SEQ, (sb + 1) * SEQ)
    for prev_ref, state_ref in zip(prev_refs, state_refs):
        for earlier in range(layer):
            state_ref[sb, earlier] = prev_ref[sb, earlier]
    ka_ref, va_ref, kb_ref, vb_ref, stc_ref, std_ref = (ref.at[sb, layer] for ref in state_refs)

    x = x_ref[rows, :]
    h = _norm_mod(x, n2_ref[lrow, :], sc_ref[0:1, :], sh_ref[0:1, :]).astype(BF16)
    c_cols = slice(COL_CX, COL_CX + 2 * GROUP_W)
    u_ref[:, c_cols] = _dot(h, win_s[:, c_cols])
    _rglru_prepare(u_ref[:, COL_CX:COL_CX + GROUP_W], u_ref[:, COL_CY:COL_CY + GROUP_W],
                   convw_ref, convb_ref[lrow, :], gate_w_ref, ba_ref, bx_ref, lam_ref,
                   af_ref, bf_ref, ab_ref, bb_ref, gel_ref)
    u_ref[:, 0:COL_CX] = _dot(h, win_s[:, 0:COL_CX])
    u_ref[:, COL_DQ:IN_WIDTH] = _dot(h, win_s[:, COL_DQ:IN_WIDTH])

    for (cq, ck, cv, qn_ref, kn_ref, k_out, v_out, col0, use_sink) in (
            (COL_AQ, COL_AK, COL_AV, aqn_ref, akn_ref, ka_ref, va_ref, 0, True),
            (COL_BQ, COL_BK, COL_BV, bqn_ref, bkn_ref, kb_ref, vb_ref, GROUP_W, False)):
        q = _head_norm(u_ref[:, cq:cq + GROUP_W], qn_ref[lrow, :])
        k = _head_norm(u_ref[:, ck:ck + KV_W], kn_ref[lrow, :])
        v = u_ref[:, cv:cv + KV_W]
        k_out[...] = k
        v_out[...] = v
        qs = q * (HEAD_DIM ** -0.5)
        heads = []
        for hd in range(N_HEADS):
            kv = hd // 2
            s = _dot_nt(_head_cols(qs, hd), _head_cols(k, kv))
            sink = jnp.full((t, 1), sink_ref[layer, hd], F32) if use_sink else None
            heads.append(_softmax_pv([s], [_head_cols(v, kv)], sink))
        mixed_ref[:, col0:col0 + GROUP_W] = jnp.concatenate(heads, axis=-1).astype(BF16)

    zero = jnp.zeros((1, GROUP_W), F32)
    oc, cf, cb = _rglru_finish(zero, zero, af_ref, bf_ref, ab_ref, bb_ref, hf_ref, hb_ref, gel_ref)
    mixed_ref[:, 2 * GROUP_W:3 * GROUP_W] = oc.astype(BF16)
    stc_ref[0:1, :] = cf
    stc_ref[1:2, :] = cb

    masks = _lane_head_masks(GROUP_W)
    lgf, lgb = _log_decays(theta_ref, masks)
    k8 = u_ref[:, COL_DK:COL_DK + GROUP_W] * (HEAD_DIM ** -0.5)
    vb = u_ref[:, COL_DV:COL_DV + GROUP_W].astype(BF16)
    final_states = _retention(u_ref[:, COL_DQ:COL_DQ + GROUP_W], k8, vb, None, lgf, lgb, masks, ret_ref)
    o = ret_ref[...]
    o = o * lax.rsqrt(_head_mean_square(o) + EPS) * dn_ref[lrow, :] * _silu(u_ref[:, COL_DG:COL_DG + GROUP_W])
    mixed_ref[:, 3 * GROUP_W:4 * GROUP_W] = o.astype(BF16)
    for d, s_full in enumerate(final_states):
        for hd in range(N_HEADS):
            std_ref[d, hd] = s_full[hd * HEAD_DIM:(hd + 1) * HEAD_DIM, hd * HEAD_DIM:(hd + 1) * HEAD_DIM]

    xn_ref[rows, :] = x + g2_ref[0:1, :] * _dot(mixed_ref[...], wout_s[...])


CTX_SUB = 2


def _ctx_mixers(x, mod, layer, prev, norm2_g, w_in, w_out,
                a_qn, a_kn, a_sink, b_qn, b_kn, c_conv_w, c_conv_b, c_wa, c_ba, c_wx, c_bx,
                c_lambda, d_theta, d_norm_g):
    per_request = lambda slots, shape: pl.BlockSpec((CTX_SUB, slots) + shape,
                                                    lambda b: (b,) + (0,) * (1 + len(shape)))
    state_dims = [(SEQ, KV_W)] * 4 + [(2, GROUP_W), (2, N_HEADS, HEAD_DIM, HEAD_DIM)]
    scr = pltpu.VMEM((SEQ, GROUP_W), F32)
    out = pl.pallas_call(
        functools.partial(_ctx_mixer_kernel, layer=layer, n_prev=len(prev)),
        grid=(BATCH // CTX_SUB,),
        in_specs=[per_request(layer, dims) for dims in state_dims[:len(prev)]] + [
            pl.BlockSpec((CTX_SUB * SEQ, D_MODEL), lambda b: (b, 0)),
            _full((DEPTH, D_MODEL)),
            _mod_chunk(layer, 3), _mod_chunk(layer, 4), _mod_chunk(layer, 5),
            _once((None, D_MODEL, IN_WIDTH), lambda b: (layer, 0, 0)),
            _once((None, D_MODEL, D_MODEL), lambda b: (layer, 0, 0)),
            _full((DEPTH, HEAD_DIM)), _full((DEPTH, HEAD_DIM)), _full((DEPTH, HEAD_DIM)), _full((DEPTH, HEAD_DIM)),
            pl.BlockSpec(memory_space=pltpu.SMEM),
            _layer_block((4, GROUP_W), layer), _full((DEPTH, GROUP_W)),
            _layer_block((2, N_HEADS, HEAD_DIM, HEAD_DIM), layer), _layer_block((2, GROUP_W), layer),
            _layer_block((2, N_HEADS, HEAD_DIM, HEAD_DIM), layer), _layer_block((2, GROUP_W), layer),
            _layer_block((2, GROUP_W), layer),
            _layer_block((2, N_HEADS), layer), _full((DEPTH, GROUP_W)),
        ],
        out_specs=[pl.BlockSpec((CTX_SUB * SEQ, D_MODEL), lambda b: (b, 0))] + [
            per_request(layer + 1, dims) for dims in state_dims],
        out_shape=[jax.ShapeDtypeStruct((N_ROWS, D_MODEL), F32)] + [
            jax.ShapeDtypeStruct((BATCH, layer + 1) + dims, F32) for dims in state_dims],
        input_output_aliases={len(prev): 0},
        scratch_shapes=[pltpu.VMEM((D_MODEL, IN_WIDTH), BF16), pltpu.VMEM((D_MODEL, D_MODEL), BF16),
                        pltpu.VMEM((SEQ, IN_WIDTH), F32), pltpu.VMEM((SEQ, D_MODEL), BF16),
                        pltpu.VMEM((4, GROUP_W, GROUP_W), BF16)] + [scr] * 8,
        compiler_params=_cparams("arbitrary"),
        name="ctx_mixers",
    )(*prev, x, norm2_g, mod, mod, mod, w_in, w_out,
      a_qn, a_kn, b_qn, b_kn, a_sink, c_conv_w, c_conv_b, c_wa, c_ba, c_wx, c_bx,
      c_lambda, d_theta, d_norm_g)
    return out[0], tuple(out[1:])


LAT_BLOCK0 = N_CTX_ROWS // DEC_SEQ


def _lat_attn_kernel(x_ref, n2_ref, sh_ref, sc_ref, g2_ref, win_ref, wout_ref,
                     kca_ref, vca_ref, kcb_ref, vcb_ref,
                     aqn_ref, akn_ref, bqn_ref, bkn_ref, sink_ref, cos_ref, sinl_ref, sinh_ref,
                     xn_ref, h_ref, u_ref, o_ref, *, layer):
    t = DEC_SEQ
    lrow = slice(layer, layer + 1)
    mrow = pl.ds(1 + pl.program_id(0), 1)
    cos, sin_lo, sin_hi = cos_ref[...], sinl_ref[...], sinh_ref[...]
    scale = HEAD_DIM ** -0.5
    x = x_ref[...]
    h_ref[...] = _norm_mod(x, n2_ref[lrow, :], sc_ref[mrow, :], sh_ref[mrow, :]).astype(BF16)
    u_ref[...] = _dot(h_ref[...], win_ref[...].astype(BF16))

    q = _rope(_head_norm(u_ref[:, COL_AQ:COL_AQ + GROUP_W], aqn_ref[lrow, :]), cos, sin_lo, sin_hi)
    k = _rope(_head_norm(u_ref[:, COL_AK:COL_AK + KV_W], akn_ref[lrow, :]), cos, sin_lo, sin_hi)
    qh = [_head_cols(q * scale, h) for h in range(4)]
    v = u_ref[:, COL_AV:COL_AV + KV_W]
    kh = [_head_cols(k, kv) for kv in range(2)]
    vh = [_head_cols(v, kv) for kv in range(2)]
    kch = [_head_cols(kca_ref[...], kv) for kv in range(2)]
    vch = [_head_cols(vca_ref[...], kv) for kv in range(2)]
    w = ATT_BLOCK
    span = 3 * w
    for n in range(t // w):
        start = min(max((n - 1) * w, 0), t - span)
        rows = slice(n * w, (n + 1) * w)
        band = slice(start, start + span)
        qpos = (lax.broadcasted_iota(jnp.int32, (2 * w, span), 0) & (w - 1)) + n * w
        kpos = lax.broadcasted_iota(jnp.int32, (2 * w, span), 1) + start
        valid = jnp.abs(qpos - kpos) <= WINDOW
        heads = []
        for kv in range(2):
            qp = jnp.concatenate([qh[2 * kv][rows, :], qh[2 * kv + 1][rows, :]], axis=0)
            s_ctx = _dot_nt(qp, kch[kv])
            s_band = jnp.where(valid, _dot_nt(qp, kh[kv][band, :]), NEG_INF)
            row = lax.broadcasted_iota(jnp.int32, (2 * w, 1), 0)
            sink = jnp.where(row < w, sink_ref[layer, 2 * kv], sink_ref[layer, 2 * kv + 1])
            o = _softmax_pv([s_ctx, s_band], [vch[kv], vh[kv][band, :]], sink)
            heads += [o[0:w, :], o[w:2 * w, :]]
        o_ref[rows, 0:GROUP_W] = jnp.concatenate(heads, axis=-1).astype(BF16)

    q = _rope(_head_norm(u_ref[:, COL_BQ:COL_BQ + GROUP_W], bqn_ref[lrow, :]), cos, sin_lo, sin_hi)
    k = _rope(_head_norm(u_ref[:, COL_BK:COL_BK + KV_W], bkn_ref[lrow, :]), cos, sin_lo, sin_hi)
    qh = [_head_cols(q * scale, h) for h in range(4)]
    v = u_ref[:, COL_BV:COL_BV + KV_W]
    kh = [_head_cols(k, kv) for kv in range(2)]
    vh = [_head_cols(v, kv) for kv in range(2)]
    kch = [_head_cols(kcb_ref[...], kv) for kv in range(2)]
    vch = [_head_cols(vcb_ref[...], kv) for kv in range(2)]
    tq = 2 * ATT_BLOCK
    for n in range(t // tq):
        rows = slice(n * tq, (n + 1) * tq)
        heads = []
        for kv in range(2):
            qp = jnp.concatenate([qh[2 * kv][rows, :], qh[2 * kv + 1][rows, :]], axis=0)
            o = _softmax_pv([_dot_nt(qp, kch[kv]), _dot_nt(qp, kh[kv])], [vch[kv], vh[kv]], None)
            heads += [o[0:tq, :], o[tq:2 * tq, :]]
        o_ref[rows, GROUP_W:2 * GROUP_W] = jnp.concatenate(heads, axis=-1).astype(BF16)

    xn_ref[...] = x + g2_ref[mrow, :] * _dot(o_ref[...], wout_ref[...].astype(BF16))


def _lat_recurrent_kernel(xn_in_ref, h_ref, g2_ref, wc_ref, wqk_ref, wvg_ref, wout_ref, h0_ref,
                          convw_ref, convb_ref, wa_ref, ba_ref, wx_ref, bx_ref, lam_ref,
                          s0_ref, theta_ref, dn_ref,
                          xn_ref, gate_w_ref, af_ref, bf_ref, ab_ref, bb_ref, hf_ref, hb_ref, gel_ref, ret_ref,
                          *, layer):
    lrow = slice(layer, layer + 1)
    mrow = pl.ds(1 + pl.program_id(0), 1)

    @pl.when(pl.program_id(0) == 0)
    def _():
        _store_gate_weights(gate_w_ref, wa_ref, wx_ref)

    h = h_ref[...]
    u = _dot(h, wc_ref[...].astype(BF16))
    _rglru_prepare(u[:, 0:GROUP_W], u[:, GROUP_W:2 * GROUP_W], convw_ref, convb_ref[lrow, :],
                   gate_w_ref, ba_ref, bx_ref, lam_ref, af_ref, bf_ref, ab_ref, bb_ref, gel_ref)
    oc, _, _ = _rglru_finish(h0_ref[0:1, :], h0_ref[1:2, :],
                             af_ref, bf_ref, ab_ref, bb_ref, hf_ref, hb_ref, gel_ref)
    y = _dot(oc.astype(BF16), wout_ref[0:GROUP_W, :].astype(BF16))

    uqk = _dot(h, wqk_ref[...].astype(BF16))
    uvg = _dot(h, wvg_ref[...].astype(BF16))
    masks = _lane_head_masks(GROUP_W)
    lgf, lgb = _log_decays(theta_ref, masks)
    s0 = tuple(_block_diag([s0_ref[d, hd] for hd in range(N_HEADS)]) for d in range(2))
    _retention(uqk[:, 0:GROUP_W], uqk[:, GROUP_W:2 * GROUP_W] * (HEAD_DIM ** -0.5),
               uvg[:, 0:GROUP_W].astype(BF16), s0, lgf, lgb, masks, ret_ref)
    o = ret_ref[...]
    o = o * lax.rsqrt(_head_mean_square(o) + EPS) * dn_ref[lrow, :] * _silu(uvg[:, GROUP_W:2 * GROUP_W])
    y = y + _dot(o.astype(BF16), wout_ref[GROUP_W:2 * GROUP_W, :].astype(BF16))
    xn_ref[...] = xn_in_ref[...] + g2_ref[mrow, :] * y


def _lat_mixers(x, mod, layer, caches, state_c, state_d, rope, norm2_g, w_in, w_out,
                a_qn, a_kn, a_sink, b_qn, b_kn, c_conv_w, c_conv_b, c_wa, c_ba, c_wx, c_bx,
                c_lambda, d_theta, d_norm_g):
    rows = pl.BlockSpec((DEC_SEQ, D_MODEL), lambda b: (LAT_BLOCK0 + b, 0))
    h_rows = pl.BlockSpec((DEC_SEQ, D_MODEL), lambda b: (b, 0))
    cache_spec = pl.BlockSpec((None, None, PAST_LEN, KV_W), lambda b: (b, layer, 0, 0))
    gain = _full((DEPTH, HEAD_DIM))
    table = _once((DEC_SEQ, LANES), lambda b: (0, 0))
    out_shape = jax.ShapeDtypeStruct((N_ROWS, D_MODEL), F32)
    win_cols = lambda w, c: _once((None, D_MODEL, w), lambda b: (layer, 0, c))
    wout_rows = lambda h, r: _once((None, h, D_MODEL), lambda b: (layer, r, 0))

    xn, h = pl.pallas_call(
        functools.partial(_lat_attn_kernel, layer=layer),
        grid=(DEC_BATCH,),
        in_specs=[rows, _full((DEPTH, D_MODEL)),
                  _mod_chunk(layer, 3), _mod_chunk(layer, 4), _mod_chunk(layer, 5),
                  win_cols(4 * GROUP_W, 0), wout_rows(2 * GROUP_W, 0),
                  cache_spec, cache_spec, cache_spec, cache_spec,
                  gain, gain, gain, gain,
                  pl.BlockSpec(memory_space=pltpu.SMEM),
                  table, table, table],
        out_specs=[rows, pl.BlockSpec((DEC_SEQ, D_MODEL), lambda b: (b, 0), pipeline_mode=pl.Buffered(1))],
        out_shape=[out_shape, jax.ShapeDtypeStruct((N_LAT_ROWS, D_MODEL), BF16)],
        input_output_aliases={0: 0},
        scratch_shapes=[pltpu.VMEM((DEC_SEQ, 4 * GROUP_W), F32), pltpu.VMEM((DEC_SEQ, 2 * GROUP_W), BF16)],
        compiler_params=_cparams("arbitrary"),
        name="lat_attention",
    )(x, norm2_g, mod, mod, mod, w_in, w_out, *caches, a_qn, a_kn, b_qn, b_kn, a_sink, *rope)

    scr = pltpu.VMEM((DEC_SEQ, GROUP_W), F32)
    xn = pl.pallas_call(
        functools.partial(_lat_recurrent_kernel, layer=layer),
        grid=(DEC_BATCH,),
        in_specs=[
            rows, h_rows, _mod_chunk(layer, 5),
            win_cols(2 * GROUP_W, COL_CX // (2 * GROUP_W)),
            win_cols(2 * GROUP_W, COL_DQ // (2 * GROUP_W)), win_cols(2 * GROUP_W, COL_DV // (2 * GROUP_W)),
            wout_rows(2 * GROUP_W, 1),
            pl.BlockSpec((None, None, 2, GROUP_W), lambda b: (b, layer, 0, 0)),
            _layer_block((4, GROUP_W), layer), _full((DEPTH, GROUP_W)),
            _layer_block((2, N_HEADS, HEAD_DIM, HEAD_DIM), layer), _layer_block((2, GROUP_W), layer),
            _layer_block((2, N_HEADS, HEAD_DIM, HEAD_DIM), layer), _layer_block((2, GROUP_W), layer),
            _layer_block((2, GROUP_W), layer),
            pl.BlockSpec((None, None, 2, N_HEADS, HEAD_DIM, HEAD_DIM), lambda b: (b, layer, 0, 0, 0, 0)),
            _layer_block((2, N_HEADS), layer), _full((DEPTH, GROUP_W))],
        out_specs=rows,
        out_shape=out_shape,
        input_output_aliases={0: 0},
        scratch_shapes=[pltpu.VMEM((4, GROUP_W, GROUP_W), BF16)] + [scr] * 8,
        compiler_params=_cparams("arbitrary"),
        name="lat_recurrent",
    )(xn, h, mod, w_in, w_in, w_in, w_out, state_c, c_conv_w, c_conv_b, c_wa, c_ba, c_wx, c_bx, c_lambda,
      state_d, d_theta, d_norm_g)
    return xn


def _rope_tables():
    t = np.arange(DEC_SEQ)
    row = (t // GRID_W).astype(np.float64)[:, None]
    col = (t % GRID_W).astype(np.float64)[:, None]
    half = HEAD_DIM // 2
    inv = 1.0 / (ROPE_BASE ** (np.arange(0, half, 2, dtype=np.float64) / half))
    j = np.arange(LANES) % HEAD_DIM
    ang = np.where((j < half)[None, :], row, col) * inv[j % (half // 2)][None, :]
    first = ((j % half) < half // 2)[None, :]
    cos, sin = np.cos(ang), np.sin(ang)
    return tuple(jnp.asarray(a, F32) for a in (cos, np.where(first, -sin, 0.0), np.where(first, 0.0, sin)))


def kernel(x_prompt, x_sample, cache_a_k, cache_a_v, cache_b_k, cache_b_v, state_c, state_d, c, c_ctx, norm1_g, norm2_g, norm3_g, w_mod, b_mod, ffn1_wg, ffn1_wu, ffn1_wd, ffn2_wg, ffn2_wu, ffn2_wd, w_in, w_out, a_qn, a_kn, a_sink, b_qn, b_kn, c_conv_w, c_conv_b, c_wa, c_ba, c_wx, c_bx, c_lambda, d_theta, d_norm_g):
    mod = _modulation(c_ctx, c, w_mod, b_mod)
    rope = _rope_tables()
    caches = tuple(t.reshape(DEC_BATCH, DEPTH, PAST_LEN, KV_W) for t in (cache_a_k, cache_a_v, cache_b_k, cache_b_v))
    mixer_params = (a_qn, a_kn, a_sink, b_qn, b_kn, c_conv_w, c_conv_b, c_wa, c_ba, c_wx, c_bx,
                    c_lambda, d_theta, d_norm_g)
    xs = (x_prompt.reshape(N_CTX_ROWS, D_MODEL), x_sample.reshape(N_LAT_ROWS, D_MODEL))
    states = ()
    for l in range(DEPTH):
        (x,) = _ffn(xs, mod, l, 0, norm1_g, ffn1_wg, ffn1_wu, ffn1_wd)
        x, states = _ctx_mixers(x, mod, l, states, norm2_g, w_in, w_out, *mixer_params)
        x = _lat_mixers(x, mod, l, caches, state_c, state_d, rope, norm2_g, w_in, w_out, *mixer_params)
        xs = _ffn((x,), mod, l, 6, norm3_g, ffn2_wg, ffn2_wu, ffn2_wd, split_out=(l == DEPTH - 1))
    y_p, y_s = xs
    ka, va, kb, vb, st_c, st_d = states
    kv_shape = (BATCH, DEPTH, SEQ, 2, HEAD_DIM)
    return (y_p.reshape(BATCH, SEQ, D_MODEL), y_s.reshape(DEC_BATCH, DEC_SEQ, D_MODEL),
            ka.reshape(kv_shape), va.reshape(kv_shape), kb.reshape(kv_shape), vb.reshape(kv_shape),
            st_c, st_d)
```

```python
import functools
import math

import numpy as np
import jax
import jax.numpy as jnp
from jax import lax
from jax.experimental import pallas as pl
from jax.experimental.pallas import tpu as pltpu

F32 = jnp.float32
BF16 = jnp.bfloat16

D_MODEL = 1024
BATCH = 16
SEQ = 256
DEPTH = 2
DEC_BATCH = 2
DEC_SEQ = 1024
PAST_LEN = 512
GRID_W = 64
HEAD_DIM = 64
HEAD_SHIFT = 6
N_HEADS = 4
GROUP_W = 256
KV_W = 2 * HEAD_DIM
LANES = 128
WINDOW = 128
ATT_BLOCK = 128
ROPE_BASE = 10000.0
LRU_C = 8.0
D_FF = 2816
N_MOD = 9
EPS = 1e-6
NEG_INF = -1e30
IN_WIDTH = 2560

N_CTX_ROWS = BATCH * SEQ
N_LAT_ROWS = DEC_BATCH * DEC_SEQ
N_ROWS = N_CTX_ROWS + N_LAT_ROWS
MOD_ROWS = 8
MOD_GROUP = 1024

VMEM_LIMIT_BYTES = 56 * 1024 * 1024

COL_AQ, COL_AK, COL_AV = 0, 256, 384
COL_BQ, COL_BK, COL_BV = 512, 768, 896
COL_CX, COL_CY = 1024, 1280
COL_DQ, COL_DK, COL_DV, COL_DG = 1536, 1792, 2048, 2304


def _cparams(*sem):
    return pltpu.CompilerParams(dimension_semantics=sem, vmem_limit_bytes=VMEM_LIMIT_BYTES)


def _dot(a, b):
    return jnp.dot(a, b, preferred_element_type=F32)


def _dot_nt(a, b):
    return lax.dot_general(a, b, (((1,), (1,)), ((), ())), preferred_element_type=F32)


def _dot_tn(a, b):
    return lax.dot_general(a, b, (((0,), (0,)), ((), ())), preferred_element_type=F32)


def _sigmoid(x):
    return 0.5 * jnp.tanh(0.5 * x) + 0.5


def _silu(x):
    return x * _sigmoid(x)


def _gelu_tanh(x):
    return 0.5 * x * (1.0 + jnp.tanh(math.sqrt(2.0 / math.pi) * (x + 0.044715 * (x * x * x))))


def _mod_row(i, tm, s):
    if tm >= MOD_GROUP:
        block_index = i * (tm // MOD_GROUP) + s
    else:
        block_index = i >> int(math.log2(MOD_GROUP // tm))
    return jnp.maximum(block_index - (N_CTX_ROWS // MOD_GROUP - 1), 0)


def _norm_mod(x, g, sc, sh):
    ms = jnp.mean(x * x, axis=-1, keepdims=True)
    return (x * lax.rsqrt(ms + EPS) * g) * (1.0 + sc) + sh


def _full(shape):
    return pl.BlockSpec(shape, lambda *_: (0,) * len(shape))


def _layer_block(shape, layer):
    return pl.BlockSpec((None,) + shape, lambda *_: (layer,) + (0,) * len(shape))


MOD_TN = 3072


def _mod_kernel(cc_ref, c_ref, w_ref, b_ref, o_ref):
    l = pl.program_id(0)
    pad = jnp.zeros((MOD_ROWS - 1 - DEC_BATCH, D_MODEL), F32)
    cond = jnp.concatenate([cc_ref[...], c_ref[...], pad], axis=0)
    o_ref[...] = _dot(_silu(cond).astype(BF16), w_ref[...].astype(BF16)) + b_ref[pl.ds(l, 1), :]


def _modulation(c_ctx, c, w_mod, b_mod):
    n = N_MOD * D_MODEL
    return pl.pallas_call(
        _mod_kernel,
        grid=(DEPTH, n // MOD_TN),
        in_specs=[
            pl.BlockSpec((1, D_MODEL), lambda l, j: (0, 0)),
            pl.BlockSpec((DEC_BATCH, D_MODEL), lambda l, j: (0, 0)),
            pl.BlockSpec((None, D_MODEL, MOD_TN), lambda l, j: (l, 0, j)),
            pl.BlockSpec((DEPTH, MOD_TN), lambda l, j: (0, j)),
        ],
        out_specs=pl.BlockSpec((None, MOD_ROWS, MOD_TN), lambda l, j: (l, 0, j)),
        out_shape=jax.ShapeDtypeStruct((DEPTH, MOD_ROWS, n), F32),
        compiler_params=_cparams("arbitrary", "arbitrary"),
        name="modulation",
    )(c_ctx.reshape(1, D_MODEL), c, w_mod, b_mod)


FFN_TM = 1024
FFN_TF = 256
N_CTX_TILES = N_CTX_ROWS // FFN_TM


FFN_NJ = D_FF // FFN_TF
N_FFN_TILES = N_ROWS // FFN_TM
N_FFN_STEPS = FFN_NJ + N_FFN_TILES


def _ffn_tile(step):
    return jnp.maximum(step - FFN_NJ, 0)


def _on_stream_part(tile, x_refs, o_refs, fn):
    if len(x_refs) == 1 and len(o_refs) == 1:
        fn(x_refs[0], o_refs[0])
    else:
        pl.when(tile < N_CTX_TILES)(lambda: fn(x_refs[0], o_refs[0]))
        pl.when(tile >= N_CTX_TILES)(lambda: fn(x_refs[-1], o_refs[-1]))


def _ffn_kernel(*refs, layer, n_in, n_out):
    x_refs = refs[:n_in]
    n_ref, sh_ref, sc_ref, g_ref, wg_ref, wu_ref, wd_ref = refs[n_in:n_in + 7]
    o_refs = refs[n_in + 7:n_in + 7 + n_out]
    h_ref, a_ref, wg_s, wu_s, wd_s = refs[n_in + 7 + n_out:]
    nj, tf = FFN_NJ, FFN_TF
    s = pl.program_id(0)
    tile = _ffn_tile(s)
    r = _mod_row(tile, FFN_TM, 0)

    def load_tile():
        x = x_refs[0][...] if len(x_refs) == 1 else jnp.where(tile < N_CTX_TILES, x_refs[0][...], x_refs[1][...])
        h = _norm_mod(x, n_ref[layer:layer + 1, :], sc_ref[pl.ds(r, 1), :], sh_ref[pl.ds(r, 1), :])
        h_ref[...] = h.astype(BF16)

    def up_chunk(j, cols):
        h = h_ref[...]
        a_ref[:, cols] = (_silu(_dot(h, wg_s[j])) * _dot(h, wu_s[j])).astype(BF16)

    def down_and_store():
        y = (0.5 * g_ref[pl.ds(r, 1), :]) * _dot(a_ref[...], wd_s[...])

        def store(x_ref, o_ref):
            o_ref[...] = x_ref[...] + y
        _on_stream_part(tile, x_refs, o_refs, store)

    def keep_arrived_chunk():
        wg_s[s] = wg_ref[...].astype(BF16)
        wu_s[s] = wu_ref[...].astype(BF16)
        wd_s[pl.ds(pl.multiple_of(s * tf, tf), tf), :] = wd_ref[...].astype(BF16)

    def up_previous_chunk():
        up_chunk(s - 1, pl.ds(pl.multiple_of((s - 1) * tf, tf), tf))

    @pl.when(s == 0)
    def _():
        load_tile()
        keep_arrived_chunk()

    @pl.when((s > 0) & (s < nj))
    def _():
        up_previous_chunk()
        keep_arrived_chunk()

    @pl.when(s == nj)
    def _():
        up_previous_chunk()
        down_and_store()

    @pl.when(s > nj)
    def _():
        load_tile()
        for j in range(nj):
            up_chunk(j, slice(j * tf, (j + 1) * tf))
        down_and_store()


def _stream_specs(split, buffered_once):
    tm = FFN_TM
    kw = {"pipeline_mode": pl.Buffered(1)} if buffered_once else {}
    if not split:
        return [pl.BlockSpec((tm, D_MODEL), lambda s: (_ffn_tile(s), 0), **kw)]
    last_ctx = N_CTX_TILES - 1
    return [pl.BlockSpec((tm, D_MODEL), lambda s: (jnp.minimum(_ffn_tile(s), last_ctx), 0), **kw),
            pl.BlockSpec((tm, D_MODEL), lambda s: (jnp.maximum(_ffn_tile(s) - N_CTX_TILES, 0), 0), **kw)]


def _ffn(xs, mod, layer, chunk0, norm_g, wg, wu, wd, split_out=False):
    tm, tf, nj = FFN_TM, FFN_TF, FFN_NJ
    split_in = len(xs) == 2
    mod_spec = lambda c: pl.BlockSpec((None, MOD_ROWS, D_MODEL), lambda s: (layer, 0, c))
    w_col = lambda s: (layer, 0, jnp.minimum(s, nj - 1))
    w_row = lambda s: (layer, jnp.minimum(s, nj - 1), 0)
    if split_out:
        out_shape = [jax.ShapeDtypeStruct((N_CTX_ROWS, D_MODEL), F32),
                     jax.ShapeDtypeStruct((N_LAT_ROWS, D_MODEL), F32)]
    else:
        out_shape = [jax.ShapeDtypeStruct((N_ROWS, D_MODEL), F32)]
    out = pl.pallas_call(
        functools.partial(_ffn_kernel, layer=layer, n_in=len(xs), n_out=len(out_shape)),
        grid=(N_FFN_STEPS,),
        in_specs=_stream_specs(split_in, False) + [
            _full((DEPTH, D_MODEL)),
            mod_spec(chunk0), mod_spec(chunk0 + 1), mod_spec(chunk0 + 2),
            pl.BlockSpec((None, D_MODEL, tf), w_col),
            pl.BlockSpec((None, D_MODEL, tf), w_col),
            pl.BlockSpec((None, tf, D_MODEL), w_row),
        ],
        out_specs=_stream_specs(split_out, True),
        out_shape=out_shape,
        scratch_shapes=[pltpu.VMEM((tm, D_MODEL), BF16),
                        pltpu.VMEM((tm, D_FF), BF16),
                        pltpu.VMEM((nj, D_MODEL, tf), BF16),
                        pltpu.VMEM((nj, D_MODEL, tf), BF16),
                        pltpu.VMEM((D_FF, D_MODEL), BF16)],
        compiler_params=_cparams("arbitrary"),
        name="ffn",
    )(*xs, norm_g, mod, mod, mod, wg, wu, wd)
    return tuple(out)


def _once(shape, index_map):
    return pl.BlockSpec(shape, index_map, pipeline_mode=pl.Buffered(1))


def _mod_chunk(layer, c):
    return pl.BlockSpec((None, MOD_ROWS, D_MODEL), lambda *_: (layer, 0, c))


def _head_mean_square(x):
    n = x.shape[-1]
    r = lax.broadcasted_iota(jnp.int32, (n, n), 0) >> HEAD_SHIFT
    c = lax.broadcasted_iota(jnp.int32, (n, n), 1) >> HEAD_SHIFT
    ones_bd = jnp.where(r == c, 1.0, 0.0).astype(BF16)
    return _dot((x * x).astype(BF16), ones_bd) * (1.0 / HEAD_DIM)


def _head_norm(x, head_gain):
    gain_row = jnp.concatenate([head_gain] * (x.shape[-1] // HEAD_DIM), axis=-1)
    return x * lax.rsqrt(_head_mean_square(x) + EPS) * gain_row


def _head_cols(x, h):
    return x[:, h * HEAD_DIM:(h + 1) * HEAD_DIM].astype(BF16)


def _softmax_pv(scores, values, sink):
    m = jnp.max(scores[0], axis=-1, keepdims=True)
    for s in scores[1:]:
        m = jnp.maximum(m, jnp.max(s, axis=-1, keepdims=True))
    if sink is not None:
        m = jnp.maximum(m, sink)
    denom = None
    acc = None
    for s, v in zip(scores, values):
        p = jnp.exp(s - m)
        d = jnp.sum(p, axis=-1, keepdims=True)
        o = _dot(p.astype(BF16), v)
        denom = d if denom is None else denom + d
        acc = o if acc is None else acc + o
    if sink is not None:
        denom = denom + jnp.exp(sink - m)
    return acc / denom


def _rope(x, cos, sin_lo, sin_hi):
    cols = []
    for c in range(x.shape[-1] // LANES):
        xc = x[:, c * LANES:(c + 1) * LANES]
        cols.append(xc * cos + pltpu.roll(xc, 112, 1) * sin_lo + pltpu.roll(xc, 16, 1) * sin_hi)
    return cols[0] if len(cols) == 1 else jnp.concatenate(cols, axis=-1)


def _block_diag(blocks):
    n = len(blocks)
    w = blocks[0].shape[0]
    rows = []
    for k, blk in enumerate(blocks):
        parts = []
        if k > 0:
            parts.append(jnp.zeros((w, k * w), F32))
        parts.append(blk)
        if k < n - 1:
            parts.append(jnp.zeros((w, (n - 1 - k) * w), F32))
        rows.append(jnp.concatenate(parts, axis=-1))
    return jnp.concatenate(rows, axis=0)


def _rglru_gates(xc, wa, ba, wx, bx, lam):
    xb = xc.astype(BF16)
    r = _sigmoid(_dot(xb, wa) + ba)
    i = _sigmoid(_dot(xb, wx) + bx)
    softplus = jnp.maximum(-lam, 0.0) + jnp.log1p(jnp.exp(-jnp.abs(lam)))
    log_a = (-LRU_C) * r * softplus
    a = jnp.exp(log_a)
    b = jnp.sqrt(1.0 - a * a) * (i * xc)
    return a, b


def _block_prefix(a, b, reverse):
    t = a.shape[0]
    row = lax.broadcasted_iota(jnp.int32, a.shape, 0) & 7
    for d in (1, 2, 4):
        if reverse:
            a_s = pltpu.roll(a, t - d, 0)
            b_s = pltpu.roll(b, t - d, 0)
            ok = row < 8 - d
        else:
            a_s = pltpu.roll(a, d, 0)
            b_s = pltpu.roll(b, d, 0)
            ok = row >= d
        b = jnp.where(ok, a * b_s + b, b)
        a = jnp.where(ok, a * a_s, a)
    return a, b


def _conv4(x, w_ref, b_row):
    t = x.shape[0]
    row = lax.broadcasted_iota(jnp.int32, x.shape, 0)
    xm2 = jnp.where(row >= 2, pltpu.roll(x, 2, 0), 0.0)
    xm1 = jnp.where(row >= 1, pltpu.roll(x, 1, 0), 0.0)
    xp1 = jnp.where(row < t - 1, pltpu.roll(x, t - 1, 0), 0.0)
    return (xm2 * w_ref[0:1, :] + xm1 * w_ref[1:2, :] + x * w_ref[2:3, :] + xp1 * w_ref[3:4, :]) + b_row


def _rglru_prepare(cx, cy, conv_w_ref, conv_b, gate_w_ref, ba_ref, bx_ref, lam_ref,
                   af_ref, bf_ref, ab_ref, bb_ref, gel_ref):
    xc = _conv4(cx, conv_w_ref, conv_b)
    a, b = _rglru_gates(xc, gate_w_ref[0], ba_ref[0:1, :], gate_w_ref[1], bx_ref[0:1, :], lam_ref[0:1, :])
    a, b = _block_prefix(a, b, reverse=False)
    af_ref[...] = a
    bf_ref[...] = b
    a, b = _rglru_gates(xc, gate_w_ref[2], ba_ref[1:2, :], gate_w_ref[3], bx_ref[1:2, :], lam_ref[1:2, :])
    a, b = _block_prefix(a, b, reverse=True)
    ab_ref[...] = a
    bb_ref[...] = b
    gel_ref[...] = _gelu_tanh(cy)


SCAN_UNROLL = 8


def _rglru_finish(h0f, h0b, af_ref, bf_ref, ab_ref, bb_ref, hf_ref, hb_ref, gel_ref):
    nblk = af_ref.shape[0] // 8

    def body(k, carry):
        cf, cb = carry
        rf = pl.ds(pl.multiple_of(k * 8, 8), 8)
        hf = bf_ref[rf, :] + af_ref[rf, :] * cf
        hf_ref[rf, :] = hf
        rb = pl.ds(pl.multiple_of((nblk - 1 - k) * 8, 8), 8)
        hb = bb_ref[rb, :] + ab_ref[rb, :] * cb
        hb_ref[rb, :] = hb
        return hf[7:8, :], hb[0:1, :]

    cf, cb = lax.fori_loop(0, nblk, body, (h0f, h0b), unroll=SCAN_UNROLL)
    oc = (hf_ref[...] + hb_ref[...]) * gel_ref[...]
    return oc, cf, cb


def _store_gate_weights(gate_w_ref, wa_ref, wx_ref):
    for d in range(2):
        gate_w_ref[2 * d] = _block_diag([wa_ref[d, n] for n in range(N_HEADS)]).astype(BF16)
        gate_w_ref[2 * d + 1] = _block_diag([wx_ref[d, n] for n in range(N_HEADS)]).astype(BF16)


def _lane_head_masks(n):
    lane = lax.broadcasted_iota(jnp.int32, (1, n), 1) >> HEAD_SHIFT
    return [jnp.where(lane == h, 1.0, 0.0) for h in range(n // HEAD_DIM)]


def _log_decays(theta_ref, masks):
    theta = theta_ref[...]
    lanes = theta[:, 0:1] * masks[0]
    for h in range(1, N_HEADS):
        lanes = lanes + theta[:, h:h + 1] * masks[h]
    lg = jnp.log1p(-jnp.exp(lanes))
    return lg[0:1, :], lg[1:2, :]


RET_BLOCK = 256


def _retention(q, k8, vb, s0, lgf, lgb, masks, o_ref):
    t, w = q.shape
    c = RET_BLOCK
    nh = w // HEAD_DIM
    pos = lax.broadcasted_iota(jnp.int32, (c, w), 0).astype(F32)
    q_dec = (jnp.exp(lgf * (pos + 1.0)), jnp.exp(lgb * (float(c) - pos)))
    k_dec = (jnp.exp(lgf * (float(c - 1) - pos)), jnp.exp(lgb * pos))
    chunk_dec = (jnp.exp(lgf * float(c)), jnp.exp(lgb * float(c)))
    rel = (lax.broadcasted_iota(jnp.int32, (c, c), 0) - lax.broadcasted_iota(jnp.int32, (c, c), 1)).astype(F32)
    decs = []
    for h in range(nh):
        gf = lgf[:, h * HEAD_DIM:h * HEAD_DIM + 1]
        gb = lgb[:, h * HEAD_DIM:h * HEAD_DIM + 1]
        e = jnp.exp(jnp.where(rel >= 0, gf * rel, gb * (-rel)))
        decs.append(jnp.where(rel == 0, 2.0, e))
    dec = jnp.concatenate(decs, axis=0)
    r_head = lax.broadcasted_iota(jnp.int32, (w, w), 0) >> HEAD_SHIFT
    c_head = lax.broadcasted_iota(jnp.int32, (w, w), 1) >> HEAD_SHIFT
    same_head = jnp.where(r_head == c_head, 1.0, 0.0)
    states = [None, None] if s0 is None else list(s0)

    def carry(d, rows, o):
        if states[d] is not None:
            o = o + _dot((q[rows, :] * q_dec[d]).astype(BF16), states[d].astype(BF16))
        upd = _dot_tn((k8[rows, :] * k_dec[d]).astype(BF16), vb[rows, :]) * same_head
        states[d] = upd if states[d] is None else states[d] * chunk_dec[d] + upd
        return o

    for ci in range(t // c):
        rows = slice(ci * c, (ci + 1) * c)
        qc = q[rows, :]
        q_stack = jnp.concatenate([(qc * masks[h]).astype(BF16) for h in range(nh)], axis=0)
        inner = (_dot_nt(q_stack, k8[rows, :].astype(BF16)) * dec).astype(BF16)
        out = _dot(inner, vb[rows, :])
        o = out[0:c, :] * masks[0]
        for h in range(1, nh):
            o = o + out[h * c:(h + 1) * c, :] * masks[h]
        o_ref[rows, :] = carry(0, rows, o)
    for ci in reversed(range(t // c)):
        rows = slice(ci * c, (ci + 1) * c)
        if states[1] is not None:
            o_ref[rows, :] = carry(1, rows, o_ref[rows, :])
        else:
            carry(1, rows, None)
    return states[0], states[1]


def _ctx_mixer_kernel(*refs, layer, n_prev):
    prev_refs = refs[:n_prev]
    (x_ref, n2_ref, sh_ref, sc_ref, g2_ref, win_ref, wout_ref,
     aqn_ref, akn_ref, bqn_ref, bkn_ref, sink_ref,
     convw_ref, convb_ref, wa_ref, ba_ref, wx_ref, bx_ref, lam_ref, theta_ref, dn_ref,
     xn_ref, *state_refs) = refs[n_prev:n_prev + 28]
    (win_s, wout_s, u_ref, mixed_ref,
     gate_w_ref, af_ref, bf_ref, ab_ref, bb_ref, hf_ref, hb_ref, gel_ref, ret_ref) = refs[n_prev + 28:]
    t = SEQ
    lrow = slice(layer, layer + 1)
    for prev_ref, state_ref in zip(prev_refs, state_refs):
        for earlier in range(layer):
            state_ref[earlier] = prev_ref[earlier]
    ka_ref, va_ref, kb_ref, vb_ref, stc_ref, std_ref = (ref.at[layer] for ref in state_refs)

    @pl.when(pl.program_id(0) == 0)
    def _():
        for c0 in range(0, IN_WIDTH, 2 * GROUP_W):
            win_s[:, c0:c0 + 2 * GROUP_W] = win_ref[:, c0:c0 + 2 * GROUP_W].astype(BF16)
        wout_s[...] = wout_ref[...].astype(BF16)
        _store_gate_weights(gate_w_ref, wa_ref, wx_ref)

    x = x_ref[...]
    h = _norm_mod(x, n2_ref[lrow, :], sc_ref[0:1, :], sh_ref[0:1, :]).astype(BF16)
    c_cols = slice(COL_CX, COL_CX + 2 * GROUP_W)
    u_ref[:, c_cols] = _dot(h, win_s[:, c_cols])
    _rglru_prepare(u_ref[:, COL_CX:COL_CX + GROUP_W], u_ref[:, COL_CY:COL_CY + GROUP_W],
                   convw_ref, convb_ref[lrow, :], gate_w_ref, ba_ref, bx_ref, lam_ref,
                   af_ref, bf_ref, ab_ref, bb_ref, gel_ref)
    u_ref[:, 0:COL_CX] = _dot(h, win_s[:, 0:COL_CX])
    u_ref[:, COL_DQ:IN_WIDTH] = _dot(h, win_s[:, COL_DQ:IN_WIDTH])

    for (cq, ck, cv, qn_ref, kn_ref, k_out, v_out, col0, use_sink) in (
            (COL_AQ, COL_AK, COL_AV, aqn_ref, akn_ref, ka_ref, va_ref, 0, True),
            (COL_BQ, COL_BK, COL_BV, bqn_ref, bkn_ref, kb_ref, vb_ref, GROUP_W, False)):
        q = _head_norm(u_ref[:, cq:cq + GROUP_W], qn_ref[lrow, :])
        k = _head_norm(u_ref[:, ck:ck + KV_W], kn_ref[lrow, :])
        v = u_ref[:, cv:cv + KV_W]
        k_out[...] = k
        v_out[...] = v
        qs = q * (HEAD_DIM ** -0.5)
        heads = []
        for hd in range(N_HEADS):
            kv = hd // 2
            s = _dot_nt(_head_cols(qs, hd), _head_cols(k, kv))
            sink = jnp.full((t, 1), sink_ref[layer, hd], F32) if use_sink else None
            heads.append(_softmax_pv([s], [_head_cols(v, kv)], sink))
        mixed_ref[:, col0:col0 + GROUP_W] = jnp.concatenate(heads, axis=-1).astype(BF16)

    zero = jnp.zeros((1, GROUP_W), F32)
    oc, cf, cb = _rglru_finish(zero, zero, af_ref, bf_ref, ab_ref, bb_ref, hf_ref, hb_ref, gel_ref)
    mixed_ref[:, 2 * GROUP_W:3 * GROUP_W] = oc.astype(BF16)
    stc_ref[0:1, :] = cf
    stc_ref[1:2, :] = cb

    masks = _lane_head_masks(GROUP_W)
    lgf, lgb = _log_decays(theta_ref, masks)
    k8 = u_ref[:, COL_DK:COL_DK + GROUP_W] * (HEAD_DIM ** -0.5)
    vb = u_ref[:, COL_DV:COL_DV + GROUP_W].astype(BF16)
    final_states = _retention(u_ref[:, COL_DQ:COL_DQ + GROUP_W], k8, vb, None, lgf, lgb, masks, ret_ref)
    o = ret_ref[...]
    o = o * lax.rsqrt(_head_mean_square(o) + EPS) * dn_ref[lrow, :] * _silu(u_ref[:, COL_DG:COL_DG + GROUP_W])
    mixed_ref[:, 3 * GROUP_W:4 * GROUP_W] = o.astype(BF16)
    for d, s_full in enumerate(final_states):
        for hd in range(N_HEADS):
            std_ref[d, hd] = s_full[hd * HEAD_DIM:(hd + 1) * HEAD_DIM, hd * HEAD_DIM:(hd + 1) * HEAD_DIM]

    xn_ref[...] = x + g2_ref[0:1, :] * _dot(mixed_ref[...], wout_s[...])


def _ctx_mixers(x, mod, layer, prev, norm2_g, w_in, w_out,
                a_qn, a_kn, a_sink, b_qn, b_kn, c_conv_w, c_conv_b, c_wa, c_ba, c_wx, c_bx,
                c_lambda, d_theta, d_norm_g):
    per_request = lambda slots, shape: pl.BlockSpec((None, slots) + shape, lambda b: (b,) + (0,) * (1 + len(shape)))
    state_dims = [(SEQ, KV_W)] * 4 + [(2, GROUP_W), (2, N_HEADS, HEAD_DIM, HEAD_DIM)]
    scr = pltpu.VMEM((SEQ, GROUP_W), F32)
    out = pl.pallas_call(
        functools.partial(_ctx_mixer_kernel, layer=layer, n_prev=len(prev)),
        grid=(BATCH,),
        in_specs=[per_request(layer, dims) for dims in state_dims[:len(prev)]] + [
            pl.BlockSpec((SEQ, D_MODEL), lambda b: (b, 0)),
            _full((DEPTH, D_MODEL)),
            _mod_chunk(layer, 3), _mod_chunk(layer, 4), _mod_chunk(layer, 5),
            _once((None, D_MODEL, IN_WIDTH), lambda b: (layer, 0, 0)),
            _once((None, D_MODEL, D_MODEL), lambda b: (layer, 0, 0)),
            _full((DEPTH, HEAD_DIM)), _full((DEPTH, HEAD_DIM)), _full((DEPTH, HEAD_DIM)), _full((DEPTH, HEAD_DIM)),
            pl.BlockSpec(memory_space=pltpu.SMEM),
            _layer_block((4, GROUP_W), layer), _full((DEPTH, GROUP_W)),
            _layer_block((2, N_HEADS, HEAD_DIM, HEAD_DIM), layer), _layer_block((2, GROUP_W), layer),
            _layer_block((2, N_HEADS, HEAD_DIM, HEAD_DIM), layer), _layer_block((2, GROUP_W), layer),
            _layer_block((2, GROUP_W), layer),
            _layer_block((2, N_HEADS), layer), _full((DEPTH, GROUP_W)),
        ],
        out_specs=[pl.BlockSpec((SEQ, D_MODEL), lambda b: (b, 0))] + [
            per_request(layer + 1, dims) for dims in state_dims],
        out_shape=[jax.ShapeDtypeStruct((N_ROWS, D_MODEL), F32)] + [
            jax.ShapeDtypeStruct((BATCH, layer + 1) + dims, F32) for dims in state_dims],
        input_output_aliases={len(prev): 0},
        scratch_shapes=[pltpu.VMEM((D_MODEL, IN_WIDTH), BF16), pltpu.VMEM((D_MODEL, D_MODEL), BF16),
                        pltpu.VMEM((SEQ, IN_WIDTH), F32), pltpu.VMEM((SEQ, D_MODEL), BF16),
                        pltpu.VMEM((4, GROUP_W, GROUP_W), BF16)] + [scr] * 8,
        compiler_params=_cparams("arbitrary"),
        name="ctx_mixers",
    )(*prev, x, norm2_g, mod, mod, mod, w_in, w_out,
      a_qn, a_kn, b_qn, b_kn, a_sink, c_conv_w, c_conv_b, c_wa, c_ba, c_wx, c_bx,
      c_lambda, d_theta, d_norm_g)
    return out[0], tuple(out[1:])


LAT_BLOCK0 = N_CTX_ROWS // DEC_SEQ


def _lat_attn_kernel(x_ref, n2_ref, sh_ref, sc_ref, g2_ref, win_ref, wout_ref,
                     kca_ref, vca_ref, kcb_ref, vcb_ref,
                     aqn_ref, akn_ref, bqn_ref, bkn_ref, sink_ref, cos_ref, sinl_ref, sinh_ref,
                     xn_ref, h_ref, u_ref, o_ref, *, layer):
    t = DEC_SEQ
    lrow = slice(layer, layer + 1)
    mrow = pl.ds(1 + pl.program_id(0), 1)
    cos, sin_lo, sin_hi = cos_ref[...], sinl_ref[...], sinh_ref[...]
    scale = HEAD_DIM ** -0.5
    x = x_ref[...]
    h_ref[...] = _norm_mod(x, n2_ref[lrow, :], sc_ref[mrow, :], sh_ref[mrow, :]).astype(BF16)
    u_ref[...] = _dot(h_ref[...], win_ref[...].astype(BF16))

    q = _rope(_head_norm(u_ref[:, COL_AQ:COL_AQ + GROUP_W], aqn_ref[lrow, :]), cos, sin_lo, sin_hi)
    k = _rope(_head_norm(u_ref[:, COL_AK:COL_AK + KV_W], akn_ref[lrow, :]), cos, sin_lo, sin_hi)
    qh = [_head_cols(q * scale, h) for h in range(4)]
    v = u_ref[:, COL_AV:COL_AV + KV_W]
    kh = [_head_cols(k, kv) for kv in range(2)]
    vh = [_head_cols(v, kv) for kv in range(2)]
    kch = [_head_cols(kca_ref[...], kv) for kv in range(2)]
    vch = [_head_cols(vca_ref[...], kv) for kv in range(2)]
    w = ATT_BLOCK
    span = 3 * w
    for n in range(t // w):
        start = min(max((n - 1) * w, 0), t - span)
        rows = slice(n * w, (n + 1) * w)
        band = slice(start, start + span)
        qpos = (lax.broadcasted_iota(jnp.int32, (2 * w, span), 0) & (w - 1)) + n * w
        kpos = lax.broadcasted_iota(jnp.int32, (2 * w, span), 1) + start
        valid = jnp.abs(qpos - kpos) <= WINDOW
        heads = []
        for kv in range(2):
            qp = jnp.concatenate([qh[2 * kv][rows, :], qh[2 * kv + 1][rows, :]], axis=0)
            s_ctx = _dot_nt(qp, kch[kv])
            s_band = jnp.where(valid, _dot_nt(qp, kh[kv][band, :]), NEG_INF)
            row = lax.broadcasted_iota(jnp.int32, (2 * w, 1), 0)
            sink = jnp.where(row < w, sink_ref[layer, 2 * kv], sink_ref[layer, 2 * kv + 1])
            o = _softmax_pv([s_ctx, s_band], [vch[kv], vh[kv][band, :]], sink)
            heads += [o[0:w, :], o[w:2 * w, :]]
        o_ref[rows, 0:GROUP_W] = jnp.concatenate(heads, axis=-1).astype(BF16)

    q = _rope(_head_norm(u_ref[:, COL_BQ:COL_BQ + GROUP_W], bqn_ref[lrow, :]), cos, sin_lo, sin_hi)
    k = _rope(_head_norm(u_ref[:, COL_BK:COL_BK + KV_W], bkn_ref[lrow, :]), cos, sin_lo, sin_hi)
    qh = [_head_cols(q * scale, h) for h in range(4)]
    v = u_ref[:, COL_BV:COL_BV + KV_W]
    kh = [_head_cols(k, kv) for kv in range(2)]
    vh = [_head_cols(v, kv) for kv in range(2)]
    kch = [_head_cols(kcb_ref[...], kv) for kv in range(2)]
    vch = [_head_cols(vcb_ref[...], kv) for kv in range(2)]
    tq = 2 * ATT_BLOCK
    for n in range(t // tq):
        rows = slice(n * tq, (n + 1) * tq)
        heads = []
        for kv in range(2):
            qp = jnp.concatenate([qh[2 * kv][rows, :], qh[2 * kv + 1][rows, :]], axis=0)
            o = _softmax_pv([_dot_nt(qp, kch[kv]), _dot_nt(qp, kh[kv])], [vch[kv], vh[kv]], None)
            heads += [o[0:tq, :], o[tq:2 * tq, :]]
        o_ref[rows, GROUP_W:2 * GROUP_W] = jnp.concatenate(heads, axis=-1).astype(BF16)

    xn_ref[...] = x + g2_ref[mrow, :] * _dot(o_ref[...], wout_ref[...].astype(BF16))


def _lat_recurrent_kernel(xn_in_ref, h_ref, g2_ref, wc_ref, wqk_ref, wvg_ref, wout_ref, h0_ref,
                          convw_ref, convb_ref, wa_ref, ba_ref, wx_ref, bx_ref, lam_ref,
                          s0_ref, theta_ref, dn_ref,
                          xn_ref, gate_w_ref, af_ref, bf_ref, ab_ref, bb_ref, hf_ref, hb_ref, gel_ref, ret_ref,
                          *, layer):
    lrow = slice(layer, layer + 1)
    mrow = pl.ds(1 + pl.program_id(0), 1)

    @pl.when(pl.program_id(0) == 0)
    def _():
        _store_gate_weights(gate_w_ref, wa_ref, wx_ref)

    h = h_ref[...]
    u = _dot(h, wc_ref[...].astype(BF16))
    _rglru_prepare(u[:, 0:GROUP_W], u[:, GROUP_W:2 * GROUP_W], convw_ref, convb_ref[lrow, :],
                   gate_w_ref, ba_ref, bx_ref, lam_ref, af_ref, bf_ref, ab_ref, bb_ref, gel_ref)
    oc, _, _ = _rglru_finish(h0_ref[0:1, :], h0_ref[1:2, :],
                             af_ref, bf_ref, ab_ref, bb_ref, hf_ref, hb_ref, gel_ref)
    y = _dot(oc.astype(BF16), wout_ref[0:GROUP_W, :].astype(BF16))

    uqk = _dot(h, wqk_ref[...].astype(BF16))
    uvg = _dot(h, wvg_ref[...].astype(BF16))
    masks = _lane_head_masks(GROUP_W)
    lgf, lgb = _log_decays(theta_ref, masks)
    s0 = tuple(_block_diag([s0_ref[d, hd] for hd in range(N_HEADS)]) for d in range(2))
    _retention(uqk[:, 0:GROUP_W], uqk[:, GROUP_W:2 * GROUP_W] * (HEAD_DIM ** -0.5),
               uvg[:, 0:GROUP_W].astype(BF16), s0, lgf, lgb, masks, ret_ref)
    o = ret_ref[...]
    o = o * lax.rsqrt(_head_mean_square(o) + EPS) * dn_ref[lrow, :] * _silu(uvg[:, GROUP_W:2 * GROUP_W])
    y = y + _dot(o.astype(BF16), wout_ref[GROUP_W:2 * GROUP_W, :].astype(BF16))
    xn_ref[...] = xn_in_ref[...] + g2_ref[mrow, :] * y


def _lat_mixers(x, mod, layer, caches, state_c, state_d, rope, norm2_g, w_in, w_out,
                a_qn, a_kn, a_sink, b_qn, b_kn, c_conv_w, c_conv_b, c_wa, c_ba, c_wx, c_bx,
                c_lambda, d_theta, d_norm_g):
    rows = pl.BlockSpec((DEC_SEQ, D_MODEL), lambda b: (LAT_BLOCK0 + b, 0))
    h_rows = pl.BlockSpec((DEC_SEQ, D_MODEL), lambda b: (b, 0))
    cache_spec = pl.BlockSpec((None, None, PAST_LEN, KV_W), lambda b: (b, layer, 0, 0))
    gain = _full((DEPTH, HEAD_DIM))
    table = _once((DEC_SEQ, LANES), lambda b: (0, 0))
    out_shape = jax.ShapeDtypeStruct((N_ROWS, D_MODEL), F32)
    win_cols = lambda w, c: _once((None, D_MODEL, w), lambda b: (layer, 0, c))
    wout_rows = lambda h, r: _once((None, h, D_MODEL), lambda b: (layer, r, 0))

    xn, h = pl.pallas_call(
        functools.partial(_lat_attn_kernel, layer=layer),
        grid=(DEC_BATCH,),
        in_specs=[rows, _full((DEPTH, D_MODEL)),
                  _mod_chunk(layer, 3), _mod_chunk(layer, 4), _mod_chunk(layer, 5),
                  win_cols(4 * GROUP_W, 0), wout_rows(2 * GROUP_W, 0),
                  cache_spec, cache_spec, cache_spec, cache_spec,
                  gain, gain, gain, gain,
                  pl.BlockSpec(memory_space=pltpu.SMEM),
                  table, table, table],
        out_specs=[rows, pl.BlockSpec((DEC_SEQ, D_MODEL), lambda b: (b, 0), pipeline_mode=pl.Buffered(1))],
        out_shape=[out_shape, jax.ShapeDtypeStruct((N_LAT_ROWS, D_MODEL), BF16)],
        input_output_aliases={0: 0},
        scratch_shapes=[pltpu.VMEM((DEC_SEQ, 4 * GROUP_W), F32), pltpu.VMEM((DEC_SEQ, 2 * GROUP_W), BF16)],
        compiler_params=_cparams("arbitrary"),
        name="lat_attention",
    )(x, norm2_g, mod, mod, mod, w_in, w_out, *caches, a_qn, a_kn, b_qn, b_kn, a_sink, *rope)

    scr = pltpu.VMEM((DEC_SEQ, GROUP_W), F32)
    xn = pl.pallas_call(
        functools.partial(_lat_recurrent_kernel, layer=layer),
        grid=(DEC_BATCH,),
        in_specs=[
            rows, h_rows, _mod_chunk(layer, 5),
            win_cols(2 * GROUP_W, COL_CX // (2 * GROUP_W)),
            win_cols(2 * GROUP_W, COL_DQ // (2 * GROUP_W)), win_cols(2 * GROUP_W, COL_DV // (2 * GROUP_W)),
            wout_rows(2 * GROUP_W, 1),
            pl.BlockSpec((None, None, 2, GROUP_W), lambda b: (b, layer, 0, 0)),
            _layer_block((4, GROUP_W), layer), _full((DEPTH, GROUP_W)),
            _layer_block((2, N_HEADS, HEAD_DIM, HEAD_DIM), layer), _layer_block((2, GROUP_W), layer),
            _layer_block((2, N_HEADS, HEAD_DIM, HEAD_DIM), layer), _layer_block((2, GROUP_W), layer),
            _layer_block((2, GROUP_W), layer),
            pl.BlockSpec((None, None, 2, N_HEADS, HEAD_DIM, HEAD_DIM), lambda b: (b, layer, 0, 0, 0, 0)),
            _layer_block((2, N_HEADS), layer), _full((DEPTH, GROUP_W))],
        out_specs=rows,
        out_shape=out_shape,
        input_output_aliases={0: 0},
        scratch_shapes=[pltpu.VMEM((4, GROUP_W, GROUP_W), BF16)] + [scr] * 8,
        compiler_params=_cparams("arbitrary"),
        name="lat_recurrent",
    )(xn, h, mod, w_in, w_in, w_in, w_out, state_c, c_conv_w, c_conv_b, c_wa, c_ba, c_wx, c_bx, c_lambda,
      state_d, d_theta, d_norm_g)
    return xn


def _rope_tables():
    t = np.arange(DEC_SEQ)
    row = (t // GRID_W).astype(np.float64)[:, None]
    col = (t % GRID_W).astype(np.float64)[:, None]
    half = HEAD_DIM // 2
    inv = 1.0 / (ROPE_BASE ** (np.arange(0, half, 2, dtype=np.float64) / half))
    j = np.arange(LANES) % HEAD_DIM
    ang = np.where((j < half)[None, :], row, col) * inv[j % (half // 2)][None, :]
    first = ((j % half) < half // 2)[None, :]
    cos, sin = np.cos(ang), np.sin(ang)
    return tuple(jnp.asarray(a, F32) for a in (cos, np.where(first, -sin, 0.0), np.where(first, 0.0, sin)))


def kernel(x_prompt, x_sample, cache_a_k, cache_a_v, cache_b_k, cache_b_v, state_c, state_d, c, c_ctx, norm1_g, norm2_g, norm3_g, w_mod, b_mod, ffn1_wg, ffn1_wu, ffn1_wd, ffn2_wg, ffn2_wu, ffn2_wd, w_in, w_out, a_qn, a_kn, a_sink, b_qn, b_kn, c_conv_w, c_conv_b, c_wa, c_ba, c_wx, c_bx, c_lambda, d_theta, d_norm_g):
    mod = _modulation(c_ctx, c, w_mod, b_mod)
    rope = _rope_tables()
    caches = tuple(t.reshape(DEC_BATCH, DEPTH, PAST_LEN, KV_W) for t in (cache_a_k, cache_a_v, cache_b_k, cache_b_v))
    mixer_params = (a_qn, a_kn, a_sink, b_qn, b_kn, c_conv_w, c_conv_b, c_wa, c_ba, c_wx, c_bx,
                    c_lambda, d_theta, d_norm_g)
    xs = (x_prompt.reshape(N_CTX_ROWS, D_MODEL), x_sample.reshape(N_LAT_ROWS, D_MODEL))
    states = ()
    for l in range(DEPTH):
        (x,) = _ffn(xs, mod, l, 0, norm1_g, ffn1_wg, ffn1_wu, ffn1_wd)
        x, states = _ctx_mixers(x, mod, l, states, norm2_g, w_in, w_out, *mixer_params)
        x = _lat_mixers(x, mod, l, caches, state_c, state_d, rope, norm2_g, w_in, w_out, *mixer_params)
        xs = _ffn((x,), mod, l, 6, norm3_g, ffn2_wg, ffn2_wu, ffn2_wd, split_out=(l == DEPTH - 1))
    y_p, y_s = xs
    ka, va, kb, vb, st_c, st_d = states
    kv_shape = (BATCH, DEPTH, SEQ, 2, HEAD_DIM)
    return (y_p.reshape(BATCH, SEQ, D_MODEL), y_s.reshape(DEC_BATCH, DEC_SEQ, D_MODEL),
            ka.reshape(kv_shape), va.reshape(kv_shape), kb.reshape(kv_shape), vb.reshape(kv_shape),
            st_c, st_d)
```

```python
import functools
import math

import numpy as np
import jax
import jax.numpy as jnp
from jax import lax
from jax.experimental import pallas as pl
from jax.experimental.pallas import tpu as pltpu

F32 = jnp.float32
BF16 = jnp.bfloat16

D_MODEL = 1024
BATCH = 16
SEQ = 256
DEPTH = 2
DEC_BATCH = 2
DEC_SEQ = 1024
PAST_LEN = 512
GRID_W = 64
HEAD_DIM = 64
HEAD_SHIFT = 6
N_HEADS = 4
GROUP_W = 256
KV_W = 2 * HEAD_DIM
LANES = 128
WINDOW = 128
ATT_BLOCK = 128
ROPE_BASE = 10000.0
LRU_C = 8.0
D_FF = 2816
N_MOD = 9
EPS = 1e-6
NEG_INF = -1e30
IN_WIDTH = 2560

N_CTX_ROWS = BATCH * SEQ
N_LAT_ROWS = DEC_BATCH * DEC_SEQ
N_ROWS = N_CTX_ROWS + N_LAT_ROWS
MOD_ROWS = 8
MOD_GROUP = 1024

VMEM_LIMIT_BYTES = 56 * 1024 * 1024

COL_AQ, COL_AK, COL_AV = 0, 256, 384
COL_BQ, COL_BK, COL_BV = 512, 768, 896
COL_CX, COL_CY = 1024, 1280
COL_DQ, COL_DK, COL_DV, COL_DG = 1536, 1792, 2048, 2304


def _cparams(*sem):
    return pltpu.CompilerParams(dimension_semantics=sem, vmem_limit_bytes=VMEM_LIMIT_BYTES)


def _dot(a, b):
    return jnp.dot(a, b, preferred_element_type=F32)


def _dot_nt(a, b):
    return lax.dot_general(a, b, (((1,), (1,)), ((), ())), preferred_element_type=F32)


def _dot_tn(a, b):
    return lax.dot_general(a, b, (((0,), (0,)), ((), ())), preferred_element_type=F32)


def _sigmoid(x):
    return 0.5 * jnp.tanh(0.5 * x) + 0.5


def _silu(x):
    return x * _sigmoid(x)


def _gelu_tanh(x):
    return 0.5 * x * (1.0 + jnp.tanh(math.sqrt(2.0 / math.pi) * (x + 0.044715 * (x * x * x))))


def _mod_row(i, tm, s):
    if tm >= MOD_GROUP:
        block_index = i * (tm // MOD_GROUP) + s
    else:
        block_index = i >> int(math.log2(MOD_GROUP // tm))
    return jnp.maximum(block_index - (N_CTX_ROWS // MOD_GROUP - 1), 0)


def _norm_mod(x, g, sc, sh):
    ms = jnp.mean(x * x, axis=-1, keepdims=True)
    return (x * lax.rsqrt(ms + EPS) * g) * (1.0 + sc) + sh


def _full(shape):
    return pl.BlockSpec(shape, lambda *_: (0,) * len(shape))


def _layer_block(shape, layer):
    return pl.BlockSpec((None,) + shape, lambda *_: (layer,) + (0,) * len(shape))


MOD_TN = 3072


def _mod_kernel(cc_ref, c_ref, w_ref, b_ref, o_ref):
    l = pl.program_id(0)
    pad = jnp.zeros((MOD_ROWS - 1 - DEC_BATCH, D_MODEL), F32)
    cond = jnp.concatenate([cc_ref[...], c_ref[...], pad], axis=0)
    o_ref[...] = _dot(_silu(cond).astype(BF16), w_ref[...].astype(BF16)) + b_ref[pl.ds(l, 1), :]


def _modulation(c_ctx, c, w_mod, b_mod):
    n = N_MOD * D_MODEL
    return pl.pallas_call(
        _mod_kernel,
        grid=(DEPTH, n // MOD_TN),
        in_specs=[
            pl.BlockSpec((1, D_MODEL), lambda l, j: (0, 0)),
            pl.BlockSpec((DEC_BATCH, D_MODEL), lambda l, j: (0, 0)),
            pl.BlockSpec((None, D_MODEL, MOD_TN), lambda l, j: (l, 0, j)),
            pl.BlockSpec((DEPTH, MOD_TN), lambda l, j: (0, j)),
        ],
        out_specs=pl.BlockSpec((None, MOD_ROWS, MOD_TN), lambda l, j: (l, 0, j)),
        out_shape=jax.ShapeDtypeStruct((DEPTH, MOD_ROWS, n), F32),
        compiler_params=_cparams("arbitrary", "arbitrary"),
        name="modulation",
    )(c_ctx.reshape(1, D_MODEL), c, w_mod, b_mod)


FFN_TM = 1024
FFN_TF = 256
N_CTX_TILES = N_CTX_ROWS // FFN_TM


FFN_NJ = D_FF // FFN_TF
N_FFN_TILES = N_ROWS // FFN_TM
N_FFN_STEPS = FFN_NJ + N_FFN_TILES


def _ffn_tile(step):
    return jnp.maximum(step - FFN_NJ, 0)


def _on_stream_part(tile, x_refs, o_refs, fn):
    if len(x_refs) == 1 and len(o_refs) == 1:
        fn(x_refs[0], o_refs[0])
    else:
        pl.when(tile < N_CTX_TILES)(lambda: fn(x_refs[0], o_refs[0]))
        pl.when(tile >= N_CTX_TILES)(lambda: fn(x_refs[-1], o_refs[-1]))


def _ffn_kernel(*refs, layer, n_in, n_out):
    x_refs = refs[:n_in]
    n_ref, sh_ref, sc_ref, g_ref, wg_ref, wu_ref, wd_ref = refs[n_in:n_in + 7]
    o_refs = refs[n_in + 7:n_in + 7 + n_out]
    h_ref, a_ref, wg_s, wu_s, wd_s = refs[n_in + 7 + n_out:]
    nj, tf = FFN_NJ, FFN_TF
    s = pl.program_id(0)
    tile = _ffn_tile(s)
    r = _mod_row(tile, FFN_TM, 0)

    def load_tile():
        x = x_refs[0][...] if len(x_refs) == 1 else jnp.where(tile < N_CTX_TILES, x_refs[0][...], x_refs[1][...])
        h = _norm_mod(x, n_ref[layer:layer + 1, :], sc_ref[pl.ds(r, 1), :], sh_ref[pl.ds(r, 1), :])
        h_ref[...] = h.astype(BF16)

    def up_chunk(j, cols):
        h = h_ref[...]
        a_ref[:, cols] = (_silu(_dot(h, wg_s[j])) * _dot(h, wu_s[j])).astype(BF16)

    def down_and_store():
        y = (0.5 * g_ref[pl.ds(r, 1), :]) * _dot(a_ref[...], wd_s[...])

        def store(x_ref, o_ref):
            o_ref[...] = x_ref[...] + y
        _on_stream_part(tile, x_refs, o_refs, store)

    def keep_arrived_chunk():
        wg_s[s] = wg_ref[...].astype(BF16)
        wu_s[s] = wu_ref[...].astype(BF16)
        wd_s[pl.ds(pl.multiple_of(s * tf, tf), tf), :] = wd_ref[...].astype(BF16)

    def up_previous_chunk():
        up_chunk(s - 1, pl.ds(pl.multiple_of((s - 1) * tf, tf), tf))

    @pl.when(s == 0)
    def _():
        load_tile()
        keep_arrived_chunk()

    @pl.when((s > 0) & (s < nj))
    def _():
        up_previous_chunk()
        keep_arrived_chunk()

    @pl.when(s == nj)
    def _():
        up_previous_chunk()
        down_and_store()

    @pl.when(s > nj)
    def _():
        load_tile()
        for j in range(nj):
            up_chunk(j, slice(j * tf, (j + 1) * tf))
        down_and_store()


def _stream_specs(split, buffered_once):
    tm = FFN_TM
    kw = {"pipeline_mode": pl.Buffered(1)} if buffered_once else {}
    if not split:
        return [pl.BlockSpec((tm, D_MODEL), lambda s: (_ffn_tile(s), 0), **kw)]
    last_ctx = N_CTX_TILES - 1
    return [pl.BlockSpec((tm, D_MODEL), lambda s: (jnp.minimum(_ffn_tile(s), last_ctx), 0), **kw),
            pl.BlockSpec((tm, D_MODEL), lambda s: (jnp.maximum(_ffn_tile(s) - N_CTX_TILES, 0), 0), **kw)]


def _ffn(xs, mod, layer, chunk0, norm_g, wg, wu, wd, split_out=False):
    tm, tf, nj = FFN_TM, FFN_TF, FFN_NJ
    split_in = len(xs) == 2
    mod_spec = lambda c: pl.BlockSpec((None, MOD_ROWS, D_MODEL), lambda s: (layer, 0, c))
    w_col = lambda s: (layer, 0, jnp.minimum(s, nj - 1))
    w_row = lambda s: (layer, jnp.minimum(s, nj - 1), 0)
    if split_out:
        out_shape = [jax.ShapeDtypeStruct((N_CTX_ROWS, D_MODEL), F32),
                     jax.ShapeDtypeStruct((N_LAT_ROWS, D_MODEL), F32)]
    else:
        out_shape = [jax.ShapeDtypeStruct((N_ROWS, D_MODEL), F32)]
    out = pl.pallas_call(
        functools.partial(_ffn_kernel, layer=layer, n_in=len(xs), n_out=len(out_shape)),
        grid=(N_FFN_STEPS,),
        in_specs=_stream_specs(split_in, False) + [
            _full((DEPTH, D_MODEL)),
            mod_spec(chunk0), mod_spec(chunk0 + 1), mod_spec(chunk0 + 2),
            pl.BlockSpec((None, D_MODEL, tf), w_col),
            pl.BlockSpec((None, D_MODEL, tf), w_col),
            pl.BlockSpec((None, tf, D_MODEL), w_row),
        ],
        out_specs=_stream_specs(split_out, True),
        out_shape=out_shape,
        scratch_shapes=[pltpu.VMEM((tm, D_MODEL), BF16),
                        pltpu.VMEM((tm, D_FF), BF16),
                        pltpu.VMEM((nj, D_MODEL, tf), BF16),
                        pltpu.VMEM((nj, D_MODEL, tf), BF16),
                        pltpu.VMEM((D_FF, D_MODEL), BF16)],
        compiler_params=_cparams("arbitrary"),
        name="ffn",
    )(*xs, norm_g, mod, mod, mod, wg, wu, wd)
    return tuple(out)


def _once(shape, index_map):
    return pl.BlockSpec(shape, index_map, pipeline_mode=pl.Buffered(1))


def _mod_chunk(layer, c):
    return pl.BlockSpec((None, MOD_ROWS, D_MODEL), lambda *_: (layer, 0, c))


def _head_mean_square(x):
    n = x.shape[-1]
    r = lax.broadcasted_iota(jnp.int32, (n, n), 0) >> HEAD_SHIFT
    c = lax.broadcasted_iota(jnp.int32, (n, n), 1) >> HEAD_SHIFT
    ones_bd = jnp.where(r == c, 1.0, 0.0).astype(BF16)
    return _dot((x * x).astype(BF16), ones_bd) * (1.0 / HEAD_DIM)


def _head_norm(x, head_gain):
    gain_row = jnp.concatenate([head_gain] * (x.shape[-1] // HEAD_DIM), axis=-1)
    return x * lax.rsqrt(_head_mean_square(x) + EPS) * gain_row


def _head_cols(x, h):
    return x[:, h * HEAD_DIM:(h + 1) * HEAD_DIM].astype(BF16)


def _softmax_pv(scores, values, sink):
    m = jnp.max(scores[0], axis=-1, keepdims=True)
    for s in scores[1:]:
        m = jnp.maximum(m, jnp.max(s, axis=-1, keepdims=True))
    if sink is not None:
        m = jnp.maximum(m, sink)
    denom = None
    acc = None
    for s, v in zip(scores, values):
        p = jnp.exp(s - m)
        d = jnp.sum(p, axis=-1, keepdims=True)
        o = _dot(p.astype(BF16), v)
        denom = d if denom is None else denom + d
        acc = o if acc is None else acc + o
    if sink is not None:
        denom = denom + jnp.exp(sink - m)
    return acc / denom


def _rope(x, cos, sin_lo, sin_hi):
    cols = []
    for c in range(x.shape[-1] // LANES):
        xc = x[:, c * LANES:(c + 1) * LANES]
        cols.append(xc * cos + pltpu.roll(xc, 112, 1) * sin_lo + pltpu.roll(xc, 16, 1) * sin_hi)
    return cols[0] if len(cols) == 1 else jnp.concatenate(cols, axis=-1)


def _block_diag(blocks):
    n = len(blocks)
    w = blocks[0].shape[0]
    rows = []
    for k, blk in enumerate(blocks):
        parts = []
        if k > 0:
            parts.append(jnp.zeros((w, k * w), F32))
        parts.append(blk)
        if k < n - 1:
            parts.append(jnp.zeros((w, (n - 1 - k) * w), F32))
        rows.append(jnp.concatenate(parts, axis=-1))
    return jnp.concatenate(rows, axis=0)


def _rglru_gates(xc, wa, ba, wx, bx, lam):
    xb = xc.astype(BF16)
    r = _sigmoid(_dot(xb, wa) + ba)
    i = _sigmoid(_dot(xb, wx) + bx)
    softplus = jnp.maximum(-lam, 0.0) + jnp.log1p(jnp.exp(-jnp.abs(lam)))
    log_a = (-LRU_C) * r * softplus
    a = jnp.exp(log_a)
    b = jnp.sqrt(1.0 - a * a) * (i * xc)
    return a, b


def _block_prefix(a, b, reverse):
    t = a.shape[0]
    row = lax.broadcasted_iota(jnp.int32, a.shape, 0) & 7
    for d in (1, 2, 4):
        if reverse:
            a_s = pltpu.roll(a, t - d, 0)
            b_s = pltpu.roll(b, t - d, 0)
            ok = row < 8 - d
        else:
            a_s = pltpu.roll(a, d, 0)
            b_s = pltpu.roll(b, d, 0)
            ok = row >= d
        b = jnp.where(ok, a * b_s + b, b)
        a = jnp.where(ok, a * a_s, a)
    return a, b


def _conv4(x, w_ref, b_row):
    t = x.shape[0]
    row = lax.broadcasted_iota(jnp.int32, x.shape, 0)
    xm2 = jnp.where(row >= 2, pltpu.roll(x, 2, 0), 0.0)
    xm1 = jnp.where(row >= 1, pltpu.roll(x, 1, 0), 0.0)
    xp1 = jnp.where(row < t - 1, pltpu.roll(x, t - 1, 0), 0.0)
    return (xm2 * w_ref[0:1, :] + xm1 * w_ref[1:2, :] + x * w_ref[2:3, :] + xp1 * w_ref[3:4, :]) + b_row


def _rglru_prepare(cx, cy, conv_w_ref, conv_b, gate_w_ref, ba_ref, bx_ref, lam_ref,
                   af_ref, bf_ref, ab_ref, bb_ref, gel_ref):
    xc = _conv4(cx, conv_w_ref, conv_b)
    a, b = _rglru_gates(xc, gate_w_ref[0], ba_ref[0:1, :], gate_w_ref[1], bx_ref[0:1, :], lam_ref[0:1, :])
    a, b = _block_prefix(a, b, reverse=False)
    af_ref[...] = a
    bf_ref[...] = b
    a, b = _rglru_gates(xc, gate_w_ref[2], ba_ref[1:2, :], gate_w_ref[3], bx_ref[1:2, :], lam_ref[1:2, :])
    a, b = _block_prefix(a, b, reverse=True)
    ab_ref[...] = a
    bb_ref[...] = b
    gel_ref[...] = _gelu_tanh(cy)


SCAN_UNROLL = 8


def _rglru_finish(h0f, h0b, af_ref, bf_ref, ab_ref, bb_ref, hf_ref, hb_ref, gel_ref):
    nblk = af_ref.shape[0] // 8

    def body(k, carry):
        cf, cb = carry
        rf = pl.ds(pl.multiple_of(k * 8, 8), 8)
        hf = bf_ref[rf, :] + af_ref[rf, :] * cf
        hf_ref[rf, :] = hf
        rb = pl.ds(pl.multiple_of((nblk - 1 - k) * 8, 8), 8)
        hb = bb_ref[rb, :] + ab_ref[rb, :] * cb
        hb_ref[rb, :] = hb
        return hf[7:8, :], hb[0:1, :]

    cf, cb = lax.fori_loop(0, nblk, body, (h0f, h0b), unroll=SCAN_UNROLL)
    oc = (hf_ref[...] + hb_ref[...]) * gel_ref[...]
    return oc, cf, cb


def _store_gate_weights(gate_w_ref, wa_ref, wx_ref):
    for d in range(2):
        gate_w_ref[2 * d] = _block_diag([wa_ref[d, n] for n in range(N_HEADS)]).astype(BF16)
        gate_w_ref[2 * d + 1] = _block_diag([wx_ref[d, n] for n in range(N_HEADS)]).astype(BF16)


def _lane_head_masks(n):
    lane = lax.broadcasted_iota(jnp.int32, (1, n), 1) >> HEAD_SHIFT
    return [jnp.where(lane == h, 1.0, 0.0) for h in range(n // HEAD_DIM)]


def _log_decays(theta_ref, masks):
    theta = theta_ref[...]
    lanes = theta[:, 0:1] * masks[0]
    for h in range(1, N_HEADS):
        lanes = lanes + theta[:, h:h + 1] * masks[h]
    lg = jnp.log1p(-jnp.exp(lanes))
    return lg[0:1, :], lg[1:2, :]


RET_BLOCK = 256


def _retention(q, k8, vb, s0, lgf, lgb, masks, o_ref):
    t, w = q.shape
    c = RET_BLOCK
    nh = w // HEAD_DIM
    pos = lax.broadcasted_iota(jnp.int32, (c, w), 0).astype(F32)
    q_dec = (jnp.exp(lgf * (pos + 1.0)), jnp.exp(lgb * (float(c) - pos)))
    k_dec = (jnp.exp(lgf * (float(c - 1) - pos)), jnp.exp(lgb * pos))
    chunk_dec = (jnp.exp(lgf * float(c)), jnp.exp(lgb * float(c)))
    rel = (lax.broadcasted_iota(jnp.int32, (c, c), 0) - lax.broadcasted_iota(jnp.int32, (c, c), 1)).astype(F32)
    decs = []
    for h in range(nh):
        gf = lgf[:, h * HEAD_DIM:h * HEAD_DIM + 1]
        gb = lgb[:, h * HEAD_DIM:h * HEAD_DIM + 1]
        e = jnp.exp(jnp.where(rel >= 0, gf * rel, gb * (-rel)))
        decs.append(jnp.where(rel == 0, 2.0, e))
    dec = jnp.concatenate(decs, axis=0)
    r_head = lax.broadcasted_iota(jnp.int32, (w, w), 0) >> HEAD_SHIFT
    c_head = lax.broadcasted_iota(jnp.int32, (w, w), 1) >> HEAD_SHIFT
    same_head = jnp.where(r_head == c_head, 1.0, 0.0)
    states = [None, None] if s0 is None else list(s0)

    def carry(d, rows, o):
        if states[d] is not None:
            o = o + _dot((q[rows, :] * q_dec[d]).astype(BF16), states[d].astype(BF16))
        upd = _dot_tn((k8[rows, :] * k_dec[d]).astype(BF16), vb[rows, :]) * same_head
        states[d] = upd if states[d] is None else states[d] * chunk_dec[d] + upd
        return o

    for ci in range(t // c):
        rows = slice(ci * c, (ci + 1) * c)
        qc = q[rows, :]
        q_stack = jnp.concatenate([(qc * masks[h]).astype(BF16) for h in range(nh)], axis=0)
        inner = (_dot_nt(q_stack, k8[rows, :].astype(BF16)) * dec).astype(BF16)
        out = _dot(inner, vb[rows, :])
        o = out[0:c, :] * masks[0]
        for h in range(1, nh):
            o = o + out[h * c:(h + 1) * c, :] * masks[h]
        o_ref[rows, :] = carry(0, rows, o)
    for ci in reversed(range(t // c)):
        rows = slice(ci * c, (ci + 1) * c)
        if states[1] is not None:
            o_ref[rows, :] = carry(1, rows, o_ref[rows, :])
        else:
            carry(1, rows, None)
    return states[0], states[1]


def _ctx_mixer_kernel(*refs, layer, n_prev):
    prev_refs = refs[:n_prev]
    (x_ref, n2_ref, sh_ref, sc_ref, g2_ref, win_ref, wout_ref,
     aqn_ref, akn_ref, bqn_ref, bkn_ref, sink_ref,
     convw_ref, convb_ref, wa_ref, ba_ref, wx_ref, bx_ref, lam_ref, theta_ref, dn_ref,
     xn_ref, *state_refs) = refs[n_prev:n_prev + 28]
    (win_s, wout_s, u_ref, mixed_ref,
     gate_w_ref, af_ref, bf_ref, ab_ref, bb_ref, hf_ref, hb_ref, gel_ref, ret_ref) = refs[n_prev + 28:]
    t = SEQ
    lrow = slice(layer, layer + 1)
    for prev_ref, state_ref in zip(prev_refs, state_refs):
        for earlier in range(layer):
            state_ref[earlier] = prev_ref[earlier]
    ka_ref, va_ref, kb_ref, vb_ref, stc_ref, std_ref = (ref.at[layer] for ref in state_refs)

    @pl.when(pl.program_id(0) == 0)
    def _():
        for c0 in range(0, IN_WIDTH, 2 * GROUP_W):
            win_s[:, c0:c0 + 2 * GROUP_W] = win_ref[:, c0:c0 + 2 * GROUP_W].astype(BF16)
        wout_s[...] = wout_ref[...].astype(BF16)
        _store_gate_weights(gate_w_ref, wa_ref, wx_ref)

    x = x_ref[...]
    h = _norm_mod(x, n2_ref[lrow, :], sc_ref[0:1, :], sh_ref[0:1, :]).astype(BF16)
    c_cols = slice(COL_CX, COL_CX + 2 * GROUP_W)
    u_ref[:, c_cols] = _dot(h, win_s[:, c_cols])
    _rglru_prepare(u_ref[:, COL_CX:COL_CX + GROUP_W], u_ref[:, COL_CY:COL_CY + GROUP_W],
                   convw_ref, convb_ref[lrow, :], gate_w_ref, ba_ref, bx_ref, lam_ref,
                   af_ref, bf_ref, ab_ref, bb_ref, gel_ref)
    u_ref[:, 0:COL_CX] = _dot(h, win_s[:, 0:COL_CX])
    u_ref[:, COL_DQ:IN_WIDTH] = _dot(h, win_s[:, COL_DQ:IN_WIDTH])

    for (cq, ck, cv, qn_ref, kn_ref, k_out, v_out, col0, use_sink) in (
            (COL_AQ, COL_AK, COL_AV, aqn_ref, akn_ref, ka_ref, va_ref, 0, True),
            (COL_BQ, COL_BK, COL_BV, bqn_ref, bkn_ref, kb_ref, vb_ref, GROUP_W, False)):
        q = _head_norm(u_ref[:, cq:cq + GROUP_W], qn_ref[lrow, :])
        k = _head_norm(u_ref[:, ck:ck + KV_W], kn_ref[lrow, :])
        v = u_ref[:, cv:cv + KV_W]
        k_out[...] = k
        v_out[...] = v
        qs = q * (HEAD_DIM ** -0.5)
        heads = []
        for hd in range(N_HEADS):
            kv = hd // 2
            s = _dot_nt(_head_cols(qs, hd), _head_cols(k, kv))
            sink = jnp.full((t, 1), sink_ref[layer, hd], F32) if use_sink else None
            heads.append(_softmax_pv([s], [_head_cols(v, kv)], sink))
        mixed_ref[:, col0:col0 + GROUP_W] = jnp.concatenate(heads, axis=-1).astype(BF16)

    zero = jnp.zeros((1, GROUP_W), F32)
    oc, cf, cb = _rglru_finish(zero, zero, af_ref, bf_ref, ab_ref, bb_ref, hf_ref, hb_ref, gel_ref)
    mixed_ref[:, 2 * GROUP_W:3 * GROUP_W] = oc.astype(BF16)
    stc_ref[0:1, :] = cf
    stc_ref[1:2, :] = cb

    masks = _lane_head_masks(GROUP_W)
    lgf, lgb = _log_decays(theta_ref, masks)
    k8 = u_ref[:, COL_DK:COL_DK + GROUP_W] * (HEAD_DIM ** -0.5)
    vb = u_ref[:, COL_DV:COL_DV + GROUP_W].astype(BF16)
    final_states = _retention(u_ref[:, COL_DQ:COL_DQ + GROUP_W], k8, vb, None, lgf, lgb, masks, ret_ref)
    o = ret_ref[...]
    o = o * lax.rsqrt(_head_mean_square(o) + EPS) * dn_ref[lrow, :] * _silu(u_ref[:, COL_DG:COL_DG + GROUP_W])
    mixed_ref[:, 3 * GROUP_W:4 * GROUP_W] = o.astype(BF16)
    for d, s_full in enumerate(final_states):
        for hd in range(N_HEADS):
            std_ref[d, hd] = s_full[hd * HEAD_DIM:(hd + 1) * HEAD_DIM, hd * HEAD_DIM:(hd + 1) * HEAD_DIM]

    xn_ref[...] = x + g2_ref[0:1, :] * _dot(mixed_ref[...], wout_s[...])


def _ctx_mixers(x, mod, layer, prev, norm2_g, w_in, w_out,
                a_qn, a_kn, a_sink, b_qn, b_kn, c_conv_w, c_conv_b, c_wa, c_ba, c_wx, c_bx,
                c_lambda, d_theta, d_norm_g):
    per_request = lambda slots, shape: pl.BlockSpec((None, slots) + shape, lambda b: (b,) + (0,) * (1 + len(shape)))
    state_dims = [(SEQ, KV_W)] * 4 + [(2, GROUP_W), (2, N_HEADS, HEAD_DIM, HEAD_DIM)]
    scr = pltpu.VMEM((SEQ, GROUP_W), F32)
    out = pl.pallas_call(
        functools.partial(_ctx_mixer_kernel, layer=layer, n_prev=len(prev)),
        grid=(BATCH,),
        in_specs=[per_request(layer, dims) for dims in state_dims[:len(prev)]] + [
            pl.BlockSpec((SEQ, D_MODEL), lambda b: (b, 0)),
            _full((DEPTH, D_MODEL)),
            _mod_chunk(layer, 3), _mod_chunk(layer, 4), _mod_chunk(layer, 5),
            _once((None, D_MODEL, IN_WIDTH), lambda b: (layer, 0, 0)),
            _once((None, D_MODEL, D_MODEL), lambda b: (layer, 0, 0)),
            _full((DEPTH, HEAD_DIM)), _full((DEPTH, HEAD_DIM)), _full((DEPTH, HEAD_DIM)), _full((DEPTH, HEAD_DIM)),
            pl.BlockSpec(memory_space=pltpu.SMEM),
            _layer_block((4, GROUP_W), layer), _full((DEPTH, GROUP_W)),
            _layer_block((2, N_HEADS, HEAD_DIM, HEAD_DIM), layer), _layer_block((2, GROUP_W), layer),
            _layer_block((2, N_HEADS, HEAD_DIM, HEAD_DIM), layer), _layer_block((2, GROUP_W), layer),
            _layer_block((2, GROUP_W), layer),
            _layer_block((2, N_HEADS), layer), _full((DEPTH, GROUP_W)),
        ],
        out_specs=[pl.BlockSpec((SEQ, D_MODEL), lambda b: (b, 0))] + [
            per_request(layer + 1, dims) for dims in state_dims],
        out_shape=[jax.ShapeDtypeStruct((N_ROWS, D_MODEL), F32)] + [
            jax.ShapeDtypeStruct((BATCH, layer + 1) + dims, F32) for dims in state_dims],
        input_output_aliases={len(prev): 0},
        scratch_shapes=[pltpu.VMEM((D_MODEL, IN_WIDTH), BF16), pltpu.VMEM((D_MODEL, D_MODEL), BF16),
                        pltpu.VMEM((SEQ, IN_WIDTH), F32), pltpu.VMEM((SEQ, D_MODEL), BF16),
                        pltpu.VMEM((4, GROUP_W, GROUP_W), BF16)] + [scr] * 8,
        compiler_params=_cparams("arbitrary"),
        name="ctx_mixers",
    )(*prev, x, norm2_g, mod, mod, mod, w_in, w_out,
      a_qn, a_kn, b_qn, b_kn, a_sink, c_conv_w, c_conv_b, c_wa, c_ba, c_wx, c_bx,
      c_lambda, d_theta, d_norm_g)
    return out[0], tuple(out[1:])


LAT_BLOCK0 = N_CTX_ROWS // DEC_SEQ


def _lat_attn_kernel(x_ref, n2_ref, sh_ref, sc_ref, g2_ref, win_ref, wout_ref,
                     kca_ref, vca_ref, kcb_ref, vcb_ref,
                     aqn_ref, akn_ref, bqn_ref, bkn_ref, sink_ref, cos_ref, sinl_ref, sinh_ref,
                     xn_ref, h_ref, u_ref, o_ref, *, layer):
    t = DEC_SEQ
    lrow = slice(layer, layer + 1)
    mrow = pl.ds(1 + pl.program_id(0), 1)
    cos, sin_lo, sin_hi = cos_ref[...], sinl_ref[...], sinh_ref[...]
    scale = HEAD_DIM ** -0.5
    x = x_ref[...]
    h_ref[...] = _norm_mod(x, n2_ref[lrow, :], sc_ref[mrow, :], sh_ref[mrow, :]).astype(BF16)
    u_ref[...] = _dot(h_ref[...], win_ref[...].astype(BF16))

    q = _rope(_head_norm(u_ref[:, COL_AQ:COL_AQ + GROUP_W], aqn_ref[lrow, :]), cos, sin_lo, sin_hi)
    k = _rope(_head_norm(u_ref[:, COL_AK:COL_AK + KV_W], akn_ref[lrow, :]), cos, sin_lo, sin_hi)
    qh = [_head_cols(q * scale, h) for h in range(4)]
    v = u_ref[:, COL_AV:COL_AV + KV_W]
    kh = [_head_cols(k, kv) for kv in range(2)]
    vh = [_head_cols(v, kv) for kv in range(2)]
    kch = [kca_ref[:, kv, :].astype(BF16) for kv in range(2)]
    vch = [vca_ref[:, kv, :].astype(BF16) for kv in range(2)]
    w = ATT_BLOCK
    span = 3 * w
    for n in range(t // w):
        start = min(max((n - 1) * w, 0), t - span)
        rows = slice(n * w, (n + 1) * w)
        band = slice(start, start + span)
        qpos = (lax.broadcasted_iota(jnp.int32, (2 * w, span), 0) & (w - 1)) + n * w
        kpos = lax.broadcasted_iota(jnp.int32, (2 * w, span), 1) + start
        valid = jnp.abs(qpos - kpos) <= WINDOW
        heads = []
        for kv in range(2):
            qp = jnp.concatenate([qh[2 * kv][rows, :], qh[2 * kv + 1][rows, :]], axis=0)
            s_ctx = _dot_nt(qp, kch[kv])
            s_band = jnp.where(valid, _dot_nt(qp, kh[kv][band, :]), NEG_INF)
            row = lax.broadcasted_iota(jnp.int32, (2 * w, 1), 0)
            sink = jnp.where(row < w, sink_ref[layer, 2 * kv], sink_ref[layer, 2 * kv + 1])
            o = _softmax_pv([s_ctx, s_band], [vch[kv], vh[kv][band, :]], sink)
            heads += [o[0:w, :], o[w:2 * w, :]]
        o_ref[rows, 0:GROUP_W] = jnp.concatenate(heads, axis=-1).astype(BF16)

    q = _rope(_head_norm(u_ref[:, COL_BQ:COL_BQ + GROUP_W], bqn_ref[lrow, :]), cos, sin_lo, sin_hi)
    k = _rope(_head_norm(u_ref[:, COL_BK:COL_BK + KV_W], bkn_ref[lrow, :]), cos, sin_lo, sin_hi)
    qh = [_head_cols(q * scale, h) for h in range(4)]
    v = u_ref[:, COL_BV:COL_BV + KV_W]
    kh = [_head_cols(k, kv) for kv in range(2)]
    vh = [_head_cols(v, kv) for kv in range(2)]
    kch = [kcb_ref[:, kv, :].astype(BF16) for kv in range(2)]
    vch = [vcb_ref[:, kv, :].astype(BF16) for kv in range(2)]
    tq = 2 * ATT_BLOCK
    for n in range(t // tq):
        rows = slice(n * tq, (n + 1) * tq)
        heads = []
        for kv in range(2):
            qp = jnp.concatenate([qh[2 * kv][rows, :], qh[2 * kv + 1][rows, :]], axis=0)
            o = _softmax_pv([_dot_nt(qp, kch[kv]), _dot_nt(qp, kh[kv])], [vch[kv], vh[kv]], None)
            heads += [o[0:tq, :], o[tq:2 * tq, :]]
        o_ref[rows, GROUP_W:2 * GROUP_W] = jnp.concatenate(heads, axis=-1).astype(BF16)

    xn_ref[...] = x + g2_ref[mrow, :] * _dot(o_ref[...], wout_ref[...].astype(BF16))


def _lat_recurrent_kernel(xn_in_ref, h_ref, g2_ref, wc_ref, wqk_ref, wvg_ref, wout_ref, h0_ref,
                          convw_ref, convb_ref, wa_ref, ba_ref, wx_ref, bx_ref, lam_ref,
                          s0_ref, theta_ref, dn_ref,
                          xn_ref, gate_w_ref, af_ref, bf_ref, ab_ref, bb_ref, hf_ref, hb_ref, gel_ref, ret_ref,
                          *, layer):
    lrow = slice(layer, layer + 1)
    mrow = pl.ds(1 + pl.program_id(0), 1)

    @pl.when(pl.program_id(0) == 0)
    def _():
        _store_gate_weights(gate_w_ref, wa_ref, wx_ref)

    h = h_ref[...]
    u = _dot(h, wc_ref[...].astype(BF16))
    _rglru_prepare(u[:, 0:GROUP_W], u[:, GROUP_W:2 * GROUP_W], convw_ref, convb_ref[lrow, :],
                   gate_w_ref, ba_ref, bx_ref, lam_ref, af_ref, bf_ref, ab_ref, bb_ref, gel_ref)
    oc, _, _ = _rglru_finish(h0_ref[0:1, :], h0_ref[1:2, :],
                             af_ref, bf_ref, ab_ref, bb_ref, hf_ref, hb_ref, gel_ref)
    y = _dot(oc.astype(BF16), wout_ref[0:GROUP_W, :].astype(BF16))

    uqk = _dot(h, wqk_ref[...].astype(BF16))
    uvg = _dot(h, wvg_ref[...].astype(BF16))
    masks = _lane_head_masks(GROUP_W)
    lgf, lgb = _log_decays(theta_ref, masks)
    s0 = tuple(_block_diag([s0_ref[d, hd] for hd in range(N_HEADS)]) for d in range(2))
    _retention(uqk[:, 0:GROUP_W], uqk[:, GROUP_W:2 * GROUP_W] * (HEAD_DIM ** -0.5),
               uvg[:, 0:GROUP_W].astype(BF16), s0, lgf, lgb, masks, ret_ref)
    o = ret_ref[...]
    o = o * lax.rsqrt(_head_mean_square(o) + EPS) * dn_ref[lrow, :] * _silu(uvg[:, GROUP_W:2 * GROUP_W])
    y = y + _dot(o.astype(BF16), wout_ref[GROUP_W:2 * GROUP_W, :].astype(BF16))
    xn_ref[...] = xn_in_ref[...] + g2_ref[mrow, :] * y


def _lat_mixers(x, mod, layer, caches, state_c, state_d, rope, norm2_g, w_in, w_out,
                a_qn, a_kn, a_sink, b_qn, b_kn, c_conv_w, c_conv_b, c_wa, c_ba, c_wx, c_bx,
                c_lambda, d_theta, d_norm_g):
    rows = pl.BlockSpec((DEC_SEQ, D_MODEL), lambda b: (LAT_BLOCK0 + b, 0))
    h_rows = pl.BlockSpec((DEC_SEQ, D_MODEL), lambda b: (b, 0))
    cache_spec = pl.BlockSpec((None, None, PAST_LEN, 2, HEAD_DIM), lambda b: (b, layer, 0, 0, 0),
                              pipeline_mode=pl.Buffered(1))
    gain = _full((DEPTH, HEAD_DIM))
    table = _once((DEC_SEQ, LANES), lambda b: (0, 0))
    out_shape = jax.ShapeDtypeStruct((N_ROWS, D_MODEL), F32)
    win_cols = lambda w, c: _once((None, D_MODEL, w), lambda b: (layer, 0, c))
    wout_rows = lambda h, r: _once((None, h, D_MODEL), lambda b: (layer, r, 0))

    xn, h = pl.pallas_call(
        functools.partial(_lat_attn_kernel, layer=layer),
        grid=(DEC_BATCH,),
        in_specs=[rows, _full((DEPTH, D_MODEL)),
                  _mod_chunk(layer, 3), _mod_chunk(layer, 4), _mod_chunk(layer, 5),
                  win_cols(4 * GROUP_W, 0), wout_rows(2 * GROUP_W, 0),
                  cache_spec, cache_spec, cache_spec, cache_spec,
                  gain, gain, gain, gain,
                  pl.BlockSpec(memory_space=pltpu.SMEM),
                  table, table, table],
        out_specs=[rows, pl.BlockSpec((DEC_SEQ, D_MODEL), lambda b: (b, 0), pipeline_mode=pl.Buffered(1))],
        out_shape=[out_shape, jax.ShapeDtypeStruct((N_LAT_ROWS, D_MODEL), BF16)],
        input_output_aliases={0: 0},
        scratch_shapes=[pltpu.VMEM((DEC_SEQ, 4 * GROUP_W), F32), pltpu.VMEM((DEC_SEQ, 2 * GROUP_W), BF16)],
        compiler_params=_cparams("arbitrary"),
        name="lat_attention",
    )(x, norm2_g, mod, mod, mod, w_in, w_out, *caches, a_qn, a_kn, b_qn, b_kn, a_sink, *rope)

    scr = pltpu.VMEM((DEC_SEQ, GROUP_W), F32)
    xn = pl.pallas_call(
        functools.partial(_lat_recurrent_kernel, layer=layer),
        grid=(DEC_BATCH,),
        in_specs=[
            rows, h_rows, _mod_chunk(layer, 5),
            win_cols(2 * GROUP_W, COL_CX // (2 * GROUP_W)),
            win_cols(2 * GROUP_W, COL_DQ // (2 * GROUP_W)), win_cols(2 * GROUP_W, COL_DV // (2 * GROUP_W)),
            wout_rows(2 * GROUP_W, 1),
            pl.BlockSpec((None, None, 2, GROUP_W), lambda b: (b, layer, 0, 0)),
            _layer_block((4, GROUP_W), layer), _full((DEPTH, GROUP_W)),
            _layer_block((2, N_HEADS, HEAD_DIM, HEAD_DIM), layer), _layer_block((2, GROUP_W), layer),
            _layer_block((2, N_HEADS, HEAD_DIM, HEAD_DIM), layer), _layer_block((2, GROUP_W), layer),
            _layer_block((2, GROUP_W), layer),
            pl.BlockSpec((None, None, 2, N_HEADS, HEAD_DIM, HEAD_DIM), lambda b: (b, layer, 0, 0, 0, 0)),
            _layer_block((2, N_HEADS), layer), _full((DEPTH, GROUP_W))],
        out_specs=rows,
        out_shape=out_shape,
        input_output_aliases={0: 0},
        scratch_shapes=[pltpu.VMEM((4, GROUP_W, GROUP_W), BF16)] + [scr] * 8,
        compiler_params=_cparams("arbitrary"),
        name="lat_recurrent",
    )(xn, h, mod, w_in, w_in, w_in, w_out, state_c, c_conv_w, c_conv_b, c_wa, c_ba, c_wx, c_bx, c_lambda,
      state_d, d_theta, d_norm_g)
    return xn


def _rope_tables():
    t = np.arange(DEC_SEQ)
    row = (t // GRID_W).astype(np.float64)[:, None]
    col = (t % GRID_W).astype(np.float64)[:, None]
    half = HEAD_DIM // 2
    inv = 1.0 / (ROPE_BASE ** (np.arange(0, half, 2, dtype=np.float64) / half))
    j = np.arange(LANES) % HEAD_DIM
    ang = np.where((j < half)[None, :], row, col) * inv[j % (half // 2)][None, :]
    first = ((j % half) < half // 2)[None, :]
    cos, sin = np.cos(ang), np.sin(ang)
    return tuple(jnp.asarray(a, F32) for a in (cos, np.where(first, -sin, 0.0), np.where(first, 0.0, sin)))


def kernel(x_prompt, x_sample, cache_a_k, cache_a_v, cache_b_k, cache_b_v, state_c, state_d, c, c_ctx, norm1_g, norm2_g, norm3_g, w_mod, b_mod, ffn1_wg, ffn1_wu, ffn1_wd, ffn2_wg, ffn2_wu, ffn2_wd, w_in, w_out, a_qn, a_kn, a_sink, b_qn, b_kn, c_conv_w, c_conv_b, c_wa, c_ba, c_wx, c_bx, c_lambda, d_theta, d_norm_g):
    mod = _modulation(c_ctx, c, w_mod, b_mod)
    rope = _rope_tables()
    caches = (cache_a_k, cache_a_v, cache_b_k, cache_b_v)
    mixer_params = (a_qn, a_kn, a_sink, b_qn, b_kn, c_conv_w, c_conv_b, c_wa, c_ba, c_wx, c_bx,
                    c_lambda, d_theta, d_norm_g)
    xs = (x_prompt.reshape(N_CTX_ROWS, D_MODEL), x_sample.reshape(N_LAT_ROWS, D_MODEL))
    states = ()
    for l in range(DEPTH):
        (x,) = _ffn(xs, mod, l, 0, norm1_g, ffn1_wg, ffn1_wu, ffn1_wd)
        x, states = _ctx_mixers(x, mod, l, states, norm2_g, w_in, w_out, *mixer_params)
        x = _lat_mixers(x, mod, l, caches, state_c, state_d, rope, norm2_g, w_in, w_out, *mixer_params)
        xs = _ffn((x,), mod, l, 6, norm3_g, ffn2_wg, ffn2_wu, ffn2_wd, split_out=(l == DEPTH - 1))
    y_p, y_s = xs
    ka, va, kb, vb, st_c, st_d = states
    kv_shape = (BATCH, DEPTH, SEQ, 2, HEAD_DIM)
    return (y_p.reshape(BATCH, SEQ, D_MODEL), y_s.reshape(DEC_BATCH, DEC_SEQ, D_MODEL),
            ka.reshape(kv_shape), va.reshape(kv_shape), kb.reshape(kv_shape), vb.reshape(kv_shape),
            st_c, st_d)
```

```python
import functools
import math

import numpy as np
import jax
import jax.numpy as jnp
from jax import lax
from jax.experimental import pallas as pl
from jax.experimental.pallas import tpu as pltpu

F32 = jnp.float32
BF16 = jnp.bfloat16

D_MODEL = 1024
BATCH = 16
SEQ = 256
DEPTH = 2
DEC_BATCH = 2
DEC_SEQ = 1024
PAST_LEN = 512
GRID_W = 64
HEAD_DIM = 64
HEAD_SHIFT = 6
N_HEADS = 4
GROUP_W = 256
KV_W = 2 * HEAD_DIM
LANES = 128
WINDOW = 128
ATT_BLOCK = 128
ROPE_BASE = 10000.0
LRU_C = 8.0
D_FF = 2816
N_MOD = 9
EPS = 1e-6
NEG_INF = -1e30
IN_WIDTH = 2560

N_CTX_ROWS = BATCH * SEQ
N_LAT_ROWS = DEC_BATCH * DEC_SEQ
N_ROWS = N_CTX_ROWS + N_LAT_ROWS
MOD_ROWS = 8
MOD_GROUP = 1024

VMEM_LIMIT_BYTES = 56 * 1024 * 1024

COL_AQ, COL_AK, COL_AV = 0, 256, 384
COL_BQ, COL_BK, COL_BV = 512, 768, 896
COL_CX, COL_CY = 1024, 1280
COL_DQ, COL_DK, COL_DV, COL_DG = 1536, 1792, 2048, 2304


def _cparams(*sem):
    return pltpu.CompilerParams(dimension_semantics=sem, vmem_limit_bytes=VMEM_LIMIT_BYTES)


def _dot(a, b):
    return jnp.dot(a, b, preferred_element_type=F32)


def _dot_nt(a, b):
    return lax.dot_general(a, b, (((1,), (1,)), ((), ())), preferred_element_type=F32)


def _dot_tn(a, b):
    return lax.dot_general(a, b, (((0,), (0,)), ((), ())), preferred_element_type=F32)


def _sigmoid(x):
    return 0.5 * jnp.tanh(0.5 * x) + 0.5


def _silu(x):
    return x * _sigmoid(x)


def _gelu_tanh(x):
    return 0.5 * x * (1.0 + jnp.tanh(math.sqrt(2.0 / math.pi) * (x + 0.044715 * (x * x * x))))


def _mod_row(i, tm, s):
    if tm >= MOD_GROUP:
        block_index = i * (tm // MOD_GROUP) + s
    else:
        block_index = i >> int(math.log2(MOD_GROUP // tm))
    return jnp.maximum(block_index - (N_CTX_ROWS // MOD_GROUP - 1), 0)


def _norm_mod(x, g, sc, sh):
    ms = jnp.mean(x * x, axis=-1, keepdims=True)
    return (x * lax.rsqrt(ms + EPS) * g) * (1.0 + sc) + sh


def _full(shape):
    return pl.BlockSpec(shape, lambda *_: (0,) * len(shape))


def _layer_block(shape, layer):
    return pl.BlockSpec((None,) + shape, lambda *_: (layer,) + (0,) * len(shape))


MOD_TN = 3072


def _mod_kernel(cc_ref, c_ref, w_top_ref, w_bot_ref, b_ref, o_ref):
    l = pl.program_id(0)
    half = D_MODEL // 2
    pad = jnp.zeros((MOD_ROWS - 1 - DEC_BATCH, D_MODEL), F32)
    s = _silu(jnp.concatenate([cc_ref[...], c_ref[...], pad], axis=0)).astype(BF16)
    o_ref[...] = (_dot(s[:, 0:half], w_top_ref[...].astype(BF16))
                  + _dot(s[:, half:D_MODEL], w_bot_ref[...].astype(BF16)) + b_ref[pl.ds(l, 1), :])


def _modulation(c_ctx, c, w_mod, b_mod):
    n = N_MOD * D_MODEL
    return pl.pallas_call(
        _mod_kernel,
        grid=(DEPTH, n // MOD_TN),
        in_specs=[
            pl.BlockSpec((1, D_MODEL), lambda l, j: (0, 0)),
            pl.BlockSpec((DEC_BATCH, D_MODEL), lambda l, j: (0, 0)),
            pl.BlockSpec((None, D_MODEL // 2, MOD_TN), lambda l, j: (l, 0, j)),
            pl.BlockSpec((None, D_MODEL // 2, MOD_TN), lambda l, j: (l, 1, j)),
            pl.BlockSpec((DEPTH, MOD_TN), lambda l, j: (0, j)),
        ],
        out_specs=pl.BlockSpec((None, MOD_ROWS, MOD_TN), lambda l, j: (l, 0, j)),
        out_shape=jax.ShapeDtypeStruct((DEPTH, MOD_ROWS, n), F32),
        compiler_params=_cparams("arbitrary", "arbitrary"),
        name="modulation",
    )(c_ctx.reshape(1, D_MODEL), c, w_mod, w_mod, b_mod)


FFN_TM = 1024
FFN_TF = 256
N_CTX_TILES = N_CTX_ROWS // FFN_TM


FFN_NJ = D_FF // FFN_TF
N_FFN_TILES = N_ROWS // FFN_TM
N_FFN_STEPS = FFN_NJ + N_FFN_TILES


def _ffn_tile(step):
    return jnp.maximum(step - FFN_NJ, 0)


def _on_stream_part(tile, x_refs, o_refs, fn):
    if len(x_refs) == 1 and len(o_refs) == 1:
        fn(x_refs[0], o_refs[0])
    else:
        pl.when(tile < N_CTX_TILES)(lambda: fn(x_refs[0], o_refs[0]))
        pl.when(tile >= N_CTX_TILES)(lambda: fn(x_refs[-1], o_refs[-1]))


def _ffn_kernel(*refs, layer, n_in, n_out):
    x_refs = refs[:n_in]
    n_ref, sh_ref, sc_ref, g_ref, wg_ref, wu_ref, wd_ref = refs[n_in:n_in + 7]
    o_refs = refs[n_in + 7:n_in + 7 + n_out]
    h_ref, a_ref, wg_s, wu_s, wd_s = refs[n_in + 7 + n_out:]
    nj, tf = FFN_NJ, FFN_TF
    s = pl.program_id(0)
    tile = _ffn_tile(s)
    r = _mod_row(tile, FFN_TM, 0)

    def load_tile():
        x = x_refs[0][...] if len(x_refs) == 1 else jnp.where(tile < N_CTX_TILES, x_refs[0][...], x_refs[1][...])
        h = _norm_mod(x, n_ref[layer:layer + 1, :], sc_ref[pl.ds(r, 1), :], sh_ref[pl.ds(r, 1), :])
        h_ref[...] = h.astype(BF16)

    def up_chunk(j, cols):
        h = h_ref[...]
        a_ref[:, cols] = (_silu(_dot(h, wg_s[j])) * _dot(h, wu_s[j])).astype(BF16)

    def down_and_store():
        y = (0.5 * g_ref[pl.ds(r, 1), :]) * _dot(a_ref[...], wd_s[...])

        def store(x_ref, o_ref):
            o_ref[...] = x_ref[...] + y
        _on_stream_part(tile, x_refs, o_refs, store)

    def keep_arrived_chunk():
        wg_s[s] = wg_ref[...].astype(BF16)
        wu_s[s] = wu_ref[...].astype(BF16)
        wd_s[pl.ds(pl.multiple_of(s * tf, tf), tf), :] = wd_ref[...].astype(BF16)

    def up_previous_chunk():
        up_chunk(s - 1, pl.ds(pl.multiple_of((s - 1) * tf, tf), tf))

    @pl.when(s == 0)
    def _():
        load_tile()
        keep_arrived_chunk()

    @pl.when((s > 0) & (s < nj))
    def _():
        up_previous_chunk()
        keep_arrived_chunk()

    @pl.when(s == nj)
    def _():
        up_previous_chunk()
        down_and_store()

    @pl.when(s > nj)
    def _():
        load_tile()
        for j in range(nj):
            up_chunk(j, slice(j * tf, (j + 1) * tf))
        down_and_store()


def _stream_specs(split, buffered_once):
    tm = FFN_TM
    kw = {"pipeline_mode": pl.Buffered(1)} if buffered_once else {}
    if not split:
        return [pl.BlockSpec((tm, D_MODEL), lambda s: (_ffn_tile(s), 0), **kw)]
    last_ctx = N_CTX_TILES - 1
    return [pl.BlockSpec((tm, D_MODEL), lambda s: (jnp.minimum(_ffn_tile(s), last_ctx), 0), **kw),
            pl.BlockSpec((tm, D_MODEL), lambda s: (jnp.maximum(_ffn_tile(s) - N_CTX_TILES, 0), 0), **kw)]


def _ffn(xs, mod, layer, chunk0, norm_g, wg, wu, wd, split_out=False):
    tm, tf, nj = FFN_TM, FFN_TF, FFN_NJ
    split_in = len(xs) == 2
    mod_spec = lambda c: pl.BlockSpec((None, MOD_ROWS, D_MODEL), lambda s: (layer, 0, c))
    w_col = lambda s: (layer, 0, jnp.minimum(s, nj - 1))
    w_row = lambda s: (layer, jnp.minimum(s, nj - 1), 0)
    if split_out:
        out_shape = [jax.ShapeDtypeStruct((N_CTX_ROWS, D_MODEL), F32),
                     jax.ShapeDtypeStruct((N_LAT_ROWS, D_MODEL), F32)]
    else:
        out_shape = [jax.ShapeDtypeStruct((N_ROWS, D_MODEL), F32)]
    out = pl.pallas_call(
        functools.partial(_ffn_kernel, layer=layer, n_in=len(xs), n_out=len(out_shape)),
        grid=(N_FFN_STEPS,),
        in_specs=_stream_specs(split_in, False) + [
            _full((DEPTH, D_MODEL)),
            mod_spec(chunk0), mod_spec(chunk0 + 1), mod_spec(chunk0 + 2),
            pl.BlockSpec((None, D_MODEL, tf), w_col),
            pl.BlockSpec((None, D_MODEL, tf), w_col),
            pl.BlockSpec((None, tf, D_MODEL), w_row),
        ],
        out_specs=_stream_specs(split_out, True),
        out_shape=out_shape,
        scratch_shapes=[pltpu.VMEM((tm, D_MODEL), BF16),
                        pltpu.VMEM((tm, D_FF), BF16),
                        pltpu.VMEM((nj, D_MODEL, tf), BF16),
                        pltpu.VMEM((nj, D_MODEL, tf), BF16),
                        pltpu.VMEM((D_FF, D_MODEL), BF16)],
        compiler_params=_cparams("arbitrary"),
        name="ffn",
    )(*xs, norm_g, mod, mod, mod, wg, wu, wd)
    return tuple(out)


def _once(shape, index_map):
    return pl.BlockSpec(shape, index_map, pipeline_mode=pl.Buffered(1))


def _mod_chunk(layer, c):
    return pl.BlockSpec((None, MOD_ROWS, D_MODEL), lambda *_: (layer, 0, c))


def _head_mean_square(x):
    n = x.shape[-1]
    r = lax.broadcasted_iota(jnp.int32, (n, n), 0) >> HEAD_SHIFT
    c = lax.broadcasted_iota(jnp.int32, (n, n), 1) >> HEAD_SHIFT
    ones_bd = jnp.where(r == c, 1.0, 0.0).astype(BF16)
    return _dot((x * x).astype(BF16), ones_bd) * (1.0 / HEAD_DIM)


def _head_norm(x, head_gain):
    gain_row = jnp.concatenate([head_gain] * (x.shape[-1] // HEAD_DIM), axis=-1)
    return x * lax.rsqrt(_head_mean_square(x) + EPS) * gain_row


def _head_cols(x, h):
    return x[:, h * HEAD_DIM:(h + 1) * HEAD_DIM].astype(BF16)


def _softmax_pv(scores, values, sink):
    m = jnp.max(scores[0], axis=-1, keepdims=True)
    for s in scores[1:]:
        m = jnp.maximum(m, jnp.max(s, axis=-1, keepdims=True))
    if sink is not None:
        m = jnp.maximum(m, sink)
    denom = None
    acc = None
    for s, v in zip(scores, values):
        p = jnp.exp(s - m)
        d = jnp.sum(p, axis=-1, keepdims=True)
        o = _dot(p.astype(BF16), v)
        denom = d if denom is None else denom + d
        acc = o if acc is None else acc + o
    if sink is not None:
        denom = denom + jnp.exp(sink - m)
    return acc / denom


def _rope(x, cos, sin_lo, sin_hi):
    cols = []
    for c in range(x.shape[-1] // LANES):
        xc = x[:, c * LANES:(c + 1) * LANES]
        cols.append(xc * cos + pltpu.roll(xc, 112, 1) * sin_lo + pltpu.roll(xc, 16, 1) * sin_hi)
    return cols[0] if len(cols) == 1 else jnp.concatenate(cols, axis=-1)


def _block_diag(blocks):
    n = len(blocks)
    w = blocks[0].shape[0]
    rows = []
    for k, blk in enumerate(blocks):
        parts = []
        if k > 0:
            parts.append(jnp.zeros((w, k * w), F32))
        parts.append(blk)
        if k < n - 1:
            parts.append(jnp.zeros((w, (n - 1 - k) * w), F32))
        rows.append(jnp.concatenate(parts, axis=-1))
    return jnp.concatenate(rows, axis=0)


def _rglru_gates(xc, wa, ba, wx, bx, lam):
    xb = xc.astype(BF16)
    r = _sigmoid(_dot(xb, wa) + ba)
    i = _sigmoid(_dot(xb, wx) + bx)
    softplus = jnp.maximum(-lam, 0.0) + jnp.log1p(jnp.exp(-jnp.abs(lam)))
    log_a = (-LRU_C) * r * softplus
    a = jnp.exp(log_a)
    b = jnp.sqrt(1.0 - a * a) * (i * xc)
    return a, b


def _block_prefix(a, b, reverse):
    t = a.shape[0]
    row = lax.broadcasted_iota(jnp.int32, a.shape, 0) & 7
    for d in (1, 2, 4):
        if reverse:
            a_s = pltpu.roll(a, t - d, 0)
            b_s = pltpu.roll(b, t - d, 0)
            ok = row < 8 - d
        else:
            a_s = pltpu.roll(a, d, 0)
            b_s = pltpu.roll(b, d, 0)
            ok = row >= d
        b = jnp.where(ok, a * b_s + b, b)
        a = jnp.where(ok, a * a_s, a)
    return a, b


def _conv4(x, w_ref, b_row):
    t = x.shape[0]
    row = lax.broadcasted_iota(jnp.int32, x.shape, 0)
    xm2 = jnp.where(row >= 2, pltpu.roll(x, 2, 0), 0.0)
    xm1 = jnp.where(row >= 1, pltpu.roll(x, 1, 0), 0.0)
    xp1 = jnp.where(row < t - 1, pltpu.roll(x, t - 1, 0), 0.0)
    return (xm2 * w_ref[0:1, :] + xm1 * w_ref[1:2, :] + x * w_ref[2:3, :] + xp1 * w_ref[3:4, :]) + b_row


def _rglru_prepare(cx, cy, conv_w_ref, conv_b, gate_w_ref, ba_ref, bx_ref, lam_ref,
                   af_ref, bf_ref, ab_ref, bb_ref, gel_ref):
    xc = _conv4(cx, conv_w_ref, conv_b)
    a, b = _rglru_gates(xc, gate_w_ref[0], ba_ref[0:1, :], gate_w_ref[1], bx_ref[0:1, :], lam_ref[0:1, :])
    a, b = _block_prefix(a, b, reverse=False)
    af_ref[...] = a
    bf_ref[...] = b
    a, b = _rglru_gates(xc, gate_w_ref[2], ba_ref[1:2, :], gate_w_ref[3], bx_ref[1:2, :], lam_ref[1:2, :])
    a, b = _block_prefix(a, b, reverse=True)
    ab_ref[...] = a
    bb_ref[...] = b
    gel_ref[...] = _gelu_tanh(cy)


SCAN_UNROLL = 8


def _rglru_finish(h0f, h0b, af_ref, bf_ref, ab_ref, bb_ref, hf_ref, hb_ref, gel_ref):
    nblk = af_ref.shape[0] // 8

    def body(k, carry):
        cf, cb = carry
        rf = pl.ds(pl.multiple_of(k * 8, 8), 8)
        hf = bf_ref[rf, :] + af_ref[rf, :] * cf
        hf_ref[rf, :] = hf
        rb = pl.ds(pl.multiple_of((nblk - 1 - k) * 8, 8), 8)
        hb = bb_ref[rb, :] + ab_ref[rb, :] * cb
        hb_ref[rb, :] = hb
        return hf[7:8, :], hb[0:1, :]

    cf, cb = lax.fori_loop(0, nblk, body, (h0f, h0b), unroll=SCAN_UNROLL)
    oc = (hf_ref[...] + hb_ref[...]) * gel_ref[...]
    return oc, cf, cb


def _store_gate_weights(gate_w_ref, wa_ref, wx_ref):
    for d in range(2):
        gate_w_ref[2 * d] = _block_diag([wa_ref[d, n] for n in range(N_HEADS)]).astype(BF16)
        gate_w_ref[2 * d + 1] = _block_diag([wx_ref[d, n] for n in range(N_HEADS)]).astype(BF16)


def _lane_head_masks(n):
    lane = lax.broadcasted_iota(jnp.int32, (1, n), 1) >> HEAD_SHIFT
    return [jnp.where(lane == h, 1.0, 0.0) for h in range(n // HEAD_DIM)]


def _log_decays(theta_ref, masks):
    theta = theta_ref[...]
    lanes = theta[:, 0:1] * masks[0]
    for h in range(1, N_HEADS):
        lanes = lanes + theta[:, h:h + 1] * masks[h]
    lg = jnp.log1p(-jnp.exp(lanes))
    return lg[0:1, :], lg[1:2, :]


RET_BLOCK = 256


def _retention(q, k8, vb, s0, lgf, lgb, masks, o_ref):
    t, w = q.shape
    c = RET_BLOCK
    nh = w // HEAD_DIM
    pos = lax.broadcasted_iota(jnp.int32, (c, w), 0).astype(F32)
    q_dec = (jnp.exp(lgf * (pos + 1.0)), jnp.exp(lgb * (float(c) - pos)))
    k_dec = (jnp.exp(lgf * (float(c - 1) - pos)), jnp.exp(lgb * pos))
    chunk_dec = (jnp.exp(lgf * float(c)), jnp.exp(lgb * float(c)))
    rel = (lax.broadcasted_iota(jnp.int32, (c, c), 0) - lax.broadcasted_iota(jnp.int32, (c, c), 1)).astype(F32)
    decs = []
    for h in range(nh):
        gf = lgf[:, h * HEAD_DIM:h * HEAD_DIM + 1]
        gb = lgb[:, h * HEAD_DIM:h * HEAD_DIM + 1]
        e = jnp.exp(jnp.where(rel >= 0, gf * rel, gb * (-rel)))
        decs.append(jnp.where(rel == 0, 2.0, e))
    dec = jnp.concatenate(decs, axis=0)
    r_head = lax.broadcasted_iota(jnp.int32, (w, w), 0) >> HEAD_SHIFT
    c_head = lax.broadcasted_iota(jnp.int32, (w, w), 1) >> HEAD_SHIFT
    same_head = jnp.where(r_head == c_head, 1.0, 0.0)
    states = [None, None] if s0 is None else list(s0)

    def carry(d, rows, o):
        if states[d] is not None:
            o = o + _dot((q[rows, :] * q_dec[d]).astype(BF16), states[d].astype(BF16))
        upd = _dot_tn((k8[rows, :] * k_dec[d]).astype(BF16), vb[rows, :]) * same_head
        states[d] = upd if states[d] is None else states[d] * chunk_dec[d] + upd
        return o

    for ci in range(t // c):
        rows = slice(ci * c, (ci + 1) * c)
        qc = q[rows, :]
        q_stack = jnp.concatenate([(qc * masks[h]).astype(BF16) for h in range(nh)], axis=0)
        inner = (_dot_nt(q_stack, k8[rows, :].astype(BF16)) * dec).astype(BF16)
        out = _dot(inner, vb[rows, :])
        o = out[0:c, :] * masks[0]
        for h in range(1, nh):
            o = o + out[h * c:(h + 1) * c, :] * masks[h]
        o_ref[rows, :] = carry(0, rows, o)
    for ci in reversed(range(t // c)):
        rows = slice(ci * c, (ci + 1) * c)
        if states[1] is not None:
            o_ref[rows, :] = carry(1, rows, o_ref[rows, :])
        else:
            carry(1, rows, None)
    return states[0], states[1]


def _ctx_mixer_kernel(*refs, layer, n_prev):
    prev_refs = refs[:n_prev]
    (x_ref, n2_ref, sh_ref, sc_ref, g2_ref, win_ref, wout_ref,
     aqn_ref, akn_ref, bqn_ref, bkn_ref, sink_ref,
     convw_ref, convb_ref, wa_ref, ba_ref, wx_ref, bx_ref, lam_ref, theta_ref, dn_ref,
     xn_ref, *state_refs) = refs[n_prev:n_prev + 28]
    (win_s, wout_s, u_ref, mixed_ref,
     gate_w_ref, af_ref, bf_ref, ab_ref, bb_ref, hf_ref, hb_ref, gel_ref, ret_ref) = refs[n_prev + 28:]
    t = SEQ
    lrow = slice(layer, layer + 1)
    for prev_ref, state_ref in zip(prev_refs, state_refs):
        for earlier in range(layer):
            state_ref[earlier] = prev_ref[earlier]
    ka_ref, va_ref, kb_ref, vb_ref, stc_ref, std_ref = (ref.at[layer] for ref in state_refs)

    @pl.when(pl.program_id(0) == 0)
    def _():
        for c0 in range(0, IN_WIDTH, 2 * GROUP_W):
            win_s[:, c0:c0 + 2 * GROUP_W] = win_ref[:, c0:c0 + 2 * GROUP_W].astype(BF16)
        wout_s[...] = wout_ref[...].astype(BF16)
        _store_gate_weights(gate_w_ref, wa_ref, wx_ref)

    x = x_ref[...]
    h = _norm_mod(x, n2_ref[lrow, :], sc_ref[0:1, :], sh_ref[0:1, :]).astype(BF16)
    c_cols = slice(COL_CX, COL_CX + 2 * GROUP_W)
    u_ref[:, c_cols] = _dot(h, win_s[:, c_cols])
    _rglru_prepare(u_ref[:, COL_CX:COL_CX + GROUP_W], u_ref[:, COL_CY:COL_CY + GROUP_W],
                   convw_ref, convb_ref[lrow, :], gate_w_ref, ba_ref, bx_ref, lam_ref,
                   af_ref, bf_ref, ab_ref, bb_ref, gel_ref)
    u_ref[:, 0:COL_CX] = _dot(h, win_s[:, 0:COL_CX])
    u_ref[:, COL_DQ:IN_WIDTH] = _dot(h, win_s[:, COL_DQ:IN_WIDTH])

    for (cq, ck, cv, qn_ref, kn_ref, k_out, v_out, col0, use_sink) in (
            (COL_AQ, COL_AK, COL_AV, aqn_ref, akn_ref, ka_ref, va_ref, 0, True),
            (COL_BQ, COL_BK, COL_BV, bqn_ref, bkn_ref, kb_ref, vb_ref, GROUP_W, False)):
        q = _head_norm(u_ref[:, cq:cq + GROUP_W], qn_ref[lrow, :])
        k = _head_norm(u_ref[:, ck:ck + KV_W], kn_ref[lrow, :])
        v = u_ref[:, cv:cv + KV_W]
        k_out[...] = k
        v_out[...] = v
        qs = q * (HEAD_DIM ** -0.5)
        heads = []
        for hd in range(N_HEADS):
            kv = hd // 2
            s = _dot_nt(_head_cols(qs, hd), _head_cols(k, kv))
            sink = jnp.full((t, 1), sink_ref[layer, hd], F32) if use_sink else None
            heads.append(_softmax_pv([s], [_head_cols(v, kv)], sink))
        mixed_ref[:, col0:col0 + GROUP_W] = jnp.concatenate(heads, axis=-1).astype(BF16)

    zero = jnp.zeros((1, GROUP_W), F32)
    oc, cf, cb = _rglru_finish(zero, zero, af_ref, bf_ref, ab_ref, bb_ref, hf_ref, hb_ref, gel_ref)
    mixed_ref[:, 2 * GROUP_W:3 * GROUP_W] = oc.astype(BF16)
    stc_ref[0:1, :] = cf
    stc_ref[1:2, :] = cb

    masks = _lane_head_masks(GROUP_W)
    lgf, lgb = _log_decays(theta_ref, masks)
    k8 = u_ref[:, COL_DK:COL_DK + GROUP_W] * (HEAD_DIM ** -0.5)
    vb = u_ref[:, COL_DV:COL_DV + GROUP_W].astype(BF16)
    final_states = _retention(u_ref[:, COL_DQ:COL_DQ + GROUP_W], k8, vb, None, lgf, lgb, masks, ret_ref)
    o = ret_ref[...]
    o = o * lax.rsqrt(_head_mean_square(o) + EPS) * dn_ref[lrow, :] * _silu(u_ref[:, COL_DG:COL_DG + GROUP_W])
    mixed_ref[:, 3 * GROUP_W:4 * GROUP_W] = o.astype(BF16)
    for d, s_full in enumerate(final_states):
        for hd in range(N_HEADS):
            std_ref[d, hd] = s_full[hd * HEAD_DIM:(hd + 1) * HEAD_DIM, hd * HEAD_DIM:(hd + 1) * HEAD_DIM]

    xn_ref[...] = x + g2_ref[0:1, :] * _dot(mixed_ref[...], wout_s[...])


def _ctx_mixers(x, mod, layer, prev, norm2_g, w_in, w_out,
                a_qn, a_kn, a_sink, b_qn, b_kn, c_conv_w, c_conv_b, c_wa, c_ba, c_wx, c_bx,
                c_lambda, d_theta, d_norm_g):
    per_request = lambda slots, shape: pl.BlockSpec((None, slots) + shape, lambda b: (b,) + (0,) * (1 + len(shape)))
    state_dims = [(SEQ, KV_W)] * 4 + [(2, GROUP_W), (2, N_HEADS, HEAD_DIM, HEAD_DIM)]
    scr = pltpu.VMEM((SEQ, GROUP_W), F32)
    out = pl.pallas_call(
        functools.partial(_ctx_mixer_kernel, layer=layer, n_prev=len(prev)),
        grid=(BATCH,),
        in_specs=[per_request(layer, dims) for dims in state_dims[:len(prev)]] + [
            pl.BlockSpec((SEQ, D_MODEL), lambda b: (b, 0)),
            _full((DEPTH, D_MODEL)),
            _mod_chunk(layer, 3), _mod_chunk(layer, 4), _mod_chunk(layer, 5),
            _once((None, D_MODEL, IN_WIDTH), lambda b: (layer, 0, 0)),
            _once((None, D_MODEL, D_MODEL), lambda b: (layer, 0, 0)),
            _full((DEPTH, HEAD_DIM)), _full((DEPTH, HEAD_DIM)), _full((DEPTH, HEAD_DIM)), _full((DEPTH, HEAD_DIM)),
            pl.BlockSpec(memory_space=pltpu.SMEM),
            _layer_block((4, GROUP_W), layer), _full((DEPTH, GROUP_W)),
            _layer_block((2, N_HEADS, HEAD_DIM, HEAD_DIM), layer), _layer_block((2, GROUP_W), layer),
            _layer_block((2, N_HEADS, HEAD_DIM, HEAD_DIM), layer), _layer_block((2, GROUP_W), layer),
            _layer_block((2, GROUP_W), layer),
            _layer_block((2, N_HEADS), layer), _full((DEPTH, GROUP_W)),
        ],
        out_specs=[pl.BlockSpec((SEQ, D_MODEL), lambda b: (b, 0))] + [
            per_request(layer + 1, dims) for dims in state_dims],
        out_shape=[jax.ShapeDtypeStruct((N_ROWS, D_MODEL), F32)] + [
            jax.ShapeDtypeStruct((BATCH, layer + 1) + dims, F32) for dims in state_dims],
        input_output_aliases={len(prev): 0},
        scratch_shapes=[pltpu.VMEM((D_MODEL, IN_WIDTH), BF16), pltpu.VMEM((D_MODEL, D_MODEL), BF16),
                        pltpu.VMEM((SEQ, IN_WIDTH), F32), pltpu.VMEM((SEQ, D_MODEL), BF16),
                        pltpu.VMEM((4, GROUP_W, GROUP_W), BF16)] + [scr] * 8,
        compiler_params=_cparams("arbitrary"),
        name="ctx_mixers",
    )(*prev, x, norm2_g, mod, mod, mod, w_in, w_out,
      a_qn, a_kn, b_qn, b_kn, a_sink, c_conv_w, c_conv_b, c_wa, c_ba, c_wx, c_bx,
      c_lambda, d_theta, d_norm_g)
    return out[0], tuple(out[1:])


LAT_BLOCK0 = N_CTX_ROWS // DEC_SEQ


def _lat_attn_kernel(x_ref, n2_ref, sh_ref, sc_ref, g2_ref, win_ref, wout_ref,
                     kca_ref, vca_ref, kcb_ref, vcb_ref,
                     aqn_ref, akn_ref, bqn_ref, bkn_ref, sink_ref, cos_ref, sinl_ref, sinh_ref,
                     xn_ref, h_ref, u_ref, o_ref, *, layer):
    t = DEC_SEQ
    lrow = slice(layer, layer + 1)
    mrow = pl.ds(1 + pl.program_id(0), 1)
    cos, sin_lo, sin_hi = cos_ref[...], sinl_ref[...], sinh_ref[...]
    scale = HEAD_DIM ** -0.5
    x = x_ref[...]
    h_ref[...] = _norm_mod(x, n2_ref[lrow, :], sc_ref[mrow, :], sh_ref[mrow, :]).astype(BF16)
    u_ref[...] = _dot(h_ref[...], win_ref[...].astype(BF16))

    q = _rope(_head_norm(u_ref[:, COL_AQ:COL_AQ + GROUP_W], aqn_ref[lrow, :]), cos, sin_lo, sin_hi)
    k = _rope(_head_norm(u_ref[:, COL_AK:COL_AK + KV_W], akn_ref[lrow, :]), cos, sin_lo, sin_hi)
    qh = [_head_cols(q * scale, h) for h in range(4)]
    v = u_ref[:, COL_AV:COL_AV + KV_W]
    kh = [_head_cols(k, kv) for kv in range(2)]
    vh = [_head_cols(v, kv) for kv in range(2)]
    kch = [_head_cols(kca_ref[...], kv) for kv in range(2)]
    vch = [_head_cols(vca_ref[...], kv) for kv in range(2)]
    w = ATT_BLOCK
    span = 3 * w
    for n in range(t // w):
        start = min(max((n - 1) * w, 0), t - span)
        rows = slice(n * w, (n + 1) * w)
        band = slice(start, start + span)
        qpos = (lax.broadcasted_iota(jnp.int32, (2 * w, span), 0) & (w - 1)) + n * w
        kpos = lax.broadcasted_iota(jnp.int32, (2 * w, span), 1) + start
        valid = jnp.abs(qpos - kpos) <= WINDOW
        heads = []
        for kv in range(2):
            qp = jnp.concatenate([qh[2 * kv][rows, :], qh[2 * kv + 1][rows, :]], axis=0)
            s_ctx = _dot_nt(qp, kch[kv])
            s_band = jnp.where(valid, _dot_nt(qp, kh[kv][band, :]), NEG_INF)
            row = lax.broadcasted_iota(jnp.int32, (2 * w, 1), 0)
            sink = jnp.where(row < w, sink_ref[layer, 2 * kv], sink_ref[layer, 2 * kv + 1])
            o = _softmax_pv([s_ctx, s_band], [vch[kv], vh[kv][band, :]], sink)
            heads += [o[0:w, :], o[w:2 * w, :]]
        o_ref[rows, 0:GROUP_W] = jnp.concatenate(heads, axis=-1).astype(BF16)

    q = _rope(_head_norm(u_ref[:, COL_BQ:COL_BQ + GROUP_W], bqn_ref[lrow, :]), cos, sin_lo, sin_hi)
    k = _rope(_head_norm(u_ref[:, COL_BK:COL_BK + KV_W], bkn_ref[lrow, :]), cos, sin_lo, sin_hi)
    qh = [_head_cols(q * scale, h) for h in range(4)]
    v = u_ref[:, COL_BV:COL_BV + KV_W]
    kh = [_head_cols(k, kv) for kv in range(2)]
    vh = [_head_cols(v, kv) for kv in range(2)]
    kch = [_head_cols(kcb_ref[...], kv) for kv in range(2)]
    vch = [_head_cols(vcb_ref[...], kv) for kv in range(2)]
    tq = 2 * ATT_BLOCK
    for n in range(t // tq):
        rows = slice(n * tq, (n + 1) * tq)
        heads = []
        for kv in range(2):
            qp = jnp.concatenate([qh[2 * kv][rows, :], qh[2 * kv + 1][rows, :]], axis=0)
            o = _softmax_pv([_dot_nt(qp, kch[kv]), _dot_nt(qp, kh[kv])], [vch[kv], vh[kv]], None)
            heads += [o[0:tq, :], o[tq:2 * tq, :]]
        o_ref[rows, GROUP_W:2 * GROUP_W] = jnp.concatenate(heads, axis=-1).astype(BF16)

    xn_ref[...] = x + g2_ref[mrow, :] * _dot(o_ref[...], wout_ref[...].astype(BF16))


def _lat_recurrent_kernel(xn_in_ref, h_ref, g2_ref, wc_ref, wqk_ref, wvg_ref, wout_ref, h0_ref,
                          convw_ref, convb_ref, wa_ref, ba_ref, wx_ref, bx_ref, lam_ref,
                          s0_ref, theta_ref, dn_ref,
                          xn_ref, gate_w_ref, af_ref, bf_ref, ab_ref, bb_ref, hf_ref, hb_ref, gel_ref, ret_ref,
                          *, layer):
    lrow = slice(layer, layer + 1)
    mrow = pl.ds(1 + pl.program_id(0), 1)

    @pl.when(pl.program_id(0) == 0)
    def _():
        _store_gate_weights(gate_w_ref, wa_ref, wx_ref)

    h = h_ref[...]
    u = _dot(h, wc_ref[...].astype(BF16))
    _rglru_prepare(u[:, 0:GROUP_W], u[:, GROUP_W:2 * GROUP_W], convw_ref, convb_ref[lrow, :],
                   gate_w_ref, ba_ref, bx_ref, lam_ref, af_ref, bf_ref, ab_ref, bb_ref, gel_ref)
    oc, _, _ = _rglru_finish(h0_ref[0:1, :], h0_ref[1:2, :],
                             af_ref, bf_ref, ab_ref, bb_ref, hf_ref, hb_ref, gel_ref)
    y = _dot(oc.astype(BF16), wout_ref[0:GROUP_W, :].astype(BF16))

    uqk = _dot(h, wqk_ref[...].astype(BF16))
    uvg = _dot(h, wvg_ref[...].astype(BF16))
    masks = _lane_head_masks(GROUP_W)
    lgf, lgb = _log_decays(theta_ref, masks)
    s0 = tuple(_block_diag([s0_ref[d, hd] for hd in range(N_HEADS)]) for d in range(2))
    _retention(uqk[:, 0:GROUP_W], uqk[:, GROUP_W:2 * GROUP_W] * (HEAD_DIM ** -0.5),
               uvg[:, 0:GROUP_W].astype(BF16), s0, lgf, lgb, masks, ret_ref)
    o = ret_ref[...]
    o = o * lax.rsqrt(_head_mean_square(o) + EPS) * dn_ref[lrow, :] * _silu(uvg[:, GROUP_W:2 * GROUP_W])
    y = y + _dot(o.astype(BF16), wout_ref[GROUP_W:2 * GROUP_W, :].astype(BF16))
    xn_ref[...] = xn_in_ref[...] + g2_ref[mrow, :] * y


def _lat_mixers(x, mod, layer, caches, state_c, state_d, rope, norm2_g, w_in, w_out,
                a_qn, a_kn, a_sink, b_qn, b_kn, c_conv_w, c_conv_b, c_wa, c_ba, c_wx, c_bx,
                c_lambda, d_theta, d_norm_g):
    rows = pl.BlockSpec((DEC_SEQ, D_MODEL), lambda b: (LAT_BLOCK0 + b, 0))
    h_rows = pl.BlockSpec((DEC_SEQ, D_MODEL), lambda b: (b, 0))
    cache_spec = pl.BlockSpec((None, None, PAST_LEN, KV_W), lambda b: (b, layer, 0, 0))
    gain = _full((DEPTH, HEAD_DIM))
    table = _once((DEC_SEQ, LANES), lambda b: (0, 0))
    out_shape = jax.ShapeDtypeStruct((N_ROWS, D_MODEL), F32)
    win_cols = lambda w, c: _once((None, D_MODEL, w), lambda b: (layer, 0, c))
    wout_rows = lambda h, r: _once((None, h, D_MODEL), lambda b: (layer, r, 0))

    xn, h = pl.pallas_call(
        functools.partial(_lat_attn_kernel, layer=layer),
        grid=(DEC_BATCH,),
        in_specs=[rows, _full((DEPTH, D_MODEL)),
                  _mod_chunk(layer, 3), _mod_chunk(layer, 4), _mod_chunk(layer, 5),
                  win_cols(4 * GROUP_W, 0), wout_rows(2 * GROUP_W, 0),
                  cache_spec, cache_spec, cache_spec, cache_spec,
                  gain, gain, gain, gain,
                  pl.BlockSpec(memory_space=pltpu.SMEM),
                  table, table, table],
        out_specs=[rows, pl.BlockSpec((DEC_SEQ, D_MODEL), lambda b: (b, 0), pipeline_mode=pl.Buffered(1))],
        out_shape=[out_shape, jax.ShapeDtypeStruct((N_LAT_ROWS, D_MODEL), BF16)],
        input_output_aliases={0: 0},
        scratch_shapes=[pltpu.VMEM((DEC_SEQ, 4 * GROUP_W), F32), pltpu.VMEM((DEC_SEQ, 2 * GROUP_W), BF16)],
        compiler_params=_cparams("arbitrary"),
        name="lat_attention",
    )(x, norm2_g, mod, mod, mod, w_in, w_out, *caches, a_qn, a_kn, b_qn, b_kn, a_sink, *rope)

    scr = pltpu.VMEM((DEC_SEQ, GROUP_W), F32)
    xn = pl.pallas_call(
        functools.partial(_lat_recurrent_kernel, layer=layer),
        grid=(DEC_BATCH,),
        in_specs=[
            rows, h_rows, _mod_chunk(layer, 5),
            win_cols(2 * GROUP_W, COL_CX // (2 * GROUP_W)),
            win_cols(2 * GROUP_W, COL_DQ // (2 * GROUP_W)), win_cols(2 * GROUP_W, COL_DV // (2 * GROUP_W)),
            wout_rows(2 * GROUP_W, 1),
            pl.BlockSpec((None, None, 2, GROUP_W), lambda b: (b, layer, 0, 0)),
            _layer_block((4, GROUP_W), layer), _full((DEPTH, GROUP_W)),
            _layer_block((2, N_HEADS, HEAD_DIM, HEAD_DIM), layer), _layer_block((2, GROUP_W), layer),
            _layer_block((2, N_HEADS, HEAD_DIM, HEAD_DIM), layer), _layer_block((2, GROUP_W), layer),
            _layer_block((2, GROUP_W), layer),
            pl.BlockSpec((None, None, 2, N_HEADS, HEAD_DIM, HEAD_DIM), lambda b: (b, layer, 0, 0, 0, 0)),
            _layer_block((2, N_HEADS), layer), _full((DEPTH, GROUP_W))],
        out_specs=rows,
        out_shape=out_shape,
        input_output_aliases={0: 0},
        scratch_shapes=[pltpu.VMEM((4, GROUP_W, GROUP_W), BF16)] + [scr] * 8,
        compiler_params=_cparams("arbitrary"),
        name="lat_recurrent",
    )(xn, h, mod, w_in, w_in, w_in, w_out, state_c, c_conv_w, c_conv_b, c_wa, c_ba, c_wx, c_bx, c_lambda,
      state_d, d_theta, d_norm_g)
    return xn


def _rope_tables():
    t = np.arange(DEC_SEQ)
    row = (t // GRID_W).astype(np.float64)[:, None]
    col = (t % GRID_W).astype(np.float64)[:, None]
    half = HEAD_DIM // 2
    inv = 1.0 / (ROPE_BASE ** (np.arange(0, half, 2, dtype=np.float64) / half))
    j = np.arange(LANES) % HEAD_DIM
    ang = np.where((j < half)[None, :], row, col) * inv[j % (half // 2)][None, :]
    first = ((j % half) < half // 2)[None, :]
    cos, sin = np.cos(ang), np.sin(ang)
    return tuple(jnp.asarray(a, F32) for a in (cos, np.where(first, -sin, 0.0), np.where(first, 0.0, sin)))


def kernel(x_prompt, x_sample, cache_a_k, cache_a_v, cache_b_k, cache_b_v, state_c, state_d, c, c_ctx, norm1_g, norm2_g, norm3_g, w_mod, b_mod, ffn1_wg, ffn1_wu, ffn1_wd, ffn2_wg, ffn2_wu, ffn2_wd, w_in, w_out, a_qn, a_kn, a_sink, b_qn, b_kn, c_conv_w, c_conv_b, c_wa, c_ba, c_wx, c_bx, c_lambda, d_theta, d_norm_g):
    mod = _modulation(c_ctx, c, w_mod, b_mod)
    rope = _rope_tables()
    caches = tuple(t.reshape(DEC_BATCH, DEPTH, PAST_LEN, KV_W) for t in (cache_a_k, cache_a_v, cache_b_k, cache_b_v))
    mixer_params = (a_qn, a_kn, a_sink, b_qn, b_kn, c_conv_w, c_conv_b, c_wa, c_ba, c_wx, c_bx,
                    c_lambda, d_theta, d_norm_g)
    xs = (x_prompt.reshape(N_CTX_ROWS, D_MODEL), x_sample.reshape(N_LAT_ROWS, D_MODEL))
    states = ()
    for l in range(DEPTH):
        (x,) = _ffn(xs, mod, l, 0, norm1_g, ffn1_wg, ffn1_wu, ffn1_wd)
        x, states = _ctx_mixers(x, mod, l, states, norm2_g, w_in, w_out, *mixer_params)
        x = _lat_mixers(x, mod, l, caches, state_c, state_d, rope, norm2_g, w_in, w_out, *mixer_params)
        xs = _ffn((x,), mod, l, 6, norm3_g, ffn2_wg, ffn2_wu, ffn2_wd, split_out=(l == DEPTH - 1))
    y_p, y_s = xs
    ka, va, kb, vb, st_c, st_d = states
    kv_shape = (BATCH, DEPTH, SEQ, 2, HEAD_DIM)
    return (y_p.reshape(BATCH, SEQ, D_MODEL), y_s.reshape(DEC_BATCH, DEC_SEQ, D_MODEL),
            ka.reshape(kv_shape), va.reshape(kv_shape), kb.reshape(kv_shape), vb.reshape(kv_shape),
            st_c, st_d)
```

```python
import functools
import math

import numpy as np
import jax
import jax.numpy as jnp
from jax import lax
from jax.experimental import pallas as pl
from jax.experimental.pallas import tpu as pltpu

F32 = jnp.float32
BF16 = jnp.bfloat16

D_MODEL = 1024
BATCH = 16
SEQ = 256
DEPTH = 2
DEC_BATCH = 2
DEC_SEQ = 1024
PAST_LEN = 512
GRID_W = 64
HEAD_DIM = 64
HEAD_SHIFT = 6
N_HEADS = 4
GROUP_W = 256
KV_W = 2 * HEAD_DIM
LANES = 128
WINDOW = 128
ATT_BLOCK = 128
ROPE_BASE = 10000.0
LRU_C = 8.0
D_FF = 2816
N_MOD = 9
EPS = 1e-6
NEG_INF = -1e30
IN_WIDTH = 2560

N_CTX_ROWS = BATCH * SEQ
N_LAT_ROWS = DEC_BATCH * DEC_SEQ
N_ROWS = N_CTX_ROWS + N_LAT_ROWS
MOD_ROWS = 8
MOD_GROUP = 1024

VMEM_LIMIT_BYTES = 56 * 1024 * 1024

COL_AQ, COL_AK, COL_AV = 0, 256, 384
COL_BQ, COL_BK, COL_BV = 512, 768, 896
COL_CX, COL_CY = 1024, 1280
COL_DQ, COL_DK, COL_DV, COL_DG = 1536, 1792, 2048, 2304


def _cparams(*sem):
    return pltpu.CompilerParams(dimension_semantics=sem, vmem_limit_bytes=VMEM_LIMIT_BYTES)


def _dot(a, b):
    return jnp.dot(a, b, preferred_element_type=F32)


def _dot_nt(a, b):
    return lax.dot_general(a, b, (((1,), (1,)), ((), ())), preferred_element_type=F32)


def _dot_tn(a, b):
    return lax.dot_general(a, b, (((0,), (0,)), ((), ())), preferred_element_type=F32)


def _sigmoid(x):
    return 0.5 * jnp.tanh(0.5 * x) + 0.5


def _silu(x):
    return x * _sigmoid(x)


def _gelu_tanh(x):
    return 0.5 * x * (1.0 + jnp.tanh(math.sqrt(2.0 / math.pi) * (x + 0.044715 * (x * x * x))))


def _mod_row(i, tm, s):
    if tm >= MOD_GROUP:
        block_index = i * (tm // MOD_GROUP) + s
    else:
        block_index = i >> int(math.log2(MOD_GROUP // tm))
    return jnp.maximum(block_index - (N_CTX_ROWS // MOD_GROUP - 1), 0)


def _norm_mod(x, g, sc, sh):
    ms = jnp.mean(x * x, axis=-1, keepdims=True)
    return (x * lax.rsqrt(ms + EPS) * g) * (1.0 + sc) + sh


def _full(shape):
    return pl.BlockSpec(shape, lambda *_: (0,) * len(shape))


def _layer_block(shape, layer):
    return pl.BlockSpec((None,) + shape, lambda *_: (layer,) + (0,) * len(shape))


MOD_TN = 3072


def _mod_kernel(cc_ref, c_ref, w_ref, b_ref, o_ref):
    l = pl.program_id(0)
    pad = jnp.zeros((MOD_ROWS - 1 - DEC_BATCH, D_MODEL), F32)
    cond = jnp.concatenate([cc_ref[...], c_ref[...], pad], axis=0)
    o_ref[...] = _dot(_silu(cond).astype(BF16), w_ref[...].astype(BF16)) + b_ref[pl.ds(l, 1), :]


def _modulation(c_ctx, c, w_mod, b_mod):
    n = N_MOD * D_MODEL
    return pl.pallas_call(
        _mod_kernel,
        grid=(DEPTH, n // MOD_TN),
        in_specs=[
            pl.BlockSpec((1, D_MODEL), lambda l, j: (0, 0)),
            pl.BlockSpec((DEC_BATCH, D_MODEL), lambda l, j: (0, 0)),
            pl.BlockSpec((None, D_MODEL, MOD_TN), lambda l, j: (l, 0, j)),
            pl.BlockSpec((DEPTH, MOD_TN), lambda l, j: (0, j)),
        ],
        out_specs=pl.BlockSpec((None, MOD_ROWS, MOD_TN), lambda l, j: (l, 0, j)),
        out_shape=jax.ShapeDtypeStruct((DEPTH, MOD_ROWS, n), F32),
        compiler_params=_cparams("arbitrary", "arbitrary"),
        name="modulation",
    )(c_ctx.reshape(1, D_MODEL), c, w_mod, b_mod)


FFN_TM = 1024
FFN_TF = 256
N_CTX_TILES = N_CTX_ROWS // FFN_TM


FFN_NJ = D_FF // FFN_TF
N_FFN_TILES = N_ROWS // FFN_TM
N_FFN_STEPS = FFN_NJ + N_FFN_TILES


def _ffn_tile(step):
    return jnp.maximum(step - FFN_NJ, 0)


def _on_stream_part(tile, x_refs, o_refs, fn):
    if len(x_refs) == 1 and len(o_refs) == 1:
        fn(x_refs[0], o_refs[0])
    else:
        pl.when(tile < N_CTX_TILES)(lambda: fn(x_refs[0], o_refs[0]))
        pl.when(tile >= N_CTX_TILES)(lambda: fn(x_refs[-1], o_refs[-1]))


def _ffn_kernel(*refs, layer, n_in, n_out):
    x_refs = refs[:n_in]
    n_ref, sh_ref, sc_ref, g_ref, wg_ref, wu_ref, wd_ref = refs[n_in:n_in + 7]
    o_refs = refs[n_in + 7:n_in + 7 + n_out]
    h_ref, a_ref, wg_s, wu_s, wd_s = refs[n_in + 7 + n_out:]
    nj, tf = FFN_NJ, FFN_TF
    s = pl.program_id(0)
    tile = _ffn_tile(s)
    r = _mod_row(tile, FFN_TM, 0)

    def load_tile():
        x = x_refs[0][...] if len(x_refs) == 1 else jnp.where(tile < N_CTX_TILES, x_refs[0][...], x_refs[1][...])
        h = _norm_mod(x, n_ref[layer:layer + 1, :], sc_ref[pl.ds(r, 1), :], sh_ref[pl.ds(r, 1), :])
        h_ref[...] = h.astype(BF16)

    def up_chunk(j, cols):
        h = h_ref[...]
        a_ref[:, cols] = (_silu(_dot(h, wg_s[j])) * _dot(h, wu_s[j])).astype(BF16)

    def down_and_store():
        y = (0.5 * g_ref[pl.ds(r, 1), :]) * _dot(a_ref[...], wd_s[...])

        def store(x_ref, o_ref):
            o_ref[...] = x_ref[...] + y
        _on_stream_part(tile, x_refs, o_refs, store)

    def keep_arrived_chunk():
        wg_s[s] = wg_ref[...].astype(BF16)
        wu_s[s] = wu_ref[...].astype(BF16)
        wd_s[pl.ds(pl.multiple_of(s * tf, tf), tf), :] = wd_ref[...].astype(BF16)

    def up_previous_chunk():
        up_chunk(s - 1, pl.ds(pl.multiple_of((s - 1) * tf, tf), tf))

    @pl.when(s == 0)
    def _():
        load_tile()
        keep_arrived_chunk()

    @pl.when((s > 0) & (s < nj))
    def _():
        up_previous_chunk()
        keep_arrived_chunk()

    @pl.when(s == nj)
    def _():
        up_previous_chunk()
        down_and_store()

    @pl.when(s > nj)
    def _():
        load_tile()
        for j in range(nj):
            up_chunk(j, slice(j * tf, (j + 1) * tf))
        down_and_store()


def _stream_specs(split, buffered_once):
    tm = FFN_TM
    kw = {"pipeline_mode": pl.Buffered(1)} if buffered_once else {}
    if not split:
        return [pl.BlockSpec((tm, D_MODEL), lambda s: (_ffn_tile(s), 0), **kw)]
    last_ctx = N_CTX_TILES - 1
    return [pl.BlockSpec((tm, D_MODEL), lambda s: (jnp.minimum(_ffn_tile(s), last_ctx), 0), **kw),
            pl.BlockSpec((tm, D_MODEL), lambda s: (jnp.maximum(_ffn_tile(s) - N_CTX_TILES, 0), 0), **kw)]


def _ffn(xs, mod, layer, chunk0, norm_g, wg, wu, wd, split_out=False):
    tm, tf, nj = FFN_TM, FFN_TF, FFN_NJ
    split_in = len(xs) == 2
    mod_spec = lambda c: pl.BlockSpec((None, MOD_ROWS, D_MODEL), lambda s: (layer, 0, c))
    w_col = lambda s: (layer, 0, jnp.minimum(s, nj - 1))
    w_row = lambda s: (layer, jnp.minimum(s, nj - 1), 0)
    if split_out:
        out_shape = [jax.ShapeDtypeStruct((N_CTX_ROWS, D_MODEL), F32),
                     jax.ShapeDtypeStruct((N_LAT_ROWS, D_MODEL), F32)]
    else:
        out_shape = [jax.ShapeDtypeStruct((N_ROWS, D_MODEL), F32)]
    out = pl.pallas_call(
        functools.partial(_ffn_kernel, layer=layer, n_in=len(xs), n_out=len(out_shape)),
        grid=(N_FFN_STEPS,),
        in_specs=_stream_specs(split_in, False) + [
            _full((DEPTH, D_MODEL)),
            mod_spec(chunk0), mod_spec(chunk0 + 1), mod_spec(chunk0 + 2),
            pl.BlockSpec((None, D_MODEL, tf), w_col),
            pl.BlockSpec((None, D_MODEL, tf), w_col),
            pl.BlockSpec((None, tf, D_MODEL), w_row),
        ],
        out_specs=_stream_specs(split_out, True),
        out_shape=out_shape,
        scratch_shapes=[pltpu.VMEM((tm, D_MODEL), BF16),
                        pltpu.VMEM((tm, D_FF), BF16),
                        pltpu.VMEM((nj, D_MODEL, tf), BF16),
                        pltpu.VMEM((nj, D_MODEL, tf), BF16),
                        pltpu.VMEM((D_FF, D_MODEL), BF16)],
        compiler_params=_cparams("arbitrary"),
        name="ffn",
    )(*xs, norm_g, mod, mod, mod, wg, wu, wd)
    return tuple(out)


def _once(shape, index_map):
    return pl.BlockSpec(shape, index_map, pipeline_mode=pl.Buffered(1))


def _mod_chunk(layer, c):
    return pl.BlockSpec((None, MOD_ROWS, D_MODEL), lambda *_: (layer, 0, c))


def _head_mean_square(x):
    n = x.shape[-1]
    r = lax.broadcasted_iota(jnp.int32, (n, n), 0) >> HEAD_SHIFT
    c = lax.broadcasted_iota(jnp.int32, (n, n), 1) >> HEAD_SHIFT
    ones_bd = jnp.where(r == c, 1.0, 0.0).astype(BF16)
    return _dot((x * x).astype(BF16), ones_bd) * (1.0 / HEAD_DIM)


def _head_norm(x, head_gain):
    gain_row = jnp.concatenate([head_gain] * (x.shape[-1] // HEAD_DIM), axis=-1)
    return x * lax.rsqrt(_head_mean_square(x) + EPS) * gain_row


def _head_cols(x, h):
    return x[:, h * HEAD_DIM:(h + 1) * HEAD_DIM].astype(BF16)


def _softmax_pv(scores, values, sink):
    m = jnp.max(scores[0], axis=-1, keepdims=True)
    for s in scores[1:]:
        m = jnp.maximum(m, jnp.max(s, axis=-1, keepdims=True))
    if sink is not None:
        m = jnp.maximum(m, sink)
    denom = None
    acc = None
    for s, v in zip(scores, values):
        p = jnp.exp(s - m)
        d = jnp.sum(p, axis=-1, keepdims=True)
        o = _dot(p.astype(BF16), v)
        denom = d if denom is None else denom + d
        acc = o if acc is None else acc + o
    if sink is not None:
        denom = denom + jnp.exp(sink - m)
    return acc / denom


def _rope(x, cos, sin_lo, sin_hi):
    cols = []
    for c in range(x.shape[-1] // LANES):
        xc = x[:, c * LANES:(c + 1) * LANES]
        cols.append(xc * cos + pltpu.roll(xc, 112, 1) * sin_lo + pltpu.roll(xc, 16, 1) * sin_hi)
    return cols[0] if len(cols) == 1 else jnp.concatenate(cols, axis=-1)


def _block_diag(blocks):
    n = len(blocks)
    w = blocks[0].shape[0]
    rows = []
    for k, blk in enumerate(blocks):
        parts = []
        if k > 0:
            parts.append(jnp.zeros((w, k * w), F32))
        parts.append(blk)
        if k < n - 1:
            parts.append(jnp.zeros((w, (n - 1 - k) * w), F32))
        rows.append(jnp.concatenate(parts, axis=-1))
    return jnp.concatenate(rows, axis=0)


def _rglru_gates(xc, wa, ba, wx, bx, lam):
    xb = xc.astype(BF16)
    r = _sigmoid(_dot(xb, wa) + ba)
    i = _sigmoid(_dot(xb, wx) + bx)
    softplus = jnp.maximum(-lam, 0.0) + jnp.log1p(jnp.exp(-jnp.abs(lam)))
    log_a = (-LRU_C) * r * softplus
    a = jnp.exp(log_a)
    b = jnp.sqrt(1.0 - a * a) * (i * xc)
    return a, b


def _block_prefix(a, b, reverse):
    t = a.shape[0]
    row = lax.broadcasted_iota(jnp.int32, a.shape, 0) & 7
    for d in (1, 2, 4):
        if reverse:
            a_s = pltpu.roll(a, t - d, 0)
            b_s = pltpu.roll(b, t - d, 0)
            ok = row < 8 - d
        else:
            a_s = pltpu.roll(a, d, 0)
            b_s = pltpu.roll(b, d, 0)
            ok = row >= d
        b = jnp.where(ok, a * b_s + b, b)
        a = jnp.where(ok, a * a_s, a)
    return a, b


def _conv4(x, w_ref, b_row):
    t = x.shape[0]
    row = lax.broadcasted_iota(jnp.int32, x.shape, 0)
    xm2 = jnp.where(row >= 2, pltpu.roll(x, 2, 0), 0.0)
    xm1 = jnp.where(row >= 1, pltpu.roll(x, 1, 0), 0.0)
    xp1 = jnp.where(row < t - 1, pltpu.roll(x, t - 1, 0), 0.0)
    return (xm2 * w_ref[0:1, :] + xm1 * w_ref[1:2, :] + x * w_ref[2:3, :] + xp1 * w_ref[3:4, :]) + b_row


def _rglru_prepare(cx, cy, conv_w_ref, conv_b, gate_w_ref, ba_ref, bx_ref, lam_ref,
                   af_ref, bf_ref, ab_ref, bb_ref, gel_ref):
    xc = _conv4(cx, conv_w_ref, conv_b)
    a, b = _rglru_gates(xc, gate_w_ref[0], ba_ref[0:1, :], gate_w_ref[1], bx_ref[0:1, :], lam_ref[0:1, :])
    a, b = _block_prefix(a, b, reverse=False)
    af_ref[...] = a
    bf_ref[...] = b
    a, b = _rglru_gates(xc, gate_w_ref[2], ba_ref[1:2, :], gate_w_ref[3], bx_ref[1:2, :], lam_ref[1:2, :])
    a, b = _block_prefix(a, b, reverse=True)
    ab_ref[...] = a
    bb_ref[...] = b
    gel_ref[...] = _gelu_tanh(cy)


SCAN_UNROLL = 8


def _rglru_finish(h0f, h0b, af_ref, bf_ref, ab_ref, bb_ref, hf_ref, hb_ref, gel_ref):
    nblk = af_ref.shape[0] // 8

    def body(k, carry):
        cf, cb = carry
        rf = pl.ds(pl.multiple_of(k * 8, 8), 8)
        hf = bf_ref[rf, :] + af_ref[rf, :] * cf
        hf_ref[rf, :] = hf
        rb = pl.ds(pl.multiple_of((nblk - 1 - k) * 8, 8), 8)
        hb = bb_ref[rb, :] + ab_ref[rb, :] * cb
        hb_ref[rb, :] = hb
        return hf[7:8, :], hb[0:1, :]

    cf, cb = lax.fori_loop(0, nblk, body, (h0f, h0b), unroll=SCAN_UNROLL)
    oc = (hf_ref[...] + hb_ref[...]) * gel_ref[...]
    return oc, cf, cb


def _store_gate_weights(gate_w_ref, wa_ref, wx_ref):
    for d in range(2):
        gate_w_ref[2 * d] = _block_diag([wa_ref[d, n] for n in range(N_HEADS)]).astype(BF16)
        gate_w_ref[2 * d + 1] = _block_diag([wx_ref[d, n] for n in range(N_HEADS)]).astype(BF16)


def _lane_head_masks(n):
    lane = lax.broadcasted_iota(jnp.int32, (1, n), 1) >> HEAD_SHIFT
    return [jnp.where(lane == h, 1.0, 0.0) for h in range(n // HEAD_DIM)]


def _log_decays(theta_ref, masks):
    theta = theta_ref[...]
    lanes = theta[:, 0:1] * masks[0]
    for h in range(1, N_HEADS):
        lanes = lanes + theta[:, h:h + 1] * masks[h]
    lg = jnp.log1p(-jnp.exp(lanes))
    return lg[0:1, :], lg[1:2, :]


RET_BLOCK = 256


def _retention(q, k8, vb, s0, lgf, lgb, masks, o_ref):
    t, w = q.shape
    c = RET_BLOCK
    nh = w // HEAD_DIM
    pos = lax.broadcasted_iota(jnp.int32, (c, w), 0).astype(F32)
    q_dec = (jnp.exp(lgf * (pos + 1.0)), jnp.exp(lgb * (float(c) - pos)))
    k_dec = (jnp.exp(lgf * (float(c - 1) - pos)), jnp.exp(lgb * pos))
    chunk_dec = (jnp.exp(lgf * float(c)), jnp.exp(lgb * float(c)))
    rel = (lax.broadcasted_iota(jnp.int32, (c, c), 0) - lax.broadcasted_iota(jnp.int32, (c, c), 1)).astype(F32)
    decs = []
    for h in range(nh):
        gf = lgf[:, h * HEAD_DIM:h * HEAD_DIM + 1]
        gb = lgb[:, h * HEAD_DIM:h * HEAD_DIM + 1]
        e = jnp.exp(jnp.where(rel >= 0, gf * rel, gb * (-rel)))
        decs.append(jnp.where(rel == 0, 2.0, e))
    dec = jnp.concatenate(decs, axis=0)
    r_head = lax.broadcasted_iota(jnp.int32, (w, w), 0) >> HEAD_SHIFT
    c_head = lax.broadcasted_iota(jnp.int32, (w, w), 1) >> HEAD_SHIFT
    same_head = jnp.where(r_head == c_head, 1.0, 0.0)
    states = [None, None] if s0 is None else list(s0)

    def carry(d, rows, o):
        if states[d] is not None:
            o = o + _dot((q[rows, :] * q_dec[d]).astype(BF16), states[d].astype(BF16))
        upd = _dot_tn((k8[rows, :] * k_dec[d]).astype(BF16), vb[rows, :]) * same_head
        states[d] = upd if states[d] is None else states[d] * chunk_dec[d] + upd
        return o

    for ci in range(t // c):
        rows = slice(ci * c, (ci + 1) * c)
        qc = q[rows, :]
        q_stack = jnp.concatenate([(qc * masks[h]).astype(BF16) for h in range(nh)], axis=0)
        inner = (_dot_nt(q_stack, k8[rows, :].astype(BF16)) * dec).astype(BF16)
        out = _dot(inner, vb[rows, :])
        o = out[0:c, :] * masks[0]
        for h in range(1, nh):
            o = o + out[h * c:(h + 1) * c, :] * masks[h]
        o_ref[rows, :] = carry(0, rows, o)
    for ci in reversed(range(t // c)):
        rows = slice(ci * c, (ci + 1) * c)
        if states[1] is not None:
            o_ref[rows, :] = carry(1, rows, o_ref[rows, :])
        else:
            carry(1, rows, None)
    return states[0], states[1]


def _ctx_mixer_kernel(*refs, layer, n_prev):
    prev_refs = refs[:n_prev]
    (x_ref, n2_ref, sh_ref, sc_ref, g2_ref, win_ref, wout_ref,
     aqn_ref, akn_ref, bqn_ref, bkn_ref, sink_ref,
     convw_ref, convb_ref, wa_ref, ba_ref, wx_ref, bx_ref, lam_ref, theta_ref, dn_ref,
     xn_ref, *state_refs) = refs[n_prev:n_prev + 28]
    (win_s, wout_s, u_ref, mixed_ref,
     gate_w_ref, af_ref, bf_ref, ab_ref, bb_ref, hf_ref, hb_ref, gel_ref, ret_ref) = refs[n_prev + 28:]
    t = SEQ
    lrow = slice(layer, layer + 1)
    for prev_ref, state_ref in zip(prev_refs, state_refs):
        for earlier in range(layer):
            state_ref[earlier] = prev_ref[earlier]
    ka_ref, va_ref, kb_ref, vb_ref, stc_ref, std_ref = (ref.at[layer] for ref in state_refs)

    @pl.when(pl.program_id(0) == 0)
    def _():
        for c0 in range(0, IN_WIDTH, 2 * GROUP_W):
            win_s[:, c0:c0 + 2 * GROUP_W] = win_ref[:, c0:c0 + 2 * GROUP_W].astype(BF16)
        wout_s[...] = wout_ref[...].astype(BF16)
        _store_gate_weights(gate_w_ref, wa_ref, wx_ref)

    x = x_ref[...]
    h = _norm_mod(x, n2_ref[lrow, :], sc_ref[0:1, :], sh_ref[0:1, :]).astype(BF16)
    c_cols = slice(COL_CX, COL_CX + 2 * GROUP_W)
    u_ref[:, c_cols] = _dot(h, win_s[:, c_cols])
    _rglru_prepare(u_ref[:, COL_CX:COL_CX + GROUP_W], u_ref[:, COL_CY:COL_CY + GROUP_W],
                   convw_ref, convb_ref[lrow, :], gate_w_ref, ba_ref, bx_ref, lam_ref,
                   af_ref, bf_ref, ab_ref, bb_ref, gel_ref)
    u_ref[:, 0:COL_CX] = _dot(h, win_s[:, 0:COL_CX])
    u_ref[:, COL_DQ:IN_WIDTH] = _dot(h, win_s[:, COL_DQ:IN_WIDTH])

    groups = []
    for (cq, ck, cv, qn_ref, kn_ref, k_out, v_out, use_sink) in (
            (COL_AQ, COL_AK, COL_AV, aqn_ref, akn_ref, ka_ref, va_ref, True),
            (COL_BQ, COL_BK, COL_BV, bqn_ref, bkn_ref, kb_ref, vb_ref, False)):
        q = _head_norm(u_ref[:, cq:cq + GROUP_W], qn_ref[lrow, :])
        k = _head_norm(u_ref[:, ck:ck + KV_W], kn_ref[lrow, :])
        v = u_ref[:, cv:cv + KV_W]
        k_out[...] = k
        v_out[...] = v
        groups.append((q * (HEAD_DIM ** -0.5), k, v, use_sink, []))
    for hd in range(N_HEADS):
        kv = hd // 2
        for qs, k, v, use_sink, heads in groups:
            s = _dot_nt(_head_cols(qs, hd), _head_cols(k, kv))
            sink = jnp.full((t, 1), sink_ref[layer, hd], F32) if use_sink else None
            heads.append(_softmax_pv([s], [_head_cols(v, kv)], sink))
    for g, (_, _, _, _, heads) in enumerate(groups):
        mixed_ref[:, g * GROUP_W:(g + 1) * GROUP_W] = jnp.concatenate(heads, axis=-1).astype(BF16)

    zero = jnp.zeros((1, GROUP_W), F32)
    oc, cf, cb = _rglru_finish(zero, zero, af_ref, bf_ref, ab_ref, bb_ref, hf_ref, hb_ref, gel_ref)
    mixed_ref[:, 2 * GROUP_W:3 * GROUP_W] = oc.astype(BF16)
    stc_ref[0:1, :] = cf
    stc_ref[1:2, :] = cb

    masks = _lane_head_masks(GROUP_W)
    lgf, lgb = _log_decays(theta_ref, masks)
    k8 = u_ref[:, COL_DK:COL_DK + GROUP_W] * (HEAD_DIM ** -0.5)
    vb = u_ref[:, COL_DV:COL_DV + GROUP_W].astype(BF16)
    final_states = _retention(u_ref[:, COL_DQ:COL_DQ + GROUP_W], k8, vb, None, lgf, lgb, masks, ret_ref)
    o = ret_ref[...]
    o = o * lax.rsqrt(_head_mean_square(o) + EPS) * dn_ref[lrow, :] * _silu(u_ref[:, COL_DG:COL_DG + GROUP_W])
    mixed_ref[:, 3 * GROUP_W:4 * GROUP_W] = o.astype(BF16)
    for d, s_full in enumerate(final_states):
        for hd in range(N_HEADS):
            std_ref[d, hd] = s_full[hd * HEAD_DIM:(hd + 1) * HEAD_DIM, hd * HEAD_DIM:(hd + 1) * HEAD_DIM]

    xn_ref[...] = x + g2_ref[0:1, :] * _dot(mixed_ref[...], wout_s[...])


def _ctx_mixers(x, mod, layer, prev, norm2_g, w_in, w_out,
                a_qn, a_kn, a_sink, b_qn, b_kn, c_conv_w, c_conv_b, c_wa, c_ba, c_wx, c_bx,
                c_lambda, d_theta, d_norm_g):
    per_request = lambda slots, shape: pl.BlockSpec((None, slots) + shape, lambda b: (b,) + (0,) * (1 + len(shape)))
    state_dims = [(SEQ, KV_W)] * 4 + [(2, GROUP_W), (2, N_HEADS, HEAD_DIM, HEAD_DIM)]
    scr = pltpu.VMEM((SEQ, GROUP_W), F32)
    out = pl.pallas_call(
        functools.partial(_ctx_mixer_kernel, layer=layer, n_prev=len(prev)),
        grid=(BATCH,),
        in_specs=[per_request(layer, dims) for dims in state_dims[:len(prev)]] + [
            pl.BlockSpec((SEQ, D_MODEL), lambda b: (b, 0)),
            _full((DEPTH, D_MODEL)),
            _mod_chunk(layer, 3), _mod_chunk(layer, 4), _mod_chunk(layer, 5),
            _once((None, D_MODEL, IN_WIDTH), lambda b: (layer, 0, 0)),
            _once((None, D_MODEL, D_MODEL), lambda b: (layer, 0, 0)),
            _full((DEPTH, HEAD_DIM)), _full((DEPTH, HEAD_DIM)), _full((DEPTH, HEAD_DIM)), _full((DEPTH, HEAD_DIM)),
            pl.BlockSpec(memory_space=pltpu.SMEM),
            _layer_block((4, GROUP_W), layer), _full((DEPTH, GROUP_W)),
            _layer_block((2, N_HEADS, HEAD_DIM, HEAD_DIM), layer), _layer_block((2, GROUP_W), layer),
            _layer_block((2, N_HEADS, HEAD_DIM, HEAD_DIM), layer), _layer_block((2, GROUP_W), layer),
            _layer_block((2, GROUP_W), layer),
            _layer_block((2, N_HEADS), layer), _full((DEPTH, GROUP_W)),
        ],
        out_specs=[pl.BlockSpec((SEQ, D_MODEL), lambda b: (b, 0))] + [
            per_request(layer + 1, dims) for dims in state_dims],
        out_shape=[jax.ShapeDtypeStruct((N_ROWS, D_MODEL), F32)] + [
            jax.ShapeDtypeStruct((BATCH, layer + 1) + dims, F32) for dims in state_dims],
        input_output_aliases={len(prev): 0},
        scratch_shapes=[pltpu.VMEM((D_MODEL, IN_WIDTH), BF16), pltpu.VMEM((D_MODEL, D_MODEL), BF16),
                        pltpu.VMEM((SEQ, IN_WIDTH), F32), pltpu.VMEM((SEQ, D_MODEL), BF16),
                        pltpu.VMEM((4, GROUP_W, GROUP_W), BF16)] + [scr] * 8,
        compiler_params=_cparams("arbitrary"),
        name="ctx_mixers",
    )(*prev, x, norm2_g, mod, mod, mod, w_in, w_out,
      a_qn, a_kn, b_qn, b_kn, a_sink, c_conv_w, c_conv_b, c_wa, c_ba, c_wx, c_bx,
      c_lambda, d_theta, d_norm_g)
    return out[0], tuple(out[1:])


LAT_BLOCK0 = N_CTX_ROWS // DEC_SEQ


def _lat_attn_kernel(x_ref, n2_ref, sh_ref, sc_ref, g2_ref, win_ref, wout_ref,
                     kca_ref, vca_ref, kcb_ref, vcb_ref,
                     aqn_ref, akn_ref, bqn_ref, bkn_ref, sink_ref, cos_ref, sinl_ref, sinh_ref,
                     xn_ref, h_ref, u_ref, o_ref, *, layer):
    t = DEC_SEQ
    lrow = slice(layer, layer + 1)
    mrow = pl.ds(1 + pl.program_id(0), 1)
    cos, sin_lo, sin_hi = cos_ref[...], sinl_ref[...], sinh_ref[...]
    scale = HEAD_DIM ** -0.5
    x = x_ref[...]
    h_ref[...] = _norm_mod(x, n2_ref[lrow, :], sc_ref[mrow, :], sh_ref[mrow, :]).astype(BF16)
    u_ref[...] = _dot(h_ref[...], win_ref[...].astype(BF16))

    q = _rope(_head_norm(u_ref[:, COL_AQ:COL_AQ + GROUP_W], aqn_ref[lrow, :]), cos, sin_lo, sin_hi)
    k = _rope(_head_norm(u_ref[:, COL_AK:COL_AK + KV_W], akn_ref[lrow, :]), cos, sin_lo, sin_hi)
    qh = [_head_cols(q * scale, h) for h in range(4)]
    v = u_ref[:, COL_AV:COL_AV + KV_W]
    kh = [_head_cols(k, kv) for kv in range(2)]
    vh = [_head_cols(v, kv) for kv in range(2)]
    kch = [_head_cols(kca_ref[...], kv) for kv in range(2)]
    vch = [_head_cols(vca_ref[...], kv) for kv in range(2)]
    w = ATT_BLOCK
    span = 3 * w
    for n in range(t // w):
        start = min(max((n - 1) * w, 0), t - span)
        rows = slice(n * w, (n + 1) * w)
        band = slice(start, start + span)
        qpos = (lax.broadcasted_iota(jnp.int32, (2 * w, span), 0) & (w - 1)) + n * w
        kpos = lax.broadcasted_iota(jnp.int32, (2 * w, span), 1) + start
        valid = jnp.abs(qpos - kpos) <= WINDOW
        heads = []
        for kv in range(2):
            qp = jnp.concatenate([qh[2 * kv][rows, :], qh[2 * kv + 1][rows, :]], axis=0)
            s_ctx = _dot_nt(qp, kch[kv])
            s_band = jnp.where(valid, _dot_nt(qp, kh[kv][band, :]), NEG_INF)
            row = lax.broadcasted_iota(jnp.int32, (2 * w, 1), 0)
            sink = jnp.where(row < w, sink_ref[layer, 2 * kv], sink_ref[layer, 2 * kv + 1])
            o = _softmax_pv([s_ctx, s_band], [vch[kv], vh[kv][band, :]], sink)
            heads += [o[0:w, :], o[w:2 * w, :]]
        o_ref[rows, 0:GROUP_W] = jnp.concatenate(heads, axis=-1).astype(BF16)

    q = _rope(_head_norm(u_ref[:, COL_BQ:COL_BQ + GROUP_W], bqn_ref[lrow, :]), cos, sin_lo, sin_hi)
    k = _rope(_head_norm(u_ref[:, COL_BK:COL_BK + KV_W], bkn_ref[lrow, :]), cos, sin_lo, sin_hi)
    qh = [_head_cols(q * scale, h) for h in range(4)]
    v = u_ref[:, COL_BV:COL_BV + KV_W]
    kh = [_head_cols(k, kv) for kv in range(2)]
    vh = [_head_cols(v, kv) for kv in range(2)]
    kch = [_head_cols(kcb_ref[...], kv) for kv in range(2)]
    vch = [_head_cols(vcb_ref[...], kv) for kv in range(2)]
    tq = 2 * ATT_BLOCK
    for n in range(t // tq):
        rows = slice(n * tq, (n + 1) * tq)
        heads = []
        for kv in range(2):
            qp = jnp.concatenate([qh[2 * kv][rows, :], qh[2 * kv + 1][rows, :]], axis=0)
            o = _softmax_pv([_dot_nt(qp, kch[kv]), _dot_nt(qp, kh[kv])], [vch[kv], vh[kv]], None)
            heads += [o[0:tq, :], o[tq:2 * tq, :]]
        o_ref[rows, GROUP_W:2 * GROUP_W] = jnp.concatenate(heads, axis=-1).astype(BF16)

    xn_ref[...] = x + g2_ref[mrow, :] * _dot(o_ref[...], wout_ref[...].astype(BF16))


def _lat_recurrent_kernel(xn_in_ref, h_ref, g2_ref, wc_ref, wqk_ref, wvg_ref, wout_ref, h0_ref,
                          convw_ref, convb_ref, wa_ref, ba_ref, wx_ref, bx_ref, lam_ref,
                          s0_ref, theta_ref, dn_ref,
                          xn_ref, gate_w_ref, af_ref, bf_ref, ab_ref, bb_ref, hf_ref, hb_ref, gel_ref, ret_ref,
                          *, layer):
    lrow = slice(layer, layer + 1)
    mrow = pl.ds(1 + pl.program_id(0), 1)

    @pl.when(pl.program_id(0) == 0)
    def _():
        _store_gate_weights(gate_w_ref, wa_ref, wx_ref)

    h = h_ref[...]
    u = _dot(h, wc_ref[...].astype(BF16))
    _rglru_prepare(u[:, 0:GROUP_W], u[:, GROUP_W:2 * GROUP_W], convw_ref, convb_ref[lrow, :],
                   gate_w_ref, ba_ref, bx_ref, lam_ref, af_ref, bf_ref, ab_ref, bb_ref, gel_ref)
    oc, _, _ = _rglru_finish(h0_ref[0:1, :], h0_ref[1:2, :],
                             af_ref, bf_ref, ab_ref, bb_ref, hf_ref, hb_ref, gel_ref)
    y = _dot(oc.astype(BF16), wout_ref[0:GROUP_W, :].astype(BF16))

    uqk = _dot(h, wqk_ref[...].astype(BF16))
    uvg = _dot(h, wvg_ref[...].astype(BF16))
    masks = _lane_head_masks(GROUP_W)
    lgf, lgb = _log_decays(theta_ref, masks)
    s0 = tuple(_block_diag([s0_ref[d, hd] for hd in range(N_HEADS)]) for d in range(2))
    _retention(uqk[:, 0:GROUP_W], uqk[:, GROUP_W:2 * GROUP_W] * (HEAD_DIM ** -0.5),
               uvg[:, 0:GROUP_W].astype(BF16), s0, lgf, lgb, masks, ret_ref)
    o = ret_ref[...]
    o = o * lax.rsqrt(_head_mean_square(o) + EPS) * dn_ref[lrow, :] * _silu(uvg[:, GROUP_W:2 * GROUP_W])
    y = y + _dot(o.astype(BF16), wout_ref[GROUP_W:2 * GROUP_W, :].astype(BF16))
    xn_ref[...] = xn_in_ref[...] + g2_ref[mrow, :] * y


def _lat_mixers(x, mod, layer, caches, state_c, state_d, rope, norm2_g, w_in, w_out,
                a_qn, a_kn, a_sink, b_qn, b_kn, c_conv_w, c_conv_b, c_wa, c_ba, c_wx, c_bx,
                c_lambda, d_theta, d_norm_g):
    rows = pl.BlockSpec((DEC_SEQ, D_MODEL), lambda b: (LAT_BLOCK0 + b, 0))
    h_rows = pl.BlockSpec((DEC_SEQ, D_MODEL), lambda b: (b, 0))
    cache_spec = pl.BlockSpec((None, None, PAST_LEN, KV_W), lambda b: (b, layer, 0, 0))
    gain = _full((DEPTH, HEAD_DIM))
    table = _once((DEC_SEQ, LANES), lambda b: (0, 0))
    out_shape = jax.ShapeDtypeStruct((N_ROWS, D_MODEL), F32)
    win_cols = lambda w, c: _once((None, D_MODEL, w), lambda b: (layer, 0, c))
    wout_rows = lambda h, r: _once((None, h, D_MODEL), lambda b: (layer, r, 0))

    xn, h = pl.pallas_call(
        functools.partial(_lat_attn_kernel, layer=layer),
        grid=(DEC_BATCH,),
        in_specs=[rows, _full((DEPTH, D_MODEL)),
                  _mod_chunk(layer, 3), _mod_chunk(layer, 4), _mod_chunk(layer, 5),
                  win_cols(4 * GROUP_W, 0), wout_rows(2 * GROUP_W, 0),
                  cache_spec, cache_spec, cache_spec, cache_spec,
                  gain, gain, gain, gain,
                  pl.BlockSpec(memory_space=pltpu.SMEM),
                  table, table, table],
        out_specs=[rows, pl.BlockSpec((DEC_SEQ, D_MODEL), lambda b: (b, 0), pipeline_mode=pl.Buffered(1))],
        out_shape=[out_shape, jax.ShapeDtypeStruct((N_LAT_ROWS, D_MODEL), BF16)],
        input_output_aliases={0: 0},
        scratch_shapes=[pltpu.VMEM((DEC_SEQ, 4 * GROUP_W), F32), pltpu.VMEM((DEC_SEQ, 2 * GROUP_W), BF16)],
        compiler_params=_cparams("arbitrary"),
        name="lat_attention",
    )(x, norm2_g, mod, mod, mod, w_in, w_out, *caches, a_qn, a_kn, b_qn, b_kn, a_sink, *rope)

    scr = pltpu.VMEM((DEC_SEQ, GROUP_W), F32)
    xn = pl.pallas_call(
        functools.partial(_lat_recurrent_kernel, layer=layer),
        grid=(DEC_BATCH,),
        in_specs=[
            rows, h_rows, _mod_chunk(layer, 5),
            win_cols(2 * GROUP_W, COL_CX // (2 * GROUP_W)),
            win_cols(2 * GROUP_W, COL_DQ // (2 * GROUP_W)), win_cols(2 * GROUP_W, COL_DV // (2 * GROUP_W)),
            wout_rows(2 * GROUP_W, 1),
            pl.BlockSpec((None, None, 2, GROUP_W), lambda b: (b, layer, 0, 0)),
            _layer_block((4, GROUP_W), layer), _full((DEPTH, GROUP_W)),
            _layer_block((2, N_HEADS, HEAD_DIM, HEAD_DIM), layer), _layer_block((2, GROUP_W), layer),
            _layer_block((2, N_HEADS, HEAD_DIM, HEAD_DIM), layer), _layer_block((2, GROUP_W), layer),
            _layer_block((2, GROUP_W), layer),
            pl.BlockSpec((None, None, 2, N_HEADS, HEAD_DIM, HEAD_DIM), lambda b: (b, layer, 0, 0, 0, 0)),
            _layer_block((2, N_HEADS), layer), _full((DEPTH, GROUP_W))],
        out_specs=rows,
        out_shape=out_shape,
        input_output_aliases={0: 0},
        scratch_shapes=[pltpu.VMEM((4, GROUP_W, GROUP_W), BF16)] + [scr] * 8,
        compiler_params=_cparams("arbitrary"),
        name="lat_recurrent",
    )(xn, h, mod, w_in, w_in, w_in, w_out, state_c, c_conv_w, c_conv_b, c_wa, c_ba, c_wx, c_bx, c_lambda,
      state_d, d_theta, d_norm_g)
    return xn


def _rope_tables():
    t = np.arange(DEC_SEQ)
    row = (t // GRID_W).astype(np.float64)[:, None]
    col = (t % GRID_W).astype(np.float64)[:, None]
    half = HEAD_DIM // 2
    inv = 1.0 / (ROPE_BASE ** (np.arange(0, half, 2, dtype=np.float64) / half))
    j = np.arange(LANES) % HEAD_DIM
    ang = np.where((j < half)[None, :], row, col) * inv[j % (half // 2)][None, :]
    first = ((j % half) < half // 2)[None, :]
    cos, sin = np.cos(ang), np.sin(ang)
    return tuple(jnp.asarray(a, F32) for a in (cos, np.where(first, -sin, 0.0), np.where(first, 0.0, sin)))


def kernel(x_prompt, x_sample, cache_a_k, cache_a_v, cache_b_k, cache_b_v, state_c, state_d, c, c_ctx, norm1_g, norm2_g, norm3_g, w_mod, b_mod, ffn1_wg, ffn1_wu, ffn1_wd, ffn2_wg, ffn2_wu, ffn2_wd, w_in, w_out, a_qn, a_kn, a_sink, b_qn, b_kn, c_conv_w, c_conv_b, c_wa, c_ba, c_wx, c_bx, c_lambda, d_theta, d_norm_g):
    mod = _modulation(c_ctx, c, w_mod, b_mod)
    rope = _rope_tables()
    caches = tuple(t.reshape(DEC_BATCH, DEPTH, PAST_LEN, KV_W) for t in (cache_a_k, cache_a_v, cache_b_k, cache_b_v))
    mixer_params = (a_qn, a_kn, a_sink, b_qn, b_kn, c_conv_w, c_conv_b, c_wa, c_ba, c_wx, c_bx,
                    c_lambda, d_theta, d_norm_g)
    xs = (x_prompt.reshape(N_CTX_ROWS, D_MODEL), x_sample.reshape(N_LAT_ROWS, D_MODEL))
    states = ()
    for l in range(DEPTH):
        (x,) = _ffn(xs, mod, l, 0, norm1_g, ffn1_wg, ffn1_wu, ffn1_wd)
        x, states = _ctx_mixers(x, mod, l, states, norm2_g, w_in, w_out, *mixer_params)
        x = _lat_mixers(x, mod, l, caches, state_c, state_d, rope, norm2_g, w_in, w_out, *mixer_params)
        xs = _ffn((x,), mod, l, 6, norm3_g, ffn2_wg, ffn2_wu, ffn2_wd, split_out=(l == DEPTH - 1))
    y_p, y_s = xs
    ka, va, kb, vb, st_c, st_d = states
    kv_shape = (BATCH, DEPTH, SEQ, 2, HEAD_DIM)
    return (y_p.reshape(BATCH, SEQ, D_MODEL), y_s.reshape(DEC_BATCH, DEC_SEQ, D_MODEL),
            ka.reshape(kv_shape), va.reshape(kv_shape), kb.reshape(kv_shape), vb.reshape(kv_shape),
            st_c, st_d)
```

```python
import functools
import math

import numpy as np
import jax
import jax.numpy as jnp
from jax import lax
from jax.experimental import pallas as pl
from jax.experimental.pallas import tpu as pltpu

F32 = jnp.float32
BF16 = jnp.bfloat16

D_MODEL = 1024
BATCH = 16
SEQ = 256
DEPTH = 2
DEC_BATCH = 2
DEC_SEQ = 1024
PAST_LEN = 512
GRID_W = 64
HEAD_DIM = 64
HEAD_SHIFT = 6
N_HEADS = 4
GROUP_W = 256
KV_W = 2 * HEAD_DIM
LANES = 128
WINDOW = 128
ATT_BLOCK = 128
ROPE_BASE = 10000.0
LRU_C = 8.0
D_FF = 2816
N_MOD = 9
EPS = 1e-6
NEG_INF = -1e30
IN_WIDTH = 2560

N_CTX_ROWS = BATCH * SEQ
N_LAT_ROWS = DEC_BATCH * DEC_SEQ
N_ROWS = N_CTX_ROWS + N_LAT_ROWS
MOD_ROWS = 8
MOD_GROUP = 1024

VMEM_LIMIT_BYTES = 56 * 1024 * 1024

COL_AQ, COL_AK, COL_AV = 0, 256, 384
COL_BQ, COL_BK, COL_BV = 512, 768, 896
COL_CX, COL_CY = 1024, 1280
COL_DQ, COL_DK, COL_DV, COL_DG = 1536, 1792, 2048, 2304


def _cparams(*sem):
    return pltpu.CompilerParams(dimension_semantics=sem, vmem_limit_bytes=VMEM_LIMIT_BYTES)


def _dot(a, b):
    return jnp.dot(a, b, preferred_element_type=F32)


def _dot_nt(a, b):
    return lax.dot_general(a, b, (((1,), (1,)), ((), ())), preferred_element_type=F32)


def _dot_tn(a, b):
    return lax.dot_general(a, b, (((0,), (0,)), ((), ())), preferred_element_type=F32)


def _sigmoid(x):
    return 0.5 * jnp.tanh(0.5 * x) + 0.5


def _silu(x):
    return x * _sigmoid(x)


def _gelu_tanh(x):
    return 0.5 * x * (1.0 + jnp.tanh(math.sqrt(2.0 / math.pi) * (x + 0.044715 * (x * x * x))))


def _mod_row(i, tm, s):
    if tm >= MOD_GROUP:
        block_index = i * (tm // MOD_GROUP) + s
    else:
        block_index = i >> int(math.log2(MOD_GROUP // tm))
    return jnp.maximum(block_index - (N_CTX_ROWS // MOD_GROUP - 1), 0)


def _norm_mod(x, g, sc, sh):
    ms = jnp.mean(x * x, axis=-1, keepdims=True)
    return (x * lax.rsqrt(ms + EPS) * g) * (1.0 + sc) + sh


def _full(shape):
    return pl.BlockSpec(shape, lambda *_: (0,) * len(shape))


def _layer_block(shape, layer):
    return pl.BlockSpec((None,) + shape, lambda *_: (layer,) + (0,) * len(shape))


MOD_TN = 3072


def _mod_kernel(cc_ref, c_ref, w_ref, b_ref, o_ref):
    l = pl.program_id(0)
    pad = jnp.zeros((MOD_ROWS - 1 - DEC_BATCH, D_MODEL), F32)
    cond = jnp.concatenate([cc_ref[...], c_ref[...], pad], axis=0)
    o_ref[...] = _dot(_silu(cond).astype(BF16), w_ref[...].astype(BF16)) + b_ref[pl.ds(l, 1), :]


def _modulation(c_ctx, c, w_mod, b_mod):
    n = N_MOD * D_MODEL
    return pl.pallas_call(
        _mod_kernel,
        grid=(DEPTH, n // MOD_TN),
        in_specs=[
            pl.BlockSpec((1, D_MODEL), lambda l, j: (0, 0)),
            pl.BlockSpec((DEC_BATCH, D_MODEL), lambda l, j: (0, 0)),
            pl.BlockSpec((None, D_MODEL, MOD_TN), lambda l, j: (l, 0, j)),
            pl.BlockSpec((DEPTH, MOD_TN), lambda l, j: (0, j)),
        ],
        out_specs=pl.BlockSpec((None, MOD_ROWS, MOD_TN), lambda l, j: (l, 0, j)),
        out_shape=jax.ShapeDtypeStruct((DEPTH, MOD_ROWS, n), F32),
        compiler_params=_cparams("arbitrary", "arbitrary"),
        name="modulation",
    )(c_ctx.reshape(1, D_MODEL), c, w_mod, b_mod)


FFN_TM = 1024
FFN_TF = 256
N_CTX_TILES = N_CTX_ROWS // FFN_TM


FFN_NJ = D_FF // FFN_TF
N_FFN_TILES = N_ROWS // FFN_TM
N_FFN_STEPS = FFN_NJ + N_FFN_TILES


def _ffn_tile(step):
    return jnp.maximum(step - FFN_NJ, 0)


def _on_stream_part(tile, x_refs, o_refs, fn):
    if len(x_refs) == 1 and len(o_refs) == 1:
        fn(x_refs[0], o_refs[0])
    else:
        pl.when(tile < N_CTX_TILES)(lambda: fn(x_refs[0], o_refs[0]))
        pl.when(tile >= N_CTX_TILES)(lambda: fn(x_refs[-1], o_refs[-1]))


def _ffn_kernel(*refs, layer, n_in, n_out):
    x_refs = refs[:n_in]
    n_ref, sh_ref, sc_ref, g_ref, wg_ref, wu_ref, wd_ref = refs[n_in:n_in + 7]
    o_refs = refs[n_in + 7:n_in + 7 + n_out]
    h_ref, a_ref, wg_s, wu_s, wd_s = refs[n_in + 7 + n_out:]
    nj, tf = FFN_NJ, FFN_TF
    s = pl.program_id(0)
    tile = _ffn_tile(s)
    r = _mod_row(tile, FFN_TM, 0)

    def load_tile():
        x = x_refs[0][...] if len(x_refs) == 1 else jnp.where(tile < N_CTX_TILES, x_refs[0][...], x_refs[1][...])
        h = _norm_mod(x, n_ref[layer:layer + 1, :], sc_ref[pl.ds(r, 1), :], sh_ref[pl.ds(r, 1), :])
        h_ref[...] = h.astype(BF16)

    def up_chunk(j, cols):
        h = h_ref[...]
        a_ref[:, cols] = (_silu(_dot(h, wg_s[j])) * _dot(h, wu_s[j])).astype(BF16)

    def down_and_store():
        y = (0.5 * g_ref[pl.ds(r, 1), :]) * _dot(a_ref[...], wd_s[...])

        def store(x_ref, o_ref):
            o_ref[...] = x_ref[...] + y
        _on_stream_part(tile, x_refs, o_refs, store)

    def keep_arrived_chunk():
        wg_s[s] = wg_ref[...].astype(BF16)
        wu_s[s] = wu_ref[...].astype(BF16)
        wd_s[pl.ds(pl.multiple_of(s * tf, tf), tf), :] = wd_ref[...].astype(BF16)

    def up_previous_chunk():
        up_chunk(s - 1, pl.ds(pl.multiple_of((s - 1) * tf, tf), tf))

    @pl.when(s == 0)
    def _():
        load_tile()
        keep_arrived_chunk()

    @pl.when((s > 0) & (s < nj))
    def _():
        up_previous_chunk()
        keep_arrived_chunk()

    @pl.when(s == nj)
    def _():
        up_previous_chunk()
        down_and_store()

    @pl.when(s > nj)
    def _():
        load_tile()
        for j in range(nj):
            up_chunk(j, slice(j * tf, (j + 1) * tf))
        down_and_store()


def _stream_specs(split, buffered_once):
    tm = FFN_TM
    kw = {"pipeline_mode": pl.Buffered(1)} if buffered_once else {}
    if not split:
        return [pl.BlockSpec((tm, D_MODEL), lambda s: (_ffn_tile(s), 0), **kw)]
    last_ctx = N_CTX_TILES - 1
    return [pl.BlockSpec((tm, D_MODEL), lambda s: (jnp.minimum(_ffn_tile(s), last_ctx), 0), **kw),
            pl.BlockSpec((tm, D_MODEL), lambda s: (jnp.maximum(_ffn_tile(s) - N_CTX_TILES, 0), 0), **kw)]


def _ffn(xs, mod, layer, chunk0, norm_g, wg, wu, wd, split_out=False):
    tm, tf, nj = FFN_TM, FFN_TF, FFN_NJ
    split_in = len(xs) == 2
    mod_spec = lambda c: pl.BlockSpec((None, MOD_ROWS, D_MODEL), lambda s: (layer, 0, c))
    w_col = lambda s: (layer, 0, jnp.minimum(s, nj - 1))
    w_row = lambda s: (layer, jnp.minimum(s, nj - 1), 0)
    if split_out:
        out_shape = [jax.ShapeDtypeStruct((N_CTX_ROWS, D_MODEL), F32),
                     jax.ShapeDtypeStruct((N_LAT_ROWS, D_MODEL), F32)]
    else:
        out_shape = [jax.ShapeDtypeStruct((N_ROWS, D_MODEL), F32)]
    out = pl.pallas_call(
        functools.partial(_ffn_kernel, layer=layer, n_in=len(xs), n_out=len(out_shape)),
        grid=(N_FFN_STEPS,),
        in_specs=_stream_specs(split_in, False) + [
            _full((DEPTH, D_MODEL)),
            mod_spec(chunk0), mod_spec(chunk0 + 1), mod_spec(chunk0 + 2),
            pl.BlockSpec((None, D_MODEL, tf), w_col),
            pl.BlockSpec((None, D_MODEL, tf), w_col),
            pl.BlockSpec((None, tf, D_MODEL), w_row),
        ],
        out_specs=_stream_specs(split_out, True),
        out_shape=out_shape,
        scratch_shapes=[pltpu.VMEM((tm, D_MODEL), BF16),
                        pltpu.VMEM((tm, D_FF), BF16),
                        pltpu.VMEM((nj, D_MODEL, tf), BF16),
                        pltpu.VMEM((nj, D_MODEL, tf), BF16),
                        pltpu.VMEM((D_FF, D_MODEL), BF16)],
        compiler_params=_cparams("arbitrary"),
        name="ffn",
    )(*xs, norm_g, mod, mod, mod, wg, wu, wd)
    return tuple(out)


def _once(shape, index_map):
    return pl.BlockSpec(shape, index_map, pipeline_mode=pl.Buffered(1))


def _mod_chunk(layer, c):
    return pl.BlockSpec((None, MOD_ROWS, D_MODEL), lambda *_: (layer, 0, c))


def _head_mean_square(x):
    n = x.shape[-1]
    r = lax.broadcasted_iota(jnp.int32, (n, n), 0) >> HEAD_SHIFT
    c = lax.broadcasted_iota(jnp.int32, (n, n), 1) >> HEAD_SHIFT
    ones_bd = jnp.where(r == c, 1.0, 0.0).astype(BF16)
    return _dot((x * x).astype(BF16), ones_bd) * (1.0 / HEAD_DIM)


def _head_norm(x, head_gain):
    gain_row = jnp.concatenate([head_gain] * (x.shape[-1] // HEAD_DIM), axis=-1)
    return x * lax.rsqrt(_head_mean_square(x) + EPS) * gain_row


def _head_cols(x, h):
    return x[:, h * HEAD_DIM:(h + 1) * HEAD_DIM].astype(BF16)


def _softmax_pv(scores, values, sink):
    m = jnp.max(scores[0], axis=-1, keepdims=True)
    for s in scores[1:]:
        m = jnp.maximum(m, jnp.max(s, axis=-1, keepdims=True))
    if sink is not None:
        m = jnp.maximum(m, sink)
    denom = None
    acc = None
    for s, v in zip(scores, values):
        p = jnp.exp(s - m)
        d = jnp.sum(p, axis=-1, keepdims=True)
        o = _dot(p.astype(BF16), v)
        denom = d if denom is None else denom + d
        acc = o if acc is None else acc + o
    if sink is not None:
        denom = denom + jnp.exp(sink - m)
    return acc / denom


def _rope(x, cos, sin_lo, sin_hi):
    cols = []
    for c in range(x.shape[-1] // LANES):
        xc = x[:, c * LANES:(c + 1) * LANES]
        cols.append(xc * cos + pltpu.roll(xc, 112, 1) * sin_lo + pltpu.roll(xc, 16, 1) * sin_hi)
    return cols[0] if len(cols) == 1 else jnp.concatenate(cols, axis=-1)


def _block_diag(blocks):
    n = len(blocks)
    w = blocks[0].shape[0]
    rows = []
    for k, blk in enumerate(blocks):
        parts = []
        if k > 0:
            parts.append(jnp.zeros((w, k * w), F32))
        parts.append(blk)
        if k < n - 1:
            parts.append(jnp.zeros((w, (n - 1 - k) * w), F32))
        rows.append(jnp.concatenate(parts, axis=-1))
    return jnp.concatenate(rows, axis=0)


def _rglru_gates(xc, wa, ba, wx, bx, lam):
    xb = xc.astype(BF16)
    r = _sigmoid(_dot(xb, wa) + ba)
    i = _sigmoid(_dot(xb, wx) + bx)
    softplus = jnp.maximum(-lam, 0.0) + jnp.log1p(jnp.exp(-jnp.abs(lam)))
    log_a = (-LRU_C) * r * softplus
    a = jnp.exp(log_a)
    b = jnp.sqrt(1.0 - a * a) * (i * xc)
    return a, b


def _block_prefix(a, b, reverse):
    t = a.shape[0]
    row = lax.broadcasted_iota(jnp.int32, a.shape, 0) & 7
    for d in (1, 2, 4):
        if reverse:
            a_s = pltpu.roll(a, t - d, 0)
            b_s = pltpu.roll(b, t - d, 0)
            ok = row < 8 - d
        else:
            a_s = pltpu.roll(a, d, 0)
            b_s = pltpu.roll(b, d, 0)
            ok = row >= d
        b = jnp.where(ok, a * b_s + b, b)
        a = jnp.where(ok, a * a_s, a)
    return a, b


def _conv4(x, w_ref, b_row):
    t = x.shape[0]
    row = lax.broadcasted_iota(jnp.int32, x.shape, 0)
    xm2 = jnp.where(row >= 2, pltpu.roll(x, 2, 0), 0.0)
    xm1 = jnp.where(row >= 1, pltpu.roll(x, 1, 0), 0.0)
    xp1 = jnp.where(row < t - 1, pltpu.roll(x, t - 1, 0), 0.0)
    return (xm2 * w_ref[0:1, :] + xm1 * w_ref[1:2, :] + x * w_ref[2:3, :] + xp1 * w_ref[3:4, :]) + b_row


def _rglru_prepare(cx, cy, conv_w_ref, conv_b, gate_w_ref, ba_ref, bx_ref, lam_ref,
                   af_ref, bf_ref, ab_ref, bb_ref, gel_ref):
    xc = _conv4(cx, conv_w_ref, conv_b)
    a, b = _rglru_gates(xc, gate_w_ref[0], ba_ref[0:1, :], gate_w_ref[1], bx_ref[0:1, :], lam_ref[0:1, :])
    a, b = _block_prefix(a, b, reverse=False)
    af_ref[...] = a
    bf_ref[...] = b
    a, b = _rglru_gates(xc, gate_w_ref[2], ba_ref[1:2, :], gate_w_ref[3], bx_ref[1:2, :], lam_ref[1:2, :])
    a, b = _block_prefix(a, b, reverse=True)
    ab_ref[...] = a
    bb_ref[...] = b
    gel_ref[...] = _gelu_tanh(cy)


SCAN_UNROLL = 8


def _rglru_finish(h0f, h0b, af_ref, bf_ref, ab_ref, bb_ref, hf_ref, hb_ref, gel_ref):
    nblk = af_ref.shape[0] // 8

    def body(k, carry):
        cf, cb = carry
        rf = pl.ds(pl.multiple_of(k * 8, 8), 8)
        hf = bf_ref[rf, :] + af_ref[rf, :] * cf
        hf_ref[rf, :] = hf
        rb = pl.ds(pl.multiple_of((nblk - 1 - k) * 8, 8), 8)
        hb = bb_ref[rb, :] + ab_ref[rb, :] * cb
        hb_ref[rb, :] = hb
        return hf[7:8, :], hb[0:1, :]

    cf, cb = lax.fori_loop(0, nblk, body, (h0f, h0b), unroll=SCAN_UNROLL)
    oc = (hf_ref[...] + hb_ref[...]) * gel_ref[...]
    return oc, cf, cb


def _store_gate_weights(gate_w_ref, wa_ref, wx_ref):
    for d in range(2):
        gate_w_ref[2 * d] = _block_diag([wa_ref[d, n] for n in range(N_HEADS)]).astype(BF16)
        gate_w_ref[2 * d + 1] = _block_diag([wx_ref[d, n] for n in range(N_HEADS)]).astype(BF16)


def _lane_head_masks(n):
    lane = lax.broadcasted_iota(jnp.int32, (1, n), 1) >> HEAD_SHIFT
    return [jnp.where(lane == h, 1.0, 0.0) for h in range(n // HEAD_DIM)]


def _log_decays(theta_ref, masks):
    theta = theta_ref[...]
    lanes = theta[:, 0:1] * masks[0]
    for h in range(1, N_HEADS):
        lanes = lanes + theta[:, h:h + 1] * masks[h]
    lg = jnp.log1p(-jnp.exp(lanes))
    return lg[0:1, :], lg[1:2, :]


RET_BLOCK = 256


def _retention(q, k8, vb, s0, lgf, lgb, masks, o_ref):
    t, w = q.shape
    c = RET_BLOCK
    nh = w // HEAD_DIM
    pos = lax.broadcasted_iota(jnp.int32, (c, w), 0).astype(F32)
    q_dec = (jnp.exp(lgf * (pos + 1.0)), jnp.exp(lgb * (float(c) - pos)))
    k_dec = (jnp.exp(lgf * (float(c - 1) - pos)), jnp.exp(lgb * pos))
    chunk_dec = (jnp.exp(lgf * float(c)), jnp.exp(lgb * float(c)))
    rel = (lax.broadcasted_iota(jnp.int32, (c, c), 0) - lax.broadcasted_iota(jnp.int32, (c, c), 1)).astype(F32)
    decs = []
    for h in range(nh):
        gf = lgf[:, h * HEAD_DIM:h * HEAD_DIM + 1]
        gb = lgb[:, h * HEAD_DIM:h * HEAD_DIM + 1]
        e = jnp.exp(jnp.where(rel >= 0, gf * rel, gb * (-rel)))
        decs.append(jnp.where(rel == 0, 2.0, e))
    dec = jnp.concatenate(decs, axis=0)
    r_head = lax.broadcasted_iota(jnp.int32, (w, w), 0) >> HEAD_SHIFT
    c_head = lax.broadcasted_iota(jnp.int32, (w, w), 1) >> HEAD_SHIFT
    same_head = jnp.where(r_head == c_head, 1.0, 0.0)
    states = [None, None] if s0 is None else list(s0)

    def carry(d, rows, o):
        if states[d] is not None:
            o = o + _dot((q[rows, :] * q_dec[d]).astype(BF16), states[d].astype(BF16))
        upd = _dot_tn((k8[rows, :] * k_dec[d]).astype(BF16), vb[rows, :]) * same_head
        states[d] = upd if states[d] is None else states[d] * chunk_dec[d] + upd
        return o

    for ci in range(t // c):
        rows = slice(ci * c, (ci + 1) * c)
        qc = q[rows, :]
        q_stack = jnp.concatenate([(qc * masks[h]).astype(BF16) for h in range(nh)], axis=0)
        inner = (_dot_nt(q_stack, k8[rows, :].astype(BF16)) * dec).astype(BF16)
        out = _dot(inner, vb[rows, :])
        o = out[0:c, :] * masks[0]
        for h in range(1, nh):
            o = o + out[h * c:(h + 1) * c, :] * masks[h]
        o_ref[rows, :] = carry(0, rows, o)
    for ci in reversed(range(t // c)):
        rows = slice(ci * c, (ci + 1) * c)
        if states[1] is not None:
            o_ref[rows, :] = carry(1, rows, o_ref[rows, :])
        else:
            carry(1, rows, None)
    return states[0], states[1]


def _ctx_mixer_kernel(*refs, layer, n_prev):
    prev_refs = refs[:n_prev]
    (x_ref, n2_ref, sh_ref, sc_ref, g2_ref, win_ref, wout_ref,
     aqn_ref, akn_ref, bqn_ref, bkn_ref, sink_ref,
     convw_ref, convb_ref, wa_ref, ba_ref, wx_ref, bx_ref, lam_ref, theta_ref, dn_ref,
     xn_ref, *state_refs) = refs[n_prev:n_prev + 28]
    (win_s, wout_s, u_ref, mixed_ref,
     gate_w_ref, af_ref, bf_ref, ab_ref, bb_ref, hf_ref, hb_ref, gel_ref, ret_ref) = refs[n_prev + 28:]
    t = SEQ
    lrow = slice(layer, layer + 1)
    for prev_ref, state_ref in zip(prev_refs, state_refs):
        for earlier in range(layer):
            state_ref[earlier] = prev_ref[earlier]
    ka_ref, va_ref, kb_ref, vb_ref, stc_ref, std_ref = (ref.at[layer] for ref in state_refs)

    @pl.when(pl.program_id(0) == 0)
    def _():
        for c0 in range(0, IN_WIDTH, 2 * GROUP_W):
            win_s[:, c0:c0 + 2 * GROUP_W] = win_ref[:, c0:c0 + 2 * GROUP_W].astype(BF16)
        wout_s[...] = wout_ref[...].astype(BF16)
        _store_gate_weights(gate_w_ref, wa_ref, wx_ref)

    x = x_ref[...]
    h = _norm_mod(x, n2_ref[lrow, :], sc_ref[0:1, :], sh_ref[0:1, :]).astype(BF16)
    c_cols = slice(COL_CX, COL_CX + 2 * GROUP_W)
    u_ref[:, c_cols] = _dot(h, win_s[:, c_cols])
    _rglru_prepare(u_ref[:, COL_CX:COL_CX + GROUP_W], u_ref[:, COL_CY:COL_CY + GROUP_W],
                   convw_ref, convb_ref[lrow, :], gate_w_ref, ba_ref, bx_ref, lam_ref,
                   af_ref, bf_ref, ab_ref, bb_ref, gel_ref)
    u_ref[:, 0:COL_CX] = _dot(h, win_s[:, 0:COL_CX])
    u_ref[:, COL_DQ:IN_WIDTH] = _dot(h, win_s[:, COL_DQ:IN_WIDTH])

    groups = []
    for (cq, ck, cv, qn_ref, kn_ref, k_out, v_out, use_sink) in (
            (COL_AQ, COL_AK, COL_AV, aqn_ref, akn_ref, ka_ref, va_ref, True),
            (COL_BQ, COL_BK, COL_BV, bqn_ref, bkn_ref, kb_ref, vb_ref, False)):
        q = _head_norm(u_ref[:, cq:cq + GROUP_W], qn_ref[lrow, :])
        k = _head_norm(u_ref[:, ck:ck + KV_W], kn_ref[lrow, :])
        v = u_ref[:, cv:cv + KV_W]
        k_out[...] = k
        v_out[...] = v
        groups.append((q * (HEAD_DIM ** -0.5), k, v, use_sink, []))
    for hd in range(N_HEADS):
        kv = hd // 2
        scores = [_dot_nt(_head_cols(qs, hd), _head_cols(k, kv)) for qs, k, _, _, _ in groups]
        for s, (_, _, v, use_sink, heads) in zip(scores, groups):
            sink = jnp.full((t, 1), sink_ref[layer, hd], F32) if use_sink else None
            heads.append(_softmax_pv([s], [_head_cols(v, kv)], sink))
    for g, (_, _, _, _, heads) in enumerate(groups):
        mixed_ref[:, g * GROUP_W:(g + 1) * GROUP_W] = jnp.concatenate(heads, axis=-1).astype(BF16)

    zero = jnp.zeros((1, GROUP_W), F32)
    oc, cf, cb = _rglru_finish(zero, zero, af_ref, bf_ref, ab_ref, bb_ref, hf_ref, hb_ref, gel_ref)
    mixed_ref[:, 2 * GROUP_W:3 * GROUP_W] = oc.astype(BF16)
    stc_ref[0:1, :] = cf
    stc_ref[1:2, :] = cb

    masks = _lane_head_masks(GROUP_W)
    lgf, lgb = _log_decays(theta_ref, masks)
    k8 = u_ref[:, COL_DK:COL_DK + GROUP_W] * (HEAD_DIM ** -0.5)
    vb = u_ref[:, COL_DV:COL_DV + GROUP_W].astype(BF16)
    final_states = _retention(u_ref[:, COL_DQ:COL_DQ + GROUP_W], k8, vb, None, lgf, lgb, masks, ret_ref)
    o = ret_ref[...]
    o = o * lax.rsqrt(_head_mean_square(o) + EPS) * dn_ref[lrow, :] * _silu(u_ref[:, COL_DG:COL_DG + GROUP_W])
    mixed_ref[:, 3 * GROUP_W:4 * GROUP_W] = o.astype(BF16)
    for d, s_full in enumerate(final_states):
        for hd in range(N_HEADS):
            std_ref[d, hd] = s_full[hd * HEAD_DIM:(hd + 1) * HEAD_DIM, hd * HEAD_DIM:(hd + 1) * HEAD_DIM]

    xn_ref[...] = x + g2_ref[0:1, :] * _dot(mixed_ref[...], wout_s[...])


def _ctx_mixers(x, mod, layer, prev, norm2_g, w_in, w_out,
                a_qn, a_kn, a_sink, b_qn, b_kn, c_conv_w, c_conv_b, c_wa, c_ba, c_wx, c_bx,
                c_lambda, d_theta, d_norm_g):
    per_request = lambda slots, shape: pl.BlockSpec((None, slots) + shape, lambda b: (b,) + (0,) * (1 + len(shape)))
    state_dims = [(SEQ, KV_W)] * 4 + [(2, GROUP_W), (2, N_HEADS, HEAD_DIM, HEAD_DIM)]
    scr = pltpu.VMEM((SEQ, GROUP_W), F32)
    out = pl.pallas_call(
        functools.partial(_ctx_mixer_kernel, layer=layer, n_prev=len(prev)),
        grid=(BATCH,),
        in_specs=[per_request(layer, dims) for dims in state_dims[:len(prev)]] + [
            pl.BlockSpec((SEQ, D_MODEL), lambda b: (b, 0)),
            _full((DEPTH, D_MODEL)),
            _mod_chunk(layer, 3), _mod_chunk(layer, 4), _mod_chunk(layer, 5),
            _once((None, D_MODEL, IN_WIDTH), lambda b: (layer, 0, 0)),
            _once((None, D_MODEL, D_MODEL), lambda b: (layer, 0, 0)),
            _full((DEPTH, HEAD_DIM)), _full((DEPTH, HEAD_DIM)), _full((DEPTH, HEAD_DIM)), _full((DEPTH, HEAD_DIM)),
            pl.BlockSpec(memory_space=pltpu.SMEM),
            _layer_block((4, GROUP_W), layer), _full((DEPTH, GROUP_W)),
            _layer_block((2, N_HEADS, HEAD_DIM, HEAD_DIM), layer), _layer_block((2, GROUP_W), layer),
            _layer_block((2, N_HEADS, HEAD_DIM, HEAD_DIM), layer), _layer_block((2, GROUP_W), layer),
            _layer_block((2, GROUP_W), layer),
            _layer_block((2, N_HEADS), layer), _full((DEPTH, GROUP_W)),
        ],
        out_specs=[pl.BlockSpec((SEQ, D_MODEL), lambda b: (b, 0))] + [
            per_request(layer + 1, dims) for dims in state_dims],
        out_shape=[jax.ShapeDtypeStruct((N_ROWS, D_MODEL), F32)] + [
            jax.ShapeDtypeStruct((BATCH, layer + 1) + dims, F32) for dims in state_dims],
        input_output_aliases={len(prev): 0},
        scratch_shapes=[pltpu.VMEM((D_MODEL, IN_WIDTH), BF16), pltpu.VMEM((D_MODEL, D_MODEL), BF16),
                        pltpu.VMEM((SEQ, IN_WIDTH), F32), pltpu.VMEM((SEQ, D_MODEL), BF16),
                        pltpu.VMEM((4, GROUP_W, GROUP_W), BF16)] + [scr] * 8,
        compiler_params=_cparams("arbitrary"),
        name="ctx_mixers",
    )(*prev, x, norm2_g, mod, mod, mod, w_in, w_out,
      a_qn, a_kn, b_qn, b_kn, a_sink, c_conv_w, c_conv_b, c_wa, c_ba, c_wx, c_bx,
      c_lambda, d_theta, d_norm_g)
    return out[0], tuple(out[1:])


LAT_BLOCK0 = N_CTX_ROWS // DEC_SEQ


def _lat_attn_kernel(x_ref, n2_ref, sh_ref, sc_ref, g2_ref, win_ref, wout_ref,
                     kca_ref, vca_ref, kcb_ref, vcb_ref,
                     aqn_ref, akn_ref, bqn_ref, bkn_ref, sink_ref, cos_ref, sinl_ref, sinh_ref,
                     xn_ref, h_ref, u_ref, o_ref, *, layer):
    t = DEC_SEQ
    lrow = slice(layer, layer + 1)
    mrow = pl.ds(1 + pl.program_id(0), 1)
    cos, sin_lo, sin_hi = cos_ref[...], sinl_ref[...], sinh_ref[...]
    scale = HEAD_DIM ** -0.5
    x = x_ref[...]
    h_ref[...] = _norm_mod(x, n2_ref[lrow, :], sc_ref[mrow, :], sh_ref[mrow, :]).astype(BF16)
    u_ref[...] = _dot(h_ref[...], win_ref[...].astype(BF16))

    q = _rope(_head_norm(u_ref[:, COL_AQ:COL_AQ + GROUP_W], aqn_ref[lrow, :]), cos, sin_lo, sin_hi)
    k = _rope(_head_norm(u_ref[:, COL_AK:COL_AK + KV_W], akn_ref[lrow, :]), cos, sin_lo, sin_hi)
    qh = [_head_cols(q * scale, h) for h in range(4)]
    v = u_ref[:, COL_AV:COL_AV + KV_W]
    kh = [_head_cols(k, kv) for kv in range(2)]
    vh = [_head_cols(v, kv) for kv in range(2)]
    kch = [_head_cols(kca_ref[...], kv) for kv in range(2)]
    vch = [_head_cols(vca_ref[...], kv) for kv in range(2)]
    w = ATT_BLOCK
    span = 3 * w
    for n in range(t // w):
        start = min(max((n - 1) * w, 0), t - span)
        rows = slice(n * w, (n + 1) * w)
        band = slice(start, start + span)
        qpos = (lax.broadcasted_iota(jnp.int32, (2 * w, span), 0) & (w - 1)) + n * w
        kpos = lax.broadcasted_iota(jnp.int32, (2 * w, span), 1) + start
        valid = jnp.abs(qpos - kpos) <= WINDOW
        heads = []
        for kv in range(2):
            qp = jnp.concatenate([qh[2 * kv][rows, :], qh[2 * kv + 1][rows, :]], axis=0)
            s_ctx = _dot_nt(qp, kch[kv])
            s_band = jnp.where(valid, _dot_nt(qp, kh[kv][band, :]), NEG_INF)
            row = lax.broadcasted_iota(jnp.int32, (2 * w, 1), 0)
            sink = jnp.where(row < w, sink_ref[layer, 2 * kv], sink_ref[layer, 2 * kv + 1])
            o = _softmax_pv([s_ctx, s_band], [vch[kv], vh[kv][band, :]], sink)
            heads += [o[0:w, :], o[w:2 * w, :]]
        o_ref[rows, 0:GROUP_W] = jnp.concatenate(heads, axis=-1).astype(BF16)

    q = _rope(_head_norm(u_ref[:, COL_BQ:COL_BQ + GROUP_W], bqn_ref[lrow, :]), cos, sin_lo, sin_hi)
    k = _rope(_head_norm(u_ref[:, COL_BK:COL_BK + KV_W], bkn_ref[lrow, :]), cos, sin_lo, sin_hi)
    qh = [_head_cols(q * scale, h) for h in range(4)]
    v = u_ref[:, COL_BV:COL_BV + KV_W]
    kh = [_head_cols(k, kv) for kv in range(2)]
    vh = [_head_cols(v, kv) for kv in range(2)]
    kch = [_head_cols(kcb_ref[...], kv) for kv in range(2)]
    vch = [_head_cols(vcb_ref[...], kv) for kv in range(2)]
    tq = 2 * ATT_BLOCK
    for n in range(t // tq):
        rows = slice(n * tq, (n + 1) * tq)
        heads = []
        for kv in range(2):
            qp = jnp.concatenate([qh[2 * kv][rows, :], qh[2 * kv + 1][rows, :]], axis=0)
            o = _softmax_pv([_dot_nt(qp, kch[kv]), _dot_nt(qp, kh[kv])], [vch[kv], vh[kv]], None)
            heads += [o[0:tq, :], o[tq:2 * tq, :]]
        o_ref[rows, GROUP_W:2 * GROUP_W] = jnp.concatenate(heads, axis=-1).astype(BF16)

    xn_ref[...] = x + g2_ref[mrow, :] * _dot(o_ref[...], wout_ref[...].astype(BF16))


def _lat_recurrent_kernel(xn_in_ref, h_ref, g2_ref, wc_ref, wqk_ref, wvg_ref, wout_ref, h0_ref,
                          convw_ref, convb_ref, wa_ref, ba_ref, wx_ref, bx_ref, lam_ref,
                          s0_ref, theta_ref, dn_ref,
                          xn_ref, gate_w_ref, af_ref, bf_ref, ab_ref, bb_ref, hf_ref, hb_ref, gel_ref, ret_ref,
                          *, layer):
    lrow = slice(layer, layer + 1)
    mrow = pl.ds(1 + pl.program_id(0), 1)

    @pl.when(pl.program_id(0) == 0)
    def _():
        _store_gate_weights(gate_w_ref, wa_ref, wx_ref)

    h = h_ref[...]
    u = _dot(h, wc_ref[...].astype(BF16))
    _rglru_prepare(u[:, 0:GROUP_W], u[:, GROUP_W:2 * GROUP_W], convw_ref, convb_ref[lrow, :],
                   gate_w_ref, ba_ref, bx_ref, lam_ref, af_ref, bf_ref, ab_ref, bb_ref, gel_ref)
    oc, _, _ = _rglru_finish(h0_ref[0:1, :], h0_ref[1:2, :],
                             af_ref, bf_ref, ab_ref, bb_ref, hf_ref, hb_ref, gel_ref)
    y = _dot(oc.astype(BF16), wout_ref[0:GROUP_W, :].astype(BF16))

    uqk = _dot(h, wqk_ref[...].astype(BF16))
    uvg = _dot(h, wvg_ref[...].astype(BF16))
    masks = _lane_head_masks(GROUP_W)
    lgf, lgb = _log_decays(theta_ref, masks)
    s0 = tuple(_block_diag([s0_ref[d, hd] for hd in range(N_HEADS)]) for d in range(2))
    _retention(uqk[:, 0:GROUP_W], uqk[:, GROUP_W:2 * GROUP_W] * (HEAD_DIM ** -0.5),
               uvg[:, 0:GROUP_W].astype(BF16), s0, lgf, lgb, masks, ret_ref)
    o = ret_ref[...]
    o = o * lax.rsqrt(_head_mean_square(o) + EPS) * dn_ref[lrow, :] * _silu(uvg[:, GROUP_W:2 * GROUP_W])
    y = y + _dot(o.astype(BF16), wout_ref[GROUP_W:2 * GROUP_W, :].astype(BF16))
    xn_ref[...] = xn_in_ref[...] + g2_ref[mrow, :] * y


def _lat_mixers(x, mod, layer, caches, state_c, state_d, rope, norm2_g, w_in, w_out,
                a_qn, a_kn, a_sink, b_qn, b_kn, c_conv_w, c_conv_b, c_wa, c_ba, c_wx, c_bx,
                c_lambda, d_theta, d_norm_g):
    rows = pl.BlockSpec((DEC_SEQ, D_MODEL), lambda b: (LAT_BLOCK0 + b, 0))
    h_rows = pl.BlockSpec((DEC_SEQ, D_MODEL), lambda b: (b, 0))
    cache_spec = pl.BlockSpec((None, None, PAST_LEN, KV_W), lambda b: (b, layer, 0, 0))
    gain = _full((DEPTH, HEAD_DIM))
    table = _once((DEC_SEQ, LANES), lambda b: (0, 0))
    out_shape = jax.ShapeDtypeStruct((N_ROWS, D_MODEL), F32)
    win_cols = lambda w, c: _once((None, D_MODEL, w), lambda b: (layer, 0, c))
    wout_rows = lambda h, r: _once((None, h, D_MODEL), lambda b: (layer, r, 0))

    xn, h = pl.pallas_call(
        functools.partial(_lat_attn_kernel, layer=layer),
        grid=(DEC_BATCH,),
        in_specs=[rows, _full((DEPTH, D_MODEL)),
                  _mod_chunk(layer, 3), _mod_chunk(layer, 4), _mod_chunk(layer, 5),
                  win_cols(4 * GROUP_W, 0), wout_rows(2 * GROUP_W, 0),
                  cache_spec, cache_spec, cache_spec, cache_spec,
                  gain, gain, gain, gain,
                  pl.BlockSpec(memory_space=pltpu.SMEM),
                  table, table, table],
        out_specs=[rows, pl.BlockSpec((DEC_SEQ, D_MODEL), lambda b: (b, 0), pipeline_mode=pl.Buffered(1))],
        out_shape=[out_shape, jax.ShapeDtypeStruct((N_LAT_ROWS, D_MODEL), BF16)],
        input_output_aliases={0: 0},
        scratch_shapes=[pltpu.VMEM((DEC_SEQ, 4 * GROUP_W), F32), pltpu.VMEM((DEC_SEQ, 2 * GROUP_W), BF16)],
        compiler_params=_cparams("arbitrary"),
        name="lat_attention",
    )(x, norm2_g, mod, mod, mod, w_in, w_out, *caches, a_qn, a_kn, b_qn, b_kn, a_sink, *rope)

    scr = pltpu.VMEM((DEC_SEQ, GROUP_W), F32)
    xn = pl.pallas_call(
        functools.partial(_lat_recurrent_kernel, layer=layer),
        grid=(DEC_BATCH,),
        in_specs=[
            rows, h_rows, _mod_chunk(layer, 5),
            win_cols(2 * GROUP_W, COL_CX // (2 * GROUP_W)),
            win_cols(2 * GROUP_W, COL_DQ // (2 * GROUP_W)), win_cols(2 * GROUP_W, COL_DV // (2 * GROUP_W)),
            wout_rows(2 * GROUP_W, 1),
            pl.BlockSpec((None, None, 2, GROUP_W), lambda b: (b, layer, 0, 0)),
            _layer_block((4, GROUP_W), layer), _full((DEPTH, GROUP_W)),
            _layer_block((2, N_HEADS, HEAD_DIM, HEAD_DIM), layer), _layer_block((2, GROUP_W), layer),
            _layer_block((2, N_HEADS, HEAD_DIM, HEAD_DIM), layer), _layer_block((2, GROUP_W), layer),
            _layer_block((2, GROUP_W), layer),
            pl.BlockSpec((None, None, 2, N_HEADS, HEAD_DIM, HEAD_DIM), lambda b: (b, layer, 0, 0, 0, 0)),
            _layer_block((2, N_HEADS), layer), _full((DEPTH, GROUP_W))],
        out_specs=rows,
        out_shape=out_shape,
        input_output_aliases={0: 0},
        scratch_shapes=[pltpu.VMEM((4, GROUP_W, GROUP_W), BF16)] + [scr] * 8,
        compiler_params=_cparams("arbitrary"),
        name="lat_recurrent",
    )(xn, h, mod, w_in, w_in, w_in, w_out, state_c, c_conv_w, c_conv_b, c_wa, c_ba, c_wx, c_bx, c_lambda,
      state_d, d_theta, d_norm_g)
    return xn


def _rope_tables():
    t = np.arange(DEC_SEQ)
    row = (t // GRID_W).astype(np.float64)[:, None]
    col = (t % GRID_W).astype(np.float64)[:, None]
    half = HEAD_DIM // 2
    inv = 1.0 / (ROPE_BASE ** (np.arange(0, half, 2, dtype=np.float64) / half))
    j = np.arange(LANES) % HEAD_DIM
    ang = np.where((j < half)[None, :], row, col) * inv[j % (half // 2)][None, :]
    first = ((j % half) < half // 2)[None, :]
    cos, sin = np.cos(ang), np.sin(ang)
    return tuple(jnp.asarray(a, F32) for a in (cos, np.where(first, -sin, 0.0), np.where(first, 0.0, sin)))


def kernel(x_prompt, x_sample, cache_a_k, cache_a_v, cache_b_k, cache_b_v, state_c, state_d, c, c_ctx, norm1_g, norm2_g, norm3_g, w_mod, b_mod, ffn1_wg, ffn1_wu, ffn1_wd, ffn2_wg, ffn2_wu, ffn2_wd, w_in, w_out, a_qn, a_kn, a_sink, b_qn, b_kn, c_conv_w, c_conv_b, c_wa, c_ba, c_wx, c_bx, c_lambda, d_theta, d_norm_g):
    mod = _modulation(c_ctx, c, w_mod, b_mod)
    rope = _rope_tables()
    caches = tuple(t.reshape(DEC_BATCH, DEPTH, PAST_LEN, KV_W) for t in (cache_a_k, cache_a_v, cache_b_k, cache_b_v))
    mixer_params = (a_qn, a_kn, a_sink, b_qn, b_kn, c_conv_w, c_conv_b, c_wa, c_ba, c_wx, c_bx,
                    c_lambda, d_theta, d_norm_g)
    xs = (x_prompt.reshape(N_CTX_ROWS, D_MODEL), x_sample.reshape(N_LAT_ROWS, D_MODEL))
    states = ()
    for l in range(DEPTH):
        (x,) = _ffn(xs, mod, l, 0, norm1_g, ffn1_wg, ffn1_wu, ffn1_wd)
        x, states = _ctx_mixers(x, mod, l, states, norm2_g, w_in, w_out, *mixer_params)
        x = _lat_mixers(x, mod, l, caches, state_c, state_d, rope, norm2_g, w_in, w_out, *mixer_params)
        xs = _ffn((x,), mod, l, 6, norm3_g, ffn2_wg, ffn2_wu, ffn2_wd, split_out=(l == DEPTH - 1))
    y_p, y_s = xs
    ka, va, kb, vb, st_c, st_d = states
    kv_shape = (BATCH, DEPTH, SEQ, 2, HEAD_DIM)
    return (y_p.reshape(BATCH, SEQ, D_MODEL), y_s.reshape(DEC_BATCH, DEC_SEQ, D_MODEL),
            ka.reshape(kv_shape), va.reshape(kv_shape), kb.reshape(kv_shape), vb.reshape(kv_shape),
            st_c, st_d)
```

```python
import functools
import math

import numpy as np
import jax
import jax.numpy as jnp
from jax import lax
from jax.experimental import pallas as pl
from jax.experimental.pallas import tpu as pltpu

F32 = jnp.float32
BF16 = jnp.bfloat16

D_MODEL = 1024
BATCH = 16
SEQ = 256
DEPTH = 2
DEC_BATCH = 2
DEC_SEQ = 1024
PAST_LEN = 512
GRID_W = 64
HEAD_DIM = 64
HEAD_SHIFT = 6
N_HEADS = 4
GROUP_W = 256
KV_W = 2 * HEAD_DIM
LANES = 128
WINDOW = 128
ATT_BLOCK = 128
ROPE_BASE = 10000.0
LRU_C = 8.0
D_FF = 2816
N_MOD = 9
EPS = 1e-6
NEG_INF = -1e30
IN_WIDTH = 2560

N_CTX_ROWS = BATCH * SEQ
N_LAT_ROWS = DEC_BATCH * DEC_SEQ
N_ROWS = N_CTX_ROWS + N_LAT_ROWS
MOD_ROWS = 8
MOD_GROUP = 1024

VMEM_LIMIT_BYTES = 56 * 1024 * 1024

COL_AQ, COL_AK, COL_AV = 0, 256, 384
COL_BQ, COL_BK, COL_BV = 512, 768, 896
COL_CX, COL_CY = 1024, 1280
COL_DQ, COL_DK, COL_DV, COL_DG = 1536, 1792, 2048, 2304


def _cparams(*sem):
    return pltpu.CompilerParams(dimension_semantics=sem, vmem_limit_bytes=VMEM_LIMIT_BYTES)


def _dot(a, b):
    return jnp.dot(a, b, preferred_element_type=F32)


def _dot_nt(a, b):
    return lax.dot_general(a, b, (((1,), (1,)), ((), ())), preferred_element_type=F32)


def _dot_tn(a, b):
    return lax.dot_general(a, b, (((0,), (0,)), ((), ())), preferred_element_type=F32)


def _sigmoid(x):
    return 0.5 * jnp.tanh(0.5 * x) + 0.5


def _silu(x):
    return x * _sigmoid(x)


def _gelu_tanh(x):
    return 0.5 * x * (1.0 + jnp.tanh(math.sqrt(2.0 / math.pi) * (x + 0.044715 * (x * x * x))))


def _mod_row(i, tm, s):
    if tm >= MOD_GROUP:
        block_index = i * (tm // MOD_GROUP) + s
    else:
        block_index = i >> int(math.log2(MOD_GROUP // tm))
    return jnp.maximum(block_index - (N_CTX_ROWS // MOD_GROUP - 1), 0)


def _norm_mod(x, g, sc, sh):
    ms = jnp.mean(x * x, axis=-1, keepdims=True)
    return (x * lax.rsqrt(ms + EPS) * g) * (1.0 + sc) + sh


def _full(shape):
    return pl.BlockSpec(shape, lambda *_: (0,) * len(shape))


def _layer_block(shape, layer):
    return pl.BlockSpec((None,) + shape, lambda *_: (layer,) + (0,) * len(shape))


MOD_TN = 3072


def _mod_kernel(cc_ref, c_ref, w_ref, b_ref, o_ref):
    l = pl.program_id(0)
    pad = jnp.zeros((MOD_ROWS - 1 - DEC_BATCH, D_MODEL), F32)
    cond = jnp.concatenate([cc_ref[...], c_ref[...], pad], axis=0)
    o_ref[...] = _dot(_silu(cond).astype(BF16), w_ref[...].astype(BF16)) + b_ref[pl.ds(l, 1), :]


def _modulation(c_ctx, c, w_mod, b_mod):
    n = N_MOD * D_MODEL
    return pl.pallas_call(
        _mod_kernel,
        grid=(DEPTH, n // MOD_TN),
        in_specs=[
            pl.BlockSpec((1, D_MODEL), lambda l, j: (0, 0)),
            pl.BlockSpec((DEC_BATCH, D_MODEL), lambda l, j: (0, 0)),
            pl.BlockSpec((None, D_MODEL, MOD_TN), lambda l, j: (l, 0, j)),
            pl.BlockSpec((DEPTH, MOD_TN), lambda l, j: (0, j)),
        ],
        out_specs=pl.BlockSpec((None, MOD_ROWS, MOD_TN), lambda l, j: (l, 0, j)),
        out_shape=jax.ShapeDtypeStruct((DEPTH, MOD_ROWS, n), F32),
        compiler_params=_cparams("arbitrary", "arbitrary"),
        name="modulation",
    )(c_ctx.reshape(1, D_MODEL), c, w_mod, b_mod)


FFN_TM = 1024
FFN_TF = 256
N_CTX_TILES = N_CTX_ROWS // FFN_TM


FFN_NJ = D_FF // FFN_TF
N_FFN_TILES = N_ROWS // FFN_TM
N_FFN_STEPS = FFN_NJ + N_FFN_TILES


def _ffn_tile(step):
    return jnp.maximum(step - FFN_NJ, 0)


def _on_stream_part(tile, x_refs, o_refs, fn):
    if len(x_refs) == 1 and len(o_refs) == 1:
        fn(x_refs[0], o_refs[0])
    else:
        pl.when(tile < N_CTX_TILES)(lambda: fn(x_refs[0], o_refs[0]))
        pl.when(tile >= N_CTX_TILES)(lambda: fn(x_refs[-1], o_refs[-1]))


def _ffn_kernel(*refs, layer, n_in, n_out):
    x_refs = refs[:n_in]
    n_ref, sh_ref, sc_ref, g_ref, wg_ref, wu_ref, wd_ref = refs[n_in:n_in + 7]
    o_refs = refs[n_in + 7:n_in + 7 + n_out]
    h_ref, a_ref, wg_s, wu_s, wd_s = refs[n_in + 7 + n_out:]
    nj, tf = FFN_NJ, FFN_TF
    s = pl.program_id(0)
    tile = _ffn_tile(s)
    r = _mod_row(tile, FFN_TM, 0)

    def load_tile():
        x = x_refs[0][...] if len(x_refs) == 1 else jnp.where(tile < N_CTX_TILES, x_refs[0][...], x_refs[1][...])
        h = _norm_mod(x, n_ref[layer:layer + 1, :], sc_ref[pl.ds(r, 1), :], sh_ref[pl.ds(r, 1), :])
        h_ref[...] = h.astype(BF16)

    def up_chunk(j, cols):
        h = h_ref[...]
        a_ref[:, cols] = (_silu(_dot(h, wg_s[j])) * _dot(h, wu_s[j])).astype(BF16)

    def down_and_store():
        y = (0.5 * g_ref[pl.ds(r, 1), :]) * _dot(a_ref[...], wd_s[...])

        def store(x_ref, o_ref):
            o_ref[...] = x_ref[...] + y
        _on_stream_part(tile, x_refs, o_refs, store)

    def keep_arrived_chunk():
        wg_s[s] = wg_ref[...].astype(BF16)
        wu_s[s] = wu_ref[...].astype(BF16)
        wd_s[pl.ds(pl.multiple_of(s * tf, tf), tf), :] = wd_ref[...].astype(BF16)

    def up_previous_chunk():
        up_chunk(s - 1, pl.ds(pl.multiple_of((s - 1) * tf, tf), tf))

    @pl.when(s == 0)
    def _():
        load_tile()
        keep_arrived_chunk()

    @pl.when((s > 0) & (s < nj))
    def _():
        up_previous_chunk()
        keep_arrived_chunk()

    @pl.when(s == nj)
    def _():
        up_previous_chunk()
        down_and_store()

    @pl.when(s > nj)
    def _():
        load_tile()
        for j in range(nj):
            up_chunk(j, slice(j * tf, (j + 1) * tf))
        down_and_store()


def _stream_specs(split, buffered_once):
    tm = FFN_TM
    kw = {"pipeline_mode": pl.Buffered(1)} if buffered_once else {}
    if not split:
        return [pl.BlockSpec((tm, D_MODEL), lambda s: (_ffn_tile(s), 0), **kw)]
    last_ctx = N_CTX_TILES - 1
    return [pl.BlockSpec((tm, D_MODEL), lambda s: (jnp.minimum(_ffn_tile(s), last_ctx), 0), **kw),
            pl.BlockSpec((tm, D_MODEL), lambda s: (jnp.maximum(_ffn_tile(s) - N_CTX_TILES, 0), 0), **kw)]


def _ffn(xs, mod, layer, chunk0, norm_g, wg, wu, wd, split_out=False):
    tm, tf, nj = FFN_TM, FFN_TF, FFN_NJ
    split_in = len(xs) == 2
    mod_spec = lambda c: pl.BlockSpec((None, MOD_ROWS, D_MODEL), lambda s: (layer, 0, c))
    w_col = lambda s: (layer, 0, jnp.minimum(s, nj - 1))
    w_row = lambda s: (layer, jnp.minimum(s, nj - 1), 0)
    if split_out:
        out_shape = [jax.ShapeDtypeStruct((N_CTX_ROWS, D_MODEL), F32),
                     jax.ShapeDtypeStruct((N_LAT_ROWS, D_MODEL), F32)]
    else:
        out_shape = [jax.ShapeDtypeStruct((N_ROWS, D_MODEL), F32)]
    out = pl.pallas_call(
        functools.partial(_ffn_kernel, layer=layer, n_in=len(xs), n_out=len(out_shape)),
        grid=(N_FFN_STEPS,),
        in_specs=_stream_specs(split_in, False) + [
            _full((DEPTH, D_MODEL)),
            mod_spec(chunk0), mod_spec(chunk0 + 1), mod_spec(chunk0 + 2),
            pl.BlockSpec((None, D_MODEL, tf), w_col),
            pl.BlockSpec((None, D_MODEL, tf), w_col),
            pl.BlockSpec((None, tf, D_MODEL), w_row),
        ],
        out_specs=_stream_specs(split_out, True),
        out_shape=out_shape,
        scratch_shapes=[pltpu.VMEM((tm, D_MODEL), BF16),
                        pltpu.VMEM((tm, D_FF), BF16),
                        pltpu.VMEM((nj, D_MODEL, tf), BF16),
                        pltpu.VMEM((nj, D_MODEL, tf), BF16),
                        pltpu.VMEM((D_FF, D_MODEL), BF16)],
        compiler_params=_cparams("arbitrary"),
        name="ffn",
    )(*xs, norm_g, mod, mod, mod, wg, wu, wd)
    return tuple(out)


def _once(shape, index_map):
    return pl.BlockSpec(shape, index_map, pipeline_mode=pl.Buffered(1))


def _mod_chunk(layer, c):
    return pl.BlockSpec((None, MOD_ROWS, D_MODEL), lambda *_: (layer, 0, c))


def _head_mean_square(x):
    n = x.shape[-1]
    r = lax.broadcasted_iota(jnp.int32, (n, n), 0) >> HEAD_SHIFT
    c = lax.broadcasted_iota(jnp.int32, (n, n), 1) >> HEAD_SHIFT
    ones_bd = jnp.where(r == c, 1.0, 0.0).astype(BF16)
    return _dot((x * x).astype(BF16), ones_bd) * (1.0 / HEAD_DIM)


def _head_norm(x, head_gain):
    gain_row = jnp.concatenate([head_gain] * (x.shape[-1] // HEAD_DIM), axis=-1)
    return x * lax.rsqrt(_head_mean_square(x) + EPS) * gain_row


def _head_cols(x, h):
    return x[:, h * HEAD_DIM:(h + 1) * HEAD_DIM].astype(BF16)


def _softmax_pv(scores, values, sink):
    m = jnp.max(scores[0], axis=-1, keepdims=True)
    for s in scores[1:]:
        m = jnp.maximum(m, jnp.max(s, axis=-1, keepdims=True))
    if sink is not None:
        m = jnp.maximum(m, sink)
    denom = None
    acc = None
    for s, v in zip(scores, values):
        p = jnp.exp(s - m)
        d = jnp.sum(p, axis=-1, keepdims=True)
        o = _dot(p.astype(BF16), v)
        denom = d if denom is None else denom + d
        acc = o if acc is None else acc + o
    if sink is not None:
        denom = denom + jnp.exp(sink - m)
    return acc / denom


def _rope(x, cos, sin_lo, sin_hi):
    cols = []
    for c in range(x.shape[-1] // LANES):
        xc = x[:, c * LANES:(c + 1) * LANES]
        cols.append(xc * cos + pltpu.roll(xc, 112, 1) * sin_lo + pltpu.roll(xc, 16, 1) * sin_hi)
    return cols[0] if len(cols) == 1 else jnp.concatenate(cols, axis=-1)


def _block_diag(blocks):
    n = len(blocks)
    w = blocks[0].shape[0]
    rows = []
    for k, blk in enumerate(blocks):
        parts = []
        if k > 0:
            parts.append(jnp.zeros((w, k * w), F32))
        parts.append(blk)
        if k < n - 1:
            parts.append(jnp.zeros((w, (n - 1 - k) * w), F32))
        rows.append(jnp.concatenate(parts, axis=-1))
    return jnp.concatenate(rows, axis=0)


def _rglru_gates(xc, wa, ba, wx, bx, lam):
    xb = xc.astype(BF16)
    r = _sigmoid(_dot(xb, wa) + ba)
    i = _sigmoid(_dot(xb, wx) + bx)
    softplus = jnp.maximum(-lam, 0.0) + jnp.log1p(jnp.exp(-jnp.abs(lam)))
    log_a = (-LRU_C) * r * softplus
    a = jnp.exp(log_a)
    b = jnp.sqrt(1.0 - a * a) * (i * xc)
    return a, b


def _block_prefix(a, b, reverse):
    t = a.shape[0]
    row = lax.broadcasted_iota(jnp.int32, a.shape, 0) & 7
    for d in (1, 2, 4):
        if reverse:
            a_s = pltpu.roll(a, t - d, 0)
            b_s = pltpu.roll(b, t - d, 0)
            ok = row < 8 - d
        else:
            a_s = pltpu.roll(a, d, 0)
            b_s = pltpu.roll(b, d, 0)
            ok = row >= d
        b = jnp.where(ok, a * b_s + b, b)
        a = jnp.where(ok, a * a_s, a)
    return a, b


def _conv4(x, w_ref, b_row):
    t = x.shape[0]
    row = lax.broadcasted_iota(jnp.int32, x.shape, 0)
    xm2 = jnp.where(row >= 2, pltpu.roll(x, 2, 0), 0.0)
    xm1 = jnp.where(row >= 1, pltpu.roll(x, 1, 0), 0.0)
    xp1 = jnp.where(row < t - 1, pltpu.roll(x, t - 1, 0), 0.0)
    return (xm2 * w_ref[0:1, :] + xm1 * w_ref[1:2, :] + x * w_ref[2:3, :] + xp1 * w_ref[3:4, :]) + b_row


def _rglru_prepare(cx, cy, conv_w_ref, conv_b, gate_w_ref, ba_ref, bx_ref, lam_ref,
                   af_ref, bf_ref, ab_ref, bb_ref, gel_ref):
    xc = _conv4(cx, conv_w_ref, conv_b)
    a, b = _rglru_gates(xc, gate_w_ref[0], ba_ref[0:1, :], gate_w_ref[1], bx_ref[0:1, :], lam_ref[0:1, :])
    a, b = _block_prefix(a, b, reverse=False)
    af_ref[...] = a
    bf_ref[...] = b
    a, b = _rglru_gates(xc, gate_w_ref[2], ba_ref[1:2, :], gate_w_ref[3], bx_ref[1:2, :], lam_ref[1:2, :])
    a, b = _block_prefix(a, b, reverse=True)
    ab_ref[...] = a
    bb_ref[...] = b
    gel_ref[...] = _gelu_tanh(cy)


SCAN_UNROLL = 8


def _rglru_finish(h0f, h0b, af_ref, bf_ref, ab_ref, bb_ref, hf_ref, hb_ref, gel_ref):
    nblk = af_ref.shape[0] // 8

    def body(k, carry):
        cf, cb = carry
        rf = pl.ds(pl.multiple_of(k * 8, 8), 8)
        hf = bf_ref[rf, :] + af_ref[rf, :] * cf
        hf_ref[rf, :] = hf
        rb = pl.ds(pl.multiple_of((nblk - 1 - k) * 8, 8), 8)
        hb = bb_ref[rb, :] + ab_ref[rb, :] * cb
        hb_ref[rb, :] = hb
        return hf[7:8, :], hb[0:1, :]

    cf, cb = lax.fori_loop(0, nblk, body, (h0f, h0b), unroll=SCAN_UNROLL)
    oc = (hf_ref[...] + hb_ref[...]) * gel_ref[...]
    return oc, cf, cb


def _store_gate_weights(gate_w_ref, wa_ref, wx_ref):
    for d in range(2):
        gate_w_ref[2 * d] = _block_diag([wa_ref[d, n] for n in range(N_HEADS)]).astype(BF16)
        gate_w_ref[2 * d + 1] = _block_diag([wx_ref[d, n] for n in range(N_HEADS)]).astype(BF16)


def _lane_head_masks(n):
    lane = lax.broadcasted_iota(jnp.int32, (1, n), 1) >> HEAD_SHIFT
    return [jnp.where(lane == h, 1.0, 0.0) for h in range(n // HEAD_DIM)]


def _log_decays(theta_ref, masks):
    theta = theta_ref[...]
    lanes = theta[:, 0:1] * masks[0]
    for h in range(1, N_HEADS):
        lanes = lanes + theta[:, h:h + 1] * masks[h]
    lg = jnp.log1p(-jnp.exp(lanes))
    return lg[0:1, :], lg[1:2, :]


RET_BLOCK = 256


def _retention(q, k8, vb, s0, lgf, lgb, masks, o_ref):
    t, w = q.shape
    c = RET_BLOCK
    nh = w // HEAD_DIM
    pos = lax.broadcasted_iota(jnp.int32, (c, w), 0).astype(F32)
    q_dec = (jnp.exp(lgf * (pos + 1.0)), jnp.exp(lgb * (float(c) - pos)))
    k_dec = (jnp.exp(lgf * (float(c - 1) - pos)), jnp.exp(lgb * pos))
    chunk_dec = (jnp.exp(lgf * float(c)), jnp.exp(lgb * float(c)))
    rel = (lax.broadcasted_iota(jnp.int32, (c, c), 0) - lax.broadcasted_iota(jnp.int32, (c, c), 1)).astype(F32)
    decs = []
    for h in range(nh):
        gf = lgf[:, h * HEAD_DIM:h * HEAD_DIM + 1]
        gb = lgb[:, h * HEAD_DIM:h * HEAD_DIM + 1]
        e = jnp.exp(jnp.where(rel >= 0, gf * rel, gb * (-rel)))
        decs.append(jnp.where(rel == 0, 2.0, e))
    dec = jnp.concatenate(decs, axis=0)
    r_head = lax.broadcasted_iota(jnp.int32, (w, w), 0) >> HEAD_SHIFT
    c_head = lax.broadcasted_iota(jnp.int32, (w, w), 1) >> HEAD_SHIFT
    same_head = jnp.where(r_head == c_head, 1.0, 0.0)
    states = [None, None] if s0 is None else list(s0)

    def carry(d, rows, o):
        if states[d] is not None:
            o = o + _dot((q[rows, :] * q_dec[d]).astype(BF16), states[d].astype(BF16))
        upd = _dot_tn((k8[rows, :] * k_dec[d]).astype(BF16), vb[rows, :]) * same_head
        states[d] = upd if states[d] is None else states[d] * chunk_dec[d] + upd
        return o

    for ci in range(t // c):
        rows = slice(ci * c, (ci + 1) * c)
        qc = q[rows, :]
        q_stack = jnp.concatenate([(qc * masks[h]).astype(BF16) for h in range(nh)], axis=0)
        inner = (_dot_nt(q_stack, k8[rows, :].astype(BF16)) * dec).astype(BF16)
        out = _dot(inner, vb[rows, :])
        o = out[0:c, :] * masks[0]
        for h in range(1, nh):
            o = o + out[h * c:(h + 1) * c, :] * masks[h]
        o_ref[rows, :] = carry(0, rows, o)
    for ci in reversed(range(t // c)):
        rows = slice(ci * c, (ci + 1) * c)
        if states[1] is not None:
            o_ref[rows, :] = carry(1, rows, o_ref[rows, :])
        else:
            carry(1, rows, None)
    return states[0], states[1]


def _ctx_mixer_kernel(*refs, layer, n_prev):
    prev_refs = refs[:n_prev]
    (x_ref, n2_ref, sh_ref, sc_ref, g2_ref, win_ref, wout_ref,
     aqn_ref, akn_ref, bqn_ref, bkn_ref, sink_ref,
     convw_ref, convb_ref, wa_ref, ba_ref, wx_ref, bx_ref, lam_ref, theta_ref, dn_ref,
     xn_ref, *state_refs) = refs[n_prev:n_prev + 28]
    (win_s, wout_s, u_ref, mixed_ref,
     gate_w_ref, af_ref, bf_ref, ab_ref, bb_ref, hf_ref, hb_ref, gel_ref, ret_ref) = refs[n_prev + 28:]
    t = SEQ
    lrow = slice(layer, layer + 1)
    for prev_ref, state_ref in zip(prev_refs, state_refs):
        for earlier in range(layer):
            state_ref[earlier] = prev_ref[earlier]
    ka_ref, va_ref, kb_ref, vb_ref, stc_ref, std_ref = (ref.at[layer] for ref in state_refs)

    @pl.when(pl.program_id(0) == 0)
    def _():
        for c0 in range(0, IN_WIDTH, 2 * GROUP_W):
            win_s[:, c0:c0 + 2 * GROUP_W] = win_ref[:, c0:c0 + 2 * GROUP_W].astype(BF16)
        wout_s[...] = wout_ref[...].astype(BF16)
        _store_gate_weights(gate_w_ref, wa_ref, wx_ref)

    x = x_ref[...]
    h = _norm_mod(x, n2_ref[lrow, :], sc_ref[0:1, :], sh_ref[0:1, :]).astype(BF16)
    c_cols = slice(COL_CX, COL_CX + 2 * GROUP_W)
    u_ref[:, c_cols] = _dot(h, win_s[:, c_cols])
    _rglru_prepare(u_ref[:, COL_CX:COL_CX + GROUP_W], u_ref[:, COL_CY:COL_CY + GROUP_W],
                   convw_ref, convb_ref[lrow, :], gate_w_ref, ba_ref, bx_ref, lam_ref,
                   af_ref, bf_ref, ab_ref, bb_ref, gel_ref)
    u_ref[:, 0:COL_CX] = _dot(h, win_s[:, 0:COL_CX])
    u_ref[:, COL_DQ:IN_WIDTH] = _dot(h, win_s[:, COL_DQ:IN_WIDTH])

    groups = []
    for (cq, ck, cv, qn_ref, kn_ref, k_out, v_out, use_sink) in (
            (COL_AQ, COL_AK, COL_AV, aqn_ref, akn_ref, ka_ref, va_ref, True),
            (COL_BQ, COL_BK, COL_BV, bqn_ref, bkn_ref, kb_ref, vb_ref, False)):
        q = _head_norm(u_ref[:, cq:cq + GROUP_W], qn_ref[lrow, :])
        k = _head_norm(u_ref[:, ck:ck + KV_W], kn_ref[lrow, :])
        v = u_ref[:, cv:cv + KV_W]
        k_out[...] = k
        v_out[...] = v
        groups.append((q * (HEAD_DIM ** -0.5), k, v, use_sink, []))
    half = t // 2
    for hd in range(N_HEADS):
        kv = hd // 2
        parts = [[] for _ in groups]
        for r0 in (0, half):
            for part, (qs, k, v, use_sink, _) in zip(parts, groups):
                s = _dot_nt(_head_cols(qs[r0:r0 + half, :], hd), _head_cols(k, kv))
                sink = jnp.full((half, 1), sink_ref[layer, hd], F32) if use_sink else None
                part.append(_softmax_pv([s], [_head_cols(v, kv)], sink))
        for part, (_, _, _, _, heads) in zip(parts, groups):
            heads.append(jnp.concatenate(part, axis=0))
    for g, (_, _, _, _, heads) in enumerate(groups):
        mixed_ref[:, g * GROUP_W:(g + 1) * GROUP_W] = jnp.concatenate(heads, axis=-1).astype(BF16)

    zero = jnp.zeros((1, GROUP_W), F32)
    oc, cf, cb = _rglru_finish(zero, zero, af_ref, bf_ref, ab_ref, bb_ref, hf_ref, hb_ref, gel_ref)
    mixed_ref[:, 2 * GROUP_W:3 * GROUP_W] = oc.astype(BF16)
    stc_ref[0:1, :] = cf
    stc_ref[1:2, :] = cb

    masks = _lane_head_masks(GROUP_W)
    lgf, lgb = _log_decays(theta_ref, masks)
    k8 = u_ref[:, COL_DK:COL_DK + GROUP_W] * (HEAD_DIM ** -0.5)
    vb = u_ref[:, COL_DV:COL_DV + GROUP_W].astype(BF16)
    final_states = _retention(u_ref[:, COL_DQ:COL_DQ + GROUP_W], k8, vb, None, lgf, lgb, masks, ret_ref)
    o = ret_ref[...]
    o = o * lax.rsqrt(_head_mean_square(o) + EPS) * dn_ref[lrow, :] * _silu(u_ref[:, COL_DG:COL_DG + GROUP_W])
    mixed_ref[:, 3 * GROUP_W:4 * GROUP_W] = o.astype(BF16)
    for d, s_full in enumerate(final_states):
        for hd in range(N_HEADS):
            std_ref[d, hd] = s_full[hd * HEAD_DIM:(hd + 1) * HEAD_DIM, hd * HEAD_DIM:(hd + 1) * HEAD_DIM]

    xn_ref[...] = x + g2_ref[0:1, :] * _dot(mixed_ref[...], wout_s[...])


def _ctx_mixers(x, mod, layer, prev, norm2_g, w_in, w_out,
                a_qn, a_kn, a_sink, b_qn, b_kn, c_conv_w, c_conv_b, c_wa, c_ba, c_wx, c_bx,
                c_lambda, d_theta, d_norm_g):
    per_request = lambda slots, shape: pl.BlockSpec((None, slots) + shape, lambda b: (b,) + (0,) * (1 + len(shape)))
    state_dims = [(SEQ, KV_W)] * 4 + [(2, GROUP_W), (2, N_HEADS, HEAD_DIM, HEAD_DIM)]
    scr = pltpu.VMEM((SEQ, GROUP_W), F32)
    out = pl.pallas_call(
        functools.partial(_ctx_mixer_kernel, layer=layer, n_prev=len(prev)),
        grid=(BATCH,),
        in_specs=[per_request(layer, dims) for dims in state_dims[:len(prev)]] + [
            pl.BlockSpec((SEQ, D_MODEL), lambda b: (b, 0)),
            _full((DEPTH, D_MODEL)),
            _mod_chunk(layer, 3), _mod_chunk(layer, 4), _mod_chunk(layer, 5),
            _once((None, D_MODEL, IN_WIDTH), lambda b: (layer, 0, 0)),
            _once((None, D_MODEL, D_MODEL), lambda b: (layer, 0, 0)),
            _full((DEPTH, HEAD_DIM)), _full((DEPTH, HEAD_DIM)), _full((DEPTH, HEAD_DIM)), _full((DEPTH, HEAD_DIM)),
            pl.BlockSpec(memory_space=pltpu.SMEM),
            _layer_block((4, GROUP_W), layer), _full((DEPTH, GROUP_W)),
            _layer_block((2, N_HEADS, HEAD_DIM, HEAD_DIM), layer), _layer_block((2, GROUP_W), layer),
            _layer_block((2, N_HEADS, HEAD_DIM, HEAD_DIM), layer), _layer_block((2, GROUP_W), layer),
            _layer_block((2, GROUP_W), layer),
            _layer_block((2, N_HEADS), layer), _full((DEPTH, GROUP_W)),
        ],
        out_specs=[pl.BlockSpec((SEQ, D_MODEL), lambda b: (b, 0))] + [
            per_request(layer + 1, dims) for dims in state_dims],
        out_shape=[jax.ShapeDtypeStruct((N_ROWS, D_MODEL), F32)] + [
            jax.ShapeDtypeStruct((BATCH, layer + 1) + dims, F32) for dims in state_dims],
        input_output_aliases={len(prev): 0},
        scratch_shapes=[pltpu.VMEM((D_MODEL, IN_WIDTH), BF16), pltpu.VMEM((D_MODEL, D_MODEL), BF16),
                        pltpu.VMEM((SEQ, IN_WIDTH), F32), pltpu.VMEM((SEQ, D_MODEL), BF16),
                        pltpu.VMEM((4, GROUP_W, GROUP_W), BF16)] + [scr] * 8,
        compiler_params=_cparams("arbitrary"),
        name="ctx_mixers",
    )(*prev, x, norm2_g, mod, mod, mod, w_in, w_out,
      a_qn, a_kn, b_qn, b_kn, a_sink, c_conv_w, c_conv_b, c_wa, c_ba, c_wx, c_bx,
      c_lambda, d_theta, d_norm_g)
    return out[0], tuple(out[1:])


LAT_BLOCK0 = N_CTX_ROWS // DEC_SEQ


def _lat_attn_kernel(x_ref, n2_ref, sh_ref, sc_ref, g2_ref, win_ref, wout_ref,
                     kca_ref, vca_ref, kcb_ref, vcb_ref,
                     aqn_ref, akn_ref, bqn_ref, bkn_ref, sink_ref, cos_ref, sinl_ref, sinh_ref,
                     xn_ref, h_ref, u_ref, o_ref, *, layer):
    t = DEC_SEQ
    lrow = slice(layer, layer + 1)
    mrow = pl.ds(1 + pl.program_id(0), 1)
    cos, sin_lo, sin_hi = cos_ref[...], sinl_ref[...], sinh_ref[...]
    scale = HEAD_DIM ** -0.5
    x = x_ref[...]
    h_ref[...] = _norm_mod(x, n2_ref[lrow, :], sc_ref[mrow, :], sh_ref[mrow, :]).astype(BF16)
    u_ref[...] = _dot(h_ref[...], win_ref[...].astype(BF16))

    q = _rope(_head_norm(u_ref[:, COL_AQ:COL_AQ + GROUP_W], aqn_ref[lrow, :]), cos, sin_lo, sin_hi)
    k = _rope(_head_norm(u_ref[:, COL_AK:COL_AK + KV_W], akn_ref[lrow, :]), cos, sin_lo, sin_hi)
    qh = [_head_cols(q * scale, h) for h in range(4)]
    v = u_ref[:, COL_AV:COL_AV + KV_W]
    kh = [_head_cols(k, kv) for kv in range(2)]
    vh = [_head_cols(v, kv) for kv in range(2)]
    kch = [_head_cols(kca_ref[...], kv) for kv in range(2)]
    vch = [_head_cols(vca_ref[...], kv) for kv in range(2)]
    w = ATT_BLOCK
    span = 3 * w
    for n in range(t // w):
        start = min(max((n - 1) * w, 0), t - span)
        rows = slice(n * w, (n + 1) * w)
        band = slice(start, start + span)
        qpos = (lax.broadcasted_iota(jnp.int32, (2 * w, span), 0) & (w - 1)) + n * w
        kpos = lax.broadcasted_iota(jnp.int32, (2 * w, span), 1) + start
        valid = jnp.abs(qpos - kpos) <= WINDOW
        heads = []
        for kv in range(2):
            qp = jnp.concatenate([qh[2 * kv][rows, :], qh[2 * kv + 1][rows, :]], axis=0)
            s_ctx = _dot_nt(qp, kch[kv])
            s_band = jnp.where(valid, _dot_nt(qp, kh[kv][band, :]), NEG_INF)
            row = lax.broadcasted_iota(jnp.int32, (2 * w, 1), 0)
            sink = jnp.where(row < w, sink_ref[layer, 2 * kv], sink_ref[layer, 2 * kv + 1])
            o = _softmax_pv([s_ctx, s_band], [vch[kv], vh[kv][band, :]], sink)
            heads += [o[0:w, :], o[w:2 * w, :]]
        o_ref[rows, 0:GROUP_W] = jnp.concatenate(heads, axis=-1).astype(BF16)

    q = _rope(_head_norm(u_ref[:, COL_BQ:COL_BQ + GROUP_W], bqn_ref[lrow, :]), cos, sin_lo, sin_hi)
    k = _rope(_head_norm(u_ref[:, COL_BK:COL_BK + KV_W], bkn_ref[lrow, :]), cos, sin_lo, sin_hi)
    qh = [_head_cols(q * scale, h) for h in range(4)]
    v = u_ref[:, COL_BV:COL_BV + KV_W]
    kh = [_head_cols(k, kv) for kv in range(2)]
    vh = [_head_cols(v, kv) for kv in range(2)]
    kch = [_head_cols(kcb_ref[...], kv) for kv in range(2)]
    vch = [_head_cols(vcb_ref[...], kv) for kv in range(2)]
    tq = 2 * ATT_BLOCK
    for n in range(t // tq):
        rows = slice(n * tq, (n + 1) * tq)
        heads = []
        for kv in range(2):
            qp = jnp.concatenate([qh[2 * kv][rows, :], qh[2 * kv + 1][rows, :]], axis=0)
            o = _softmax_pv([_dot_nt(qp, kch[kv]), _dot_nt(qp, kh[kv])], [vch[kv], vh[kv]], None)
            heads += [o[0:tq, :], o[tq:2 * tq, :]]
        o_ref[rows, GROUP_W:2 * GROUP_W] = jnp.concatenate(heads, axis=-1).astype(BF16)

    xn_ref[...] = x + g2_ref[mrow, :] * _dot(o_ref[...], wout_ref[...].astype(BF16))


def _lat_recurrent_kernel(xn_in_ref, h_ref, g2_ref, wc_ref, wqk_ref, wvg_ref, wout_ref, h0_ref,
                          convw_ref, convb_ref, wa_ref, ba_ref, wx_ref, bx_ref, lam_ref,
                          s0_ref, theta_ref, dn_ref,
                          xn_ref, gate_w_ref, af_ref, bf_ref, ab_ref, bb_ref, hf_ref, hb_ref, gel_ref, ret_ref,
                          *, layer):
    lrow = slice(layer, layer + 1)
    mrow = pl.ds(1 + pl.program_id(0), 1)

    @pl.when(pl.program_id(0) == 0)
    def _():
        _store_gate_weights(gate_w_ref, wa_ref, wx_ref)

    h = h_ref[...]
    u = _dot(h, wc_ref[...].astype(BF16))
    _rglru_prepare(u[:, 0:GROUP_W], u[:, GROUP_W:2 * GROUP_W], convw_ref, convb_ref[lrow, :],
                   gate_w_ref, ba_ref, bx_ref, lam_ref, af_ref, bf_ref, ab_ref, bb_ref, gel_ref)
    oc, _, _ = _rglru_finish(h0_ref[0:1, :], h0_ref[1:2, :],
                             af_ref, bf_ref, ab_ref, bb_ref, hf_ref, hb_ref, gel_ref)
    y = _dot(oc.astype(BF16), wout_ref[0:GROUP_W, :].astype(BF16))

    uqk = _dot(h, wqk_ref[...].astype(BF16))
    uvg = _dot(h, wvg_ref[...].astype(BF16))
    masks = _lane_head_masks(GROUP_W)
    lgf, lgb = _log_decays(theta_ref, masks)
    s0 = tuple(_block_diag([s0_ref[d, hd] for hd in range(N_HEADS)]) for d in range(2))
    _retention(uqk[:, 0:GROUP_W], uqk[:, GROUP_W:2 * GROUP_W] * (HEAD_DIM ** -0.5),
               uvg[:, 0:GROUP_W].astype(BF16), s0, lgf, lgb, masks, ret_ref)
    o = ret_ref[...]
    o = o * lax.rsqrt(_head_mean_square(o) + EPS) * dn_ref[lrow, :] * _silu(uvg[:, GROUP_W:2 * GROUP_W])
    y = y + _dot(o.astype(BF16), wout_ref[GROUP_W:2 * GROUP_W, :].astype(BF16))
    xn_ref[...] = xn_in_ref[...] + g2_ref[mrow, :] * y


def _lat_mixers(x, mod, layer, caches, state_c, state_d, rope, norm2_g, w_in, w_out,
                a_qn, a_kn, a_sink, b_qn, b_kn, c_conv_w, c_conv_b, c_wa, c_ba, c_wx, c_bx,
                c_lambda, d_theta, d_norm_g):
    rows = pl.BlockSpec((DEC_SEQ, D_MODEL), lambda b: (LAT_BLOCK0 + b, 0))
    h_rows = pl.BlockSpec((DEC_SEQ, D_MODEL), lambda b: (b, 0))
    cache_spec = pl.BlockSpec((None, None, PAST_LEN, KV_W), lambda b: (b, layer, 0, 0))
    gain = _full((DEPTH, HEAD_DIM))
    table = _once((DEC_SEQ, LANES), lambda b: (0, 0))
    out_shape = jax.ShapeDtypeStruct((N_ROWS, D_MODEL), F32)
    win_cols = lambda w, c: _once((None, D_MODEL, w), lambda b: (layer, 0, c))
    wout_rows = lambda h, r: _once((None, h, D_MODEL), lambda b: (layer, r, 0))

    xn, h = pl.pallas_call(
        functools.partial(_lat_attn_kernel, layer=layer),
        grid=(DEC_BATCH,),
        in_specs=[rows, _full((DEPTH, D_MODEL)),
                  _mod_chunk(layer, 3), _mod_chunk(layer, 4), _mod_chunk(layer, 5),
                  win_cols(4 * GROUP_W, 0), wout_rows(2 * GROUP_W, 0),
                  cache_spec, cache_spec, cache_spec, cache_spec,
                  gain, gain, gain, gain,
                  pl.BlockSpec(memory_space=pltpu.SMEM),
                  table, table, table],
        out_specs=[rows, pl.BlockSpec((DEC_SEQ, D_MODEL), lambda b: (b, 0), pipeline_mode=pl.Buffered(1))],
        out_shape=[out_shape, jax.ShapeDtypeStruct((N_LAT_ROWS, D_MODEL), BF16)],
        input_output_aliases={0: 0},
        scratch_shapes=[pltpu.VMEM((DEC_SEQ, 4 * GROUP_W), F32), pltpu.VMEM((DEC_SEQ, 2 * GROUP_W), BF16)],
        compiler_params=_cparams("arbitrary"),
        name="lat_attention",
    )(x, norm2_g, mod, mod, mod, w_in, w_out, *caches, a_qn, a_kn, b_qn, b_kn, a_sink, *rope)

    scr = pltpu.VMEM((DEC_SEQ, GROUP_W), F32)
    xn = pl.pallas_call(
        functools.partial(_lat_recurrent_kernel, layer=layer),
        grid=(DEC_BATCH,),
        in_specs=[
            rows, h_rows, _mod_chunk(layer, 5),
            win_cols(2 * GROUP_W, COL_CX // (2 * GROUP_W)),
            win_cols(2 * GROUP_W, COL_DQ // (2 * GROUP_W)), win_cols(2 * GROUP_W, COL_DV // (2 * GROUP_W)),
            wout_rows(2 * GROUP_W, 1),
            pl.BlockSpec((None, None, 2, GROUP_W), lambda b: (b, layer, 0, 0)),
            _layer_block((4, GROUP_W), layer), _full((DEPTH, GROUP_W)),
            _layer_block((2, N_HEADS, HEAD_DIM, HEAD_DIM), layer), _layer_block((2, GROUP_W), layer),
            _layer_block((2, N_HEADS, HEAD_DIM, HEAD_DIM), layer), _layer_block((2, GROUP_W), layer),
            _layer_block((2, GROUP_W), layer),
            pl.BlockSpec((None, None, 2, N_HEADS, HEAD_DIM, HEAD_DIM), lambda b: (b, layer, 0, 0, 0, 0)),
            _layer_block((2, N_HEADS), layer), _full((DEPTH, GROUP_W))],
        out_specs=rows,
        out_shape=out_shape,
        input_output_aliases={0: 0},
        scratch_shapes=[pltpu.VMEM((4, GROUP_W, GROUP_W), BF16)] + [scr] * 8,
        compiler_params=_cparams("arbitrary"),
        name="lat_recurrent",
    )(xn, h, mod, w_in, w_in, w_in, w_out, state_c, c_conv_w, c_conv_b, c_wa, c_ba, c_wx, c_bx, c_lambda,
      state_d, d_theta, d_norm_g)
    return xn


def _rope_tables():
    t = np.arange(DEC_SEQ)
    row = (t // GRID_W).astype(np.float64)[:, None]
    col = (t % GRID_W).astype(np.float64)[:, None]
    half = HEAD_DIM // 2
    inv = 1.0 / (ROPE_BASE ** (np.arange(0, half, 2, dtype=np.float64) / half))
    j = np.arange(LANES) % HEAD_DIM
    ang = np.where((j < half)[None, :], row, col) * inv[j % (half // 2)][None, :]
    first = ((j % half) < half // 2)[None, :]
    cos, sin = np.cos(ang), np.sin(ang)
    return tuple(jnp.asarray(a, F32) for a in (cos, np.where(first, -sin, 0.0), np.where(first, 0.0, sin)))


def kernel(x_prompt, x_sample, cache_a_k, cache_a_v, cache_b_k, cache_b_v, state_c, state_d, c, c_ctx, norm1_g, norm2_g, norm3_g, w_mod, b_mod, ffn1_wg, ffn1_wu, ffn1_wd, ffn2_wg, ffn2_wu, ffn2_wd, w_in, w_out, a_qn, a_kn, a_sink, b_qn, b_kn, c_conv_w, c_conv_b, c_wa, c_ba, c_wx, c_bx, c_lambda, d_theta, d_norm_g):
    mod = _modulation(c_ctx, c, w_mod, b_mod)
    rope = _rope_tables()
    caches = tuple(t.reshape(DEC_BATCH, DEPTH, PAST_LEN, KV_W) for t in (cache_a_k, cache_a_v, cache_b_k, cache_b_v))
    mixer_params = (a_qn, a_kn, a_sink, b_qn, b_kn, c_conv_w, c_conv_b, c_wa, c_ba, c_wx, c_bx,
                    c_lambda, d_theta, d_norm_g)
    xs = (x_prompt.reshape(N_CTX_ROWS, D_MODEL), x_sample.reshape(N_LAT_ROWS, D_MODEL))
    states = ()
    for l in range(DEPTH):
        (x,) = _ffn(xs, mod, l, 0, norm1_g, ffn1_wg, ffn1_wu, ffn1_wd)
        x, states = _ctx_mixers(x, mod, l, states, norm2_g, w_in, w_out, *mixer_params)
        x = _lat_mixers(x, mod, l, caches, state_c, state_d, rope, norm2_g, w_in, w_out, *mixer_params)
        xs = _ffn((x,), mod, l, 6, norm3_g, ffn2_wg, ffn2_wu, ffn2_wd, split_out=(l == DEPTH - 1))
    y_p, y_s = xs
    ka, va, kb, vb, st_c, st_d = states
    kv_shape = (BATCH, DEPTH, SEQ, 2, HEAD_DIM)
    return (y_p.reshape(BATCH, SEQ, D_MODEL), y_s.reshape(DEC_BATCH, DEC_SEQ, D_MODEL),
            ka.reshape(kv_shape), va.reshape(kv_shape), kb.reshape(kv_shape), vb.reshape(kv_shape),
            st_c, st_d)
```

```python
import functools
import math

import numpy as np
import jax
import jax.numpy as jnp
from jax import lax
from jax.experimental import pallas as pl
from jax.experimental.pallas import tpu as pltpu

F32 = jnp.float32
BF16 = jnp.bfloat16

D_MODEL = 1024
BATCH = 16
SEQ = 256
DEPTH = 2
DEC_BATCH = 2
DEC_SEQ = 1024
PAST_LEN = 512
GRID_W = 64
HEAD_DIM = 64
HEAD_SHIFT = 6
N_HEADS = 4
GROUP_W = 256
KV_W = 2 * HEAD_DIM
LANES = 128
WINDOW = 128
ATT_BLOCK = 128
ROPE_BASE = 10000.0
LRU_C = 8.0
D_FF = 2816
N_MOD = 9
EPS = 1e-6
NEG_INF = -1e30
IN_WIDTH = 2560

N_CTX_ROWS = BATCH * SEQ
N_LAT_ROWS = DEC_BATCH * DEC_SEQ
N_ROWS = N_CTX_ROWS + N_LAT_ROWS
MOD_ROWS = 8
MOD_GROUP = 1024

VMEM_LIMIT_BYTES = 56 * 1024 * 1024

COL_AQ, COL_AK, COL_AV = 0, 256, 384
COL_BQ, COL_BK, COL_BV = 512, 768, 896
COL_CX, COL_CY = 1024, 1280
COL_DQ, COL_DK, COL_DV, COL_DG = 1536, 1792, 2048, 2304


def _cparams(*sem):
    return pltpu.CompilerParams(dimension_semantics=sem, vmem_limit_bytes=VMEM_LIMIT_BYTES)


def _dot(a, b):
    return jnp.dot(a, b, preferred_element_type=F32)


def _dot_nt(a, b):
    return lax.dot_general(a, b, (((1,), (1,)), ((), ())), preferred_element_type=F32)


def _dot_tn(a, b):
    return lax.dot_general(a, b, (((0,), (0,)), ((), ())), preferred_element_type=F32)


def _sigmoid(x):
    return 0.5 * jnp.tanh(0.5 * x) + 0.5


def _silu(x):
    return x * _sigmoid(x)


def _gelu_tanh(x):
    return 0.5 * x * (1.0 + jnp.tanh(math.sqrt(2.0 / math.pi) * (x + 0.044715 * (x * x * x))))


def _mod_row(i, tm, s):
    if tm >= MOD_GROUP:
        block_index = i * (tm // MOD_GROUP) + s
    else:
        block_index = i >> int(math.log2(MOD_GROUP // tm))
    return jnp.maximum(block_index - (N_CTX_ROWS // MOD_GROUP - 1), 0)


def _norm_mod(x, g, sc, sh):
    ms = jnp.mean(x * x, axis=-1, keepdims=True)
    return (x * lax.rsqrt(ms + EPS) * g) * (1.0 + sc) + sh


def _full(shape):
    return pl.BlockSpec(shape, lambda *_: (0,) * len(shape))


def _layer_block(shape, layer):
    return pl.BlockSpec((None,) + shape, lambda *_: (layer,) + (0,) * len(shape))


MOD_TN = 3072


def _mod_kernel(cc_ref, c_ref, w_ref, b_ref, o_ref):
    l = pl.program_id(0)
    pad = jnp.zeros((MOD_ROWS - 1 - DEC_BATCH, D_MODEL), F32)
    cond = jnp.concatenate([cc_ref[...], c_ref[...], pad], axis=0)
    o_ref[...] = _dot(_silu(cond).astype(BF16), w_ref[...].astype(BF16)) + b_ref[pl.ds(l, 1), :]


def _modulation(c_ctx, c, w_mod, b_mod):
    n = N_MOD * D_MODEL
    return pl.pallas_call(
        _mod_kernel,
        grid=(DEPTH, n // MOD_TN),
        in_specs=[
            pl.BlockSpec((1, D_MODEL), lambda l, j: (0, 0)),
            pl.BlockSpec((DEC_BATCH, D_MODEL), lambda l, j: (0, 0)),
            pl.BlockSpec((None, D_MODEL, MOD_TN), lambda l, j: (l, 0, j)),
            pl.BlockSpec((DEPTH, MOD_TN), lambda l, j: (0, j)),
        ],
        out_specs=pl.BlockSpec((None, MOD_ROWS, MOD_TN), lambda l, j: (l, 0, j)),
        out_shape=jax.ShapeDtypeStruct((DEPTH, MOD_ROWS, n), F32),
        compiler_params=_cparams("arbitrary", "arbitrary"),
        name="modulation",
    )(c_ctx.reshape(1, D_MODEL), c, w_mod, b_mod)


FFN_TM = 1024
FFN_TF = 256
N_CTX_TILES = N_CTX_ROWS // FFN_TM


FFN_NJ = D_FF // FFN_TF
N_FFN_TILES = N_ROWS // FFN_TM
N_FFN_STEPS = FFN_NJ + N_FFN_TILES


def _ffn_tile(step):
    return jnp.maximum(step - FFN_NJ, 0)


def _on_stream_part(tile, x_refs, o_refs, fn):
    if len(x_refs) == 1 and len(o_refs) == 1:
        fn(x_refs[0], o_refs[0])
    else:
        pl.when(tile < N_CTX_TILES)(lambda: fn(x_refs[0], o_refs[0]))
        pl.when(tile >= N_CTX_TILES)(lambda: fn(x_refs[-1], o_refs[-1]))


def _ffn_kernel(*refs, layer, n_in, n_out):
    x_refs = refs[:n_in]
    n_ref, sh_ref, sc_ref, g_ref, wg_ref, wu_ref, wd_ref = refs[n_in:n_in + 7]
    o_refs = refs[n_in + 7:n_in + 7 + n_out]
    h_ref, a_ref, wg_s, wu_s, wd_s = refs[n_in + 7 + n_out:]
    nj, tf = FFN_NJ, FFN_TF
    s = pl.program_id(0)
    tile = _ffn_tile(s)
    r = _mod_row(tile, FFN_TM, 0)

    def load_tile():
        x = x_refs[0][...] if len(x_refs) == 1 else jnp.where(tile < N_CTX_TILES, x_refs[0][...], x_refs[1][...])
        h = _norm_mod(x, n_ref[layer:layer + 1, :], sc_ref[pl.ds(r, 1), :], sh_ref[pl.ds(r, 1), :])
        h_ref[...] = h.astype(BF16)

    def up_chunk(j, cols):
        h = h_ref[...]
        a_ref[:, cols] = (_silu(_dot(h, wg_s[j])) * _dot(h, wu_s[j])).astype(BF16)

    def down_and_store():
        y = (0.5 * g_ref[pl.ds(r, 1), :]) * _dot(a_ref[...], wd_s[...])

        def store(x_ref, o_ref):
            o_ref[...] = x_ref[...] + y
        _on_stream_part(tile, x_refs, o_refs, store)

    def keep_arrived_chunk():
        wg_s[s] = wg_ref[...].astype(BF16)
        wu_s[s] = wu_ref[...].astype(BF16)
        wd_s[pl.ds(pl.multiple_of(s * tf, tf), tf), :] = wd_ref[...].astype(BF16)

    def up_previous_chunk():
        up_chunk(s - 1, pl.ds(pl.multiple_of((s - 1) * tf, tf), tf))

    @pl.when(s == 0)
    def _():
        load_tile()
        keep_arrived_chunk()

    @pl.when((s > 0) & (s < nj))
    def _():
        up_previous_chunk()
        keep_arrived_chunk()

    @pl.when(s == nj)
    def _():
        up_previous_chunk()
        down_and_store()

    @pl.when(s > nj)
    def _():
        load_tile()
        for j in range(nj):
            up_chunk(j, slice(j * tf, (j + 1) * tf))
        down_and_store()


def _stream_specs(split, buffered_once):
    tm = FFN_TM
    kw = {"pipeline_mode": pl.Buffered(1)} if buffered_once else {}
    if not split:
        return [pl.BlockSpec((tm, D_MODEL), lambda s: (_ffn_tile(s), 0), **kw)]
    last_ctx = N_CTX_TILES - 1
    return [pl.BlockSpec((tm, D_MODEL), lambda s: (jnp.minimum(_ffn_tile(s), last_ctx), 0), **kw),
            pl.BlockSpec((tm, D_MODEL), lambda s: (jnp.maximum(_ffn_tile(s) - N_CTX_TILES, 0), 0), **kw)]


def _ffn(xs, mod, layer, chunk0, norm_g, wg, wu, wd, split_out=False):
    tm, tf, nj = FFN_TM, FFN_TF, FFN_NJ
    split_in = len(xs) == 2
    mod_spec = lambda c: pl.BlockSpec((None, MOD_ROWS, D_MODEL), lambda s: (layer, 0, c))
    w_col = lambda s: (layer, 0, jnp.minimum(s, nj - 1))
    w_row = lambda s: (layer, jnp.minimum(s, nj - 1), 0)
    if split_out:
        out_shape = [jax.ShapeDtypeStruct((N_CTX_ROWS, D_MODEL), F32),
                     jax.ShapeDtypeStruct((N_LAT_ROWS, D_MODEL), F32)]
    else:
        out_shape = [jax.ShapeDtypeStruct((N_ROWS, D_MODEL), F32)]
    out = pl.pallas_call(
        functools.partial(_ffn_kernel, layer=layer, n_in=len(xs), n_out=len(out_shape)),
        grid=(N_FFN_STEPS,),
        in_specs=_stream_specs(split_in, False) + [
            _full((DEPTH, D_MODEL)),
            mod_spec(chunk0), mod_spec(chunk0 + 1), mod_spec(chunk0 + 2),
            pl.BlockSpec((None, D_MODEL, tf), w_col),
            pl.BlockSpec((None, D_MODEL, tf), w_col),
            pl.BlockSpec((None, tf, D_MODEL), w_row),
        ],
        out_specs=_stream_specs(split_out, True),
        out_shape=out_shape,
        scratch_shapes=[pltpu.VMEM((tm, D_MODEL), BF16),
                        pltpu.VMEM((tm, D_FF), BF16),
                        pltpu.VMEM((nj, D_MODEL, tf), BF16),
                        pltpu.VMEM((nj, D_MODEL, tf), BF16),
                        pltpu.VMEM((D_FF, D_MODEL), BF16)],
        compiler_params=_cparams("arbitrary"),
        name="ffn",
    )(*xs, norm_g, mod, mod, mod, wg, wu, wd)
    return tuple(out)


def _once(shape, index_map):
    return pl.BlockSpec(shape, index_map, pipeline_mode=pl.Buffered(1))


def _mod_chunk(layer, c):
    return pl.BlockSpec((None, MOD_ROWS, D_MODEL), lambda *_: (layer, 0, c))


def _head_mean_square(x):
    n = x.shape[-1]
    r = lax.broadcasted_iota(jnp.int32, (n, n), 0) >> HEAD_SHIFT
    c = lax.broadcasted_iota(jnp.int32, (n, n), 1) >> HEAD_SHIFT
    ones_bd = jnp.where(r == c, 1.0, 0.0).astype(BF16)
    return _dot((x * x).astype(BF16), ones_bd) * (1.0 / HEAD_DIM)


def _head_norm(x, head_gain):
    gain_row = jnp.concatenate([head_gain] * (x.shape[-1] // HEAD_DIM), axis=-1)
    return x * lax.rsqrt(_head_mean_square(x) + EPS) * gain_row


def _head_cols(x, h):
    return x[:, h * HEAD_DIM:(h + 1) * HEAD_DIM].astype(BF16)


def _softmax_pv(scores, values, sink):
    m = jnp.max(scores[0], axis=-1, keepdims=True)
    for s in scores[1:]:
        m = jnp.maximum(m, jnp.max(s, axis=-1, keepdims=True))
    if sink is not None:
        m = jnp.maximum(m, sink)
    denom = None
    acc = None
    for s, v in zip(scores, values):
        p = jnp.exp(s - m)
        d = jnp.sum(p, axis=-1, keepdims=True)
        o = _dot(p.astype(BF16), v)
        denom = d if denom is None else denom + d
        acc = o if acc is None else acc + o
    if sink is not None:
        denom = denom + jnp.exp(sink - m)
    return acc / denom


def _rope(x, cos, sin_lo, sin_hi):
    cols = []
    for c in range(x.shape[-1] // LANES):
        xc = x[:, c * LANES:(c + 1) * LANES]
        cols.append(xc * cos + pltpu.roll(xc, 112, 1) * sin_lo + pltpu.roll(xc, 16, 1) * sin_hi)
    return cols[0] if len(cols) == 1 else jnp.concatenate(cols, axis=-1)


def _block_diag(blocks):
    n = len(blocks)
    w = blocks[0].shape[0]
    rows = []
    for k, blk in enumerate(blocks):
        parts = []
        if k > 0:
            parts.append(jnp.zeros((w, k * w), F32))
        parts.append(blk)
        if k < n - 1:
            parts.append(jnp.zeros((w, (n - 1 - k) * w), F32))
        rows.append(jnp.concatenate(parts, axis=-1))
    return jnp.concatenate(rows, axis=0)


def _rglru_gates(xc, wa, ba, wx, bx, lam):
    xb = xc.astype(BF16)
    r = _sigmoid(_dot(xb, wa) + ba)
    i = _sigmoid(_dot(xb, wx) + bx)
    softplus = jnp.maximum(-lam, 0.0) + jnp.log1p(jnp.exp(-jnp.abs(lam)))
    log_a = (-LRU_C) * r * softplus
    a = jnp.exp(log_a)
    b = jnp.sqrt(1.0 - a * a) * (i * xc)
    return a, b


def _block_prefix(a, b, reverse):
    t = a.shape[0]
    row = lax.broadcasted_iota(jnp.int32, a.shape, 0) & 7
    for d in (1, 2, 4):
        if reverse:
            a_s = pltpu.roll(a, t - d, 0)
            b_s = pltpu.roll(b, t - d, 0)
            ok = row < 8 - d
        else:
            a_s = pltpu.roll(a, d, 0)
            b_s = pltpu.roll(b, d, 0)
            ok = row >= d
        b = jnp.where(ok, a * b_s + b, b)
        a = jnp.where(ok, a * a_s, a)
    return a, b


def _conv4(x, w_ref, b_row):
    t = x.shape[0]
    row = lax.broadcasted_iota(jnp.int32, x.shape, 0)
    xm2 = jnp.where(row >= 2, pltpu.roll(x, 2, 0), 0.0)
    xm1 = jnp.where(row >= 1, pltpu.roll(x, 1, 0), 0.0)
    xp1 = jnp.where(row < t - 1, pltpu.roll(x, t - 1, 0), 0.0)
    return (xm2 * w_ref[0:1, :] + xm1 * w_ref[1:2, :] + x * w_ref[2:3, :] + xp1 * w_ref[3:4, :]) + b_row


def _rglru_prepare(cx, cy, conv_w_ref, conv_b, gate_w_ref, ba_ref, bx_ref, lam_ref,
                   af_ref, bf_ref, ab_ref, bb_ref, gel_ref):
    xc = _conv4(cx, conv_w_ref, conv_b)
    a, b = _rglru_gates(xc, gate_w_ref[0], ba_ref[0:1, :], gate_w_ref[1], bx_ref[0:1, :], lam_ref[0:1, :])
    a, b = _block_prefix(a, b, reverse=False)
    af_ref[...] = a
    bf_ref[...] = b
    a, b = _rglru_gates(xc, gate_w_ref[2], ba_ref[1:2, :], gate_w_ref[3], bx_ref[1:2, :], lam_ref[1:2, :])
    a, b = _block_prefix(a, b, reverse=True)
    ab_ref[...] = a
    bb_ref[...] = b
    gel_ref[...] = _gelu_tanh(cy)


SCAN_UNROLL = 8


def _rglru_finish(h0f, h0b, af_ref, bf_ref, ab_ref, bb_ref, hf_ref, hb_ref, gel_ref):
    nblk = af_ref.shape[0] // 8

    def body(k, carry):
        cf, cb = carry
        rf = pl.ds(pl.multiple_of(k * 8, 8), 8)
        hf = bf_ref[rf, :] + af_ref[rf, :] * cf
        hf_ref[rf, :] = hf
        rb = pl.ds(pl.multiple_of((nblk - 1 - k) * 8, 8), 8)
        hb = bb_ref[rb, :] + ab_ref[rb, :] * cb
        hb_ref[rb, :] = hb
        return hf[7:8, :], hb[0:1, :]

    cf, cb = lax.fori_loop(0, nblk, body, (h0f, h0b), unroll=SCAN_UNROLL)
    oc = (hf_ref[...] + hb_ref[...]) * gel_ref[...]
    return oc, cf, cb


def _store_gate_weights(gate_w_ref, wa_ref, wx_ref):
    for d in range(2):
        gate_w_ref[2 * d] = _block_diag([wa_ref[d, n] for n in range(N_HEADS)]).astype(BF16)
        gate_w_ref[2 * d + 1] = _block_diag([wx_ref[d, n] for n in range(N_HEADS)]).astype(BF16)


def _lane_head_masks(n):
    lane = lax.broadcasted_iota(jnp.int32, (1, n), 1) >> HEAD_SHIFT
    return [jnp.where(lane == h, 1.0, 0.0) for h in range(n // HEAD_DIM)]


def _log_decays(theta_ref, masks):
    theta = theta_ref[...]
    lanes = theta[:, 0:1] * masks[0]
    for h in range(1, N_HEADS):
        lanes = lanes + theta[:, h:h + 1] * masks[h]
    lg = jnp.log1p(-jnp.exp(lanes))
    return lg[0:1, :], lg[1:2, :]


RET_BLOCK = 256


def _retention(q, k8, vb, s0, lgf, lgb, masks, o_ref):
    t, w = q.shape
    c = RET_BLOCK
    nh = w // HEAD_DIM
    pos = lax.broadcasted_iota(jnp.int32, (c, w), 0).astype(F32)
    q_dec = (jnp.exp(lgf * (pos + 1.0)), jnp.exp(lgb * (float(c) - pos)))
    k_dec = (jnp.exp(lgf * (float(c - 1) - pos)), jnp.exp(lgb * pos))
    chunk_dec = (jnp.exp(lgf * float(c)), jnp.exp(lgb * float(c)))
    rel = (lax.broadcasted_iota(jnp.int32, (c, c), 0) - lax.broadcasted_iota(jnp.int32, (c, c), 1)).astype(F32)
    decs = []
    for h in range(nh):
        gf = lgf[:, h * HEAD_DIM:h * HEAD_DIM + 1]
        gb = lgb[:, h * HEAD_DIM:h * HEAD_DIM + 1]
        e = jnp.exp(jnp.where(rel >= 0, gf * rel, gb * (-rel)))
        decs.append(jnp.where(rel == 0, 2.0, e))
    dec = jnp.concatenate(decs, axis=0)
    r_head = lax.broadcasted_iota(jnp.int32, (w, w), 0) >> HEAD_SHIFT
    c_head = lax.broadcasted_iota(jnp.int32, (w, w), 1) >> HEAD_SHIFT
    same_head = jnp.where(r_head == c_head, 1.0, 0.0)
    states = [None, None] if s0 is None else list(s0)

    def carry(d, rows, o):
        if states[d] is not None:
            o = o + _dot((q[rows, :] * q_dec[d]).astype(BF16), states[d].astype(BF16))
        upd = _dot_tn((k8[rows, :] * k_dec[d]).astype(BF16), vb[rows, :]) * same_head
        states[d] = upd if states[d] is None else states[d] * chunk_dec[d] + upd
        return o

    for ci in range(t // c):
        rows = slice(ci * c, (ci + 1) * c)
        qc = q[rows, :]
        q_stack = jnp.concatenate([(qc * masks[h]).astype(BF16) for h in range(nh)], axis=0)
        inner = (_dot_nt(q_stack, k8[rows, :].astype(BF16)) * dec).astype(BF16)
        out = _dot(inner, vb[rows, :])
        o = out[0:c, :] * masks[0]
        for h in range(1, nh):
            o = o + out[h * c:(h + 1) * c, :] * masks[h]
        o_ref[rows, :] = carry(0, rows, o)
    for ci in reversed(range(t // c)):
        rows = slice(ci * c, (ci + 1) * c)
        if states[1] is not None:
            o_ref[rows, :] = carry(1, rows, o_ref[rows, :])
        else:
            carry(1, rows, None)
    return states[0], states[1]


def _ctx_mixer_kernel(*refs, layer, n_prev):
    prev_refs = refs[:n_prev]
    (x_ref, n2_ref, sh_ref, sc_ref, g2_ref, win_ref, wout_ref,
     aqn_ref, akn_ref, bqn_ref, bkn_ref, sink_ref,
     convw_ref, convb_ref, wa_ref, ba_ref, wx_ref, bx_ref, lam_ref, theta_ref, dn_ref,
     xn_ref, *state_refs) = refs[n_prev:n_prev + 28]
    (win_s, wout_s, u_ref, mixed_ref,
     gate_w_ref, af_ref, bf_ref, ab_ref, bb_ref, hf_ref, hb_ref, gel_ref, ret_ref) = refs[n_prev + 28:]
    t = SEQ
    lrow = slice(layer, layer + 1)
    for prev_ref, state_ref in zip(prev_refs, state_refs):
        for earlier in range(layer):
            state_ref[earlier] = prev_ref[earlier]
    ka_ref, va_ref, kb_ref, vb_ref, stc_ref, std_ref = (ref.at[layer] for ref in state_refs)

    @pl.when(pl.program_id(0) == 0)
    def _():
        for c0 in range(0, IN_WIDTH, 2 * GROUP_W):
            win_s[:, c0:c0 + 2 * GROUP_W] = win_ref[:, c0:c0 + 2 * GROUP_W].astype(BF16)
        wout_s[...] = wout_ref[...].astype(BF16)
        _store_gate_weights(gate_w_ref, wa_ref, wx_ref)

    x = x_ref[...]
    h = _norm_mod(x, n2_ref[lrow, :], sc_ref[0:1, :], sh_ref[0:1, :]).astype(BF16)
    c_cols = slice(COL_CX, COL_CX + 2 * GROUP_W)
    u_ref[:, c_cols] = _dot(h, win_s[:, c_cols])
    _rglru_prepare(u_ref[:, COL_CX:COL_CX + GROUP_W], u_ref[:, COL_CY:COL_CY + GROUP_W],
                   convw_ref, convb_ref[lrow, :], gate_w_ref, ba_ref, bx_ref, lam_ref,
                   af_ref, bf_ref, ab_ref, bb_ref, gel_ref)
    u_ref[:, 0:COL_CX] = _dot(h, win_s[:, 0:COL_CX])
    u_ref[:, COL_DQ:IN_WIDTH] = _dot(h, win_s[:, COL_DQ:IN_WIDTH])

    groups = []
    for (cq, ck, cv, qn_ref, kn_ref, k_out, v_out, use_sink) in (
            (COL_AQ, COL_AK, COL_AV, aqn_ref, akn_ref, ka_ref, va_ref, True),
            (COL_BQ, COL_BK, COL_BV, bqn_ref, bkn_ref, kb_ref, vb_ref, False)):
        q = _head_norm(u_ref[:, cq:cq + GROUP_W], qn_ref[lrow, :])
        k = _head_norm(u_ref[:, ck:ck + KV_W], kn_ref[lrow, :])
        v = u_ref[:, cv:cv + KV_W]
        k_out[...] = k
        v_out[...] = v
        groups.append((q * (HEAD_DIM ** -0.5), k, v, use_sink, []))
    def retention_group():
        masks = _lane_head_masks(GROUP_W)
        lgf, lgb = _log_decays(theta_ref, masks)
        k8 = u_ref[:, COL_DK:COL_DK + GROUP_W] * (HEAD_DIM ** -0.5)
        vb = u_ref[:, COL_DV:COL_DV + GROUP_W].astype(BF16)
        final_states = _retention(u_ref[:, COL_DQ:COL_DQ + GROUP_W], k8, vb, None, lgf, lgb, masks, ret_ref)
        o = ret_ref[...]
        o = o * lax.rsqrt(_head_mean_square(o) + EPS) * dn_ref[lrow, :] * _silu(u_ref[:, COL_DG:COL_DG + GROUP_W])
        mixed_ref[:, 3 * GROUP_W:4 * GROUP_W] = o.astype(BF16)
        for d, s_full in enumerate(final_states):
            for h in range(N_HEADS):
                std_ref[d, h] = s_full[h * HEAD_DIM:(h + 1) * HEAD_DIM, h * HEAD_DIM:(h + 1) * HEAD_DIM]

    for hd in range(N_HEADS):
        kv = hd // 2
        for qs, k, v, use_sink, heads in groups:
            s = _dot_nt(_head_cols(qs, hd), _head_cols(k, kv))
            sink = jnp.full((t, 1), sink_ref[layer, hd], F32) if use_sink else None
            heads.append(_softmax_pv([s], [_head_cols(v, kv)], sink))
        if hd == N_HEADS // 2 - 1:
            retention_group()
    for g, (_, _, _, _, heads) in enumerate(groups):
        mixed_ref[:, g * GROUP_W:(g + 1) * GROUP_W] = jnp.concatenate(heads, axis=-1).astype(BF16)

    zero = jnp.zeros((1, GROUP_W), F32)
    oc, cf, cb = _rglru_finish(zero, zero, af_ref, bf_ref, ab_ref, bb_ref, hf_ref, hb_ref, gel_ref)
    mixed_ref[:, 2 * GROUP_W:3 * GROUP_W] = oc.astype(BF16)
    stc_ref[0:1, :] = cf
    stc_ref[1:2, :] = cb

    xn_ref[...] = x + g2_ref[0:1, :] * _dot(mixed_ref[...], wout_s[...])


def _ctx_mixers(x, mod, layer, prev, norm2_g, w_in, w_out,
                a_qn, a_kn, a_sink, b_qn, b_kn, c_conv_w, c_conv_b, c_wa, c_ba, c_wx, c_bx,
                c_lambda, d_theta, d_norm_g):
    per_request = lambda slots, shape: pl.BlockSpec((None, slots) + shape, lambda b: (b,) + (0,) * (1 + len(shape)))
    state_dims = [(SEQ, KV_W)] * 4 + [(2, GROUP_W), (2, N_HEADS, HEAD_DIM, HEAD_DIM)]
    scr = pltpu.VMEM((SEQ, GROUP_W), F32)
    out = pl.pallas_call(
        functools.partial(_ctx_mixer_kernel, layer=layer, n_prev=len(prev)),
        grid=(BATCH,),
        in_specs=[per_request(layer, dims) for dims in state_dims[:len(prev)]] + [
            pl.BlockSpec((SEQ, D_MODEL), lambda b: (b, 0)),
            _full((DEPTH, D_MODEL)),
            _mod_chunk(layer, 3), _mod_chunk(layer, 4), _mod_chunk(layer, 5),
            _once((None, D_MODEL, IN_WIDTH), lambda b: (layer, 0, 0)),
            _once((None, D_MODEL, D_MODEL), lambda b: (layer, 0, 0)),
            _full((DEPTH, HEAD_DIM)), _full((DEPTH, HEAD_DIM)), _full((DEPTH, HEAD_DIM)), _full((DEPTH, HEAD_DIM)),
            pl.BlockSpec(memory_space=pltpu.SMEM),
            _layer_block((4, GROUP_W), layer), _full((DEPTH, GROUP_W)),
            _layer_block((2, N_HEADS, HEAD_DIM, HEAD_DIM), layer), _layer_block((2, GROUP_W), layer),
            _layer_block((2, N_HEADS, HEAD_DIM, HEAD_DIM), layer), _layer_block((2, GROUP_W), layer),
            _layer_block((2, GROUP_W), layer),
            _layer_block((2, N_HEADS), layer), _full((DEPTH, GROUP_W)),
        ],
        out_specs=[pl.BlockSpec((SEQ, D_MODEL), lambda b: (b, 0))] + [
            per_request(layer + 1, dims) for dims in state_dims],
        out_shape=[jax.ShapeDtypeStruct((N_ROWS, D_MODEL), F32)] + [
            jax.ShapeDtypeStruct((BATCH, layer + 1) + dims, F32) for dims in state_dims],
        input_output_aliases={len(prev): 0},
        scratch_shapes=[pltpu.VMEM((D_MODEL, IN_WIDTH), BF16), pltpu.VMEM((D_MODEL, D_MODEL), BF16),
                        pltpu.VMEM((SEQ, IN_WIDTH), F32), pltpu.VMEM((SEQ, D_MODEL), BF16),
                        pltpu.VMEM((4, GROUP_W, GROUP_W), BF16)] + [scr] * 8,
        compiler_params=_cparams("arbitrary"),
        name="ctx_mixers",
    )(*prev, x, norm2_g, mod, mod, mod, w_in, w_out,
      a_qn, a_kn, b_qn, b_kn, a_sink, c_conv_w, c_conv_b, c_wa, c_ba, c_wx, c_bx,
      c_lambda, d_theta, d_norm_g)
    return out[0], tuple(out[1:])


LAT_BLOCK0 = N_CTX_ROWS // DEC_SEQ


def _lat_attn_kernel(x_ref, n2_ref, sh_ref, sc_ref, g2_ref, win_ref, wout_ref,
                     kca_ref, vca_ref, kcb_ref, vcb_ref,
                     aqn_ref, akn_ref, bqn_ref, bkn_ref, sink_ref, cos_ref, sinl_ref, sinh_ref,
                     xn_ref, h_ref, u_ref, o_ref, *, layer):
    t = DEC_SEQ
    lrow = slice(layer, layer + 1)
    mrow = pl.ds(1 + pl.program_id(0), 1)
    cos, sin_lo, sin_hi = cos_ref[...], sinl_ref[...], sinh_ref[...]
    scale = HEAD_DIM ** -0.5
    x = x_ref[...]
    h_ref[...] = _norm_mod(x, n2_ref[lrow, :], sc_ref[mrow, :], sh_ref[mrow, :]).astype(BF16)
    u_ref[...] = _dot(h_ref[...], win_ref[...].astype(BF16))

    q = _rope(_head_norm(u_ref[:, COL_AQ:COL_AQ + GROUP_W], aqn_ref[lrow, :]), cos, sin_lo, sin_hi)
    k = _rope(_head_norm(u_ref[:, COL_AK:COL_AK + KV_W], akn_ref[lrow, :]), cos, sin_lo, sin_hi)
    qh = [_head_cols(q * scale, h) for h in range(4)]
    v = u_ref[:, COL_AV:COL_AV + KV_W]
    kh = [_head_cols(k, kv) for kv in range(2)]
    vh = [_head_cols(v, kv) for kv in range(2)]
    kch = [_head_cols(kca_ref[...], kv) for kv in range(2)]
    vch = [_head_cols(vca_ref[...], kv) for kv in range(2)]
    w = ATT_BLOCK
    span = 3 * w
    for n in range(t // w):
        start = min(max((n - 1) * w, 0), t - span)
        rows = slice(n * w, (n + 1) * w)
        band = slice(start, start + span)
        qpos = (lax.broadcasted_iota(jnp.int32, (2 * w, span), 0) & (w - 1)) + n * w
        kpos = lax.broadcasted_iota(jnp.int32, (2 * w, span), 1) + start
        valid = jnp.abs(qpos - kpos) <= WINDOW
        heads = []
        for kv in range(2):
            qp = jnp.concatenate([qh[2 * kv][rows, :], qh[2 * kv + 1][rows, :]], axis=0)
            s_ctx = _dot_nt(qp, kch[kv])
            s_band = jnp.where(valid, _dot_nt(qp, kh[kv][band, :]), NEG_INF)
            row = lax.broadcasted_iota(jnp.int32, (2 * w, 1), 0)
            sink = jnp.where(row < w, sink_ref[layer, 2 * kv], sink_ref[layer, 2 * kv + 1])
            o = _softmax_pv([s_ctx, s_band], [vch[kv], vh[kv][band, :]], sink)
            heads += [o[0:w, :], o[w:2 * w, :]]
        o_ref[rows, 0:GROUP_W] = jnp.concatenate(heads, axis=-1).astype(BF16)

    q = _rope(_head_norm(u_ref[:, COL_BQ:COL_BQ + GROUP_W], bqn_ref[lrow, :]), cos, sin_lo, sin_hi)
    k = _rope(_head_norm(u_ref[:, COL_BK:COL_BK + KV_W], bkn_ref[lrow, :]), cos, sin_lo, sin_hi)
    qh = [_head_cols(q * scale, h) for h in range(4)]
    v = u_ref[:, COL_BV:COL_BV + KV_W]
    kh = [_head_cols(k, kv) for kv in range(2)]
    vh = [_head_cols(v, kv) for kv in range(2)]
    kch = [_head_cols(kcb_ref[...], kv) for kv in range(2)]
    vch = [_head_cols(vcb_ref[...], kv) for kv in range(2)]
    tq = 2 * ATT_BLOCK
    for n in range(t // tq):
        rows = slice(n * tq, (n + 1) * tq)
        heads = []
        for kv in range(2):
            qp = jnp.concatenate([qh[2 * kv][rows, :], qh[2 * kv + 1][rows, :]], axis=0)
            o = _softmax_pv([_dot_nt(qp, kch[kv]), _dot_nt(qp, kh[kv])], [vch[kv], vh[kv]], None)
            heads += [o[0:tq, :], o[tq:2 * tq, :]]
        o_ref[rows, GROUP_W:2 * GROUP_W] = jnp.concatenate(heads, axis=-1).astype(BF16)

    xn_ref[...] = x + g2_ref[mrow, :] * _dot(o_ref[...], wout_ref[...].astype(BF16))


def _lat_recurrent_kernel(xn_in_ref, h_ref, g2_ref, wc_ref, wqk_ref, wvg_ref, wout_ref, h0_ref,
                          convw_ref, convb_ref, wa_ref, ba_ref, wx_ref, bx_ref, lam_ref,
                          s0_ref, theta_ref, dn_ref,
                          xn_ref, gate_w_ref, af_ref, bf_ref, ab_ref, bb_ref, hf_ref, hb_ref, gel_ref, ret_ref,
                          *, layer):
    lrow = slice(layer, layer + 1)
    mrow = pl.ds(1 + pl.program_id(0), 1)

    @pl.when(pl.program_id(0) == 0)
    def _():
        _store_gate_weights(gate_w_ref, wa_ref, wx_ref)

    h = h_ref[...]
    u = _dot(h, wc_ref[...].astype(BF16))
    _rglru_prepare(u[:, 0:GROUP_W], u[:, GROUP_W:2 * GROUP_W], convw_ref, convb_ref[lrow, :],
                   gate_w_ref, ba_ref, bx_ref, lam_ref, af_ref, bf_ref, ab_ref, bb_ref, gel_ref)
    oc, _, _ = _rglru_finish(h0_ref[0:1, :], h0_ref[1:2, :],
                             af_ref, bf_ref, ab_ref, bb_ref, hf_ref, hb_ref, gel_ref)
    y = _dot(oc.astype(BF16), wout_ref[0:GROUP_W, :].astype(BF16))

    uqk = _dot(h, wqk_ref[...].astype(BF16))
    uvg = _dot(h, wvg_ref[...].astype(BF16))
    masks = _lane_head_masks(GROUP_W)
    lgf, lgb = _log_decays(theta_ref, masks)
    s0 = tuple(_block_diag([s0_ref[d, hd] for hd in range(N_HEADS)]) for d in range(2))
    _retention(uqk[:, 0:GROUP_W], uqk[:, GROUP_W:2 * GROUP_W] * (HEAD_DIM ** -0.5),
               uvg[:, 0:GROUP_W].astype(BF16), s0, lgf, lgb, masks, ret_ref)
    o = ret_ref[...]
    o = o * lax.rsqrt(_head_mean_square(o) + EPS) * dn_ref[lrow, :] * _silu(uvg[:, GROUP_W:2 * GROUP_W])
    y = y + _dot(o.astype(BF16), wout_ref[GROUP_W:2 * GROUP_W, :].astype(BF16))
    xn_ref[...] = xn_in_ref[...] + g2_ref[mrow, :] * y


def _lat_mixers(x, mod, layer, caches, state_c, state_d, rope, norm2_g, w_in, w_out,
                a_qn, a_kn, a_sink, b_qn, b_kn, c_conv_w, c_conv_b, c_wa, c_ba, c_wx, c_bx,
                c_lambda, d_theta, d_norm_g):
    rows = pl.BlockSpec((DEC_SEQ, D_MODEL), lambda b: (LAT_BLOCK0 + b, 0))
    h_rows = pl.BlockSpec((DEC_SEQ, D_MODEL), lambda b: (b, 0))
    cache_spec = pl.BlockSpec((None, None, PAST_LEN, KV_W), lambda b: (b, layer, 0, 0))
    gain = _full((DEPTH, HEAD_DIM))
    table = _once((DEC_SEQ, LANES), lambda b: (0, 0))
    out_shape = jax.ShapeDtypeStruct((N_ROWS, D_MODEL), F32)
    win_cols = lambda w, c: _once((None, D_MODEL, w), lambda b: (layer, 0, c))
    wout_rows = lambda h, r: _once((None, h, D_MODEL), lambda b: (layer, r, 0))

    xn, h = pl.pallas_call(
        functools.partial(_lat_attn_kernel, layer=layer),
        grid=(DEC_BATCH,),
        in_specs=[rows, _full((DEPTH, D_MODEL)),
                  _mod_chunk(layer, 3), _mod_chunk(layer, 4), _mod_chunk(layer, 5),
                  win_cols(4 * GROUP_W, 0), wout_rows(2 * GROUP_W, 0),
                  cache_spec, cache_spec, cache_spec, cache_spec,
                  gain, gain, gain, gain,
                  pl.BlockSpec(memory_space=pltpu.SMEM),
                  table, table, table],
        out_specs=[rows, pl.BlockSpec((DEC_SEQ, D_MODEL), lambda b: (b, 0), pipeline_mode=pl.Buffered(1))],
        out_shape=[out_shape, jax.ShapeDtypeStruct((N_LAT_ROWS, D_MODEL), BF16)],
        input_output_aliases={0: 0},
        scratch_shapes=[pltpu.VMEM((DEC_SEQ, 4 * GROUP_W), F32), pltpu.VMEM((DEC_SEQ, 2 * GROUP_W), BF16)],
        compiler_params=_cparams("arbitrary"),
        name="lat_attention",
    )(x, norm2_g, mod, mod, mod, w_in, w_out, *caches, a_qn, a_kn, b_qn, b_kn, a_sink, *rope)

    scr = pltpu.VMEM((DEC_SEQ, GROUP_W), F32)
    xn = pl.pallas_call(
        functools.partial(_lat_recurrent_kernel, layer=layer),
        grid=(DEC_BATCH,),
        in_specs=[
            rows, h_rows, _mod_chunk(layer, 5),
            win_cols(2 * GROUP_W, COL_CX // (2 * GROUP_W)),
            win_cols(2 * GROUP_W, COL_DQ // (2 * GROUP_W)), win_cols(2 * GROUP_W, COL_DV // (2 * GROUP_W)),
            wout_rows(2 * GROUP_W, 1),
            pl.BlockSpec((None, None, 2, GROUP_W), lambda b: (b, layer, 0, 0)),
            _layer_block((4, GROUP_W), layer), _full((DEPTH, GROUP_W)),
            _layer_block((2, N_HEADS, HEAD_DIM, HEAD_DIM), layer), _layer_block((2, GROUP_W), layer),
            _layer_block((2, N_HEADS, HEAD_DIM, HEAD_DIM), layer), _layer_block((2, GROUP_W), layer),
            _layer_block((2, GROUP_W), layer),
            pl.BlockSpec((None, None, 2, N_HEADS, HEAD_DIM, HEAD_DIM), lambda b: (b, layer, 0, 0, 0, 0)),
            _layer_block((2, N_HEADS), layer), _full((DEPTH, GROUP_W))],
        out_specs=rows,
        out_shape=out_shape,
        input_output_aliases={0: 0},
        scratch_shapes=[pltpu.VMEM((4, GROUP_W, GROUP_W), BF16)] + [scr] * 8,
        compiler_params=_cparams("arbitrary"),
        name="lat_recurrent",
    )(xn, h, mod, w_in, w_in, w_in, w_out, state_c, c_conv_w, c_conv_b, c_wa, c_ba, c_wx, c_bx, c_lambda,
      state_d, d_theta, d_norm_g)
    return xn


def _rope_tables():
    t = np.arange(DEC_SEQ)
    row = (t // GRID_W).astype(np.float64)[:, None]
    col = (t % GRID_W).astype(np.float64)[:, None]
    half = HEAD_DIM // 2
    inv = 1.0 / (ROPE_BASE ** (np.arange(0, half, 2, dtype=np.float64) / half))
    j = np.arange(LANES) % HEAD_DIM
    ang = np.where((j < half)[None, :], row, col) * inv[j % (half // 2)][None, :]
    first = ((j % half) < half // 2)[None, :]
    cos, sin = np.cos(ang), np.sin(ang)
    return tuple(jnp.asarray(a, F32) for a in (cos, np.where(first, -sin, 0.0), np.where(first, 0.0, sin)))


def kernel(x_prompt, x_sample, cache_a_k, cache_a_v, cache_b_k, cache_b_v, state_c, state_d, c, c_ctx, norm1_g, norm2_g, norm3_g, w_mod, b_mod, ffn1_wg, ffn1_wu, ffn1_wd, ffn2_wg, ffn2_wu, ffn2_wd, w_in, w_out, a_qn, a_kn, a_sink, b_qn, b_kn, c_conv_w, c_conv_b, c_wa, c_ba, c_wx, c_bx, c_lambda, d_theta, d_norm_g):
    mod = _modulation(c_ctx, c, w_mod, b_mod)
    rope = _rope_tables()
    caches = tuple(t.reshape(DEC_BATCH, DEPTH, PAST_LEN, KV_W) for t in (cache_a_k, cache_a_v, cache_b_k, cache_b_v))
    mixer_params = (a_qn, a_kn, a_sink, b_qn, b_kn, c_conv_w, c_conv_b, c_wa, c_ba, c_wx, c_bx,
                    c_lambda, d_theta, d_norm_g)
    xs = (x_prompt.reshape(N_CTX_ROWS, D_MODEL), x_sample.reshape(N_LAT_ROWS, D_MODEL))
    states = ()
    for l in range(DEPTH):
        (x,) = _ffn(xs, mod, l, 0, norm1_g, ffn1_wg, ffn1_wu, ffn1_wd)
        x, states = _ctx_mixers(x, mod, l, states, norm2_g, w_in, w_out, *mixer_params)
        x = _lat_mixers(x, mod, l, caches, state_c, state_d, rope, norm2_g, w_in, w_out, *mixer_params)
        xs = _ffn((x,), mod, l, 6, norm3_g, ffn2_wg, ffn2_wu, ffn2_wd, split_out=(l == DEPTH - 1))
    y_p, y_s = xs
    ka, va, kb, vb, st_c, st_d = states
    kv_shape = (BATCH, DEPTH, SEQ, 2, HEAD_DIM)
    return (y_p.reshape(BATCH, SEQ, D_MODEL), y_s.reshape(DEC_BATCH, DEC_SEQ, D_MODEL),
            ka.reshape(kv_shape), va.reshape(kv_shape), kb.reshape(kv_shape), vb.reshape(kv_shape),
            st_c, st_d)
```

```python
import functools
import math

import numpy as np
import jax
import jax.numpy as jnp
from jax import lax
from jax.experimental import pallas as pl
from jax.experimental.pallas import tpu as pltpu

F32 = jnp.float32
BF16 = jnp.bfloat16

D_MODEL = 1024
BATCH = 16
SEQ = 256
DEPTH = 2
DEC_BATCH = 2
DEC_SEQ = 1024
PAST_LEN = 512
GRID_W = 64
HEAD_DIM = 64
HEAD_SHIFT = 6
N_HEADS = 4
GROUP_W = 256
KV_W = 2 * HEAD_DIM
LANES = 128
WINDOW = 128
ATT_BLOCK = 128
ROPE_BASE = 10000.0
LRU_C = 8.0
D_FF = 2816
N_MOD = 9
EPS = 1e-6
NEG_INF = -1e30
IN_WIDTH = 2560

N_CTX_ROWS = BATCH * SEQ
N_LAT_ROWS = DEC_BATCH * DEC_SEQ
N_ROWS = N_CTX_ROWS + N_LAT_ROWS
MOD_ROWS = 8
MOD_GROUP = 1024

VMEM_LIMIT_BYTES = 56 * 1024 * 1024

COL_AQ, COL_AK, COL_AV = 0, 256, 384
COL_BQ, COL_BK, COL_BV = 512, 768, 896
COL_CX, COL_CY = 1024, 1280
COL_DQ, COL_DK, COL_DV, COL_DG = 1536, 1792, 2048, 2304


def _cparams(*sem):
    return pltpu.CompilerParams(dimension_semantics=sem, vmem_limit_bytes=VMEM_LIMIT_BYTES)


def _dot(a, b):
    return jnp.dot(a, b, preferred_element_type=F32)


def _dot_nt(a, b):
    return lax.dot_general(a, b, (((1,), (1,)), ((), ())), preferred_element_type=F32)


def _dot_tn(a, b):
    return lax.dot_general(a, b, (((0,), (0,)), ((), ())), preferred_element_type=F32)


def _sigmoid(x):
    return 0.5 * jnp.tanh(0.5 * x) + 0.5


def _silu(x):
    return x * _sigmoid(x)


def _gelu_tanh(x):
    return 0.5 * x * (1.0 + jnp.tanh(math.sqrt(2.0 / math.pi) * (x + 0.044715 * (x * x * x))))


def _mod_row(i, tm, s):
    if tm >= MOD_GROUP:
        block_index = i * (tm // MOD_GROUP) + s
    else:
        block_index = i >> int(math.log2(MOD_GROUP // tm))
    return jnp.maximum(block_index - (N_CTX_ROWS // MOD_GROUP - 1), 0)


def _norm_mod(x, g, sc, sh):
    ms = jnp.mean(x * x, axis=-1, keepdims=True)
    return (x * lax.rsqrt(ms + EPS) * g) * (1.0 + sc) + sh


def _full(shape):
    return pl.BlockSpec(shape, lambda *_: (0,) * len(shape))


def _layer_block(shape, layer):
    return pl.BlockSpec((None,) + shape, lambda *_: (layer,) + (0,) * len(shape))


MOD_TN = 3072


def _mod_kernel(cc_ref, c_ref, w_ref, b_ref, o_ref):
    l = pl.program_id(0)
    pad = jnp.zeros((MOD_ROWS - 1 - DEC_BATCH, D_MODEL), F32)
    cond = jnp.concatenate([cc_ref[...], c_ref[...], pad], axis=0)
    o_ref[...] = _dot(_silu(cond).astype(BF16), w_ref[...].astype(BF16)) + b_ref[pl.ds(l, 1), :]


def _modulation(c_ctx, c, w_mod, b_mod):
    n = N_MOD * D_MODEL
    return pl.pallas_call(
        _mod_kernel,
        grid=(DEPTH, n // MOD_TN),
        in_specs=[
            pl.BlockSpec((1, D_MODEL), lambda l, j: (0, 0)),
            pl.BlockSpec((DEC_BATCH, D_MODEL), lambda l, j: (0, 0)),
            pl.BlockSpec((None, D_MODEL, MOD_TN), lambda l, j: (l, 0, j)),
            pl.BlockSpec((DEPTH, MOD_TN), lambda l, j: (0, j)),
        ],
        out_specs=pl.BlockSpec((None, MOD_ROWS, MOD_TN), lambda l, j: (l, 0, j)),
        out_shape=jax.ShapeDtypeStruct((DEPTH, MOD_ROWS, n), F32),
        compiler_params=_cparams("arbitrary", "arbitrary"),
        name="modulation",
    )(c_ctx.reshape(1, D_MODEL), c, w_mod, b_mod)


FFN_TM = 1024
FFN_TF = 256
N_CTX_TILES = N_CTX_ROWS // FFN_TM


FFN_NJ = D_FF // FFN_TF
N_FFN_TILES = N_ROWS // FFN_TM
N_FFN_STEPS = FFN_NJ + N_FFN_TILES


def _ffn_tile(step):
    return jnp.maximum(step - FFN_NJ, 0)


def _on_stream_part(tile, x_refs, o_refs, fn):
    if len(x_refs) == 1 and len(o_refs) == 1:
        fn(x_refs[0], o_refs[0])
    else:
        pl.when(tile < N_CTX_TILES)(lambda: fn(x_refs[0], o_refs[0]))
        pl.when(tile >= N_CTX_TILES)(lambda: fn(x_refs[-1], o_refs[-1]))


def _ffn_kernel(*refs, layer, n_in, n_out):
    x_refs = refs[:n_in]
    n_ref, sh_ref, sc_ref, g_ref, wg_ref, wu_ref, wd_ref = refs[n_in:n_in + 7]
    o_refs = refs[n_in + 7:n_in + 7 + n_out]
    h_ref, a_ref, wg_s, wu_s, wd_s = refs[n_in + 7 + n_out:]
    nj, tf = FFN_NJ, FFN_TF
    s = pl.program_id(0)
    tile = _ffn_tile(s)
    r = _mod_row(tile, FFN_TM, 0)

    def load_tile():
        x = x_refs[0][...] if len(x_refs) == 1 else jnp.where(tile < N_CTX_TILES, x_refs[0][...], x_refs[1][...])
        h = _norm_mod(x, n_ref[layer:layer + 1, :], sc_ref[pl.ds(r, 1), :], sh_ref[pl.ds(r, 1), :])
        h_ref[...] = h.astype(BF16)

    def up_chunk(j, cols):
        h = h_ref[...]
        a_ref[:, cols] = (_silu(_dot(h, wg_s[j])) * _dot(h, wu_s[j])).astype(BF16)

    def down_and_store():
        y = (0.5 * g_ref[pl.ds(r, 1), :]) * _dot(a_ref[...], wd_s[...])

        def store(x_ref, o_ref):
            o_ref[...] = x_ref[...] + y
        _on_stream_part(tile, x_refs, o_refs, store)

    def keep_arrived_chunk():
        wg_s[s] = wg_ref[...].astype(BF16)
        wu_s[s] = wu_ref[...].astype(BF16)
        wd_s[pl.ds(pl.multiple_of(s * tf, tf), tf), :] = wd_ref[...].astype(BF16)

    def up_previous_chunk():
        up_chunk(s - 1, pl.ds(pl.multiple_of((s - 1) * tf, tf), tf))

    @pl.when(s == 0)
    def _():
        load_tile()
        keep_arrived_chunk()

    @pl.when((s > 0) & (s < nj))
    def _():
        up_previous_chunk()
        keep_arrived_chunk()

    @pl.when(s == nj)
    def _():
        up_previous_chunk()
        down_and_store()

    @pl.when(s > nj)
    def _():
        load_tile()
        for j in range(nj):
            up_chunk(j, slice(j * tf, (j + 1) * tf))
        down_and_store()


def _stream_specs(split, buffered_once):
    tm = FFN_TM
    kw = {"pipeline_mode": pl.Buffered(1)} if buffered_once else {}
    if not split:
        return [pl.BlockSpec((tm, D_MODEL), lambda s: (_ffn_tile(s), 0), **kw)]
    last_ctx = N_CTX_TILES - 1
    return [pl.BlockSpec((tm, D_MODEL), lambda s: (jnp.minimum(_ffn_tile(s), last_ctx), 0), **kw),
            pl.BlockSpec((tm, D_MODEL), lambda s: (jnp.maximum(_ffn_tile(s) - N_CTX_TILES, 0), 0), **kw)]


def _ffn(xs, mod, layer, chunk0, norm_g, wg, wu, wd, split_out=False):
    tm, tf, nj = FFN_TM, FFN_TF, FFN_NJ
    split_in = len(xs) == 2
    mod_spec = lambda c: pl.BlockSpec((None, MOD_ROWS, D_MODEL), lambda s: (layer, 0, c))
    w_col = lambda s: (layer, 0, jnp.minimum(s, nj - 1))
    w_row = lambda s: (layer, jnp.minimum(s, nj - 1), 0)
    if split_out:
        out_shape = [jax.ShapeDtypeStruct((N_CTX_ROWS, D_MODEL), F32),
                     jax.ShapeDtypeStruct((N_LAT_ROWS, D_MODEL), F32)]
    else:
        out_shape = [jax.ShapeDtypeStruct((N_ROWS, D_MODEL), F32)]
    out = pl.pallas_call(
        functools.partial(_ffn_kernel, layer=layer, n_in=len(xs), n_out=len(out_shape)),
        grid=(N_FFN_STEPS,),
        in_specs=_stream_specs(split_in, False) + [
            _full((DEPTH, D_MODEL)),
            mod_spec(chunk0), mod_spec(chunk0 + 1), mod_spec(chunk0 + 2),
            pl.BlockSpec((None, D_MODEL, tf), w_col),
            pl.BlockSpec((None, D_MODEL, tf), w_col),
            pl.BlockSpec((None, tf, D_MODEL), w_row),
        ],
        out_specs=_stream_specs(split_out, True),
        out_shape=out_shape,
        scratch_shapes=[pltpu.VMEM((tm, D_MODEL), BF16),
                        pltpu.VMEM((tm, D_FF), BF16),
                        pltpu.VMEM((nj, D_MODEL, tf), BF16),
                        pltpu.VMEM((nj, D_MODEL, tf), BF16),
                        pltpu.VMEM((D_FF, D_MODEL), BF16)],
        compiler_params=_cparams("arbitrary"),
        name="ffn",
    )(*xs, norm_g, mod, mod, mod, wg, wu, wd)
    return tuple(out)


def _once(shape, index_map):
    return pl.BlockSpec(shape, index_map, pipeline_mode=pl.Buffered(1))


def _mod_chunk(layer, c):
    return pl.BlockSpec((None, MOD_ROWS, D_MODEL), lambda *_: (layer, 0, c))


def _head_mean_square(x):
    n = x.shape[-1]
    r = lax.broadcasted_iota(jnp.int32, (n, n), 0) >> HEAD_SHIFT
    c = lax.broadcasted_iota(jnp.int32, (n, n), 1) >> HEAD_SHIFT
    ones_bd = jnp.where(r == c, 1.0, 0.0).astype(BF16)
    return _dot((x * x).astype(BF16), ones_bd) * (1.0 / HEAD_DIM)


def _head_norm(x, head_gain):
    gain_row = jnp.concatenate([head_gain] * (x.shape[-1] // HEAD_DIM), axis=-1)
    return x * lax.rsqrt(_head_mean_square(x) + EPS) * gain_row


def _head_cols(x, h):
    return x[:, h * HEAD_DIM:(h + 1) * HEAD_DIM].astype(BF16)


def _softmax_pv(scores, values, sink):
    m = jnp.max(scores[0], axis=-1, keepdims=True)
    for s in scores[1:]:
        m = jnp.maximum(m, jnp.max(s, axis=-1, keepdims=True))
    if sink is not None:
        m = jnp.maximum(m, sink)
    denom = None
    acc = None
    for s, v in zip(scores, values):
        p = jnp.exp(s - m)
        d = jnp.sum(p, axis=-1, keepdims=True)
        o = _dot(p.astype(BF16), v)
        denom = d if denom is None else denom + d
        acc = o if acc is None else acc + o
    if sink is not None:
        denom = denom + jnp.exp(sink - m)
    return acc / denom


def _rope(x, cos, sin_lo, sin_hi):
    cols = []
    for c in range(x.shape[-1] // LANES):
        xc = x[:, c * LANES:(c + 1) * LANES]
        cols.append(xc * cos + pltpu.roll(xc, 112, 1) * sin_lo + pltpu.roll(xc, 16, 1) * sin_hi)
    return cols[0] if len(cols) == 1 else jnp.concatenate(cols, axis=-1)


def _block_diag(blocks):
    n = len(blocks)
    w = blocks[0].shape[0]
    rows = []
    for k, blk in enumerate(blocks):
        parts = []
        if k > 0:
            parts.append(jnp.zeros((w, k * w), F32))
        parts.append(blk)
        if k < n - 1:
            parts.append(jnp.zeros((w, (n - 1 - k) * w), F32))
        rows.append(jnp.concatenate(parts, axis=-1))
    return jnp.concatenate(rows, axis=0)


def _rglru_gates(xc, wa, ba, wx, bx, lam):
    xb = xc.astype(BF16)
    r = _sigmoid(_dot(xb, wa) + ba)
    i = _sigmoid(_dot(xb, wx) + bx)
    softplus = jnp.maximum(-lam, 0.0) + jnp.log1p(jnp.exp(-jnp.abs(lam)))
    log_a = (-LRU_C) * r * softplus
    a = jnp.exp(log_a)
    b = jnp.sqrt(1.0 - a * a) * (i * xc)
    return a, b


def _block_prefix(a, b, reverse):
    t = a.shape[0]
    row = lax.broadcasted_iota(jnp.int32, a.shape, 0) & 7
    for d in (1, 2, 4):
        if reverse:
            a_s = pltpu.roll(a, t - d, 0)
            b_s = pltpu.roll(b, t - d, 0)
            ok = row < 8 - d
        else:
            a_s = pltpu.roll(a, d, 0)
            b_s = pltpu.roll(b, d, 0)
            ok = row >= d
        b = jnp.where(ok, a * b_s + b, b)
        a = jnp.where(ok, a * a_s, a)
    return a, b


def _conv4(x, w_ref, b_row):
    t = x.shape[0]
    row = lax.broadcasted_iota(jnp.int32, x.shape, 0)
    xm2 = jnp.where(row >= 2, pltpu.roll(x, 2, 0), 0.0)
    xm1 = jnp.where(row >= 1, pltpu.roll(x, 1, 0), 0.0)
    xp1 = jnp.where(row < t - 1, pltpu.roll(x, t - 1, 0), 0.0)
    return (xm2 * w_ref[0:1, :] + xm1 * w_ref[1:2, :] + x * w_ref[2:3, :] + xp1 * w_ref[3:4, :]) + b_row


def _rglru_prepare(cx, cy, conv_w_ref, conv_b, gate_w_ref, ba_ref, bx_ref, lam_ref,
                   af_ref, bf_ref, ab_ref, bb_ref, gel_ref):
    xc = _conv4(cx, conv_w_ref, conv_b)
    a, b = _rglru_gates(xc, gate_w_ref[0], ba_ref[0:1, :], gate_w_ref[1], bx_ref[0:1, :], lam_ref[0:1, :])
    a, b = _block_prefix(a, b, reverse=False)
    af_ref[...] = a
    bf_ref[...] = b
    a, b = _rglru_gates(xc, gate_w_ref[2], ba_ref[1:2, :], gate_w_ref[3], bx_ref[1:2, :], lam_ref[1:2, :])
    a, b = _block_prefix(a, b, reverse=True)
    ab_ref[...] = a
    bb_ref[...] = b
    gel_ref[...] = _gelu_tanh(cy)


SCAN_UNROLL = 8


def _rglru_finish(h0f, h0b, af_ref, bf_ref, ab_ref, bb_ref, hf_ref, hb_ref, gel_ref):
    nblk = af_ref.shape[0] // 8

    def body(k, carry):
        cf, cb = carry
        rf = pl.ds(pl.multiple_of(k * 8, 8), 8)
        hf = bf_ref[rf, :] + af_ref[rf, :] * cf
        hf_ref[rf, :] = hf
        rb = pl.ds(pl.multiple_of((nblk - 1 - k) * 8, 8), 8)
        hb = bb_ref[rb, :] + ab_ref[rb, :] * cb
        hb_ref[rb, :] = hb
        return hf[7:8, :], hb[0:1, :]

    cf, cb = lax.fori_loop(0, nblk, body, (h0f, h0b), unroll=SCAN_UNROLL)
    oc = (hf_ref[...] + hb_ref[...]) * gel_ref[...]
    return oc, cf, cb


def _store_gate_weights(gate_w_ref, wa_ref, wx_ref):
    for d in range(2):
        gate_w_ref[2 * d] = _block_diag([wa_ref[d, n] for n in range(N_HEADS)]).astype(BF16)
        gate_w_ref[2 * d + 1] = _block_diag([wx_ref[d, n] for n in range(N_HEADS)]).astype(BF16)


def _lane_head_masks(n):
    lane = lax.broadcasted_iota(jnp.int32, (1, n), 1) >> HEAD_SHIFT
    return [jnp.where(lane == h, 1.0, 0.0) for h in range(n // HEAD_DIM)]


def _log_decays(theta_ref, masks):
    theta = theta_ref[...]
    lanes = theta[:, 0:1] * masks[0]
    for h in range(1, N_HEADS):
        lanes = lanes + theta[:, h:h + 1] * masks[h]
    lg = jnp.log1p(-jnp.exp(lanes))
    return lg[0:1, :], lg[1:2, :]


RET_BLOCK = 256


def _retention(q, k8, vb, s0, lgf, lgb, masks, o_ref):
    t, w = q.shape
    c = RET_BLOCK
    nh = w // HEAD_DIM
    pos = lax.broadcasted_iota(jnp.int32, (c, w), 0).astype(F32)
    q_dec = (jnp.exp(lgf * (pos + 1.0)), jnp.exp(lgb * (float(c) - pos)))
    k_dec = (jnp.exp(lgf * (float(c - 1) - pos)), jnp.exp(lgb * pos))
    chunk_dec = (jnp.exp(lgf * float(c)), jnp.exp(lgb * float(c)))
    rel = (lax.broadcasted_iota(jnp.int32, (c, c), 0) - lax.broadcasted_iota(jnp.int32, (c, c), 1)).astype(F32)
    decs = []
    for h in range(nh):
        gf = lgf[:, h * HEAD_DIM:h * HEAD_DIM + 1]
        gb = lgb[:, h * HEAD_DIM:h * HEAD_DIM + 1]
        e = jnp.exp(jnp.where(rel >= 0, gf * rel, gb * (-rel)))
        decs.append(jnp.where(rel == 0, 2.0, e))
    dec = jnp.concatenate(decs, axis=0)
    r_head = lax.broadcasted_iota(jnp.int32, (w, w), 0) >> HEAD_SHIFT
    c_head = lax.broadcasted_iota(jnp.int32, (w, w), 1) >> HEAD_SHIFT
    same_head = jnp.where(r_head == c_head, 1.0, 0.0)
    states = [None, None] if s0 is None else list(s0)

    def carry(d, rows, o):
        if states[d] is not None:
            o = o + _dot((q[rows, :] * q_dec[d]).astype(BF16), states[d].astype(BF16))
        upd = _dot_tn((k8[rows, :] * k_dec[d]).astype(BF16), vb[rows, :]) * same_head
        states[d] = upd if states[d] is None else states[d] * chunk_dec[d] + upd
        return o

    for ci in range(t // c):
        rows = slice(ci * c, (ci + 1) * c)
        qc = q[rows, :]
        q_stack = jnp.concatenate([(qc * masks[h]).astype(BF16) for h in range(nh)], axis=0)
        inner = (_dot_nt(q_stack, k8[rows, :].astype(BF16)) * dec).astype(BF16)
        out = _dot(inner, vb[rows, :])
        o = out[0:c, :] * masks[0]
        for h in range(1, nh):
            o = o + out[h * c:(h + 1) * c, :] * masks[h]
        o_ref[rows, :] = carry(0, rows, o)
    for ci in reversed(range(t // c)):
        rows = slice(ci * c, (ci + 1) * c)
        if states[1] is not None:
            o_ref[rows, :] = carry(1, rows, o_ref[rows, :])
        else:
            carry(1, rows, None)
    return states[0], states[1]


def _ctx_mixer_kernel(*refs, layer, n_prev):
    prev_refs = refs[:n_prev]
    (x_ref, n2_ref, sh_ref, sc_ref, g2_ref, win_ref, wout_ref,
     aqn_ref, akn_ref, bqn_ref, bkn_ref, sink_ref,
     convw_ref, convb_ref, wa_ref, ba_ref, wx_ref, bx_ref, lam_ref, theta_ref, dn_ref,
     xn_ref, *state_refs) = refs[n_prev:n_prev + 28]
    (win_s, wout_s, u_ref, mixed_ref,
     gate_w_ref, af_ref, bf_ref, ab_ref, bb_ref, hf_ref, hb_ref, gel_ref, ret_ref) = refs[n_prev + 28:]
    t = SEQ
    lrow = slice(layer, layer + 1)
    for prev_ref, state_ref in zip(prev_refs, state_refs):
        for earlier in range(layer):
            state_ref[earlier] = prev_ref[earlier]
    ka_ref, va_ref, kb_ref, vb_ref, stc_ref, std_ref = (ref.at[layer] for ref in state_refs)

    @pl.when(pl.program_id(0) == 0)
    def _():
        for c0 in range(0, IN_WIDTH, 2 * GROUP_W):
            win_s[:, c0:c0 + 2 * GROUP_W] = win_ref[:, c0:c0 + 2 * GROUP_W].astype(BF16)
        wout_s[...] = wout_ref[...].astype(BF16)
        _store_gate_weights(gate_w_ref, wa_ref, wx_ref)

    x = x_ref[...]
    h = _norm_mod(x, n2_ref[lrow, :], sc_ref[0:1, :], sh_ref[0:1, :]).astype(BF16)
    c_cols = slice(COL_CX, COL_CX + 2 * GROUP_W)
    u_ref[:, c_cols] = _dot(h, win_s[:, c_cols])
    _rglru_prepare(u_ref[:, COL_CX:COL_CX + GROUP_W], u_ref[:, COL_CY:COL_CY + GROUP_W],
                   convw_ref, convb_ref[lrow, :], gate_w_ref, ba_ref, bx_ref, lam_ref,
                   af_ref, bf_ref, ab_ref, bb_ref, gel_ref)
    u_ref[:, 0:COL_CX] = _dot(h, win_s[:, 0:COL_CX])
    u_ref[:, COL_DQ:IN_WIDTH] = _dot(h, win_s[:, COL_DQ:IN_WIDTH])

    groups = []
    for (cq, ck, cv, qn_ref, kn_ref, k_out, v_out, use_sink) in (
            (COL_AQ, COL_AK, COL_AV, aqn_ref, akn_ref, ka_ref, va_ref, True),
            (COL_BQ, COL_BK, COL_BV, bqn_ref, bkn_ref, kb_ref, vb_ref, False)):
        q = _head_norm(u_ref[:, cq:cq + GROUP_W], qn_ref[lrow, :])
        k = _head_norm(u_ref[:, ck:ck + KV_W], kn_ref[lrow, :])
        v = u_ref[:, cv:cv + KV_W]
        k_out[...] = k
        v_out[...] = v
        groups.append((q * (HEAD_DIM ** -0.5), k, v, use_sink, []))
    for kv in range(N_HEADS // 2):
        for qs, k, v, use_sink, heads in groups:
            for hd in (2 * kv, 2 * kv + 1):
                s = _dot_nt(_head_cols(qs, hd), _head_cols(k, kv))
                sink = jnp.full((t, 1), sink_ref[layer, hd], F32) if use_sink else None
                heads.append(_softmax_pv([s], [_head_cols(v, kv)], sink))
    for g, (_, _, _, _, heads) in enumerate(groups):
        mixed_ref[:, g * GROUP_W:(g + 1) * GROUP_W] = jnp.concatenate(heads, axis=-1).astype(BF16)

    zero = jnp.zeros((1, GROUP_W), F32)
    oc, cf, cb = _rglru_finish(zero, zero, af_ref, bf_ref, ab_ref, bb_ref, hf_ref, hb_ref, gel_ref)
    mixed_ref[:, 2 * GROUP_W:3 * GROUP_W] = oc.astype(BF16)
    stc_ref[0:1, :] = cf
    stc_ref[1:2, :] = cb

    masks = _lane_head_masks(GROUP_W)
    lgf, lgb = _log_decays(theta_ref, masks)
    k8 = u_ref[:, COL_DK:COL_DK + GROUP_W] * (HEAD_DIM ** -0.5)
    vb = u_ref[:, COL_DV:COL_DV + GROUP_W].astype(BF16)
    final_states = _retention(u_ref[:, COL_DQ:COL_DQ + GROUP_W], k8, vb, None, lgf, lgb, masks, ret_ref)
    o = ret_ref[...]
    o = o * lax.rsqrt(_head_mean_square(o) + EPS) * dn_ref[lrow, :] * _silu(u_ref[:, COL_DG:COL_DG + GROUP_W])
    mixed_ref[:, 3 * GROUP_W:4 * GROUP_W] = o.astype(BF16)
    for d, s_full in enumerate(final_states):
        for hd in range(N_HEADS):
            std_ref[d, hd] = s_full[hd * HEAD_DIM:(hd + 1) * HEAD_DIM, hd * HEAD_DIM:(hd + 1) * HEAD_DIM]

    xn_ref[...] = x + g2_ref[0:1, :] * _dot(mixed_ref[...], wout_s[...])


def _ctx_mixers(x, mod, layer, prev, norm2_g, w_in, w_out,
                a_qn, a_kn, a_sink, b_qn, b_kn, c_conv_w, c_conv_b, c_wa, c_ba, c_wx, c_bx,
                c_lambda, d_theta, d_norm_g):
    per_request = lambda slots, shape: pl.BlockSpec((None, slots) + shape, lambda b: (b,) + (0,) * (1 + len(shape)))
    state_dims = [(SEQ, KV_W)] * 4 + [(2, GROUP_W), (2, N_HEADS, HEAD_DIM, HEAD_DIM)]
    scr = pltpu.VMEM((SEQ, GROUP_W), F32)
    out = pl.pallas_call(
        functools.partial(_ctx_mixer_kernel, layer=layer, n_prev=len(prev)),
        grid=(BATCH,),
        in_specs=[per_request(layer, dims) for dims in state_dims[:len(prev)]] + [
            pl.BlockSpec((SEQ, D_MODEL), lambda b: (b, 0)),
            _full((DEPTH, D_MODEL)),
            _mod_chunk(layer, 3), _mod_chunk(layer, 4), _mod_chunk(layer, 5),
            _once((None, D_MODEL, IN_WIDTH), lambda b: (layer, 0, 0)),
            _once((None, D_MODEL, D_MODEL), lambda b: (layer, 0, 0)),
            _full((DEPTH, HEAD_DIM)), _full((DEPTH, HEAD_DIM)), _full((DEPTH, HEAD_DIM)), _full((DEPTH, HEAD_DIM)),
            pl.BlockSpec(memory_space=pltpu.SMEM),
            _layer_block((4, GROUP_W), layer), _full((DEPTH, GROUP_W)),
            _layer_block((2, N_HEADS, HEAD_DIM, HEAD_DIM), layer), _layer_block((2, GROUP_W), layer),
            _layer_block((2, N_HEADS, HEAD_DIM, HEAD_DIM), layer), _layer_block((2, GROUP_W), layer),
            _layer_block((2, GROUP_W), layer),
            _layer_block((2, N_HEADS), layer), _full((DEPTH, GROUP_W)),
        ],
        out_specs=[pl.BlockSpec((SEQ, D_MODEL), lambda b: (b, 0))] + [
            per_request(layer + 1, dims) for dims in state_dims],
        out_shape=[jax.ShapeDtypeStruct((N_ROWS, D_MODEL), F32)] + [
            jax.ShapeDtypeStruct((BATCH, layer + 1) + dims, F32) for dims in state_dims],
        input_output_aliases={len(prev): 0},
        scratch_shapes=[pltpu.VMEM((D_MODEL, IN_WIDTH), BF16), pltpu.VMEM((D_MODEL, D_MODEL), BF16),
                        pltpu.VMEM((SEQ, IN_WIDTH), F32), pltpu.VMEM((SEQ, D_MODEL), BF16),
                        pltpu.VMEM((4, GROUP_W, GROUP_W), BF16)] + [scr] * 8,
        compiler_params=_cparams("arbitrary"),
        name="ctx_mixers",
    )(*prev, x, norm2_g, mod, mod, mod, w_in, w_out,
      a_qn, a_kn, b_qn, b_kn, a_sink, c_conv_w, c_conv_b, c_wa, c_ba, c_wx, c_bx,
      c_lambda, d_theta, d_norm_g)
    return out[0], tuple(out[1:])


LAT_BLOCK0 = N_CTX_ROWS // DEC_SEQ


def _lat_attn_kernel(x_ref, n2_ref, sh_ref, sc_ref, g2_ref, win_ref, wout_ref,
                     kca_ref, vca_ref, kcb_ref, vcb_ref,
                     aqn_ref, akn_ref, bqn_ref, bkn_ref, sink_ref, cos_ref, sinl_ref, sinh_ref,
                     xn_ref, h_ref, u_ref, o_ref, *, layer):
    t = DEC_SEQ
    lrow = slice(layer, layer + 1)
    mrow = pl.ds(1 + pl.program_id(0), 1)
    cos, sin_lo, sin_hi = cos_ref[...], sinl_ref[...], sinh_ref[...]
    scale = HEAD_DIM ** -0.5
    x = x_ref[...]
    h_ref[...] = _norm_mod(x, n2_ref[lrow, :], sc_ref[mrow, :], sh_ref[mrow, :]).astype(BF16)
    u_ref[...] = _dot(h_ref[...], win_ref[...].astype(BF16))

    q = _rope(_head_norm(u_ref[:, COL_AQ:COL_AQ + GROUP_W], aqn_ref[lrow, :]), cos, sin_lo, sin_hi)
    k = _rope(_head_norm(u_ref[:, COL_AK:COL_AK + KV_W], akn_ref[lrow, :]), cos, sin_lo, sin_hi)
    qh = [_head_cols(q * scale, h) for h in range(4)]
    v = u_ref[:, COL_AV:COL_AV + KV_W]
    kh = [_head_cols(k, kv) for kv in range(2)]
    vh = [_head_cols(v, kv) for kv in range(2)]
    kch = [_head_cols(kca_ref[...], kv) for kv in range(2)]
    vch = [_head_cols(vca_ref[...], kv) for kv in range(2)]
    w = ATT_BLOCK
    span = 3 * w
    for n in range(t // w):
        start = min(max((n - 1) * w, 0), t - span)
        rows = slice(n * w, (n + 1) * w)
        band = slice(start, start + span)
        qpos = (lax.broadcasted_iota(jnp.int32, (2 * w, span), 0) & (w - 1)) + n * w
        kpos = lax.broadcasted_iota(jnp.int32, (2 * w, span), 1) + start
        valid = jnp.abs(qpos - kpos) <= WINDOW
        heads = []
        for kv in range(2):
            qp = jnp.concatenate([qh[2 * kv][rows, :], qh[2 * kv + 1][rows, :]], axis=0)
            s_ctx = _dot_nt(qp, kch[kv])
            s_band = jnp.where(valid, _dot_nt(qp, kh[kv][band, :]), NEG_INF)
            row = lax.broadcasted_iota(jnp.int32, (2 * w, 1), 0)
            sink = jnp.where(row < w, sink_ref[layer, 2 * kv], sink_ref[layer, 2 * kv + 1])
            o = _softmax_pv([s_ctx, s_band], [vch[kv], vh[kv][band, :]], sink)
            heads += [o[0:w, :], o[w:2 * w, :]]
        o_ref[rows, 0:GROUP_W] = jnp.concatenate(heads, axis=-1).astype(BF16)

    q = _rope(_head_norm(u_ref[:, COL_BQ:COL_BQ + GROUP_W], bqn_ref[lrow, :]), cos, sin_lo, sin_hi)
    k = _rope(_head_norm(u_ref[:, COL_BK:COL_BK + KV_W], bkn_ref[lrow, :]), cos, sin_lo, sin_hi)
    qh = [_head_cols(q * scale, h) for h in range(4)]
    v = u_ref[:, COL_BV:COL_BV + KV_W]
    kh = [_head_cols(k, kv) for kv in range(2)]
    vh = [_head_cols(v, kv) for kv in range(2)]
    kch = [_head_cols(kcb_ref[...], kv) for kv in range(2)]
    vch = [_head_cols(vcb_ref[...], kv) for kv in range(2)]
    tq = 2 * ATT_BLOCK
    for n in range(t // tq):
        rows = slice(n * tq, (n + 1) * tq)
        heads = []
        for kv in range(2):
            qp = jnp.concatenate([qh[2 * kv][rows, :], qh[2 * kv + 1][rows, :]], axis=0)
            o = _softmax_pv([_dot_nt(qp, kch[kv]), _dot_nt(qp, kh[kv])], [vch[kv], vh[kv]], None)
            heads += [o[0:tq, :], o[tq:2 * tq, :]]
        o_ref[rows, GROUP_W:2 * GROUP_W] = jnp.concatenate(heads, axis=-1).astype(BF16)

    xn_ref[...] = x + g2_ref[mrow, :] * _dot(o_ref[...], wout_ref[...].astype(BF16))


def _lat_recurrent_kernel(xn_in_ref, h_ref, g2_ref, wc_ref, wqk_ref, wvg_ref, wout_ref, h0_ref,
                          convw_ref, convb_ref, wa_ref, ba_ref, wx_ref, bx_ref, lam_ref,
                          s0_ref, theta_ref, dn_ref,
                          xn_ref, gate_w_ref, af_ref, bf_ref, ab_ref, bb_ref, hf_ref, hb_ref, gel_ref, ret_ref,
                          *, layer):
    lrow = slice(layer, layer + 1)
    mrow = pl.ds(1 + pl.program_id(0), 1)

    @pl.when(pl.program_id(0) == 0)
    def _():
        _store_gate_weights(gate_w_ref, wa_ref, wx_ref)

    h = h_ref[...]
    u = _dot(h, wc_ref[...].astype(BF16))
    _rglru_prepare(u[:, 0:GROUP_W], u[:, GROUP_W:2 * GROUP_W], convw_ref, convb_ref[lrow, :],
                   gate_w_ref, ba_ref, bx_ref, lam_ref, af_ref, bf_ref, ab_ref, bb_ref, gel_ref)
    oc, _, _ = _rglru_finish(h0_ref[0:1, :], h0_ref[1:2, :],
                             af_ref, bf_ref, ab_ref, bb_ref, hf_ref, hb_ref, gel_ref)
    y = _dot(oc.astype(BF16), wout_ref[0:GROUP_W, :].astype(BF16))

    uqk = _dot(h, wqk_ref[...].astype(BF16))
    uvg = _dot(h, wvg_ref[...].astype(BF16))
    masks = _lane_head_masks(GROUP_W)
    lgf, lgb = _log_decays(theta_ref, masks)
    s0 = tuple(_block_diag([s0_ref[d, hd] for hd in range(N_HEADS)]) for d in range(2))
    _retention(uqk[:, 0:GROUP_W], uqk[:, GROUP_W:2 * GROUP_W] * (HEAD_DIM ** -0.5),
               uvg[:, 0:GROUP_W].astype(BF16), s0, lgf, lgb, masks, ret_ref)
    o = ret_ref[...]
    o = o * lax.rsqrt(_head_mean_square(o) + EPS) * dn_ref[lrow, :] * _silu(uvg[:, GROUP_W:2 * GROUP_W])
    y = y + _dot(o.astype(BF16), wout_ref[GROUP_W:2 * GROUP_W, :].astype(BF16))
    xn_ref[...] = xn_in_ref[...] + g2_ref[mrow, :] * y


def _lat_mixers(x, mod, layer, caches, state_c, state_d, rope, norm2_g, w_in, w_out,
                a_qn, a_kn, a_sink, b_qn, b_kn, c_conv_w, c_conv_b, c_wa, c_ba, c_wx, c_bx,
                c_lambda, d_theta, d_norm_g):
    rows = pl.BlockSpec((DEC_SEQ, D_MODEL), lambda b: (LAT_BLOCK0 + b, 0))
    h_rows = pl.BlockSpec((DEC_SEQ, D_MODEL), lambda b: (b, 0))
    cache_spec = pl.BlockSpec((None, None, PAST_LEN, KV_W), lambda b: (b, layer, 0, 0))
    gain = _full((DEPTH, HEAD_DIM))
    table = _once((DEC_SEQ, LANES), lambda b: (0, 0))
    out_shape = jax.ShapeDtypeStruct((N_ROWS, D_MODEL), F32)
    win_cols = lambda w, c: _once((None, D_MODEL, w), lambda b: (layer, 0, c))
    wout_rows = lambda h, r: _once((None, h, D_MODEL), lambda b: (layer, r, 0))

    xn, h = pl.pallas_call(
        functools.partial(_lat_attn_kernel, layer=layer),
        grid=(DEC_BATCH,),
        in_specs=[rows, _full((DEPTH, D_MODEL)),
                  _mod_chunk(layer, 3), _mod_chunk(layer, 4), _mod_chunk(layer, 5),
                  win_cols(4 * GROUP_W, 0), wout_rows(2 * GROUP_W, 0),
                  cache_spec, cache_spec, cache_spec, cache_spec,
                  gain, gain, gain, gain,
                  pl.BlockSpec(memory_space=pltpu.SMEM),
                  table, table, table],
        out_specs=[rows, pl.BlockSpec((DEC_SEQ, D_MODEL), lambda b: (b, 0), pipeline_mode=pl.Buffered(1))],
        out_shape=[out_shape, jax.ShapeDtypeStruct((N_LAT_ROWS, D_MODEL), BF16)],
        input_output_aliases={0: 0},
        scratch_shapes=[pltpu.VMEM((DEC_SEQ, 4 * GROUP_W), F32), pltpu.VMEM((DEC_SEQ, 2 * GROUP_W), BF16)],
        compiler_params=_cparams("arbitrary"),
        name="lat_attention",
    )(x, norm2_g, mod, mod, mod, w_in, w_out, *caches, a_qn, a_kn, b_qn, b_kn, a_sink, *rope)

    scr = pltpu.VMEM((DEC_SEQ, GROUP_W), F32)
    xn = pl.pallas_call(
        functools.partial(_lat_recurrent_kernel, layer=layer),
        grid=(DEC_BATCH,),
        in_specs=[
            rows, h_rows, _mod_chunk(layer, 5),
            win_cols(2 * GROUP_W, COL_CX // (2 * GROUP_W)),
            win_cols(2 * GROUP_W, COL_DQ // (2 * GROUP_W)), win_cols(2 * GROUP_W, COL_DV // (2 * GROUP_W)),
            wout_rows(2 * GROUP_W, 1),
            pl.BlockSpec((None, None, 2, GROUP_W), lambda b: (b, layer, 0, 0)),
            _layer_block((4, GROUP_W), layer), _full((DEPTH, GROUP_W)),
            _layer_block((2, N_HEADS, HEAD_DIM, HEAD_DIM), layer), _layer_block((2, GROUP_W), layer),
            _layer_block((2, N_HEADS, HEAD_DIM, HEAD_DIM), layer), _layer_block((2, GROUP_W), layer),
            _layer_block((2, GROUP_W), layer),
            pl.BlockSpec((None, None, 2, N_HEADS, HEAD_DIM, HEAD_DIM), lambda b: (b, layer, 0, 0, 0, 0)),
            _layer_block((2, N_HEADS), layer), _full((DEPTH, GROUP_W))],
        out_specs=rows,
        out_shape=out_shape,
        input_output_aliases={0: 0},
        scratch_shapes=[pltpu.VMEM((4, GROUP_W, GROUP_W), BF16)] + [scr] * 8,
        compiler_params=_cparams("arbitrary"),
        name="lat_recurrent",
    )(xn, h, mod, w_in, w_in, w_in, w_out, state_c, c_conv_w, c_conv_b, c_wa, c_ba, c_wx, c_bx, c_lambda,
      state_d, d_theta, d_norm_g)
    return xn


def _rope_tables():
    t = np.arange(DEC_SEQ)
    row = (t // GRID_W).astype(np.float64)[:, None]
    col = (t % GRID_W).astype(np.float64)[:, None]
    half = HEAD_DIM // 2
    inv = 1.0 / (ROPE_BASE ** (np.arange(0, half, 2, dtype=np.float64) / half))
    j = np.arange(LANES) % HEAD_DIM
    ang = np.where((j < half)[None, :], row, col) * inv[j % (half // 2)][None, :]
    first = ((j % half) < half // 2)[None, :]
    cos, sin = np.cos(ang), np.sin(ang)
    return tuple(jnp.asarray(a, F32) for a in (cos, np.where(first, -sin, 0.0), np.where(first, 0.0, sin)))


def kernel(x_prompt, x_sample, cache_a_k, cache_a_v, cache_b_k, cache_b_v, state_c, state_d, c, c_ctx, norm1_g, norm2_g, norm3_g, w_mod, b_mod, ffn1_wg, ffn1_wu, ffn1_wd, ffn2_wg, ffn2_wu, ffn2_wd, w_in, w_out, a_qn, a_kn, a_sink, b_qn, b_kn, c_conv_w, c_conv_b, c_wa, c_ba, c_wx, c_bx, c_lambda, d_theta, d_norm_g):
    mod = _modulation(c_ctx, c, w_mod, b_mod)
    rope = _rope_tables()
    caches = tuple(t.reshape(DEC_BATCH, DEPTH, PAST_LEN, KV_W) for t in (cache_a_k, cache_a_v, cache_b_k, cache_b_v))
    mixer_params = (a_qn, a_kn, a_sink, b_qn, b_kn, c_conv_w, c_conv_b, c_wa, c_ba, c_wx, c_bx,
                    c_lambda, d_theta, d_norm_g)
    xs = (x_prompt.reshape(N_CTX_ROWS, D_MODEL), x_sample.reshape(N_LAT_ROWS, D_MODEL))
    states = ()
    for l in range(DEPTH):
        (x,) = _ffn(xs, mod, l, 0, norm1_g, ffn1_wg, ffn1_wu, ffn1_wd)
        x, states = _ctx_mixers(x, mod, l, states, norm2_g, w_in, w_out, *mixer_params)
        x = _lat_mixers(x, mod, l, caches, state_c, state_d, rope, norm2_g, w_in, w_out, *mixer_params)
        xs = _ffn((x,), mod, l, 6, norm3_g, ffn2_wg, ffn2_wu, ffn2_wd, split_out=(l == DEPTH - 1))
    y_p, y_s = xs
    ka, va, kb, vb, st_c, st_d = states
    kv_shape = (BATCH, DEPTH, SEQ, 2, HEAD_DIM)
    return (y_p.reshape(BATCH, SEQ, D_MODEL), y_s.reshape(DEC_BATCH, DEC_SEQ, D_MODEL),
            ka.reshape(kv_shape), va.reshape(kv_shape), kb.reshape(kv_shape), vb.reshape(kv_shape),
            st_c, st_d)
```
